```python
import math
import jax, jax.numpy as jnp
from jax import lax
import numpy as np

D_MODEL = 2048
BATCH = 8
SEQ = 4096
DEPTH = 4

HEAD_DIM = 128
ROPE_THETA = 10000.0
NORM_EPS = 1e-6

A_Q_HEADS = 8
A_KV_HEADS = 2
A_GROUP = A_Q_HEADS // A_KV_HEADS
A_RADIUS = 128
A_BLOCK = 128

B_PATTERNS = ((128, 1), (512, 4), (2048, 16))
B_GROUPS = len(B_PATTERNS)
B_HEADS_PER_GROUP = 4
B_HEADS = B_GROUPS * B_HEADS_PER_GROUP
B_BLOCK = 64

C_HEADS = 8
GRID_W = 64
C_WIN_ROWS = 8
C_WIN_COLS = 16

N_BRANCH = 3
A_Q_W = A_Q_HEADS * HEAD_DIM
A_KV_W = A_KV_HEADS * HEAD_DIM
B_W = B_HEADS * HEAD_DIM
B_OUT_W = B_HEADS_PER_GROUP * HEAD_DIM
C_W = C_HEADS * HEAD_DIM
IN_WIDTHS = (A_Q_W, A_KV_W, A_KV_W, B_W, B_W, B_W, C_W, C_W, C_W, N_BRANCH * D_MODEL)
N_IN = sum(IN_WIDTHS)

D_FF = ((8 * D_MODEL + 3 * 256 - 1) // (3 * 256)) * 256

kernel_name = 'hybrid_gated_local_dilated_grid_attention_encoder'


def rmsnorm(x, g):
    x32 = x.astype(jnp.float32)
    y = x32 * lax.rsqrt(jnp.mean(x32 * x32, axis=-1, keepdims=True) + NORM_EPS)
    return (y * g.astype(jnp.float32)).astype(x.dtype)


def rope_tables(n):
    half = HEAD_DIM // 2
    inv_freq = ROPE_THETA ** (-jnp.arange(half, dtype=jnp.float32) * 2.0 / HEAD_DIM)
    ang = jnp.arange(n, dtype=jnp.float32)[:, None] * inv_freq[None, :]
    return jnp.cos(ang), jnp.sin(ang)


def apply_rope(x, cos, sin):
    half = HEAD_DIM // 2
    x32 = x.astype(jnp.float32)
    x1, x2 = x32[..., :half], x32[..., half:]
    return jnp.concatenate([x1 * cos - x2 * sin, x2 * cos + x1 * sin], axis=-1).astype(x.dtype)


def banded_attention(q, k, v, radius, block, sink=None):
    bsz, hk, grp, n, dh = q.shape
    nb = n // block
    width = block + 2 * radius
    pad = ((0, 0), (0, 0), (radius, radius), (0, 0))
    idx = jnp.arange(nb)[:, None] * block + jnp.arange(width)[None, :]
    kb = jnp.pad(k, pad)[:, :, idx]
    vb = jnp.pad(v, pad)[:, :, idx]
    qb = q.reshape(bsz, hk, grp, nb, block, dh)
    s = jnp.einsum('bhgnqd,bhnkd->bhgnqk', qb, kb, preferred_element_type=jnp.float32) * (dh ** -0.5)
    qpos = (jnp.arange(nb)[:, None] * block + jnp.arange(block)[None, :])[:, :, None]
    kpos = (idx - radius)[:, None, :]
    valid = (jnp.abs(kpos - qpos) <= radius) & (kpos >= 0) & (kpos < n)
    s = jnp.where(valid, s, -jnp.inf)
    m = jnp.max(s, axis=-1, keepdims=True)
    if sink is not None:
        sk = sink.astype(jnp.float32).reshape(1, hk, grp, 1, 1, 1)
        m = jnp.maximum(m, sk)
    p = jnp.exp(s - m)
    denom = jnp.sum(p, axis=-1, keepdims=True)
    if sink is not None:
        denom = denom + jnp.exp(sk - m)
    o = jnp.einsum('bhgnqk,bhnkd->bhgnqd', (p / denom).astype(v.dtype), vb)
    lse = (m + jnp.log(denom))[..., 0]
    return o.reshape(bsz, hk, grp, n, dh), lse.reshape(bsz, hk, grp, n)


def mixer_a(qa, ka, va, cos, sin, gq, gk, sink):
    bsz, n, _ = qa.shape
    q = qa.reshape(bsz, n, A_KV_HEADS, A_GROUP, HEAD_DIM).transpose(0, 2, 3, 1, 4)
    k = ka.reshape(bsz, n, A_KV_HEADS, HEAD_DIM).transpose(0, 2, 1, 3)
    v = va.reshape(bsz, n, A_KV_HEADS, HEAD_DIM).transpose(0, 2, 1, 3)
    q = apply_rope(rmsnorm(q, gq), cos, sin)
    k = apply_rope(rmsnorm(k, gk), cos, sin)
    o, _ = banded_attention(q, k, v, A_RADIUS, math.gcd(n, A_BLOCK),
                            sink.reshape(A_KV_HEADS, A_GROUP))
    return o.transpose(0, 3, 1, 2, 4).reshape(bsz, n, A_Q_W)


def to_residue_classes(t, dil):
    bsz, h, n, dh = t.shape
    return t.reshape(bsz, h, n // dil, dil, dh).transpose(0, 1, 3, 2, 4).reshape(bsz, h * dil, n // dil, dh)


def mixer_b(qb, kb, vb, cos, sin, gq, gk):
    bsz, n, _ = qb.shape
    hg = B_HEADS_PER_GROUP

    def heads(t):
        return t.reshape(bsz, n, B_GROUPS, hg, HEAD_DIM).transpose(0, 2, 3, 1, 4)

    q = apply_rope(rmsnorm(heads(qb), gq), cos, sin)
    k = apply_rope(rmsnorm(heads(kb), gk), cos, sin)
    v = heads(vb)
    outs, lses = [], []
    for g, (window, dil) in enumerate(B_PATTERNS):
        m = n // dil
        o, lse = banded_attention(to_residue_classes(q[:, g], dil)[:, :, None],
                                  to_residue_classes(k[:, g], dil),
                                  to_residue_classes(v[:, g], dil),
                                  window // (2 * dil), math.gcd(m, B_BLOCK))
        o = o[:, :, 0].reshape(bsz, hg, dil, m, HEAD_DIM).transpose(0, 1, 3, 2, 4).reshape(bsz, hg, n, HEAD_DIM)
        lse = lse[:, :, 0].reshape(bsz, hg, dil, m).transpose(0, 1, 3, 2).reshape(bsz, hg, n)
        outs.append(o)
        lses.append(lse)
    o = jnp.stack(outs, axis=1)
    lse = jnp.stack(lses, axis=1)
    w = jax.nn.softmax(lse, axis=1)
    out = jnp.einsum('bghl,bghld->bhld', w.astype(o.dtype), o)
    return out.transpose(0, 2, 1, 3).reshape(bsz, n, B_OUT_W)


def mixer_c(qc, kc, vc, gq, gk, rpb):
    bsz, n, _ = qc.shape
    rows = n // GRID_W
    wr = min(C_WIN_ROWS, rows)

    def grid(t):
        return t.reshape(bsz, rows, GRID_W, C_HEADS, HEAD_DIM).transpose(0, 3, 1, 2, 4)

    q = rmsnorm(grid(qc), gq)
    k = rmsnorm(grid(kc), gk)
    v = grid(vc)
    r = jnp.arange(rows)
    row_start = jnp.clip(r - wr // 2, 0, rows - wr)
    krow = row_start[:, None] + jnp.arange(wr)[None, :]
    kg = k[:, :, krow]
    vg = v[:, :, krow]
    s = jnp.einsum('bhrcd,bhrwkd->bhrcwk', q, kg, preferred_element_type=jnp.float32) * (HEAD_DIM ** -0.5)
    cq = jnp.arange(GRID_W)
    col_start = jnp.clip(cq - C_WIN_COLS // 2, 0, GRID_W - C_WIN_COLS)
    col_ok = (cq[None, :] >= col_start[:, None]) & (cq[None, :] < col_start[:, None] + C_WIN_COLS)
    drow = krow - r[:, None]
    dcol = jnp.clip(cq[None, :] - cq[:, None], -(C_WIN_COLS - 1), C_WIN_COLS - 1)
    bias = rpb[:, drow[:, None, :, None] + (C_WIN_ROWS - 1), dcol[None, :, None, :] + (C_WIN_COLS - 1)]
    s = s + bias[None].astype(jnp.float32)
    s = jnp.where(col_ok[:, None, :], s, -jnp.inf)
    p = jax.nn.softmax(s.reshape(bsz, C_HEADS, rows, GRID_W, wr * GRID_W), axis=-1).reshape(s.shape)
    o = jnp.einsum('bhrcwk,bhrwkd->bhrcd', p.astype(v.dtype), vg)
    return o.transpose(0, 2, 3, 1, 4).reshape(bsz, n, C_W)


def _fwd_setup_inputs(seed: int = 0) -> dict:
    key = jax.random.key(seed)
    ks = jax.random.split(key, 14)
    f32 = jnp.float32

    def w(k, shape, fan_in):
        return jax.random.normal(k, shape, f32) * (fan_in ** -0.5)

    return {
        'x': jax.random.normal(ks[0], (BATCH, SEQ, D_MODEL), f32),
        'norm1_g': 1.0 + 0.02 * jax.random.normal(ks[1], (DEPTH, D_MODEL), f32),
        'w_in': w(ks[2], (DEPTH, D_MODEL, N_IN), D_MODEL),
        'qk_norm_g': 1.0 + 0.02 * jax.random.normal(ks[3], (DEPTH, 6, HEAD_DIM), f32),
        'sink_a': jax.random.normal(ks[4], (DEPTH, A_Q_HEADS), f32),
        'rpb_c': 0.1 * jax.random.normal(ks[5], (DEPTH, C_HEADS, 2 * C_WIN_ROWS - 1, 2 * C_WIN_COLS - 1), f32),
        'w_br_a': w(ks[6], (DEPTH, A_Q_W, D_MODEL), A_Q_W),
        'w_br_b': w(ks[7], (DEPTH, B_OUT_W, D_MODEL), B_OUT_W),
        'w_br_c': w(ks[8], (DEPTH, C_W, D_MODEL), C_W),
        'w_o': w(ks[9], (DEPTH, D_MODEL, D_MODEL), D_MODEL),
        'norm2_g': 1.0 + 0.02 * jax.random.normal(ks[10], (DEPTH, D_MODEL), f32),
        'w_gate_up': w(ks[11], (DEPTH, D_MODEL, 2 * D_FF), D_MODEL),
        'w_down': w(ks[12], (DEPTH, D_FF, D_MODEL), D_FF),
    }


def _fwd_reference(x, norm1_g, w_in, qk_norm_g, sink_a, rpb_c, w_br_a, w_br_b, w_br_c, w_o,
              norm2_g, w_gate_up, w_down):
    bsz, n, _ = x.shape
    cos, sin = rope_tables(n)
    split_points = []
    acc = 0
    for wdt in IN_WIDTHS[:-1]:
        acc += wdt
        split_points.append(acc)
    for i in range(DEPTH):
        h = rmsnorm(x, norm1_g[i])
        proj = h @ w_in[i]
        qa, ka, va, qb, kb, vb, qc, kc, vc, gl = jnp.split(proj, split_points, axis=-1)
        g = qk_norm_g[i]
        oa = mixer_a(qa, ka, va, cos, sin, g[0], g[1], sink_a[i])
        ob = mixer_b(qb, kb, vb, cos, sin, g[2], g[3])
        oc = mixer_c(qc, kc, vc, g[4], g[5], rpb_c[i])
        gates = jax.nn.sigmoid(gl.astype(jnp.float32)).astype(x.dtype).reshape(bsz, n, N_BRANCH, D_MODEL)
        merged = (gates[:, :, 0] * (oa @ w_br_a[i])
                  + gates[:, :, 1] * (ob @ w_br_b[i])
                  + gates[:, :, 2] * (oc @ w_br_c[i]))
        x = x + merged @ w_o[i]
        h2 = rmsnorm(x, norm2_g[i])
        gt, up = jnp.split(h2 @ w_gate_up[i], 2, axis=-1)
        x = x + (jax.nn.silu(gt) * up) @ w_down[i]
    return x


import jax as _jax
import jax.numpy as _jnp

TWIN_FORMAT = 'train_step'
FWD_PARAMS = ['x', 'norm1_g', 'w_in', 'qk_norm_g', 'sink_a', 'rpb_c', 'w_br_a', 'w_br_b', 'w_br_c', 'w_o', 'norm2_g', 'w_gate_up', 'w_down']
TWIN_WEIGHTS = ['norm1_g', 'w_in', 'qk_norm_g', 'sink_a', 'rpb_c', 'w_br_a', 'w_br_b', 'w_br_c', 'w_o', 'norm2_g', 'w_gate_up', 'w_down']
TWIN_DIFF_INPUT = 'x'
TWIN_INPUTS = ['x', 'norm1_g', 'w_in', 'qk_norm_g', 'sink_a', 'rpb_c', 'w_br_a', 'w_br_b', 'w_br_c', 'w_o', 'norm2_g', 'w_gate_up', 'w_down', 'loss_target', 'm_norm1_g', 'm_w_in', 'm_qk_norm_g', 'm_sink_a', 'm_rpb_c', 'm_w_br_a', 'm_w_br_b', 'm_w_br_c', 'm_w_o', 'm_norm2_g', 'm_w_gate_up', 'm_w_down', 'v_norm1_g', 'v_w_in', 'v_qk_norm_g', 'v_sink_a', 'v_rpb_c', 'v_w_br_a', 'v_w_br_b', 'v_w_br_c', 'v_w_o', 'v_norm2_g', 'v_w_gate_up', 'v_w_down']
TWIN_OUTPUTS = ['loss', 'grad_x', 'grad_norm1_g', 'grad_w_in', 'grad_qk_norm_g', 'grad_sink_a', 'grad_rpb_c', 'grad_w_br_a', 'grad_w_br_b', 'grad_w_br_c', 'grad_w_o', 'grad_norm2_g', 'grad_w_gate_up', 'grad_w_down', 'delta_norm1_g', 'delta_w_in', 'delta_qk_norm_g', 'delta_sink_a', 'delta_rpb_c', 'delta_w_br_a', 'delta_w_br_b', 'delta_w_br_c', 'delta_w_o', 'delta_norm2_g', 'delta_w_gate_up', 'delta_w_down', 'new_m_norm1_g', 'new_m_w_in', 'new_m_qk_norm_g', 'new_m_sink_a', 'new_m_rpb_c', 'new_m_w_br_a', 'new_m_w_br_b', 'new_m_w_br_c', 'new_m_w_o', 'new_m_norm2_g', 'new_m_w_gate_up', 'new_m_w_down', 'new_v_norm1_g', 'new_v_w_in', 'new_v_qk_norm_g', 'new_v_sink_a', 'new_v_rpb_c', 'new_v_w_br_a', 'new_v_w_br_b', 'new_v_w_br_c', 'new_v_w_o', 'new_v_norm2_g', 'new_v_w_gate_up', 'new_v_w_down']
TWIN_LEAF_KINDS = {'loss': 'loss', 'grad_x': 'grad_x', 'grad_norm1_g': 'grad_w', 'grad_w_in': 'grad_w', 'grad_qk_norm_g': 'grad_w', 'grad_sink_a': 'grad_w', 'grad_rpb_c': 'grad_w', 'grad_w_br_a': 'grad_w', 'grad_w_br_b': 'grad_w', 'grad_w_br_c': 'grad_w', 'grad_w_o': 'grad_w', 'grad_norm2_g': 'grad_w', 'grad_w_gate_up': 'grad_w', 'grad_w_down': 'grad_w', 'delta_norm1_g': 'delta_w', 'delta_w_in': 'delta_w', 'delta_qk_norm_g': 'delta_w', 'delta_sink_a': 'delta_w', 'delta_rpb_c': 'delta_w', 'delta_w_br_a': 'delta_w', 'delta_w_br_b': 'delta_w', 'delta_w_br_c': 'delta_w', 'delta_w_o': 'delta_w', 'delta_norm2_g': 'delta_w', 'delta_w_gate_up': 'delta_w', 'delta_w_down': 'delta_w', 'new_m_norm1_g': 'new_m', 'new_m_w_in': 'new_m', 'new_m_qk_norm_g': 'new_m', 'new_m_sink_a': 'new_m', 'new_m_rpb_c': 'new_m', 'new_m_w_br_a': 'new_m', 'new_m_w_br_b': 'new_m', 'new_m_w_br_c': 'new_m', 'new_m_w_o': 'new_m', 'new_m_norm2_g': 'new_m', 'new_m_w_gate_up': 'new_m', 'new_m_w_down': 'new_m', 'new_v_norm1_g': 'new_v', 'new_v_w_in': 'new_v', 'new_v_qk_norm_g': 'new_v', 'new_v_sink_a': 'new_v', 'new_v_rpb_c': 'new_v', 'new_v_w_br_a': 'new_v', 'new_v_w_br_b': 'new_v', 'new_v_w_br_c': 'new_v', 'new_v_w_o': 'new_v', 'new_v_norm2_g': 'new_v', 'new_v_w_gate_up': 'new_v', 'new_v_w_down': 'new_v'}


def _forward(args):
    return _fwd_reference(*[args[k] for k in FWD_PARAMS])


def _output_shape():
    def fwd():
        inp = _fwd_setup_inputs(0)
        return _fwd_reference(*[inp[k] for k in FWD_PARAMS])
    out = _jax.eval_shape(fwd)
    return out.shape, out.dtype

N_MICROBATCH = 1
ADAM_LR = 0.001
ADAM_B1 = 0.9
ADAM_B2 = 0.999
ADAM_EPS = 1e-08
ADAM_WD = 0.01
ADAM_STEP = 10
PER_EXAMPLE_BATCH_AXIS = {'x': 0, 'loss_target': 0}
SHARED_INPUTS = []
_WEIGHT_DTYPES = {'norm1_g': _jnp.float32, 'w_in': _jnp.float32, 'qk_norm_g': _jnp.float32, 'sink_a': _jnp.float32, 'rpb_c': _jnp.float32, 'w_br_a': _jnp.float32, 'w_br_b': _jnp.float32, 'w_br_c': _jnp.float32, 'w_o': _jnp.float32, 'norm2_g': _jnp.float32, 'w_gate_up': _jnp.float32, 'w_down': _jnp.float32}
MOMENT_SCALE = {'norm1_g': 1.444593e-01, 'w_in': 3.146424e-02, 'qk_norm_g': 5.777561e-01, 'sink_a': 4.718306e-02, 'rpb_c': 2.376689e-02, 'w_br_a': 3.047329e-02, 'w_br_b': 2.540336e-02, 'w_br_c': 4.237715e-02, 'w_o': 5.735241e-02, 'norm2_g': 1.234309e+01, 'w_gate_up': 1.050484e-01, 'w_down': 1.687471e-01}


def _to_microbatches(a, axis):
    t = _jnp.moveaxis(a, axis, 0)
    t = t.reshape((N_MICROBATCH, t.shape[0] // N_MICROBATCH) + t.shape[1:])
    return _jnp.moveaxis(t, 1, axis + 1)


def setup_inputs(seed: int = 0) -> dict:
    inp = _fwd_setup_inputs(seed)
    key = _jax.random.fold_in(_jax.random.key(seed), 7919)
    shape, _ = _output_shape()
    out = dict(inp)
    out["loss_target"] = _jax.random.normal(_jax.random.fold_in(key, 0), shape, _jnp.float32)
    for i, name in enumerate(TWIN_WEIGHTS):
        w = inp[name].astype(_jnp.float32)
        if MOMENT_SCALE is None:
            s = _jnp.sqrt(_jnp.mean(_jnp.square(w)) + 1e-30)
        else:
            s = MOMENT_SCALE[name]
        km, kv = _jax.random.split(_jax.random.fold_in(key, i + 1))
        out[name] = w
        out["m_" + name] = s * _jax.random.normal(km, w.shape, _jnp.float32)
        out["v_" + name] = (s * s) * _jax.random.uniform(kv, w.shape, _jnp.float32, 0.5, 1.5)
    if N_MICROBATCH > 1:
        for name, axis in PER_EXAMPLE_BATCH_AXIS.items():
            out[name] = _to_microbatches(out[name], axis)
    return {'x': out['x'], 'norm1_g': out['norm1_g'], 'w_in': out['w_in'], 'qk_norm_g': out['qk_norm_g'], 'sink_a': out['sink_a'], 'rpb_c': out['rpb_c'], 'w_br_a': out['w_br_a'], 'w_br_b': out['w_br_b'], 'w_br_c': out['w_br_c'], 'w_o': out['w_o'], 'norm2_g': out['norm2_g'], 'w_gate_up': out['w_gate_up'], 'w_down': out['w_down'], 'loss_target': out['loss_target'], 'm_norm1_g': out['m_norm1_g'], 'm_w_in': out['m_w_in'], 'm_qk_norm_g': out['m_qk_norm_g'], 'm_sink_a': out['m_sink_a'], 'm_rpb_c': out['m_rpb_c'], 'm_w_br_a': out['m_w_br_a'], 'm_w_br_b': out['m_w_br_b'], 'm_w_br_c': out['m_w_br_c'], 'm_w_o': out['m_w_o'], 'm_norm2_g': out['m_norm2_g'], 'm_w_gate_up': out['m_w_gate_up'], 'm_w_down': out['m_w_down'], 'v_norm1_g': out['v_norm1_g'], 'v_w_in': out['v_w_in'], 'v_qk_norm_g': out['v_qk_norm_g'], 'v_sink_a': out['v_sink_a'], 'v_rpb_c': out['v_rpb_c'], 'v_w_br_a': out['v_w_br_a'], 'v_w_br_b': out['v_w_br_b'], 'v_w_br_c': out['v_w_br_c'], 'v_w_o': out['v_w_o'], 'v_norm2_g': out['v_norm2_g'], 'v_w_gate_up': out['v_w_gate_up'], 'v_w_down': out['v_w_down']}


def _loss(weights, diff, rest, loss_target):
    with _jax.named_scope("forward"):
        args = {**rest, TWIN_DIFF_INPUT: diff, **{k: w.astype(_WEIGHT_DTYPES[k]) for k, w in weights.items()}}
        y = _forward(args)
    with _jax.named_scope("loss_head"):
        err = _jnp.square(y.astype(_jnp.float32) - loss_target)
        return 0.5 * _jnp.sum(_jnp.mean(err, axis=-1)) if err.ndim else 0.5 * err


def _adamw(w, g, m, v):
    m = ADAM_B1 * m + (1.0 - ADAM_B1) * g
    v = ADAM_B2 * v + (1.0 - ADAM_B2) * _jnp.square(g)
    m_hat = m / (1.0 - ADAM_B1 ** ADAM_STEP)
    v_hat = v / (1.0 - ADAM_B2 ** ADAM_STEP)
    delta = -ADAM_LR * (m_hat / (_jnp.sqrt(v_hat) + ADAM_EPS) + ADAM_WD * w)
    return delta, m, v


def reference(x, norm1_g, w_in, qk_norm_g, sink_a, rpb_c, w_br_a, w_br_b, w_br_c, w_o, norm2_g, w_gate_up, w_down, loss_target, m_norm1_g, m_w_in, m_qk_norm_g, m_sink_a, m_rpb_c, m_w_br_a, m_w_br_b, m_w_br_c, m_w_o, m_norm2_g, m_w_gate_up, m_w_down, v_norm1_g, v_w_in, v_qk_norm_g, v_sink_a, v_rpb_c, v_w_br_a, v_w_br_b, v_w_br_c, v_w_o, v_norm2_g, v_w_gate_up, v_w_down):
    given = dict(x=x, norm1_g=norm1_g, w_in=w_in, qk_norm_g=qk_norm_g, sink_a=sink_a, rpb_c=rpb_c, w_br_a=w_br_a, w_br_b=w_br_b, w_br_c=w_br_c, w_o=w_o, norm2_g=norm2_g, w_gate_up=w_gate_up, w_down=w_down, loss_target=loss_target, m_norm1_g=m_norm1_g, m_w_in=m_w_in, m_qk_norm_g=m_qk_norm_g, m_sink_a=m_sink_a, m_rpb_c=m_rpb_c, m_w_br_a=m_w_br_a, m_w_br_b=m_w_br_b, m_w_br_c=m_w_br_c, m_w_o=m_w_o, m_norm2_g=m_norm2_g, m_w_gate_up=m_w_gate_up, m_w_down=m_w_down, v_norm1_g=v_norm1_g, v_w_in=v_w_in, v_qk_norm_g=v_qk_norm_g, v_sink_a=v_sink_a, v_rpb_c=v_rpb_c, v_w_br_a=v_w_br_a, v_w_br_b=v_w_br_b, v_w_br_c=v_w_br_c, v_w_o=v_w_o, v_norm2_g=v_norm2_g, v_w_gate_up=v_w_gate_up, v_w_down=v_w_down)
    weights = {n: given[n] for n in TWIN_WEIGHTS}
    shared = {n: given[n] for n in SHARED_INPUTS}
    per_example = {n: given[n] for n in ['x']}
    grad_fn = _jax.value_and_grad(_loss, argnums=(0, 1))

    def one_microbatch(ex, loss_target):
        ex = dict(ex)
        diff = ex.pop(TWIN_DIFF_INPUT)
        return grad_fn(weights, diff, {**shared, **ex}, loss_target)

    if N_MICROBATCH == 1:
        loss, (grad_w, grad_x) = one_microbatch(per_example, given["loss_target"])
    else:
        def body(carry, xs):
            loss_sum, grad_sum = carry
            l_k, (gw_k, gx_k) = one_microbatch(xs[0], xs[1])
            with _jax.named_scope("update"):
                return (loss_sum + l_k, _jax.tree.map(_jnp.add, grad_sum, gw_k)), gx_k

        init = (_jnp.zeros((), _jnp.float32), _jax.tree.map(_jnp.zeros_like, weights))
        (loss, grad_w), grad_x = _jax.lax.scan(body, init, (per_example, given["loss_target"]))
    with _jax.named_scope("update"):
        delta_w, new_m, new_v = {}, {}, {}
        for n in TWIN_WEIGHTS:
            delta_w[n], new_m[n], new_v[n] = _adamw(weights[n], grad_w[n], given["m_" + n], given["v_" + n])
    return (loss, grad_x, *[grad_w[n] for n in TWIN_WEIGHTS], *[delta_w[n] for n in TWIN_WEIGHTS],
            *[new_m[n] for n in TWIN_WEIGHTS], *[new_v[n] for n in TWIN_WEIGHTS])
```

```python
import functools
import math

import jax
import jax.numpy as jnp
import numpy as np
from jax import lax
from jax.experimental import pallas as pl
from jax.experimental.pallas import tpu as pltpu

F32 = jnp.float32
CDT = jnp.bfloat16

HEAD = 128
NORM_EPS = 1e-6
ROPE_THETA = 10000.0
A_Q_HEADS, A_KV_HEADS, A_GROUP, A_RADIUS = 8, 2, 4, 128
B_DILS = (1, 4, 16)
B_RADIUS = 64
B_HG = 4
C_HEADS, GRID_W, C_WIN_ROWS, C_WIN_COLS = 8, 64, 8, 16
QKV_W = 9216
COL = dict(qa=0, ka=1024, va=1280, qb=1536, kb=3072, vb=4608, qc=6144, kc=7168, vc=8192)
NEG = -1e30
SCALE = HEAD ** -0.5
N_CHIPS = 4

ADAM_LR, ADAM_B1, ADAM_B2, ADAM_EPS, ADAM_WD, ADAM_STEP = 0.001, 0.9, 0.999, 1e-08, 0.01, 10

VMEM_LIMIT = 56 * 1024 * 1024
MESH = pl.DeviceIdType.MESH


def _pallas(body, **kw):
    return pl.pallas_call(body, **kw)


def _params(sem=None, **kw):
    if sem is not None:
        kw["dimension_semantics"] = sem
    return pltpu.CompilerParams(vmem_limit_bytes=VMEM_LIMIT, **kw)


def _tile(dim, pref, mult=128):
    best = None
    for t in range(mult, min(dim, pref) + 1, mult):
        if dim % t == 0:
            best = t
    return dim if best is None else best


def _sds(shape, dtype):
    return jax.ShapeDtypeStruct(tuple(shape), dtype)


_DIMS = {"nn": (((1,), (0,)), ((), ())), "nt": (((1,), (1,)), ((), ())), "tn": (((0,), (0,)), ((), ()))}


def _matmul(a, b, *, mode, grid, a_spec, b_spec, o_spec, out_shape, acc_shape, name, res=None, res_spec=None):
    nk = grid[2]
    has_res = res is not None

    def body(*refs):
        if has_res:
            a_ref, b_ref, r_ref, o_ref = refs[:4]
            rest = refs[4:]
        else:
            a_ref, b_ref, o_ref = refs[:3]
            r_ref = None
            rest = refs[3:]
        p = lax.dot_general(a_ref[...].astype(CDT), b_ref[...].astype(CDT), _DIMS[mode],
                            preferred_element_type=F32)

        def finish(acc):
            if has_res:
                acc = acc + r_ref[...].astype(F32)
            o_ref[...] = acc.astype(o_ref.dtype)

        if nk == 1:
            finish(p)
        else:
            acc_ref = rest[0]
            k = pl.program_id(2)

            @pl.when(k == 0)
            def _():
                acc_ref[...] = p

            @pl.when(k > 0)
            def _():
                acc_ref[...] += p

            @pl.when(k == nk - 1)
            def _():
                finish(acc_ref[...])

    in_specs = [a_spec, b_spec] + ([res_spec] if has_res else [])
    args = (a, b) + ((res,) if has_res else ())
    scratch = [] if nk == 1 else [pltpu.VMEM(acc_shape, F32)]
    return _pallas(body, name=name, grid=grid, in_specs=in_specs, out_specs=o_spec, out_shape=out_shape,
                   scratch_shapes=scratch, compiler_params=_params(("parallel", "parallel", "arbitrary")))(*args)


def mm_x_wcol(a, wg, out_dtype, name, tm_pref=1024, tn_pref=1024, stacked_out=1):
    m, k = a.shape
    ns = wg.shape[2]
    tm, tn = _tile(m, tm_pref, 8), _tile(ns, tn_pref)
    nj = ns // tn
    grid = (m // tm, N_CHIPS * nj, 1)
    a_spec = pl.BlockSpec((tm, k), lambda i, j, kk: (i, 0))
    b_spec = pl.BlockSpec((None, k, tn), lambda i, j, kk: (j // nj, 0, j % nj))
    if stacked_out == 1:
        o_spec = pl.BlockSpec((tm, tn), lambda i, j, kk: (i, j))
        out_shape = _sds((m, N_CHIPS * ns), out_dtype)
    else:
        per = N_CHIPS * nj // stacked_out
        o_spec = pl.BlockSpec((None, tm, tn), lambda i, j, kk: (j // per, i, j % per))
        out_shape = _sds((stacked_out, m, N_CHIPS * ns // stacked_out), out_dtype)
    return _matmul(a, wg, mode="nn", grid=grid, a_spec=a_spec, b_spec=b_spec, o_spec=o_spec,
                   out_shape=out_shape, acc_shape=(tm, tn), name=name)


def mm_x_wcolT(d, wg, name, res=None, tm_pref=1024, tk_pref=1024, stacked_in=1):
    kdim, ns = wg.shape[1], wg.shape[2]
    m = d.shape[-2]
    tm, tk = _tile(m, tm_pref, 8), _tile(ns, tk_pref)
    nkk = ns // tk
    grid = (m // tm, 1, N_CHIPS * nkk)
    if stacked_in == 1:
        a_spec = pl.BlockSpec((tm, tk), lambda i, j, kk: (i, kk))
    else:
        per = N_CHIPS * nkk // stacked_in
        a_spec = pl.BlockSpec((None, tm, tk), lambda i, j, kk: (kk // per, i, kk % per))
    b_spec = pl.BlockSpec((None, kdim, tk), lambda i, j, kk: (kk // nkk, 0, kk % nkk))
    o_spec = pl.BlockSpec((tm, kdim), lambda i, j, kk: (i, 0))
    return _matmul(d, wg, mode="nt", grid=grid, a_spec=a_spec, b_spec=b_spec, o_spec=o_spec,
                   out_shape=_sds((m, kdim), F32), acc_shape=(tm, kdim), name=name,
                   res=res, res_spec=None if res is None else o_spec)


def mm_aT_d_wcol(a, d, name, tm_pref=1024, tk_pref=1024, tn_pref=1024, stacked_in=1):
    m, kdim = a.shape
    ntot = d.shape[-1] * stacked_in
    ns = ntot // N_CHIPS
    tm, tkm, tn = _tile(kdim, tm_pref), _tile(m, tk_pref, 8), _tile(ns, tn_pref)
    nj = ns // tn
    grid = (kdim // tm, N_CHIPS * nj, m // tkm)
    a_spec = pl.BlockSpec((tkm, tm), lambda i, j, kk: (kk, i))
    if stacked_in == 1:
        b_spec = pl.BlockSpec((tkm, tn), lambda i, j, kk: (kk, j))
    else:
        per = N_CHIPS * nj // stacked_in
        b_spec = pl.BlockSpec((None, tkm, tn), lambda i, j, kk: (j // per, kk, j % per))
    o_spec = pl.BlockSpec((None, tm, tn), lambda i, j, kk: (j // nj, i, j % nj))
    return _matmul(a, d, mode="tn", grid=grid, a_spec=a_spec, b_spec=b_spec, o_spec=o_spec,
                   out_shape=_sds((N_CHIPS, kdim, ns), F32), acc_shape=(tm, tn), name=name)


def mm_x_w(a, w, name, res=None, out_dtype=F32, tm_pref=1024, tn_pref=1024, tk_pref=2048):
    m, k = a.shape
    n = w.shape[1]
    tm, tn, tk = _tile(m, tm_pref, 8), _tile(n, tn_pref), _tile(k, tk_pref)
    grid = (m // tm, n // tn, k // tk)
    o_spec = pl.BlockSpec((tm, tn), lambda i, j, kk: (i, j))
    return _matmul(a, w, mode="nn", grid=grid,
                   a_spec=pl.BlockSpec((tm, tk), lambda i, j, kk: (i, kk)),
                   b_spec=pl.BlockSpec((tk, tn), lambda i, j, kk: (kk, j)),
                   o_spec=o_spec, out_shape=_sds((m, n), out_dtype), acc_shape=(tm, tn), name=name,
                   res=res, res_spec=None if res is None else o_spec)


def mm_x_wT(d, w, name, out_dtype=F32, tm_pref=1024, tn_pref=1024):
    m, n = d.shape
    k = w.shape[0]
    tm, tn = _tile(m, tm_pref, 8), _tile(k, tn_pref)
    grid = (m // tm, k // tn, 1)
    return _matmul(d, w, mode="nt", grid=grid,
                   a_spec=pl.BlockSpec((tm, n), lambda i, j, kk: (i, 0)),
                   b_spec=pl.BlockSpec((tn, n), lambda i, j, kk: (j, 0)),
                   o_spec=pl.BlockSpec((tm, tn), lambda i, j, kk: (i, j)),
                   out_shape=_sds((m, k), out_dtype), acc_shape=(tm, tn), name=name)


def mm_aT_d(a, d, name, tm_pref=1024, tn_pref=1024, tk_pref=1024):
    m, k = a.shape
    n = d.shape[1]
    tm, tn, tk = _tile(k, tm_pref), _tile(n, tn_pref), _tile(m, tk_pref, 8)
    grid = (k // tm, n // tn, m // tk)
    return _matmul(a, d, mode="tn", grid=grid,
                   a_spec=pl.BlockSpec((tk, tm), lambda i, j, kk: (kk, i)),
                   b_spec=pl.BlockSpec((tk, tn), lambda i, j, kk: (kk, j)),
                   o_spec=pl.BlockSpec((tm, tn), lambda i, j, kk: (i, j)),
                   out_shape=_sds((k, n), F32), acc_shape=(tm, tn), name=name)


def rmsnorm_fwd(x, g, name):
    n, d = x.shape
    tm = _tile(n, 512, 8)

    def body(x_ref, g_ref, h_ref):
        xv = x_ref[...]
        r = lax.rsqrt(jnp.mean(xv * xv, axis=-1, keepdims=True) + NORM_EPS)
        h_ref[...] = (xv * r * g_ref[...]).astype(h_ref.dtype)

    return _pallas(body, name=name, grid=(n // tm,),
                   in_specs=[pl.BlockSpec((tm, d), lambda i: (i, 0)), pl.BlockSpec((1, d), lambda i: (0, 0))],
                   out_specs=pl.BlockSpec((tm, d), lambda i: (i, 0)), out_shape=_sds((n, d), CDT),
                   compiler_params=_params(("parallel",)))(x, g.reshape(1, d))


def rmsnorm_bwd(x, g, dh, dres, name):
    n, d = x.shape
    tm = _tile(n, 256, 8)

    def body(x_ref, g_ref, dh_ref, dres_ref, dx_ref, dg_ref):
        xv = x_ref[...]
        r = lax.rsqrt(jnp.mean(xv * xv, axis=-1, keepdims=True) + NORM_EPS)
        dhv = dh_ref[...]
        u = dhv * g_ref[...]
        c = jnp.mean(xv * u, axis=-1, keepdims=True)
        dx_ref[...] = dres_ref[...] + r * u - xv * (r * r * r * c)
        part = jnp.broadcast_to(jnp.sum(dhv * xv * r, axis=0, keepdims=True), (8, d))

        @pl.when(pl.program_id(0) == 0)
        def _():
            dg_ref[...] = part

        @pl.when(pl.program_id(0) > 0)
        def _():
            dg_ref[...] += part

    row = pl.BlockSpec((tm, d), lambda i: (i, 0))
    dx, dg = _pallas(body, name=name, grid=(n // tm,),
                     in_specs=[row, pl.BlockSpec((1, d), lambda i: (0, 0)), row, row],
                     out_specs=[row, pl.BlockSpec((8, d), lambda i: (0, 0))],
                     out_shape=[_sds((n, d), F32), _sds((8, d), F32)],
                     compiler_params=_params(("arbitrary",)))(x, g.reshape(1, d), dh, dres)
    return dx, dg


def _norm_rope(xh, g, cos2, sin2):
    r = lax.rsqrt(jnp.mean(xh * xh, axis=-1, keepdims=True) + NORM_EPS)
    y = xh * r * g
    if cos2 is not None:
        y = y * cos2 + pltpu.roll(y, HEAD // 2, 1) * sin2
    return y


def _norm_rope_bwd(xh, g, cos2, sin2, dout):
    if cos2 is not None:
        dy = dout * cos2 + pltpu.roll(dout * sin2, HEAD // 2, 1)
    else:
        dy = dout
    r = lax.rsqrt(jnp.mean(xh * xh, axis=-1, keepdims=True) + NORM_EPS)
    u = dy * g
    c = jnp.mean(xh * u, axis=-1, keepdims=True)
    return r * u - xh * (r * r * r * c), dy * xh * r


_QK_GROUPS = (("qa", COL["qa"], 8, 0, True), ("ka", COL["ka"], 2, 1, True),
              ("qb", COL["qb"], 12, 2, True), ("kb", COL["kb"], 12, 3, True),
              ("qc", COL["qc"], 8, 4, False), ("kc", COL["kc"], 8, 5, False))
_V_GROUPS = (("va", COL["va"], 2), ("vb", COL["vb"], 12), ("vc", COL["vc"], 8))
_PREP_OUT = (("qa", 8), ("ka", 2), ("va", 2)) + tuple((f"{t}b{g}", 4) for t in "qkv" for g in range(3)) + (
    ("qc", 8), ("kc", 8), ("vc", 8))


def _prep_src(name):
    if name[1] == "b":
        base = COL[name[0] + "b"] + int(name[2]) * B_HG * HEAD
        gain = {"q": 2, "k": 3, "v": None}[name[0]]
        return base, gain, name[0] != "v"
    base = COL[name]
    gain = {"qa": 0, "ka": 1, "va": None, "qc": 4, "kc": 5, "vc": None}[name]
    return base, gain, name in ("qa", "ka")


def qk_prep(proj, gains, cos2, sin2, name):
    n = proj.shape[0]
    tm = _tile(n, 256, 8)

    def body(p_ref, g_ref, c_ref, s_ref, *outs):
        cos2v, sin2v = c_ref[...], s_ref[...]
        for (nm, heads), o_ref in zip(_PREP_OUT, outs):
            base, gain, rope = _prep_src(nm)
            for h in range(heads):
                xh = p_ref[:, base + h * HEAD: base + (h + 1) * HEAD]
                if gain is None:
                    y = xh
                else:
                    y = _norm_rope(xh, g_ref[gain:gain + 1, :], cos2v if rope else None, sin2v if rope else None)
                o_ref[:, h * HEAD:(h + 1) * HEAD] = y.astype(o_ref.dtype)

    tab = pl.BlockSpec((tm, HEAD), lambda i: (i, 0))
    outs = _pallas(body, name=name, grid=(n // tm,),
                   in_specs=[pl.BlockSpec((tm, QKV_W), lambda i: (i, 0)), pl.BlockSpec((8, HEAD), lambda i: (0, 0)), tab, tab],
                   out_specs=[pl.BlockSpec((tm, h * HEAD), lambda i: (i, 0)) for _, h in _PREP_OUT],
                   out_shape=[_sds((n, h * HEAD), CDT) for _, h in _PREP_OUT],
                   compiler_params=_params(("parallel",)))(proj, gains, cos2, sin2)
    return dict(zip([nm for nm, _ in _PREP_OUT], outs))


def qk_prep_bwd(proj, gains, cos2, sin2, grads, name):
    n = proj.shape[0]
    tm = _tile(n, 128, 8)
    names = [nm for nm, _ in _PREP_OUT]

    def body(p_ref, g_ref, c_ref, s_ref, *refs):
        g_refs, dp_ref, dg_ref = refs[:len(names)], refs[len(names)], refs[len(names) + 1]
        cos2v, sin2v = c_ref[...], s_ref[...]
        dg = [jnp.zeros((tm, HEAD), F32) for _ in range(6)]
        for (nm, heads), gr in zip(_PREP_OUT, g_refs):
            base, gain, rope = _prep_src(nm)
            for h in range(heads):
                sl = slice(base + h * HEAD, base + (h + 1) * HEAD)
                dout = gr[:, h * HEAD:(h + 1) * HEAD]
                if gain is None:
                    dx = dout
                else:
                    dx, dgr = _norm_rope_bwd(p_ref[:, sl], g_ref[gain:gain + 1, :], cos2v if rope else None,
                                             sin2v if rope else None, dout)
                    dg[gain] = dg[gain] + dgr
                dp_ref[:, sl] = dx.astype(dp_ref.dtype)
        part = jnp.concatenate([jnp.sum(t, axis=0, keepdims=True) for t in dg] + [jnp.zeros((2, HEAD), F32)], axis=0)

        @pl.when(pl.program_id(0) == 0)
        def _():
            dg_ref[...] = part

        @pl.when(pl.program_id(0) > 0)
        def _():
            dg_ref[...] += part

    tab = pl.BlockSpec((tm, HEAD), lambda i: (i, 0))
    dp, dg = _pallas(body, name=name, grid=(n // tm,),
                     in_specs=[pl.BlockSpec((tm, QKV_W), lambda i: (i, 0)), pl.BlockSpec((8, HEAD), lambda i: (0, 0)), tab, tab]
                     + [pl.BlockSpec((tm, h * HEAD), lambda i: (i, 0)) for _, h in _PREP_OUT],
                     out_specs=[pl.BlockSpec((tm, QKV_W), lambda i: (i, 0)), pl.BlockSpec((8, HEAD), lambda i: (0, 0))],
                     out_shape=[_sds((n, QKV_W), CDT), _sds((8, HEAD), F32)],
                     compiler_params=_params(("arbitrary",)))(proj, gains, cos2, sin2, *[grads[k] for k in names])
    return dp, dg


def _band_geometry(m, bq_pref, radius):
    bq = min(bq_pref, m)
    return bq, min(bq + 2 * radius, m)


def _band_window(i, bq, radius, m, w):
    start = pl.multiple_of(jnp.clip(i * bq - radius, 0, m - w), 64)
    qpos = i * bq + lax.broadcasted_iota(jnp.int32, (bq, w), 0)
    kpos = start + lax.broadcasted_iota(jnp.int32, (bq, w), 1)
    return start, jnp.abs(kpos - qpos) <= radius


def band_attn_fwd(q, k, v, sink, *, seqs, G, radius, name, bq_pref=128):
    m = q.shape[0]
    bq, w = _band_geometry(m, bq_pref, radius)
    has_sink = sink is not None

    def body(*refs):
        if has_sink:
            sink_ref, q_ref, k_ref, v_ref, o_ref, lse_ref = refs
        else:
            q_ref, k_ref, v_ref, o_ref, lse_ref = refs
        s_id, i = pl.program_id(0), pl.program_id(1)
        start, valid = _band_window(i, bq, radius, m, w)
        k_t = k_ref[pl.ds(start, w), :]
        v_t = v_ref[pl.ds(start, w), :]
        for g in range(G):
            sl = slice(g * HEAD, (g + 1) * HEAD)
            s = lax.dot_general(q_ref[:, sl], k_t, _DIMS["nt"], preferred_element_type=F32) * SCALE
            s = jnp.where(valid, s, NEG)
            mx = jnp.max(s, axis=-1, keepdims=True)
            if has_sink:
                sk = sink_ref[0, s_id * G + g]
                mx = jnp.maximum(mx, sk)
            p = jnp.exp(s - mx)
            den = jnp.sum(p, axis=-1, keepdims=True)
            if has_sink:
                den = den + jnp.exp(sk - mx)
            o_ref[:, sl] = jnp.dot((p / den).astype(CDT), v_t, preferred_element_type=F32)
            lse_ref[:, sl] = jnp.broadcast_to(mx + jnp.log(den), (bq, HEAD))

    qspec = pl.BlockSpec((bq, G * HEAD), lambda s, i: (i, s))
    kspec = pl.BlockSpec((m, HEAD), lambda s, i: (0, s))
    in_specs = ([pl.BlockSpec(memory_space=pltpu.SMEM)] if has_sink else []) + [qspec, kspec, kspec]
    args = ((sink,) if has_sink else ()) + (q, k, v)
    return _pallas(body, name=name, grid=(seqs, m // bq), in_specs=in_specs, out_specs=[qspec, qspec],
                   out_shape=[_sds(q.shape, F32), _sds(q.shape, F32)],
                   compiler_params=_params(("parallel", "arbitrary")))(*args)


def band_attn_bwd(q, k, v, sink, o, lse, do, dlse, *, seqs, G, radius, name, bq_pref=128):
    m = q.shape[0]
    bq, w = _band_geometry(m, bq_pref, radius)
    has_sink, has_dlse = sink is not None, dlse is not None

    def body(*refs):
        refs = list(refs)
        sink_ref = refs.pop(0) if has_sink else None
        q_ref, k_ref, v_ref, o_ref, lse_ref, do_ref = refs[:6]
        refs = refs[6:]
        dlse_ref = refs.pop(0) if has_dlse else None
        dq_ref, dk_ref, dv_ref = refs[:3]
        dsink_ref = refs[3] if has_sink else None
        s_id, i = pl.program_id(0), pl.program_id(1)

        @pl.when(i == 0)
        def _():
            dk_ref[...] = jnp.zeros_like(dk_ref)
            dv_ref[...] = jnp.zeros_like(dv_ref)
            if has_sink:
                dsink_ref[...] = jnp.zeros_like(dsink_ref)

        start, valid = _band_window(i, bq, radius, m, w)
        k_t = k_ref[pl.ds(start, w), :]
        v_t = v_ref[pl.ds(start, w), :]
        dk_acc = jnp.zeros((w, HEAD), F32)
        dv_acc = jnp.zeros((w, HEAD), F32)
        for g in range(G):
            sl = slice(g * HEAD, (g + 1) * HEAD)
            q_t = q_ref[:, sl]
            lse_t = lse_ref[:, sl][:, :1]
            s = lax.dot_general(q_t, k_t, _DIMS["nt"], preferred_element_type=F32) * SCALE
            p = jnp.exp(jnp.where(valid, s, NEG) - lse_t)
            do_t = do_ref[:, sl]
            delta = jnp.sum(do_t * o_ref[:, sl], axis=-1, keepdims=True)
            do_c = do_t.astype(CDT)
            dp = lax.dot_general(do_c, v_t, _DIMS["nt"], preferred_element_type=F32)
            dv_acc = dv_acc + lax.dot_general(p.astype(CDT), do_c, _DIMS["tn"], preferred_element_type=F32)
            t = dp - delta
            if has_dlse:
                t = t + dlse_ref[:, sl][:, :1]
            ds = ((p * t) * SCALE).astype(CDT)
            dq_ref[:, sl] = jnp.dot(ds, k_t, preferred_element_type=F32)
            dk_acc = dk_acc + lax.dot_general(ds, q_t, _DIMS["tn"], preferred_element_type=F32)
            if has_sink:
                sk = sink_ref[0, s_id * G + g]
                part = -jnp.sum(jnp.exp(sk - lse_t) * delta, axis=0, keepdims=True)
                dsink_ref[g:g + 1, :] += jnp.broadcast_to(part, (1, HEAD))
        dk_ref[pl.ds(start, w), :] += dk_acc
        dv_ref[pl.ds(start, w), :] += dv_acc

    qspec = pl.BlockSpec((bq, G * HEAD), lambda s, i: (i, s))
    kspec = pl.BlockSpec((m, HEAD), lambda s, i: (0, s))
    in_specs = ([pl.BlockSpec(memory_space=pltpu.SMEM)] if has_sink else []) + [qspec, kspec, kspec, qspec, qspec, qspec]
    in_specs += [qspec] if has_dlse else []
    args = ((sink,) if has_sink else ()) + (q, k, v, o, lse, do) + ((dlse,) if has_dlse else ())
    out_specs = [qspec, kspec, kspec]
    out_shape = [_sds(q.shape, F32), _sds(k.shape, F32), _sds(k.shape, F32)]
    if has_sink:
        out_specs.append(pl.BlockSpec((None, 8, HEAD), lambda s, i: (s, 0, 0)))
        out_shape.append(_sds((seqs, 8, HEAD), F32))
    return _pallas(body, name=name, grid=(seqs, m // bq), in_specs=in_specs, out_specs=out_specs,
                   out_shape=out_shape, compiler_params=_params(("parallel", "arbitrary")))(*args)


def _group_weights(lses):
    mx = jnp.maximum(jnp.maximum(lses[0], lses[1]), lses[2])
    e = [jnp.exp(l - mx) for l in lses]
    tot = e[0] + e[1] + e[2]
    return [t / tot for t in e]


def b_combine_fwd(os_, lses, name):
    n, wd = os_[0].shape
    tm = _tile(n, 512, 8)

    def body(o0, o1, o2, l0, l1, l2, out_ref):
        wts = _group_weights([l0[...], l1[...], l2[...]])
        out_ref[...] = wts[0] * o0[...] + wts[1] * o1[...] + wts[2] * o2[...]

    row = pl.BlockSpec((tm, wd), lambda i: (i, 0))
    return _pallas(body, name=name, grid=(n // tm,), in_specs=[row] * 6, out_specs=row,
                   out_shape=_sds((n, wd), F32), compiler_params=_params(("parallel",)))(*os_, *lses)


def b_combine_bwd(dout, os_, lses, name):
    n, wd = dout.shape
    tm = _tile(n, 256, 8)

    def body(d_ref, o0, o1, o2, l0, l1, l2, do0, do1, do2, dl0, dl1, dl2):
        dv = d_ref[...]
        wts = _group_weights([l0[...], l1[...], l2[...]])
        dws = []
        for o_ref in (o0, o1, o2):
            prod = dv * o_ref[...]
            cols = []
            for h in range(wd // HEAD):
                sseg = jnp.sum(prod[:, h * HEAD:(h + 1) * HEAD], axis=-1, keepdims=True)
                cols.append(jnp.broadcast_to(sseg, (tm, HEAD)))
            dws.append(jnp.concatenate(cols, axis=-1))
        mean = wts[0] * dws[0] + wts[1] * dws[1] + wts[2] * dws[2]
        for wt, dw, do_ref, dl_ref in zip(wts, dws, (do0, do1, do2), (dl0, dl1, dl2)):
            do_ref[...] = wt * dv
            dl_ref[...] = wt * (dw - mean)

    row = pl.BlockSpec((tm, wd), lambda i: (i, 0))
    outs = _pallas(body, name=name, grid=(n // tm,), in_specs=[row] * 7, out_specs=[row] * 6,
                   out_shape=[_sds((n, wd), F32)] * 6, compiler_params=_params(("parallel",)))(dout, *os_, *lses)
    return outs[:3], outs[3:]


def _c_rows(n):
    rows = n // GRID_W
    return rows, min(C_WIN_ROWS, rows)


def _c_row_start(r, rows, wr):
    return jnp.clip(r - wr // 2, 0, rows - wr)


def _c_bias_index(r, rows, wr):
    return _c_row_start(r, rows, wr) - r + (C_WIN_ROWS - 1)


def _col_shift_select(tile, cq, inverse):
    lanes = tile.shape[1]
    for b in range(6):
        amt = (lanes - (1 << b)) if inverse else (1 << b)
        tile = jnp.where(((cq >> b) & 1) == 1, pltpu.roll(tile, amt, 1), tile)
    return tile


def rpb_expand(rwin, name):
    lanes = rwin.shape[-1]

    def body(r_ref, b_ref):
        cq = lax.broadcasted_iota(jnp.int32, (GRID_W, lanes), 0)
        ck = lax.broadcasted_iota(jnp.int32, (GRID_W, lanes), 1) % GRID_W
        tile = jnp.broadcast_to(r_ref[...], (GRID_W, lanes))
        tile = pltpu.roll(tile, lanes - (C_WIN_COLS - 1), 1)
        tile = _col_shift_select(tile, cq, False)
        cs = jnp.clip(cq - C_WIN_COLS // 2, 0, GRID_W - C_WIN_COLS)
        ok = (ck >= cs) & (ck < cs + C_WIN_COLS)
        b_ref[...] = jnp.where(ok, tile, NEG)

    return _pallas(body, name=name, grid=(C_HEADS, C_WIN_ROWS),
                   in_specs=[pl.BlockSpec((None, None, 1, lanes), lambda h, i: (h, i, 0, 0))],
                   out_specs=pl.BlockSpec((None, None, GRID_W, lanes), lambda h, i: (h, i, 0, 0)),
                   out_shape=_sds((C_HEADS, C_WIN_ROWS, GRID_W, lanes), F32),
                   compiler_params=_params(("parallel", "parallel")))(rwin)


def rpb_reduce(dbias, name):
    lanes = dbias.shape[-1]
    wr = lanes // GRID_W

    def body(d_ref, o_ref, acc_ref):
        i0 = pl.program_id(1)
        cq = lax.broadcasted_iota(jnp.int32, (GRID_W, lanes), 0)
        tile = _col_shift_select(d_ref[...], cq, True)
        tile = pltpu.roll(tile, C_WIN_COLS - 1, 1)
        vec = jnp.sum(tile, axis=0, keepdims=True)

        @pl.when(i0 == 0)
        def _():
            acc_ref[...] = jnp.zeros_like(acc_ref)

        for w in range(wr):
            acc_ref[pl.ds(i0 + w, 1), :] += vec[:, w * GRID_W:(w + 1) * GRID_W]

        @pl.when(i0 == C_WIN_ROWS - 1)
        def _():
            o_ref[...] = acc_ref[...]

    return _pallas(body, name=name, grid=(C_HEADS, C_WIN_ROWS),
                   in_specs=[pl.BlockSpec((None, None, GRID_W, lanes), lambda h, i: (h, i, 0, 0))],
                   out_specs=pl.BlockSpec((None, 16, GRID_W), lambda h, i: (h, 0, 0)),
                   out_shape=_sds((C_HEADS, 16, GRID_W), F32),
                   scratch_shapes=[pltpu.VMEM((16, GRID_W), F32)],
                   compiler_params=_params(("parallel", "arbitrary")))(dbias)


def c_attn_fwd(q, k, v, bias, name):
    n = q.shape[0]
    rows, wr = _c_rows(n)
    wk = wr * GRID_W

    def body(q_ref, k_ref, v_ref, b_ref, o_ref, lse_ref):
        r = pl.program_id(1)
        start = pl.multiple_of(_c_row_start(r, rows, wr) * GRID_W, GRID_W)
        k_t = k_ref[pl.ds(start, wk), :]
        v_t = v_ref[pl.ds(start, wk), :]
        s = lax.dot_general(q_ref[...], k_t, _DIMS["nt"], preferred_element_type=F32) * SCALE + b_ref[...]
        mx = jnp.max(s, axis=-1, keepdims=True)
        p = jnp.exp(s - mx)
        den = jnp.sum(p, axis=-1, keepdims=True)
        o_ref[...] = jnp.dot((p / den).astype(CDT), v_t, preferred_element_type=F32)
        lse_ref[...] = jnp.broadcast_to(mx + jnp.log(den), (GRID_W, HEAD))

    qspec = pl.BlockSpec((GRID_W, HEAD), lambda h, r: (r, h))
    kspec = pl.BlockSpec((n, HEAD), lambda h, r: (0, h))
    bspec = pl.BlockSpec((None, None, GRID_W, wk), lambda h, r: (h, _c_bias_index(r, rows, wr), 0, 0))
    return _pallas(body, name=name, grid=(C_HEADS, rows), in_specs=[qspec, kspec, kspec, bspec],
                   out_specs=[qspec, qspec], out_shape=[_sds(q.shape, F32), _sds(q.shape, F32)],
                   compiler_params=_params(("parallel", "arbitrary")))(q, k, v, bias)


def c_attn_bwd(q, k, v, bias, o, lse, do, name):
    n = q.shape[0]
    rows, wr = _c_rows(n)
    wk = wr * GRID_W

    def body(q_ref, k_ref, v_ref, b_ref, o_ref, lse_ref, do_ref, dq_ref, dk_ref, dv_ref, db_ref):
        r = pl.program_id(1)
        rs = _c_row_start(r, rows, wr)
        start = pl.multiple_of(rs * GRID_W, GRID_W)

        @pl.when(r == 0)
        def _():
            dk_ref[...] = jnp.zeros_like(dk_ref)
            dv_ref[...] = jnp.zeros_like(dv_ref)

        k_t = k_ref[pl.ds(start, wk), :]
        v_t = v_ref[pl.ds(start, wk), :]
        q_t = q_ref[...]
        s = lax.dot_general(q_t, k_t, _DIMS["nt"], preferred_element_type=F32) * SCALE + b_ref[...]
        p = jnp.exp(s - lse_ref[...][:, :1])
        do_t = do_ref[...]
        delta = jnp.sum(do_t * o_ref[...], axis=-1, keepdims=True)
        do_c = do_t.astype(CDT)
        dp = lax.dot_general(do_c, v_t, _DIMS["nt"], preferred_element_type=F32)
        dv_ref[pl.ds(start, wk), :] += lax.dot_general(p.astype(CDT), do_c, _DIMS["tn"], preferred_element_type=F32)
        ds = p * (dp - delta)
        ds_c = (ds * SCALE).astype(CDT)
        dq_ref[...] = jnp.dot(ds_c, k_t, preferred_element_type=F32)
        dk_ref[pl.ds(start, wk), :] += lax.dot_general(ds_c, q_t, _DIMS["tn"], preferred_element_type=F32)
        prev = _c_row_start(jnp.maximum(r - 1, 0), rows, wr) - jnp.maximum(r - 1, 0)
        first = (r == 0) | (prev != rs - r)

        @pl.when(first)
        def _():
            db_ref[...] = ds

        @pl.when(jnp.logical_not(first))
        def _():
            db_ref[...] += ds

    qspec = pl.BlockSpec((GRID_W, HEAD), lambda h, r: (r, h))
    kspec = pl.BlockSpec((n, HEAD), lambda h, r: (0, h))
    bspec = pl.BlockSpec((None, None, GRID_W, wk), lambda h, r: (h, _c_bias_index(r, rows, wr), 0, 0))
    return _pallas(body, name=name, grid=(C_HEADS, rows),
                   in_specs=[qspec, kspec, kspec, bspec, qspec, qspec, qspec],
                   out_specs=[qspec, kspec, kspec, bspec],
                   out_shape=[_sds(q.shape, F32), _sds(k.shape, F32), _sds(k.shape, F32), _sds(bias.shape, F32)],
                   compiler_params=_params(("parallel", "arbitrary")))(q, k, v, bias, o, lse, do)


def _sigmoid(z):
    return 1.0 / (1.0 + jnp.exp(-z))


def _gate_specs(n, d):
    tm, tn = _tile(n, 256, 8), _tile(math.gcd(d, QKV_W), 1024)
    nj = d // tn
    tile = pl.BlockSpec((tm, tn), lambda i, j: (i, j))
    gl = [pl.BlockSpec((tm, tn), functools.partial(lambda i, j, b: (i, (QKV_W + b * d) // tn + j), b=b)) for b in range(3)]
    return tm, tn, nj, tile, gl


def gate_merge(proj, ys, name):
    n, d = ys[0].shape
    tm, tn, nj, tile, gl = _gate_specs(n, d)

    def body(g0, g1, g2, y0, y1, y2, out_ref):
        acc = _sigmoid(g0[...]) * y0[...] + _sigmoid(g1[...]) * y1[...] + _sigmoid(g2[...]) * y2[...]
        out_ref[...] = acc.astype(out_ref.dtype)

    return _pallas(body, name=name, grid=(n // tm, nj), in_specs=gl + [tile] * 3, out_specs=tile,
                   out_shape=_sds((n, d), CDT), compiler_params=_params(("parallel", "parallel")))(proj, proj, proj, *ys)


def gate_bwd(proj, ys, dmerged, name):
    n, d = dmerged.shape
    tm, tn, nj, tile, gl = _gate_specs(n, d)

    def body(g0, g1, g2, y0, y1, y2, dm_ref, dy0, dy1, dy2, dg0, dg1, dg2):
        dm = dm_ref[...]
        for g_ref, y_ref, dy_ref, dg_ref in ((g0, y0, dy0, dg0), (g1, y1, dy1, dg1), (g2, y2, dy2, dg2)):
            sg = _sigmoid(g_ref[...])
            dy_ref[...] = (dm * sg).astype(dy_ref.dtype)
            dg_ref[...] = (dm * y_ref[...] * sg * (1.0 - sg)).astype(dg_ref.dtype)

    outs = _pallas(body, name=name, grid=(n // tm, nj), in_specs=gl + [tile] * 4, out_specs=[tile] * 6,
                   out_shape=[_sds((n, d), CDT)] * 6,
                   compiler_params=_params(("parallel", "parallel")))(proj, proj, proj, *ys, dmerged)
    return outs[:3], outs[3:]


def swiglu_fwd(gu, name):
    _, n, ff = gu.shape
    tm, tn = _tile(n, 512, 8), _tile(ff, 1024)

    def body(g_ref, u_ref, a_ref):
        gt = g_ref[...]
        a_ref[...] = (gt * _sigmoid(gt) * u_ref[...]).astype(a_ref.dtype)

    return _pallas(body, name=name, grid=(n // tm, ff // tn),
                   in_specs=[pl.BlockSpec((None, tm, tn), lambda i, j: (0, i, j)),
                             pl.BlockSpec((None, tm, tn), lambda i, j: (1, i, j))],
                   out_specs=pl.BlockSpec((tm, tn), lambda i, j: (i, j)), out_shape=_sds((n, ff), CDT),
                   compiler_params=_params(("parallel", "parallel")))(gu, gu)


def swiglu_bwd(gu, dact, name):
    _, n, ff = gu.shape
    tm, tn = _tile(n, 512, 8), _tile(ff, 1024)

    def body(g_ref, u_ref, da_ref, d_ref):
        gt, up, da = g_ref[...], u_ref[...], da_ref[...]
        sg = _sigmoid(gt)
        half = pl.program_id(2)

        @pl.when(half == 0)
        def _():
            d_ref[...] = (da * up * (sg + gt * sg * (1.0 - sg))).astype(d_ref.dtype)

        @pl.when(half == 1)
        def _():
            d_ref[...] = (da * gt * sg).astype(d_ref.dtype)

    return _pallas(body, name=name, grid=(n // tm, ff // tn, 2),
                   in_specs=[pl.BlockSpec((None, tm, tn), lambda i, j, h: (0, i, j)),
                             pl.BlockSpec((None, tm, tn), lambda i, j, h: (1, i, j)),
                             pl.BlockSpec((tm, tn), lambda i, j, h: (i, j))],
                   out_specs=pl.BlockSpec((None, tm, tn), lambda i, j, h: (h, i, j)), out_shape=_sds((2, n, ff), CDT),
                   compiler_params=_params(("parallel", "parallel", "arbitrary")))(gu, gu, dact)


def loss_head(y, target, name):
    n, d = y.shape
    tm = _tile(n, 512, 8)
    nsteps = n // tm

    def body(y_ref, t_ref, l_ref, dy_ref, acc_ref):
        i = pl.program_id(0)
        e = y_ref[...] - t_ref[...]
        dy_ref[...] = e * (1.0 / d)
        part = jnp.sum((e * e).reshape(tm // 8, 8, d), axis=0)

        @pl.when(i == 0)
        def _():
            acc_ref[...] = part

        @pl.when(i > 0)
        def _():
            acc_ref[...] += part

        @pl.when(i == nsteps - 1)
        def _():
            tot = jnp.sum(jnp.sum(acc_ref[...], axis=1, keepdims=True), axis=0, keepdims=True) * (0.5 / d)
            l_ref[...] = jnp.broadcast_to(tot, (8, HEAD))

    row = pl.BlockSpec((tm, d), lambda i: (i, 0))
    return _pallas(body, name=name, grid=(nsteps,), in_specs=[row, row],
                   out_specs=[pl.BlockSpec((8, HEAD), lambda i: (0, 0)), row],
                   out_shape=[_sds((8, HEAD), F32), _sds((n, d), F32)],
                   scratch_shapes=[pltpu.VMEM((8, d), F32)],
                   compiler_params=_params(("arbitrary",)))(y, target)


def adamw(w, g, m, v, name):
    r, c = w.shape
    tr = _tile(r, max(8, (1 << 19) // c), 8)
    c1 = 1.0 - ADAM_B1 ** ADAM_STEP
    c2 = 1.0 - ADAM_B2 ** ADAM_STEP

    def body(w_ref, g_ref, m_ref, v_ref, d_ref, mo_ref, vo_ref):
        gv = g_ref[...]
        mn = ADAM_B1 * m_ref[...] + (1.0 - ADAM_B1) * gv
        vn = ADAM_B2 * v_ref[...] + (1.0 - ADAM_B2) * (gv * gv)
        d_ref[...] = -ADAM_LR * ((mn / c1) / (jnp.sqrt(vn / c2) + ADAM_EPS) + ADAM_WD * w_ref[...])
        mo_ref[...] = mn
        vo_ref[...] = vn

    row = pl.BlockSpec((tr, c), lambda i: (i, 0))
    return _pallas(body, name=name, grid=(r // tr,), in_specs=[row] * 4, out_specs=[row] * 3,
                   out_shape=[_sds((r, c), F32)] * 3, compiler_params=_params(("parallel",)))(w, g, m, v)


ANY = pl.BlockSpec(memory_space=pl.ANY)


def _place():
    x, y, c = lax.axis_index("x"), lax.axis_index("y"), lax.axis_index("c")
    return x, y, c, [(1 - x, y), (x, 1 - y), (1 - x, 1 - y)]


def _rcopy(src, dst, send_sems, recv_sems, k, to):
    return pltpu.make_async_remote_copy(src_ref=src, dst_ref=dst, send_sem=send_sems.at[k], recv_sem=recv_sems.at[k],
                                        device_id=to, device_id_type=MESH)


def gather_weights(shards, name):
    nt = len(shards)

    def body(*refs):
        ins, outs = refs[:nt], refs[nt:2 * nt]
        send_sems, recv_sems, local_sems = refs[2 * nt:]
        x, y, c, chips = _place()
        me, sibling = 2 * x + y, (x, y, 1 - c)
        local, remote = [], []
        for t in range(nt):
            kh = ins[t].shape[0] // 2
            mine = pl.ds(c * kh, kh)
            local.append(pltpu.make_async_copy(ins[t], outs[t].at[me], local_sems.at[t]))
            local[-1].start()
            for j, (px, py) in enumerate(chips):
                remote.append(_rcopy(ins[t].at[mine], outs[t].at[me, mine], send_sems, recv_sems, 6 * t + j, (px, py, c)))
                remote[-1].start()
        for t in range(nt):
            kh = ins[t].shape[0] // 2
            mine = pl.ds(c * kh, kh)
            for j, (px, py) in enumerate(chips):
                blk = outs[t].at[2 * px + py, mine]
                _rcopy(blk, blk, send_sems, recv_sems, 6 * t + j, (px, py, c)).wait_recv()
                remote.append(_rcopy(blk, blk, send_sems, recv_sems, 6 * t + 3 + j, sibling))
                remote[-1].start()
        for t in range(nt):
            kh = ins[t].shape[0] // 2
            other = pl.ds((1 - c) * kh, kh)
            for j, (px, py) in enumerate(chips):
                blk = outs[t].at[2 * px + py, other]
                _rcopy(blk, blk, send_sems, recv_sems, 6 * t + 3 + j, sibling).wait_recv()
        for cp in remote:
            cp.wait_send()
        for cp in local:
            cp.wait()

    return _pallas(body, name=name, in_specs=[ANY] * nt, out_specs=[ANY] * nt,
                   out_shape=[_sds((N_CHIPS,) + s.shape, s.dtype) for s in shards],
                   scratch_shapes=[pltpu.SemaphoreType.DMA((6 * nt,)), pltpu.SemaphoreType.DMA((6 * nt,)),
                                   pltpu.SemaphoreType.DMA((nt,))],
                   compiler_params=pltpu.CompilerParams(has_side_effects=True))(*shards)


def pair_exchange(grads, name):
    nt = len(grads)

    def body(*refs):
        ins, outs = refs[:nt], refs[nt:2 * nt]
        send_sems, recv_sems = refs[2 * nt:]
        x, y, c, _ = _place()
        sibling = (x, y, 1 - c)
        cps = []
        for t in range(nt):
            kh = ins[t].shape[1] // 2
            cps.append(_rcopy(ins[t].at[:, pl.ds((1 - c) * kh, kh), :], outs[t], send_sems, recv_sems, t, sibling))
            cps[-1].start()
        for cp in cps:
            cp.wait_recv()
        for cp in cps:
            cp.wait_send()

    return _pallas(body, name=name, in_specs=[ANY] * nt, out_specs=[ANY] * nt,
                   out_shape=[_sds((N_CHIPS, g.shape[1] // 2, g.shape[2]), g.dtype) for g in grads],
                   scratch_shapes=[pltpu.SemaphoreType.DMA((nt,)), pltpu.SemaphoreType.DMA((nt,))],
                   compiler_params=pltpu.CompilerParams(has_side_effects=True))(*grads)


def chip_exchange(sums, name):
    nt = len(sums)

    def body(*refs):
        ins, outs = refs[:nt], refs[nt:2 * nt]
        send_sems, recv_sems = refs[2 * nt:]
        x, y, c, chips = _place()
        cps = []
        for t in range(nt):
            for j, (px, py) in enumerate(chips):
                cps.append(_rcopy(ins[t].at[2 * px + py], outs[t].at[j], send_sems, recv_sems, 3 * t + j, (px, py, c)))
                cps[-1].start()
        for cp in cps:
            cp.wait_recv()
        for cp in cps:
            cp.wait_send()

    return _pallas(body, name=name, in_specs=[ANY] * nt, out_specs=[ANY] * nt,
                   out_shape=[_sds((3,) + s.shape[1:], s.dtype) for s in sums],
                   scratch_shapes=[pltpu.SemaphoreType.DMA((3 * nt,)), pltpu.SemaphoreType.DMA((3 * nt,))],
                   compiler_params=pltpu.CompilerParams(has_side_effects=True))(*sums)


def pair_share(halves, n_layers, name):
    nt = len(halves) // n_layers
    nc = len(halves)

    def body(*refs):
        ins, outs = refs[:nc], refs[nc:nc + nt]
        send_sems, recv_sems, local_sems = refs[nc + nt:]
        x, y, c, _ = _place()
        sibling = (x, y, 1 - c)
        local, remote = [], []
        for l in range(n_layers):
            for t in range(nt):
                k = l * nt + t
                kh = ins[k].shape[0]
                mine = outs[t].at[l, pl.ds(c * kh, kh), :]
                local.append(pltpu.make_async_copy(ins[k], mine, local_sems.at[k]))
                local[-1].start()
                remote.append(_rcopy(ins[k], mine, send_sems, recv_sems, k, sibling))
                remote[-1].start()
        for cp in remote:
            cp.wait_recv()
        for cp in remote:
            cp.wait_send()
        for cp in local:
            cp.wait()

    return _pallas(body, name=name, in_specs=[ANY] * nc, out_specs=[ANY] * nt,
                   out_shape=[_sds((n_layers, 2 * halves[t].shape[0], halves[t].shape[1]), halves[t].dtype) for t in range(nt)],
                   scratch_shapes=[pltpu.SemaphoreType.DMA((nc,)), pltpu.SemaphoreType.DMA((nc,)),
                                   pltpu.SemaphoreType.DMA((nc,))],
                   compiler_params=pltpu.CompilerParams(has_side_effects=True))(*halves)


def small_allreduce(pack, name):
    r = pack.shape[0]

    def body(in_ref, out_ref, buf, send_sems, recv_sems):
        x, y, c, _ = _place()
        me = 4 * x + 2 * y + c
        sends = []
        for k in range(1, 8):
            to = ((x + ((k >> 2) & 1)) % 2, (y + ((k >> 1) & 1)) % 2, (c + (k & 1)) % 2)
            cp = _rcopy(in_ref, buf.at[me], send_sems, recv_sems, k - 1, to)
            cp.start()
            sends.append((cp, to))
        buf[pl.ds(me, 1)] = in_ref[...][None]
        for k, (_, to) in enumerate(sends):
            peer = 4 * to[0] + 2 * to[1] + to[2]
            _rcopy(in_ref, buf.at[peer], send_sems, recv_sems, k, to).wait_recv()
        for cp, _ in sends:
            cp.wait_send()
        acc = buf[0]
        for d in range(1, 8):
            acc = acc + buf[d]
        out_ref[...] = acc

    vm = pl.BlockSpec(memory_space=pltpu.VMEM)
    return _pallas(body, name=name, in_specs=[vm], out_specs=vm, out_shape=_sds((r, HEAD), F32),
                   scratch_shapes=[pltpu.VMEM((8, r, HEAD), F32), pltpu.SemaphoreType.DMA((7,)),
                                   pltpu.SemaphoreType.DMA((7,))],
                   compiler_params=pltpu.CompilerParams(has_side_effects=True))(pack)


def add_halves(c_idx, grad, other, name):
    _, k, ns = grad.shape
    kh = k // 2
    tr = _tile(kh, max(8, (1 << 19) // ns), 8)
    nr = kh // tr

    def body(c_ref, g_ref, o_ref, s_ref):
        s_ref[...] = g_ref[...] + o_ref[...]

    gs = pltpu.PrefetchScalarGridSpec(
        num_scalar_prefetch=1, grid=(N_CHIPS, nr),
        in_specs=[pl.BlockSpec((None, tr, ns), lambda g, i, c_ref: (g, c_ref[0] * nr + i, 0)),
                  pl.BlockSpec((None, tr, ns), lambda g, i, c_ref: (g, i, 0))],
        out_specs=pl.BlockSpec((None, tr, ns), lambda g, i, c_ref: (g, i, 0)))
    return _pallas(body, name=name, grid_spec=gs, out_shape=_sds((N_CHIPS, kh, ns), F32),
                   compiler_params=_params(("parallel", "parallel")))(c_idx, grad, other)


def add_chips(chip_idx, sums, recv, name):
    _, kh, ns = sums.shape
    tr = _tile(kh, max(8, (1 << 19) // ns), 8)

    def body(k_ref, s_ref, r0, r1, r2, o_ref):
        o_ref[...] = ((s_ref[...] + r0[...]) + r1[...]) + r2[...]

    rspec = [pl.BlockSpec((None, tr, ns), functools.partial(lambda i, k_ref, j: (j, i, 0), j=j)) for j in range(3)]
    gs = pltpu.PrefetchScalarGridSpec(
        num_scalar_prefetch=1, grid=(kh // tr,),
        in_specs=[pl.BlockSpec((None, tr, ns), lambda i, k_ref: (k_ref[0], i, 0))] + rspec,
        out_specs=pl.BlockSpec((tr, ns), lambda i, k_ref: (i, 0)))
    return _pallas(body, name=name, grid_spec=gs, out_shape=_sds((kh, ns), F32),
                   compiler_params=_params(("parallel",)))(chip_idx, sums, recv, recv, recv)


W_NAMES = ("w_in", "w_br_a", "w_br_b", "w_br_c", "w_o", "w_gate_up", "w_down")


def _rope_tables(n):
    half = HEAD // 2
    inv_freq = ROPE_THETA ** (-jnp.arange(half, dtype=F32) * 2.0 / HEAD)
    ang = jnp.arange(n, dtype=F32)[:, None] * inv_freq[None, :]
    cos, sin = jnp.cos(ang), jnp.sin(ang)
    return jnp.concatenate([cos, cos], axis=-1), jnp.concatenate([-sin, sin], axis=-1)


def _rpb_windows(rpb):
    pad = jnp.pad(rpb, ((0, 0), (0, 1), (0, GRID_W - rpb.shape[2])))
    wins = [pad[:, i0:i0 + C_WIN_ROWS].reshape(C_HEADS, 1, C_WIN_ROWS * GRID_W) for i0 in range(C_WIN_ROWS)]
    return jnp.stack(wins, axis=1)


def _b_view(t, dil):
    n, wd = t.shape
    return t.reshape(n // dil, dil * wd)


def layer_fwd(x, p, w, cos2, sin2):
    n, d = x.shape
    s = {"x": x}
    s["h"] = rmsnorm_fwd(x, p["norm1_g"], "norm1")
    s["proj"] = mm_x_wcol(s["h"], w["w_in"], F32, "proj")
    gains = jnp.pad(p["qk_norm_g"], ((0, 2), (0, 0)))
    pp = s["pp"] = qk_prep(s["proj"], gains, cos2, sin2, "qk_prep")
    sink = p["sink_a"].reshape(1, A_Q_HEADS)
    s["oa"], s["lse_a"] = band_attn_fwd(pp["qa"], pp["ka"], pp["va"], sink, seqs=A_KV_HEADS, G=A_GROUP,
                                        radius=A_RADIUS, name="attn_a")
    s["ob"], s["lse_b"] = [], []
    for g, dil in enumerate(B_DILS):
        o, lse = band_attn_fwd(_b_view(pp[f"qb{g}"], dil), _b_view(pp[f"kb{g}"], dil), _b_view(pp[f"vb{g}"], dil),
                               None, seqs=dil * B_HG, G=1, radius=B_RADIUS, name=f"attn_b{g}")
        s["ob"].append(o.reshape(n, B_HG * HEAD))
        s["lse_b"].append(lse.reshape(n, B_HG * HEAD))
    ob = b_combine_fwd(s["ob"], s["lse_b"], "b_combine")
    s["bias"] = rpb_expand(_rpb_windows(p["rpb_c"]), "rpb_expand")
    s["oc"], s["lse_c"] = c_attn_fwd(pp["qc"], pp["kc"], pp["vc"], s["bias"], "attn_c")
    s["o_in"] = (s["oa"], ob, s["oc"])
    s["ys"] = [mm_x_wcol(o, w[k], F32, "branch_" + k[-1]) for o, k in zip(s["o_in"], ("w_br_a", "w_br_b", "w_br_c"))]
    s["merged"] = gate_merge(s["proj"], s["ys"], "gate_merge")
    s["x_mid"] = mm_x_w(s["merged"], w["w_o"], "out_proj", res=x)
    s["h2"] = rmsnorm_fwd(s["x_mid"], p["norm2_g"], "norm2")
    s["gu"] = mm_x_wcol(s["h2"], w["w_gate_up"], F32, "gate_up", tn_pref=1408, stacked_out=2)
    s["act"] = swiglu_fwd(s["gu"], "swiglu")
    x_out = mm_x_w(s["act"], w["w_down"], "down", res=s["x_mid"])
    return x_out, s


def layer_bwd(dx_out, s, p, w, cos2, sin2):
    n, d = dx_out.shape
    pp = s["pp"]
    dact = mm_x_wT(dx_out, w["w_down"], "d_act")
    dw_down = mm_aT_d(s["act"], dx_out, "dw_down")
    dgu = swiglu_bwd(s["gu"], dact, "swiglu_bwd")
    dh2 = mm_x_wcolT(dgu, w["w_gate_up"], "d_h2", tm_pref=512, tk_pref=1408, stacked_in=2)
    dw_gu = mm_aT_d_wcol(s["h2"], dgu, "dw_gate_up", tn_pref=1408, stacked_in=2)
    dx_mid, dg2 = rmsnorm_bwd(s["x_mid"], p["norm2_g"], dh2, dx_out, "norm2_bwd")

    dmerged = mm_x_wT(dx_mid, w["w_o"], "d_merged")
    dw_o = mm_aT_d(s["merged"], dx_mid, "dw_o")
    dys, dgls = gate_bwd(s["proj"], s["ys"], dmerged, "gate_bwd")
    dos, dw_br = [], []
    for o, dy, k in zip(s["o_in"], dys, ("w_br_a", "w_br_b", "w_br_c")):
        dos.append(mm_x_wcolT(dy, w[k], "d_o_" + k[-1], tm_pref=512))
        dw_br.append(mm_aT_d_wcol(o, dy, "dw_br_" + k[-1]))

    grads = {}
    sink = p["sink_a"].reshape(1, A_Q_HEADS)
    grads["qa"], grads["ka"], grads["va"], dsink = band_attn_bwd(
        pp["qa"], pp["ka"], pp["va"], sink, s["oa"], s["lse_a"], dos[0], None,
        seqs=A_KV_HEADS, G=A_GROUP, radius=A_RADIUS, name="attn_a_bwd")
    dobs, dlses = b_combine_bwd(dos[1], s["ob"], s["lse_b"], "b_combine_bwd")
    for g, dil in enumerate(B_DILS):
        dq, dk, dv = band_attn_bwd(_b_view(pp[f"qb{g}"], dil), _b_view(pp[f"kb{g}"], dil), _b_view(pp[f"vb{g}"], dil),
                                   None, _b_view(s["ob"][g], dil), _b_view(s["lse_b"][g], dil), _b_view(dobs[g], dil),
                                   _b_view(dlses[g], dil), seqs=dil * B_HG, G=1, radius=B_RADIUS, name=f"attn_b{g}_bwd")
        grads[f"qb{g}"], grads[f"kb{g}"], grads[f"vb{g}"] = [t.reshape(n, B_HG * HEAD) for t in (dq, dk, dv)]
    grads["qc"], grads["kc"], grads["vc"], dbias = c_attn_bwd(pp["qc"], pp["kc"], pp["vc"], s["bias"], s["oc"],
                                                              s["lse_c"], dos[2], "attn_c_bwd")
    drpb = rpb_reduce(dbias, "rpb_reduce")[:, :2 * C_WIN_ROWS - 1, :2 * C_WIN_COLS - 1]
    gains = jnp.pad(p["qk_norm_g"], ((0, 2), (0, 0)))
    dqkv, dgains = qk_prep_bwd(s["proj"], gains, cos2, sin2, grads, "qk_prep_bwd")
    dproj = jnp.concatenate([dqkv] + list(dgls), axis=1)
    dh = mm_x_wcolT(dproj, w["w_in"], "d_h", tm_pref=512)
    dw_in = mm_aT_d_wcol(s["h"], dproj, "dw_in")
    dx_in, dg1 = rmsnorm_bwd(s["x"], p["norm1_g"], dh, dx_mid, "norm1_bwd")

    dws = [dw_in] + dw_br + [dw_o.reshape(N_CHIPS, d // N_CHIPS, d), dw_gu,
                             dw_down.reshape(N_CHIPS, dw_down.shape[0] // N_CHIPS, d)]
    small = {"norm1_g": dg1[0], "qk_norm_g": dgains[:6], "sink_a": dsink[:, :A_GROUP, 0].reshape(A_Q_HEADS),
             "rpb_c": drpb, "norm2_g": dg2[0]}
    return dx_in, dws, small


SMALL_NAMES = ("norm1_g", "qk_norm_g", "sink_a", "rpb_c", "norm2_g")


def _pack_small(parts, extra=None):
    flat = [parts[k].reshape(-1) for k in SMALL_NAMES]
    flat.append(jnp.zeros((1,), F32) if extra is None else extra.reshape(1))
    v = jnp.concatenate(flat)
    rows = -(-v.shape[0] // (8 * HEAD)) * 8
    return jnp.pad(v, (0, rows * HEAD - v.shape[0])).reshape(rows, HEAD)


def _unpack_small(pack, like):
    v = pack.reshape(-1)
    out, off = {}, 0
    for k in SMALL_NAMES:
        size = math.prod(like[k].shape)
        out[k] = v[off:off + size].reshape(like[k].shape)
        off += size
    return out, v[off]


def kernel(x, norm1_g, w_in, qk_norm_g, sink_a, rpb_c, w_br_a, w_br_b, w_br_c, w_o, norm2_g, w_gate_up, w_down, loss_target, m_norm1_g, m_w_in, m_qk_norm_g, m_sink_a, m_rpb_c, m_w_br_a, m_w_br_b, m_w_br_c, m_w_o, m_norm2_g, m_w_gate_up, m_w_down, v_norm1_g, v_w_in, v_qk_norm_g, v_sink_a, v_rpb_c, v_w_br_a, v_w_br_b, v_w_br_c, v_w_o, v_norm2_g, v_w_gate_up, v_w_down):
    big = dict(w_in=w_in, w_br_a=w_br_a, w_br_b=w_br_b, w_br_c=w_br_c, w_o=w_o, w_gate_up=w_gate_up, w_down=w_down)
    big_m = dict(w_in=m_w_in, w_br_a=m_w_br_a, w_br_b=m_w_br_b, w_br_c=m_w_br_c, w_o=m_w_o, w_gate_up=m_w_gate_up, w_down=m_w_down)
    big_v = dict(w_in=v_w_in, w_br_a=v_w_br_a, w_br_b=v_w_br_b, w_br_c=v_w_br_c, w_o=v_w_o, w_gate_up=v_w_gate_up, w_down=v_w_down)
    small = dict(norm1_g=norm1_g, qk_norm_g=qk_norm_g, sink_a=sink_a, rpb_c=rpb_c, norm2_g=norm2_g)
    small_m = dict(norm1_g=m_norm1_g, qk_norm_g=m_qk_norm_g, sink_a=m_sink_a, rpb_c=m_rpb_c, norm2_g=m_norm2_g)
    small_v = dict(norm1_g=v_norm1_g, qk_norm_g=v_qk_norm_g, sink_a=v_sink_a, rpb_c=v_rpb_c, norm2_g=v_norm2_g)
    n_layers = w_in.shape[0]
    n, d = x.shape[1], x.shape[2]
    c_idx = lax.axis_index("c").astype(jnp.int32).reshape(1)
    chip_idx = (2 * lax.axis_index("x") + lax.axis_index("y")).astype(jnp.int32).reshape(1)
    cos2, sin2 = _rope_tables(n)

    weights = []
    for l in range(n_layers):
        got = gather_weights([big[k][l].astype(CDT) for k in W_NAMES], "gather_weights")
        w = dict(zip(W_NAMES, got))
        w["w_o"] = w["w_o"].reshape(d, d)
        w["w_down"] = w["w_down"].reshape(-1, d)
        weights.append(w)

    xs, saved = x[0], []
    for l in range(n_layers):
        p = {k: small[k][l] for k in SMALL_NAMES}
        xs, s = layer_fwd(xs, p, weights[l], cos2, sin2)
        saved.append(s)
    loss_tile, dx = loss_head(xs, loss_target[0], "loss_head")

    halves = [None] * n_layers
    small_g = [None] * n_layers
    for l in reversed(range(n_layers)):
        p = {k: small[k][l] for k in SMALL_NAMES}
        dx, dws, small_g[l] = layer_bwd(dx, saved[l], p, weights[l], cos2, sin2)
        from_sibling = pair_exchange(dws, "pair_exchange")
        sums = [add_halves(c_idx, g, o, "add_halves_" + k) for g, o, k in zip(dws, from_sibling, W_NAMES)]
        from_chips = chip_exchange(sums, "chip_exchange")
        halves[l] = [add_chips(chip_idx, sm, r, "add_chips_" + k) for sm, r, k in zip(sums, from_chips, W_NAMES)]
    grad_big = dict(zip(W_NAMES, pair_share([h for hs in halves for h in hs], n_layers, "pair_share")))

    mine = {k: jnp.stack([small_g[l][k] for l in range(n_layers)]) for k in SMALL_NAMES}
    total = small_allreduce(_pack_small(mine, loss_tile[0, 0]), "small_allreduce")
    grad_small, loss = _unpack_small(total, small)

    outs = {}
    for k in W_NAMES:
        shape = big[k].shape
        flat = lambda t: t.reshape(-1, shape[-1])
        res = adamw(flat(big[k]), flat(grad_big[k]), flat(big_m[k]), flat(big_v[k]), "adamw_" + k)
        outs[k] = (grad_big[k],) + tuple(t.reshape(shape) for t in res)
    res = adamw(_pack_small(small), _pack_small(grad_small), _pack_small(small_m), _pack_small(small_v), "adamw_small")
    unp = [_unpack_small(t, small)[0] for t in res]
    for k in SMALL_NAMES:
        outs[k] = (grad_small[k],) + tuple(u[k] for u in unp)

    order = ("norm1_g", "w_in", "qk_norm_g", "sink_a", "rpb_c", "w_br_a", "w_br_b", "w_br_c", "w_o", "norm2_g",
             "w_gate_up", "w_down")
    return (loss, dx[None]) + tuple(outs[k][i] for i in range(4) for k in order)
```

```python
import functools
import math

import jax
import jax.numpy as jnp
from jax import lax
from jax.experimental import pallas as pl
from jax.experimental.pallas import tpu as pltpu

F32 = jnp.float32
CDT = jnp.bfloat16
XDT = jnp.bfloat16

HEAD = 128
NORM_EPS = 1e-6
ROPE_THETA = 10000.0
A_Q_HEADS, A_KV_HEADS, A_GROUP, A_RADIUS = 8, 2, 4, 128
B_DILS = (1, 4, 16)
B_RADIUS = 64
B_HG = 4
C_HEADS, GRID_W, C_WIN_ROWS, C_WIN_COLS = 8, 64, 8, 16
QKV_W = 9216
COL = dict(qa=0, ka=1024, va=1280, qb=1536, kb=3072, vb=4608, qc=6144, kc=7168, vc=8192)
NEG = -1e30
SCALE = HEAD ** -0.5
N_CHIPS = 4

ADAM_LR, ADAM_B1, ADAM_B2, ADAM_EPS, ADAM_WD, ADAM_STEP = 0.001, 0.9, 0.999, 1e-08, 0.01, 10

VMEM_LIMIT = 56 * 1024 * 1024
MESH = pl.DeviceIdType.MESH


def _pallas(body, **kw):
    return pl.pallas_call(body, **kw)


def _params(sem=None, **kw):
    if sem is not None:
        kw["dimension_semantics"] = sem
    return pltpu.CompilerParams(vmem_limit_bytes=VMEM_LIMIT, **kw)


def _tile(dim, pref, mult=128):
    best = None
    for t in range(mult, min(dim, pref) + 1, mult):
        if dim % t == 0:
            best = t
    return dim if best is None else best


def _sds(shape, dtype):
    return jax.ShapeDtypeStruct(tuple(shape), dtype)


_DIMS = {"nn": (((1,), (0,)), ((), ())), "nt": (((1,), (1,)), ((), ())), "tn": (((0,), (0,)), ((), ()))}


def _matmul(a, b, *, mode, grid, a_spec, b_spec, o_spec, out_shape, acc_shape, name, res=None, res_spec=None):
    nk = grid[2]
    has_res = res is not None

    def body(*refs):
        if has_res:
            a_ref, b_ref, r_ref, o_ref = refs[:4]
            rest = refs[4:]
        else:
            a_ref, b_ref, o_ref = refs[:3]
            r_ref = None
            rest = refs[3:]
        p = lax.dot_general(a_ref[...].astype(CDT), b_ref[...].astype(CDT), _DIMS[mode],
                            preferred_element_type=F32)

        def finish(acc):
            if has_res:
                acc = acc + r_ref[...].astype(F32)
            o_ref[...] = acc.astype(o_ref.dtype)

        if nk == 1:
            finish(p)
        else:
            acc_ref = rest[0]
            k = pl.program_id(2)

            @pl.when(k == 0)
            def _():
                acc_ref[...] = p

            @pl.when(k > 0)
            def _():
                acc_ref[...] += p

            @pl.when(k == nk - 1)
            def _():
                finish(acc_ref[...])

    in_specs = [a_spec, b_spec] + ([res_spec] if has_res else [])
    args = (a, b) + ((res,) if has_res else ())
    scratch = [] if nk == 1 else [pltpu.VMEM(acc_shape, F32)]
    return _pallas(body, name=name, grid=grid, in_specs=in_specs, out_specs=o_spec, out_shape=out_shape,
                   scratch_shapes=scratch, compiler_params=_params(("parallel", "parallel", "arbitrary")))(*args)


def mm_x_wcol(a, wg, out_dtype, name, tm_pref=1024, tn_pref=1024, stacked_out=1):
    m, k = a.shape
    ns = wg.shape[2]
    tm, tn = _tile(m, tm_pref, 8), _tile(ns, tn_pref)
    nj = ns // tn
    grid = (m // tm, N_CHIPS * nj, 1)
    a_spec = pl.BlockSpec((tm, k), lambda i, j, kk: (i, 0))
    b_spec = pl.BlockSpec((None, k, tn), lambda i, j, kk: (j // nj, 0, j % nj))
    if stacked_out == 1:
        o_spec = pl.BlockSpec((tm, tn), lambda i, j, kk: (i, j))
        out_shape = _sds((m, N_CHIPS * ns), out_dtype)
    else:
        per = N_CHIPS * nj // stacked_out
        o_spec = pl.BlockSpec((None, tm, tn), lambda i, j, kk: (j // per, i, j % per))
        out_shape = _sds((stacked_out, m, N_CHIPS * ns // stacked_out), out_dtype)
    return _matmul(a, wg, mode="nn", grid=grid, a_spec=a_spec, b_spec=b_spec, o_spec=o_spec,
                   out_shape=out_shape, acc_shape=(tm, tn), name=name)


def mm_x_wcolT(d, wg, name, res=None, tm_pref=1024, tk_pref=1024, stacked_in=1):
    kdim, ns = wg.shape[1], wg.shape[2]
    m = d.shape[-2]
    tm, tk = _tile(m, tm_pref, 8), _tile(ns, tk_pref)
    nkk = ns // tk
    grid = (m // tm, 1, N_CHIPS * nkk)
    if stacked_in == 1:
        a_spec = pl.BlockSpec((tm, tk), lambda i, j, kk: (i, kk))
    else:
        per = N_CHIPS * nkk // stacked_in
        a_spec = pl.BlockSpec((None, tm, tk), lambda i, j, kk: (kk // per, i, kk % per))
    b_spec = pl.BlockSpec((None, kdim, tk), lambda i, j, kk: (kk // nkk, 0, kk % nkk))
    o_spec = pl.BlockSpec((tm, kdim), lambda i, j, kk: (i, 0))
    return _matmul(d, wg, mode="nt", grid=grid, a_spec=a_spec, b_spec=b_spec, o_spec=o_spec,
                   out_shape=_sds((m, kdim), F32), acc_shape=(tm, kdim), name=name,
                   res=res, res_spec=None if res is None else o_spec)


def mm_aT_d_wcol(a, d, name, tm_pref=1024, tk_pref=1024, tn_pref=1024, stacked_in=1):
    m, kdim = a.shape
    ntot = d.shape[-1] * stacked_in
    ns = ntot // N_CHIPS
    tm, tkm, tn = _tile(kdim, tm_pref), _tile(m, tk_pref, 8), _tile(ns, tn_pref)
    nj = ns // tn
    grid = (kdim // tm, N_CHIPS * nj, m // tkm)
    a_spec = pl.BlockSpec((tkm, tm), lambda i, j, kk: (kk, i))
    if stacked_in == 1:
        b_spec = pl.BlockSpec((tkm, tn), lambda i, j, kk: (kk, j))
    else:
        per = N_CHIPS * nj // stacked_in
        b_spec = pl.BlockSpec((None, tkm, tn), lambda i, j, kk: (j // per, kk, j % per))
    o_spec = pl.BlockSpec((None, tm, tn), lambda i, j, kk: (j // nj, i, j % nj))
    return _matmul(a, d, mode="tn", grid=grid, a_spec=a_spec, b_spec=b_spec, o_spec=o_spec,
                   out_shape=_sds((N_CHIPS, kdim, ns), F32), acc_shape=(tm, tn), name=name)


def mm_x_w(a, w, name, res=None, out_dtype=F32, tm_pref=1024, tn_pref=1024, tk_pref=2048):
    m, k = a.shape
    n = w.shape[1]
    tm, tn, tk = _tile(m, tm_pref, 8), _tile(n, tn_pref), _tile(k, tk_pref)
    grid = (m // tm, n // tn, k // tk)
    o_spec = pl.BlockSpec((tm, tn), lambda i, j, kk: (i, j))
    return _matmul(a, w, mode="nn", grid=grid,
                   a_spec=pl.BlockSpec((tm, tk), lambda i, j, kk: (i, kk)),
                   b_spec=pl.BlockSpec((tk, tn), lambda i, j, kk: (kk, j)),
                   o_spec=o_spec, out_shape=_sds((m, n), out_dtype), acc_shape=(tm, tn), name=name,
                   res=res, res_spec=None if res is None else o_spec)


def mm_x_wT(d, w, name, out_dtype=F32, tm_pref=1024, tn_pref=1024):
    m, n = d.shape
    k = w.shape[0]
    tm, tn = _tile(m, tm_pref, 8), _tile(k, tn_pref)
    grid = (m // tm, k // tn, 1)
    return _matmul(d, w, mode="nt", grid=grid,
                   a_spec=pl.BlockSpec((tm, n), lambda i, j, kk: (i, 0)),
                   b_spec=pl.BlockSpec((tn, n), lambda i, j, kk: (j, 0)),
                   o_spec=pl.BlockSpec((tm, tn), lambda i, j, kk: (i, j)),
                   out_shape=_sds((m, k), out_dtype), acc_shape=(tm, tn), name=name)


def mm_aT_d(a, d, name, tm_pref=1024, tn_pref=1024, tk_pref=1024):
    m, k = a.shape
    n = d.shape[1]
    tm, tn, tk = _tile(k, tm_pref), _tile(n, tn_pref), _tile(m, tk_pref, 8)
    grid = (k // tm, n // tn, m // tk)
    return _matmul(a, d, mode="tn", grid=grid,
                   a_spec=pl.BlockSpec((tk, tm), lambda i, j, kk: (kk, i)),
                   b_spec=pl.BlockSpec((tk, tn), lambda i, j, kk: (kk, j)),
                   o_spec=pl.BlockSpec((tm, tn), lambda i, j, kk: (i, j)),
                   out_shape=_sds((k, n), F32), acc_shape=(tm, tn), name=name)


def rmsnorm_fwd(x, g, name):
    n, d = x.shape
    tm = _tile(n, 512, 8)

    def body(x_ref, g_ref, h_ref):
        xv = x_ref[...]
        r = lax.rsqrt(jnp.mean(xv * xv, axis=-1, keepdims=True) + NORM_EPS)
        h_ref[...] = (xv * r * g_ref[...]).astype(h_ref.dtype)

    return _pallas(body, name=name, grid=(n // tm,),
                   in_specs=[pl.BlockSpec((tm, d), lambda i: (i, 0)), pl.BlockSpec((1, d), lambda i: (0, 0))],
                   out_specs=pl.BlockSpec((tm, d), lambda i: (i, 0)), out_shape=_sds((n, d), CDT),
                   compiler_params=_params(("parallel",)))(x, g.reshape(1, d))


def rmsnorm_bwd(x, g, dh, dres, name):
    n, d = x.shape
    tm = _tile(n, 256, 8)

    def body(x_ref, g_ref, dh_ref, dres_ref, dx_ref, dg_ref):
        xv = x_ref[...]
        r = lax.rsqrt(jnp.mean(xv * xv, axis=-1, keepdims=True) + NORM_EPS)
        dhv = dh_ref[...]
        u = dhv * g_ref[...]
        c = jnp.mean(xv * u, axis=-1, keepdims=True)
        dx_ref[...] = dres_ref[...] + r * u - xv * (r * r * r * c)
        part = jnp.broadcast_to(jnp.sum(dhv * xv * r, axis=0, keepdims=True), (8, d))

        @pl.when(pl.program_id(0) == 0)
        def _():
            dg_ref[...] = part

        @pl.when(pl.program_id(0) > 0)
        def _():
            dg_ref[...] += part

    row = pl.BlockSpec((tm, d), lambda i: (i, 0))
    dx, dg = _pallas(body, name=name, grid=(n // tm,),
                     in_specs=[row, pl.BlockSpec((1, d), lambda i: (0, 0)), row, row],
                     out_specs=[row, pl.BlockSpec((8, d), lambda i: (0, 0))],
                     out_shape=[_sds((n, d), F32), _sds((8, d), F32)],
                     compiler_params=_params(("arbitrary",)))(x, g.reshape(1, d), dh, dres)
    return dx, dg


def _norm_rope(xh, g, cos2, sin2):
    r = lax.rsqrt(jnp.mean(xh * xh, axis=-1, keepdims=True) + NORM_EPS)
    y = xh * r * g
    if cos2 is not None:
        y = y * cos2 + pltpu.roll(y, HEAD // 2, 1) * sin2
    return y


def _norm_rope_bwd(xh, g, cos2, sin2, dout):
    if cos2 is not None:
        dy = dout * cos2 + pltpu.roll(dout * sin2, HEAD // 2, 1)
    else:
        dy = dout
    r = lax.rsqrt(jnp.mean(xh * xh, axis=-1, keepdims=True) + NORM_EPS)
    u = dy * g
    c = jnp.mean(xh * u, axis=-1, keepdims=True)
    return r * u - xh * (r * r * r * c), dy * xh * r


_QK_GROUPS = (("qa", COL["qa"], 8, 0, True), ("ka", COL["ka"], 2, 1, True),
              ("qb", COL["qb"], 12, 2, True), ("kb", COL["kb"], 12, 3, True),
              ("qc", COL["qc"], 8, 4, False), ("kc", COL["kc"], 8, 5, False))
_V_GROUPS = (("va", COL["va"], 2), ("vb", COL["vb"], 12), ("vc", COL["vc"], 8))
_PREP_OUT = (("qa", 8), ("ka", 2), ("va", 2)) + tuple((f"{t}b{g}", 4) for t in "qkv" for g in range(3)) + (
    ("qc", 8), ("kc", 8), ("vc", 8))


def _prep_src(name):
    if name[1] == "b":
        base = COL[name[0] + "b"] + int(name[2]) * B_HG * HEAD
        gain = {"q": 2, "k": 3, "v": None}[name[0]]
        return base, gain, name[0] != "v"
    base = COL[name]
    gain = {"qa": 0, "ka": 1, "va": None, "qc": 4, "kc": 5, "vc": None}[name]
    return base, gain, name in ("qa", "ka")


def qk_prep(proj, gains, cos2, sin2, name):
    n = proj.shape[0]
    tm = _tile(n, 256, 8)

    def body(p_ref, g_ref, c_ref, s_ref, *outs):
        cos2v, sin2v = c_ref[...], s_ref[...]
        for (nm, heads), o_ref in zip(_PREP_OUT, outs):
            base, gain, rope = _prep_src(nm)
            for h in range(heads):
                xh = p_ref[:, base + h * HEAD: base + (h + 1) * HEAD]
                if gain is None:
                    y = xh
                else:
                    y = _norm_rope(xh, g_ref[gain:gain + 1, :], cos2v if rope else None, sin2v if rope else None)
                o_ref[:, h * HEAD:(h + 1) * HEAD] = y.astype(o_ref.dtype)

    tab = pl.BlockSpec((tm, HEAD), lambda i: (i, 0))
    outs = _pallas(body, name=name, grid=(n // tm,),
                   in_specs=[pl.BlockSpec((tm, QKV_W), lambda i: (i, 0)), pl.BlockSpec((8, HEAD), lambda i: (0, 0)), tab, tab],
                   out_specs=[pl.BlockSpec((tm, h * HEAD), lambda i: (i, 0)) for _, h in _PREP_OUT],
                   out_shape=[_sds((n, h * HEAD), CDT) for _, h in _PREP_OUT],
                   compiler_params=_params(("parallel",)))(proj, gains, cos2, sin2)
    return dict(zip([nm for nm, _ in _PREP_OUT], outs))


def qk_prep_bwd(proj, gains, cos2, sin2, grads, name):
    n = proj.shape[0]
    tm = _tile(n, 128, 8)
    names = [nm for nm, _ in _PREP_OUT]

    def body(p_ref, g_ref, c_ref, s_ref, *refs):
        g_refs, dp_ref, dg_ref = refs[:len(names)], refs[len(names)], refs[len(names) + 1]
        cos2v, sin2v = c_ref[...], s_ref[...]
        dg = [jnp.zeros((tm, HEAD), F32) for _ in range(6)]
        for (nm, heads), gr in zip(_PREP_OUT, g_refs):
            base, gain, rope = _prep_src(nm)
            for h in range(heads):
                sl = slice(base + h * HEAD, base + (h + 1) * HEAD)
                dout = gr[:, h * HEAD:(h + 1) * HEAD]
                if gain is None:
                    dx = dout
                else:
                    dx, dgr = _norm_rope_bwd(p_ref[:, sl], g_ref[gain:gain + 1, :], cos2v if rope else None,
                                             sin2v if rope else None, dout)
                    dg[gain] = dg[gain] + dgr
                dp_ref[:, sl] = dx.astype(dp_ref.dtype)
        part = jnp.concatenate([jnp.sum(t, axis=0, keepdims=True) for t in dg] + [jnp.zeros((2, HEAD), F32)], axis=0)

        @pl.when(pl.program_id(0) == 0)
        def _():
            dg_ref[...] = part

        @pl.when(pl.program_id(0) > 0)
        def _():
            dg_ref[...] += part

    tab = pl.BlockSpec((tm, HEAD), lambda i: (i, 0))
    dp, dg = _pallas(body, name=name, grid=(n // tm,),
                     in_specs=[pl.BlockSpec((tm, QKV_W), lambda i: (i, 0)), pl.BlockSpec((8, HEAD), lambda i: (0, 0)), tab, tab]
                     + [pl.BlockSpec((tm, h * HEAD), lambda i: (i, 0)) for _, h in _PREP_OUT],
                     out_specs=[pl.BlockSpec((tm, QKV_W), lambda i: (i, 0)), pl.BlockSpec((8, HEAD), lambda i: (0, 0))],
                     out_shape=[_sds((n, QKV_W), CDT), _sds((8, HEAD), F32)],
                     compiler_params=_params(("arbitrary",)))(proj, gains, cos2, sin2, *[grads[k] for k in names])
    return dp, dg


def _band_geometry(m, bq_pref, radius):
    bq = min(bq_pref, m)
    return bq, min(bq + 2 * radius, m)


def _band_window(i, bq, radius, m, w):
    start = pl.multiple_of(jnp.clip(i * bq - radius, 0, m - w), 64)
    qpos = i * bq + lax.broadcasted_iota(jnp.int32, (bq, w), 0)
    kpos = start + lax.broadcasted_iota(jnp.int32, (bq, w), 1)
    return start, jnp.abs(kpos - qpos) <= radius


def band_attn_fwd(q, k, v, sink, *, seqs, G, nh, radius, name, bq_pref=128):
    m = q.shape[0]
    bq, w = _band_geometry(m, bq_pref, radius)
    has_sink = sink is not None

    def body(*refs):
        if has_sink:
            sink_ref, q_ref, k_ref, v_ref, o_ref, lse_ref = refs
        else:
            q_ref, k_ref, v_ref, o_ref, lse_ref = refs
        s_id, i = pl.program_id(0), pl.program_id(1)
        start, valid = _band_window(i, bq, radius, m, w)
        for h in range(nh):
            ks = slice(h * HEAD, (h + 1) * HEAD)
            k_t = k_ref[pl.ds(start, w), ks]
            v_t = v_ref[pl.ds(start, w), ks]
            for g in range(G):
                sl = slice((h * G + g) * HEAD, (h * G + g + 1) * HEAD)
                s = lax.dot_general(q_ref[:, sl], k_t, _DIMS["nt"], preferred_element_type=F32) * SCALE
                s = jnp.where(valid, s, NEG)
                mx = jnp.max(s, axis=-1, keepdims=True)
                if has_sink:
                    sk = sink_ref[0, (s_id * nh + h) * G + g]
                    mx = jnp.maximum(mx, sk)
                p = jnp.exp(s - mx)
                den = jnp.sum(p, axis=-1, keepdims=True)
                if has_sink:
                    den = den + jnp.exp(sk - mx)
                o_ref[:, sl] = jnp.dot((p / den).astype(CDT), v_t, preferred_element_type=F32)
                lse_ref[:, sl] = jnp.broadcast_to(mx + jnp.log(den), (bq, HEAD))

    qspec = pl.BlockSpec((bq, nh * G * HEAD), lambda s, i: (i, s))
    kspec = pl.BlockSpec((m, nh * HEAD), lambda s, i: (0, s))
    in_specs = ([pl.BlockSpec(memory_space=pltpu.SMEM)] if has_sink else []) + [qspec, kspec, kspec]
    args = ((sink,) if has_sink else ()) + (q, k, v)
    return _pallas(body, name=name, grid=(seqs // nh, m // bq), in_specs=in_specs, out_specs=[qspec, qspec],
                   out_shape=[_sds(q.shape, F32), _sds(q.shape, F32)],
                   compiler_params=_params(("parallel", "arbitrary")))(*args)


def band_attn_bwd(q, k, v, sink, o, lse, do, dlse, *, seqs, G, nh, radius, name, bq_pref=128):
    m = q.shape[0]
    bq, w = _band_geometry(m, bq_pref, radius)
    has_sink, has_dlse = sink is not None, dlse is not None
    assert nh * G <= 8

    def body(*refs):
        refs = list(refs)
        sink_ref = refs.pop(0) if has_sink else None
        q_ref, k_ref, v_ref, o_ref, lse_ref, do_ref = refs[:6]
        refs = refs[6:]
        dlse_ref = refs.pop(0) if has_dlse else None
        dq_ref, dk_ref, dv_ref = refs[:3]
        dsink_ref = refs[3] if has_sink else None
        s_id, i = pl.program_id(0), pl.program_id(1)

        @pl.when(i == 0)
        def _():
            dk_ref[...] = jnp.zeros_like(dk_ref)
            dv_ref[...] = jnp.zeros_like(dv_ref)
            if has_sink:
                dsink_ref[...] = jnp.zeros_like(dsink_ref)

        start, valid = _band_window(i, bq, radius, m, w)
        for h in range(nh):
            ks = slice(h * HEAD, (h + 1) * HEAD)
            k_t = k_ref[pl.ds(start, w), ks]
            v_t = v_ref[pl.ds(start, w), ks]
            dk_acc = jnp.zeros((w, HEAD), F32)
            dv_acc = jnp.zeros((w, HEAD), F32)
            for g in range(G):
                row = h * G + g
                sl = slice(row * HEAD, (row + 1) * HEAD)
                q_t = q_ref[:, sl]
                lse_t = lse_ref[:, sl][:, :1]
                s = lax.dot_general(q_t, k_t, _DIMS["nt"], preferred_element_type=F32) * SCALE
                p = jnp.exp(jnp.where(valid, s, NEG) - lse_t)
                do_t = do_ref[:, sl]
                delta = jnp.sum(do_t * o_ref[:, sl], axis=-1, keepdims=True)
                do_c = do_t.astype(CDT)
                dp = lax.dot_general(do_c, v_t, _DIMS["nt"], preferred_element_type=F32)
                dv_acc = dv_acc + lax.dot_general(p.astype(CDT), do_c, _DIMS["tn"], preferred_element_type=F32)
                t = dp - delta
                if has_dlse:
                    t = t + dlse_ref[:, sl][:, :1]
                ds = ((p * t) * SCALE).astype(CDT)
                dq_ref[:, sl] = jnp.dot(ds, k_t, preferred_element_type=F32)
                dk_acc = dk_acc + lax.dot_general(ds, q_t, _DIMS["tn"], preferred_element_type=F32)
                if has_sink:
                    sk = sink_ref[0, (s_id * nh + h) * G + g]
                    part = -jnp.sum(jnp.exp(sk - lse_t) * delta, axis=0, keepdims=True)
                    dsink_ref[row:row + 1, :] += jnp.broadcast_to(part, (1, HEAD))
            dk_ref[pl.ds(start, w), ks] += dk_acc
            dv_ref[pl.ds(start, w), ks] += dv_acc

    qspec = pl.BlockSpec((bq, nh * G * HEAD), lambda s, i: (i, s))
    kspec = pl.BlockSpec((m, nh * HEAD), lambda s, i: (0, s))
    in_specs = ([pl.BlockSpec(memory_space=pltpu.SMEM)] if has_sink else []) + [qspec, kspec, kspec, qspec, qspec, qspec]
    in_specs += [qspec] if has_dlse else []
    args = ((sink,) if has_sink else ()) + (q, k, v, o, lse, do) + ((dlse,) if has_dlse else ())
    out_specs = [qspec, kspec, kspec]
    out_shape = [_sds(q.shape, F32), _sds(k.shape, F32), _sds(k.shape, F32)]
    if has_sink:
        out_specs.append(pl.BlockSpec((None, 8, HEAD), lambda s, i: (s, 0, 0)))
        out_shape.append(_sds((seqs // nh, 8, HEAD), F32))
    return _pallas(body, name=name, grid=(seqs // nh, m // bq), in_specs=in_specs, out_specs=out_specs,
                   out_shape=out_shape, compiler_params=_params(("parallel", "arbitrary")))(*args)


def _group_weights(lses):
    mx = jnp.maximum(jnp.maximum(lses[0], lses[1]), lses[2])
    e = [jnp.exp(l - mx) for l in lses]
    tot = e[0] + e[1] + e[2]
    return [t / tot for t in e]


def b_combine_fwd(os_, lses, name):
    n, wd = os_[0].shape
    tm = _tile(n, 512, 8)

    def body(o0, o1, o2, l0, l1, l2, out_ref):
        wts = _group_weights([l0[...], l1[...], l2[...]])
        out_ref[...] = wts[0] * o0[...] + wts[1] * o1[...] + wts[2] * o2[...]

    row = pl.BlockSpec((tm, wd), lambda i: (i, 0))
    return _pallas(body, name=name, grid=(n // tm,), in_specs=[row] * 6, out_specs=row,
                   out_shape=_sds((n, wd), F32), compiler_params=_params(("parallel",)))(*os_, *lses)


def b_combine_bwd(dout, os_, lses, name):
    n, wd = dout.shape
    tm = _tile(n, 256, 8)

    def body(d_ref, o0, o1, o2, l0, l1, l2, do0, do1, do2, dl0, dl1, dl2):
        dv = d_ref[...]
        wts = _group_weights([l0[...], l1[...], l2[...]])
        dws = []
        for o_ref in (o0, o1, o2):
            prod = dv * o_ref[...]
            cols = []
            for h in range(wd // HEAD):
                sseg = jnp.sum(prod[:, h * HEAD:(h + 1) * HEAD], axis=-1, keepdims=True)
                cols.append(jnp.broadcast_to(sseg, (tm, HEAD)))
            dws.append(jnp.concatenate(cols, axis=-1))
        mean = wts[0] * dws[0] + wts[1] * dws[1] + wts[2] * dws[2]
        for wt, dw, do_ref, dl_ref in zip(wts, dws, (do0, do1, do2), (dl0, dl1, dl2)):
            do_ref[...] = wt * dv
            dl_ref[...] = wt * (dw - mean)

    row = pl.BlockSpec((tm, wd), lambda i: (i, 0))
    outs = _pallas(body, name=name, grid=(n // tm,), in_specs=[row] * 7, out_specs=[row] * 6,
                   out_shape=[_sds((n, wd), F32)] * 6, compiler_params=_params(("parallel",)))(dout, *os_, *lses)
    return outs[:3], outs[3:]


def _c_rows(n):
    rows = n // GRID_W
    return rows, min(C_WIN_ROWS, rows)


def _c_row_start(r, rows, wr):
    return jnp.clip(r - wr // 2, 0, rows - wr)


def _c_bias_index(r, rows, wr):
    return _c_row_start(r, rows, wr) - r + (C_WIN_ROWS - 1)


def _col_shift_select(tile, cq, inverse):
    lanes = tile.shape[1]
    for b in range(6):
        amt = (lanes - (1 << b)) if inverse else (1 << b)
        tile = jnp.where(((cq >> b) & 1) == 1, pltpu.roll(tile, amt, 1), tile)
    return tile


def rpb_expand(rwin, name):
    lanes = rwin.shape[-1]

    def body(r_ref, b_ref):
        cq = lax.broadcasted_iota(jnp.int32, (GRID_W, lanes), 0)
        ck = lax.broadcasted_iota(jnp.int32, (GRID_W, lanes), 1) % GRID_W
        tile = jnp.broadcast_to(r_ref[...], (GRID_W, lanes))
        tile = pltpu.roll(tile, lanes - (C_WIN_COLS - 1), 1)
        tile = _col_shift_select(tile, cq, False)
        cs = jnp.clip(cq - C_WIN_COLS // 2, 0, GRID_W - C_WIN_COLS)
        ok = (ck >= cs) & (ck < cs + C_WIN_COLS)
        b_ref[...] = jnp.where(ok, tile, NEG)

    return _pallas(body, name=name, grid=(C_HEADS, C_WIN_ROWS),
                   in_specs=[pl.BlockSpec((None, None, 1, lanes), lambda h, i: (h, i, 0, 0))],
                   out_specs=pl.BlockSpec((None, None, GRID_W, lanes), lambda h, i: (h, i, 0, 0)),
                   out_shape=_sds((C_HEADS, C_WIN_ROWS, GRID_W, lanes), F32),
                   compiler_params=_params(("parallel", "parallel")))(rwin)


def rpb_reduce(dbias, name):
    lanes = dbias.shape[-1]
    wr = lanes // GRID_W

    def body(d_ref, o_ref, acc_ref):
        i0 = pl.program_id(1)
        cq = lax.broadcasted_iota(jnp.int32, (GRID_W, lanes), 0)
        tile = _col_shift_select(d_ref[...], cq, True)
        tile = pltpu.roll(tile, C_WIN_COLS - 1, 1)
        vec = jnp.sum(tile, axis=0, keepdims=True)

        @pl.when(i0 == 0)
        def _():
            acc_ref[...] = jnp.zeros_like(acc_ref)

        for w in range(wr):
            acc_ref[pl.ds(i0 + w, 1), :] += vec[:, w * GRID_W:(w + 1) * GRID_W]

        @pl.when(i0 == C_WIN_ROWS - 1)
        def _():
            o_ref[...] = acc_ref[...]

    return _pallas(body, name=name, grid=(C_HEADS, C_WIN_ROWS),
                   in_specs=[pl.BlockSpec((None, None, GRID_W, lanes), lambda h, i: (h, i, 0, 0))],
                   out_specs=pl.BlockSpec((None, 16, GRID_W), lambda h, i: (h, 0, 0)),
                   out_shape=_sds((C_HEADS, 16, GRID_W), F32),
                   scratch_shapes=[pltpu.VMEM((16, GRID_W), F32)],
                   compiler_params=_params(("parallel", "arbitrary")))(dbias)


def _store_or_add(ref, val, first):
    @pl.when(first)
    def _():
        ref[...] = val

    @pl.when(jnp.logical_not(first))
    def _():
        ref[...] += val


def c_attn_fwd(q, k, v, bias, name, nh=4):
    n = q.shape[0]
    rows, wr = _c_rows(n)
    wk = wr * GRID_W

    def body(q_ref, k_ref, v_ref, b_ref, o_ref, lse_ref):
        r = pl.program_id(1)
        start = pl.multiple_of(_c_row_start(r, rows, wr) * GRID_W, GRID_W)
        for h in range(nh):
            sl = slice(h * HEAD, (h + 1) * HEAD)
            k_t = k_ref[pl.ds(start, wk), sl]
            v_t = v_ref[pl.ds(start, wk), sl]
            s = lax.dot_general(q_ref[:, sl], k_t, _DIMS["nt"], preferred_element_type=F32) * SCALE + b_ref[h]
            mx = jnp.max(s, axis=-1, keepdims=True)
            p = jnp.exp(s - mx)
            den = jnp.sum(p, axis=-1, keepdims=True)
            o_ref[:, sl] = jnp.dot((p / den).astype(CDT), v_t, preferred_element_type=F32)
            lse_ref[:, sl] = jnp.broadcast_to(mx + jnp.log(den), (GRID_W, HEAD))

    qspec = pl.BlockSpec((GRID_W, nh * HEAD), lambda h, r: (r, h))
    kspec = pl.BlockSpec((n, nh * HEAD), lambda h, r: (0, h))
    bspec = pl.BlockSpec((nh, None, GRID_W, wk), lambda h, r: (h, _c_bias_index(r, rows, wr), 0, 0))
    return _pallas(body, name=name, grid=(C_HEADS // nh, rows), in_specs=[qspec, kspec, kspec, bspec],
                   out_specs=[qspec, qspec], out_shape=[_sds(q.shape, F32), _sds(q.shape, F32)],
                   compiler_params=_params(("parallel", "arbitrary")))(q, k, v, bias)


def c_attn_bwd(q, k, v, bias, o, lse, do, name, nh=2):
    n = q.shape[0]
    rows, wr = _c_rows(n)
    wk = wr * GRID_W

    def body(q_ref, k_ref, v_ref, b_ref, o_ref, lse_ref, do_ref, dq_ref, dk_ref, dv_ref, db_ref):
        r = pl.program_id(1)
        rs = _c_row_start(r, rows, wr)
        start = pl.multiple_of(rs * GRID_W, GRID_W)

        @pl.when(r == 0)
        def _():
            dk_ref[...] = jnp.zeros_like(dk_ref)
            dv_ref[...] = jnp.zeros_like(dv_ref)

        prev = _c_row_start(jnp.maximum(r - 1, 0), rows, wr) - jnp.maximum(r - 1, 0)
        first = (r == 0) | (prev != rs - r)
        for h in range(nh):
            sl = slice(h * HEAD, (h + 1) * HEAD)
            k_t = k_ref[pl.ds(start, wk), sl]
            v_t = v_ref[pl.ds(start, wk), sl]
            q_t = q_ref[:, sl]
            s = lax.dot_general(q_t, k_t, _DIMS["nt"], preferred_element_type=F32) * SCALE + b_ref[h]
            p = jnp.exp(s - lse_ref[:, sl][:, :1])
            do_t = do_ref[:, sl]
            delta = jnp.sum(do_t * o_ref[:, sl], axis=-1, keepdims=True)
            do_c = do_t.astype(CDT)
            dp = lax.dot_general(do_c, v_t, _DIMS["nt"], preferred_element_type=F32)
            dv_ref[pl.ds(start, wk), sl] += lax.dot_general(p.astype(CDT), do_c, _DIMS["tn"], preferred_element_type=F32)
            ds = p * (dp - delta)
            ds_c = (ds * SCALE).astype(CDT)
            dq_ref[:, sl] = jnp.dot(ds_c, k_t, preferred_element_type=F32)
            dk_ref[pl.ds(start, wk), sl] += lax.dot_general(ds_c, q_t, _DIMS["tn"], preferred_element_type=F32)
            _store_or_add(db_ref.at[h], ds, first)

    qspec = pl.BlockSpec((GRID_W, nh * HEAD), lambda h, r: (r, h))
    kspec = pl.BlockSpec((n, nh * HEAD), lambda h, r: (0, h))
    bspec = pl.BlockSpec((nh, None, GRID_W, wk), lambda h, r: (h, _c_bias_index(r, rows, wr), 0, 0))
    return _pallas(body, name=name, grid=(C_HEADS // nh, rows),
                   in_specs=[qspec, kspec, kspec, bspec, qspec, qspec, qspec],
                   out_specs=[qspec, kspec, kspec, bspec],
                   out_shape=[_sds(q.shape, F32), _sds(k.shape, F32), _sds(k.shape, F32), _sds(bias.shape, F32)],
                   compiler_params=_params(("parallel", "arbitrary")))(q, k, v, bias, o, lse, do)


def _sigmoid(z):
    return 1.0 / (1.0 + jnp.exp(-z))


def _gate_specs(n, d):
    tm, tn = _tile(n, 256, 8), _tile(math.gcd(d, QKV_W), 1024)
    nj = d // tn
    tile = pl.BlockSpec((tm, tn), lambda i, j: (i, j))
    gl = [pl.BlockSpec((tm, tn), functools.partial(lambda i, j, b: (i, (QKV_W + b * d) // tn + j), b=b)) for b in range(3)]
    return tm, tn, nj, tile, gl


def gate_merge(proj, ys, name):
    n, d = ys[0].shape
    tm, tn, nj, tile, gl = _gate_specs(n, d)

    def body(g0, g1, g2, y0, y1, y2, out_ref):
        acc = _sigmoid(g0[...]) * y0[...] + _sigmoid(g1[...]) * y1[...] + _sigmoid(g2[...]) * y2[...]
        out_ref[...] = acc.astype(out_ref.dtype)

    return _pallas(body, name=name, grid=(n // tm, nj), in_specs=gl + [tile] * 3, out_specs=tile,
                   out_shape=_sds((n, d), CDT), compiler_params=_params(("parallel", "parallel")))(proj, proj, proj, *ys)


def gate_bwd(proj, ys, dmerged, name):
    n, d = dmerged.shape
    tm, tn, nj, tile, gl = _gate_specs(n, d)

    def body(g0, g1, g2, y0, y1, y2, dm_ref, dy0, dy1, dy2, dg0, dg1, dg2):
        dm = dm_ref[...]
        for g_ref, y_ref, dy_ref, dg_ref in ((g0, y0, dy0, dg0), (g1, y1, dy1, dg1), (g2, y2, dy2, dg2)):
            sg = _sigmoid(g_ref[...])
            dy_ref[...] = (dm * sg).astype(dy_ref.dtype)
            dg_ref[...] = (dm * y_ref[...] * sg * (1.0 - sg)).astype(dg_ref.dtype)

    outs = _pallas(body, name=name, grid=(n // tm, nj), in_specs=gl + [tile] * 4, out_specs=[tile] * 6,
                   out_shape=[_sds((n, d), CDT)] * 6,
                   compiler_params=_params(("parallel", "parallel")))(proj, proj, proj, *ys, dmerged)
    return outs[:3], outs[3:]


def swiglu_fwd(gu, name):
    _, n, ff = gu.shape
    tm, tn = _tile(n, 512, 8), _tile(ff, 1024)

    def body(g_ref, u_ref, a_ref):
        gt = g_ref[...]
        a_ref[...] = (gt * _sigmoid(gt) * u_ref[...]).astype(a_ref.dtype)

    return _pallas(body, name=name, grid=(n // tm, ff // tn),
                   in_specs=[pl.BlockSpec((None, tm, tn), lambda i, j: (0, i, j)),
                             pl.BlockSpec((None, tm, tn), lambda i, j: (1, i, j))],
                   out_specs=pl.BlockSpec((tm, tn), lambda i, j: (i, j)), out_shape=_sds((n, ff), CDT),
                   compiler_params=_params(("parallel", "parallel")))(gu, gu)


def swiglu_bwd(gu, dact, name):
    _, n, ff = gu.shape
    tm, tn = _tile(n, 512, 8), _tile(ff, 1024)

    def body(g_ref, u_ref, da_ref, d_ref):
        gt, up, da = g_ref[...], u_ref[...], da_ref[...]
        sg = _sigmoid(gt)
        d_ref[0] = (da * up * (sg + gt * sg * (1.0 - sg))).astype(d_ref.dtype)
        d_ref[1] = (da * gt * sg).astype(d_ref.dtype)

    return _pallas(body, name=name, grid=(n // tm, ff // tn),
                   in_specs=[pl.BlockSpec((None, tm, tn), lambda i, j: (0, i, j)),
                             pl.BlockSpec((None, tm, tn), lambda i, j: (1, i, j)),
                             pl.BlockSpec((tm, tn), lambda i, j: (i, j))],
                   out_specs=pl.BlockSpec((2, tm, tn), lambda i, j: (0, i, j)), out_shape=_sds((2, n, ff), CDT),
                   compiler_params=_params(("parallel", "parallel")))(gu, gu, dact)


def loss_head(y, target, name):
    n, d = y.shape
    tm = _tile(n, 512, 8)
    nsteps = n // tm

    def body(y_ref, t_ref, l_ref, dy_ref, acc_ref):
        i = pl.program_id(0)
        e = y_ref[...] - t_ref[...]
        dy_ref[...] = e * (1.0 / d)
        part = jnp.sum((e * e).reshape(tm // 8, 8, d), axis=0)

        @pl.when(i == 0)
        def _():
            acc_ref[...] = part

        @pl.when(i > 0)
        def _():
            acc_ref[...] += part

        @pl.when(i == nsteps - 1)
        def _():
            tot = jnp.sum(jnp.sum(acc_ref[...], axis=1, keepdims=True), axis=0, keepdims=True) * (0.5 / d)
            l_ref[...] = jnp.broadcast_to(tot, (8, HEAD))

    row = pl.BlockSpec((tm, d), lambda i: (i, 0))
    return _pallas(body, name=name, grid=(nsteps,), in_specs=[row, row],
                   out_specs=[pl.BlockSpec((8, HEAD), lambda i: (0, 0)), row],
                   out_shape=[_sds((8, HEAD), F32), _sds((n, d), F32)],
                   scratch_shapes=[pltpu.VMEM((8, d), F32)],
                   compiler_params=_params(("arbitrary",)))(y, target)


def adamw(w, g, m, v, name):
    r, c = w.shape
    tr = _tile(r, max(8, (1 << 19) // c), 8)
    c1 = 1.0 - ADAM_B1 ** ADAM_STEP
    c2 = 1.0 - ADAM_B2 ** ADAM_STEP

    def body(w_ref, g_ref, m_ref, v_ref, d_ref, mo_ref, vo_ref):
        gv = g_ref[...]
        mn = ADAM_B1 * m_ref[...] + (1.0 - ADAM_B1) * gv
        vn = ADAM_B2 * v_ref[...] + (1.0 - ADAM_B2) * (gv * gv)
        d_ref[...] = -ADAM_LR * ((mn / c1) / (jnp.sqrt(vn / c2) + ADAM_EPS) + ADAM_WD * w_ref[...])
        mo_ref[...] = mn
        vo_ref[...] = vn

    row = pl.BlockSpec((tr, c), lambda i: (i, 0))
    return _pallas(body, name=name, grid=(r // tr,), in_specs=[row] * 4, out_specs=[row] * 3,
                   out_shape=[_sds((r, c), F32)] * 3, compiler_params=_params(("parallel",)))(w, g, m, v)


ANY = pl.BlockSpec(memory_space=pl.ANY)


def _place():
    x, y, c = lax.axis_index("x"), lax.axis_index("y"), lax.axis_index("c")
    return x, y, c, [(1 - x, y), (x, 1 - y), (1 - x, 1 - y)]


def _rcopy(src, dst, send_sems, recv_sems, k, to):
    return pltpu.make_async_remote_copy(src_ref=src, dst_ref=dst, send_sem=send_sems.at[k], recv_sem=recv_sems.at[k],
                                        device_id=to, device_id_type=MESH)


def cast_place(chip_idx, shard, name):
    k, ns = shard.shape
    tr = _tile(k, max(16, (1 << 19) // ns), 16)

    def body(k_ref, s_ref, o_ref):
        o_ref[...] = s_ref[...].astype(o_ref.dtype)

    gs = pltpu.PrefetchScalarGridSpec(
        num_scalar_prefetch=1, grid=(k // tr,),
        in_specs=[pl.BlockSpec((tr, ns), lambda i, k_ref: (i, 0))],
        out_specs=pl.BlockSpec((None, tr, ns), lambda i, k_ref: (k_ref[0], i, 0)))
    return _pallas(body, name=name, grid_spec=gs, out_shape=_sds((N_CHIPS, k, ns), CDT),
                   compiler_params=_params(("parallel",)))(chip_idx, shard)


def gather_weights(bufs, name):
    nt = len(bufs)

    def body(*refs):
        outs = refs[nt:2 * nt]
        send_sems, recv_sems = refs[2 * nt:]
        x, y, c, chips = _place()
        me, sibling = 2 * x + y, (x, y, 1 - c)
        remote = []
        for t in range(nt):
            kh = outs[t].shape[1] // 2
            mine = outs[t].at[me, pl.ds(c * kh, kh)]
            for j, (px, py) in enumerate(chips):
                remote.append(_rcopy(mine, mine, send_sems, recv_sems, 6 * t + j, (px, py, c)))
                remote[-1].start()
        for t in range(nt):
            kh = outs[t].shape[1] // 2
            for j, (px, py) in enumerate(chips):
                blk = outs[t].at[2 * px + py, pl.ds(c * kh, kh)]
                _rcopy(blk, blk, send_sems, recv_sems, 6 * t + j, (px, py, c)).wait_recv()
                remote.append(_rcopy(blk, blk, send_sems, recv_sems, 6 * t + 3 + j, sibling))
                remote[-1].start()
        for t in range(nt):
            kh = outs[t].shape[1] // 2
            for j, (px, py) in enumerate(chips):
                blk = outs[t].at[2 * px + py, pl.ds((1 - c) * kh, kh)]
                _rcopy(blk, blk, send_sems, recv_sems, 6 * t + 3 + j, sibling).wait_recv()
        for cp in remote:
            cp.wait_send()

    return _pallas(body, name=name, in_specs=[ANY] * nt, out_specs=[ANY] * nt,
                   out_shape=[_sds(b.shape, b.dtype) for b in bufs],
                   input_output_aliases={t: t for t in range(nt)},
                   scratch_shapes=[pltpu.SemaphoreType.DMA((6 * nt,)), pltpu.SemaphoreType.DMA((6 * nt,))],
                   compiler_params=pltpu.CompilerParams(has_side_effects=True))(*bufs)


def pair_exchange(grads, name):
    nt = len(grads)

    def body(*refs):
        ins, outs = refs[:nt], refs[nt:2 * nt]
        send_sems, recv_sems = refs[2 * nt:]
        x, y, c, _ = _place()
        sibling = (x, y, 1 - c)
        cps = []
        for t in range(nt):
            kh = ins[t].shape[1] // 2
            cps.append(_rcopy(ins[t].at[:, pl.ds((1 - c) * kh, kh), :], outs[t], send_sems, recv_sems, t, sibling))
            cps[-1].start()
        for cp in cps:
            cp.wait_recv()
        for cp in cps:
            cp.wait_send()

    return _pallas(body, name=name, in_specs=[ANY] * nt, out_specs=[ANY] * nt,
                   out_shape=[_sds((N_CHIPS, g.shape[1] // 2, g.shape[2]), g.dtype) for g in grads],
                   scratch_shapes=[pltpu.SemaphoreType.DMA((nt,)), pltpu.SemaphoreType.DMA((nt,))],
                   compiler_params=pltpu.CompilerParams(has_side_effects=True))(*grads)


def chip_exchange(sums, name):
    nt = len(sums)

    def body(*refs):
        ins, outs = refs[:nt], refs[nt:2 * nt]
        send_sems, recv_sems = refs[2 * nt:]
        x, y, c, chips = _place()
        cps = []
        for t in range(nt):
            for j, (px, py) in enumerate(chips):
                cps.append(_rcopy(ins[t].at[2 * px + py], outs[t].at[j], send_sems, recv_sems, 3 * t + j, (px, py, c)))
                cps[-1].start()
        for cp in cps:
            cp.wait_recv()
        for cp in cps:
            cp.wait_send()

    return _pallas(body, name=name, in_specs=[ANY] * nt, out_specs=[ANY] * nt,
                   out_shape=[_sds((3,) + s.shape[1:], s.dtype) for s in sums],
                   scratch_shapes=[pltpu.SemaphoreType.DMA((3 * nt,)), pltpu.SemaphoreType.DMA((3 * nt,))],
                   compiler_params=pltpu.CompilerParams(has_side_effects=True))(*sums)


def pair_share(halves, name):
    nt = len(halves)

    def body(*refs):
        ins, outs = refs[:nt], refs[nt:2 * nt]
        send_sems, recv_sems = refs[2 * nt:]
        x, y, c, _ = _place()
        cps = []
        for t in range(nt):
            cps.append(_rcopy(ins[t], outs[t], send_sems, recv_sems, t, (x, y, 1 - c)))
            cps[-1].start()
        for cp in cps:
            cp.wait_recv()
        for cp in cps:
            cp.wait_send()

    return _pallas(body, name=name, in_specs=[ANY] * nt, out_specs=[ANY] * nt,
                   out_shape=[_sds(h.shape, h.dtype) for h in halves],
                   scratch_shapes=[pltpu.SemaphoreType.DMA((nt,)), pltpu.SemaphoreType.DMA((nt,))],
                   compiler_params=pltpu.CompilerParams(has_side_effects=True))(*halves)


def small_allreduce(pack, name):
    r = pack.shape[0]

    def body(in_ref, out_ref, buf, send_sems, recv_sems):
        x, y, c, _ = _place()
        me = 4 * x + 2 * y + c
        sends = []
        for k in range(1, 8):
            to = ((x + ((k >> 2) & 1)) % 2, (y + ((k >> 1) & 1)) % 2, (c + (k & 1)) % 2)
            cp = _rcopy(in_ref, buf.at[me], send_sems, recv_sems, k - 1, to)
            cp.start()
            sends.append((cp, to))
        buf[pl.ds(me, 1)] = in_ref[...][None]
        for k, (_, to) in enumerate(sends):
            peer = 4 * to[0] + 2 * to[1] + to[2]
            _rcopy(in_ref, buf.at[peer], send_sems, recv_sems, k, to).wait_recv()
        for cp, _ in sends:
            cp.wait_send()
        acc = buf[0]
        for d in range(1, 8):
            acc = acc + buf[d]
        out_ref[...] = acc

    vm = pl.BlockSpec(memory_space=pltpu.VMEM)
    return _pallas(body, name=name, in_specs=[vm], out_specs=vm, out_shape=_sds((r, HEAD), F32),
                   scratch_shapes=[pltpu.VMEM((8, r, HEAD), F32), pltpu.SemaphoreType.DMA((7,)),
                                   pltpu.SemaphoreType.DMA((7,))],
                   compiler_params=pltpu.CompilerParams(has_side_effects=True))(pack)


def add_halves(c_idx, grad, other, name):
    _, k, ns = grad.shape
    kh = k // 2
    tr = _tile(kh, max(16, (1 << 19) // ns), 16)
    nr = kh // tr

    def body(c_ref, g_ref, o_ref, s_ref):
        s_ref[...] = (g_ref[...] + o_ref[...]).astype(s_ref.dtype)

    gs = pltpu.PrefetchScalarGridSpec(
        num_scalar_prefetch=1, grid=(N_CHIPS, nr),
        in_specs=[pl.BlockSpec((None, tr, ns), lambda g, i, c_ref: (g, c_ref[0] * nr + i, 0)),
                  pl.BlockSpec((None, tr, ns), lambda g, i, c_ref: (g, i, 0))],
        out_specs=pl.BlockSpec((None, tr, ns), lambda g, i, c_ref: (g, i, 0)))
    return _pallas(body, name=name, grid_spec=gs, out_shape=_sds((N_CHIPS, kh, ns), XDT),
                   compiler_params=_params(("parallel", "parallel")))(c_idx, grad, other)


def add_chips(chip_idx, sums, recv, stack, layer, n_layers, name):
    _, kh, ns = sums.shape
    tr = _tile(kh, max(16, (1 << 19) // ns), 16)
    has_stack = stack is not None

    def body(k_ref, s_ref, r0, r1, r2, *rest):
        o_ref = rest[-1]
        o_ref[...] = ((s_ref[...].astype(F32) + r0[...].astype(F32)) + r1[...].astype(F32)) + r2[...].astype(F32)

    rspec = [pl.BlockSpec((None, tr, ns), functools.partial(lambda i, k_ref, j: (j, i, 0), j=j)) for j in range(3)]
    gs = pltpu.PrefetchScalarGridSpec(
        num_scalar_prefetch=1, grid=(kh // tr,),
        in_specs=[pl.BlockSpec((None, tr, ns), lambda i, k_ref: (k_ref[0], i, 0))] + rspec + ([ANY] if has_stack else []),
        out_specs=pl.BlockSpec((None, tr, ns), lambda i, k_ref: (layer, i, 0)))
    args = (chip_idx, sums, recv, recv, recv) + ((stack,) if has_stack else ())
    return _pallas(body, name=name, grid_spec=gs, out_shape=_sds((n_layers, kh, ns), F32),
                   input_output_aliases={5: 0} if has_stack else {},
                   compiler_params=_params(("parallel",)))(*args)


def adamw_big(c_idx, w, m, v, mine, other, name):
    nl, k, ns = w.shape
    kh = k // 2
    tr = _tile(kh, max(8, (1 << 18) // ns), 8)
    nr = kh // tr
    c1 = 1.0 - ADAM_B1 ** ADAM_STEP
    c2 = 1.0 - ADAM_B2 ** ADAM_STEP

    def body(c_ref, w_ref, m_ref, v_ref, a_ref, b_ref, g_ref, d_ref, mo_ref, vo_ref):
        gv = jnp.where(pl.program_id(2) == c_ref[0], a_ref[...], b_ref[...])
        mn = ADAM_B1 * m_ref[...] + (1.0 - ADAM_B1) * gv
        vn = ADAM_B2 * v_ref[...] + (1.0 - ADAM_B2) * (gv * gv)
        g_ref[...] = gv
        d_ref[...] = -ADAM_LR * ((mn / c1) / (jnp.sqrt(vn / c2) + ADAM_EPS) + ADAM_WD * w_ref[...])
        mo_ref[...] = mn
        vo_ref[...] = vn

    full = pl.BlockSpec((None, tr, ns), lambda l, i, hh, c_ref: (l, hh * nr + i, 0))
    half = pl.BlockSpec((None, tr, ns), lambda l, i, hh, c_ref: (l, i, 0))
    gs = pltpu.PrefetchScalarGridSpec(num_scalar_prefetch=1, grid=(nl, nr, 2),
                                      in_specs=[full, full, full, half, half], out_specs=[full] * 4)
    return _pallas(body, name=name, grid_spec=gs, out_shape=[_sds(w.shape, F32)] * 4,
                   compiler_params=_params(("parallel", "parallel", "arbitrary")))(c_idx, w, m, v, mine, other)


W_NAMES = ("w_in", "w_br_a", "w_br_b", "w_br_c", "w_o", "w_gate_up", "w_down")


def _rope_tables(n):
    half = HEAD // 2
    inv_freq = ROPE_THETA ** (-jnp.arange(half, dtype=F32) * 2.0 / HEAD)
    ang = jnp.arange(n, dtype=F32)[:, None] * inv_freq[None, :]
    cos, sin = jnp.cos(ang), jnp.sin(ang)
    return jnp.concatenate([cos, cos], axis=-1), jnp.concatenate([-sin, sin], axis=-1)


def _rpb_windows(rpb):
    pad = jnp.pad(rpb, ((0, 0), (0, 1), (0, GRID_W - rpb.shape[2])))
    wins = [pad[:, i0:i0 + C_WIN_ROWS].reshape(C_HEADS, 1, C_WIN_ROWS * GRID_W) for i0 in range(C_WIN_ROWS)]
    return jnp.stack(wins, axis=1)


def _b_view(t, dil):
    n, wd = t.shape
    return t.reshape(n // dil, dil * wd)


def layer_fwd(x, p, w, cos2, sin2):
    n, d = x.shape
    s = {"x": x}
    s["h"] = rmsnorm_fwd(x, p["norm1_g"], "norm1")
    s["proj"] = mm_x_wcol(s["h"], w["w_in"], F32, "proj")
    gains = jnp.pad(p["qk_norm_g"], ((0, 2), (0, 0)))
    pp = s["pp"] = qk_prep(s["proj"], gains, cos2, sin2, "qk_prep")
    sink = p["sink_a"].reshape(1, A_Q_HEADS)
    s["oa"], s["lse_a"] = band_attn_fwd(pp["qa"], pp["ka"], pp["va"], sink, seqs=A_KV_HEADS, G=A_GROUP,
                                        nh=A_KV_HEADS, radius=A_RADIUS, name="attn_a")
    s["ob"], s["lse_b"] = [], []
    for g, dil in enumerate(B_DILS):
        o, lse = band_attn_fwd(_b_view(pp[f"qb{g}"], dil), _b_view(pp[f"kb{g}"], dil), _b_view(pp[f"vb{g}"], dil),
                               None, seqs=dil * B_HG, G=1, nh=B_HG, radius=B_RADIUS, name=f"attn_b{g}")
        s["ob"].append(o.reshape(n, B_HG * HEAD))
        s["lse_b"].append(lse.reshape(n, B_HG * HEAD))
    ob = b_combine_fwd(s["ob"], s["lse_b"], "b_combine")
    s["bias"] = rpb_expand(_rpb_windows(p["rpb_c"]), "rpb_expand")
    s["oc"], s["lse_c"] = c_attn_fwd(pp["qc"], pp["kc"], pp["vc"], s["bias"], "attn_c")
    s["o_in"] = (s["oa"], ob, s["oc"])
    s["ys"] = [mm_x_wcol(o, w[k], F32, "branch_" + k[-1]) for o, k in zip(s["o_in"], ("w_br_a", "w_br_b", "w_br_c"))]
    s["merged"] = gate_merge(s["proj"], s["ys"], "gate_merge")
    s["x_mid"] = mm_x_w(s["merged"], w["w_o"], "out_proj", res=x)
    s["h2"] = rmsnorm_fwd(s["x_mid"], p["norm2_g"], "norm2")
    s["gu"] = mm_x_wcol(s["h2"], w["w_gate_up"], F32, "gate_up", tn_pref=1408, stacked_out=2)
    s["act"] = swiglu_fwd(s["gu"], "swiglu")
    x_out = mm_x_w(s["act"], w["w_down"], "down", res=s["x_mid"])
    return x_out, s


def layer_bwd(dx_out, s, p, w, cos2, sin2):
    n, d = dx_out.shape
    pp = s["pp"]
    dact = mm_x_wT(dx_out, w["w_down"], "d_act")
    dw_down = mm_aT_d(s["act"], dx_out, "dw_down")
    dgu = swiglu_bwd(s["gu"], dact, "swiglu_bwd")
    dh2 = mm_x_wcolT(dgu, w["w_gate_up"], "d_h2", tm_pref=512, tk_pref=1408, stacked_in=2)
    dw_gu = mm_aT_d_wcol(s["h2"], dgu, "dw_gate_up", tn_pref=1408, stacked_in=2)
    dx_mid, dg2 = rmsnorm_bwd(s["x_mid"], p["norm2_g"], dh2, dx_out, "norm2_bwd")

    dmerged = mm_x_wT(dx_mid, w["w_o"], "d_merged")
    dw_o = mm_aT_d(s["merged"], dx_mid, "dw_o")
    dys, dgls = gate_bwd(s["proj"], s["ys"], dmerged, "gate_bwd")
    dos, dw_br = [], []
    for o, dy, k in zip(s["o_in"], dys, ("w_br_a", "w_br_b", "w_br_c")):
        dos.append(mm_x_wcolT(dy, w[k], "d_o_" + k[-1], tm_pref=512))
        dw_br.append(mm_aT_d_wcol(o, dy, "dw_br_" + k[-1]))

    grads = {}
    sink = p["sink_a"].reshape(1, A_Q_HEADS)
    grads["qa"], grads["ka"], grads["va"], dsink = band_attn_bwd(
        pp["qa"], pp["ka"], pp["va"], sink, s["oa"], s["lse_a"], dos[0], None,
        seqs=A_KV_HEADS, G=A_GROUP, nh=A_KV_HEADS, radius=A_RADIUS, name="attn_a_bwd")
    dobs, dlses = b_combine_bwd(dos[1], s["ob"], s["lse_b"], "b_combine_bwd")
    for g, dil in enumerate(B_DILS):
        dq, dk, dv = band_attn_bwd(_b_view(pp[f"qb{g}"], dil), _b_view(pp[f"kb{g}"], dil), _b_view(pp[f"vb{g}"], dil),
                                   None, _b_view(s["ob"][g], dil), _b_view(s["lse_b"][g], dil), _b_view(dobs[g], dil),
                                   _b_view(dlses[g], dil), seqs=dil * B_HG, G=1, nh=2, radius=B_RADIUS,
                                   name=f"attn_b{g}_bwd")
        grads[f"qb{g}"], grads[f"kb{g}"], grads[f"vb{g}"] = [t.reshape(n, B_HG * HEAD) for t in (dq, dk, dv)]
    grads["qc"], grads["kc"], grads["vc"], dbias = c_attn_bwd(pp["qc"], pp["kc"], pp["vc"], s["bias"], s["oc"],
                                                              s["lse_c"], dos[2], "attn_c_bwd")
    drpb = rpb_reduce(dbias, "rpb_reduce")[:, :2 * C_WIN_ROWS - 1, :2 * C_WIN_COLS - 1]
    gains = jnp.pad(p["qk_norm_g"], ((0, 2), (0, 0)))
    dqkv, dgains = qk_prep_bwd(s["proj"], gains, cos2, sin2, grads, "qk_prep_bwd")
    dproj = jnp.concatenate([dqkv] + list(dgls), axis=1)
    dh = mm_x_wcolT(dproj, w["w_in"], "d_h", tm_pref=512)
    dw_in = mm_aT_d_wcol(s["h"], dproj, "dw_in")
    dx_in, dg1 = rmsnorm_bwd(s["x"], p["norm1_g"], dh, dx_mid, "norm1_bwd")

    dws = [dw_in] + dw_br + [dw_o.reshape(N_CHIPS, d // N_CHIPS, d), dw_gu,
                             dw_down.reshape(N_CHIPS, dw_down.shape[0] // N_CHIPS, d)]
    small = {"norm1_g": dg1[0], "qk_norm_g": dgains[:6], "sink_a": dsink[0, :, 0],
             "rpb_c": drpb, "norm2_g": dg2[0]}
    return dx_in, dws, small


SMALL_NAMES = ("norm1_g", "qk_norm_g", "sink_a", "rpb_c", "norm2_g")


def _pack_small(parts, extra=None):
    flat = [parts[k].reshape(-1) for k in SMALL_NAMES]
    flat.append(jnp.zeros((1,), F32) if extra is None else extra.reshape(1))
    v = jnp.concatenate(flat)
    rows = -(-v.shape[0] // (8 * HEAD)) * 8
    return jnp.pad(v, (0, rows * HEAD - v.shape[0])).reshape(rows, HEAD)


def _unpack_small(pack, like):
    v = pack.reshape(-1)
    out, off = {}, 0
    for k in SMALL_NAMES:
        size = math.prod(like[k].shape)
        out[k] = v[off:off + size].reshape(like[k].shape)
        off += size
    return out, v[off]


def kernel(x, norm1_g, w_in, qk_norm_g, sink_a, rpb_c, w_br_a, w_br_b, w_br_c, w_o, norm2_g, w_gate_up, w_down, loss_target, m_norm1_g, m_w_in, m_qk_norm_g, m_sink_a, m_rpb_c, m_w_br_a, m_w_br_b, m_w_br_c, m_w_o, m_norm2_g, m_w_gate_up, m_w_down, v_norm1_g, v_w_in, v_qk_norm_g, v_sink_a, v_rpb_c, v_w_br_a, v_w_br_b, v_w_br_c, v_w_o, v_norm2_g, v_w_gate_up, v_w_down):
    big = dict(w_in=w_in, w_br_a=w_br_a, w_br_b=w_br_b, w_br_c=w_br_c, w_o=w_o, w_gate_up=w_gate_up, w_down=w_down)
    big_m = dict(w_in=m_w_in, w_br_a=m_w_br_a, w_br_b=m_w_br_b, w_br_c=m_w_br_c, w_o=m_w_o, w_gate_up=m_w_gate_up, w_down=m_w_down)
    big_v = dict(w_in=v_w_in, w_br_a=v_w_br_a, w_br_b=v_w_br_b, w_br_c=v_w_br_c, w_o=v_w_o, w_gate_up=v_w_gate_up, w_down=v_w_down)
    small = dict(norm1_g=norm1_g, qk_norm_g=qk_norm_g, sink_a=sink_a, rpb_c=rpb_c, norm2_g=norm2_g)
    small_m = dict(norm1_g=m_norm1_g, qk_norm_g=m_qk_norm_g, sink_a=m_sink_a, rpb_c=m_rpb_c, norm2_g=m_norm2_g)
    small_v = dict(norm1_g=v_norm1_g, qk_norm_g=v_qk_norm_g, sink_a=v_sink_a, rpb_c=v_rpb_c, norm2_g=v_norm2_g)
    n_layers = w_in.shape[0]
    n, d = x.shape[1], x.shape[2]
    c_idx = lax.axis_index("c").astype(jnp.int32).reshape(1)
    chip_idx = (2 * lax.axis_index("x") + lax.axis_index("y")).astype(jnp.int32).reshape(1)
    cos2, sin2 = _rope_tables(n)

    weights = []
    for l in range(n_layers):
        got = gather_weights([cast_place(chip_idx, big[k][l], "cast_" + k) for k in W_NAMES], "gather_weights")
        w = dict(zip(W_NAMES, got))
        w["w_o"] = w["w_o"].reshape(d, d)
        w["w_down"] = w["w_down"].reshape(-1, d)
        weights.append(w)

    xs, saved = x[0], []
    for l in range(n_layers):
        p = {k: small[k][l] for k in SMALL_NAMES}
        xs, s = layer_fwd(xs, p, weights[l], cos2, sin2)
        saved.append(s)
    loss_tile, dx = loss_head(xs, loss_target[0], "loss_head")

    halves = [None] * len(W_NAMES)
    small_g = [None] * n_layers
    for l in reversed(range(n_layers)):
        p = {k: small[k][l] for k in SMALL_NAMES}
        dx, dws, small_g[l] = layer_bwd(dx, saved[l], p, weights[l], cos2, sin2)
        from_sibling = pair_exchange(dws, "pair_exchange")
        sums = [add_halves(c_idx, g, o, "add_halves_" + k) for g, o, k in zip(dws, from_sibling, W_NAMES)]
        from_chips = chip_exchange(sums, "chip_exchange")
        halves = [add_chips(chip_idx, sm, r, st, l, n_layers, "add_chips_" + k)
                  for sm, r, st, k in zip(sums, from_chips, halves, W_NAMES)]
    others = pair_share(halves, "pair_share")

    mine = {k: jnp.stack([small_g[l][k] for l in range(n_layers)]) for k in SMALL_NAMES}
    total = small_allreduce(_pack_small(mine, loss_tile[0, 0]), "small_allreduce")
    grad_small, loss = _unpack_small(total, small)

    outs = {}
    for k, mine_half, other_half in zip(W_NAMES, halves, others):
        outs[k] = adamw_big(c_idx, big[k], big_m[k], big_v[k], mine_half, other_half, "adamw_" + k)
    res = adamw(_pack_small(small), _pack_small(grad_small), _pack_small(small_m), _pack_small(small_v), "adamw_small")
    unp = [_unpack_small(t, small)[0] for t in res]
    for k in SMALL_NAMES:
        outs[k] = (grad_small[k],) + tuple(u[k] for u in unp)

    order = ("norm1_g", "w_in", "qk_norm_g", "sink_a", "rpb_c", "w_br_a", "w_br_b", "w_br_c", "w_o", "norm2_g",
             "w_gate_up", "w_down")
    return (loss, dx[None]) + tuple(outs[k][i] for i in range(4) for k in order)
```

```python
import functools
import math

import jax
import jax.numpy as jnp
from jax import lax
from jax.experimental import pallas as pl
from jax.experimental.pallas import tpu as pltpu

F32 = jnp.float32
CDT = jnp.bfloat16
XDT = jnp.bfloat16

HEAD = 128
NORM_EPS = 1e-6
ROPE_THETA = 10000.0
A_Q_HEADS, A_KV_HEADS, A_GROUP, A_RADIUS = 8, 2, 4, 128
B_DILS = (1, 4, 16)
B_RADIUS = 64
B_HG = 4
C_HEADS, GRID_W, C_WIN_ROWS, C_WIN_COLS = 8, 64, 8, 16
QKV_W = 9216
COL = dict(qa=0, ka=1024, va=1280, qb=1536, kb=3072, vb=4608, qc=6144, kc=7168, vc=8192)
NEG = -1e30
SCALE = HEAD ** -0.5
N_CHIPS = 4

ADAM_LR, ADAM_B1, ADAM_B2, ADAM_EPS, ADAM_WD, ADAM_STEP = 0.001, 0.9, 0.999, 1e-08, 0.01, 10

VMEM_LIMIT = 56 * 1024 * 1024
MESH = pl.DeviceIdType.MESH


def _pallas(body, **kw):
    return pl.pallas_call(body, **kw)


def _params(sem=None, **kw):
    if sem is not None:
        kw["dimension_semantics"] = sem
    return pltpu.CompilerParams(vmem_limit_bytes=VMEM_LIMIT, **kw)


def _tile(dim, pref, mult=128):
    best = None
    for t in range(mult, min(dim, pref) + 1, mult):
        if dim % t == 0:
            best = t
    return dim if best is None else best


def _sds(shape, dtype):
    return jax.ShapeDtypeStruct(tuple(shape), dtype)


_DIMS = {"nn": (((1,), (0,)), ((), ())), "nt": (((1,), (1,)), ((), ())), "tn": (((0,), (0,)), ((), ()))}


def _matmul(a, b, *, mode, grid, a_spec, b_spec, o_spec, out_shape, acc_shape, name, res=None, res_spec=None):
    nk = grid[2]
    has_res = res is not None

    def body(*refs):
        if has_res:
            a_ref, b_ref, r_ref, o_ref = refs[:4]
            rest = refs[4:]
        else:
            a_ref, b_ref, o_ref = refs[:3]
            r_ref = None
            rest = refs[3:]
        p = lax.dot_general(a_ref[...].astype(CDT), b_ref[...].astype(CDT), _DIMS[mode],
                            preferred_element_type=F32)

        def finish(acc):
            if has_res:
                acc = acc + r_ref[...].astype(F32)
            o_ref[...] = acc.astype(o_ref.dtype)

        if nk == 1:
            finish(p)
        else:
            acc_ref = rest[0]
            k = pl.program_id(2)

            @pl.when(k == 0)
            def _():
                acc_ref[...] = p

            @pl.when(k > 0)
            def _():
                acc_ref[...] += p

            @pl.when(k == nk - 1)
            def _():
                finish(acc_ref[...])

    in_specs = [a_spec, b_spec] + ([res_spec] if has_res else [])
    args = (a, b) + ((res,) if has_res else ())
    scratch = [] if nk == 1 else [pltpu.VMEM(acc_shape, F32)]
    return _pallas(body, name=name, grid=grid, in_specs=in_specs, out_specs=o_spec, out_shape=out_shape,
                   scratch_shapes=scratch, compiler_params=_params(("parallel", "parallel", "arbitrary")))(*args)


def mm_x_wcol(a, wg, out_dtype, name, tm_pref=1024, tn_pref=1024, stacked_out=1):
    m, k = a.shape
    ns = wg.shape[2]
    tm, tn = _tile(m, tm_pref, 8), _tile(ns, tn_pref)
    nj = ns // tn
    grid = (m // tm, N_CHIPS * nj, 1)
    a_spec = pl.BlockSpec((tm, k), lambda i, j, kk: (i, 0))
    b_spec = pl.BlockSpec((None, k, tn), lambda i, j, kk: (j // nj, 0, j % nj))
    if stacked_out == 1:
        o_spec = pl.BlockSpec((tm, tn), lambda i, j, kk: (i, j))
        out_shape = _sds((m, N_CHIPS * ns), out_dtype)
    else:
        per = N_CHIPS * nj // stacked_out
        o_spec = pl.BlockSpec((None, tm, tn), lambda i, j, kk: (j // per, i, j % per))
        out_shape = _sds((stacked_out, m, N_CHIPS * ns // stacked_out), out_dtype)
    return _matmul(a, wg, mode="nn", grid=grid, a_spec=a_spec, b_spec=b_spec, o_spec=o_spec,
                   out_shape=out_shape, acc_shape=(tm, tn), name=name)


def mm_x_wcolT(d, wg, name, res=None, tm_pref=1024, tk_pref=1024, stacked_in=1):
    kdim, ns = wg.shape[1], wg.shape[2]
    m = d.shape[-2]
    tm, tk = _tile(m, tm_pref, 8), _tile(ns, tk_pref)
    nkk = ns // tk
    grid = (m // tm, 1, N_CHIPS * nkk)
    if stacked_in == 1:
        a_spec = pl.BlockSpec((tm, tk), lambda i, j, kk: (i, kk))
    else:
        per = N_CHIPS * nkk // stacked_in
        a_spec = pl.BlockSpec((None, tm, tk), lambda i, j, kk: (kk // per, i, kk % per))
    b_spec = pl.BlockSpec((None, kdim, tk), lambda i, j, kk: (kk // nkk, 0, kk % nkk))
    o_spec = pl.BlockSpec((tm, kdim), lambda i, j, kk: (i, 0))
    return _matmul(d, wg, mode="nt", grid=grid, a_spec=a_spec, b_spec=b_spec, o_spec=o_spec,
                   out_shape=_sds((m, kdim), F32), acc_shape=(tm, kdim), name=name,
                   res=res, res_spec=None if res is None else o_spec)


def mm_aT_d_wcol(a, d, name, tm_pref=1024, tk_pref=1024, tn_pref=1024, stacked_in=1):
    m, kdim = a.shape
    ntot = d.shape[-1] * stacked_in
    ns = ntot // N_CHIPS
    tm, tkm, tn = _tile(kdim, tm_pref), _tile(m, tk_pref, 8), _tile(ns, tn_pref)
    nj = ns // tn
    grid = (kdim // tm, N_CHIPS * nj, m // tkm)
    a_spec = pl.BlockSpec((tkm, tm), lambda i, j, kk: (kk, i))
    if stacked_in == 1:
        b_spec = pl.BlockSpec((tkm, tn), lambda i, j, kk: (kk, j))
    else:
        per = N_CHIPS * nj // stacked_in
        b_spec = pl.BlockSpec((None, tkm, tn), lambda i, j, kk: (j // per, kk, j % per))
    o_spec = pl.BlockSpec((None, tm, tn), lambda i, j, kk: (j // nj, i, j % nj))
    return _matmul(a, d, mode="tn", grid=grid, a_spec=a_spec, b_spec=b_spec, o_spec=o_spec,
                   out_shape=_sds((N_CHIPS, kdim, ns), F32), acc_shape=(tm, tn), name=name)


def mm_x_w(a, w, name, res=None, out_dtype=F32, tm_pref=1024, tn_pref=1024, tk_pref=2048):
    m, k = a.shape
    n = w.shape[1]
    tm, tn, tk = _tile(m, tm_pref, 8), _tile(n, tn_pref), _tile(k, tk_pref)
    grid = (m // tm, n // tn, k // tk)
    o_spec = pl.BlockSpec((tm, tn), lambda i, j, kk: (i, j))
    return _matmul(a, w, mode="nn", grid=grid,
                   a_spec=pl.BlockSpec((tm, tk), lambda i, j, kk: (i, kk)),
                   b_spec=pl.BlockSpec((tk, tn), lambda i, j, kk: (kk, j)),
                   o_spec=o_spec, out_shape=_sds((m, n), out_dtype), acc_shape=(tm, tn), name=name,
                   res=res, res_spec=None if res is None else o_spec)


def mm_x_wT(d, w, name, out_dtype=F32, tm_pref=1024, tn_pref=1024):
    m, n = d.shape
    k = w.shape[0]
    tm, tn = _tile(m, tm_pref, 8), _tile(k, tn_pref)
    grid = (m // tm, k // tn, 1)
    return _matmul(d, w, mode="nt", grid=grid,
                   a_spec=pl.BlockSpec((tm, n), lambda i, j, kk: (i, 0)),
                   b_spec=pl.BlockSpec((tn, n), lambda i, j, kk: (j, 0)),
                   o_spec=pl.BlockSpec((tm, tn), lambda i, j, kk: (i, j)),
                   out_shape=_sds((m, k), out_dtype), acc_shape=(tm, tn), name=name)


def mm_aT_d(a, d, name, tm_pref=1024, tn_pref=1024, tk_pref=1024):
    m, k = a.shape
    n = d.shape[1]
    tm, tn, tk = _tile(k, tm_pref), _tile(n, tn_pref), _tile(m, tk_pref, 8)
    grid = (k // tm, n // tn, m // tk)
    return _matmul(a, d, mode="tn", grid=grid,
                   a_spec=pl.BlockSpec((tk, tm), lambda i, j, kk: (kk, i)),
                   b_spec=pl.BlockSpec((tk, tn), lambda i, j, kk: (kk, j)),
                   o_spec=pl.BlockSpec((tm, tn), lambda i, j, kk: (i, j)),
                   out_shape=_sds((k, n), F32), acc_shape=(tm, tn), name=name)


def rmsnorm_fwd(x, g, name):
    n, d = x.shape
    tm = _tile(n, 512, 8)

    def body(x_ref, g_ref, h_ref):
        xv = x_ref[...]
        r = lax.rsqrt(jnp.mean(xv * xv, axis=-1, keepdims=True) + NORM_EPS)
        h_ref[...] = (xv * r * g_ref[...]).astype(h_ref.dtype)

    return _pallas(body, name=name, grid=(n // tm,),
                   in_specs=[pl.BlockSpec((tm, d), lambda i: (i, 0)), pl.BlockSpec((1, d), lambda i: (0, 0))],
                   out_specs=pl.BlockSpec((tm, d), lambda i: (i, 0)), out_shape=_sds((n, d), CDT),
                   compiler_params=_params(("parallel",)))(x, g.reshape(1, d))


def rmsnorm_bwd(x, g, dh, dres, name):
    n, d = x.shape
    tm = _tile(n, 256, 8)

    def body(x_ref, g_ref, dh_ref, dres_ref, dx_ref, dg_ref):
        xv = x_ref[...]
        r = lax.rsqrt(jnp.mean(xv * xv, axis=-1, keepdims=True) + NORM_EPS)
        dhv = dh_ref[...]
        u = dhv * g_ref[...]
        c = jnp.mean(xv * u, axis=-1, keepdims=True)
        dx_ref[...] = dres_ref[...] + r * u - xv * (r * r * r * c)
        part = jnp.broadcast_to(jnp.sum(dhv * xv * r, axis=0, keepdims=True), (8, d))

        @pl.when(pl.program_id(0) == 0)
        def _():
            dg_ref[...] = part

        @pl.when(pl.program_id(0) > 0)
        def _():
            dg_ref[...] += part

    row = pl.BlockSpec((tm, d), lambda i: (i, 0))
    dx, dg = _pallas(body, name=name, grid=(n // tm,),
                     in_specs=[row, pl.BlockSpec((1, d), lambda i: (0, 0)), row, row],
                     out_specs=[row, pl.BlockSpec((8, d), lambda i: (0, 0))],
                     out_shape=[_sds((n, d), F32), _sds((8, d), F32)],
                     compiler_params=_params(("arbitrary",)))(x, g.reshape(1, d), dh, dres)
    return dx, dg


def _norm_rope(xh, g, cos2, sin2):
    r = lax.rsqrt(jnp.mean(xh * xh, axis=-1, keepdims=True) + NORM_EPS)
    y = xh * r * g
    if cos2 is not None:
        y = y * cos2 + pltpu.roll(y, HEAD // 2, 1) * sin2
    return y


def _norm_rope_bwd(xh, g, cos2, sin2, dout):
    if cos2 is not None:
        dy = dout * cos2 + pltpu.roll(dout * sin2, HEAD // 2, 1)
    else:
        dy = dout
    r = lax.rsqrt(jnp.mean(xh * xh, axis=-1, keepdims=True) + NORM_EPS)
    u = dy * g
    c = jnp.mean(xh * u, axis=-1, keepdims=True)
    return r * u - xh * (r * r * r * c), dy * xh * r


_QK_GROUPS = (("qa", COL["qa"], 8, 0, True), ("ka", COL["ka"], 2, 1, True),
              ("qb", COL["qb"], 12, 2, True), ("kb", COL["kb"], 12, 3, True),
              ("qc", COL["qc"], 8, 4, False), ("kc", COL["kc"], 8, 5, False))
_V_GROUPS = (("va", COL["va"], 2), ("vb", COL["vb"], 12), ("vc", COL["vc"], 8))
_PREP_OUT = (("qa", 8), ("ka", 2), ("va", 2)) + tuple((f"{t}b{g}", 4) for t in "qkv" for g in range(3)) + (
    ("qc", 8), ("kc", 8), ("vc", 8))


def _prep_src(name):
    if name[1] == "b":
        base = COL[name[0] + "b"] + int(name[2]) * B_HG * HEAD
        gain = {"q": 2, "k": 3, "v": None}[name[0]]
        return base, gain, name[0] != "v"
    base = COL[name]
    gain = {"qa": 0, "ka": 1, "va": None, "qc": 4, "kc": 5, "vc": None}[name]
    return base, gain, name in ("qa", "ka")


def qk_prep(proj, gains, cos2, sin2, name):
    n = proj.shape[0]
    tm = _tile(n, 256, 8)

    def body(p_ref, g_ref, c_ref, s_ref, *outs):
        cos2v, sin2v = c_ref[...], s_ref[...]
        for (nm, heads), o_ref in zip(_PREP_OUT, outs):
            base, gain, rope = _prep_src(nm)
            for h in range(heads):
                xh = p_ref[:, base + h * HEAD: base + (h + 1) * HEAD]
                if gain is None:
                    y = xh
                else:
                    y = _norm_rope(xh, g_ref[gain:gain + 1, :], cos2v if rope else None, sin2v if rope else None)
                o_ref[:, h * HEAD:(h + 1) * HEAD] = y.astype(o_ref.dtype)

    tab = pl.BlockSpec((tm, HEAD), lambda i: (i, 0))
    outs = _pallas(body, name=name, grid=(n // tm,),
                   in_specs=[pl.BlockSpec((tm, QKV_W), lambda i: (i, 0)), pl.BlockSpec((8, HEAD), lambda i: (0, 0)), tab, tab],
                   out_specs=[pl.BlockSpec((tm, h * HEAD), lambda i: (i, 0)) for _, h in _PREP_OUT],
                   out_shape=[_sds((n, h * HEAD), CDT) for _, h in _PREP_OUT],
                   compiler_params=_params(("parallel",)))(proj, gains, cos2, sin2)
    return dict(zip([nm for nm, _ in _PREP_OUT], outs))


def qk_prep_bwd(proj, gains, cos2, sin2, grads, name):
    n = proj.shape[0]
    tm = _tile(n, 128, 8)
    names = [nm for nm, _ in _PREP_OUT]

    def body(p_ref, g_ref, c_ref, s_ref, *refs):
        g_refs, dp_ref, dg_ref = refs[:len(names)], refs[len(names)], refs[len(names) + 1]
        cos2v, sin2v = c_ref[...], s_ref[...]
        dg = [jnp.zeros((tm, HEAD), F32) for _ in range(6)]
        for (nm, heads), gr in zip(_PREP_OUT, g_refs):
            base, gain, rope = _prep_src(nm)
            for h in range(heads):
                sl = slice(base + h * HEAD, base + (h + 1) * HEAD)
                dout = gr[:, h * HEAD:(h + 1) * HEAD]
                if gain is None:
                    dx = dout
                else:
                    dx, dgr = _norm_rope_bwd(p_ref[:, sl], g_ref[gain:gain + 1, :], cos2v if rope else None,
                                             sin2v if rope else None, dout)
                    dg[gain] = dg[gain] + dgr
                dp_ref[:, sl] = dx.astype(dp_ref.dtype)
        part = jnp.concatenate([jnp.sum(t, axis=0, keepdims=True) for t in dg] + [jnp.zeros((2, HEAD), F32)], axis=0)

        @pl.when(pl.program_id(0) == 0)
        def _():
            dg_ref[...] = part

        @pl.when(pl.program_id(0) > 0)
        def _():
            dg_ref[...] += part

    tab = pl.BlockSpec((tm, HEAD), lambda i: (i, 0))
    dp, dg = _pallas(body, name=name, grid=(n // tm,),
                     in_specs=[pl.BlockSpec((tm, QKV_W), lambda i: (i, 0)), pl.BlockSpec((8, HEAD), lambda i: (0, 0)), tab, tab]
                     + [pl.BlockSpec((tm, h * HEAD), lambda i: (i, 0)) for _, h in _PREP_OUT],
                     out_specs=[pl.BlockSpec((tm, QKV_W), lambda i: (i, 0)), pl.BlockSpec((8, HEAD), lambda i: (0, 0))],
                     out_shape=[_sds((n, QKV_W), CDT), _sds((8, HEAD), F32)],
                     compiler_params=_params(("arbitrary",)))(proj, gains, cos2, sin2, *[grads[k] for k in names])
    return dp, dg


def _band_geometry(m, bq_pref, radius):
    bq = min(bq_pref, m)
    return bq, min(bq + 2 * radius, m)


def _band_window(i, bq, radius, m, w):
    start = pl.multiple_of(jnp.clip(i * bq - radius, 0, m - w), 64)
    qpos = i * bq + lax.broadcasted_iota(jnp.int32, (bq, w), 0)
    kpos = start + lax.broadcasted_iota(jnp.int32, (bq, w), 1)
    return start, jnp.abs(kpos - qpos) <= radius


def band_attn_fwd(q, k, v, sink, *, seqs, G, nh, radius, name, bq_pref=128):
    m = q.shape[0]
    bq, w = _band_geometry(m, bq_pref, radius)
    has_sink = sink is not None

    def body(*refs):
        if has_sink:
            sink_ref, q_ref, k_ref, v_ref, o_ref, lse_ref = refs
        else:
            q_ref, k_ref, v_ref, o_ref, lse_ref = refs
        s_id, i = pl.program_id(0), pl.program_id(1)
        start, valid = _band_window(i, bq, radius, m, w)
        for h in range(nh):
            ks = slice(h * HEAD, (h + 1) * HEAD)
            k_t = k_ref[pl.ds(start, w), ks]
            v_t = v_ref[pl.ds(start, w), ks]
            for g in range(G):
                sl = slice((h * G + g) * HEAD, (h * G + g + 1) * HEAD)
                s = lax.dot_general(q_ref[:, sl], k_t, _DIMS["nt"], preferred_element_type=F32) * SCALE
                s = jnp.where(valid, s, NEG)
                mx = jnp.max(s, axis=-1, keepdims=True)
                if has_sink:
                    sk = sink_ref[0, (s_id * nh + h) * G + g]
                    mx = jnp.maximum(mx, sk)
                p = jnp.exp(s - mx)
                den = jnp.sum(p, axis=-1, keepdims=True)
                if has_sink:
                    den = den + jnp.exp(sk - mx)
                o_ref[:, sl] = jnp.dot((p / den).astype(CDT), v_t, preferred_element_type=F32)
                lse_ref[:, sl] = jnp.broadcast_to(mx + jnp.log(den), (bq, HEAD))

    qspec = pl.BlockSpec((bq, nh * G * HEAD), lambda s, i: (i, s))
    kspec = pl.BlockSpec((m, nh * HEAD), lambda s, i: (0, s))
    in_specs = ([pl.BlockSpec(memory_space=pltpu.SMEM)] if has_sink else []) + [qspec, kspec, kspec]
    args = ((sink,) if has_sink else ()) + (q, k, v)
    return _pallas(body, name=name, grid=(seqs // nh, m // bq), in_specs=in_specs, out_specs=[qspec, qspec],
                   out_shape=[_sds(q.shape, F32), _sds(q.shape, F32)],
                   compiler_params=_params(("parallel", "arbitrary")))(*args)


def band_attn_bwd(q, k, v, sink, o, lse, do, dlse, *, seqs, G, nh, radius, name, bq_pref=128):
    m = q.shape[0]
    bq, w = _band_geometry(m, bq_pref, radius)
    has_sink, has_dlse = sink is not None, dlse is not None
    assert nh * G <= 8

    def body(*refs):
        refs = list(refs)
        sink_ref = refs.pop(0) if has_sink else None
        q_ref, k_ref, v_ref, o_ref, lse_ref, do_ref = refs[:6]
        refs = refs[6:]
        dlse_ref = refs.pop(0) if has_dlse else None
        dq_ref, dk_ref, dv_ref = refs[:3]
        dsink_ref = refs[3] if has_sink else None
        s_id, i = pl.program_id(0), pl.program_id(1)

        @pl.when(i == 0)
        def _():
            dk_ref[...] = jnp.zeros_like(dk_ref)
            dv_ref[...] = jnp.zeros_like(dv_ref)
            if has_sink:
                dsink_ref[...] = jnp.zeros_like(dsink_ref)

        start, valid = _band_window(i, bq, radius, m, w)
        for h in range(nh):
            ks = slice(h * HEAD, (h + 1) * HEAD)
            k_t = k_ref[pl.ds(start, w), ks]
            v_t = v_ref[pl.ds(start, w), ks]
            dk_acc = jnp.zeros((w, HEAD), F32)
            dv_acc = jnp.zeros((w, HEAD), F32)
            for g in range(G):
                row = h * G + g
                sl = slice(row * HEAD, (row + 1) * HEAD)
                q_t = q_ref[:, sl]
                lse_t = lse_ref[:, sl][:, :1]
                s = lax.dot_general(q_t, k_t, _DIMS["nt"], preferred_element_type=F32) * SCALE
                p = jnp.exp(jnp.where(valid, s, NEG) - lse_t)
                do_t = do_ref[:, sl]
                delta = jnp.sum(do_t * o_ref[:, sl], axis=-1, keepdims=True)
                do_c = do_t.astype(CDT)
                dp = lax.dot_general(do_c, v_t, _DIMS["nt"], preferred_element_type=F32)
                dv_acc = dv_acc + lax.dot_general(p.astype(CDT), do_c, _DIMS["tn"], preferred_element_type=F32)
                t = dp - delta
                if has_dlse:
                    t = t + dlse_ref[:, sl][:, :1]
                ds = ((p * t) * SCALE).astype(CDT)
                dq_ref[:, sl] = jnp.dot(ds, k_t, preferred_element_type=F32)
                dk_acc = dk_acc + lax.dot_general(ds, q_t, _DIMS["tn"], preferred_element_type=F32)
                if has_sink:
                    sk = sink_ref[0, (s_id * nh + h) * G + g]
                    part = -jnp.sum(jnp.exp(sk - lse_t) * delta, axis=0, keepdims=True)
                    dsink_ref[row:row + 1, :] += jnp.broadcast_to(part, (1, HEAD))
            dk_ref[pl.ds(start, w), ks] += dk_acc
            dv_ref[pl.ds(start, w), ks] += dv_acc

    qspec = pl.BlockSpec((bq, nh * G * HEAD), lambda s, i: (i, s))
    kspec = pl.BlockSpec((m, nh * HEAD), lambda s, i: (0, s))
    in_specs = ([pl.BlockSpec(memory_space=pltpu.SMEM)] if has_sink else []) + [qspec, kspec, kspec, qspec, qspec, qspec]
    in_specs += [qspec] if has_dlse else []
    args = ((sink,) if has_sink else ()) + (q, k, v, o, lse, do) + ((dlse,) if has_dlse else ())
    out_specs = [qspec, kspec, kspec]
    out_shape = [_sds(q.shape, F32), _sds(k.shape, F32), _sds(k.shape, F32)]
    if has_sink:
        out_specs.append(pl.BlockSpec((None, 8, HEAD), lambda s, i: (s, 0, 0)))
        out_shape.append(_sds((seqs // nh, 8, HEAD), F32))
    return _pallas(body, name=name, grid=(seqs // nh, m // bq), in_specs=in_specs, out_specs=out_specs,
                   out_shape=out_shape, compiler_params=_params(("parallel", "arbitrary")))(*args)


def _group_weights(lses):
    mx = jnp.maximum(jnp.maximum(lses[0], lses[1]), lses[2])
    e = [jnp.exp(l - mx) for l in lses]
    tot = e[0] + e[1] + e[2]
    return [t / tot for t in e]


def b_combine_fwd(os_, lses, name):
    n, wd = os_[0].shape
    tm = _tile(n, 512, 8)

    def body(o0, o1, o2, l0, l1, l2, out_ref):
        wts = _group_weights([l0[...], l1[...], l2[...]])
        out_ref[...] = wts[0] * o0[...] + wts[1] * o1[...] + wts[2] * o2[...]

    row = pl.BlockSpec((tm, wd), lambda i: (i, 0))
    return _pallas(body, name=name, grid=(n // tm,), in_specs=[row] * 6, out_specs=row,
                   out_shape=_sds((n, wd), F32), compiler_params=_params(("parallel",)))(*os_, *lses)


def b_combine_bwd(dout, os_, lses, name):
    n, wd = dout.shape
    tm = _tile(n, 256, 8)

    def body(d_ref, o0, o1, o2, l0, l1, l2, do0, do1, do2, dl0, dl1, dl2):
        dv = d_ref[...]
        wts = _group_weights([l0[...], l1[...], l2[...]])
        dws = []
        for o_ref in (o0, o1, o2):
            prod = dv * o_ref[...]
            cols = []
            for h in range(wd // HEAD):
                sseg = jnp.sum(prod[:, h * HEAD:(h + 1) * HEAD], axis=-1, keepdims=True)
                cols.append(jnp.broadcast_to(sseg, (tm, HEAD)))
            dws.append(jnp.concatenate(cols, axis=-1))
        mean = wts[0] * dws[0] + wts[1] * dws[1] + wts[2] * dws[2]
        for wt, dw, do_ref, dl_ref in zip(wts, dws, (do0, do1, do2), (dl0, dl1, dl2)):
            do_ref[...] = wt * dv
            dl_ref[...] = wt * (dw - mean)

    row = pl.BlockSpec((tm, wd), lambda i: (i, 0))
    outs = _pallas(body, name=name, grid=(n // tm,), in_specs=[row] * 7, out_specs=[row] * 6,
                   out_shape=[_sds((n, wd), F32)] * 6, compiler_params=_params(("parallel",)))(dout, *os_, *lses)
    return outs[:3], outs[3:]


def _c_rows(n):
    rows = n // GRID_W
    return rows, min(C_WIN_ROWS, rows)


def _c_row_start(r, rows, wr):
    return jnp.clip(r - wr // 2, 0, rows - wr)


def _c_bias_index(r, rows, wr):
    return _c_row_start(r, rows, wr) - r + (C_WIN_ROWS - 1)


def _col_shift_select(tile, cq, inverse):
    lanes = tile.shape[1]
    for b in range(6):
        amt = (lanes - (1 << b)) if inverse else (1 << b)
        tile = jnp.where(((cq >> b) & 1) == 1, pltpu.roll(tile, amt, 1), tile)
    return tile


def rpb_expand(rwin, name):
    lanes = rwin.shape[-1]

    def body(r_ref, b_ref):
        cq = lax.broadcasted_iota(jnp.int32, (GRID_W, lanes), 0)
        ck = lax.broadcasted_iota(jnp.int32, (GRID_W, lanes), 1) % GRID_W
        tile = jnp.broadcast_to(r_ref[...], (GRID_W, lanes))
        tile = pltpu.roll(tile, lanes - (C_WIN_COLS - 1), 1)
        tile = _col_shift_select(tile, cq, False)
        cs = jnp.clip(cq - C_WIN_COLS // 2, 0, GRID_W - C_WIN_COLS)
        ok = (ck >= cs) & (ck < cs + C_WIN_COLS)
        b_ref[...] = jnp.where(ok, tile, NEG)

    return _pallas(body, name=name, grid=(C_HEADS, C_WIN_ROWS),
                   in_specs=[pl.BlockSpec((None, None, 1, lanes), lambda h, i: (h, i, 0, 0))],
                   out_specs=pl.BlockSpec((None, None, GRID_W, lanes), lambda h, i: (h, i, 0, 0)),
                   out_shape=_sds((C_HEADS, C_WIN_ROWS, GRID_W, lanes), F32),
                   compiler_params=_params(("parallel", "parallel")))(rwin)


def rpb_reduce(dbias, name):
    lanes = dbias.shape[-1]
    wr = lanes // GRID_W

    def body(d_ref, o_ref, acc_ref):
        i0 = pl.program_id(1)
        cq = lax.broadcasted_iota(jnp.int32, (GRID_W, lanes), 0)
        tile = _col_shift_select(d_ref[...], cq, True)
        tile = pltpu.roll(tile, C_WIN_COLS - 1, 1)
        vec = jnp.sum(tile, axis=0, keepdims=True)

        @pl.when(i0 == 0)
        def _():
            acc_ref[...] = jnp.zeros_like(acc_ref)

        for w in range(wr):
            acc_ref[pl.ds(i0 + w, 1), :] += vec[:, w * GRID_W:(w + 1) * GRID_W]

        @pl.when(i0 == C_WIN_ROWS - 1)
        def _():
            o_ref[...] = acc_ref[...]

    return _pallas(body, name=name, grid=(C_HEADS, C_WIN_ROWS),
                   in_specs=[pl.BlockSpec((None, None, GRID_W, lanes), lambda h, i: (h, i, 0, 0))],
                   out_specs=pl.BlockSpec((None, 16, GRID_W), lambda h, i: (h, 0, 0)),
                   out_shape=_sds((C_HEADS, 16, GRID_W), F32),
                   scratch_shapes=[pltpu.VMEM((16, GRID_W), F32)],
                   compiler_params=_params(("parallel", "arbitrary")))(dbias)


def _store_or_add(ref, val, first):
    @pl.when(first)
    def _():
        ref[...] = val

    @pl.when(jnp.logical_not(first))
    def _():
        ref[...] += val


def c_attn_fwd(q, k, v, bias, name, nh=4):
    n = q.shape[0]
    rows, wr = _c_rows(n)
    wk = wr * GRID_W

    def body(q_ref, k_ref, v_ref, b_ref, o_ref, lse_ref):
        r = pl.program_id(1)
        start = pl.multiple_of(_c_row_start(r, rows, wr) * GRID_W, GRID_W)
        for h in range(nh):
            sl = slice(h * HEAD, (h + 1) * HEAD)
            k_t = k_ref[pl.ds(start, wk), sl]
            v_t = v_ref[pl.ds(start, wk), sl]
            s = lax.dot_general(q_ref[:, sl], k_t, _DIMS["nt"], preferred_element_type=F32) * SCALE + b_ref[h]
            mx = jnp.max(s, axis=-1, keepdims=True)
            p = jnp.exp(s - mx)
            den = jnp.sum(p, axis=-1, keepdims=True)
            o_ref[:, sl] = jnp.dot((p / den).astype(CDT), v_t, preferred_element_type=F32)
            lse_ref[:, sl] = jnp.broadcast_to(mx + jnp.log(den), (GRID_W, HEAD))

    qspec = pl.BlockSpec((GRID_W, nh * HEAD), lambda h, r: (r, h))
    kspec = pl.BlockSpec((n, nh * HEAD), lambda h, r: (0, h))
    bspec = pl.BlockSpec((nh, None, GRID_W, wk), lambda h, r: (h, _c_bias_index(r, rows, wr), 0, 0))
    return _pallas(body, name=name, grid=(C_HEADS // nh, rows), in_specs=[qspec, kspec, kspec, bspec],
                   out_specs=[qspec, qspec], out_shape=[_sds(q.shape, F32), _sds(q.shape, F32)],
                   compiler_params=_params(("parallel", "arbitrary")))(q, k, v, bias)


def c_attn_bwd(q, k, v, bias, o, lse, do, name, nh=2):
    n = q.shape[0]
    rows, wr = _c_rows(n)
    wk = wr * GRID_W

    def body(q_ref, k_ref, v_ref, b_ref, o_ref, lse_ref, do_ref, dq_ref, dk_ref, dv_ref, db_ref):
        r = pl.program_id(1)
        rs = _c_row_start(r, rows, wr)
        start = pl.multiple_of(rs * GRID_W, GRID_W)

        @pl.when(r == 0)
        def _():
            dk_ref[...] = jnp.zeros_like(dk_ref)
            dv_ref[...] = jnp.zeros_like(dv_ref)

        prev = _c_row_start(jnp.maximum(r - 1, 0), rows, wr) - jnp.maximum(r - 1, 0)
        first = (r == 0) | (prev != rs - r)
        for h in range(nh):
            sl = slice(h * HEAD, (h + 1) * HEAD)
            k_t = k_ref[pl.ds(start, wk), sl]
            v_t = v_ref[pl.ds(start, wk), sl]
            q_t = q_ref[:, sl]
            s = lax.dot_general(q_t, k_t, _DIMS["nt"], preferred_element_type=F32) * SCALE + b_ref[h]
            p = jnp.exp(s - lse_ref[:, sl][:, :1])
            do_t = do_ref[:, sl]
            delta = jnp.sum(do_t * o_ref[:, sl], axis=-1, keepdims=True)
            do_c = do_t.astype(CDT)
            dp = lax.dot_general(do_c, v_t, _DIMS["nt"], preferred_element_type=F32)
            dv_ref[pl.ds(start, wk), sl] += lax.dot_general(p.astype(CDT), do_c, _DIMS["tn"], preferred_element_type=F32)
            ds = p * (dp - delta)
            ds_c = (ds * SCALE).astype(CDT)
            dq_ref[:, sl] = jnp.dot(ds_c, k_t, preferred_element_type=F32)
            dk_ref[pl.ds(start, wk), sl] += lax.dot_general(ds_c, q_t, _DIMS["tn"], preferred_element_type=F32)
            _store_or_add(db_ref.at[h], ds, first)

    qspec = pl.BlockSpec((GRID_W, nh * HEAD), lambda h, r: (r, h))
    kspec = pl.BlockSpec((n, nh * HEAD), lambda h, r: (0, h))
    bspec = pl.BlockSpec((nh, None, GRID_W, wk), lambda h, r: (h, _c_bias_index(r, rows, wr), 0, 0))
    return _pallas(body, name=name, grid=(C_HEADS // nh, rows),
                   in_specs=[qspec, kspec, kspec, bspec, qspec, qspec, qspec],
                   out_specs=[qspec, kspec, kspec, bspec],
                   out_shape=[_sds(q.shape, F32), _sds(k.shape, F32), _sds(k.shape, F32), _sds(bias.shape, F32)],
                   compiler_params=_params(("parallel", "arbitrary")))(q, k, v, bias, o, lse, do)


def _sigmoid(z):
    return 1.0 / (1.0 + jnp.exp(-z))


def _gate_specs(n, d):
    tm, tn = _tile(n, 256, 8), _tile(math.gcd(d, QKV_W), 1024)
    nj = d // tn
    tile = pl.BlockSpec((tm, tn), lambda i, j: (i, j))
    gl = [pl.BlockSpec((tm, tn), functools.partial(lambda i, j, b: (i, (QKV_W + b * d) // tn + j), b=b)) for b in range(3)]
    return tm, tn, nj, tile, gl


def gate_merge(proj, ys, name):
    n, d = ys[0].shape
    tm, tn, nj, tile, gl = _gate_specs(n, d)

    def body(g0, g1, g2, y0, y1, y2, out_ref):
        acc = _sigmoid(g0[...]) * y0[...] + _sigmoid(g1[...]) * y1[...] + _sigmoid(g2[...]) * y2[...]
        out_ref[...] = acc.astype(out_ref.dtype)

    return _pallas(body, name=name, grid=(n // tm, nj), in_specs=gl + [tile] * 3, out_specs=tile,
                   out_shape=_sds((n, d), CDT), compiler_params=_params(("parallel", "parallel")))(proj, proj, proj, *ys)


def gate_bwd(proj, ys, dmerged, name):
    n, d = dmerged.shape
    tm, tn, nj, tile, gl = _gate_specs(n, d)

    def body(g0, g1, g2, y0, y1, y2, dm_ref, dy0, dy1, dy2, dg0, dg1, dg2):
        dm = dm_ref[...]
        for g_ref, y_ref, dy_ref, dg_ref in ((g0, y0, dy0, dg0), (g1, y1, dy1, dg1), (g2, y2, dy2, dg2)):
            sg = _sigmoid(g_ref[...])
            dy_ref[...] = (dm * sg).astype(dy_ref.dtype)
            dg_ref[...] = (dm * y_ref[...] * sg * (1.0 - sg)).astype(dg_ref.dtype)

    outs = _pallas(body, name=name, grid=(n // tm, nj), in_specs=gl + [tile] * 4, out_specs=[tile] * 6,
                   out_shape=[_sds((n, d), CDT)] * 6,
                   compiler_params=_params(("parallel", "parallel")))(proj, proj, proj, *ys, dmerged)
    return outs[:3], outs[3:]


def swiglu_fwd(gu, name):
    _, n, ff = gu.shape
    tm, tn = _tile(n, 512, 8), _tile(ff, 1024)

    def body(g_ref, u_ref, a_ref):
        gt = g_ref[...]
        a_ref[...] = (gt * _sigmoid(gt) * u_ref[...]).astype(a_ref.dtype)

    return _pallas(body, name=name, grid=(n // tm, ff // tn),
                   in_specs=[pl.BlockSpec((None, tm, tn), lambda i, j: (0, i, j)),
                             pl.BlockSpec((None, tm, tn), lambda i, j: (1, i, j))],
                   out_specs=pl.BlockSpec((tm, tn), lambda i, j: (i, j)), out_shape=_sds((n, ff), CDT),
                   compiler_params=_params(("parallel", "parallel")))(gu, gu)


def swiglu_bwd(gu, dact, name):
    _, n, ff = gu.shape
    tm, tn = _tile(n, 512, 8), _tile(ff, 1024)

    def body(g_ref, u_ref, da_ref, d_ref):
        gt, up, da = g_ref[...], u_ref[...], da_ref[...]
        sg = _sigmoid(gt)
        d_ref[0] = (da * up * (sg + gt * sg * (1.0 - sg))).astype(d_ref.dtype)
        d_ref[1] = (da * gt * sg).astype(d_ref.dtype)

    return _pallas(body, name=name, grid=(n // tm, ff // tn),
                   in_specs=[pl.BlockSpec((None, tm, tn), lambda i, j: (0, i, j)),
                             pl.BlockSpec((None, tm, tn), lambda i, j: (1, i, j)),
                             pl.BlockSpec((tm, tn), lambda i, j: (i, j))],
                   out_specs=pl.BlockSpec((2, tm, tn), lambda i, j: (0, i, j)), out_shape=_sds((2, n, ff), CDT),
                   compiler_params=_params(("parallel", "parallel")))(gu, gu, dact)


def loss_head(y, target, name):
    n, d = y.shape
    tm = _tile(n, 512, 8)
    nsteps = n // tm

    def body(y_ref, t_ref, l_ref, dy_ref, acc_ref):
        i = pl.program_id(0)
        e = y_ref[...] - t_ref[...]
        dy_ref[...] = e * (1.0 / d)
        part = jnp.sum((e * e).reshape(tm // 8, 8, d), axis=0)

        @pl.when(i == 0)
        def _():
            acc_ref[...] = part

        @pl.when(i > 0)
        def _():
            acc_ref[...] += part

        @pl.when(i == nsteps - 1)
        def _():
            tot = jnp.sum(jnp.sum(acc_ref[...], axis=1, keepdims=True), axis=0, keepdims=True) * (0.5 / d)
            l_ref[...] = jnp.broadcast_to(tot, (8, HEAD))

    row = pl.BlockSpec((tm, d), lambda i: (i, 0))
    return _pallas(body, name=name, grid=(nsteps,), in_specs=[row, row],
                   out_specs=[pl.BlockSpec((8, HEAD), lambda i: (0, 0)), row],
                   out_shape=[_sds((8, HEAD), F32), _sds((n, d), F32)],
                   scratch_shapes=[pltpu.VMEM((8, d), F32)],
                   compiler_params=_params(("arbitrary",)))(y, target)


def adamw(w, g, m, v, name):
    r, c = w.shape
    tr = _tile(r, max(8, (1 << 19) // c), 8)
    c1 = 1.0 - ADAM_B1 ** ADAM_STEP
    c2 = 1.0 - ADAM_B2 ** ADAM_STEP

    def body(w_ref, g_ref, m_ref, v_ref, d_ref, mo_ref, vo_ref):
        gv = g_ref[...]
        mn = ADAM_B1 * m_ref[...] + (1.0 - ADAM_B1) * gv
        vn = ADAM_B2 * v_ref[...] + (1.0 - ADAM_B2) * (gv * gv)
        d_ref[...] = -ADAM_LR * ((mn / c1) / (jnp.sqrt(vn / c2) + ADAM_EPS) + ADAM_WD * w_ref[...])
        mo_ref[...] = mn
        vo_ref[...] = vn

    row = pl.BlockSpec((tr, c), lambda i: (i, 0))
    return _pallas(body, name=name, grid=(r // tr,), in_specs=[row] * 4, out_specs=[row] * 3,
                   out_shape=[_sds((r, c), F32)] * 3, compiler_params=_params(("parallel",)))(w, g, m, v)


ANY = pl.BlockSpec(memory_space=pl.ANY)


def _place():
    x, y, c = lax.axis_index("x"), lax.axis_index("y"), lax.axis_index("c")
    return x, y, c, [(1 - x, y), (x, 1 - y), (1 - x, 1 - y)]


def _rcopy(src, dst, send_sems, recv_sems, k, to):
    return pltpu.make_async_remote_copy(src_ref=src, dst_ref=dst, send_sem=send_sems.at[k], recv_sem=recv_sems.at[k],
                                        device_id=to, device_id_type=MESH)


def cast_place(chip_idx, shard, name):
    k, ns = shard.shape
    tr = _tile(k, max(16, (1 << 19) // ns), 16)

    def body(k_ref, s_ref, o_ref):
        o_ref[...] = s_ref[...].astype(o_ref.dtype)

    gs = pltpu.PrefetchScalarGridSpec(
        num_scalar_prefetch=1, grid=(k // tr,),
        in_specs=[pl.BlockSpec((tr, ns), lambda i, k_ref: (i, 0))],
        out_specs=pl.BlockSpec((None, tr, ns), lambda i, k_ref: (k_ref[0], i, 0)))
    return _pallas(body, name=name, grid_spec=gs, out_shape=_sds((N_CHIPS, k, ns), CDT),
                   compiler_params=_params(("parallel",)))(chip_idx, shard)


HBM = pl.BlockSpec(memory_space=pltpu.HBM)
SEM = pl.BlockSpec(memory_space=pltpu.SEMAPHORE)
EFFECT = pltpu.SideEffectType.DATAFLOW_SIDE_EFFECTING


def _in_hbm(a):
    return pltpu.with_memory_space_constraint(a, pltpu.HBM)


def _gather_copies(refs, send_sems, recv_sems):
    x, y, c, chips = _place()
    me = 2 * x + y
    out = []
    for t, ref in enumerate(refs):
        kh = ref.shape[1] // 2
        for j, (px, py) in enumerate(chips):
            send = _rcopy(ref.at[me, pl.ds(c * kh, kh)], ref.at[me, pl.ds(c * kh, kh)], send_sems, recv_sems,
                          3 * t + j, (px, py, c))
            land = ref.at[2 * px + py, pl.ds(c * kh, kh)]
            out.append((send, _rcopy(land, land, send_sems, recv_sems, 3 * t + j, (px, py, c))))
    return out


def gather_start(bufs, after, name):
    nt = len(bufs)

    def body(*refs):
        ins, send_sems, recv_sems, token = refs[:nt], refs[nt + 1], refs[nt + 2], refs[-1]
        for send, _ in _gather_copies(ins, send_sems, recv_sems):
            send.start()
        token[...] = jnp.zeros_like(token)

    outs = _pallas(body, name=name, in_specs=[HBM] * nt + [ANY],
                   out_specs=(SEM, SEM) + (HBM,) * nt + (pl.BlockSpec(memory_space=pltpu.VMEM),),
                   out_shape=(pltpu.SemaphoreType.DMA((3 * nt,)), pltpu.SemaphoreType.DMA((3 * nt,)))
                   + tuple(pltpu.HBM(b.shape, b.dtype) for b in bufs) + (_sds((8, HEAD), F32),),
                   input_output_aliases={t: 2 + t for t in range(nt)},
                   compiler_params=pltpu.CompilerParams(has_side_effects=EFFECT))(*[_in_hbm(b) for b in bufs], after)
    return outs[0], outs[1], list(outs[2:2 + nt]), outs[-1]


def gather_wait(bufs, send_sems, recv_sems, after, name):
    nt = len(bufs)

    def body(*refs):
        ins, s_sems, r_sems = refs[:nt], refs[nt], refs[nt + 1]
        for send, land in _gather_copies(ins, s_sems, r_sems):
            send.wait_send()
            land.wait_recv()

    return _pallas(body, name=name, in_specs=[HBM] * nt + [SEM, SEM, ANY], out_specs=[HBM] * nt,
                   out_shape=[pltpu.HBM(b.shape, b.dtype) for b in bufs],
                   input_output_aliases={t: t for t in range(nt)},
                   compiler_params=pltpu.CompilerParams(has_side_effects=EFFECT))(*bufs, send_sems, recv_sems, after)


def pair_forward(bufs, name):
    nt = len(bufs)

    def body(*refs):
        outs = refs[nt:2 * nt]
        send_sems, recv_sems = refs[2 * nt:]
        x, y, c, chips = _place()
        cps = []
        for t in range(nt):
            kh = outs[t].shape[1] // 2
            for j, (px, py) in enumerate(chips):
                blk = outs[t].at[2 * px + py, pl.ds(c * kh, kh)]
                cps.append(_rcopy(blk, blk, send_sems, recv_sems, 3 * t + j, (x, y, 1 - c)))
                cps[-1].start()
        for t in range(nt):
            kh = outs[t].shape[1] // 2
            for j, (px, py) in enumerate(chips):
                blk = outs[t].at[2 * px + py, pl.ds((1 - c) * kh, kh)]
                _rcopy(blk, blk, send_sems, recv_sems, 3 * t + j, (x, y, 1 - c)).wait_recv()
        for cp in cps:
            cp.wait_send()

    return _pallas(body, name=name, in_specs=[ANY] * nt, out_specs=[ANY] * nt,
                   out_shape=[_sds(b.shape, b.dtype) for b in bufs],
                   input_output_aliases={t: t for t in range(nt)},
                   scratch_shapes=[pltpu.SemaphoreType.DMA((3 * nt,)), pltpu.SemaphoreType.DMA((3 * nt,))],
                   compiler_params=pltpu.CompilerParams(has_side_effects=True))(*bufs)


def pair_exchange(grads, name):
    nt = len(grads)

    def body(*refs):
        ins, outs = refs[:nt], refs[nt:2 * nt]
        send_sems, recv_sems = refs[2 * nt:]
        x, y, c, _ = _place()
        sibling = (x, y, 1 - c)
        cps = []
        for t in range(nt):
            kh = ins[t].shape[1] // 2
            cps.append(_rcopy(ins[t].at[:, pl.ds((1 - c) * kh, kh), :], outs[t], send_sems, recv_sems, t, sibling))
            cps[-1].start()
        for cp in cps:
            cp.wait_recv()
        for cp in cps:
            cp.wait_send()

    return _pallas(body, name=name, in_specs=[ANY] * nt, out_specs=[ANY] * nt,
                   out_shape=[_sds((N_CHIPS, g.shape[1] // 2, g.shape[2]), g.dtype) for g in grads],
                   scratch_shapes=[pltpu.SemaphoreType.DMA((nt,)), pltpu.SemaphoreType.DMA((nt,))],
                   compiler_params=pltpu.CompilerParams(has_side_effects=True))(*grads)


def _exchange_copies(sums, lands, send_sems, recv_sems):
    x, y, c, chips = _place()
    return [_rcopy(s.at[2 * px + py], l.at[j], send_sems, recv_sems, 3 * t + j, (px, py, c))
            for t, (s, l) in enumerate(zip(sums, lands)) for j, (px, py) in enumerate(chips)]


def exchange_start(sums, name):
    nt = len(sums)
    lands = [lax.empty((3,) + s.shape[1:], s.dtype) for s in sums]

    def body(*refs):
        ins, zones, send_sems, recv_sems, token = refs[:nt], refs[nt:2 * nt], refs[2 * nt], refs[2 * nt + 1], refs[-1]
        for cp in _exchange_copies(ins, zones, send_sems, recv_sems):
            cp.start()
        token[...] = jnp.zeros_like(token)

    outs = _pallas(body, name=name, in_specs=[HBM] * (2 * nt),
                   out_specs=(SEM, SEM) + (HBM,) * (2 * nt) + (pl.BlockSpec(memory_space=pltpu.VMEM),),
                   out_shape=(pltpu.SemaphoreType.DMA((3 * nt,)), pltpu.SemaphoreType.DMA((3 * nt,)))
                   + tuple(pltpu.HBM(a.shape, a.dtype) for a in list(sums) + lands) + (_sds((8, HEAD), F32),),
                   input_output_aliases={t: 2 + t for t in range(2 * nt)},
                   compiler_params=pltpu.CompilerParams(has_side_effects=EFFECT))(*[_in_hbm(a) for a in list(sums) + lands])
    return outs[0], outs[1], list(outs[2:2 + nt]), list(outs[2 + nt:2 + 2 * nt]), outs[-1]


def exchange_wait(sums, lands, send_sems, recv_sems, after, name):
    nt = len(sums)

    def body(*refs):
        ins, zones, s_sems, r_sems = refs[:nt], refs[nt:2 * nt], refs[2 * nt], refs[2 * nt + 1]
        for cp in _exchange_copies(ins, zones, s_sems, r_sems):
            cp.wait_send()
            cp.wait_recv()

    outs = _pallas(body, name=name, in_specs=[HBM] * (2 * nt) + [SEM, SEM, ANY], out_specs=[HBM] * (2 * nt),
                   out_shape=[pltpu.HBM(a.shape, a.dtype) for a in list(sums) + list(lands)],
                   input_output_aliases={t: t for t in range(2 * nt)},
                   compiler_params=pltpu.CompilerParams(has_side_effects=EFFECT))(*sums, *lands, send_sems, recv_sems, after)
    return list(outs[:nt]), list(outs[nt:])


def pair_share(halves, name):
    nt = len(halves)

    def body(*refs):
        ins, outs = refs[:nt], refs[nt:2 * nt]
        send_sems, recv_sems = refs[2 * nt:]
        x, y, c, _ = _place()
        cps = []
        for t in range(nt):
            cps.append(_rcopy(ins[t], outs[t], send_sems, recv_sems, t, (x, y, 1 - c)))
            cps[-1].start()
        for cp in cps:
            cp.wait_recv()
        for cp in cps:
            cp.wait_send()

    return _pallas(body, name=name, in_specs=[ANY] * nt, out_specs=[ANY] * nt,
                   out_shape=[_sds(h.shape, h.dtype) for h in halves],
                   scratch_shapes=[pltpu.SemaphoreType.DMA((nt,)), pltpu.SemaphoreType.DMA((nt,))],
                   compiler_params=pltpu.CompilerParams(has_side_effects=True))(*halves)


def small_allreduce(pack, name):
    r = pack.shape[0]

    def body(in_ref, out_ref, buf, send_sems, recv_sems):
        x, y, c, _ = _place()
        me = 4 * x + 2 * y + c
        sends = []
        for k in range(1, 8):
            to = ((x + ((k >> 2) & 1)) % 2, (y + ((k >> 1) & 1)) % 2, (c + (k & 1)) % 2)
            cp = _rcopy(in_ref, buf.at[me], send_sems, recv_sems, k - 1, to)
            cp.start()
            sends.append((cp, to))
        buf[pl.ds(me, 1)] = in_ref[...][None]
        for k, (_, to) in enumerate(sends):
            peer = 4 * to[0] + 2 * to[1] + to[2]
            _rcopy(in_ref, buf.at[peer], send_sems, recv_sems, k, to).wait_recv()
        for cp, _ in sends:
            cp.wait_send()
        acc = buf[0]
        for d in range(1, 8):
            acc = acc + buf[d]
        out_ref[...] = acc

    vm = pl.BlockSpec(memory_space=pltpu.VMEM)
    return _pallas(body, name=name, in_specs=[vm], out_specs=vm, out_shape=_sds((r, HEAD), F32),
                   scratch_shapes=[pltpu.VMEM((8, r, HEAD), F32), pltpu.SemaphoreType.DMA((7,)),
                                   pltpu.SemaphoreType.DMA((7,))],
                   compiler_params=pltpu.CompilerParams(has_side_effects=True))(pack)


def add_halves(c_idx, grad, other, name):
    _, k, ns = grad.shape
    kh = k // 2
    tr = _tile(kh, max(16, (1 << 19) // ns), 16)
    nr = kh // tr

    def body(c_ref, g_ref, o_ref, s_ref):
        s_ref[...] = (g_ref[...] + o_ref[...]).astype(s_ref.dtype)

    gs = pltpu.PrefetchScalarGridSpec(
        num_scalar_prefetch=1, grid=(N_CHIPS, nr),
        in_specs=[pl.BlockSpec((None, tr, ns), lambda g, i, c_ref: (g, c_ref[0] * nr + i, 0)),
                  pl.BlockSpec((None, tr, ns), lambda g, i, c_ref: (g, i, 0))],
        out_specs=pl.BlockSpec((None, tr, ns), lambda g, i, c_ref: (g, i, 0)))
    return _pallas(body, name=name, grid_spec=gs, out_shape=_sds((N_CHIPS, kh, ns), XDT),
                   compiler_params=_params(("parallel", "parallel")))(c_idx, grad, other)


def add_chips(chip_idx, sums, recv, stack, layer, n_layers, name):
    _, kh, ns = sums.shape
    tr = _tile(kh, max(16, (1 << 19) // ns), 16)
    has_stack = stack is not None

    def body(k_ref, s_ref, r0, r1, r2, *rest):
        o_ref = rest[-1]
        o_ref[...] = ((s_ref[...].astype(F32) + r0[...].astype(F32)) + r1[...].astype(F32)) + r2[...].astype(F32)

    rspec = [pl.BlockSpec((None, tr, ns), functools.partial(lambda i, k_ref, j: (j, i, 0), j=j)) for j in range(3)]
    gs = pltpu.PrefetchScalarGridSpec(
        num_scalar_prefetch=1, grid=(kh // tr,),
        in_specs=[pl.BlockSpec((None, tr, ns), lambda i, k_ref: (k_ref[0], i, 0))] + rspec + ([ANY] if has_stack else []),
        out_specs=pl.BlockSpec((None, tr, ns), lambda i, k_ref: (layer, i, 0)))
    args = (chip_idx, sums, recv, recv, recv) + ((stack,) if has_stack else ())
    return _pallas(body, name=name, grid_spec=gs, out_shape=_sds((n_layers, kh, ns), F32),
                   input_output_aliases={5: 0} if has_stack else {},
                   compiler_params=_params(("parallel",)))(*args)


def adamw_big(c_idx, w, m, v, mine, other, name):
    nl, k, ns = w.shape
    kh = k // 2
    tr = _tile(kh, max(8, (1 << 18) // ns), 8)
    nr = kh // tr
    c1 = 1.0 - ADAM_B1 ** ADAM_STEP
    c2 = 1.0 - ADAM_B2 ** ADAM_STEP

    def body(c_ref, w_ref, m_ref, v_ref, a_ref, b_ref, g_ref, d_ref, mo_ref, vo_ref):
        gv = jnp.where(pl.program_id(2) == c_ref[0], a_ref[...], b_ref[...])
        mn = ADAM_B1 * m_ref[...] + (1.0 - ADAM_B1) * gv
        vn = ADAM_B2 * v_ref[...] + (1.0 - ADAM_B2) * (gv * gv)
        g_ref[...] = gv
        d_ref[...] = -ADAM_LR * ((mn / c1) / (jnp.sqrt(vn / c2) + ADAM_EPS) + ADAM_WD * w_ref[...])
        mo_ref[...] = mn
        vo_ref[...] = vn

    full = pl.BlockSpec((None, tr, ns), lambda l, i, hh, c_ref: (l, hh * nr + i, 0))
    half = pl.BlockSpec((None, tr, ns), lambda l, i, hh, c_ref: (l, i, 0))
    gs = pltpu.PrefetchScalarGridSpec(num_scalar_prefetch=1, grid=(nl, nr, 2),
                                      in_specs=[full, full, full, half, half], out_specs=[full] * 4)
    return _pallas(body, name=name, grid_spec=gs, out_shape=[_sds(w.shape, F32)] * 4,
                   compiler_params=_params(("parallel", "parallel", "arbitrary")))(c_idx, w, m, v, mine, other)


W_NAMES = ("w_in", "w_br_a", "w_br_b", "w_br_c", "w_o", "w_gate_up", "w_down")


def _rope_tables(n):
    half = HEAD // 2
    inv_freq = ROPE_THETA ** (-jnp.arange(half, dtype=F32) * 2.0 / HEAD)
    ang = jnp.arange(n, dtype=F32)[:, None] * inv_freq[None, :]
    cos, sin = jnp.cos(ang), jnp.sin(ang)
    return jnp.concatenate([cos, cos], axis=-1), jnp.concatenate([-sin, sin], axis=-1)


def _rpb_windows(rpb):
    pad = jnp.pad(rpb, ((0, 0), (0, 1), (0, GRID_W - rpb.shape[2])))
    wins = [pad[:, i0:i0 + C_WIN_ROWS].reshape(C_HEADS, 1, C_WIN_ROWS * GRID_W) for i0 in range(C_WIN_ROWS)]
    return jnp.stack(wins, axis=1)


def _b_view(t, dil):
    n, wd = t.shape
    return t.reshape(n // dil, dil * wd)


def layer_fwd(x, p, w, cos2, sin2):
    n, d = x.shape
    s = {"x": x}
    s["h"] = rmsnorm_fwd(x, p["norm1_g"], "norm1")
    s["proj"] = mm_x_wcol(s["h"], w["w_in"], F32, "proj")
    gains = jnp.pad(p["qk_norm_g"], ((0, 2), (0, 0)))
    pp = s["pp"] = qk_prep(s["proj"], gains, cos2, sin2, "qk_prep")
    sink = p["sink_a"].reshape(1, A_Q_HEADS)
    s["oa"], s["lse_a"] = band_attn_fwd(pp["qa"], pp["ka"], pp["va"], sink, seqs=A_KV_HEADS, G=A_GROUP,
                                        nh=A_KV_HEADS, radius=A_RADIUS, name="attn_a")
    s["ob"], s["lse_b"] = [], []
    for g, dil in enumerate(B_DILS):
        o, lse = band_attn_fwd(_b_view(pp[f"qb{g}"], dil), _b_view(pp[f"kb{g}"], dil), _b_view(pp[f"vb{g}"], dil),
                               None, seqs=dil * B_HG, G=1, nh=B_HG, radius=B_RADIUS, name=f"attn_b{g}")
        s["ob"].append(o.reshape(n, B_HG * HEAD))
        s["lse_b"].append(lse.reshape(n, B_HG * HEAD))
    ob = b_combine_fwd(s["ob"], s["lse_b"], "b_combine")
    s["bias"] = rpb_expand(_rpb_windows(p["rpb_c"]), "rpb_expand")
    s["oc"], s["lse_c"] = c_attn_fwd(pp["qc"], pp["kc"], pp["vc"], s["bias"], "attn_c")
    s["o_in"] = (s["oa"], ob, s["oc"])
    s["ys"] = [mm_x_wcol(o, w[k], F32, "branch_" + k[-1]) for o, k in zip(s["o_in"], ("w_br_a", "w_br_b", "w_br_c"))]
    s["merged"] = gate_merge(s["proj"], s["ys"], "gate_merge")
    s["x_mid"] = mm_x_w(s["merged"], w["w_o"], "out_proj", res=x)
    s["h2"] = rmsnorm_fwd(s["x_mid"], p["norm2_g"], "norm2")
    s["gu"] = mm_x_wcol(s["h2"], w["w_gate_up"], F32, "gate_up", tn_pref=1408, stacked_out=2)
    s["act"] = swiglu_fwd(s["gu"], "swiglu")
    x_out = mm_x_w(s["act"], w["w_down"], "down", res=s["x_mid"])
    return x_out, s


def layer_bwd(dx_out, s, p, w, cos2, sin2):
    n, d = dx_out.shape
    pp = s["pp"]
    dact = mm_x_wT(dx_out, w["w_down"], "d_act")
    dw_down = mm_aT_d(s["act"], dx_out, "dw_down")
    dgu = swiglu_bwd(s["gu"], dact, "swiglu_bwd")
    dh2 = mm_x_wcolT(dgu, w["w_gate_up"], "d_h2", tm_pref=512, tk_pref=1408, stacked_in=2)
    dw_gu = mm_aT_d_wcol(s["h2"], dgu, "dw_gate_up", tn_pref=1408, stacked_in=2)
    dx_mid, dg2 = rmsnorm_bwd(s["x_mid"], p["norm2_g"], dh2, dx_out, "norm2_bwd")

    dmerged = mm_x_wT(dx_mid, w["w_o"], "d_merged")
    dw_o = mm_aT_d(s["merged"], dx_mid, "dw_o")
    dys, dgls = gate_bwd(s["proj"], s["ys"], dmerged, "gate_bwd")
    dos, dw_br = [], []
    for o, dy, k in zip(s["o_in"], dys, ("w_br_a", "w_br_b", "w_br_c")):
        dos.append(mm_x_wcolT(dy, w[k], "d_o_" + k[-1], tm_pref=512))
        dw_br.append(mm_aT_d_wcol(o, dy, "dw_br_" + k[-1]))

    grads = {}
    sink = p["sink_a"].reshape(1, A_Q_HEADS)
    grads["qa"], grads["ka"], grads["va"], dsink = band_attn_bwd(
        pp["qa"], pp["ka"], pp["va"], sink, s["oa"], s["lse_a"], dos[0], None,
        seqs=A_KV_HEADS, G=A_GROUP, nh=A_KV_HEADS, radius=A_RADIUS, name="attn_a_bwd")
    dobs, dlses = b_combine_bwd(dos[1], s["ob"], s["lse_b"], "b_combine_bwd")
    for g, dil in enumerate(B_DILS):
        dq, dk, dv = band_attn_bwd(_b_view(pp[f"qb{g}"], dil), _b_view(pp[f"kb{g}"], dil), _b_view(pp[f"vb{g}"], dil),
                                   None, _b_view(s["ob"][g], dil), _b_view(s["lse_b"][g], dil), _b_view(dobs[g], dil),
                                   _b_view(dlses[g], dil), seqs=dil * B_HG, G=1, nh=2, radius=B_RADIUS,
                                   name=f"attn_b{g}_bwd")
        grads[f"qb{g}"], grads[f"kb{g}"], grads[f"vb{g}"] = [t.reshape(n, B_HG * HEAD) for t in (dq, dk, dv)]
    grads["qc"], grads["kc"], grads["vc"], dbias = c_attn_bwd(pp["qc"], pp["kc"], pp["vc"], s["bias"], s["oc"],
                                                              s["lse_c"], dos[2], "attn_c_bwd")
    drpb = rpb_reduce(dbias, "rpb_reduce")[:, :2 * C_WIN_ROWS - 1, :2 * C_WIN_COLS - 1]
    gains = jnp.pad(p["qk_norm_g"], ((0, 2), (0, 0)))
    dqkv, dgains = qk_prep_bwd(s["proj"], gains, cos2, sin2, grads, "qk_prep_bwd")
    dproj = jnp.concatenate([dqkv] + list(dgls), axis=1)
    dh = mm_x_wcolT(dproj, w["w_in"], "d_h", tm_pref=512)
    dw_in = mm_aT_d_wcol(s["h"], dproj, "dw_in")
    dx_in, dg1 = rmsnorm_bwd(s["x"], p["norm1_g"], dh, dx_mid, "norm1_bwd")

    dws = [dw_in] + dw_br + [dw_o.reshape(N_CHIPS, d // N_CHIPS, d), dw_gu,
                             dw_down.reshape(N_CHIPS, dw_down.shape[0] // N_CHIPS, d)]
    small = {"norm1_g": dg1[0], "qk_norm_g": dgains[:6], "sink_a": dsink[0, :, 0],
             "rpb_c": drpb, "norm2_g": dg2[0]}
    return dx_in, dws, small


SMALL_NAMES = ("norm1_g", "qk_norm_g", "sink_a", "rpb_c", "norm2_g")


def _pack_small(parts, extra=None):
    flat = [parts[k].reshape(-1) for k in SMALL_NAMES]
    flat.append(jnp.zeros((1,), F32) if extra is None else extra.reshape(1))
    v = jnp.concatenate(flat)
    rows = -(-v.shape[0] // (8 * HEAD)) * 8
    return jnp.pad(v, (0, rows * HEAD - v.shape[0])).reshape(rows, HEAD)


def _unpack_small(pack, like):
    v = pack.reshape(-1)
    out, off = {}, 0
    for k in SMALL_NAMES:
        size = math.prod(like[k].shape)
        out[k] = v[off:off + size].reshape(like[k].shape)
        off += size
    return out, v[off]


def kernel(x, norm1_g, w_in, qk_norm_g, sink_a, rpb_c, w_br_a, w_br_b, w_br_c, w_o, norm2_g, w_gate_up, w_down, loss_target, m_norm1_g, m_w_in, m_qk_norm_g, m_sink_a, m_rpb_c, m_w_br_a, m_w_br_b, m_w_br_c, m_w_o, m_norm2_g, m_w_gate_up, m_w_down, v_norm1_g, v_w_in, v_qk_norm_g, v_sink_a, v_rpb_c, v_w_br_a, v_w_br_b, v_w_br_c, v_w_o, v_norm2_g, v_w_gate_up, v_w_down):
    big = dict(w_in=w_in, w_br_a=w_br_a, w_br_b=w_br_b, w_br_c=w_br_c, w_o=w_o, w_gate_up=w_gate_up, w_down=w_down)
    big_m = dict(w_in=m_w_in, w_br_a=m_w_br_a, w_br_b=m_w_br_b, w_br_c=m_w_br_c, w_o=m_w_o, w_gate_up=m_w_gate_up, w_down=m_w_down)
    big_v = dict(w_in=v_w_in, w_br_a=v_w_br_a, w_br_b=v_w_br_b, w_br_c=v_w_br_c, w_o=v_w_o, w_gate_up=v_w_gate_up, w_down=v_w_down)
    small = dict(norm1_g=norm1_g, qk_norm_g=qk_norm_g, sink_a=sink_a, rpb_c=rpb_c, norm2_g=norm2_g)
    small_m = dict(norm1_g=m_norm1_g, qk_norm_g=m_qk_norm_g, sink_a=m_sink_a, rpb_c=m_rpb_c, norm2_g=m_norm2_g)
    small_v = dict(norm1_g=v_norm1_g, qk_norm_g=v_qk_norm_g, sink_a=v_sink_a, rpb_c=v_rpb_c, norm2_g=v_norm2_g)
    n_layers = w_in.shape[0]
    n, d = x.shape[1], x.shape[2]
    c_idx = lax.axis_index("c").astype(jnp.int32).reshape(1)
    chip_idx = (2 * lax.axis_index("x") + lax.axis_index("y")).astype(jnp.int32).reshape(1)
    cos2, sin2 = _rope_tables(n)

    def gathered(started, after):
        send_sems, recv_sems, bufs, _ = started
        got = pair_forward(gather_wait(bufs, send_sems, recv_sems, after, "gather_wait"), "pair_forward")
        w = dict(zip(W_NAMES, got))
        w["w_o"] = w["w_o"].reshape(d, d)
        w["w_down"] = w["w_down"].reshape(-1, d)
        return w

    def start_gather(l, after):
        return gather_start([cast_place(chip_idx, big[k][l], "cast_" + k) for k in W_NAMES], after, "gather_start")

    started = start_gather(0, chip_idx)
    weights = [gathered(started, started[3])]

    xs, saved = x[0], []
    for l in range(n_layers):
        p = {k: small[k][l] for k in SMALL_NAMES}
        if l + 1 < n_layers:
            started = start_gather(l + 1, weights[l]["w_in"])
            p["norm1_g"] = p["norm1_g"] + started[3][0, 0]
        xs, s = layer_fwd(xs, p, weights[l], cos2, sin2)
        saved.append(s)
        if l + 1 < n_layers:
            weights.append(gathered(started, xs))
    loss_tile, dx = loss_head(xs, loss_target[0], "loss_head")

    halves = [None] * len(W_NAMES)
    small_g = [None] * n_layers

    def finish_exchange(pending, after, halves):
        l, (send_sems, recv_sems, sums, lands, _) = pending
        sums, from_chips = exchange_wait(sums, lands, send_sems, recv_sems, after, "exchange_wait")
        return [add_chips(chip_idx, sm, r, st, l, n_layers, "add_chips_" + k)
                for sm, r, st, k in zip(sums, from_chips, halves, W_NAMES)]

    pending = None
    for l in reversed(range(n_layers)):
        p = {k: small[k][l] for k in SMALL_NAMES}
        dx, dws, small_g[l] = layer_bwd(dx, saved[l], p, weights[l], cos2, sin2)
        if pending is not None:
            halves = finish_exchange(pending, dx, halves)
        from_sibling = pair_exchange(dws, "pair_exchange")
        sums = [add_halves(c_idx, g, o, "add_halves_" + k) for g, o, k in zip(dws, from_sibling, W_NAMES)]
        pending = (l, exchange_start(sums, "exchange_start"))
        dx = dx + pending[1][4][0, 0]
    halves = finish_exchange(pending, dx, halves)
    others = pair_share(halves, "pair_share")

    mine = {k: jnp.stack([small_g[l][k] for l in range(n_layers)]) for k in SMALL_NAMES}
    total = small_allreduce(_pack_small(mine, loss_tile[0, 0]), "small_allreduce")
    grad_small, loss = _unpack_small(total, small)

    outs = {}
    for k, mine_half, other_half in zip(W_NAMES, halves, others):
        outs[k] = adamw_big(c_idx, big[k], big_m[k], big_v[k], mine_half, other_half, "adamw_" + k)
    res = adamw(_pack_small(small), _pack_small(grad_small), _pack_small(small_m), _pack_small(small_v), "adamw_small")
    unp = [_unpack_small(t, small)[0] for t in res]
    for k in SMALL_NAMES:
        outs[k] = (grad_small[k],) + tuple(u[k] for u in unp)

    order = ("norm1_g", "w_in", "qk_norm_g", "sink_a", "rpb_c", "w_br_a", "w_br_b", "w_br_c", "w_o", "norm2_g",
             "w_gate_up", "w_down")
    return (loss, dx[None]) + tuple(outs[k][i] for i in range(4) for k in order)
```

```python
import functools
import math

import jax
import jax.numpy as jnp
from jax import lax
from jax.experimental import pallas as pl
from jax.experimental.pallas import tpu as pltpu

F32 = jnp.float32
CDT = jnp.bfloat16
XDT = jnp.bfloat16

HEAD = 128
NORM_EPS = 1e-6
ROPE_THETA = 10000.0
A_Q_HEADS, A_KV_HEADS, A_GROUP, A_RADIUS = 8, 2, 4, 128
B_DILS = (1, 4, 16)
B_RADIUS = 64
B_HG = 4
C_HEADS, GRID_W, C_WIN_ROWS, C_WIN_COLS = 8, 64, 8, 16
QKV_W = 9216
COL = dict(qa=0, ka=1024, va=1280, qb=1536, kb=3072, vb=4608, qc=6144, kc=7168, vc=8192)
NEG = -1e30
SCALE = HEAD ** -0.5
N_CHIPS = 4

ADAM_LR, ADAM_B1, ADAM_B2, ADAM_EPS, ADAM_WD, ADAM_STEP = 0.001, 0.9, 0.999, 1e-08, 0.01, 10

VMEM_LIMIT = 56 * 1024 * 1024
MESH = pl.DeviceIdType.MESH


def _pallas(body, **kw):
    return pl.pallas_call(body, **kw)


def _params(sem=None, **kw):
    if sem is not None:
        kw["dimension_semantics"] = sem
    return pltpu.CompilerParams(vmem_limit_bytes=VMEM_LIMIT, **kw)


def _tile(dim, pref, mult=128):
    best = None
    for t in range(mult, min(dim, pref) + 1, mult):
        if dim % t == 0:
            best = t
    return dim if best is None else best


def _sds(shape, dtype):
    return jax.ShapeDtypeStruct(tuple(shape), dtype)


_DIMS = {"nn": (((1,), (0,)), ((), ())), "nt": (((1,), (1,)), ((), ())), "tn": (((0,), (0,)), ((), ()))}


def _matmul(a, b, *, mode, grid, a_spec, b_spec, o_spec, out_shape, acc_shape, name, res=None, res_spec=None):
    nk = grid[2]
    has_res = res is not None

    def body(*refs):
        if has_res:
            a_ref, b_ref, r_ref, o_ref = refs[:4]
            rest = refs[4:]
        else:
            a_ref, b_ref, o_ref = refs[:3]
            r_ref = None
            rest = refs[3:]
        p = lax.dot_general(a_ref[...].astype(CDT), b_ref[...].astype(CDT), _DIMS[mode],
                            preferred_element_type=F32)

        def finish(acc):
            if has_res:
                acc = acc + r_ref[...].astype(F32)
            o_ref[...] = acc.astype(o_ref.dtype)

        if nk == 1:
            finish(p)
        else:
            acc_ref = rest[0]
            k = pl.program_id(2)

            @pl.when(k == 0)
            def _():
                acc_ref[...] = p

            @pl.when(k > 0)
            def _():
                acc_ref[...] += p

            @pl.when(k == nk - 1)
            def _():
                finish(acc_ref[...])

    in_specs = [a_spec, b_spec] + ([res_spec] if has_res else [])
    args = (a, b) + ((res,) if has_res else ())
    scratch = [] if nk == 1 else [pltpu.VMEM(acc_shape, F32)]
    return _pallas(body, name=name, grid=grid, in_specs=in_specs, out_specs=o_spec, out_shape=out_shape,
                   scratch_shapes=scratch, compiler_params=_params(("parallel", "parallel", "arbitrary")))(*args)


def mm_x_wcol(a, wg, out_dtype, name, tm_pref=1024, tn_pref=1024, stacked_out=1):
    m, k = a.shape
    ns = wg.shape[2]
    tm, tn = _tile(m, tm_pref, 8), _tile(ns, tn_pref)
    nj = ns // tn
    grid = (m // tm, N_CHIPS * nj, 1)
    a_spec = pl.BlockSpec((tm, k), lambda i, j, kk: (i, 0))
    b_spec = pl.BlockSpec((None, k, tn), lambda i, j, kk: (j // nj, 0, j % nj))
    if stacked_out == 1:
        o_spec = pl.BlockSpec((tm, tn), lambda i, j, kk: (i, j))
        out_shape = _sds((m, N_CHIPS * ns), out_dtype)
    else:
        per = N_CHIPS * nj // stacked_out
        o_spec = pl.BlockSpec((None, tm, tn), lambda i, j, kk: (j // per, i, j % per))
        out_shape = _sds((stacked_out, m, N_CHIPS * ns // stacked_out), out_dtype)
    return _matmul(a, wg, mode="nn", grid=grid, a_spec=a_spec, b_spec=b_spec, o_spec=o_spec,
                   out_shape=out_shape, acc_shape=(tm, tn), name=name)


def mm_x_wcolT(d, wg, name, res=None, tm_pref=1024, tn_pref=512, tk_pref=4096, stacked_in=1):
    kdim, ns = wg.shape[1], wg.shape[2]
    m = d.shape[-2]
    tm, tn, tk = _tile(m, tm_pref, 8), _tile(kdim, tn_pref), _tile(ns, tk_pref)
    nkk = ns // tk
    grid = (m // tm, kdim // tn, N_CHIPS * nkk)
    if stacked_in == 1:
        a_spec = pl.BlockSpec((tm, tk), lambda i, j, kk: (i, kk))
    else:
        per = N_CHIPS * nkk // stacked_in
        a_spec = pl.BlockSpec((None, tm, tk), lambda i, j, kk: (kk // per, i, kk % per))
    b_spec = pl.BlockSpec((None, tn, tk), lambda i, j, kk: (kk // nkk, j, kk % nkk))
    o_spec = pl.BlockSpec((tm, tn), lambda i, j, kk: (i, j))
    return _matmul(d, wg, mode="nt", grid=grid, a_spec=a_spec, b_spec=b_spec, o_spec=o_spec,
                   out_shape=_sds((m, kdim), F32), acc_shape=(tm, tn), name=name,
                   res=res, res_spec=None if res is None else o_spec)


def mm_aT_d_wcol(a, d, name, tm_pref=512, tk_pref=4096, tn_pref=1024, stacked_in=1):
    m, kdim = a.shape
    ntot = d.shape[-1] * stacked_in
    ns = ntot // N_CHIPS
    tm, tkm, tn = _tile(kdim, tm_pref), _tile(m, tk_pref, 8), _tile(ns, tn_pref)
    nj = ns // tn
    grid = (kdim // tm, N_CHIPS * nj, m // tkm)
    a_spec = pl.BlockSpec((tkm, tm), lambda i, j, kk: (kk, i))
    if stacked_in == 1:
        b_spec = pl.BlockSpec((tkm, tn), lambda i, j, kk: (kk, j))
    else:
        per = N_CHIPS * nj // stacked_in
        b_spec = pl.BlockSpec((None, tkm, tn), lambda i, j, kk: (j // per, kk, j % per))
    o_spec = pl.BlockSpec((None, tm, tn), lambda i, j, kk: (j // nj, i, j % nj))
    return _matmul(a, d, mode="tn", grid=grid, a_spec=a_spec, b_spec=b_spec, o_spec=o_spec,
                   out_shape=_sds((N_CHIPS, kdim, ns), F32), acc_shape=(tm, tn), name=name)


def mm_x_w(a, w, name, res=None, out_dtype=F32, tm_pref=1024, tn_pref=1024, tk_pref=2048):
    m, k = a.shape
    n = w.shape[1]
    tm, tn, tk = _tile(m, tm_pref, 8), _tile(n, tn_pref), _tile(k, tk_pref)
    grid = (m // tm, n // tn, k // tk)
    o_spec = pl.BlockSpec((tm, tn), lambda i, j, kk: (i, j))
    return _matmul(a, w, mode="nn", grid=grid,
                   a_spec=pl.BlockSpec((tm, tk), lambda i, j, kk: (i, kk)),
                   b_spec=pl.BlockSpec((tk, tn), lambda i, j, kk: (kk, j)),
                   o_spec=o_spec, out_shape=_sds((m, n), out_dtype), acc_shape=(tm, tn), name=name,
                   res=res, res_spec=None if res is None else o_spec)


def mm_x_wT(d, w, name, out_dtype=F32, tm_pref=1024, tn_pref=1024):
    m, n = d.shape
    k = w.shape[0]
    tm, tn = _tile(m, tm_pref, 8), _tile(k, tn_pref)
    grid = (m // tm, k // tn, 1)
    return _matmul(d, w, mode="nt", grid=grid,
                   a_spec=pl.BlockSpec((tm, n), lambda i, j, kk: (i, 0)),
                   b_spec=pl.BlockSpec((tn, n), lambda i, j, kk: (j, 0)),
                   o_spec=pl.BlockSpec((tm, tn), lambda i, j, kk: (i, j)),
                   out_shape=_sds((m, k), out_dtype), acc_shape=(tm, tn), name=name)


def mm_aT_d(a, d, name, tm_pref=512, tn_pref=512, tk_pref=4096):
    m, k = a.shape
    n = d.shape[1]
    tm, tn, tk = _tile(k, tm_pref), _tile(n, tn_pref), _tile(m, tk_pref, 8)
    grid = (k // tm, n // tn, m // tk)
    return _matmul(a, d, mode="tn", grid=grid,
                   a_spec=pl.BlockSpec((tk, tm), lambda i, j, kk: (kk, i)),
                   b_spec=pl.BlockSpec((tk, tn), lambda i, j, kk: (kk, j)),
                   o_spec=pl.BlockSpec((tm, tn), lambda i, j, kk: (i, j)),
                   out_shape=_sds((k, n), F32), acc_shape=(tm, tn), name=name)


def rmsnorm_fwd(x, g, name):
    n, d = x.shape
    tm = _tile(n, 512, 8)

    def body(x_ref, g_ref, h_ref):
        xv = x_ref[...]
        r = lax.rsqrt(jnp.mean(xv * xv, axis=-1, keepdims=True) + NORM_EPS)
        h_ref[...] = (xv * r * g_ref[...]).astype(h_ref.dtype)

    return _pallas(body, name=name, grid=(n // tm,),
                   in_specs=[pl.BlockSpec((tm, d), lambda i: (i, 0)), pl.BlockSpec((1, d), lambda i: (0, 0))],
                   out_specs=pl.BlockSpec((tm, d), lambda i: (i, 0)), out_shape=_sds((n, d), CDT),
                   compiler_params=_params(("parallel",)))(x, g.reshape(1, d))


def rmsnorm_bwd(x, g, dh, dres, name):
    n, d = x.shape
    tm = _tile(n, 256, 8)

    def body(x_ref, g_ref, dh_ref, dres_ref, dx_ref, dg_ref):
        xv = x_ref[...]
        r = lax.rsqrt(jnp.mean(xv * xv, axis=-1, keepdims=True) + NORM_EPS)
        dhv = dh_ref[...]
        u = dhv * g_ref[...]
        c = jnp.mean(xv * u, axis=-1, keepdims=True)
        dx_ref[...] = dres_ref[...] + r * u - xv * (r * r * r * c)
        part = jnp.broadcast_to(jnp.sum(dhv * xv * r, axis=0, keepdims=True), (8, d))

        @pl.when(pl.program_id(0) == 0)
        def _():
            dg_ref[...] = part

        @pl.when(pl.program_id(0) > 0)
        def _():
            dg_ref[...] += part

    row = pl.BlockSpec((tm, d), lambda i: (i, 0))
    dx, dg = _pallas(body, name=name, grid=(n // tm,),
                     in_specs=[row, pl.BlockSpec((1, d), lambda i: (0, 0)), row, row],
                     out_specs=[row, pl.BlockSpec((8, d), lambda i: (0, 0))],
                     out_shape=[_sds((n, d), F32), _sds((8, d), F32)],
                     compiler_params=_params(("arbitrary",)))(x, g.reshape(1, d), dh, dres)
    return dx, dg


def _norm_rope(xh, g, cos2, sin2):
    r = lax.rsqrt(jnp.mean(xh * xh, axis=-1, keepdims=True) + NORM_EPS)
    y = xh * r * g
    if cos2 is not None:
        y = y * cos2 + pltpu.roll(y, HEAD // 2, 1) * sin2
    return y


def _norm_rope_bwd(xh, g, cos2, sin2, dout):
    if cos2 is not None:
        dy = dout * cos2 + pltpu.roll(dout * sin2, HEAD // 2, 1)
    else:
        dy = dout
    r = lax.rsqrt(jnp.mean(xh * xh, axis=-1, keepdims=True) + NORM_EPS)
    u = dy * g
    c = jnp.mean(xh * u, axis=-1, keepdims=True)
    return r * u - xh * (r * r * r * c), dy * xh * r


_QK_GROUPS = (("qa", COL["qa"], 8, 0, True), ("ka", COL["ka"], 2, 1, True),
              ("qb", COL["qb"], 12, 2, True), ("kb", COL["kb"], 12, 3, True),
              ("qc", COL["qc"], 8, 4, False), ("kc", COL["kc"], 8, 5, False))
_V_GROUPS = (("va", COL["va"], 2), ("vb", COL["vb"], 12), ("vc", COL["vc"], 8))
_PREP_OUT = (("qa", 8), ("ka", 2), ("va", 2)) + tuple((f"{t}b{g}", 4) for t in "qkv" for g in range(3)) + (
    ("qc", 8), ("kc", 8), ("vc", 8))


def _prep_src(name):
    if name[1] == "b":
        base = COL[name[0] + "b"] + int(name[2]) * B_HG * HEAD
        gain = {"q": 2, "k": 3, "v": None}[name[0]]
        return base, gain, name[0] != "v"
    base = COL[name]
    gain = {"qa": 0, "ka": 1, "va": None, "qc": 4, "kc": 5, "vc": None}[name]
    return base, gain, name in ("qa", "ka")


def qk_prep(proj, gains, cos2, sin2, name):
    n = proj.shape[0]
    tm = _tile(n, 256, 8)

    def body(p_ref, g_ref, c_ref, s_ref, *outs):
        cos2v, sin2v = c_ref[...], s_ref[...]
        for (nm, heads), o_ref in zip(_PREP_OUT, outs):
            base, gain, rope = _prep_src(nm)
            for h in range(heads):
                xh = p_ref[:, base + h * HEAD: base + (h + 1) * HEAD]
                if gain is None:
                    y = xh
                else:
                    y = _norm_rope(xh, g_ref[gain:gain + 1, :], cos2v if rope else None, sin2v if rope else None)
                o_ref[:, h * HEAD:(h + 1) * HEAD] = y.astype(o_ref.dtype)

    tab = pl.BlockSpec((tm, HEAD), lambda i: (i, 0))
    outs = _pallas(body, name=name, grid=(n // tm,),
                   in_specs=[pl.BlockSpec((tm, QKV_W), lambda i: (i, 0)), pl.BlockSpec((8, HEAD), lambda i: (0, 0)), tab, tab],
                   out_specs=[pl.BlockSpec((tm, h * HEAD), lambda i: (i, 0)) for _, h in _PREP_OUT],
                   out_shape=[_sds((n, h * HEAD), CDT) for _, h in _PREP_OUT],
                   compiler_params=_params(("parallel",)))(proj, gains, cos2, sin2)
    return dict(zip([nm for nm, _ in _PREP_OUT], outs))


def qk_prep_bwd(proj, gains, cos2, sin2, grads, name):
    n = proj.shape[0]
    tm = _tile(n, 128, 8)
    names = [nm for nm, _ in _PREP_OUT]

    def body(p_ref, g_ref, c_ref, s_ref, *refs):
        g_refs, dp_ref, dg_ref = refs[:len(names)], refs[len(names)], refs[len(names) + 1]
        cos2v, sin2v = c_ref[...], s_ref[...]
        dg = [jnp.zeros((tm, HEAD), F32) for _ in range(6)]
        for (nm, heads), gr in zip(_PREP_OUT, g_refs):
            base, gain, rope = _prep_src(nm)
            for h in range(heads):
                sl = slice(base + h * HEAD, base + (h + 1) * HEAD)
                dout = gr[:, h * HEAD:(h + 1) * HEAD]
                if gain is None:
                    dx = dout
                else:
                    dx, dgr = _norm_rope_bwd(p_ref[:, sl], g_ref[gain:gain + 1, :], cos2v if rope else None,
                                             sin2v if rope else None, dout)
                    dg[gain] = dg[gain] + dgr
                dp_ref[:, sl] = dx.astype(dp_ref.dtype)
        part = jnp.concatenate([jnp.sum(t, axis=0, keepdims=True) for t in dg] + [jnp.zeros((2, HEAD), F32)], axis=0)

        @pl.when(pl.program_id(0) == 0)
        def _():
            dg_ref[...] = part

        @pl.when(pl.program_id(0) > 0)
        def _():
            dg_ref[...] += part

    tab = pl.BlockSpec((tm, HEAD), lambda i: (i, 0))
    dp, dg = _pallas(body, name=name, grid=(n // tm,),
                     in_specs=[pl.BlockSpec((tm, QKV_W), lambda i: (i, 0)), pl.BlockSpec((8, HEAD), lambda i: (0, 0)), tab, tab]
                     + [pl.BlockSpec((tm, h * HEAD), lambda i: (i, 0)) for _, h in _PREP_OUT],
                     out_specs=[pl.BlockSpec((tm, QKV_W), lambda i: (i, 0)), pl.BlockSpec((8, HEAD), lambda i: (0, 0))],
                     out_shape=[_sds((n, QKV_W), CDT), _sds((8, HEAD), F32)],
                     compiler_params=_params(("arbitrary",)))(proj, gains, cos2, sin2, *[grads[k] for k in names])
    return dp, dg


def _band_geometry(m, bq_pref, radius):
    bq = min(bq_pref, m)
    return bq, min(bq + 2 * radius, m)


def _band_window(i, bq, radius, m, w):
    start = pl.multiple_of(jnp.clip(i * bq - radius, 0, m - w), 64)
    qpos = i * bq + lax.broadcasted_iota(jnp.int32, (bq, w), 0)
    kpos = start + lax.broadcasted_iota(jnp.int32, (bq, w), 1)
    return start, jnp.abs(kpos - qpos) <= radius


def band_attn_fwd(q, k, v, sink, *, seqs, G, nh, radius, name, bq_pref=128):
    m = q.shape[0]
    bq, w = _band_geometry(m, bq_pref, radius)
    has_sink = sink is not None

    def body(*refs):
        if has_sink:
            sink_ref, q_ref, k_ref, v_ref, o_ref, lse_ref = refs
        else:
            q_ref, k_ref, v_ref, o_ref, lse_ref = refs
        s_id, i = pl.program_id(0), pl.program_id(1)
        start, valid = _band_window(i, bq, radius, m, w)
        units = [(h, g) for h in range(nh) for g in range(G)]
        sls = [slice((h * G + g) * HEAD, (h * G + g + 1) * HEAD) for h, g in units]
        k_ts = [k_ref[pl.ds(start, w), h * HEAD:(h + 1) * HEAD] for h in range(nh)]
        v_ts = [v_ref[pl.ds(start, w), h * HEAD:(h + 1) * HEAD] for h in range(nh)]
        q_ts = [q_ref[:, sl] for sl in sls]
        sks = [sink_ref[0, (s_id * nh + h) * G + g] for h, g in units] if has_sink else None
        ss = [jnp.where(valid, lax.dot_general(q_t, k_ts[h], _DIMS["nt"], preferred_element_type=F32) * SCALE, NEG)
              for q_t, (h, g) in zip(q_ts, units)]
        mxs = [jnp.max(s, axis=-1, keepdims=True) for s in ss]
        if has_sink:
            mxs = [jnp.maximum(mx, sk) for mx, sk in zip(mxs, sks)]
        ps = [jnp.exp(s - mx) for s, mx in zip(ss, mxs)]
        dens = [jnp.sum(p, axis=-1, keepdims=True) for p in ps]
        if has_sink:
            dens = [den + jnp.exp(sk - mx) for den, sk, mx in zip(dens, sks, mxs)]
        outs = [jnp.dot((p / den).astype(CDT), v_ts[h], preferred_element_type=F32)
                for p, den, (h, g) in zip(ps, dens, units)]
        for sl, o, mx, den in zip(sls, outs, mxs, dens):
            o_ref[:, sl] = o
            lse_ref[:, sl] = jnp.broadcast_to(mx + jnp.log(den), (bq, HEAD))

    qspec = pl.BlockSpec((bq, nh * G * HEAD), lambda s, i: (i, s))
    kspec = pl.BlockSpec((m, nh * HEAD), lambda s, i: (0, s))
    in_specs = ([pl.BlockSpec(memory_space=pltpu.SMEM)] if has_sink else []) + [qspec, kspec, kspec]
    args = ((sink,) if has_sink else ()) + (q, k, v)
    return _pallas(body, name=name, grid=(seqs // nh, m // bq), in_specs=in_specs, out_specs=[qspec, qspec],
                   out_shape=[_sds(q.shape, F32), _sds(q.shape, F32)],
                   compiler_params=_params(("parallel", "arbitrary")))(*args)


def band_attn_bwd(q, k, v, sink, o, lse, do, dlse, *, seqs, G, nh, radius, name, bq_pref=128):
    m = q.shape[0]
    bq, w = _band_geometry(m, bq_pref, radius)
    has_sink, has_dlse = sink is not None, dlse is not None
    assert nh * G <= 8

    def body(*refs):
        refs = list(refs)
        sink_ref = refs.pop(0) if has_sink else None
        q_ref, k_ref, v_ref, o_ref, lse_ref, do_ref = refs[:6]
        refs = refs[6:]
        dlse_ref = refs.pop(0) if has_dlse else None
        dq_ref, dk_ref, dv_ref = refs[:3]
        dsink_ref = refs[3] if has_sink else None
        s_id, i = pl.program_id(0), pl.program_id(1)

        @pl.when(i == 0)
        def _():
            dk_ref[...] = jnp.zeros_like(dk_ref)
            dv_ref[...] = jnp.zeros_like(dv_ref)
            if has_sink:
                dsink_ref[...] = jnp.zeros_like(dsink_ref)

        start, valid = _band_window(i, bq, radius, m, w)
        units = [(h, g) for h in range(nh) for g in range(G)]
        sls = [slice((h * G + g) * HEAD, (h * G + g + 1) * HEAD) for h, g in units]
        kss = [slice(h * HEAD, (h + 1) * HEAD) for h in range(nh)]
        k_ts = [k_ref[pl.ds(start, w), ks] for ks in kss]
        v_ts = [v_ref[pl.ds(start, w), ks] for ks in kss]
        q_ts = [q_ref[:, sl] for sl in sls]
        lse_ts = [lse_ref[:, sl][:, :1] for sl in sls]
        do_ts = [do_ref[:, sl] for sl in sls]
        deltas = [jnp.sum(do_t * o_ref[:, sl], axis=-1, keepdims=True) for do_t, sl in zip(do_ts, sls)]
        dlse_ts = [dlse_ref[:, sl][:, :1] for sl in sls] if has_dlse else None
        dk_old = [dk_ref[pl.ds(start, w), ks] for ks in kss]
        dv_old = [dv_ref[pl.ds(start, w), ks] for ks in kss]
        dsink_old = dsink_ref[...] if has_sink else None

        ps = [jnp.exp(jnp.where(valid, lax.dot_general(q_t, k_ts[h], _DIMS["nt"], preferred_element_type=F32) * SCALE, NEG)
                      - lse_t) for q_t, lse_t, (h, g) in zip(q_ts, lse_ts, units)]
        do_cs = [do_t.astype(CDT) for do_t in do_ts]
        dps = [lax.dot_general(do_c, v_ts[h], _DIMS["nt"], preferred_element_type=F32) for do_c, (h, g) in zip(do_cs, units)]
        ts = [dp - delta for dp, delta in zip(dps, deltas)]
        if has_dlse:
            ts = [t + dl for t, dl in zip(ts, dlse_ts)]
        dss = [((p * t) * SCALE).astype(CDT) for p, t in zip(ps, ts)]
        dqs = [jnp.dot(ds, k_ts[h], preferred_element_type=F32) for ds, (h, g) in zip(dss, units)]
        dvs = [lax.dot_general(p.astype(CDT), do_c, _DIMS["tn"], preferred_element_type=F32) for p, do_c in zip(ps, do_cs)]
        dks = [lax.dot_general(ds, q_t, _DIMS["tn"], preferred_element_type=F32) for ds, q_t in zip(dss, q_ts)]
        dk_new = [dk_old[h] + sum(dks[h * G + g] for g in range(G)) for h in range(nh)]
        dv_new = [dv_old[h] + sum(dvs[h * G + g] for g in range(G)) for h in range(nh)]
        if has_sink:
            rows = []
            for (h, g), lse_t, delta in zip(units, lse_ts, deltas):
                sk = sink_ref[0, (s_id * nh + h) * G + g]
                rows.append(jnp.broadcast_to(-jnp.sum(jnp.exp(sk - lse_t) * delta, axis=0, keepdims=True), (1, HEAD)))
            rows += [jnp.zeros((1, HEAD), F32)] * (8 - len(rows))
            dsink_new = dsink_old + jnp.concatenate(rows, axis=0)

        for sl, dq in zip(sls, dqs):
            dq_ref[:, sl] = dq
        for h, ks in enumerate(kss):
            dk_ref[pl.ds(start, w), ks] = dk_new[h]
            dv_ref[pl.ds(start, w), ks] = dv_new[h]
        if has_sink:
            dsink_ref[...] = dsink_new

    qspec = pl.BlockSpec((bq, nh * G * HEAD), lambda s, i: (i, s))
    kspec = pl.BlockSpec((m, nh * HEAD), lambda s, i: (0, s))
    in_specs = ([pl.BlockSpec(memory_space=pltpu.SMEM)] if has_sink else []) + [qspec, kspec, kspec, qspec, qspec, qspec]
    in_specs += [qspec] if has_dlse else []
    args = ((sink,) if has_sink else ()) + (q, k, v, o, lse, do) + ((dlse,) if has_dlse else ())
    out_specs = [qspec, kspec, kspec]
    out_shape = [_sds(q.shape, F32), _sds(k.shape, F32), _sds(k.shape, F32)]
    if has_sink:
        out_specs.append(pl.BlockSpec((None, 8, HEAD), lambda s, i: (s, 0, 0)))
        out_shape.append(_sds((seqs // nh, 8, HEAD), F32))
    return _pallas(body, name=name, grid=(seqs // nh, m // bq), in_specs=in_specs, out_specs=out_specs,
                   out_shape=out_shape, compiler_params=_params(("parallel", "arbitrary")))(*args)


def _group_weights(lses):
    mx = jnp.maximum(jnp.maximum(lses[0], lses[1]), lses[2])
    e = [jnp.exp(l - mx) for l in lses]
    tot = e[0] + e[1] + e[2]
    return [t / tot for t in e]


def b_combine_fwd(os_, lses, name):
    n, wd = os_[0].shape
    tm = _tile(n, 512, 8)

    def body(o0, o1, o2, l0, l1, l2, out_ref):
        wts = _group_weights([l0[...], l1[...], l2[...]])
        out_ref[...] = wts[0] * o0[...] + wts[1] * o1[...] + wts[2] * o2[...]

    row = pl.BlockSpec((tm, wd), lambda i: (i, 0))
    return _pallas(body, name=name, grid=(n // tm,), in_specs=[row] * 6, out_specs=row,
                   out_shape=_sds((n, wd), F32), compiler_params=_params(("parallel",)))(*os_, *lses)


def b_combine_bwd(dout, os_, lses, name):
    n, wd = dout.shape
    tm = _tile(n, 256, 8)

    def body(d_ref, o0, o1, o2, l0, l1, l2, do0, do1, do2, dl0, dl1, dl2):
        dv = d_ref[...]
        wts = _group_weights([l0[...], l1[...], l2[...]])
        dws = []
        for o_ref in (o0, o1, o2):
            prod = dv * o_ref[...]
            cols = []
            for h in range(wd // HEAD):
                sseg = jnp.sum(prod[:, h * HEAD:(h + 1) * HEAD], axis=-1, keepdims=True)
                cols.append(jnp.broadcast_to(sseg, (tm, HEAD)))
            dws.append(jnp.concatenate(cols, axis=-1))
        mean = wts[0] * dws[0] + wts[1] * dws[1] + wts[2] * dws[2]
        for wt, dw, do_ref, dl_ref in zip(wts, dws, (do0, do1, do2), (dl0, dl1, dl2)):
            do_ref[...] = wt * dv
            dl_ref[...] = wt * (dw - mean)

    row = pl.BlockSpec((tm, wd), lambda i: (i, 0))
    outs = _pallas(body, name=name, grid=(n // tm,), in_specs=[row] * 7, out_specs=[row] * 6,
                   out_shape=[_sds((n, wd), F32)] * 6, compiler_params=_params(("parallel",)))(dout, *os_, *lses)
    return outs[:3], outs[3:]


def _c_rows(n):
    rows = n // GRID_W
    return rows, min(C_WIN_ROWS, rows)


def _c_row_start(r, rows, wr):
    return jnp.clip(r - wr // 2, 0, rows - wr)


def _c_bias_index(r, rows, wr):
    return _c_row_start(r, rows, wr) - r + (C_WIN_ROWS - 1)


def _col_shift_select(tile, cq, inverse):
    lanes = tile.shape[1]
    for b in range(6):
        amt = (lanes - (1 << b)) if inverse else (1 << b)
        tile = jnp.where(((cq >> b) & 1) == 1, pltpu.roll(tile, amt, 1), tile)
    return tile


def rpb_expand(rwin, name):
    lanes = rwin.shape[-1]

    def body(r_ref, b_ref):
        cq = lax.broadcasted_iota(jnp.int32, (GRID_W, lanes), 0)
        ck = lax.broadcasted_iota(jnp.int32, (GRID_W, lanes), 1) % GRID_W
        tile = jnp.broadcast_to(r_ref[...], (GRID_W, lanes))
        tile = pltpu.roll(tile, lanes - (C_WIN_COLS - 1), 1)
        tile = _col_shift_select(tile, cq, False)
        cs = jnp.clip(cq - C_WIN_COLS // 2, 0, GRID_W - C_WIN_COLS)
        ok = (ck >= cs) & (ck < cs + C_WIN_COLS)
        b_ref[...] = jnp.where(ok, tile, NEG)

    return _pallas(body, name=name, grid=(C_HEADS, C_WIN_ROWS),
                   in_specs=[pl.BlockSpec((None, None, 1, lanes), lambda h, i: (h, i, 0, 0))],
                   out_specs=pl.BlockSpec((None, None, GRID_W, lanes), lambda h, i: (h, i, 0, 0)),
                   out_shape=_sds((C_HEADS, C_WIN_ROWS, GRID_W, lanes), F32),
                   compiler_params=_params(("parallel", "parallel")))(rwin)


def rpb_reduce(dbias, name):
    lanes = dbias.shape[-1]
    wr = lanes // GRID_W

    def body(d_ref, o_ref, acc_ref):
        i0 = pl.program_id(1)
        cq = lax.broadcasted_iota(jnp.int32, (GRID_W, lanes), 0)
        tile = _col_shift_select(d_ref[...], cq, True)
        tile = pltpu.roll(tile, C_WIN_COLS - 1, 1)
        vec = jnp.sum(tile, axis=0, keepdims=True)

        @pl.when(i0 == 0)
        def _():
            acc_ref[...] = jnp.zeros_like(acc_ref)

        for w in range(wr):
            acc_ref[pl.ds(i0 + w, 1), :] += vec[:, w * GRID_W:(w + 1) * GRID_W]

        @pl.when(i0 == C_WIN_ROWS - 1)
        def _():
            o_ref[...] = acc_ref[...]

    return _pallas(body, name=name, grid=(C_HEADS, C_WIN_ROWS),
                   in_specs=[pl.BlockSpec((None, None, GRID_W, lanes), lambda h, i: (h, i, 0, 0))],
                   out_specs=pl.BlockSpec((None, 16, GRID_W), lambda h, i: (h, 0, 0)),
                   out_shape=_sds((C_HEADS, 16, GRID_W), F32),
                   scratch_shapes=[pltpu.VMEM((16, GRID_W), F32)],
                   compiler_params=_params(("parallel", "arbitrary")))(dbias)


def _store_or_add(ref, val, first):
    @pl.when(first)
    def _():
        ref[...] = val

    @pl.when(jnp.logical_not(first))
    def _():
        ref[...] += val


def c_attn_fwd(q, k, v, bias, name, nh=4):
    n = q.shape[0]
    rows, wr = _c_rows(n)
    wk = wr * GRID_W

    def body(q_ref, k_ref, v_ref, b_ref, o_ref, lse_ref):
        r = pl.program_id(1)
        start = pl.multiple_of(_c_row_start(r, rows, wr) * GRID_W, GRID_W)
        sls = [slice(h * HEAD, (h + 1) * HEAD) for h in range(nh)]
        ss = [lax.dot_general(q_ref[:, sl], k_ref[pl.ds(start, wk), sl], _DIMS["nt"], preferred_element_type=F32)
              * SCALE + b_ref[h] for h, sl in enumerate(sls)]
        mxs = [jnp.max(s, axis=-1, keepdims=True) for s in ss]
        ps = [jnp.exp(s - mx) for s, mx in zip(ss, mxs)]
        dens = [jnp.sum(p, axis=-1, keepdims=True) for p in ps]
        outs = [jnp.dot((p / den).astype(CDT), v_ref[pl.ds(start, wk), sl], preferred_element_type=F32)
                for p, den, sl in zip(ps, dens, sls)]
        for sl, o, mx, den in zip(sls, outs, mxs, dens):
            o_ref[:, sl] = o
            lse_ref[:, sl] = jnp.broadcast_to(mx + jnp.log(den), (GRID_W, HEAD))

    qspec = pl.BlockSpec((GRID_W, nh * HEAD), lambda h, r: (r, h))
    kspec = pl.BlockSpec((n, nh * HEAD), lambda h, r: (0, h))
    bspec = pl.BlockSpec((nh, None, GRID_W, wk), lambda h, r: (h, _c_bias_index(r, rows, wr), 0, 0))
    return _pallas(body, name=name, grid=(C_HEADS // nh, rows), in_specs=[qspec, kspec, kspec, bspec],
                   out_specs=[qspec, qspec], out_shape=[_sds(q.shape, F32), _sds(q.shape, F32)],
                   compiler_params=_params(("parallel", "arbitrary")))(q, k, v, bias)


def c_attn_bwd(q, k, v, bias, o, lse, do, name, nh=2):
    n = q.shape[0]
    rows, wr = _c_rows(n)
    wk = wr * GRID_W

    def body(q_ref, k_ref, v_ref, b_ref, o_ref, lse_ref, do_ref, dq_ref, dk_ref, dv_ref, db_ref):
        r = pl.program_id(1)
        rs = _c_row_start(r, rows, wr)
        start = pl.multiple_of(rs * GRID_W, GRID_W)

        @pl.when(r == 0)
        def _():
            dk_ref[...] = jnp.zeros_like(dk_ref)
            dv_ref[...] = jnp.zeros_like(dv_ref)

        prev = _c_row_start(jnp.maximum(r - 1, 0), rows, wr) - jnp.maximum(r - 1, 0)
        first = (r == 0) | (prev != rs - r)
        sls = [slice(h * HEAD, (h + 1) * HEAD) for h in range(nh)]
        k_ts = [k_ref[pl.ds(start, wk), sl] for sl in sls]
        v_ts = [v_ref[pl.ds(start, wk), sl] for sl in sls]
        q_ts = [q_ref[:, sl] for sl in sls]
        lse_ts = [lse_ref[:, sl][:, :1] for sl in sls]
        do_ts = [do_ref[:, sl] for sl in sls]
        deltas = [jnp.sum(do_t * o_ref[:, sl], axis=-1, keepdims=True) for do_t, sl in zip(do_ts, sls)]
        biases = [b_ref[h] for h in range(nh)]
        dk_old = [dk_ref[pl.ds(start, wk), sl] for sl in sls]
        dv_old = [dv_ref[pl.ds(start, wk), sl] for sl in sls]

        ps = [jnp.exp(lax.dot_general(q_t, k_t, _DIMS["nt"], preferred_element_type=F32) * SCALE + b - lse_t)
              for q_t, k_t, b, lse_t in zip(q_ts, k_ts, biases, lse_ts)]
        do_cs = [do_t.astype(CDT) for do_t in do_ts]
        dps = [lax.dot_general(do_c, v_t, _DIMS["nt"], preferred_element_type=F32) for do_c, v_t in zip(do_cs, v_ts)]
        dss = [p * (dp - delta) for p, dp, delta in zip(ps, dps, deltas)]
        ds_cs = [(ds * SCALE).astype(CDT) for ds in dss]
        dqs = [jnp.dot(ds_c, k_t, preferred_element_type=F32) for ds_c, k_t in zip(ds_cs, k_ts)]
        dv_new = [old + lax.dot_general(p.astype(CDT), do_c, _DIMS["tn"], preferred_element_type=F32)
                  for old, p, do_c in zip(dv_old, ps, do_cs)]
        dk_new = [old + lax.dot_general(ds_c, q_t, _DIMS["tn"], preferred_element_type=F32)
                  for old, ds_c, q_t in zip(dk_old, ds_cs, q_ts)]

        for h, sl in enumerate(sls):
            dq_ref[:, sl] = dqs[h]
            dk_ref[pl.ds(start, wk), sl] = dk_new[h]
            dv_ref[pl.ds(start, wk), sl] = dv_new[h]
        for h in range(nh):
            _store_or_add(db_ref.at[h], dss[h], first)

    qspec = pl.BlockSpec((GRID_W, nh * HEAD), lambda h, r: (r, h))
    kspec = pl.BlockSpec((n, nh * HEAD), lambda h, r: (0, h))
    bspec = pl.BlockSpec((nh, None, GRID_W, wk), lambda h, r: (h, _c_bias_index(r, rows, wr), 0, 0))
    return _pallas(body, name=name, grid=(C_HEADS // nh, rows),
                   in_specs=[qspec, kspec, kspec, bspec, qspec, qspec, qspec],
                   out_specs=[qspec, kspec, kspec, bspec],
                   out_shape=[_sds(q.shape, F32), _sds(k.shape, F32), _sds(k.shape, F32), _sds(bias.shape, F32)],
                   compiler_params=_params(("parallel", "arbitrary")))(q, k, v, bias, o, lse, do)


def _sigmoid(z):
    return 1.0 / (1.0 + jnp.exp(-z))


def _gate_specs(n, d):
    tm, tn = _tile(n, 256, 8), _tile(math.gcd(d, QKV_W), 1024)
    nj = d // tn
    tile = pl.BlockSpec((tm, tn), lambda i, j: (i, j))
    gl = [pl.BlockSpec((tm, tn), functools.partial(lambda i, j, b: (i, (QKV_W + b * d) // tn + j), b=b)) for b in range(3)]
    return tm, tn, nj, tile, gl


def gate_merge(proj, ys, name):
    n, d = ys[0].shape
    tm, tn, nj, tile, gl = _gate_specs(n, d)

    def body(g0, g1, g2, y0, y1, y2, out_ref):
        acc = _sigmoid(g0[...]) * y0[...] + _sigmoid(g1[...]) * y1[...] + _sigmoid(g2[...]) * y2[...]
        out_ref[...] = acc.astype(out_ref.dtype)

    return _pallas(body, name=name, grid=(n // tm, nj), in_specs=gl + [tile] * 3, out_specs=tile,
                   out_shape=_sds((n, d), CDT), compiler_params=_params(("parallel", "parallel")))(proj, proj, proj, *ys)


def gate_bwd(proj, ys, dmerged, name):
    n, d = dmerged.shape
    tm, tn, nj, tile, gl = _gate_specs(n, d)

    def body(g0, g1, g2, y0, y1, y2, dm_ref, dy0, dy1, dy2, dg0, dg1, dg2):
        dm = dm_ref[...]
        for g_ref, y_ref, dy_ref, dg_ref in ((g0, y0, dy0, dg0), (g1, y1, dy1, dg1), (g2, y2, dy2, dg2)):
            sg = _sigmoid(g_ref[...])
            dy_ref[...] = (dm * sg).astype(dy_ref.dtype)
            dg_ref[...] = (dm * y_ref[...] * sg * (1.0 - sg)).astype(dg_ref.dtype)

    outs = _pallas(body, name=name, grid=(n // tm, nj), in_specs=gl + [tile] * 4, out_specs=[tile] * 6,
                   out_shape=[_sds((n, d), CDT)] * 6,
                   compiler_params=_params(("parallel", "parallel")))(proj, proj, proj, *ys, dmerged)
    return outs[:3], outs[3:]


def swiglu_fwd(gu, name):
    _, n, ff = gu.shape
    tm, tn = _tile(n, 512, 8), _tile(ff, 1024)

    def body(g_ref, u_ref, a_ref):
        gt = g_ref[...]
        a_ref[...] = (gt * _sigmoid(gt) * u_ref[...]).astype(a_ref.dtype)

    return _pallas(body, name=name, grid=(n // tm, ff // tn),
                   in_specs=[pl.BlockSpec((None, tm, tn), lambda i, j: (0, i, j)),
                             pl.BlockSpec((None, tm, tn), lambda i, j: (1, i, j))],
                   out_specs=pl.BlockSpec((tm, tn), lambda i, j: (i, j)), out_shape=_sds((n, ff), CDT),
                   compiler_params=_params(("parallel", "parallel")))(gu, gu)


def swiglu_bwd(gu, dact, name):
    _, n, ff = gu.shape
    tm, tn = _tile(n, 512, 8), _tile(ff, 1024)

    def body(g_ref, u_ref, da_ref, d_ref):
        gt, up, da = g_ref[...], u_ref[...], da_ref[...]
        sg = _sigmoid(gt)
        d_ref[0] = (da * up * (sg + gt * sg * (1.0 - sg))).astype(d_ref.dtype)
        d_ref[1] = (da * gt * sg).astype(d_ref.dtype)

    return _pallas(body, name=name, grid=(n // tm, ff // tn),
                   in_specs=[pl.BlockSpec((None, tm, tn), lambda i, j: (0, i, j)),
                             pl.BlockSpec((None, tm, tn), lambda i, j: (1, i, j)),
                             pl.BlockSpec((tm, tn), lambda i, j: (i, j))],
                   out_specs=pl.BlockSpec((2, tm, tn), lambda i, j: (0, i, j)), out_shape=_sds((2, n, ff), CDT),
                   compiler_params=_params(("parallel", "parallel")))(gu, gu, dact)


def loss_head(y, target, name):
    n, d = y.shape
    tm = _tile(n, 512, 8)
    nsteps = n // tm

    def body(y_ref, t_ref, l_ref, dy_ref, acc_ref):
        i = pl.program_id(0)
        e = y_ref[...] - t_ref[...]
        dy_ref[...] = e * (1.0 / d)
        part = jnp.sum((e * e).reshape(tm // 8, 8, d), axis=0)

        @pl.when(i == 0)
        def _():
            acc_ref[...] = part

        @pl.when(i > 0)
        def _():
            acc_ref[...] += part

        @pl.when(i == nsteps - 1)
        def _():
            tot = jnp.sum(jnp.sum(acc_ref[...], axis=1, keepdims=True), axis=0, keepdims=True) * (0.5 / d)
            l_ref[...] = jnp.broadcast_to(tot, (8, HEAD))

    row = pl.BlockSpec((tm, d), lambda i: (i, 0))
    return _pallas(body, name=name, grid=(nsteps,), in_specs=[row, row],
                   out_specs=[pl.BlockSpec((8, HEAD), lambda i: (0, 0)), row],
                   out_shape=[_sds((8, HEAD), F32), _sds((n, d), F32)],
                   scratch_shapes=[pltpu.VMEM((8, d), F32)],
                   compiler_params=_params(("arbitrary",)))(y, target)


def adamw(w, g, m, v, name):
    r, c = w.shape
    tr = _tile(r, max(8, (1 << 19) // c), 8)
    c1 = 1.0 - ADAM_B1 ** ADAM_STEP
    c2 = 1.0 - ADAM_B2 ** ADAM_STEP

    def body(w_ref, g_ref, m_ref, v_ref, d_ref, mo_ref, vo_ref):
        gv = g_ref[...]
        mn = ADAM_B1 * m_ref[...] + (1.0 - ADAM_B1) * gv
        vn = ADAM_B2 * v_ref[...] + (1.0 - ADAM_B2) * (gv * gv)
        d_ref[...] = -ADAM_LR * ((mn / c1) / (jnp.sqrt(vn / c2) + ADAM_EPS) + ADAM_WD * w_ref[...])
        mo_ref[...] = mn
        vo_ref[...] = vn

    row = pl.BlockSpec((tr, c), lambda i: (i, 0))
    return _pallas(body, name=name, grid=(r // tr,), in_specs=[row] * 4, out_specs=[row] * 3,
                   out_shape=[_sds((r, c), F32)] * 3, compiler_params=_params(("parallel",)))(w, g, m, v)


ANY = pl.BlockSpec(memory_space=pl.ANY)


def _place():
    x, y, c = lax.axis_index("x"), lax.axis_index("y"), lax.axis_index("c")
    return x, y, c, [(1 - x, y), (x, 1 - y), (1 - x, 1 - y)]


def _rcopy(src, dst, send_sems, recv_sems, k, to):
    return pltpu.make_async_remote_copy(src_ref=src, dst_ref=dst, send_sem=send_sems.at[k], recv_sem=recv_sems.at[k],
                                        device_id=to, device_id_type=MESH)


def cast_place(chip_idx, shard, name):
    k, ns = shard.shape
    tr = _tile(k, max(16, (1 << 19) // ns), 16)

    def body(k_ref, s_ref, o_ref):
        o_ref[...] = s_ref[...].astype(o_ref.dtype)

    gs = pltpu.PrefetchScalarGridSpec(
        num_scalar_prefetch=1, grid=(k // tr,),
        in_specs=[pl.BlockSpec((tr, ns), lambda i, k_ref: (i, 0))],
        out_specs=pl.BlockSpec((None, tr, ns), lambda i, k_ref: (k_ref[0], i, 0)))
    return _pallas(body, name=name, grid_spec=gs, out_shape=_sds((N_CHIPS, k, ns), CDT),
                   compiler_params=_params(("parallel",)))(chip_idx, shard)


HBM = pl.BlockSpec(memory_space=pltpu.HBM)
SEM = pl.BlockSpec(memory_space=pltpu.SEMAPHORE)
EFFECT = pltpu.SideEffectType.DATAFLOW_SIDE_EFFECTING


def _in_hbm(a):
    return pltpu.with_memory_space_constraint(a, pltpu.HBM)


def _gather_copies(refs, send_sems, recv_sems):
    x, y, c, chips = _place()
    me = 2 * x + y
    out = []
    for t, ref in enumerate(refs):
        kh = ref.shape[1] // 2
        for j, (px, py) in enumerate(chips):
            send = _rcopy(ref.at[me, pl.ds(c * kh, kh)], ref.at[me, pl.ds(c * kh, kh)], send_sems, recv_sems,
                          3 * t + j, (px, py, c))
            land = ref.at[2 * px + py, pl.ds(c * kh, kh)]
            out.append((send, _rcopy(land, land, send_sems, recv_sems, 3 * t + j, (px, py, c))))
    return out


def gather_start(bufs, after, name):
    nt = len(bufs)

    def body(*refs):
        ins, send_sems, recv_sems, token = refs[:nt], refs[nt + 1], refs[nt + 2], refs[-1]
        for send, _ in _gather_copies(ins, send_sems, recv_sems):
            send.start()
        token[...] = jnp.zeros_like(token)

    outs = _pallas(body, name=name, in_specs=[HBM] * nt + [ANY],
                   out_specs=(SEM, SEM) + (HBM,) * nt + (pl.BlockSpec(memory_space=pltpu.VMEM),),
                   out_shape=(pltpu.SemaphoreType.DMA((3 * nt,)), pltpu.SemaphoreType.DMA((3 * nt,)))
                   + tuple(pltpu.HBM(b.shape, b.dtype) for b in bufs) + (_sds((8, HEAD), F32),),
                   input_output_aliases={t: 2 + t for t in range(nt)},
                   compiler_params=pltpu.CompilerParams(has_side_effects=EFFECT))(*[_in_hbm(b) for b in bufs], after)
    return outs[0], outs[1], list(outs[2:2 + nt]), outs[-1]


def gather_wait(bufs, send_sems, recv_sems, after, name):
    nt = len(bufs)

    def body(*refs):
        ins, s_sems, r_sems = refs[:nt], refs[nt], refs[nt + 1]
        for send, land in _gather_copies(ins, s_sems, r_sems):
            send.wait_send()
            land.wait_recv()

    return _pallas(body, name=name, in_specs=[HBM] * nt + [SEM, SEM, ANY], out_specs=[HBM] * nt,
                   out_shape=[pltpu.HBM(b.shape, b.dtype) for b in bufs],
                   input_output_aliases={t: t for t in range(nt)},
                   compiler_params=pltpu.CompilerParams(has_side_effects=EFFECT))(*bufs, send_sems, recv_sems, after)


def pair_forward(bufs, name):
    nt = len(bufs)

    def body(*refs):
        outs = refs[nt:2 * nt]
        send_sems, recv_sems = refs[2 * nt:]
        x, y, c, chips = _place()
        cps = []
        for t in range(nt):
            kh = outs[t].shape[1] // 2
            for j, (px, py) in enumerate(chips):
                blk = outs[t].at[2 * px + py, pl.ds(c * kh, kh)]
                cps.append(_rcopy(blk, blk, send_sems, recv_sems, 3 * t + j, (x, y, 1 - c)))
                cps[-1].start()
        for t in range(nt):
            kh = outs[t].shape[1] // 2
            for j, (px, py) in enumerate(chips):
                blk = outs[t].at[2 * px + py, pl.ds((1 - c) * kh, kh)]
                _rcopy(blk, blk, send_sems, recv_sems, 3 * t + j, (x, y, 1 - c)).wait_recv()
        for cp in cps:
            cp.wait_send()

    return _pallas(body, name=name, in_specs=[ANY] * nt, out_specs=[ANY] * nt,
                   out_shape=[_sds(b.shape, b.dtype) for b in bufs],
                   input_output_aliases={t: t for t in range(nt)},
                   scratch_shapes=[pltpu.SemaphoreType.DMA((3 * nt,)), pltpu.SemaphoreType.DMA((3 * nt,))],
                   compiler_params=pltpu.CompilerParams(has_side_effects=True))(*bufs)


def pair_exchange(grads, name):
    nt = len(grads)

    def body(*refs):
        ins, outs = refs[:nt], refs[nt:2 * nt]
        send_sems, recv_sems = refs[2 * nt:]
        x, y, c, _ = _place()
        sibling = (x, y, 1 - c)
        cps = []
        for t in range(nt):
            kh = ins[t].shape[1] // 2
            cps.append(_rcopy(ins[t].at[:, pl.ds((1 - c) * kh, kh), :], outs[t], send_sems, recv_sems, t, sibling))
            cps[-1].start()
        for cp in cps:
            cp.wait_recv()
        for cp in cps:
            cp.wait_send()

    return _pallas(body, name=name, in_specs=[ANY] * nt, out_specs=[ANY] * nt,
                   out_shape=[_sds((N_CHIPS, g.shape[1] // 2, g.shape[2]), g.dtype) for g in grads],
                   scratch_shapes=[pltpu.SemaphoreType.DMA((nt,)), pltpu.SemaphoreType.DMA((nt,))],
                   compiler_params=pltpu.CompilerParams(has_side_effects=True))(*grads)


def _exchange_copies(sums, lands, send_sems, recv_sems):
    x, y, c, chips = _place()
    return [_rcopy(s.at[2 * px + py], l.at[j], send_sems, recv_sems, 3 * t + j, (px, py, c))
            for t, (s, l) in enumerate(zip(sums, lands)) for j, (px, py) in enumerate(chips)]


def exchange_start(sums, name):
    nt = len(sums)
    lands = [lax.empty((3,) + s.shape[1:], s.dtype) for s in sums]

    def body(*refs):
        ins, zones, send_sems, recv_sems, token = refs[:nt], refs[nt:2 * nt], refs[2 * nt], refs[2 * nt + 1], refs[-1]
        for cp in _exchange_copies(ins, zones, send_sems, recv_sems):
            cp.start()
        token[...] = jnp.zeros_like(token)

    outs = _pallas(body, name=name, in_specs=[HBM] * (2 * nt),
                   out_specs=(SEM, SEM) + (HBM,) * (2 * nt) + (pl.BlockSpec(memory_space=pltpu.VMEM),),
                   out_shape=(pltpu.SemaphoreType.DMA((3 * nt,)), pltpu.SemaphoreType.DMA((3 * nt,)))
                   + tuple(pltpu.HBM(a.shape, a.dtype) for a in list(sums) + lands) + (_sds((8, HEAD), F32),),
                   input_output_aliases={t: 2 + t for t in range(2 * nt)},
                   compiler_params=pltpu.CompilerParams(has_side_effects=EFFECT))(*[_in_hbm(a) for a in list(sums) + lands])
    return outs[0], outs[1], list(outs[2:2 + nt]), list(outs[2 + nt:2 + 2 * nt]), outs[-1]


def exchange_wait(sums, lands, send_sems, recv_sems, after, name):
    nt = len(sums)

    def body(*refs):
        ins, zones, s_sems, r_sems = refs[:nt], refs[nt:2 * nt], refs[2 * nt], refs[2 * nt + 1]
        for cp in _exchange_copies(ins, zones, s_sems, r_sems):
            cp.wait_send()
            cp.wait_recv()

    outs = _pallas(body, name=name, in_specs=[HBM] * (2 * nt) + [SEM, SEM, ANY], out_specs=[HBM] * (2 * nt),
                   out_shape=[pltpu.HBM(a.shape, a.dtype) for a in list(sums) + list(lands)],
                   input_output_aliases={t: t for t in range(2 * nt)},
                   compiler_params=pltpu.CompilerParams(has_side_effects=EFFECT))(*sums, *lands, send_sems, recv_sems, after)
    return list(outs[:nt]), list(outs[nt:])


def pair_share(halves, name):
    nt = len(halves)

    def body(*refs):
        ins, outs = refs[:nt], refs[nt:2 * nt]
        send_sems, recv_sems = refs[2 * nt:]
        x, y, c, _ = _place()
        cps = []
        for t in range(nt):
            cps.append(_rcopy(ins[t], outs[t], send_sems, recv_sems, t, (x, y, 1 - c)))
            cps[-1].start()
        for cp in cps:
            cp.wait_recv()
        for cp in cps:
            cp.wait_send()

    return _pallas(body, name=name, in_specs=[ANY] * nt, out_specs=[ANY] * nt,
                   out_shape=[_sds(h.shape, h.dtype) for h in halves],
                   scratch_shapes=[pltpu.SemaphoreType.DMA((nt,)), pltpu.SemaphoreType.DMA((nt,))],
                   compiler_params=pltpu.CompilerParams(has_side_effects=True))(*halves)


def small_allreduce(pack, name):
    r = pack.shape[0]

    def body(in_ref, out_ref, buf, send_sems, recv_sems):
        x, y, c, _ = _place()
        me = 4 * x + 2 * y + c
        sends = []
        for k in range(1, 8):
            to = ((x + ((k >> 2) & 1)) % 2, (y + ((k >> 1) & 1)) % 2, (c + (k & 1)) % 2)
            cp = _rcopy(in_ref, buf.at[me], send_sems, recv_sems, k - 1, to)
            cp.start()
            sends.append((cp, to))
        buf[pl.ds(me, 1)] = in_ref[...][None]
        for k, (_, to) in enumerate(sends):
            peer = 4 * to[0] + 2 * to[1] + to[2]
            _rcopy(in_ref, buf.at[peer], send_sems, recv_sems, k, to).wait_recv()
        for cp, _ in sends:
            cp.wait_send()
        acc = buf[0]
        for d in range(1, 8):
            acc = acc + buf[d]
        out_ref[...] = acc

    vm = pl.BlockSpec(memory_space=pltpu.VMEM)
    return _pallas(body, name=name, in_specs=[vm], out_specs=vm, out_shape=_sds((r, HEAD), F32),
                   scratch_shapes=[pltpu.VMEM((8, r, HEAD), F32), pltpu.SemaphoreType.DMA((7,)),
                                   pltpu.SemaphoreType.DMA((7,))],
                   compiler_params=pltpu.CompilerParams(has_side_effects=True))(pack)


def add_halves(c_idx, grad, other, name):
    _, k, ns = grad.shape
    kh = k // 2
    tr = _tile(kh, max(16, (1 << 19) // ns), 16)
    nr = kh // tr

    def body(c_ref, g_ref, o_ref, s_ref):
        s_ref[...] = (g_ref[...] + o_ref[...]).astype(s_ref.dtype)

    gs = pltpu.PrefetchScalarGridSpec(
        num_scalar_prefetch=1, grid=(N_CHIPS, nr),
        in_specs=[pl.BlockSpec((None, tr, ns), lambda g, i, c_ref: (g, c_ref[0] * nr + i, 0)),
                  pl.BlockSpec((None, tr, ns), lambda g, i, c_ref: (g, i, 0))],
        out_specs=pl.BlockSpec((None, tr, ns), lambda g, i, c_ref: (g, i, 0)))
    return _pallas(body, name=name, grid_spec=gs, out_shape=_sds((N_CHIPS, kh, ns), XDT),
                   compiler_params=_params(("parallel", "parallel")))(c_idx, grad, other)


def add_chips(chip_idx, sums, recv, stack, layer, n_layers, name):
    _, kh, ns = sums.shape
    tr = _tile(kh, max(16, (1 << 19) // ns), 16)
    has_stack = stack is not None

    def body(k_ref, s_ref, r0, r1, r2, *rest):
        o_ref = rest[-1]
        o_ref[...] = ((s_ref[...].astype(F32) + r0[...].astype(F32)) + r1[...].astype(F32)) + r2[...].astype(F32)

    rspec = [pl.BlockSpec((None, tr, ns), functools.partial(lambda i, k_ref, j: (j, i, 0), j=j)) for j in range(3)]
    gs = pltpu.PrefetchScalarGridSpec(
        num_scalar_prefetch=1, grid=(kh // tr,),
        in_specs=[pl.BlockSpec((None, tr, ns), lambda i, k_ref: (k_ref[0], i, 0))] + rspec + ([ANY] if has_stack else []),
        out_specs=pl.BlockSpec((None, tr, ns), lambda i, k_ref: (layer, i, 0)))
    args = (chip_idx, sums, recv, recv, recv) + ((stack,) if has_stack else ())
    return _pallas(body, name=name, grid_spec=gs, out_shape=_sds((n_layers, kh, ns), F32),
                   input_output_aliases={5: 0} if has_stack else {},
                   compiler_params=_params(("parallel",)))(*args)


def adamw_big(c_idx, w, m, v, mine, other, name):
    nl, k, ns = w.shape
    kh = k // 2
    tr = _tile(kh, max(8, (1 << 18) // ns), 8)
    nr = kh // tr
    c1 = 1.0 - ADAM_B1 ** ADAM_STEP
    c2 = 1.0 - ADAM_B2 ** ADAM_STEP

    def body(c_ref, w_ref, m_ref, v_ref, a_ref, b_ref, g_ref, d_ref, mo_ref, vo_ref):
        gv = jnp.where(pl.program_id(2) == c_ref[0], a_ref[...], b_ref[...])
        mn = ADAM_B1 * m_ref[...] + (1.0 - ADAM_B1) * gv
        vn = ADAM_B2 * v_ref[...] + (1.0 - ADAM_B2) * (gv * gv)
        g_ref[...] = gv
        d_ref[...] = -ADAM_LR * ((mn / c1) / (jnp.sqrt(vn / c2) + ADAM_EPS) + ADAM_WD * w_ref[...])
        mo_ref[...] = mn
        vo_ref[...] = vn

    full = pl.BlockSpec((None, tr, ns), lambda l, i, hh, c_ref: (l, hh * nr + i, 0))
    half = pl.BlockSpec((None, tr, ns), lambda l, i, hh, c_ref: (l, i, 0))
    gs = pltpu.PrefetchScalarGridSpec(num_scalar_prefetch=1, grid=(nl, nr, 2),
                                      in_specs=[full, full, full, half, half], out_specs=[full] * 4)
    return _pallas(body, name=name, grid_spec=gs, out_shape=[_sds(w.shape, F32)] * 4,
                   compiler_params=_params(("parallel", "parallel", "arbitrary")))(c_idx, w, m, v, mine, other)


W_NAMES = ("w_in", "w_br_a", "w_br_b", "w_br_c", "w_o", "w_gate_up", "w_down")


def _rope_tables(n):
    half = HEAD // 2
    inv_freq = ROPE_THETA ** (-jnp.arange(half, dtype=F32) * 2.0 / HEAD)
    ang = jnp.arange(n, dtype=F32)[:, None] * inv_freq[None, :]
    cos, sin = jnp.cos(ang), jnp.sin(ang)
    return jnp.concatenate([cos, cos], axis=-1), jnp.concatenate([-sin, sin], axis=-1)


def _rpb_windows(rpb):
    pad = jnp.pad(rpb, ((0, 0), (0, 1), (0, GRID_W - rpb.shape[2])))
    wins = [pad[:, i0:i0 + C_WIN_ROWS].reshape(C_HEADS, 1, C_WIN_ROWS * GRID_W) for i0 in range(C_WIN_ROWS)]
    return jnp.stack(wins, axis=1)


def _b_view(t, dil):
    n, wd = t.shape
    return t.reshape(n // dil, dil * wd)


def layer_fwd(x, p, w, cos2, sin2):
    n, d = x.shape
    s = {"x": x}
    s["h"] = rmsnorm_fwd(x, p["norm1_g"], "norm1")
    s["proj"] = mm_x_wcol(s["h"], w["w_in"], F32, "proj")
    gains = jnp.pad(p["qk_norm_g"], ((0, 2), (0, 0)))
    pp = s["pp"] = qk_prep(s["proj"], gains, cos2, sin2, "qk_prep")
    sink = p["sink_a"].reshape(1, A_Q_HEADS)
    s["oa"], s["lse_a"] = band_attn_fwd(pp["qa"], pp["ka"], pp["va"], sink, seqs=A_KV_HEADS, G=A_GROUP,
                                        nh=A_KV_HEADS, radius=A_RADIUS, name="attn_a")
    s["ob"], s["lse_b"] = [], []
    for g, dil in enumerate(B_DILS):
        o, lse = band_attn_fwd(_b_view(pp[f"qb{g}"], dil), _b_view(pp[f"kb{g}"], dil), _b_view(pp[f"vb{g}"], dil),
                               None, seqs=dil * B_HG, G=1, nh=B_HG, radius=B_RADIUS, name=f"attn_b{g}")
        s["ob"].append(o.reshape(n, B_HG * HEAD))
        s["lse_b"].append(lse.reshape(n, B_HG * HEAD))
    ob = b_combine_fwd(s["ob"], s["lse_b"], "b_combine")
    s["bias"] = rpb_expand(_rpb_windows(p["rpb_c"]), "rpb_expand")
    s["oc"], s["lse_c"] = c_attn_fwd(pp["qc"], pp["kc"], pp["vc"], s["bias"], "attn_c")
    s["o_in"] = (s["oa"], ob, s["oc"])
    s["ys"] = [mm_x_wcol(o, w[k], F32, "branch_" + k[-1]) for o, k in zip(s["o_in"], ("w_br_a", "w_br_b", "w_br_c"))]
    s["merged"] = gate_merge(s["proj"], s["ys"], "gate_merge")
    s["x_mid"] = mm_x_w(s["merged"], w["w_o"], "out_proj", res=x)
    s["h2"] = rmsnorm_fwd(s["x_mid"], p["norm2_g"], "norm2")
    s["gu"] = mm_x_wcol(s["h2"], w["w_gate_up"], F32, "gate_up", tn_pref=1408, stacked_out=2)
    s["act"] = swiglu_fwd(s["gu"], "swiglu")
    x_out = mm_x_w(s["act"], w["w_down"], "down", res=s["x_mid"], tk_pref=2816)
    return x_out, s


def layer_bwd(dx_out, s, p, w, cos2, sin2):
    n, d = dx_out.shape
    pp = s["pp"]
    dact = mm_x_wT(dx_out, w["w_down"], "d_act")
    dw_down = mm_aT_d(s["act"], dx_out, "dw_down")
    dgu = swiglu_bwd(s["gu"], dact, "swiglu_bwd")
    dh2 = mm_x_wcolT(dgu, w["w_gate_up"], "d_h2", stacked_in=2)
    dw_gu = mm_aT_d_wcol(s["h2"], dgu, "dw_gate_up", tn_pref=1408, stacked_in=2)
    dx_mid, dg2 = rmsnorm_bwd(s["x_mid"], p["norm2_g"], dh2, dx_out, "norm2_bwd")

    dmerged = mm_x_wT(dx_mid, w["w_o"], "d_merged")
    dw_o = mm_aT_d(s["merged"], dx_mid, "dw_o")
    dys, dgls = gate_bwd(s["proj"], s["ys"], dmerged, "gate_bwd")
    dos, dw_br = [], []
    for o, dy, k in zip(s["o_in"], dys, ("w_br_a", "w_br_b", "w_br_c")):
        dos.append(mm_x_wcolT(dy, w[k], "d_o_" + k[-1]))
        dw_br.append(mm_aT_d_wcol(o, dy, "dw_br_" + k[-1]))

    grads = {}
    sink = p["sink_a"].reshape(1, A_Q_HEADS)
    grads["qa"], grads["ka"], grads["va"], dsink = band_attn_bwd(
        pp["qa"], pp["ka"], pp["va"], sink, s["oa"], s["lse_a"], dos[0], None,
        seqs=A_KV_HEADS, G=A_GROUP, nh=A_KV_HEADS, radius=A_RADIUS, name="attn_a_bwd")
    dobs, dlses = b_combine_bwd(dos[1], s["ob"], s["lse_b"], "b_combine_bwd")
    for g, dil in enumerate(B_DILS):
        dq, dk, dv = band_attn_bwd(_b_view(pp[f"qb{g}"], dil), _b_view(pp[f"kb{g}"], dil), _b_view(pp[f"vb{g}"], dil),
                                   None, _b_view(s["ob"][g], dil), _b_view(s["lse_b"][g], dil), _b_view(dobs[g], dil),
                                   _b_view(dlses[g], dil), seqs=dil * B_HG, G=1, nh=2, radius=B_RADIUS,
                                   name=f"attn_b{g}_bwd")
        grads[f"qb{g}"], grads[f"kb{g}"], grads[f"vb{g}"] = [t.reshape(n, B_HG * HEAD) for t in (dq, dk, dv)]
    grads["qc"], grads["kc"], grads["vc"], dbias = c_attn_bwd(pp["qc"], pp["kc"], pp["vc"], s["bias"], s["oc"],
                                                              s["lse_c"], dos[2], "attn_c_bwd")
    drpb = rpb_reduce(dbias, "rpb_reduce")[:, :2 * C_WIN_ROWS - 1, :2 * C_WIN_COLS - 1]
    gains = jnp.pad(p["qk_norm_g"], ((0, 2), (0, 0)))
    dqkv, dgains = qk_prep_bwd(s["proj"], gains, cos2, sin2, grads, "qk_prep_bwd")
    dproj = jnp.concatenate([dqkv] + list(dgls), axis=1)
    dh = mm_x_wcolT(dproj, w["w_in"], "d_h")
    dw_in = mm_aT_d_wcol(s["h"], dproj, "dw_in")
    dx_in, dg1 = rmsnorm_bwd(s["x"], p["norm1_g"], dh, dx_mid, "norm1_bwd")

    dws = [dw_in] + dw_br + [dw_o.reshape(N_CHIPS, d // N_CHIPS, d), dw_gu,
                             dw_down.reshape(N_CHIPS, dw_down.shape[0] // N_CHIPS, d)]
    small = {"norm1_g": dg1[0], "qk_norm_g": dgains[:6], "sink_a": dsink[0, :, 0],
             "rpb_c": drpb, "norm2_g": dg2[0]}
    return dx_in, dws, small


SMALL_NAMES = ("norm1_g", "qk_norm_g", "sink_a", "rpb_c", "norm2_g")


def _pack_small(parts, extra=None):
    flat = [parts[k].reshape(-1) for k in SMALL_NAMES]
    flat.append(jnp.zeros((1,), F32) if extra is None else extra.reshape(1))
    v = jnp.concatenate(flat)
    rows = -(-v.shape[0] // (8 * HEAD)) * 8
    return jnp.pad(v, (0, rows * HEAD - v.shape[0])).reshape(rows, HEAD)


def _unpack_small(pack, like):
    v = pack.reshape(-1)
    out, off = {}, 0
    for k in SMALL_NAMES:
        size = math.prod(like[k].shape)
        out[k] = v[off:off + size].reshape(like[k].shape)
        off += size
    return out, v[off]


def kernel(x, norm1_g, w_in, qk_norm_g, sink_a, rpb_c, w_br_a, w_br_b, w_br_c, w_o, norm2_g, w_gate_up, w_down, loss_target, m_norm1_g, m_w_in, m_qk_norm_g, m_sink_a, m_rpb_c, m_w_br_a, m_w_br_b, m_w_br_c, m_w_o, m_norm2_g, m_w_gate_up, m_w_down, v_norm1_g, v_w_in, v_qk_norm_g, v_sink_a, v_rpb_c, v_w_br_a, v_w_br_b, v_w_br_c, v_w_o, v_norm2_g, v_w_gate_up, v_w_down):
    big = dict(w_in=w_in, w_br_a=w_br_a, w_br_b=w_br_b, w_br_c=w_br_c, w_o=w_o, w_gate_up=w_gate_up, w_down=w_down)
    big_m = dict(w_in=m_w_in, w_br_a=m_w_br_a, w_br_b=m_w_br_b, w_br_c=m_w_br_c, w_o=m_w_o, w_gate_up=m_w_gate_up, w_down=m_w_down)
    big_v = dict(w_in=v_w_in, w_br_a=v_w_br_a, w_br_b=v_w_br_b, w_br_c=v_w_br_c, w_o=v_w_o, w_gate_up=v_w_gate_up, w_down=v_w_down)
    small = dict(norm1_g=norm1_g, qk_norm_g=qk_norm_g, sink_a=sink_a, rpb_c=rpb_c, norm2_g=norm2_g)
    small_m = dict(norm1_g=m_norm1_g, qk_norm_g=m_qk_norm_g, sink_a=m_sink_a, rpb_c=m_rpb_c, norm2_g=m_norm2_g)
    small_v = dict(norm1_g=v_norm1_g, qk_norm_g=v_qk_norm_g, sink_a=v_sink_a, rpb_c=v_rpb_c, norm2_g=v_norm2_g)
    n_layers = w_in.shape[0]
    n, d = x.shape[1], x.shape[2]
    c_idx = lax.axis_index("c").astype(jnp.int32).reshape(1)
    chip_idx = (2 * lax.axis_index("x") + lax.axis_index("y")).astype(jnp.int32).reshape(1)
    cos2, sin2 = _rope_tables(n)

    def gathered(started, after):
        send_sems, recv_sems, bufs, _ = started
        got = pair_forward(gather_wait(bufs, send_sems, recv_sems, after, "gather_wait"), "pair_forward")
        w = dict(zip(W_NAMES, got))
        w["w_o"] = w["w_o"].reshape(d, d)
        w["w_down"] = w["w_down"].reshape(-1, d)
        return w

    def start_gather(l, after):
        return gather_start([cast_place(chip_idx, big[k][l], "cast_" + k) for k in W_NAMES], after, "gather_start")

    started = start_gather(0, chip_idx)
    weights = [gathered(started, started[3])]

    xs, saved = x[0], []
    for l in range(n_layers):
        p = {k: small[k][l] for k in SMALL_NAMES}
        if l + 1 < n_layers:
            started = start_gather(l + 1, weights[l]["w_in"])
            p["norm1_g"] = p["norm1_g"] + started[3][0, 0]
        xs, s = layer_fwd(xs, p, weights[l], cos2, sin2)
        saved.append(s)
        if l + 1 < n_layers:
            weights.append(gathered(started, xs))
    loss_tile, dx = loss_head(xs, loss_target[0], "loss_head")

    halves = [None] * len(W_NAMES)
    small_g = [None] * n_layers

    def finish_exchange(pending, after, halves):
        l, (send_sems, recv_sems, sums, lands, _) = pending
        sums, from_chips = exchange_wait(sums, lands, send_sems, recv_sems, after, "exchange_wait")
        return [add_chips(chip_idx, sm, r, st, l, n_layers, "add_chips_" + k)
                for sm, r, st, k in zip(sums, from_chips, halves, W_NAMES)]

    pending = None
    for l in reversed(range(n_layers)):
        p = {k: small[k][l] for k in SMALL_NAMES}
        dx, dws, small_g[l] = layer_bwd(dx, saved[l], p, weights[l], cos2, sin2)
        if pending is not None:
            halves = finish_exchange(pending, dx, halves)
        from_sibling = pair_exchange(dws, "pair_exchange")
        sums = [add_halves(c_idx, g, o, "add_halves_" + k) for g, o, k in zip(dws, from_sibling, W_NAMES)]
        pending = (l, exchange_start(sums, "exchange_start"))
        dx = dx + pending[1][4][0, 0]
    halves = finish_exchange(pending, dx, halves)
    others = pair_share(halves, "pair_share")

    mine = {k: jnp.stack([small_g[l][k] for l in range(n_layers)]) for k in SMALL_NAMES}
    total = small_allreduce(_pack_small(mine, loss_tile[0, 0]), "small_allreduce")
    grad_small, loss = _unpack_small(total, small)

    outs = {}
    for k, mine_half, other_half in zip(W_NAMES, halves, others):
        outs[k] = adamw_big(c_idx, big[k], big_m[k], big_v[k], mine_half, other_half, "adamw_" + k)
    res = adamw(_pack_small(small), _pack_small(grad_small), _pack_small(small_m), _pack_small(small_v), "adamw_small")
    unp = [_unpack_small(t, small)[0] for t in res]
    for k in SMALL_NAMES:
        outs[k] = (grad_small[k],) + tuple(u[k] for u in unp)

    order = ("norm1_g", "w_in", "qk_norm_g", "sink_a", "rpb_c", "w_br_a", "w_br_b", "w_br_c", "w_o", "norm2_g",
             "w_gate_up", "w_down")
    return (loss, dx[None]) + tuple(outs[k][i] for i in range(4) for k in order)
```

```python
import functools
import math

import jax
import jax.numpy as jnp
from jax import lax
from jax.experimental import pallas as pl
from jax.experimental.pallas import tpu as pltpu

F32 = jnp.float32
CDT = jnp.bfloat16
XDT = jnp.bfloat16
ADT = jnp.bfloat16

HEAD = 128
NORM_EPS = 1e-6
ROPE_THETA = 10000.0
A_Q_HEADS, A_KV_HEADS, A_GROUP, A_RADIUS = 8, 2, 4, 128
B_DILS = (1, 4, 16)
B_RADIUS = 64
B_HG = 4
C_HEADS, GRID_W, C_WIN_ROWS, C_WIN_COLS = 8, 64, 8, 16
QKV_W = 9216
COL = dict(qa=0, ka=1024, va=1280, qb=1536, kb=3072, vb=4608, qc=6144, kc=7168, vc=8192)
NEG = -1e30
SCALE = HEAD ** -0.5
N_CHIPS = 4

ADAM_LR, ADAM_B1, ADAM_B2, ADAM_EPS, ADAM_WD, ADAM_STEP = 0.001, 0.9, 0.999, 1e-08, 0.01, 10

VMEM_LIMIT = 56 * 1024 * 1024
MESH = pl.DeviceIdType.MESH


def _pallas(body, **kw):
    return pl.pallas_call(body, **kw)


def _params(sem=None, **kw):
    if sem is not None:
        kw["dimension_semantics"] = sem
    return pltpu.CompilerParams(vmem_limit_bytes=VMEM_LIMIT, **kw)


def _tile(dim, pref, mult=128):
    best = None
    for t in range(mult, min(dim, pref) + 1, mult):
        if dim % t == 0:
            best = t
    return dim if best is None else best


def _sds(shape, dtype):
    return jax.ShapeDtypeStruct(tuple(shape), dtype)


_DIMS = {"nn": (((1,), (0,)), ((), ())), "nt": (((1,), (1,)), ((), ())), "tn": (((0,), (0,)), ((), ()))}


def _matmul(a, b, *, mode, grid, a_spec, b_spec, o_spec, out_shape, acc_shape, name, res=None, res_spec=None):
    nk = grid[2]
    has_res = res is not None

    def body(*refs):
        if has_res:
            a_ref, b_ref, r_ref, o_ref = refs[:4]
            rest = refs[4:]
        else:
            a_ref, b_ref, o_ref = refs[:3]
            r_ref = None
            rest = refs[3:]
        p = lax.dot_general(a_ref[...].astype(CDT), b_ref[...].astype(CDT), _DIMS[mode],
                            preferred_element_type=F32)

        def finish(acc):
            if has_res:
                acc = acc + r_ref[...].astype(F32)
            o_ref[...] = acc.astype(o_ref.dtype)

        if nk == 1:
            finish(p)
        else:
            acc_ref = rest[0]
            k = pl.program_id(2)

            @pl.when(k == 0)
            def _():
                acc_ref[...] = p

            @pl.when(k > 0)
            def _():
                acc_ref[...] += p

            @pl.when(k == nk - 1)
            def _():
                finish(acc_ref[...])

    in_specs = [a_spec, b_spec] + ([res_spec] if has_res else [])
    args = (a, b) + ((res,) if has_res else ())
    scratch = [] if nk == 1 else [pltpu.VMEM(acc_shape, F32)]
    return _pallas(body, name=name, grid=grid, in_specs=in_specs, out_specs=o_spec, out_shape=out_shape,
                   scratch_shapes=scratch, compiler_params=_params(("parallel", "parallel", "arbitrary")))(*args)


def mm_x_wcol(a, wg, out_dtype, name, tm_pref=1024, tn_pref=1024, stacked_out=1):
    m, k = a.shape
    ns = wg.shape[2]
    tm, tn = _tile(m, tm_pref, 8), _tile(ns, tn_pref)
    nj = ns // tn
    grid = (m // tm, N_CHIPS * nj, 1)
    a_spec = pl.BlockSpec((tm, k), lambda i, j, kk: (i, 0))
    b_spec = pl.BlockSpec((None, k, tn), lambda i, j, kk: (j // nj, 0, j % nj))
    if stacked_out == 1:
        o_spec = pl.BlockSpec((tm, tn), lambda i, j, kk: (i, j))
        out_shape = _sds((m, N_CHIPS * ns), out_dtype)
    else:
        per = N_CHIPS * nj // stacked_out
        o_spec = pl.BlockSpec((None, tm, tn), lambda i, j, kk: (j // per, i, j % per))
        out_shape = _sds((stacked_out, m, N_CHIPS * ns // stacked_out), out_dtype)
    return _matmul(a, wg, mode="nn", grid=grid, a_spec=a_spec, b_spec=b_spec, o_spec=o_spec,
                   out_shape=out_shape, acc_shape=(tm, tn), name=name)


def mm_x_wcolT(d, wg, name, res=None, tm_pref=1024, tn_pref=512, tk_pref=4096, stacked_in=1):
    kdim, ns = wg.shape[1], wg.shape[2]
    m = d.shape[-2]
    tm, tn, tk = _tile(m, tm_pref, 8), _tile(kdim, tn_pref), _tile(ns, tk_pref)
    nkk = ns // tk
    grid = (m // tm, kdim // tn, N_CHIPS * nkk)
    if stacked_in == 1:
        a_spec = pl.BlockSpec((tm, tk), lambda i, j, kk: (i, kk))
    else:
        per = N_CHIPS * nkk // stacked_in
        a_spec = pl.BlockSpec((None, tm, tk), lambda i, j, kk: (kk // per, i, kk % per))
    b_spec = pl.BlockSpec((None, tn, tk), lambda i, j, kk: (kk // nkk, j, kk % nkk))
    o_spec = pl.BlockSpec((tm, tn), lambda i, j, kk: (i, j))
    return _matmul(d, wg, mode="nt", grid=grid, a_spec=a_spec, b_spec=b_spec, o_spec=o_spec,
                   out_shape=_sds((m, kdim), F32), acc_shape=(tm, tn), name=name,
                   res=res, res_spec=None if res is None else o_spec)


def mm_aT_d_wcol(a, d, name, tm_pref=512, tk_pref=4096, tn_pref=1024, stacked_in=1):
    m, kdim = a.shape
    ntot = d.shape[-1] * stacked_in
    ns = ntot // N_CHIPS
    tm, tkm, tn = _tile(kdim, tm_pref), _tile(m, tk_pref, 8), _tile(ns, tn_pref)
    nj = ns // tn
    grid = (kdim // tm, N_CHIPS * nj, m // tkm)
    a_spec = pl.BlockSpec((tkm, tm), lambda i, j, kk: (kk, i))
    if stacked_in == 1:
        b_spec = pl.BlockSpec((tkm, tn), lambda i, j, kk: (kk, j))
    else:
        per = N_CHIPS * nj // stacked_in
        b_spec = pl.BlockSpec((None, tkm, tn), lambda i, j, kk: (j // per, kk, j % per))
    o_spec = pl.BlockSpec((None, tm, tn), lambda i, j, kk: (j // nj, i, j % nj))
    return _matmul(a, d, mode="tn", grid=grid, a_spec=a_spec, b_spec=b_spec, o_spec=o_spec,
                   out_shape=_sds((N_CHIPS, kdim, ns), XDT), acc_shape=(tm, tn), name=name)


def mm_x_w(a, w, name, res=None, out_dtype=F32, tm_pref=1024, tn_pref=1024, tk_pref=2048):
    m, k = a.shape
    n = w.shape[1]
    tm, tn, tk = _tile(m, tm_pref, 8), _tile(n, tn_pref), _tile(k, tk_pref)
    grid = (m // tm, n // tn, k // tk)
    o_spec = pl.BlockSpec((tm, tn), lambda i, j, kk: (i, j))
    return _matmul(a, w, mode="nn", grid=grid,
                   a_spec=pl.BlockSpec((tm, tk), lambda i, j, kk: (i, kk)),
                   b_spec=pl.BlockSpec((tk, tn), lambda i, j, kk: (kk, j)),
                   o_spec=o_spec, out_shape=_sds((m, n), out_dtype), acc_shape=(tm, tn), name=name,
                   res=res, res_spec=None if res is None else o_spec)


def mm_x_wT(d, w, name, out_dtype=F32, tm_pref=1024, tn_pref=1024):
    m, n = d.shape
    k = w.shape[0]
    tm, tn = _tile(m, tm_pref, 8), _tile(k, tn_pref)
    grid = (m // tm, k // tn, 1)
    return _matmul(d, w, mode="nt", grid=grid,
                   a_spec=pl.BlockSpec((tm, n), lambda i, j, kk: (i, 0)),
                   b_spec=pl.BlockSpec((tn, n), lambda i, j, kk: (j, 0)),
                   o_spec=pl.BlockSpec((tm, tn), lambda i, j, kk: (i, j)),
                   out_shape=_sds((m, k), out_dtype), acc_shape=(tm, tn), name=name)


def mm_aT_d(a, d, name, tm_pref=512, tn_pref=512, tk_pref=4096):
    m, k = a.shape
    n = d.shape[1]
    tm, tn, tk = _tile(k, tm_pref), _tile(n, tn_pref), _tile(m, tk_pref, 8)
    grid = (k // tm, n // tn, m // tk)
    return _matmul(a, d, mode="tn", grid=grid,
                   a_spec=pl.BlockSpec((tk, tm), lambda i, j, kk: (kk, i)),
                   b_spec=pl.BlockSpec((tk, tn), lambda i, j, kk: (kk, j)),
                   o_spec=pl.BlockSpec((tm, tn), lambda i, j, kk: (i, j)),
                   out_shape=_sds((k, n), XDT), acc_shape=(tm, tn), name=name)


def rmsnorm_fwd(x, g, name):
    n, d = x.shape
    tm = _tile(n, 512, 8)

    def body(x_ref, g_ref, h_ref):
        xv = x_ref[...]
        r = lax.rsqrt(jnp.mean(xv * xv, axis=-1, keepdims=True) + NORM_EPS)
        h_ref[...] = (xv * r * g_ref[...]).astype(h_ref.dtype)

    return _pallas(body, name=name, grid=(n // tm,),
                   in_specs=[pl.BlockSpec((tm, d), lambda i: (i, 0)), pl.BlockSpec((1, d), lambda i: (0, 0))],
                   out_specs=pl.BlockSpec((tm, d), lambda i: (i, 0)), out_shape=_sds((n, d), CDT),
                   compiler_params=_params(("parallel",)))(x, g.reshape(1, d))


def rmsnorm_bwd(x, g, dh, dres, name):
    n, d = x.shape
    tm = _tile(n, 256, 8)

    def body(x_ref, g_ref, dh_ref, dres_ref, dx_ref, dg_ref):
        xv = x_ref[...]
        r = lax.rsqrt(jnp.mean(xv * xv, axis=-1, keepdims=True) + NORM_EPS)
        dhv = dh_ref[...]
        u = dhv * g_ref[...]
        c = jnp.mean(xv * u, axis=-1, keepdims=True)
        dx_ref[...] = dres_ref[...] + r * u - xv * (r * r * r * c)
        part = jnp.broadcast_to(jnp.sum(dhv * xv * r, axis=0, keepdims=True), (8, d))

        @pl.when(pl.program_id(0) == 0)
        def _():
            dg_ref[...] = part

        @pl.when(pl.program_id(0) > 0)
        def _():
            dg_ref[...] += part

    row = pl.BlockSpec((tm, d), lambda i: (i, 0))
    dx, dg = _pallas(body, name=name, grid=(n // tm,),
                     in_specs=[row, pl.BlockSpec((1, d), lambda i: (0, 0)), row, row],
                     out_specs=[row, pl.BlockSpec((8, d), lambda i: (0, 0))],
                     out_shape=[_sds((n, d), F32), _sds((8, d), F32)],
                     compiler_params=_params(("arbitrary",)))(x, g.reshape(1, d), dh, dres)
    return dx, dg


def _norm_rope(xh, g, cos2, sin2):
    r = lax.rsqrt(jnp.mean(xh * xh, axis=-1, keepdims=True) + NORM_EPS)
    y = xh * r * g
    if cos2 is not None:
        y = y * cos2 + pltpu.roll(y, HEAD // 2, 1) * sin2
    return y


def _norm_rope_bwd(xh, g, cos2, sin2, dout):
    if cos2 is not None:
        dy = dout * cos2 + pltpu.roll(dout * sin2, HEAD // 2, 1)
    else:
        dy = dout
    r = lax.rsqrt(jnp.mean(xh * xh, axis=-1, keepdims=True) + NORM_EPS)
    u = dy * g
    c = jnp.mean(xh * u, axis=-1, keepdims=True)
    return r * u - xh * (r * r * r * c), dy * xh * r


_QK_GROUPS = (("qa", COL["qa"], 8, 0, True), ("ka", COL["ka"], 2, 1, True),
              ("qb", COL["qb"], 12, 2, True), ("kb", COL["kb"], 12, 3, True),
              ("qc", COL["qc"], 8, 4, False), ("kc", COL["kc"], 8, 5, False))
_V_GROUPS = (("va", COL["va"], 2), ("vb", COL["vb"], 12), ("vc", COL["vc"], 8))
_PREP_OUT = (("qa", 8), ("ka", 2), ("va", 2)) + tuple((f"{t}b{g}", 4) for t in "qkv" for g in range(3)) + (
    ("qc", 8), ("kc", 8), ("vc", 8))


def _prep_src(name):
    if name[1] == "b":
        base = COL[name[0] + "b"] + int(name[2]) * B_HG * HEAD
        gain = {"q": 2, "k": 3, "v": None}[name[0]]
        return base, gain, name[0] != "v"
    base = COL[name]
    gain = {"qa": 0, "ka": 1, "va": None, "qc": 4, "kc": 5, "vc": None}[name]
    return base, gain, name in ("qa", "ka")


def qk_prep(proj, gains, cos2, sin2, name):
    n = proj.shape[0]
    tm = _tile(n, 256, 8)

    def body(p_ref, g_ref, c_ref, s_ref, *outs):
        cos2v, sin2v = c_ref[...], s_ref[...]
        for (nm, heads), o_ref in zip(_PREP_OUT, outs):
            base, gain, rope = _prep_src(nm)
            for h in range(heads):
                xh = p_ref[:, base + h * HEAD: base + (h + 1) * HEAD].astype(F32)
                if gain is None:
                    y = xh
                else:
                    y = _norm_rope(xh, g_ref[gain:gain + 1, :], cos2v if rope else None, sin2v if rope else None)
                o_ref[:, h * HEAD:(h + 1) * HEAD] = y.astype(o_ref.dtype)

    tab = pl.BlockSpec((tm, HEAD), lambda i: (i, 0))
    outs = _pallas(body, name=name, grid=(n // tm,),
                   in_specs=[pl.BlockSpec((tm, QKV_W), lambda i: (i, 0)), pl.BlockSpec((8, HEAD), lambda i: (0, 0)), tab, tab],
                   out_specs=[pl.BlockSpec((tm, h * HEAD), lambda i: (i, 0)) for _, h in _PREP_OUT],
                   out_shape=[_sds((n, h * HEAD), CDT) for _, h in _PREP_OUT],
                   compiler_params=_params(("parallel",)))(proj, gains, cos2, sin2)
    return dict(zip([nm for nm, _ in _PREP_OUT], outs))


def qk_prep_bwd(proj, gains, cos2, sin2, grads, dproj, name):
    n = proj.shape[0]
    tm = _tile(n, 128, 8)
    names = [nm for nm, _ in _PREP_OUT]

    def body(p_ref, g_ref, c_ref, s_ref, *refs):
        g_refs, dp_ref, dg_ref = refs[:len(names)], refs[len(names) + 1], refs[len(names) + 2]
        cos2v, sin2v = c_ref[...], s_ref[...]
        dg = [jnp.zeros((tm, HEAD), F32) for _ in range(6)]
        for (nm, heads), gr in zip(_PREP_OUT, g_refs):
            base, gain, rope = _prep_src(nm)
            for h in range(heads):
                sl = slice(base + h * HEAD, base + (h + 1) * HEAD)
                dout = gr[:, h * HEAD:(h + 1) * HEAD]
                if gain is None:
                    dx = dout
                else:
                    dx, dgr = _norm_rope_bwd(p_ref[:, sl].astype(F32), g_ref[gain:gain + 1, :], cos2v if rope else None,
                                             sin2v if rope else None, dout)
                    dg[gain] = dg[gain] + dgr
                dp_ref[:, sl] = dx.astype(dp_ref.dtype)
        part = jnp.concatenate([jnp.sum(t, axis=0, keepdims=True) for t in dg] + [jnp.zeros((2, HEAD), F32)], axis=0)

        @pl.when(pl.program_id(0) == 0)
        def _():
            dg_ref[...] = part

        @pl.when(pl.program_id(0) > 0)
        def _():
            dg_ref[...] += part

    tab = pl.BlockSpec((tm, HEAD), lambda i: (i, 0))
    dp, dg = _pallas(body, name=name, grid=(n // tm,),
                     in_specs=[pl.BlockSpec((tm, QKV_W), lambda i: (i, 0)), pl.BlockSpec((8, HEAD), lambda i: (0, 0)), tab, tab]
                     + [pl.BlockSpec((tm, h * HEAD), lambda i: (i, 0)) for _, h in _PREP_OUT] + [ANY],
                     out_specs=[pl.BlockSpec((tm, QKV_W), lambda i: (i, 0)), pl.BlockSpec((8, HEAD), lambda i: (0, 0))],
                     out_shape=[_sds(dproj.shape, dproj.dtype), _sds((8, HEAD), F32)],
                     input_output_aliases={4 + len(names): 0},
                     compiler_params=_params(("arbitrary",)))(proj, gains, cos2, sin2, *[grads[k] for k in names], dproj)
    return dp, dg


def _band_geometry(m, bq_pref, radius):
    bq = min(bq_pref, m)
    return bq, min(bq + 2 * radius, m)


def _band_window(i, bq, radius, m, w):
    start = pl.multiple_of(jnp.clip(i * bq - radius, 0, m - w), 64)
    qpos = i * bq + lax.broadcasted_iota(jnp.int32, (bq, w), 0)
    kpos = start + lax.broadcasted_iota(jnp.int32, (bq, w), 1)
    return start, jnp.abs(kpos - qpos) <= radius


def band_attn_fwd(q, k, v, sink, *, seqs, G, nh, radius, name, bq_pref=128):
    m = q.shape[0]
    bq, w = _band_geometry(m, bq_pref, radius)
    has_sink = sink is not None

    def body(*refs):
        if has_sink:
            sink_ref, q_ref, k_ref, v_ref, o_ref, lse_ref = refs
        else:
            q_ref, k_ref, v_ref, o_ref, lse_ref = refs
        s_id, i = pl.program_id(0), pl.program_id(1)
        start, valid = _band_window(i, bq, radius, m, w)
        units = [(h, g) for h in range(nh) for g in range(G)]
        sls = [slice((h * G + g) * HEAD, (h * G + g + 1) * HEAD) for h, g in units]
        k_ts = [k_ref[pl.ds(start, w), h * HEAD:(h + 1) * HEAD] for h in range(nh)]
        v_ts = [v_ref[pl.ds(start, w), h * HEAD:(h + 1) * HEAD] for h in range(nh)]
        q_ts = [q_ref[:, sl] for sl in sls]
        sks = [sink_ref[0, (s_id * nh + h) * G + g] for h, g in units] if has_sink else None
        ss = [jnp.where(valid, lax.dot_general(q_t, k_ts[h], _DIMS["nt"], preferred_element_type=F32) * SCALE, NEG)
              for q_t, (h, g) in zip(q_ts, units)]
        mxs = [jnp.max(s, axis=-1, keepdims=True) for s in ss]
        if has_sink:
            mxs = [jnp.maximum(mx, sk) for mx, sk in zip(mxs, sks)]
        ps = [jnp.exp(s - mx) for s, mx in zip(ss, mxs)]
        dens = [jnp.sum(p, axis=-1, keepdims=True) for p in ps]
        if has_sink:
            dens = [den + jnp.exp(sk - mx) for den, sk, mx in zip(dens, sks, mxs)]
        outs = [jnp.dot((p / den).astype(CDT), v_ts[h], preferred_element_type=F32)
                for p, den, (h, g) in zip(ps, dens, units)]
        for sl, o, mx, den in zip(sls, outs, mxs, dens):
            o_ref[:, sl] = o
            lse_ref[:, sl] = jnp.broadcast_to(mx + jnp.log(den), (bq, HEAD))

    qspec = pl.BlockSpec((bq, nh * G * HEAD), lambda s, i: (i, s))
    kspec = pl.BlockSpec((m, nh * HEAD), lambda s, i: (0, s))
    in_specs = ([pl.BlockSpec(memory_space=pltpu.SMEM)] if has_sink else []) + [qspec, kspec, kspec]
    args = ((sink,) if has_sink else ()) + (q, k, v)
    return _pallas(body, name=name, grid=(seqs // nh, m // bq), in_specs=in_specs, out_specs=[qspec, qspec],
                   out_shape=[_sds(q.shape, F32), _sds(q.shape, F32)],
                   compiler_params=_params(("parallel", "arbitrary")))(*args)


def band_attn_bwd(q, k, v, sink, o, lse, do, dlse, *, seqs, G, nh, radius, name, bq_pref=128):
    m = q.shape[0]
    bq, w = _band_geometry(m, bq_pref, radius)
    has_sink, has_dlse = sink is not None, dlse is not None
    assert nh * G <= 8

    def body(*refs):
        refs = list(refs)
        sink_ref = refs.pop(0) if has_sink else None
        q_ref, k_ref, v_ref, o_ref, lse_ref, do_ref = refs[:6]
        refs = refs[6:]
        dlse_ref = refs.pop(0) if has_dlse else None
        dq_ref, dk_ref, dv_ref = refs[:3]
        dsink_ref = refs[3] if has_sink else None
        s_id, i = pl.program_id(0), pl.program_id(1)

        @pl.when(i == 0)
        def _():
            dk_ref[...] = jnp.zeros_like(dk_ref)
            dv_ref[...] = jnp.zeros_like(dv_ref)
            if has_sink:
                dsink_ref[...] = jnp.zeros_like(dsink_ref)

        start, valid = _band_window(i, bq, radius, m, w)
        units = [(h, g) for h in range(nh) for g in range(G)]
        sls = [slice((h * G + g) * HEAD, (h * G + g + 1) * HEAD) for h, g in units]
        kss = [slice(h * HEAD, (h + 1) * HEAD) for h in range(nh)]
        k_ts = [k_ref[pl.ds(start, w), ks] for ks in kss]
        v_ts = [v_ref[pl.ds(start, w), ks] for ks in kss]
        q_ts = [q_ref[:, sl] for sl in sls]
        lse_ts = [lse_ref[:, sl][:, :1] for sl in sls]
        do_ts = [do_ref[:, sl] for sl in sls]
        deltas = [jnp.sum(do_t * o_ref[:, sl], axis=-1, keepdims=True) for do_t, sl in zip(do_ts, sls)]
        dlse_ts = [dlse_ref[:, sl][:, :1] for sl in sls] if has_dlse else None
        dk_old = [dk_ref[pl.ds(start, w), ks] for ks in kss]
        dv_old = [dv_ref[pl.ds(start, w), ks] for ks in kss]
        dsink_old = dsink_ref[...] if has_sink else None

        ps = [jnp.exp(jnp.where(valid, lax.dot_general(q_t, k_ts[h], _DIMS["nt"], preferred_element_type=F32) * SCALE, NEG)
                      - lse_t) for q_t, lse_t, (h, g) in zip(q_ts, lse_ts, units)]
        do_cs = [do_t.astype(CDT) for do_t in do_ts]
        dps = [lax.dot_general(do_c, v_ts[h], _DIMS["nt"], preferred_element_type=F32) for do_c, (h, g) in zip(do_cs, units)]
        ts = [dp - delta for dp, delta in zip(dps, deltas)]
        if has_dlse:
            ts = [t + dl for t, dl in zip(ts, dlse_ts)]
        dss = [((p * t) * SCALE).astype(CDT) for p, t in zip(ps, ts)]
        dqs = [jnp.dot(ds, k_ts[h], preferred_element_type=F32) for ds, (h, g) in zip(dss, units)]
        dvs = [lax.dot_general(p.astype(CDT), do_c, _DIMS["tn"], preferred_element_type=F32) for p, do_c in zip(ps, do_cs)]
        dks = [lax.dot_general(ds, q_t, _DIMS["tn"], preferred_element_type=F32) for ds, q_t in zip(dss, q_ts)]
        dk_new = [dk_old[h] + sum(dks[h * G + g] for g in range(G)) for h in range(nh)]
        dv_new = [dv_old[h] + sum(dvs[h * G + g] for g in range(G)) for h in range(nh)]
        if has_sink:
            rows = []
            for (h, g), lse_t, delta in zip(units, lse_ts, deltas):
                sk = sink_ref[0, (s_id * nh + h) * G + g]
                rows.append(jnp.broadcast_to(-jnp.sum(jnp.exp(sk - lse_t) * delta, axis=0, keepdims=True), (1, HEAD)))
            rows += [jnp.zeros((1, HEAD), F32)] * (8 - len(rows))
            dsink_new = dsink_old + jnp.concatenate(rows, axis=0)

        for sl, dq in zip(sls, dqs):
            dq_ref[:, sl] = dq
        for h, ks in enumerate(kss):
            dk_ref[pl.ds(start, w), ks] = dk_new[h]
            dv_ref[pl.ds(start, w), ks] = dv_new[h]
        if has_sink:
            dsink_ref[...] = dsink_new

    qspec = pl.BlockSpec((bq, nh * G * HEAD), lambda s, i: (i, s))
    kspec = pl.BlockSpec((m, nh * HEAD), lambda s, i: (0, s))
    in_specs = ([pl.BlockSpec(memory_space=pltpu.SMEM)] if has_sink else []) + [qspec, kspec, kspec, qspec, qspec, qspec]
    in_specs += [qspec] if has_dlse else []
    args = ((sink,) if has_sink else ()) + (q, k, v, o, lse, do) + ((dlse,) if has_dlse else ())
    out_specs = [qspec, kspec, kspec]
    out_shape = [_sds(q.shape, F32), _sds(k.shape, F32), _sds(k.shape, F32)]
    if has_sink:
        out_specs.append(pl.BlockSpec((None, 8, HEAD), lambda s, i: (s, 0, 0)))
        out_shape.append(_sds((seqs // nh, 8, HEAD), F32))
    return _pallas(body, name=name, grid=(seqs // nh, m // bq), in_specs=in_specs, out_specs=out_specs,
                   out_shape=out_shape, compiler_params=_params(("parallel", "arbitrary")))(*args)


def _group_weights(lses):
    mx = jnp.maximum(jnp.maximum(lses[0], lses[1]), lses[2])
    e = [jnp.exp(l - mx) for l in lses]
    tot = e[0] + e[1] + e[2]
    return [t / tot for t in e]


def b_combine_fwd(os_, lses, name):
    n, wd = os_[0].shape
    tm = _tile(n, 512, 8)

    def body(o0, o1, o2, l0, l1, l2, out_ref):
        wts = _group_weights([l0[...], l1[...], l2[...]])
        out_ref[...] = wts[0] * o0[...] + wts[1] * o1[...] + wts[2] * o2[...]

    row = pl.BlockSpec((tm, wd), lambda i: (i, 0))
    return _pallas(body, name=name, grid=(n // tm,), in_specs=[row] * 6, out_specs=row,
                   out_shape=_sds((n, wd), F32), compiler_params=_params(("parallel",)))(*os_, *lses)


def b_combine_bwd(dout, os_, lses, name):
    n, wd = dout.shape
    tm = _tile(n, 256, 8)

    def body(d_ref, o0, o1, o2, l0, l1, l2, do0, do1, do2, dl0, dl1, dl2):
        dv = d_ref[...]
        wts = _group_weights([l0[...], l1[...], l2[...]])
        dws = []
        for o_ref in (o0, o1, o2):
            prod = dv * o_ref[...]
            cols = []
            for h in range(wd // HEAD):
                sseg = jnp.sum(prod[:, h * HEAD:(h + 1) * HEAD], axis=-1, keepdims=True)
                cols.append(jnp.broadcast_to(sseg, (tm, HEAD)))
            dws.append(jnp.concatenate(cols, axis=-1))
        mean = wts[0] * dws[0] + wts[1] * dws[1] + wts[2] * dws[2]
        for wt, dw, do_ref, dl_ref in zip(wts, dws, (do0, do1, do2), (dl0, dl1, dl2)):
            do_ref[...] = wt * dv
            dl_ref[...] = wt * (dw - mean)

    row = pl.BlockSpec((tm, wd), lambda i: (i, 0))
    outs = _pallas(body, name=name, grid=(n // tm,), in_specs=[row] * 7, out_specs=[row] * 6,
                   out_shape=[_sds((n, wd), F32)] * 6, compiler_params=_params(("parallel",)))(dout, *os_, *lses)
    return outs[:3], outs[3:]


def _c_rows(n):
    rows = n // GRID_W
    return rows, min(C_WIN_ROWS, rows)


def _c_row_start(r, rows, wr):
    return jnp.clip(r - wr // 2, 0, rows - wr)


def _c_bias_index(r, rows, wr):
    return _c_row_start(r, rows, wr) - r + (C_WIN_ROWS - 1)


def _col_shift_select(tile, cq, inverse):
    lanes = tile.shape[1]
    for b in range(6):
        amt = (lanes - (1 << b)) if inverse else (1 << b)
        tile = jnp.where(((cq >> b) & 1) == 1, pltpu.roll(tile, amt, 1), tile)
    return tile


def rpb_expand(rwin, name):
    lanes = rwin.shape[-1]

    def body(r_ref, b_ref):
        cq = lax.broadcasted_iota(jnp.int32, (GRID_W, lanes), 0)
        ck = lax.broadcasted_iota(jnp.int32, (GRID_W, lanes), 1) % GRID_W
        cs = jnp.clip(cq - C_WIN_COLS // 2, 0, GRID_W - C_WIN_COLS)
        ok = (ck >= cs) & (ck < cs + C_WIN_COLS)
        for i0 in range(C_WIN_ROWS):
            tile = jnp.broadcast_to(r_ref[i0], (GRID_W, lanes))
            tile = pltpu.roll(tile, lanes - (C_WIN_COLS - 1), 1)
            tile = _col_shift_select(tile, cq, False)
            b_ref[i0] = jnp.where(ok, tile, NEG)

    return _pallas(body, name=name, grid=(C_HEADS,),
                   in_specs=[pl.BlockSpec((None, C_WIN_ROWS, 1, lanes), lambda h: (h, 0, 0, 0))],
                   out_specs=pl.BlockSpec((None, C_WIN_ROWS, GRID_W, lanes), lambda h: (h, 0, 0, 0)),
                   out_shape=_sds((C_HEADS, C_WIN_ROWS, GRID_W, lanes), F32),
                   compiler_params=_params(("parallel",)))(rwin)


def rpb_reduce(dbias, name):
    lanes = dbias.shape[-1]
    wr = lanes // GRID_W

    def body(d_ref, o_ref):
        cq = lax.broadcasted_iota(jnp.int32, (GRID_W, lanes), 0)
        o_ref[...] = jnp.zeros_like(o_ref)
        for i0 in range(C_WIN_ROWS):
            tile = _col_shift_select(d_ref[i0], cq, True)
            tile = pltpu.roll(tile, C_WIN_COLS - 1, 1)
            vec = jnp.sum(tile, axis=0, keepdims=True)
            for w in range(wr):
                o_ref[i0 + w:i0 + w + 1, :] += vec[:, w * GRID_W:(w + 1) * GRID_W]

    return _pallas(body, name=name, grid=(C_HEADS,),
                   in_specs=[pl.BlockSpec((None, C_WIN_ROWS, GRID_W, lanes), lambda h: (h, 0, 0, 0))],
                   out_specs=pl.BlockSpec((None, 16, GRID_W), lambda h: (h, 0, 0)),
                   out_shape=_sds((C_HEADS, 16, GRID_W), F32),
                   compiler_params=_params(("parallel",)))(dbias)


def _store_or_add(ref, val, first):
    @pl.when(first)
    def _():
        ref[...] = val

    @pl.when(jnp.logical_not(first))
    def _():
        ref[...] += val


def c_attn_fwd(q, k, v, bias, name, nh=4):
    n = q.shape[0]
    rows, wr = _c_rows(n)
    wk = wr * GRID_W

    def body(q_ref, k_ref, v_ref, b_ref, o_ref, lse_ref):
        r = pl.program_id(1)
        start = pl.multiple_of(_c_row_start(r, rows, wr) * GRID_W, GRID_W)
        sls = [slice(h * HEAD, (h + 1) * HEAD) for h in range(nh)]
        ss = [lax.dot_general(q_ref[:, sl], k_ref[pl.ds(start, wk), sl], _DIMS["nt"], preferred_element_type=F32)
              * SCALE + b_ref[h] for h, sl in enumerate(sls)]
        mxs = [jnp.max(s, axis=-1, keepdims=True) for s in ss]
        ps = [jnp.exp(s - mx) for s, mx in zip(ss, mxs)]
        dens = [jnp.sum(p, axis=-1, keepdims=True) for p in ps]
        outs = [jnp.dot((p / den).astype(CDT), v_ref[pl.ds(start, wk), sl], preferred_element_type=F32)
                for p, den, sl in zip(ps, dens, sls)]
        for sl, o, mx, den in zip(sls, outs, mxs, dens):
            o_ref[:, sl] = o
            lse_ref[:, sl] = jnp.broadcast_to(mx + jnp.log(den), (GRID_W, HEAD))

    qspec = pl.BlockSpec((GRID_W, nh * HEAD), lambda h, r: (r, h))
    kspec = pl.BlockSpec((n, nh * HEAD), lambda h, r: (0, h))
    bspec = pl.BlockSpec((nh, None, GRID_W, wk), lambda h, r: (h, _c_bias_index(r, rows, wr), 0, 0))
    return _pallas(body, name=name, grid=(C_HEADS // nh, rows), in_specs=[qspec, kspec, kspec, bspec],
                   out_specs=[qspec, qspec], out_shape=[_sds(q.shape, F32), _sds(q.shape, F32)],
                   compiler_params=_params(("parallel", "arbitrary")))(q, k, v, bias)


def c_attn_bwd(q, k, v, bias, o, lse, do, name, nh=2):
    n = q.shape[0]
    rows, wr = _c_rows(n)
    wk = wr * GRID_W

    def body(q_ref, k_ref, v_ref, b_ref, o_ref, lse_ref, do_ref, dq_ref, dk_ref, dv_ref, db_ref):
        r = pl.program_id(1)
        rs = _c_row_start(r, rows, wr)
        start = pl.multiple_of(rs * GRID_W, GRID_W)

        @pl.when(r == 0)
        def _():
            dk_ref[...] = jnp.zeros_like(dk_ref)
            dv_ref[...] = jnp.zeros_like(dv_ref)

        prev = _c_row_start(jnp.maximum(r - 1, 0), rows, wr) - jnp.maximum(r - 1, 0)
        first = (r == 0) | (prev != rs - r)
        sls = [slice(h * HEAD, (h + 1) * HEAD) for h in range(nh)]
        k_ts = [k_ref[pl.ds(start, wk), sl] for sl in sls]
        v_ts = [v_ref[pl.ds(start, wk), sl] for sl in sls]
        q_ts = [q_ref[:, sl] for sl in sls]
        lse_ts = [lse_ref[:, sl][:, :1] for sl in sls]
        do_ts = [do_ref[:, sl] for sl in sls]
        deltas = [jnp.sum(do_t * o_ref[:, sl], axis=-1, keepdims=True) for do_t, sl in zip(do_ts, sls)]
        biases = [b_ref[h] for h in range(nh)]
        dk_old = [dk_ref[pl.ds(start, wk), sl] for sl in sls]
        dv_old = [dv_ref[pl.ds(start, wk), sl] for sl in sls]

        ps = [jnp.exp(lax.dot_general(q_t, k_t, _DIMS["nt"], preferred_element_type=F32) * SCALE + b - lse_t)
              for q_t, k_t, b, lse_t in zip(q_ts, k_ts, biases, lse_ts)]
        do_cs = [do_t.astype(CDT) for do_t in do_ts]
        dps = [lax.dot_general(do_c, v_t, _DIMS["nt"], preferred_element_type=F32) for do_c, v_t in zip(do_cs, v_ts)]
        dss = [p * (dp - delta) for p, dp, delta in zip(ps, dps, deltas)]
        ds_cs = [(ds * SCALE).astype(CDT) for ds in dss]
        dqs = [jnp.dot(ds_c, k_t, preferred_element_type=F32) for ds_c, k_t in zip(ds_cs, k_ts)]
        dv_new = [old + lax.dot_general(p.astype(CDT), do_c, _DIMS["tn"], preferred_element_type=F32)
                  for old, p, do_c in zip(dv_old, ps, do_cs)]
        dk_new = [old + lax.dot_general(ds_c, q_t, _DIMS["tn"], preferred_element_type=F32)
                  for old, ds_c, q_t in zip(dk_old, ds_cs, q_ts)]

        for h, sl in enumerate(sls):
            dq_ref[:, sl] = dqs[h]
            dk_ref[pl.ds(start, wk), sl] = dk_new[h]
            dv_ref[pl.ds(start, wk), sl] = dv_new[h]
        for h in range(nh):
            _store_or_add(db_ref.at[h], dss[h], first)

    qspec = pl.BlockSpec((GRID_W, nh * HEAD), lambda h, r: (r, h))
    kspec = pl.BlockSpec((n, nh * HEAD), lambda h, r: (0, h))
    bspec = pl.BlockSpec((nh, None, GRID_W, wk), lambda h, r: (h, _c_bias_index(r, rows, wr), 0, 0))
    return _pallas(body, name=name, grid=(C_HEADS // nh, rows),
                   in_specs=[qspec, kspec, kspec, bspec, qspec, qspec, qspec],
                   out_specs=[qspec, kspec, kspec, bspec],
                   out_shape=[_sds(q.shape, F32), _sds(k.shape, F32), _sds(k.shape, F32), _sds(bias.shape, F32)],
                   compiler_params=_params(("parallel", "arbitrary")))(q, k, v, bias, o, lse, do)


def _sigmoid(z):
    return 1.0 / (1.0 + jnp.exp(-z))


def _gate_specs(n, d):
    tm, tn = _tile(n, 256, 8), _tile(math.gcd(d, QKV_W), 1024)
    nj = d // tn
    tile = pl.BlockSpec((tm, tn), lambda i, j: (i, j))
    gl = [pl.BlockSpec((tm, tn), functools.partial(lambda i, j, b: (i, (QKV_W + b * d) // tn + j), b=b)) for b in range(3)]
    return tm, tn, nj, tile, gl


def gate_merge(proj, ys, name):
    n, d = ys[0].shape
    tm, tn, nj, tile, gl = _gate_specs(n, d)

    def body(g0, g1, g2, y0, y1, y2, out_ref):
        acc = (_sigmoid(g0[...].astype(F32)) * y0[...] + _sigmoid(g1[...].astype(F32)) * y1[...]
               + _sigmoid(g2[...].astype(F32)) * y2[...])
        out_ref[...] = acc.astype(out_ref.dtype)

    return _pallas(body, name=name, grid=(n // tm, nj), in_specs=gl + [tile] * 3, out_specs=tile,
                   out_shape=_sds((n, d), CDT), compiler_params=_params(("parallel", "parallel")))(proj, proj, proj, *ys)


def gate_bwd(proj, ys, dmerged, name):
    n, d = dmerged.shape
    tm, tn, nj, _, _ = _gate_specs(n, d)

    def body(g_ref, y0, y1, y2, dm_ref, dy_ref, dp_ref):
        b = pl.program_id(2)
        y = jnp.where(b == 0, y0[...], jnp.where(b == 1, y1[...], y2[...]))
        dm = dm_ref[...]
        sg = _sigmoid(g_ref[...].astype(F32))
        dy_ref[...] = (dm * sg).astype(dy_ref.dtype)
        dp_ref[...] = (dm * y * sg * (1.0 - sg)).astype(dp_ref.dtype)

    gl = pl.BlockSpec((tm, tn), lambda i, j, b: (i, QKV_W // tn + b * nj + j))
    tile = pl.BlockSpec((tm, tn), lambda i, j, b: (i, j))
    return _pallas(body, name=name, grid=(n // tm, nj, 3), in_specs=[gl, tile, tile, tile, tile],
                   out_specs=[pl.BlockSpec((None, tm, tn), lambda i, j, b: (b, i, j)), gl],
                   out_shape=[_sds((3, n, d), CDT), _sds(proj.shape, CDT)],
                   compiler_params=_params(("parallel", "parallel", "arbitrary")))(proj, *ys, dmerged)


def gate_up_swiglu(h2, wg, name):
    n, d = h2.shape
    ns = wg.shape[2]
    ff = 2 * ns
    tm, tn = _tile(n, 512, 8), _tile(ns, 1408)
    nj = ns // tn

    def body(a_ref, bg_ref, bu_ref, gu_ref, act_ref):
        a = a_ref[...].astype(CDT)
        gt = jnp.dot(a, bg_ref[...].astype(CDT), preferred_element_type=F32)
        up = jnp.dot(a, bu_ref[...].astype(CDT), preferred_element_type=F32)
        gu_ref[0] = gt.astype(gu_ref.dtype)
        gu_ref[1] = up.astype(gu_ref.dtype)
        act_ref[...] = (gt * _sigmoid(gt) * up).astype(act_ref.dtype)

    return _pallas(body, name=name, grid=(n // tm, 2 * nj),
                   in_specs=[pl.BlockSpec((tm, d), lambda i, j: (i, 0)),
                             pl.BlockSpec((None, d, tn), lambda i, j: (j // nj, 0, j % nj)),
                             pl.BlockSpec((None, d, tn), lambda i, j: (2 + j // nj, 0, j % nj))],
                   out_specs=[pl.BlockSpec((2, tm, tn), lambda i, j: (0, i, j)), pl.BlockSpec((tm, tn), lambda i, j: (i, j))],
                   out_shape=[_sds((2, n, ff), ADT), _sds((n, ff), CDT)],
                   compiler_params=_params(("parallel", "parallel")))(h2, wg, wg)


def d_gate_up(dx, w_down, gu, name):
    n, d = dx.shape
    ff = w_down.shape[0]
    tm, tn = _tile(n, 1024, 8), _tile(ff, 512)

    def body(a_ref, b_ref, gu_ref, d_ref):
        da = lax.dot_general(a_ref[...].astype(CDT), b_ref[...].astype(CDT), _DIMS["nt"], preferred_element_type=F32)
        gt, up = gu_ref[0].astype(F32), gu_ref[1].astype(F32)
        sg = _sigmoid(gt)
        d_ref[0] = (da * up * (sg + gt * sg * (1.0 - sg))).astype(d_ref.dtype)
        d_ref[1] = (da * gt * sg).astype(d_ref.dtype)

    blk = pl.BlockSpec((2, tm, tn), lambda i, j: (0, i, j))
    return _pallas(body, name=name, grid=(n // tm, ff // tn),
                   in_specs=[pl.BlockSpec((tm, d), lambda i, j: (i, 0)), pl.BlockSpec((tn, d), lambda i, j: (j, 0)), blk],
                   out_specs=blk, out_shape=_sds((2, n, ff), CDT),
                   compiler_params=_params(("parallel", "parallel")))(dx, w_down, gu)


def loss_head(y, target, name):
    n, d = y.shape
    tm = _tile(n, 512, 8)
    nsteps = n // tm

    def body(y_ref, t_ref, l_ref, dy_ref, acc_ref):
        i = pl.program_id(0)
        e = y_ref[...] - t_ref[...]
        dy_ref[...] = e * (1.0 / d)
        part = jnp.sum((e * e).reshape(tm // 8, 8, d), axis=0)

        @pl.when(i == 0)
        def _():
            acc_ref[...] = part

        @pl.when(i > 0)
        def _():
            acc_ref[...] += part

        @pl.when(i == nsteps - 1)
        def _():
            tot = jnp.sum(jnp.sum(acc_ref[...], axis=1, keepdims=True), axis=0, keepdims=True) * (0.5 / d)
            l_ref[...] = jnp.broadcast_to(tot, (8, HEAD))

    row = pl.BlockSpec((tm, d), lambda i: (i, 0))
    return _pallas(body, name=name, grid=(nsteps,), in_specs=[row, row],
                   out_specs=[pl.BlockSpec((8, HEAD), lambda i: (0, 0)), row],
                   out_shape=[_sds((8, HEAD), F32), _sds((n, d), F32)],
                   scratch_shapes=[pltpu.VMEM((8, d), F32)],
                   compiler_params=_params(("arbitrary",)))(y, target)


def adamw(w, g, m, v, name):
    r, c = w.shape
    tr = _tile(r, max(8, (1 << 19) // c), 8)
    c1 = 1.0 - ADAM_B1 ** ADAM_STEP
    c2 = 1.0 - ADAM_B2 ** ADAM_STEP

    def body(w_ref, g_ref, m_ref, v_ref, d_ref, mo_ref, vo_ref):
        gv = g_ref[...]
        mn = ADAM_B1 * m_ref[...] + (1.0 - ADAM_B1) * gv
        vn = ADAM_B2 * v_ref[...] + (1.0 - ADAM_B2) * (gv * gv)
        d_ref[...] = -ADAM_LR * ((mn / c1) / (jnp.sqrt(vn / c2) + ADAM_EPS) + ADAM_WD * w_ref[...])
        mo_ref[...] = mn
        vo_ref[...] = vn

    row = pl.BlockSpec((tr, c), lambda i: (i, 0))
    return _pallas(body, name=name, grid=(r // tr,), in_specs=[row] * 4, out_specs=[row] * 3,
                   out_shape=[_sds((r, c), F32)] * 3, compiler_params=_params(("parallel",)))(w, g, m, v)


ANY = pl.BlockSpec(memory_space=pl.ANY)


def _place():
    x, y, c = lax.axis_index("x"), lax.axis_index("y"), lax.axis_index("c")
    return x, y, c, [(1 - x, y), (x, 1 - y), (1 - x, 1 - y)]


def _rcopy(src, dst, send_sems, recv_sems, k, to):
    return pltpu.make_async_remote_copy(src_ref=src, dst_ref=dst, send_sem=send_sems.at[k], recv_sem=recv_sems.at[k],
                                        device_id=to, device_id_type=MESH)


def cast_place(chip_idx, shard, name):
    k, ns = shard.shape
    tr = _tile(k, max(16, (1 << 19) // ns), 16)

    def body(k_ref, s_ref, o_ref):
        o_ref[...] = s_ref[...].astype(o_ref.dtype)

    gs = pltpu.PrefetchScalarGridSpec(
        num_scalar_prefetch=1, grid=(k // tr,),
        in_specs=[pl.BlockSpec((tr, ns), lambda i, k_ref: (i, 0))],
        out_specs=pl.BlockSpec((None, tr, ns), lambda i, k_ref: (k_ref[0], i, 0)))
    return _pallas(body, name=name, grid_spec=gs, out_shape=_sds((N_CHIPS, k, ns), CDT),
                   compiler_params=_params(("parallel",)))(chip_idx, shard)


HBM = pl.BlockSpec(memory_space=pltpu.HBM)
SEM = pl.BlockSpec(memory_space=pltpu.SEMAPHORE)
EFFECT = pltpu.SideEffectType.DATAFLOW_SIDE_EFFECTING


def _in_hbm(a):
    return pltpu.with_memory_space_constraint(a, pltpu.HBM)


def _gather_copies(refs, send_sems, recv_sems):
    x, y, c, chips = _place()
    me = 2 * x + y
    out = []
    for t, ref in enumerate(refs):
        kh = ref.shape[1] // 2
        for j, (px, py) in enumerate(chips):
            send = _rcopy(ref.at[me, pl.ds(c * kh, kh)], ref.at[me, pl.ds(c * kh, kh)], send_sems, recv_sems,
                          3 * t + j, (px, py, c))
            land = ref.at[2 * px + py, pl.ds(c * kh, kh)]
            out.append((send, _rcopy(land, land, send_sems, recv_sems, 3 * t + j, (px, py, c))))
    return out


def gather_start(bufs, after, name):
    nt = len(bufs)

    def body(*refs):
        ins, send_sems, recv_sems, token = refs[:nt], refs[nt + 1], refs[nt + 2], refs[-1]
        for send, _ in _gather_copies(ins, send_sems, recv_sems):
            send.start()
        token[...] = jnp.zeros_like(token)

    outs = _pallas(body, name=name, in_specs=[HBM] * nt + [ANY],
                   out_specs=(SEM, SEM) + (HBM,) * nt + (pl.BlockSpec(memory_space=pltpu.VMEM),),
                   out_shape=(pltpu.SemaphoreType.DMA((3 * nt,)), pltpu.SemaphoreType.DMA((3 * nt,)))
                   + tuple(pltpu.HBM(b.shape, b.dtype) for b in bufs) + (_sds((8, HEAD), F32),),
                   input_output_aliases={t: 2 + t for t in range(nt)},
                   compiler_params=pltpu.CompilerParams(has_side_effects=EFFECT))(*[_in_hbm(b) for b in bufs], after)
    return outs[0], outs[1], list(outs[2:2 + nt]), outs[-1]


def gather_wait(bufs, send_sems, recv_sems, after, name):
    nt = len(bufs)

    def body(*refs):
        ins, s_sems, r_sems = refs[:nt], refs[nt], refs[nt + 1]
        for send, land in _gather_copies(ins, s_sems, r_sems):
            send.wait_send()
            land.wait_recv()

    return _pallas(body, name=name, in_specs=[HBM] * nt + [SEM, SEM, ANY], out_specs=[HBM] * nt,
                   out_shape=[pltpu.HBM(b.shape, b.dtype) for b in bufs],
                   input_output_aliases={t: t for t in range(nt)},
                   compiler_params=pltpu.CompilerParams(has_side_effects=EFFECT))(*bufs, send_sems, recv_sems, after)


def pair_forward(bufs, name):
    nt = len(bufs)

    def body(*refs):
        outs = refs[nt:2 * nt]
        send_sems, recv_sems = refs[2 * nt:]
        x, y, c, chips = _place()
        cps = []
        for t in range(nt):
            kh = outs[t].shape[1] // 2
            for j, (px, py) in enumerate(chips):
                blk = outs[t].at[2 * px + py, pl.ds(c * kh, kh)]
                cps.append(_rcopy(blk, blk, send_sems, recv_sems, 3 * t + j, (x, y, 1 - c)))
                cps[-1].start()
        for t in range(nt):
            kh = outs[t].shape[1] // 2
            for j, (px, py) in enumerate(chips):
                blk = outs[t].at[2 * px + py, pl.ds((1 - c) * kh, kh)]
                _rcopy(blk, blk, send_sems, recv_sems, 3 * t + j, (x, y, 1 - c)).wait_recv()
        for cp in cps:
            cp.wait_send()

    return _pallas(body, name=name, in_specs=[ANY] * nt, out_specs=[ANY] * nt,
                   out_shape=[_sds(b.shape, b.dtype) for b in bufs],
                   input_output_aliases={t: t for t in range(nt)},
                   scratch_shapes=[pltpu.SemaphoreType.DMA((3 * nt,)), pltpu.SemaphoreType.DMA((3 * nt,))],
                   compiler_params=pltpu.CompilerParams(has_side_effects=True))(*bufs)


def pair_exchange(grads, name):
    nt = len(grads)

    def body(*refs):
        ins, outs = refs[:nt], refs[nt:2 * nt]
        send_sems, recv_sems = refs[2 * nt:]
        x, y, c, _ = _place()
        sibling = (x, y, 1 - c)
        cps = []
        for t in range(nt):
            kh = ins[t].shape[1] // 2
            cps.append(_rcopy(ins[t].at[:, pl.ds((1 - c) * kh, kh), :], outs[t], send_sems, recv_sems, t, sibling))
            cps[-1].start()
        for cp in cps:
            cp.wait_recv()
        for cp in cps:
            cp.wait_send()

    return _pallas(body, name=name, in_specs=[ANY] * nt, out_specs=[ANY] * nt,
                   out_shape=[_sds((N_CHIPS, g.shape[1] // 2, g.shape[2]), g.dtype) for g in grads],
                   scratch_shapes=[pltpu.SemaphoreType.DMA((nt,)), pltpu.SemaphoreType.DMA((nt,))],
                   compiler_params=pltpu.CompilerParams(has_side_effects=True))(*grads)


def _exchange_copies(sums, lands, send_sems, recv_sems):
    x, y, c, chips = _place()
    return [_rcopy(s.at[2 * px + py], l.at[j], send_sems, recv_sems, 3 * t + j, (px, py, c))
            for t, (s, l) in enumerate(zip(sums, lands)) for j, (px, py) in enumerate(chips)]


def exchange_start(sums, name):
    nt = len(sums)
    lands = [lax.empty((3,) + s.shape[1:], s.dtype) for s in sums]

    def body(*refs):
        ins, zones, send_sems, recv_sems, token = refs[:nt], refs[nt:2 * nt], refs[2 * nt], refs[2 * nt + 1], refs[-1]
        for cp in _exchange_copies(ins, zones, send_sems, recv_sems):
            cp.start()
        token[...] = jnp.zeros_like(token)

    outs = _pallas(body, name=name, in_specs=[HBM] * (2 * nt),
                   out_specs=(SEM, SEM) + (HBM,) * (2 * nt) + (pl.BlockSpec(memory_space=pltpu.VMEM),),
                   out_shape=(pltpu.SemaphoreType.DMA((3 * nt,)), pltpu.SemaphoreType.DMA((3 * nt,)))
                   + tuple(pltpu.HBM(a.shape, a.dtype) for a in list(sums) + lands) + (_sds((8, HEAD), F32),),
                   input_output_aliases={t: 2 + t for t in range(2 * nt)},
                   compiler_params=pltpu.CompilerParams(has_side_effects=EFFECT))(*[_in_hbm(a) for a in list(sums) + lands])
    return outs[0], outs[1], list(outs[2:2 + nt]), list(outs[2 + nt:2 + 2 * nt]), outs[-1]


def exchange_wait(sums, lands, send_sems, recv_sems, after, name):
    nt = len(sums)

    def body(*refs):
        ins, zones, s_sems, r_sems = refs[:nt], refs[nt:2 * nt], refs[2 * nt], refs[2 * nt + 1]
        for cp in _exchange_copies(ins, zones, s_sems, r_sems):
            cp.wait_send()
            cp.wait_recv()

    outs = _pallas(body, name=name, in_specs=[HBM] * (2 * nt) + [SEM, SEM, ANY], out_specs=[HBM] * (2 * nt),
                   out_shape=[pltpu.HBM(a.shape, a.dtype) for a in list(sums) + list(lands)],
                   input_output_aliases={t: t for t in range(2 * nt)},
                   compiler_params=pltpu.CompilerParams(has_side_effects=EFFECT))(*sums, *lands, send_sems, recv_sems, after)
    return list(outs[:nt]), list(outs[nt:])


def pair_share(halves, name):
    nt = len(halves)

    def body(*refs):
        ins, outs = refs[:nt], refs[nt:2 * nt]
        send_sems, recv_sems = refs[2 * nt:]
        x, y, c, _ = _place()
        cps = []
        for t in range(nt):
            cps.append(_rcopy(ins[t], outs[t], send_sems, recv_sems, t, (x, y, 1 - c)))
            cps[-1].start()
        for cp in cps:
            cp.wait_recv()
        for cp in cps:
            cp.wait_send()

    return _pallas(body, name=name, in_specs=[ANY] * nt, out_specs=[ANY] * nt,
                   out_shape=[_sds(h.shape, h.dtype) for h in halves],
                   scratch_shapes=[pltpu.SemaphoreType.DMA((nt,)), pltpu.SemaphoreType.DMA((nt,))],
                   compiler_params=pltpu.CompilerParams(has_side_effects=True))(*halves)


def small_allreduce(pack, name):
    r = pack.shape[0]

    def body(in_ref, out_ref, buf, send_sems, recv_sems):
        x, y, c, _ = _place()
        me = 4 * x + 2 * y + c
        sends = []
        for k in range(1, 8):
            to = ((x + ((k >> 2) & 1)) % 2, (y + ((k >> 1) & 1)) % 2, (c + (k & 1)) % 2)
            cp = _rcopy(in_ref, buf.at[me], send_sems, recv_sems, k - 1, to)
            cp.start()
            sends.append((cp, to))
        buf[pl.ds(me, 1)] = in_ref[...][None]
        for k, (_, to) in enumerate(sends):
            peer = 4 * to[0] + 2 * to[1] + to[2]
            _rcopy(in_ref, buf.at[peer], send_sems, recv_sems, k, to).wait_recv()
        for cp, _ in sends:
            cp.wait_send()
        acc = buf[0]
        for d in range(1, 8):
            acc = acc + buf[d]
        out_ref[...] = acc

    vm = pl.BlockSpec(memory_space=pltpu.VMEM)
    return _pallas(body, name=name, in_specs=[vm], out_specs=vm, out_shape=_sds((r, HEAD), F32),
                   scratch_shapes=[pltpu.VMEM((8, r, HEAD), F32), pltpu.SemaphoreType.DMA((7,)),
                                   pltpu.SemaphoreType.DMA((7,))],
                   compiler_params=pltpu.CompilerParams(has_side_effects=True))(pack)


def add_halves(c_idx, grad, other, name):
    _, k, ns = grad.shape
    kh = k // 2
    tr = _tile(kh, max(16, (1 << 19) // ns), 16)
    nr = kh // tr

    def body(c_ref, g_ref, o_ref, s_ref):
        s_ref[...] = (g_ref[...].astype(F32) + o_ref[...].astype(F32)).astype(s_ref.dtype)

    gs = pltpu.PrefetchScalarGridSpec(
        num_scalar_prefetch=1, grid=(N_CHIPS, nr),
        in_specs=[pl.BlockSpec((None, tr, ns), lambda g, i, c_ref: (g, c_ref[0] * nr + i, 0)),
                  pl.BlockSpec((None, tr, ns), lambda g, i, c_ref: (g, i, 0))],
        out_specs=pl.BlockSpec((None, tr, ns), lambda g, i, c_ref: (g, i, 0)))
    return _pallas(body, name=name, grid_spec=gs, out_shape=_sds((N_CHIPS, kh, ns), XDT),
                   compiler_params=_params(("parallel", "parallel")))(c_idx, grad, other)


def add_chips(chip_idx, sums, recv, stack, layer, n_layers, name):
    _, kh, ns = sums.shape
    tr = _tile(kh, max(16, (1 << 19) // ns), 16)
    has_stack = stack is not None

    def body(k_ref, s_ref, r0, r1, r2, *rest):
        o_ref = rest[-1]
        o_ref[...] = ((s_ref[...].astype(F32) + r0[...].astype(F32)) + r1[...].astype(F32)) + r2[...].astype(F32)

    rspec = [pl.BlockSpec((None, tr, ns), functools.partial(lambda i, k_ref, j: (j, i, 0), j=j)) for j in range(3)]
    gs = pltpu.PrefetchScalarGridSpec(
        num_scalar_prefetch=1, grid=(kh // tr,),
        in_specs=[pl.BlockSpec((None, tr, ns), lambda i, k_ref: (k_ref[0], i, 0))] + rspec + ([ANY] if has_stack else []),
        out_specs=pl.BlockSpec((None, tr, ns), lambda i, k_ref: (layer, i, 0)))
    args = (chip_idx, sums, recv, recv, recv) + ((stack,) if has_stack else ())
    return _pallas(body, name=name, grid_spec=gs, out_shape=_sds((n_layers, kh, ns), F32),
                   input_output_aliases={5: 0} if has_stack else {},
                   compiler_params=_params(("parallel",)))(*args)


def adamw_big(c_idx, w, m, v, mine, other, name):
    nl, k, ns = w.shape
    kh = k // 2
    tr = _tile(kh, max(8, (1 << 18) // ns), 8)
    nr = kh // tr
    c1 = 1.0 - ADAM_B1 ** ADAM_STEP
    c2 = 1.0 - ADAM_B2 ** ADAM_STEP

    def body(c_ref, w_ref, m_ref, v_ref, a_ref, b_ref, g_ref, d_ref, mo_ref, vo_ref):
        gv = jnp.where(pl.program_id(2) == c_ref[0], a_ref[...], b_ref[...])
        mn = ADAM_B1 * m_ref[...] + (1.0 - ADAM_B1) * gv
        vn = ADAM_B2 * v_ref[...] + (1.0 - ADAM_B2) * (gv * gv)
        g_ref[...] = gv
        d_ref[...] = -ADAM_LR * ((mn / c1) / (jnp.sqrt(vn / c2) + ADAM_EPS) + ADAM_WD * w_ref[...])
        mo_ref[...] = mn
        vo_ref[...] = vn

    full = pl.BlockSpec((None, tr, ns), lambda l, i, hh, c_ref: (l, hh * nr + i, 0))
    half = pl.BlockSpec((None, tr, ns), lambda l, i, hh, c_ref: (l, i, 0))
    gs = pltpu.PrefetchScalarGridSpec(num_scalar_prefetch=1, grid=(nl, nr, 2),
                                      in_specs=[full, full, full, half, half], out_specs=[full] * 4)
    return _pallas(body, name=name, grid_spec=gs, out_shape=[_sds(w.shape, F32)] * 4,
                   compiler_params=_params(("parallel", "parallel", "arbitrary")))(c_idx, w, m, v, mine, other)


W_NAMES = ("w_in", "w_br_a", "w_br_b", "w_br_c", "w_o", "w_gate_up", "w_down")


def _rope_tables(n):
    half = HEAD // 2
    inv_freq = ROPE_THETA ** (-jnp.arange(half, dtype=F32) * 2.0 / HEAD)
    ang = jnp.arange(n, dtype=F32)[:, None] * inv_freq[None, :]
    cos, sin = jnp.cos(ang), jnp.sin(ang)
    return jnp.concatenate([cos, cos], axis=-1), jnp.concatenate([-sin, sin], axis=-1)


def _rpb_windows(rpb):
    pad = jnp.pad(rpb, ((0, 0), (0, 1), (0, GRID_W - rpb.shape[2])))
    wins = [pad[:, i0:i0 + C_WIN_ROWS].reshape(C_HEADS, 1, C_WIN_ROWS * GRID_W) for i0 in range(C_WIN_ROWS)]
    return jnp.stack(wins, axis=1)


def _b_view(t, dil):
    n, wd = t.shape
    return t.reshape(n // dil, dil * wd)


def layer_fwd(x, p, w, cos2, sin2, rest=None):
    n, d = x.shape
    s = {"x": x}
    s["h"] = rmsnorm_fwd(x, p["norm1_g"], "norm1")
    s["proj"] = mm_x_wcol(s["h"], w["w_in"], ADT, "proj")
    gains = jnp.pad(p["qk_norm_g"], ((0, 2), (0, 0)))
    pp = s["pp"] = qk_prep(s["proj"], gains, cos2, sin2, "qk_prep")
    sink = p["sink_a"].reshape(1, A_Q_HEADS)
    s["oa"], s["lse_a"] = band_attn_fwd(pp["qa"], pp["ka"], pp["va"], sink, seqs=A_KV_HEADS, G=A_GROUP,
                                        nh=A_KV_HEADS, radius=A_RADIUS, name="attn_a")
    s["ob"], s["lse_b"] = [], []
    for g, dil in enumerate(B_DILS):
        o, lse = band_attn_fwd(_b_view(pp[f"qb{g}"], dil), _b_view(pp[f"kb{g}"], dil), _b_view(pp[f"vb{g}"], dil),
                               None, seqs=dil * B_HG, G=1, nh=B_HG, radius=B_RADIUS, name=f"attn_b{g}")
        s["ob"].append(o.reshape(n, B_HG * HEAD))
        s["lse_b"].append(lse.reshape(n, B_HG * HEAD))
    ob = b_combine_fwd(s["ob"], s["lse_b"], "b_combine")
    s["bias"] = rpb_expand(_rpb_windows(p["rpb_c"]), "rpb_expand")
    s["oc"], s["lse_c"] = c_attn_fwd(pp["qc"], pp["kc"], pp["vc"], s["bias"], "attn_c")
    s["o_in"] = (s["oa"], ob, s["oc"])
    if rest is not None:
        w = {**w, **rest(s["oc"])}
    s["w"] = w
    s["ys"] = [mm_x_wcol(o, w[k], F32, "branch_" + k[-1]) for o, k in zip(s["o_in"], ("w_br_a", "w_br_b", "w_br_c"))]
    s["merged"] = gate_merge(s["proj"], s["ys"], "gate_merge")
    s["x_mid"] = mm_x_w(s["merged"], w["w_o"], "out_proj", res=x)
    s["h2"] = rmsnorm_fwd(s["x_mid"], p["norm2_g"], "norm2")
    s["gu"], s["act"] = gate_up_swiglu(s["h2"], w["w_gate_up"], "gate_up")
    x_out = mm_x_w(s["act"], w["w_down"], "down", res=s["x_mid"], tk_pref=2816)
    return x_out, s


def layer_bwd(dx_out, s, p, cos2, sin2):
    n, d = dx_out.shape
    pp, w = s["pp"], s["w"]
    dgu = d_gate_up(dx_out, w["w_down"], s["gu"], "d_gate_up")
    dw_down = mm_aT_d(s["act"], dx_out, "dw_down")
    dh2 = mm_x_wcolT(dgu, w["w_gate_up"], "d_h2", stacked_in=2)
    dw_gu = mm_aT_d_wcol(s["h2"], dgu, "dw_gate_up", tn_pref=1408, stacked_in=2)
    dx_mid, dg2 = rmsnorm_bwd(s["x_mid"], p["norm2_g"], dh2, dx_out, "norm2_bwd")

    dmerged = mm_x_wT(dx_mid, w["w_o"], "d_merged")
    dw_o = mm_aT_d(s["merged"], dx_mid, "dw_o")
    dys, dproj = gate_bwd(s["proj"], s["ys"], dmerged, "gate_bwd")
    dos, dw_br = [], []
    for b, (o, k) in enumerate(zip(s["o_in"], ("w_br_a", "w_br_b", "w_br_c"))):
        dos.append(mm_x_wcolT(dys[b], w[k], "d_o_" + k[-1]))
        dw_br.append(mm_aT_d_wcol(o, dys[b], "dw_br_" + k[-1]))

    grads = {}
    sink = p["sink_a"].reshape(1, A_Q_HEADS)
    grads["qa"], grads["ka"], grads["va"], dsink = band_attn_bwd(
        pp["qa"], pp["ka"], pp["va"], sink, s["oa"], s["lse_a"], dos[0], None,
        seqs=A_KV_HEADS, G=A_GROUP, nh=A_KV_HEADS, radius=A_RADIUS, name="attn_a_bwd")
    dobs, dlses = b_combine_bwd(dos[1], s["ob"], s["lse_b"], "b_combine_bwd")
    for g, dil in enumerate(B_DILS):
        dq, dk, dv = band_attn_bwd(_b_view(pp[f"qb{g}"], dil), _b_view(pp[f"kb{g}"], dil), _b_view(pp[f"vb{g}"], dil),
                                   None, _b_view(s["ob"][g], dil), _b_view(s["lse_b"][g], dil), _b_view(dobs[g], dil),
                                   _b_view(dlses[g], dil), seqs=dil * B_HG, G=1, nh=2, radius=B_RADIUS,
                                   name=f"attn_b{g}_bwd")
        grads[f"qb{g}"], grads[f"kb{g}"], grads[f"vb{g}"] = [t.reshape(n, B_HG * HEAD) for t in (dq, dk, dv)]
    grads["qc"], grads["kc"], grads["vc"], dbias = c_attn_bwd(pp["qc"], pp["kc"], pp["vc"], s["bias"], s["oc"],
                                                              s["lse_c"], dos[2], "attn_c_bwd")
    drpb = rpb_reduce(dbias, "rpb_reduce")[:, :2 * C_WIN_ROWS - 1, :2 * C_WIN_COLS - 1]
    gains = jnp.pad(p["qk_norm_g"], ((0, 2), (0, 0)))
    dproj, dgains = qk_prep_bwd(s["proj"], gains, cos2, sin2, grads, dproj, "qk_prep_bwd")
    dh = mm_x_wcolT(dproj, w["w_in"], "d_h")
    dw_in = mm_aT_d_wcol(s["h"], dproj, "dw_in")
    dx_in, dg1 = rmsnorm_bwd(s["x"], p["norm1_g"], dh, dx_mid, "norm1_bwd")

    dws = [dw_in] + dw_br + [dw_o.reshape(N_CHIPS, d // N_CHIPS, d), dw_gu,
                             dw_down.reshape(N_CHIPS, dw_down.shape[0] // N_CHIPS, d)]
    small = {"norm1_g": dg1[0], "qk_norm_g": dgains[:6], "sink_a": dsink[0, :, 0],
             "rpb_c": drpb, "norm2_g": dg2[0]}
    return dx_in, dws, small


SMALL_NAMES = ("norm1_g", "qk_norm_g", "sink_a", "rpb_c", "norm2_g")


def _pack_small(parts, extra=None):
    flat = [parts[k].reshape(-1) for k in SMALL_NAMES]
    flat.append(jnp.zeros((1,), F32) if extra is None else extra.reshape(1))
    v = jnp.concatenate(flat)
    rows = -(-v.shape[0] // (8 * HEAD)) * 8
    return jnp.pad(v, (0, rows * HEAD - v.shape[0])).reshape(rows, HEAD)


def _unpack_small(pack, like):
    v = pack.reshape(-1)
    out, off = {}, 0
    for k in SMALL_NAMES:
        size = math.prod(like[k].shape)
        out[k] = v[off:off + size].reshape(like[k].shape)
        off += size
    return out, v[off]


def kernel(x, norm1_g, w_in, qk_norm_g, sink_a, rpb_c, w_br_a, w_br_b, w_br_c, w_o, norm2_g, w_gate_up, w_down, loss_target, m_norm1_g, m_w_in, m_qk_norm_g, m_sink_a, m_rpb_c, m_w_br_a, m_w_br_b, m_w_br_c, m_w_o, m_norm2_g, m_w_gate_up, m_w_down, v_norm1_g, v_w_in, v_qk_norm_g, v_sink_a, v_rpb_c, v_w_br_a, v_w_br_b, v_w_br_c, v_w_o, v_norm2_g, v_w_gate_up, v_w_down):
    big = dict(w_in=w_in, w_br_a=w_br_a, w_br_b=w_br_b, w_br_c=w_br_c, w_o=w_o, w_gate_up=w_gate_up, w_down=w_down)
    big_m = dict(w_in=m_w_in, w_br_a=m_w_br_a, w_br_b=m_w_br_b, w_br_c=m_w_br_c, w_o=m_w_o, w_gate_up=m_w_gate_up, w_down=m_w_down)
    big_v = dict(w_in=v_w_in, w_br_a=v_w_br_a, w_br_b=v_w_br_b, w_br_c=v_w_br_c, w_o=v_w_o, w_gate_up=v_w_gate_up, w_down=v_w_down)
    small = dict(norm1_g=norm1_g, qk_norm_g=qk_norm_g, sink_a=sink_a, rpb_c=rpb_c, norm2_g=norm2_g)
    small_m = dict(norm1_g=m_norm1_g, qk_norm_g=m_qk_norm_g, sink_a=m_sink_a, rpb_c=m_rpb_c, norm2_g=m_norm2_g)
    small_v = dict(norm1_g=v_norm1_g, qk_norm_g=v_qk_norm_g, sink_a=v_sink_a, rpb_c=v_rpb_c, norm2_g=v_norm2_g)
    n_layers = w_in.shape[0]
    n, d = x.shape[1], x.shape[2]
    c_idx = lax.axis_index("c").astype(jnp.int32).reshape(1)
    chip_idx = (2 * lax.axis_index("x") + lax.axis_index("y")).astype(jnp.int32).reshape(1)
    cos2, sin2 = _rope_tables(n)

    def gathered(names, started, after):
        send_sems, recv_sems, bufs, _ = started
        got = pair_forward(gather_wait(bufs, send_sems, recv_sems, after, "gather_wait"), "pair_forward")
        w = dict(zip(names, got))
        if "w_o" in w:
            w["w_o"] = w["w_o"].reshape(d, d)
            w["w_down"] = w["w_down"].reshape(-1, d)
        return w

    def start_gather(names, l, after):
        return gather_start([cast_place(chip_idx, big[k][l], "cast_" + k) for k in names], after, "gather_start")

    first, others_0 = W_NAMES[:1], W_NAMES[1:]
    started = start_gather(first, 0, chip_idx)
    started_rest = start_gather(others_0, 0, started[3])
    weights = gathered(first, started, started_rest[3])
    rest = lambda after: gathered(others_0, started_rest, after)

    xs, saved = x[0], []
    for l in range(n_layers):
        p = {k: small[k][l] for k in SMALL_NAMES}
        if l + 1 < n_layers:
            started = start_gather(W_NAMES, l + 1, weights["w_in"])
            p["norm1_g"] = p["norm1_g"] + started[3][0, 0]
        xs, s = layer_fwd(xs, p, weights, cos2, sin2, rest)
        saved.append(s)
        if l + 1 < n_layers:
            weights, rest = gathered(W_NAMES, started, xs), None
    loss_tile, dx = loss_head(xs, loss_target[0], "loss_head")

    halves = [None] * len(W_NAMES)
    small_g = [None] * n_layers

    def finish_exchange(pending, after, halves):
        l, (send_sems, recv_sems, sums, lands, _) = pending
        sums, from_chips = exchange_wait(sums, lands, send_sems, recv_sems, after, "exchange_wait")
        return [add_chips(chip_idx, sm, r, st, l, n_layers, "add_chips_" + k)
                for sm, r, st, k in zip(sums, from_chips, halves, W_NAMES)]

    pending = None
    for l in reversed(range(n_layers)):
        p = {k: small[k][l] for k in SMALL_NAMES}
        dx, dws, small_g[l] = layer_bwd(dx, saved[l], p, cos2, sin2)
        if pending is not None:
            halves = finish_exchange(pending, dx, halves)
        from_sibling = pair_exchange(dws, "pair_exchange")
        sums = [add_halves(c_idx, g, o, "add_halves_" + k) for g, o, k in zip(dws, from_sibling, W_NAMES)]
        pending = (l, exchange_start(sums, "exchange_start"))
        dx = dx + pending[1][4][0, 0]
    halves = finish_exchange(pending, dx, halves)
    others = pair_share(halves, "pair_share")

    mine = {k: jnp.stack([small_g[l][k] for l in range(n_layers)]) for k in SMALL_NAMES}
    total = small_allreduce(_pack_small(mine, loss_tile[0, 0]), "small_allreduce")
    grad_small, loss = _unpack_small(total, small)

    outs = {}
    for k, mine_half, other_half in zip(W_NAMES, halves, others):
        outs[k] = adamw_big(c_idx, big[k], big_m[k], big_v[k], mine_half, other_half, "adamw_" + k)
    res = adamw(_pack_small(small), _pack_small(grad_small), _pack_small(small_m), _pack_small(small_v), "adamw_small")
    unp = [_unpack_small(t, small)[0] for t in res]
    for k in SMALL_NAMES:
        outs[k] = (grad_small[k],) + tuple(u[k] for u in unp)

    order = ("norm1_g", "w_in", "qk_norm_g", "sink_a", "rpb_c", "w_br_a", "w_br_b", "w_br_c", "w_o", "norm2_g",
             "w_gate_up", "w_down")
    return (loss, dx[None]) + tuple(outs[k][i] for i in range(4) for k in order)
```

```python
import functools
import math

import jax
import jax.numpy as jnp
from jax import lax
from jax.experimental import pallas as pl
from jax.experimental.pallas import tpu as pltpu

F32 = jnp.float32
CDT = jnp.bfloat16
XDT = jnp.bfloat16
ADT = jnp.bfloat16

HEAD = 128
NORM_EPS = 1e-6
ROPE_THETA = 10000.0
A_Q_HEADS, A_KV_HEADS, A_GROUP, A_RADIUS = 8, 2, 4, 128
B_DILS = (1, 4, 16)
B_RADIUS = 64
B_HG = 4
C_HEADS, GRID_W, C_WIN_ROWS, C_WIN_COLS = 8, 64, 8, 16
QKV_W = 9216
COL = dict(qa=0, ka=1024, va=1280, qb=1536, kb=3072, vb=4608, qc=6144, kc=7168, vc=8192)
NEG = -1e30
SCALE = HEAD ** -0.5
N_CHIPS = 4

ADAM_LR, ADAM_B1, ADAM_B2, ADAM_EPS, ADAM_WD, ADAM_STEP = 0.001, 0.9, 0.999, 1e-08, 0.01, 10

VMEM_LIMIT = 56 * 1024 * 1024
MESH = pl.DeviceIdType.MESH


def _pallas(body, **kw):
    return pl.pallas_call(body, **kw)


def _params(sem=None, **kw):
    if sem is not None:
        kw["dimension_semantics"] = sem
    return pltpu.CompilerParams(vmem_limit_bytes=VMEM_LIMIT, **kw)


def _tile(dim, pref, mult=128):
    best = None
    for t in range(mult, min(dim, pref) + 1, mult):
        if dim % t == 0:
            best = t
    return dim if best is None else best


def _sds(shape, dtype):
    return jax.ShapeDtypeStruct(tuple(shape), dtype)


_DIMS = {"nn": (((1,), (0,)), ((), ())), "nt": (((1,), (1,)), ((), ())), "tn": (((0,), (0,)), ((), ()))}


def _matmul(a, b, *, mode, grid, a_spec, b_spec, o_spec, out_shape, acc_shape, name, res=None, res_spec=None,
            after=None):
    nk = grid[2]
    has_res = res is not None
    n_in = 2 + int(has_res) + int(after is not None)

    def body(*refs):
        a_ref, b_ref = refs[:2]
        r_ref = refs[2] if has_res else None
        o_ref, rest = refs[n_in], refs[n_in + 1:]
        p = lax.dot_general(a_ref[...].astype(CDT), b_ref[...].astype(CDT), _DIMS[mode],
                            preferred_element_type=F32)

        def finish(acc):
            if has_res:
                acc = acc + r_ref[...].astype(F32)
            o_ref[...] = acc.astype(o_ref.dtype)

        if nk == 1:
            finish(p)
        else:
            acc_ref = rest[0]
            k = pl.program_id(2)

            @pl.when(k == 0)
            def _():
                acc_ref[...] = p

            @pl.when(k > 0)
            def _():
                acc_ref[...] += p

            @pl.when(k == nk - 1)
            def _():
                finish(acc_ref[...])

    in_specs = [a_spec, b_spec] + ([res_spec] if has_res else [])
    args = (a, b) + ((res,) if has_res else ())
    if after is not None:
        in_specs.append(pl.BlockSpec(after.shape, lambda i, j, kk: (0, 0)))
        args += (after,)
    scratch = [] if nk == 1 else [pltpu.VMEM(acc_shape, F32)]
    return _pallas(body, name=name, grid=grid, in_specs=in_specs, out_specs=o_spec, out_shape=out_shape,
                   scratch_shapes=scratch, compiler_params=_params(("parallel", "parallel", "arbitrary")))(*args)


def mm_x_wcol(a, wg, out_dtype, name, tm_pref=1024, tn_pref=1024, stacked_out=1):
    m, k = a.shape
    ns = wg.shape[2]
    tm, tn = _tile(m, tm_pref, 8), _tile(ns, tn_pref)
    nj = ns // tn
    grid = (m // tm, N_CHIPS * nj, 1)
    a_spec = pl.BlockSpec((tm, k), lambda i, j, kk: (i, 0))
    b_spec = pl.BlockSpec((None, k, tn), lambda i, j, kk: (j // nj, 0, j % nj))
    if stacked_out == 1:
        o_spec = pl.BlockSpec((tm, tn), lambda i, j, kk: (i, j))
        out_shape = _sds((m, N_CHIPS * ns), out_dtype)
    else:
        per = N_CHIPS * nj // stacked_out
        o_spec = pl.BlockSpec((None, tm, tn), lambda i, j, kk: (j // per, i, j % per))
        out_shape = _sds((stacked_out, m, N_CHIPS * ns // stacked_out), out_dtype)
    return _matmul(a, wg, mode="nn", grid=grid, a_spec=a_spec, b_spec=b_spec, o_spec=o_spec,
                   out_shape=out_shape, acc_shape=(tm, tn), name=name)


def mm_x_wcolT(d, wg, name, res=None, tm_pref=1024, tn_pref=512, tk_pref=4096, stacked_in=1, lead=None, after=None):
    kdim, ns = wg.shape[1], wg.shape[2]
    m = d.shape[-2]
    tm, tn, tk = _tile(m, tm_pref, 8), _tile(kdim, tn_pref), _tile(ns, tk_pref)
    nkk = ns // tk
    grid = (m // tm, kdim // tn, N_CHIPS * nkk)
    if lead is not None:
        a_spec = pl.BlockSpec((None, tm, tk), lambda i, j, kk: (lead, i, kk))
    elif stacked_in == 1:
        a_spec = pl.BlockSpec((tm, tk), lambda i, j, kk: (i, kk))
    else:
        per = N_CHIPS * nkk // stacked_in
        a_spec = pl.BlockSpec((None, tm, tk), lambda i, j, kk: (kk // per, i, kk % per))
    b_spec = pl.BlockSpec((None, tn, tk), lambda i, j, kk: (kk // nkk, j, kk % nkk))
    o_spec = pl.BlockSpec((tm, tn), lambda i, j, kk: (i, j))
    return _matmul(d, wg, mode="nt", grid=grid, a_spec=a_spec, b_spec=b_spec, o_spec=o_spec,
                   out_shape=_sds((m, kdim), F32), acc_shape=(tm, tn), name=name,
                   res=res, res_spec=None if res is None else o_spec, after=after)


def mm_aT_d_wcol(a, d, name, tm_pref=512, tk_pref=4096, tn_pref=1024, stacked_in=1, lead=None):
    m, kdim = a.shape
    ntot = d.shape[-1] * stacked_in
    ns = ntot // N_CHIPS
    tm, tkm, tn = _tile(kdim, tm_pref), _tile(m, tk_pref, 8), _tile(ns, tn_pref)
    nj = ns // tn
    grid = (kdim // tm, N_CHIPS * nj, m // tkm)
    a_spec = pl.BlockSpec((tkm, tm), lambda i, j, kk: (kk, i))
    if lead is not None:
        b_spec = pl.BlockSpec((None, tkm, tn), lambda i, j, kk: (lead, kk, j))
    elif stacked_in == 1:
        b_spec = pl.BlockSpec((tkm, tn), lambda i, j, kk: (kk, j))
    else:
        per = N_CHIPS * nj // stacked_in
        b_spec = pl.BlockSpec((None, tkm, tn), lambda i, j, kk: (j // per, kk, j % per))
    o_spec = pl.BlockSpec((None, tm, tn), lambda i, j, kk: (j // nj, i, j % nj))
    return _matmul(a, d, mode="tn", grid=grid, a_spec=a_spec, b_spec=b_spec, o_spec=o_spec,
                   out_shape=_sds((N_CHIPS, kdim, ns), XDT), acc_shape=(tm, tn), name=name)


def mm_x_w(a, w, name, res=None, out_dtype=F32, tm_pref=1024, tn_pref=1024, tk_pref=2048):
    m, k = a.shape
    n = w.shape[1]
    tm, tn, tk = _tile(m, tm_pref, 8), _tile(n, tn_pref), _tile(k, tk_pref)
    grid = (m // tm, n // tn, k // tk)
    o_spec = pl.BlockSpec((tm, tn), lambda i, j, kk: (i, j))
    return _matmul(a, w, mode="nn", grid=grid,
                   a_spec=pl.BlockSpec((tm, tk), lambda i, j, kk: (i, kk)),
                   b_spec=pl.BlockSpec((tk, tn), lambda i, j, kk: (kk, j)),
                   o_spec=o_spec, out_shape=_sds((m, n), out_dtype), acc_shape=(tm, tn), name=name,
                   res=res, res_spec=None if res is None else o_spec)


def mm_x_wT(d, w, name, out_dtype=F32, tm_pref=1024, tn_pref=1024):
    m, n = d.shape
    k = w.shape[0]
    tm, tn = _tile(m, tm_pref, 8), _tile(k, tn_pref)
    grid = (m // tm, k // tn, 1)
    return _matmul(d, w, mode="nt", grid=grid,
                   a_spec=pl.BlockSpec((tm, n), lambda i, j, kk: (i, 0)),
                   b_spec=pl.BlockSpec((tn, n), lambda i, j, kk: (j, 0)),
                   o_spec=pl.BlockSpec((tm, tn), lambda i, j, kk: (i, j)),
                   out_shape=_sds((m, k), out_dtype), acc_shape=(tm, tn), name=name)


def mm_aT_d(a, d, name, tm_pref=512, tn_pref=512, tk_pref=4096):
    m, k = a.shape
    n = d.shape[1]
    tm, tn, tk = _tile(k, tm_pref), _tile(n, tn_pref), _tile(m, tk_pref, 8)
    grid = (k // tm, n // tn, m // tk)
    return _matmul(a, d, mode="tn", grid=grid,
                   a_spec=pl.BlockSpec((tk, tm), lambda i, j, kk: (kk, i)),
                   b_spec=pl.BlockSpec((tk, tn), lambda i, j, kk: (kk, j)),
                   o_spec=pl.BlockSpec((tm, tn), lambda i, j, kk: (i, j)),
                   out_shape=_sds((k, n), XDT), acc_shape=(tm, tn), name=name)


def rmsnorm_fwd(x, g, name):
    n, d = x.shape
    tm = _tile(n, 512, 8)

    def body(x_ref, g_ref, h_ref):
        xv = x_ref[...]
        r = lax.rsqrt(jnp.mean(xv * xv, axis=-1, keepdims=True) + NORM_EPS)
        h_ref[...] = (xv * r * g_ref[...]).astype(h_ref.dtype)

    return _pallas(body, name=name, grid=(n // tm,),
                   in_specs=[pl.BlockSpec((tm, d), lambda i: (i, 0)), pl.BlockSpec((1, d), lambda i: (0, 0))],
                   out_specs=pl.BlockSpec((tm, d), lambda i: (i, 0)), out_shape=_sds((n, d), CDT),
                   compiler_params=_params(("parallel",)))(x, g.reshape(1, d))


def rmsnorm_bwd(x, g, dh, dres, name):
    n, d = x.shape
    tm = _tile(n, 256, 8)

    def body(x_ref, g_ref, dh_ref, dres_ref, dx_ref, dg_ref):
        xv = x_ref[...]
        r = lax.rsqrt(jnp.mean(xv * xv, axis=-1, keepdims=True) + NORM_EPS)
        dhv = dh_ref[...]
        u = dhv * g_ref[...]
        c = jnp.mean(xv * u, axis=-1, keepdims=True)
        dx_ref[...] = dres_ref[...] + r * u - xv * (r * r * r * c)
        part = jnp.broadcast_to(jnp.sum(dhv * xv * r, axis=0, keepdims=True), (8, d))

        @pl.when(pl.program_id(0) == 0)
        def _():
            dg_ref[...] = part

        @pl.when(pl.program_id(0) > 0)
        def _():
            dg_ref[...] += part

    row = pl.BlockSpec((tm, d), lambda i: (i, 0))
    dx, dg = _pallas(body, name=name, grid=(n // tm,),
                     in_specs=[row, pl.BlockSpec((1, d), lambda i: (0, 0)), row, row],
                     out_specs=[row, pl.BlockSpec((8, d), lambda i: (0, 0))],
                     out_shape=[_sds((n, d), F32), _sds((8, d), F32)],
                     compiler_params=_params(("arbitrary",)))(x, g.reshape(1, d), dh, dres)
    return dx, dg


def _norm_rope(xh, g, cos2, sin2):
    r = lax.rsqrt(jnp.mean(xh * xh, axis=-1, keepdims=True) + NORM_EPS)
    y = xh * r * g
    if cos2 is not None:
        y = y * cos2 + pltpu.roll(y, HEAD // 2, 1) * sin2
    return y


def _norm_rope_bwd(xh, g, cos2, sin2, dout):
    if cos2 is not None:
        dy = dout * cos2 + pltpu.roll(dout * sin2, HEAD // 2, 1)
    else:
        dy = dout
    r = lax.rsqrt(jnp.mean(xh * xh, axis=-1, keepdims=True) + NORM_EPS)
    u = dy * g
    c = jnp.mean(xh * u, axis=-1, keepdims=True)
    return r * u - xh * (r * r * r * c), dy * xh * r


_QK_GROUPS = (("qa", COL["qa"], 8, 0, True), ("ka", COL["ka"], 2, 1, True),
              ("qb", COL["qb"], 12, 2, True), ("kb", COL["kb"], 12, 3, True),
              ("qc", COL["qc"], 8, 4, False), ("kc", COL["kc"], 8, 5, False))
_V_GROUPS = (("va", COL["va"], 2), ("vb", COL["vb"], 12), ("vc", COL["vc"], 8))
_PREP_OUT = (("qa", 8), ("ka", 2), ("va", 2)) + tuple((f"{t}b{g}", 4) for t in "qkv" for g in range(3)) + (
    ("qc", 8), ("kc", 8), ("vc", 8))


def _prep_src(name):
    if name[1] == "b":
        base = COL[name[0] + "b"] + int(name[2]) * B_HG * HEAD
        gain = {"q": 2, "k": 3, "v": None}[name[0]]
        return base, gain, name[0] != "v"
    base = COL[name]
    gain = {"qa": 0, "ka": 1, "va": None, "qc": 4, "kc": 5, "vc": None}[name]
    return base, gain, name in ("qa", "ka")


def _prep_dil(name):
    return B_DILS[int(name[2])] if name[1] == "b" else 1


def _to_classes(val, scr, dil):
    scr[...] = val
    return [scr[pl.ds(r, val.shape[0] // dil, stride=dil), :] for r in range(dil)]


def _from_classes(parts, scr):
    for r, part in enumerate(parts):
        scr[pl.ds(r, part.shape[0], stride=len(parts)), :] = part
    return scr[...]


def _class_spec(tm, dil, width):
    if dil == 1:
        return pl.BlockSpec((tm, width), lambda i: (i, 0))
    return pl.BlockSpec((dil, tm // dil, width), lambda i: (0, i, 0))


def _class_shape(n, dil, width):
    return (n, width) if dil == 1 else (dil, n // dil, width)


def qk_prep(proj, gains, cos2, sin2, name):
    n = proj.shape[0]
    tm = _tile(n, 256, 8)

    def body(p_ref, g_ref, c_ref, s_ref, *refs):
        outs, scr = refs[:-1], refs[-1]
        cos2v, sin2v = c_ref[...], s_ref[...]
        for (nm, heads), o_ref in zip(_PREP_OUT, outs):
            base, gain, rope = _prep_src(nm)
            dil = _prep_dil(nm)
            for h in range(heads):
                hs = slice(h * HEAD, (h + 1) * HEAD)
                xh = p_ref[:, base + h * HEAD: base + (h + 1) * HEAD].astype(F32)
                if gain is None:
                    y = xh
                else:
                    y = _norm_rope(xh, g_ref[gain:gain + 1, :], cos2v if rope else None, sin2v if rope else None)
                if dil == 1:
                    o_ref[:, hs] = y.astype(o_ref.dtype)
                else:
                    for r, part in enumerate(_to_classes(y, scr.at[h % 4], dil)):
                        o_ref[r, :, hs] = part.astype(o_ref.dtype)

    tab = pl.BlockSpec((tm, HEAD), lambda i: (i, 0))
    outs = _pallas(body, name=name, grid=(n // tm,),
                   in_specs=[pl.BlockSpec((tm, QKV_W), lambda i: (i, 0)), pl.BlockSpec((8, HEAD), lambda i: (0, 0)), tab, tab],
                   out_specs=[_class_spec(tm, _prep_dil(nm), h * HEAD) for nm, h in _PREP_OUT],
                   out_shape=[_sds(_class_shape(n, _prep_dil(nm), h * HEAD), CDT) for nm, h in _PREP_OUT],
                   scratch_shapes=[pltpu.VMEM((4, tm, HEAD), F32)],
                   compiler_params=_params(("parallel",)))(proj, gains, cos2, sin2)
    return dict(zip([nm for nm, _ in _PREP_OUT], outs))


def qk_prep_bwd(proj, gains, cos2, sin2, grads, dproj, name):
    n = proj.shape[0]
    tm = _tile(n, 128, 8)
    names = [nm for nm, _ in _PREP_OUT]

    def body(p_ref, g_ref, c_ref, s_ref, *refs):
        g_refs, dp_ref, dg_ref, scr = refs[:len(names)], refs[len(names) + 1], refs[len(names) + 2], refs[-1]
        cos2v, sin2v = c_ref[...], s_ref[...]
        dg = [jnp.zeros((tm, HEAD), F32) for _ in range(6)]
        for (nm, heads), gr in zip(_PREP_OUT, g_refs):
            base, gain, rope = _prep_src(nm)
            dil = _prep_dil(nm)
            for h in range(heads):
                sl = slice(base + h * HEAD, base + (h + 1) * HEAD)
                hs = slice(h * HEAD, (h + 1) * HEAD)
                dout = gr[:, hs] if dil == 1 else _from_classes([gr[r, :, hs] for r in range(dil)], scr.at[h % 4])
                if gain is None:
                    dx = dout
                else:
                    dx, dgr = _norm_rope_bwd(p_ref[:, sl].astype(F32), g_ref[gain:gain + 1, :], cos2v if rope else None,
                                             sin2v if rope else None, dout)
                    dg[gain] = dg[gain] + dgr
                dp_ref[:, sl] = dx.astype(dp_ref.dtype)
        part = jnp.concatenate([jnp.sum(t, axis=0, keepdims=True) for t in dg] + [jnp.zeros((2, HEAD), F32)], axis=0)

        @pl.when(pl.program_id(0) == 0)
        def _():
            dg_ref[...] = part

        @pl.when(pl.program_id(0) > 0)
        def _():
            dg_ref[...] += part

    tab = pl.BlockSpec((tm, HEAD), lambda i: (i, 0))
    dp, dg = _pallas(body, name=name, grid=(n // tm,),
                     in_specs=[pl.BlockSpec((tm, QKV_W), lambda i: (i, 0)), pl.BlockSpec((8, HEAD), lambda i: (0, 0)), tab, tab]
                     + [_class_spec(tm, _prep_dil(nm), h * HEAD) for nm, h in _PREP_OUT] + [ANY],
                     out_specs=[pl.BlockSpec((tm, QKV_W), lambda i: (i, 0)), pl.BlockSpec((8, HEAD), lambda i: (0, 0))],
                     out_shape=[_sds(dproj.shape, dproj.dtype), _sds((8, HEAD), F32)],
                     input_output_aliases={4 + len(names): 0},
                     scratch_shapes=[pltpu.VMEM((4, tm, HEAD), F32)],
                     compiler_params=_params(("arbitrary",)))(proj, gains, cos2, sin2, *[grads[k] for k in names], dproj)
    return dp, dg


def _band_geometry(m, bq_pref, radius):
    bq = min(bq_pref, m)
    return bq, min(bq + 2 * radius, m)


def _band_window(i, bq, radius, m, w):
    start = pl.multiple_of(jnp.clip(i * bq - radius, 0, m - w), 64)
    qpos = i * bq + lax.broadcasted_iota(jnp.int32, (bq, w), 0)
    kpos = start + lax.broadcasted_iota(jnp.int32, (bq, w), 1)
    return start, jnp.abs(kpos - qpos) <= radius


def _band_specs(m, bq, G, nh, seqs):
    lg, nb = seqs // nh, m // bq
    qspec = pl.BlockSpec((bq, nh * G * HEAD), lambda s, i: ((s // lg) * nb + i, s % lg))
    kspec = pl.BlockSpec((m, nh * HEAD), lambda s, i: (s // lg, s % lg))
    return qspec, kspec, lg


def band_attn_fwd(q, k, v, sink, *, seqs, G, nh, radius, name, classes=1, bq_pref=128):
    m = q.shape[0] // classes
    bq, w = _band_geometry(m, bq_pref, radius)
    has_sink = sink is not None

    def body(*refs):
        if has_sink:
            sink_ref, q_ref, k_ref, v_ref, o_ref, lse_ref = refs
        else:
            q_ref, k_ref, v_ref, o_ref, lse_ref = refs
        s_id, i = pl.program_id(0), pl.program_id(1)
        start, valid = _band_window(i, bq, radius, m, w)
        units = [(h, g) for h in range(nh) for g in range(G)]
        sls = [slice((h * G + g) * HEAD, (h * G + g + 1) * HEAD) for h, g in units]
        k_ts = [k_ref[pl.ds(start, w), h * HEAD:(h + 1) * HEAD] for h in range(nh)]
        v_ts = [v_ref[pl.ds(start, w), h * HEAD:(h + 1) * HEAD] for h in range(nh)]
        q_ts = [q_ref[:, sl] for sl in sls]
        sks = [sink_ref[0, (s_id * nh + h) * G + g] for h, g in units] if has_sink else None
        ss = [jnp.where(valid, lax.dot_general(q_t, k_ts[h], _DIMS["nt"], preferred_element_type=F32) * SCALE, NEG)
              for q_t, (h, g) in zip(q_ts, units)]
        mxs = [jnp.max(s, axis=-1, keepdims=True) for s in ss]
        if has_sink:
            mxs = [jnp.maximum(mx, sk) for mx, sk in zip(mxs, sks)]
        ps = [jnp.exp(s - mx) for s, mx in zip(ss, mxs)]
        dens = [jnp.sum(p, axis=-1, keepdims=True) for p in ps]
        if has_sink:
            dens = [den + jnp.exp(sk - mx) for den, sk, mx in zip(dens, sks, mxs)]
        outs = [jnp.dot((p / den).astype(CDT), v_ts[h], preferred_element_type=F32)
                for p, den, (h, g) in zip(ps, dens, units)]
        for sl, o, mx, den in zip(sls, outs, mxs, dens):
            o_ref[:, sl] = o
            lse_ref[:, sl] = jnp.broadcast_to(mx + jnp.log(den), (bq, HEAD))

    qspec, kspec, lg = _band_specs(m, bq, G, nh, seqs)
    in_specs = ([pl.BlockSpec(memory_space=pltpu.SMEM)] if has_sink else []) + [qspec, kspec, kspec]
    args = ((sink,) if has_sink else ()) + (q, k, v)
    return _pallas(body, name=name, grid=(classes * lg, m // bq), in_specs=in_specs, out_specs=[qspec, qspec],
                   out_shape=[_sds(q.shape, F32), _sds(q.shape, F32)],
                   compiler_params=_params(("parallel", "arbitrary")))(*args)


def band_attn_bwd(q, k, v, sink, o, lse, do, dlse, *, seqs, G, nh, radius, name, classes=1, bq_pref=128):
    m = q.shape[0] // classes
    bq, w = _band_geometry(m, bq_pref, radius)
    has_sink, has_dlse = sink is not None, dlse is not None
    assert nh * G <= 8

    def body(*refs):
        refs = list(refs)
        sink_ref = refs.pop(0) if has_sink else None
        q_ref, k_ref, v_ref, o_ref, lse_ref, do_ref = refs[:6]
        refs = refs[6:]
        dlse_ref = refs.pop(0) if has_dlse else None
        dq_ref, dk_ref, dv_ref = refs[:3]
        dsink_ref = refs[3] if has_sink else None
        s_id, i = pl.program_id(0), pl.program_id(1)

        @pl.when(i == 0)
        def _():
            dk_ref[...] = jnp.zeros_like(dk_ref)
            dv_ref[...] = jnp.zeros_like(dv_ref)
            if has_sink:
                dsink_ref[...] = jnp.zeros_like(dsink_ref)

        start, valid = _band_window(i, bq, radius, m, w)
        units = [(h, g) for h in range(nh) for g in range(G)]
        sls = [slice((h * G + g) * HEAD, (h * G + g + 1) * HEAD) for h, g in units]
        kss = [slice(h * HEAD, (h + 1) * HEAD) for h in range(nh)]
        k_ts = [k_ref[pl.ds(start, w), ks] for ks in kss]
        v_ts = [v_ref[pl.ds(start, w), ks] for ks in kss]
        q_ts = [q_ref[:, sl] for sl in sls]
        lse_ts = [lse_ref[:, sl][:, :1] for sl in sls]
        do_ts = [do_ref[:, sl] for sl in sls]
        deltas = [jnp.sum(do_t * o_ref[:, sl], axis=-1, keepdims=True) for do_t, sl in zip(do_ts, sls)]
        dlse_ts = [dlse_ref[:, sl][:, :1] for sl in sls] if has_dlse else None
        dk_old = [dk_ref[pl.ds(start, w), ks] for ks in kss]
        dv_old = [dv_ref[pl.ds(start, w), ks] for ks in kss]
        dsink_old = dsink_ref[...] if has_sink else None

        ps = [jnp.exp(jnp.where(valid, lax.dot_general(q_t, k_ts[h], _DIMS["nt"], preferred_element_type=F32) * SCALE, NEG)
                      - lse_t) for q_t, lse_t, (h, g) in zip(q_ts, lse_ts, units)]
        do_cs = [do_t.astype(CDT) for do_t in do_ts]
        dps = [lax.dot_general(do_c, v_ts[h], _DIMS["nt"], preferred_element_type=F32) for do_c, (h, g) in zip(do_cs, units)]
        ts = [dp - delta for dp, delta in zip(dps, deltas)]
        if has_dlse:
            ts = [t + dl for t, dl in zip(ts, dlse_ts)]
        dss = [((p * t) * SCALE).astype(CDT) for p, t in zip(ps, ts)]
        dqs = [jnp.dot(ds, k_ts[h], preferred_element_type=F32) for ds, (h, g) in zip(dss, units)]
        dvs = [lax.dot_general(p.astype(CDT), do_c, _DIMS["tn"], preferred_element_type=F32) for p, do_c in zip(ps, do_cs)]
        dks = [lax.dot_general(ds, q_t, _DIMS["tn"], preferred_element_type=F32) for ds, q_t in zip(dss, q_ts)]
        dk_new = [dk_old[h] + sum(dks[h * G + g] for g in range(G)) for h in range(nh)]
        dv_new = [dv_old[h] + sum(dvs[h * G + g] for g in range(G)) for h in range(nh)]
        if has_sink:
            rows = []
            for (h, g), lse_t, delta in zip(units, lse_ts, deltas):
                sk = sink_ref[0, (s_id * nh + h) * G + g]
                rows.append(jnp.broadcast_to(-jnp.sum(jnp.exp(sk - lse_t) * delta, axis=0, keepdims=True), (1, HEAD)))
            rows += [jnp.zeros((1, HEAD), F32)] * (8 - len(rows))
            dsink_new = dsink_old + jnp.concatenate(rows, axis=0)

        for sl, dq in zip(sls, dqs):
            dq_ref[:, sl] = dq
        for h, ks in enumerate(kss):
            dk_ref[pl.ds(start, w), ks] = dk_new[h]
            dv_ref[pl.ds(start, w), ks] = dv_new[h]
        if has_sink:
            dsink_ref[...] = dsink_new

    qspec, kspec, lg = _band_specs(m, bq, G, nh, seqs)
    in_specs = ([pl.BlockSpec(memory_space=pltpu.SMEM)] if has_sink else []) + [qspec, kspec, kspec, qspec, qspec, qspec]
    in_specs += [qspec] if has_dlse else []
    args = ((sink,) if has_sink else ()) + (q, k, v, o, lse, do) + ((dlse,) if has_dlse else ())
    out_specs = [qspec, kspec, kspec]
    out_shape = [_sds(q.shape, F32), _sds(k.shape, F32), _sds(k.shape, F32)]
    if has_sink:
        out_specs.append(pl.BlockSpec((None, 8, HEAD), lambda s, i: (s, 0, 0)))
        out_shape.append(_sds((seqs // nh, 8, HEAD), F32))
    return _pallas(body, name=name, grid=(classes * lg, m // bq), in_specs=in_specs, out_specs=out_specs,
                   out_shape=out_shape, compiler_params=_params(("parallel", "arbitrary")))(*args)


def _group_weights(lses):
    mx = jnp.maximum(jnp.maximum(lses[0], lses[1]), lses[2])
    e = [jnp.exp(l - mx) for l in lses]
    tot = e[0] + e[1] + e[2]
    return [t / tot for t in e]


def _read_group(ref, dil, h, scr):
    hs = slice(h * HEAD, (h + 1) * HEAD)
    return ref[:, hs] if dil == 1 else _from_classes([ref[r, :, hs] for r in range(dil)], scr)


def b_combine_fwd(os_, lses, name):
    n, wd = os_[0].shape
    tm = _tile(n, 512, 8)

    def body(o0, o1, o2, l0, l1, l2, out_ref, scr):
        for h in range(wd // HEAD):
            ls = [_read_group(ref, dil, h, scr.at[g]) for g, (ref, dil) in enumerate(zip((l0, l1, l2), B_DILS))]
            ovs = [_read_group(ref, dil, h, scr.at[3 + g]) for g, (ref, dil) in enumerate(zip((o0, o1, o2), B_DILS))]
            wts = _group_weights(ls)
            out_ref[:, h * HEAD:(h + 1) * HEAD] = wts[0] * ovs[0] + wts[1] * ovs[1] + wts[2] * ovs[2]

    specs = [_class_spec(tm, dil, wd) for dil in B_DILS]
    return _pallas(body, name=name, grid=(n // tm,), in_specs=specs * 2, out_specs=pl.BlockSpec((tm, wd), lambda i: (i, 0)),
                   out_shape=_sds((n, wd), F32), scratch_shapes=[pltpu.VMEM((6, tm, HEAD), F32)],
                   compiler_params=_params(("parallel",)))(*os_, *lses)


def b_combine_bwd(dout, os_, lses, name):
    n, wd = dout.shape
    tm = _tile(n, 256, 8)

    def body(d_ref, o0, o1, o2, l0, l1, l2, do0, do1, do2, dl0, dl1, dl2, scr):
        for h in range(wd // HEAD):
            hs = slice(h * HEAD, (h + 1) * HEAD)
            dv = d_ref[:, hs]
            ls = [_read_group(ref, dil, h, scr.at[g]) for g, (ref, dil) in enumerate(zip((l0, l1, l2), B_DILS))]
            ovs = [_read_group(ref, dil, h, scr.at[3 + g]) for g, (ref, dil) in enumerate(zip((o0, o1, o2), B_DILS))]
            wts = _group_weights(ls)
            dws = [jnp.broadcast_to(jnp.sum(dv * ov, axis=-1, keepdims=True), (tm, HEAD)) for ov in ovs]
            mean = wts[0] * dws[0] + wts[1] * dws[1] + wts[2] * dws[2]
            for g, (wt, dw, do_ref, dl_ref, dil) in enumerate(zip(wts, dws, (do0, do1, do2), (dl0, dl1, dl2), B_DILS)):
                if dil == 1:
                    do_ref[:, hs] = wt * dv
                    dl_ref[:, hs] = wt * (dw - mean)
                else:
                    for r, part in enumerate(_to_classes(wt * dv, scr.at[g], dil)):
                        do_ref[r, :, hs] = part
                    for r, part in enumerate(_to_classes(wt * (dw - mean), scr.at[3 + g], dil)):
                        dl_ref[r, :, hs] = part

    specs = [_class_spec(tm, dil, wd) for dil in B_DILS]
    outs = _pallas(body, name=name, grid=(n // tm,), in_specs=[pl.BlockSpec((tm, wd), lambda i: (i, 0))] + specs * 2,
                   out_specs=specs * 2, out_shape=[_sds(_class_shape(n, dil, wd), F32) for dil in B_DILS] * 2,
                   scratch_shapes=[pltpu.VMEM((6, tm, HEAD), F32)],
                   compiler_params=_params(("parallel",)))(dout, *os_, *lses)
    return outs[:3], outs[3:]


def _c_rows(n):
    rows = n // GRID_W
    return rows, min(C_WIN_ROWS, rows)


def _c_row_start(r, rows, wr):
    return jnp.clip(r - wr // 2, 0, rows - wr)


def _c_bias_index(r, rows, wr):
    return _c_row_start(r, rows, wr) - r + (C_WIN_ROWS - 1)


def _col_shift_select(tile, cq, inverse):
    lanes = tile.shape[1]
    for b in range(6):
        amt = (lanes - (1 << b)) if inverse else (1 << b)
        tile = jnp.where(((cq >> b) & 1) == 1, pltpu.roll(tile, amt, 1), tile)
    return tile


def rpb_expand(rwin, name):
    lanes = rwin.shape[-1]

    def body(r_ref, b_ref):
        cq = lax.broadcasted_iota(jnp.int32, (GRID_W, lanes), 0)
        ck = lax.broadcasted_iota(jnp.int32, (GRID_W, lanes), 1) % GRID_W
        cs = jnp.clip(cq - C_WIN_COLS // 2, 0, GRID_W - C_WIN_COLS)
        ok = (ck >= cs) & (ck < cs + C_WIN_COLS)
        for i0 in range(C_WIN_ROWS):
            tile = jnp.broadcast_to(r_ref[i0], (GRID_W, lanes))
            tile = pltpu.roll(tile, lanes - (C_WIN_COLS - 1), 1)
            tile = _col_shift_select(tile, cq, False)
            b_ref[i0] = jnp.where(ok, tile, NEG)

    return _pallas(body, name=name, grid=(C_HEADS,),
                   in_specs=[pl.BlockSpec((None, C_WIN_ROWS, 1, lanes), lambda h: (h, 0, 0, 0))],
                   out_specs=pl.BlockSpec((None, C_WIN_ROWS, GRID_W, lanes), lambda h: (h, 0, 0, 0)),
                   out_shape=_sds((C_HEADS, C_WIN_ROWS, GRID_W, lanes), F32),
                   compiler_params=_params(("parallel",)))(rwin)


def rpb_reduce(dbias, name):
    lanes = dbias.shape[-1]
    wr = lanes // GRID_W

    def body(d_ref, o_ref):
        cq = lax.broadcasted_iota(jnp.int32, (GRID_W, lanes), 0)
        o_ref[...] = jnp.zeros_like(o_ref)
        for i0 in range(C_WIN_ROWS):
            tile = _col_shift_select(d_ref[i0], cq, True)
            tile = pltpu.roll(tile, C_WIN_COLS - 1, 1)
            vec = jnp.sum(tile, axis=0, keepdims=True)
            for w in range(wr):
                o_ref[i0 + w:i0 + w + 1, :] += vec[:, w * GRID_W:(w + 1) * GRID_W]

    return _pallas(body, name=name, grid=(C_HEADS,),
                   in_specs=[pl.BlockSpec((None, C_WIN_ROWS, GRID_W, lanes), lambda h: (h, 0, 0, 0))],
                   out_specs=pl.BlockSpec((None, 16, GRID_W), lambda h: (h, 0, 0)),
                   out_shape=_sds((C_HEADS, 16, GRID_W), F32),
                   compiler_params=_params(("parallel",)))(dbias)


def _store_or_add(ref, val, first):
    @pl.when(first)
    def _():
        ref[...] = val

    @pl.when(jnp.logical_not(first))
    def _():
        ref[...] += val


def c_attn_fwd(q, k, v, bias, name, nh=4):
    n = q.shape[0]
    rows, wr = _c_rows(n)
    wk = wr * GRID_W

    def body(q_ref, k_ref, v_ref, b_ref, o_ref, lse_ref):
        r = pl.program_id(1)
        start = pl.multiple_of(_c_row_start(r, rows, wr) * GRID_W, GRID_W)
        sls = [slice(h * HEAD, (h + 1) * HEAD) for h in range(nh)]
        ss = [lax.dot_general(q_ref[:, sl], k_ref[pl.ds(start, wk), sl], _DIMS["nt"], preferred_element_type=F32)
              * SCALE + b_ref[h] for h, sl in enumerate(sls)]
        mxs = [jnp.max(s, axis=-1, keepdims=True) for s in ss]
        ps = [jnp.exp(s - mx) for s, mx in zip(ss, mxs)]
        dens = [jnp.sum(p, axis=-1, keepdims=True) for p in ps]
        outs = [jnp.dot((p / den).astype(CDT), v_ref[pl.ds(start, wk), sl], preferred_element_type=F32)
                for p, den, sl in zip(ps, dens, sls)]
        for sl, o, mx, den in zip(sls, outs, mxs, dens):
            o_ref[:, sl] = o
            lse_ref[:, sl] = jnp.broadcast_to(mx + jnp.log(den), (GRID_W, HEAD))

    qspec = pl.BlockSpec((GRID_W, nh * HEAD), lambda h, r: (r, h))
    kspec = pl.BlockSpec((n, nh * HEAD), lambda h, r: (0, h))
    bspec = pl.BlockSpec((nh, None, GRID_W, wk), lambda h, r: (h, _c_bias_index(r, rows, wr), 0, 0))
    return _pallas(body, name=name, grid=(C_HEADS // nh, rows), in_specs=[qspec, kspec, kspec, bspec],
                   out_specs=[qspec, qspec], out_shape=[_sds(q.shape, F32), _sds(q.shape, F32)],
                   compiler_params=_params(("parallel", "arbitrary")))(q, k, v, bias)


def c_attn_bwd(q, k, v, bias, o, lse, do, name, nh=2):
    n = q.shape[0]
    rows, wr = _c_rows(n)
    wk = wr * GRID_W

    def body(q_ref, k_ref, v_ref, b_ref, o_ref, lse_ref, do_ref, dq_ref, dk_ref, dv_ref, db_ref):
        r = pl.program_id(1)
        rs = _c_row_start(r, rows, wr)
        start = pl.multiple_of(rs * GRID_W, GRID_W)

        @pl.when(r == 0)
        def _():
            dk_ref[...] = jnp.zeros_like(dk_ref)
            dv_ref[...] = jnp.zeros_like(dv_ref)

        prev = _c_row_start(jnp.maximum(r - 1, 0), rows, wr) - jnp.maximum(r - 1, 0)
        first = (r == 0) | (prev != rs - r)
        sls = [slice(h * HEAD, (h + 1) * HEAD) for h in range(nh)]
        k_ts = [k_ref[pl.ds(start, wk), sl] for sl in sls]
        v_ts = [v_ref[pl.ds(start, wk), sl] for sl in sls]
        q_ts = [q_ref[:, sl] for sl in sls]
        lse_ts = [lse_ref[:, sl][:, :1] for sl in sls]
        do_ts = [do_ref[:, sl] for sl in sls]
        deltas = [jnp.sum(do_t * o_ref[:, sl], axis=-1, keepdims=True) for do_t, sl in zip(do_ts, sls)]
        biases = [b_ref[h] for h in range(nh)]
        dk_old = [dk_ref[pl.ds(start, wk), sl] for sl in sls]
        dv_old = [dv_ref[pl.ds(start, wk), sl] for sl in sls]

        ps = [jnp.exp(lax.dot_general(q_t, k_t, _DIMS["nt"], preferred_element_type=F32) * SCALE + b - lse_t)
              for q_t, k_t, b, lse_t in zip(q_ts, k_ts, biases, lse_ts)]
        do_cs = [do_t.astype(CDT) for do_t in do_ts]
        dps = [lax.dot_general(do_c, v_t, _DIMS["nt"], preferred_element_type=F32) for do_c, v_t in zip(do_cs, v_ts)]
        dss = [p * (dp - delta) for p, dp, delta in zip(ps, dps, deltas)]
        ds_cs = [(ds * SCALE).astype(CDT) for ds in dss]
        dqs = [jnp.dot(ds_c, k_t, preferred_element_type=F32) for ds_c, k_t in zip(ds_cs, k_ts)]
        dv_new = [old + lax.dot_general(p.astype(CDT), do_c, _DIMS["tn"], preferred_element_type=F32)
                  for old, p, do_c in zip(dv_old, ps, do_cs)]
        dk_new = [old + lax.dot_general(ds_c, q_t, _DIMS["tn"], preferred_element_type=F32)
                  for old, ds_c, q_t in zip(dk_old, ds_cs, q_ts)]

        for h, sl in enumerate(sls):
            dq_ref[:, sl] = dqs[h]
            dk_ref[pl.ds(start, wk), sl] = dk_new[h]
            dv_ref[pl.ds(start, wk), sl] = dv_new[h]
        for h in range(nh):
            _store_or_add(db_ref.at[h], dss[h], first)

    qspec = pl.BlockSpec((GRID_W, nh * HEAD), lambda h, r: (r, h))
    kspec = pl.BlockSpec((n, nh * HEAD), lambda h, r: (0, h))
    bspec = pl.BlockSpec((nh, None, GRID_W, wk), lambda h, r: (h, _c_bias_index(r, rows, wr), 0, 0))
    return _pallas(body, name=name, grid=(C_HEADS // nh, rows),
                   in_specs=[qspec, kspec, kspec, bspec, qspec, qspec, qspec],
                   out_specs=[qspec, kspec, kspec, bspec],
                   out_shape=[_sds(q.shape, F32), _sds(k.shape, F32), _sds(k.shape, F32), _sds(bias.shape, F32)],
                   compiler_params=_params(("parallel", "arbitrary")))(q, k, v, bias, o, lse, do)


def _sigmoid(z):
    return 1.0 / (1.0 + jnp.exp(-z))


def _gate_specs(n, d):
    tm, tn = _tile(n, 256, 8), _tile(math.gcd(d, QKV_W), 1024)
    nj = d // tn
    tile = pl.BlockSpec((tm, tn), lambda i, j: (i, j))
    gl = [pl.BlockSpec((tm, tn), functools.partial(lambda i, j, b: (i, (QKV_W + b * d) // tn + j), b=b)) for b in range(3)]
    return tm, tn, nj, tile, gl


def gate_merge(proj, ys, name):
    n, d = ys[0].shape
    tm, tn, nj, tile, gl = _gate_specs(n, d)

    def body(g0, g1, g2, y0, y1, y2, out_ref):
        acc = (_sigmoid(g0[...].astype(F32)) * y0[...] + _sigmoid(g1[...].astype(F32)) * y1[...]
               + _sigmoid(g2[...].astype(F32)) * y2[...])
        out_ref[...] = acc.astype(out_ref.dtype)

    return _pallas(body, name=name, grid=(n // tm, nj), in_specs=gl + [tile] * 3, out_specs=tile,
                   out_shape=_sds((n, d), CDT), compiler_params=_params(("parallel", "parallel")))(proj, proj, proj, *ys)


def gate_bwd(proj, ys, dmerged, name):
    n, d = dmerged.shape
    tm, tn, nj, _, _ = _gate_specs(n, d)

    def body(g_ref, y0, y1, y2, dm_ref, dy_ref, dp_ref):
        b = pl.program_id(2)
        y = jnp.where(b == 0, y0[...], jnp.where(b == 1, y1[...], y2[...]))
        dm = dm_ref[...]
        sg = _sigmoid(g_ref[...].astype(F32))
        dy_ref[...] = (dm * sg).astype(dy_ref.dtype)
        dp_ref[...] = (dm * y * sg * (1.0 - sg)).astype(dp_ref.dtype)

    gl = pl.BlockSpec((tm, tn), lambda i, j, b: (i, QKV_W // tn + b * nj + j))
    tile = pl.BlockSpec((tm, tn), lambda i, j, b: (i, j))
    return _pallas(body, name=name, grid=(n // tm, nj, 3), in_specs=[gl, tile, tile, tile, tile],
                   out_specs=[pl.BlockSpec((None, tm, tn), lambda i, j, b: (b, i, j)), gl],
                   out_shape=[_sds((3, n, d), CDT), _sds(proj.shape, CDT)],
                   compiler_params=_params(("parallel", "parallel", "arbitrary")))(proj, *ys, dmerged)


def gate_up_swiglu(h2, wg, name):
    n, d = h2.shape
    ns = wg.shape[2]
    ff = 2 * ns
    tm, tn = _tile(n, 512, 8), _tile(ns, 1408)
    nj = ns // tn

    def body(a_ref, bg_ref, bu_ref, gu_ref, act_ref):
        a = a_ref[...].astype(CDT)
        gt = jnp.dot(a, bg_ref[...].astype(CDT), preferred_element_type=F32)
        up = jnp.dot(a, bu_ref[...].astype(CDT), preferred_element_type=F32)
        gu_ref[0] = gt.astype(gu_ref.dtype)
        gu_ref[1] = up.astype(gu_ref.dtype)
        act_ref[...] = (gt * _sigmoid(gt) * up).astype(act_ref.dtype)

    return _pallas(body, name=name, grid=(n // tm, 2 * nj),
                   in_specs=[pl.BlockSpec((tm, d), lambda i, j: (i, 0)),
                             pl.BlockSpec((None, d, tn), lambda i, j: (j // nj, 0, j % nj)),
                             pl.BlockSpec((None, d, tn), lambda i, j: (2 + j // nj, 0, j % nj))],
                   out_specs=[pl.BlockSpec((2, tm, tn), lambda i, j: (0, i, j)), pl.BlockSpec((tm, tn), lambda i, j: (i, j))],
                   out_shape=[_sds((2, n, ff), ADT), _sds((n, ff), CDT)],
                   compiler_params=_params(("parallel", "parallel")))(h2, wg, wg)


def d_gate_up(dx, w_down, gu, name):
    n, d = dx.shape
    ff = w_down.shape[0]
    tm, tn = _tile(n, 1024, 8), _tile(ff, 512)

    def body(a_ref, b_ref, gu_ref, d_ref):
        da = lax.dot_general(a_ref[...].astype(CDT), b_ref[...].astype(CDT), _DIMS["nt"], preferred_element_type=F32)
        gt, up = gu_ref[0].astype(F32), gu_ref[1].astype(F32)
        sg = _sigmoid(gt)
        d_ref[0] = (da * up * (sg + gt * sg * (1.0 - sg))).astype(d_ref.dtype)
        d_ref[1] = (da * gt * sg).astype(d_ref.dtype)

    blk = pl.BlockSpec((2, tm, tn), lambda i, j: (0, i, j))
    return _pallas(body, name=name, grid=(n // tm, ff // tn),
                   in_specs=[pl.BlockSpec((tm, d), lambda i, j: (i, 0)), pl.BlockSpec((tn, d), lambda i, j: (j, 0)), blk],
                   out_specs=blk, out_shape=_sds((2, n, ff), CDT),
                   compiler_params=_params(("parallel", "parallel")))(dx, w_down, gu)


def loss_head(y, target, name):
    n, d = y.shape
    tm = _tile(n, 512, 8)
    nsteps = n // tm

    def body(y_ref, t_ref, l_ref, dy_ref, acc_ref):
        i = pl.program_id(0)
        e = y_ref[...] - t_ref[...]
        dy_ref[...] = e * (1.0 / d)
        part = jnp.sum((e * e).reshape(tm // 8, 8, d), axis=0)

        @pl.when(i == 0)
        def _():
            acc_ref[...] = part

        @pl.when(i > 0)
        def _():
            acc_ref[...] += part

        @pl.when(i == nsteps - 1)
        def _():
            tot = jnp.sum(jnp.sum(acc_ref[...], axis=1, keepdims=True), axis=0, keepdims=True) * (0.5 / d)
            l_ref[...] = jnp.broadcast_to(tot, (8, HEAD))

    row = pl.BlockSpec((tm, d), lambda i: (i, 0))
    return _pallas(body, name=name, grid=(nsteps,), in_specs=[row, row],
                   out_specs=[pl.BlockSpec((8, HEAD), lambda i: (0, 0)), row],
                   out_shape=[_sds((8, HEAD), F32), _sds((n, d), F32)],
                   scratch_shapes=[pltpu.VMEM((8, d), F32)],
                   compiler_params=_params(("arbitrary",)))(y, target)


def adamw(w, g, m, v, name):
    r, c = w.shape
    tr = _tile(r, max(8, (1 << 19) // c), 8)
    c1 = 1.0 - ADAM_B1 ** ADAM_STEP
    c2 = 1.0 - ADAM_B2 ** ADAM_STEP

    def body(w_ref, g_ref, m_ref, v_ref, d_ref, mo_ref, vo_ref):
        gv = g_ref[...]
        mn = ADAM_B1 * m_ref[...] + (1.0 - ADAM_B1) * gv
        vn = ADAM_B2 * v_ref[...] + (1.0 - ADAM_B2) * (gv * gv)
        d_ref[...] = -ADAM_LR * ((mn / c1) / (jnp.sqrt(vn / c2) + ADAM_EPS) + ADAM_WD * w_ref[...])
        mo_ref[...] = mn
        vo_ref[...] = vn

    row = pl.BlockSpec((tr, c), lambda i: (i, 0))
    return _pallas(body, name=name, grid=(r // tr,), in_specs=[row] * 4, out_specs=[row] * 3,
                   out_shape=[_sds((r, c), F32)] * 3, compiler_params=_params(("parallel",)))(w, g, m, v)


ANY = pl.BlockSpec(memory_space=pl.ANY)


def _place():
    x, y, c = lax.axis_index("x"), lax.axis_index("y"), lax.axis_index("c")
    return x, y, c, [(1 - x, y), (x, 1 - y), (1 - x, 1 - y)]


def _rcopy(src, dst, send_sems, recv_sems, k, to):
    return pltpu.make_async_remote_copy(src_ref=src, dst_ref=dst, send_sem=send_sems.at[k], recv_sem=recv_sems.at[k],
                                        device_id=to, device_id_type=MESH)


def cast_place(chip_idx, shard, name):
    k, ns = shard.shape
    tr = _tile(k, max(16, (1 << 19) // ns), 16)

    def body(k_ref, s_ref, o_ref):
        o_ref[...] = s_ref[...].astype(o_ref.dtype)

    gs = pltpu.PrefetchScalarGridSpec(
        num_scalar_prefetch=1, grid=(k // tr,),
        in_specs=[pl.BlockSpec((tr, ns), lambda i, k_ref: (i, 0))],
        out_specs=pl.BlockSpec((None, tr, ns), lambda i, k_ref: (k_ref[0], i, 0)))
    return _pallas(body, name=name, grid_spec=gs, out_shape=_sds((N_CHIPS, k, ns), CDT),
                   compiler_params=_params(("parallel",)))(chip_idx, shard)


HBM = pl.BlockSpec(memory_space=pltpu.HBM)
SEM = pl.BlockSpec(memory_space=pltpu.SEMAPHORE)
EFFECT = pltpu.SideEffectType.DATAFLOW_SIDE_EFFECTING


def _in_hbm(a):
    return pltpu.with_memory_space_constraint(a, pltpu.HBM)


def _gather_copies(refs, send_sems, recv_sems):
    x, y, c, chips = _place()
    me = 2 * x + y
    out = []
    for t, ref in enumerate(refs):
        kh = ref.shape[1] // 2
        for j, (px, py) in enumerate(chips):
            send = _rcopy(ref.at[me, pl.ds(c * kh, kh)], ref.at[me, pl.ds(c * kh, kh)], send_sems, recv_sems,
                          3 * t + j, (px, py, c))
            land = ref.at[2 * px + py, pl.ds(c * kh, kh)]
            out.append((send, _rcopy(land, land, send_sems, recv_sems, 3 * t + j, (px, py, c))))
    return out


def gather_start(bufs, after, name):
    nt = len(bufs)

    def body(*refs):
        ins, send_sems, recv_sems, token = refs[:nt], refs[nt + 1], refs[nt + 2], refs[-1]
        for send, _ in _gather_copies(ins, send_sems, recv_sems):
            send.start()
        token[...] = jnp.zeros_like(token)

    outs = _pallas(body, name=name, in_specs=[HBM] * nt + [ANY],
                   out_specs=(SEM, SEM) + (HBM,) * nt + (pl.BlockSpec(memory_space=pltpu.VMEM),),
                   out_shape=(pltpu.SemaphoreType.DMA((3 * nt,)), pltpu.SemaphoreType.DMA((3 * nt,)))
                   + tuple(pltpu.HBM(b.shape, b.dtype) for b in bufs) + (_sds((8, HEAD), F32),),
                   input_output_aliases={t: 2 + t for t in range(nt)},
                   compiler_params=pltpu.CompilerParams(has_side_effects=EFFECT))(*[_in_hbm(b) for b in bufs], after)
    return outs[0], outs[1], list(outs[2:2 + nt]), outs[-1]


def gather_wait(bufs, send_sems, recv_sems, after, name):
    nt = len(bufs)

    def body(*refs):
        ins, s_sems, r_sems = refs[:nt], refs[nt], refs[nt + 1]
        for send, land in _gather_copies(ins, s_sems, r_sems):
            send.wait_send()
            land.wait_recv()

    return _pallas(body, name=name, in_specs=[HBM] * nt + [SEM, SEM, ANY], out_specs=[HBM] * nt,
                   out_shape=[pltpu.HBM(b.shape, b.dtype) for b in bufs],
                   input_output_aliases={t: t for t in range(nt)},
                   compiler_params=pltpu.CompilerParams(has_side_effects=EFFECT))(*bufs, send_sems, recv_sems, after)


def pair_forward(bufs, name):
    nt = len(bufs)

    def body(*refs):
        outs = refs[nt:2 * nt]
        send_sems, recv_sems = refs[2 * nt:]
        x, y, c, chips = _place()
        cps = []
        for t in range(nt):
            kh = outs[t].shape[1] // 2
            for j, (px, py) in enumerate(chips):
                blk = outs[t].at[2 * px + py, pl.ds(c * kh, kh)]
                cps.append(_rcopy(blk, blk, send_sems, recv_sems, 3 * t + j, (x, y, 1 - c)))
                cps[-1].start()
        for t in range(nt):
            kh = outs[t].shape[1] // 2
            for j, (px, py) in enumerate(chips):
                blk = outs[t].at[2 * px + py, pl.ds((1 - c) * kh, kh)]
                _rcopy(blk, blk, send_sems, recv_sems, 3 * t + j, (x, y, 1 - c)).wait_recv()
        for cp in cps:
            cp.wait_send()

    return _pallas(body, name=name, in_specs=[ANY] * nt, out_specs=[ANY] * nt,
                   out_shape=[_sds(b.shape, b.dtype) for b in bufs],
                   input_output_aliases={t: t for t in range(nt)},
                   scratch_shapes=[pltpu.SemaphoreType.DMA((3 * nt,)), pltpu.SemaphoreType.DMA((3 * nt,))],
                   compiler_params=pltpu.CompilerParams(has_side_effects=True))(*bufs)


def pair_exchange(grads, name):
    nt = len(grads)

    def body(*refs):
        ins, outs = refs[:nt], refs[nt:2 * nt]
        send_sems, recv_sems = refs[2 * nt:]
        x, y, c, _ = _place()
        sibling = (x, y, 1 - c)
        cps = []
        for t in range(nt):
            kh = ins[t].shape[1] // 2
            cps.append(_rcopy(ins[t].at[:, pl.ds((1 - c) * kh, kh), :], outs[t], send_sems, recv_sems, t, sibling))
            cps[-1].start()
        for cp in cps:
            cp.wait_recv()
        for cp in cps:
            cp.wait_send()

    return _pallas(body, name=name, in_specs=[ANY] * nt, out_specs=[ANY] * nt,
                   out_shape=[_sds((N_CHIPS, g.shape[1] // 2, g.shape[2]), g.dtype) for g in grads],
                   scratch_shapes=[pltpu.SemaphoreType.DMA((nt,)), pltpu.SemaphoreType.DMA((nt,))],
                   compiler_params=pltpu.CompilerParams(has_side_effects=True))(*grads)


def _exchange_copies(sums, lands, send_sems, recv_sems):
    x, y, c, chips = _place()
    return [_rcopy(s.at[2 * px + py], l.at[j], send_sems, recv_sems, 3 * t + j, (px, py, c))
            for t, (s, l) in enumerate(zip(sums, lands)) for j, (px, py) in enumerate(chips)]


def exchange_start(sums, name):
    nt = len(sums)
    lands = [lax.empty((3,) + s.shape[1:], s.dtype) for s in sums]

    def body(*refs):
        ins, zones, send_sems, recv_sems, token = refs[:nt], refs[nt:2 * nt], refs[2 * nt], refs[2 * nt + 1], refs[-1]
        for cp in _exchange_copies(ins, zones, send_sems, recv_sems):
            cp.start()
        token[...] = jnp.zeros_like(token)

    outs = _pallas(body, name=name, in_specs=[HBM] * (2 * nt),
                   out_specs=(SEM, SEM) + (HBM,) * (2 * nt) + (pl.BlockSpec(memory_space=pltpu.VMEM),),
                   out_shape=(pltpu.SemaphoreType.DMA((3 * nt,)), pltpu.SemaphoreType.DMA((3 * nt,)))
                   + tuple(pltpu.HBM(a.shape, a.dtype) for a in list(sums) + lands) + (_sds((8, HEAD), F32),),
                   input_output_aliases={t: 2 + t for t in range(2 * nt)},
                   compiler_params=pltpu.CompilerParams(has_side_effects=EFFECT))(*[_in_hbm(a) for a in list(sums) + lands])
    return outs[0], outs[1], list(outs[2:2 + nt]), list(outs[2 + nt:2 + 2 * nt]), outs[-1]


def exchange_wait(sums, lands, send_sems, recv_sems, after, name):
    nt = len(sums)

    def body(*refs):
        ins, zones, s_sems, r_sems = refs[:nt], refs[nt:2 * nt], refs[2 * nt], refs[2 * nt + 1]
        for cp in _exchange_copies(ins, zones, s_sems, r_sems):
            cp.wait_send()
            cp.wait_recv()

    outs = _pallas(body, name=name, in_specs=[HBM] * (2 * nt) + [SEM, SEM, ANY], out_specs=[HBM] * (2 * nt),
                   out_shape=[pltpu.HBM(a.shape, a.dtype) for a in list(sums) + list(lands)],
                   input_output_aliases={t: t for t in range(2 * nt)},
                   compiler_params=pltpu.CompilerParams(has_side_effects=EFFECT))(*sums, *lands, send_sems, recv_sems, after)
    return list(outs[:nt]), list(outs[nt:])


def pair_share(halves, name):
    nt = len(halves)

    def body(*refs):
        ins, outs = refs[:nt], refs[nt:2 * nt]
        send_sems, recv_sems = refs[2 * nt:]
        x, y, c, _ = _place()
        cps = []
        for t in range(nt):
            cps.append(_rcopy(ins[t], outs[t], send_sems, recv_sems, t, (x, y, 1 - c)))
            cps[-1].start()
        for cp in cps:
            cp.wait_recv()
        for cp in cps:
            cp.wait_send()

    return _pallas(body, name=name, in_specs=[ANY] * nt, out_specs=[ANY] * nt,
                   out_shape=[_sds(h.shape, h.dtype) for h in halves],
                   scratch_shapes=[pltpu.SemaphoreType.DMA((nt,)), pltpu.SemaphoreType.DMA((nt,))],
                   compiler_params=pltpu.CompilerParams(has_side_effects=True))(*halves)


def small_allreduce(pack, name):
    r = pack.shape[0]

    def body(in_ref, out_ref, buf, send_sems, recv_sems):
        x, y, c, _ = _place()
        me = 4 * x + 2 * y + c
        sends = []
        for k in range(1, 8):
            to = ((x + ((k >> 2) & 1)) % 2, (y + ((k >> 1) & 1)) % 2, (c + (k & 1)) % 2)
            cp = _rcopy(in_ref, buf.at[me], send_sems, recv_sems, k - 1, to)
            cp.start()
            sends.append((cp, to))
        buf[pl.ds(me, 1)] = in_ref[...][None]
        for k, (_, to) in enumerate(sends):
            peer = 4 * to[0] + 2 * to[1] + to[2]
            _rcopy(in_ref, buf.at[peer], send_sems, recv_sems, k, to).wait_recv()
        for cp, _ in sends:
            cp.wait_send()
        acc = buf[0]
        for d in range(1, 8):
            acc = acc + buf[d]
        out_ref[...] = acc

    vm = pl.BlockSpec(memory_space=pltpu.VMEM)
    return _pallas(body, name=name, in_specs=[vm], out_specs=vm, out_shape=_sds((r, HEAD), F32),
                   scratch_shapes=[pltpu.VMEM((8, r, HEAD), F32), pltpu.SemaphoreType.DMA((7,)),
                                   pltpu.SemaphoreType.DMA((7,))],
                   compiler_params=pltpu.CompilerParams(has_side_effects=True))(pack)


def add_halves(c_idx, grad, other, name):
    _, k, ns = grad.shape
    kh = k // 2
    tr = _tile(kh, max(16, (1 << 19) // ns), 16)
    nr = kh // tr

    def body(c_ref, g_ref, o_ref, s_ref):
        s_ref[...] = (g_ref[...].astype(F32) + o_ref[...].astype(F32)).astype(s_ref.dtype)

    gs = pltpu.PrefetchScalarGridSpec(
        num_scalar_prefetch=1, grid=(N_CHIPS, nr),
        in_specs=[pl.BlockSpec((None, tr, ns), lambda g, i, c_ref: (g, c_ref[0] * nr + i, 0)),
                  pl.BlockSpec((None, tr, ns), lambda g, i, c_ref: (g, i, 0))],
        out_specs=pl.BlockSpec((None, tr, ns), lambda g, i, c_ref: (g, i, 0)))
    return _pallas(body, name=name, grid_spec=gs, out_shape=_sds((N_CHIPS, kh, ns), XDT),
                   compiler_params=_params(("parallel", "parallel")))(c_idx, grad, other)


def add_chips(chip_idx, sums, recv, stack, layer, n_layers, name):
    _, kh, ns = sums.shape
    tr = _tile(kh, max(16, (1 << 19) // ns), 16)
    has_stack = stack is not None

    def body(k_ref, s_ref, r0, r1, r2, *rest):
        o_ref = rest[-1]
        o_ref[...] = ((s_ref[...].astype(F32) + r0[...].astype(F32)) + r1[...].astype(F32)) + r2[...].astype(F32)

    rspec = [pl.BlockSpec((None, tr, ns), functools.partial(lambda i, k_ref, j: (j, i, 0), j=j)) for j in range(3)]
    gs = pltpu.PrefetchScalarGridSpec(
        num_scalar_prefetch=1, grid=(kh // tr,),
        in_specs=[pl.BlockSpec((None, tr, ns), lambda i, k_ref: (k_ref[0], i, 0))] + rspec + ([ANY] if has_stack else []),
        out_specs=pl.BlockSpec((None, tr, ns), lambda i, k_ref: (layer, i, 0)))
    args = (chip_idx, sums, recv, recv, recv) + ((stack,) if has_stack else ())
    return _pallas(body, name=name, grid_spec=gs, out_shape=_sds((n_layers, kh, ns), F32),
                   input_output_aliases={5: 0} if has_stack else {},
                   compiler_params=_params(("parallel",)))(*args)


def adamw_big(c_idx, w, m, v, mine, other, name):
    nl, k, ns = w.shape
    kh = k // 2
    tr = _tile(kh, max(8, (1 << 18) // ns), 8)
    nr = kh // tr
    c1 = 1.0 - ADAM_B1 ** ADAM_STEP
    c2 = 1.0 - ADAM_B2 ** ADAM_STEP

    def body(c_ref, w_ref, m_ref, v_ref, a_ref, b_ref, g_ref, d_ref, mo_ref, vo_ref):
        gv = jnp.where(pl.program_id(2) == c_ref[0], a_ref[...], b_ref[...])
        mn = ADAM_B1 * m_ref[...] + (1.0 - ADAM_B1) * gv
        vn = ADAM_B2 * v_ref[...] + (1.0 - ADAM_B2) * (gv * gv)
        g_ref[...] = gv
        d_ref[...] = -ADAM_LR * ((mn / c1) / (jnp.sqrt(vn / c2) + ADAM_EPS) + ADAM_WD * w_ref[...])
        mo_ref[...] = mn
        vo_ref[...] = vn

    full = pl.BlockSpec((None, tr, ns), lambda l, i, hh, c_ref: (l, hh * nr + i, 0))
    half = pl.BlockSpec((None, tr, ns), lambda l, i, hh, c_ref: (l, i, 0))
    gs = pltpu.PrefetchScalarGridSpec(num_scalar_prefetch=1, grid=(nl, nr, 2),
                                      in_specs=[full, full, full, half, half], out_specs=[full] * 4)
    return _pallas(body, name=name, grid_spec=gs, out_shape=[_sds(w.shape, F32)] * 4,
                   compiler_params=_params(("parallel", "parallel", "arbitrary")))(c_idx, w, m, v, mine, other)


W_NAMES = ("w_in", "w_br_a", "w_br_b", "w_br_c", "w_o", "w_gate_up", "w_down")


def _rope_tables(n):
    half = HEAD // 2
    inv_freq = ROPE_THETA ** (-jnp.arange(half, dtype=F32) * 2.0 / HEAD)
    ang = jnp.arange(n, dtype=F32)[:, None] * inv_freq[None, :]
    cos, sin = jnp.cos(ang), jnp.sin(ang)
    return jnp.concatenate([cos, cos], axis=-1), jnp.concatenate([-sin, sin], axis=-1)


def _rpb_windows(rpb):
    pad = jnp.pad(rpb, ((0, 0), (0, 1), (0, GRID_W - rpb.shape[2])))
    wins = [pad[:, i0:i0 + C_WIN_ROWS].reshape(C_HEADS, 1, C_WIN_ROWS * GRID_W) for i0 in range(C_WIN_ROWS)]
    return jnp.stack(wins, axis=1)


def _rows(t):
    return t.reshape(-1, t.shape[-1])


def _like(t, ref):
    return t.reshape(ref.shape)


def layer_fwd(x, p, w, cos2, sin2, rest=None):
    n, d = x.shape
    s = {"x": x}
    s["h"] = rmsnorm_fwd(x, p["norm1_g"], "norm1")
    s["proj"] = mm_x_wcol(s["h"], w["w_in"], ADT, "proj")
    gains = jnp.pad(p["qk_norm_g"], ((0, 2), (0, 0)))
    pp = s["pp"] = qk_prep(s["proj"], gains, cos2, sin2, "qk_prep")
    sink = p["sink_a"].reshape(1, A_Q_HEADS)
    s["oa"], s["lse_a"] = band_attn_fwd(pp["qa"], pp["ka"], pp["va"], sink, seqs=A_KV_HEADS, G=A_GROUP,
                                        nh=A_KV_HEADS, radius=A_RADIUS, name="attn_a")
    s["ob"], s["lse_b"] = [], []
    for g, dil in enumerate(B_DILS):
        o, lse = band_attn_fwd(_rows(pp[f"qb{g}"]), _rows(pp[f"kb{g}"]), _rows(pp[f"vb{g}"]), None, seqs=B_HG, G=1,
                               nh=B_HG, radius=B_RADIUS, classes=dil, name=f"attn_b{g}")
        s["ob"].append(_like(o, pp[f"qb{g}"]))
        s["lse_b"].append(_like(lse, pp[f"qb{g}"]))
    ob = b_combine_fwd(s["ob"], s["lse_b"], "b_combine")
    s["bias"] = rpb_expand(_rpb_windows(p["rpb_c"]), "rpb_expand")
    s["oc"], s["lse_c"] = c_attn_fwd(pp["qc"], pp["kc"], pp["vc"], s["bias"], "attn_c")
    s["o_in"] = (s["oa"], ob, s["oc"])
    if rest is not None:
        w = {**w, **rest(s["oc"])}
    s["w"] = w
    s["ys"] = [mm_x_wcol(o, w[k], F32, "branch_" + k[-1]) for o, k in zip(s["o_in"], ("w_br_a", "w_br_b", "w_br_c"))]
    s["merged"] = gate_merge(s["proj"], s["ys"], "gate_merge")
    s["x_mid"] = mm_x_w(s["merged"], w["w_o"], "out_proj", res=x)
    s["h2"] = rmsnorm_fwd(s["x_mid"], p["norm2_g"], "norm2")
    s["gu"], s["act"] = gate_up_swiglu(s["h2"], w["w_gate_up"], "gate_up")
    x_out = mm_x_w(s["act"], w["w_down"], "down", res=s["x_mid"], tk_pref=2816)
    return x_out, s


def layer_bwd(dx_out, s, p, cos2, sin2, on_dws):
    n, d = dx_out.shape
    pp, w = s["pp"], s["w"]
    dgu = d_gate_up(dx_out, w["w_down"], s["gu"], "d_gate_up")
    dw_down = mm_aT_d(s["act"], dx_out, "dw_down")
    dh2 = mm_x_wcolT(dgu, w["w_gate_up"], "d_h2", stacked_in=2)
    dw_gu = mm_aT_d_wcol(s["h2"], dgu, "dw_gate_up", tn_pref=1408, stacked_in=2)
    dx_mid, dg2 = rmsnorm_bwd(s["x_mid"], p["norm2_g"], dh2, dx_out, "norm2_bwd")

    dmerged = mm_x_wT(dx_mid, w["w_o"], "d_merged")
    dw_o = mm_aT_d(s["merged"], dx_mid, "dw_o")
    dys, dproj = gate_bwd(s["proj"], s["ys"], dmerged, "gate_bwd")
    dos, dw_br = [], []
    for b, (o, k) in enumerate(zip(s["o_in"], ("w_br_a", "w_br_b", "w_br_c"))):
        dos.append(mm_x_wcolT(dys, w[k], "d_o_" + k[-1], lead=b))
        dw_br.append(mm_aT_d_wcol(o, dys, "dw_br_" + k[-1], lead=b))

    grads = {}
    sink = p["sink_a"].reshape(1, A_Q_HEADS)
    grads["qa"], grads["ka"], grads["va"], dsink = band_attn_bwd(
        pp["qa"], pp["ka"], pp["va"], sink, s["oa"], s["lse_a"], dos[0], None,
        seqs=A_KV_HEADS, G=A_GROUP, nh=A_KV_HEADS, radius=A_RADIUS, name="attn_a_bwd")
    dobs, dlses = b_combine_bwd(dos[1], s["ob"], s["lse_b"], "b_combine_bwd")
    for g, dil in enumerate(B_DILS):
        dq, dk, dv = band_attn_bwd(_rows(pp[f"qb{g}"]), _rows(pp[f"kb{g}"]), _rows(pp[f"vb{g}"]), None,
                                   _rows(s["ob"][g]), _rows(s["lse_b"][g]), _rows(dobs[g]), _rows(dlses[g]),
                                   seqs=B_HG, G=1, nh=2, radius=B_RADIUS, classes=dil, name=f"attn_b{g}_bwd")
        grads[f"qb{g}"], grads[f"kb{g}"], grads[f"vb{g}"] = [_like(t, pp[f"qb{g}"]) for t in (dq, dk, dv)]
    grads["qc"], grads["kc"], grads["vc"], dbias = c_attn_bwd(pp["qc"], pp["kc"], pp["vc"], s["bias"], s["oc"],
                                                              s["lse_c"], dos[2], "attn_c_bwd")
    drpb = rpb_reduce(dbias, "rpb_reduce")[:, :2 * C_WIN_ROWS - 1, :2 * C_WIN_COLS - 1]
    gains = jnp.pad(p["qk_norm_g"], ((0, 2), (0, 0)))
    dproj, dgains = qk_prep_bwd(s["proj"], gains, cos2, sin2, grads, dproj, "qk_prep_bwd")
    dw_in = mm_aT_d_wcol(s["h"], dproj, "dw_in")
    dws = [dw_in] + dw_br + [dw_o.reshape(N_CHIPS, d // N_CHIPS, d), dw_gu,
                             dw_down.reshape(N_CHIPS, dw_down.shape[0] // N_CHIPS, d)]
    token = on_dws(dws)
    dh = mm_x_wcolT(dproj, w["w_in"], "d_h", after=token)
    dx_in, dg1 = rmsnorm_bwd(s["x"], p["norm1_g"], dh, dx_mid, "norm1_bwd")
    small = {"norm1_g": dg1[0], "qk_norm_g": dgains[:6], "sink_a": dsink[0, :, 0],
             "rpb_c": drpb, "norm2_g": dg2[0]}
    return dx_in, small


SMALL_NAMES = ("norm1_g", "qk_norm_g", "sink_a", "rpb_c", "norm2_g")


def _pack_small(parts, extra=None):
    flat = [parts[k].reshape(-1) for k in SMALL_NAMES]
    flat.append(jnp.zeros((1,), F32) if extra is None else extra.reshape(1))
    v = jnp.concatenate(flat)
    rows = -(-v.shape[0] // (8 * HEAD)) * 8
    return jnp.pad(v, (0, rows * HEAD - v.shape[0])).reshape(rows, HEAD)


def _unpack_small(pack, like):
    v = pack.reshape(-1)
    out, off = {}, 0
    for k in SMALL_NAMES:
        size = math.prod(like[k].shape)
        out[k] = v[off:off + size].reshape(like[k].shape)
        off += size
    return out, v[off]


def kernel(x, norm1_g, w_in, qk_norm_g, sink_a, rpb_c, w_br_a, w_br_b, w_br_c, w_o, norm2_g, w_gate_up, w_down, loss_target, m_norm1_g, m_w_in, m_qk_norm_g, m_sink_a, m_rpb_c, m_w_br_a, m_w_br_b, m_w_br_c, m_w_o, m_norm2_g, m_w_gate_up, m_w_down, v_norm1_g, v_w_in, v_qk_norm_g, v_sink_a, v_rpb_c, v_w_br_a, v_w_br_b, v_w_br_c, v_w_o, v_norm2_g, v_w_gate_up, v_w_down):
    big = dict(w_in=w_in, w_br_a=w_br_a, w_br_b=w_br_b, w_br_c=w_br_c, w_o=w_o, w_gate_up=w_gate_up, w_down=w_down)
    big_m = dict(w_in=m_w_in, w_br_a=m_w_br_a, w_br_b=m_w_br_b, w_br_c=m_w_br_c, w_o=m_w_o, w_gate_up=m_w_gate_up, w_down=m_w_down)
    big_v = dict(w_in=v_w_in, w_br_a=v_w_br_a, w_br_b=v_w_br_b, w_br_c=v_w_br_c, w_o=v_w_o, w_gate_up=v_w_gate_up, w_down=v_w_down)
    small = dict(norm1_g=norm1_g, qk_norm_g=qk_norm_g, sink_a=sink_a, rpb_c=rpb_c, norm2_g=norm2_g)
    small_m = dict(norm1_g=m_norm1_g, qk_norm_g=m_qk_norm_g, sink_a=m_sink_a, rpb_c=m_rpb_c, norm2_g=m_norm2_g)
    small_v = dict(norm1_g=v_norm1_g, qk_norm_g=v_qk_norm_g, sink_a=v_sink_a, rpb_c=v_rpb_c, norm2_g=v_norm2_g)
    n_layers = w_in.shape[0]
    n, d = x.shape[1], x.shape[2]
    c_idx = lax.axis_index("c").astype(jnp.int32).reshape(1)
    chip_idx = (2 * lax.axis_index("x") + lax.axis_index("y")).astype(jnp.int32).reshape(1)
    cos2, sin2 = _rope_tables(n)

    def gathered(names, started, after):
        send_sems, recv_sems, bufs, _ = started
        got = pair_forward(gather_wait(bufs, send_sems, recv_sems, after, "gather_wait"), "pair_forward")
        w = dict(zip(names, got))
        if "w_o" in w:
            w["w_o"] = w["w_o"].reshape(d, d)
            w["w_down"] = w["w_down"].reshape(-1, d)
        return w

    def start_gather(names, l, after):
        return gather_start([cast_place(chip_idx, big[k][l], "cast_" + k) for k in names], after, "gather_start")

    first, others_0 = W_NAMES[:1], W_NAMES[1:]
    started = start_gather(first, 0, chip_idx)
    started_rest = start_gather(others_0, 0, started[3])
    weights = gathered(first, started, started_rest[3])
    rest = lambda after: gathered(others_0, started_rest, after)

    xs, saved = x[0], []
    for l in range(n_layers):
        p = {k: small[k][l] for k in SMALL_NAMES}
        if l + 1 < n_layers:
            started = start_gather(W_NAMES, l + 1, weights["w_in"])
            p["norm1_g"] = p["norm1_g"] + started[3][0, 0]
        xs, s = layer_fwd(xs, p, weights, cos2, sin2, rest)
        saved.append(s)
        if l + 1 < n_layers:
            weights, rest = gathered(W_NAMES, started, xs), None
    loss_tile, dx = loss_head(xs, loss_target[0], "loss_head")

    halves = [None] * len(W_NAMES)
    small_g = [None] * n_layers

    def finish_exchange(pending, after, halves):
        l, (send_sems, recv_sems, sums, lands, _) = pending
        sums, from_chips = exchange_wait(sums, lands, send_sems, recv_sems, after, "exchange_wait")
        return [add_chips(chip_idx, sm, r, st, l, n_layers, "add_chips_" + k)
                for sm, r, st, k in zip(sums, from_chips, halves, W_NAMES)]

    state = {"pending": None, "halves": halves}

    def make_on_dws(l):
        def on_dws(dws):
            if state["pending"] is not None:
                state["halves"] = finish_exchange(state["pending"], dws[0], state["halves"])
            from_sibling = pair_exchange(dws, "pair_exchange")
            sums = [add_halves(c_idx, g, o, "add_halves_" + k) for g, o, k in zip(dws, from_sibling, W_NAMES)]
            state["pending"] = (l, exchange_start(sums, "exchange_start"))
            return state["pending"][1][4]
        return on_dws

    for l in reversed(range(n_layers)):
        p = {k: small[k][l] for k in SMALL_NAMES}
        dx, small_g[l] = layer_bwd(dx, saved[l], p, cos2, sin2, make_on_dws(l))
    halves = finish_exchange(state["pending"], dx, state["halves"])
    others = pair_share(halves, "pair_share")

    mine = {k: jnp.stack([small_g[l][k] for l in range(n_layers)]) for k in SMALL_NAMES}
    total = small_allreduce(_pack_small(mine, loss_tile[0, 0]), "small_allreduce")
    grad_small, loss = _unpack_small(total, small)

    outs = {}
    for k, mine_half, other_half in zip(W_NAMES, halves, others):
        outs[k] = adamw_big(c_idx, big[k], big_m[k], big_v[k], mine_half, other_half, "adamw_" + k)
    res = adamw(_pack_small(small), _pack_small(grad_small), _pack_small(small_m), _pack_small(small_v), "adamw_small")
    unp = [_unpack_small(t, small)[0] for t in res]
    for k in SMALL_NAMES:
        outs[k] = (grad_small[k],) + tuple(u[k] for u in unp)

    order = ("norm1_g", "w_in", "qk_norm_g", "sink_a", "rpb_c", "w_br_a", "w_br_b", "w_br_c", "w_o", "norm2_g",
             "w_gate_up", "w_down")
    return (loss, dx[None]) + tuple(outs[k][i] for i in range(4) for k in order)
```

```python
import functools
import math

import jax
import jax.numpy as jnp
from jax import lax
from jax.experimental import pallas as pl
from jax.experimental.pallas import tpu as pltpu

F32 = jnp.float32
CDT = jnp.bfloat16
XDT = jnp.bfloat16
ADT = jnp.bfloat16

HEAD = 128
NORM_EPS = 1e-6
ROPE_THETA = 10000.0
A_Q_HEADS, A_KV_HEADS, A_GROUP, A_RADIUS = 8, 2, 4, 128
B_DILS = (1, 4, 16)
B_RADIUS = 64
B_HG = 4
C_HEADS, GRID_W, C_WIN_ROWS, C_WIN_COLS = 8, 64, 8, 16
QKV_W = 9216
COL = dict(qa=0, ka=1024, va=1280, qb=1536, kb=3072, vb=4608, qc=6144, kc=7168, vc=8192)
NEG = -1e30
SCALE = HEAD ** -0.5
N_CHIPS = 4

ADAM_LR, ADAM_B1, ADAM_B2, ADAM_EPS, ADAM_WD, ADAM_STEP = 0.001, 0.9, 0.999, 1e-08, 0.01, 10

VMEM_LIMIT = 56 * 1024 * 1024
MESH = pl.DeviceIdType.MESH


def _pallas(body, **kw):
    return pl.pallas_call(body, **kw)


def _params(sem=None, **kw):
    if sem is not None:
        kw["dimension_semantics"] = sem
    return pltpu.CompilerParams(vmem_limit_bytes=VMEM_LIMIT, **kw)


def _tile(dim, pref, mult=128):
    best = None
    for t in range(mult, min(dim, pref) + 1, mult):
        if dim % t == 0:
            best = t
    return dim if best is None else best


def _sds(shape, dtype):
    return jax.ShapeDtypeStruct(tuple(shape), dtype)


_DIMS = {"nn": (((1,), (0,)), ((), ())), "nt": (((1,), (1,)), ((), ())), "tn": (((0,), (0,)), ((), ()))}


def _matmul(a, b, *, mode, grid, a_spec, b_spec, o_spec, out_shape, acc_shape, name, res=None, res_spec=None,
            after=None):
    nk = grid[2]
    has_res = res is not None
    n_in = 2 + int(has_res) + int(after is not None)

    def body(*refs):
        a_ref, b_ref = refs[:2]
        r_ref = refs[2] if has_res else None
        o_ref, rest = refs[n_in], refs[n_in + 1:]
        p = lax.dot_general(a_ref[...].astype(CDT), b_ref[...].astype(CDT), _DIMS[mode],
                            preferred_element_type=F32)

        def finish(acc):
            if has_res:
                acc = acc + r_ref[...].astype(F32)
            o_ref[...] = acc.astype(o_ref.dtype)

        if nk == 1:
            finish(p)
        else:
            acc_ref = rest[0]
            k = pl.program_id(2)

            @pl.when(k == 0)
            def _():
                acc_ref[...] = p

            @pl.when(k > 0)
            def _():
                acc_ref[...] += p

            @pl.when(k == nk - 1)
            def _():
                finish(acc_ref[...])

    in_specs = [a_spec, b_spec] + ([res_spec] if has_res else [])
    args = (a, b) + ((res,) if has_res else ())
    if after is not None:
        in_specs.append(pl.BlockSpec(after.shape, lambda i, j, kk: (0, 0)))
        args += (after,)
    scratch = [] if nk == 1 else [pltpu.VMEM(acc_shape, F32)]
    return _pallas(body, name=name, grid=grid, in_specs=in_specs, out_specs=o_spec, out_shape=out_shape,
                   scratch_shapes=scratch, compiler_params=_params(("parallel", "parallel", "arbitrary")))(*args)


def mm_x_wcol(a, wg, out_dtype, name, tm_pref=1024, tn_pref=1024, stacked_out=1):
    m, k = a.shape
    ns = wg.shape[2]
    tm, tn = _tile(m, tm_pref, 8), _tile(ns, tn_pref)
    nj = ns // tn
    grid = (m // tm, N_CHIPS * nj, 1)
    a_spec = pl.BlockSpec((tm, k), lambda i, j, kk: (i, 0))
    b_spec = pl.BlockSpec((None, k, tn), lambda i, j, kk: (j // nj, 0, j % nj))
    if stacked_out == 1:
        o_spec = pl.BlockSpec((tm, tn), lambda i, j, kk: (i, j))
        out_shape = _sds((m, N_CHIPS * ns), out_dtype)
    else:
        per = N_CHIPS * nj // stacked_out
        o_spec = pl.BlockSpec((None, tm, tn), lambda i, j, kk: (j // per, i, j % per))
        out_shape = _sds((stacked_out, m, N_CHIPS * ns // stacked_out), out_dtype)
    return _matmul(a, wg, mode="nn", grid=grid, a_spec=a_spec, b_spec=b_spec, o_spec=o_spec,
                   out_shape=out_shape, acc_shape=(tm, tn), name=name)


def mm_x_wcolT(d, wg, name, res=None, tm_pref=1024, tn_pref=512, tk_pref=4096, stacked_in=1, lead=None, after=None):
    kdim, ns = wg.shape[1], wg.shape[2]
    m = d.shape[-2]
    tm, tn, tk = _tile(m, tm_pref, 8), _tile(kdim, tn_pref), _tile(ns, tk_pref)
    nkk = ns // tk
    grid = (m // tm, kdim // tn, N_CHIPS * nkk)
    if lead is not None:
        a_spec = pl.BlockSpec((None, tm, tk), lambda i, j, kk: (lead, i, kk))
    elif stacked_in == 1:
        a_spec = pl.BlockSpec((tm, tk), lambda i, j, kk: (i, kk))
    else:
        per = N_CHIPS * nkk // stacked_in
        a_spec = pl.BlockSpec((None, tm, tk), lambda i, j, kk: (kk // per, i, kk % per))
    b_spec = pl.BlockSpec((None, tn, tk), lambda i, j, kk: (kk // nkk, j, kk % nkk))
    o_spec = pl.BlockSpec((tm, tn), lambda i, j, kk: (i, j))
    return _matmul(d, wg, mode="nt", grid=grid, a_spec=a_spec, b_spec=b_spec, o_spec=o_spec,
                   out_shape=_sds((m, kdim), F32), acc_shape=(tm, tn), name=name,
                   res=res, res_spec=None if res is None else o_spec, after=after)


def mm_aT_d_wcol(a, d, name, tm_pref=512, tk_pref=4096, tn_pref=1024, stacked_in=1, lead=None):
    m, kdim = a.shape
    ntot = d.shape[-1] * stacked_in
    ns = ntot // N_CHIPS
    tm, tkm, tn = _tile(kdim, tm_pref), _tile(m, tk_pref, 8), _tile(ns, tn_pref)
    nj = ns // tn
    grid = (kdim // tm, N_CHIPS * nj, m // tkm)
    a_spec = pl.BlockSpec((tkm, tm), lambda i, j, kk: (kk, i))
    if lead is not None:
        b_spec = pl.BlockSpec((None, tkm, tn), lambda i, j, kk: (lead, kk, j))
    elif stacked_in == 1:
        b_spec = pl.BlockSpec((tkm, tn), lambda i, j, kk: (kk, j))
    else:
        per = N_CHIPS * nj // stacked_in
        b_spec = pl.BlockSpec((None, tkm, tn), lambda i, j, kk: (j // per, kk, j % per))
    o_spec = pl.BlockSpec((None, tm, tn), lambda i, j, kk: (j // nj, i, j % nj))
    return _matmul(a, d, mode="tn", grid=grid, a_spec=a_spec, b_spec=b_spec, o_spec=o_spec,
                   out_shape=_sds((N_CHIPS, kdim, ns), XDT), acc_shape=(tm, tn), name=name)


def mm_x_w(a, w, name, res=None, out_dtype=F32, tm_pref=1024, tn_pref=1024, tk_pref=2048):
    m, k = a.shape
    n = w.shape[1]
    tm, tn, tk = _tile(m, tm_pref, 8), _tile(n, tn_pref), _tile(k, tk_pref)
    grid = (m // tm, n // tn, k // tk)
    o_spec = pl.BlockSpec((tm, tn), lambda i, j, kk: (i, j))
    return _matmul(a, w, mode="nn", grid=grid,
                   a_spec=pl.BlockSpec((tm, tk), lambda i, j, kk: (i, kk)),
                   b_spec=pl.BlockSpec((tk, tn), lambda i, j, kk: (kk, j)),
                   o_spec=o_spec, out_shape=_sds((m, n), out_dtype), acc_shape=(tm, tn), name=name,
                   res=res, res_spec=None if res is None else o_spec)


def mm_x_wT(d, w, name, out_dtype=F32, tm_pref=1024, tn_pref=1024, after=None):
    m, n = d.shape
    k = w.shape[0]
    tm, tn = _tile(m, tm_pref, 8), _tile(k, tn_pref)
    grid = (m // tm, k // tn, 1)
    return _matmul(d, w, mode="nt", grid=grid,
                   a_spec=pl.BlockSpec((tm, n), lambda i, j, kk: (i, 0)),
                   b_spec=pl.BlockSpec((tn, n), lambda i, j, kk: (j, 0)),
                   o_spec=pl.BlockSpec((tm, tn), lambda i, j, kk: (i, j)),
                   out_shape=_sds((m, k), out_dtype), acc_shape=(tm, tn), name=name, after=after)


def mm_aT_d(a, d, name, tm_pref=512, tn_pref=512, tk_pref=4096):
    m, k = a.shape
    n = d.shape[1]
    tm, tn, tk = _tile(k, tm_pref), _tile(n, tn_pref), _tile(m, tk_pref, 8)
    grid = (k // tm, n // tn, m // tk)
    return _matmul(a, d, mode="tn", grid=grid,
                   a_spec=pl.BlockSpec((tk, tm), lambda i, j, kk: (kk, i)),
                   b_spec=pl.BlockSpec((tk, tn), lambda i, j, kk: (kk, j)),
                   o_spec=pl.BlockSpec((tm, tn), lambda i, j, kk: (i, j)),
                   out_shape=_sds((k, n), XDT), acc_shape=(tm, tn), name=name)


def rmsnorm_fwd(x, g, name):
    n, d = x.shape
    tm = _tile(n, 512, 8)

    def body(x_ref, g_ref, h_ref):
        xv = x_ref[...]
        r = lax.rsqrt(jnp.mean(xv * xv, axis=-1, keepdims=True) + NORM_EPS)
        h_ref[...] = (xv * r * g_ref[...]).astype(h_ref.dtype)

    return _pallas(body, name=name, grid=(n // tm,),
                   in_specs=[pl.BlockSpec((tm, d), lambda i: (i, 0)), pl.BlockSpec((1, d), lambda i: (0, 0))],
                   out_specs=pl.BlockSpec((tm, d), lambda i: (i, 0)), out_shape=_sds((n, d), CDT),
                   compiler_params=_params(("parallel",)))(x, g.reshape(1, d))


def rmsnorm_bwd(x, g, dh, dres, name):
    n, d = x.shape
    tm = _tile(n, 256, 8)

    def body(x_ref, g_ref, dh_ref, dres_ref, dx_ref, dg_ref):
        xv = x_ref[...]
        r = lax.rsqrt(jnp.mean(xv * xv, axis=-1, keepdims=True) + NORM_EPS)
        dhv = dh_ref[...]
        u = dhv * g_ref[...]
        c = jnp.mean(xv * u, axis=-1, keepdims=True)
        dx_ref[...] = dres_ref[...] + r * u - xv * (r * r * r * c)
        part = jnp.broadcast_to(jnp.sum(dhv * xv * r, axis=0, keepdims=True), (8, d))

        @pl.when(pl.program_id(0) == 0)
        def _():
            dg_ref[...] = part

        @pl.when(pl.program_id(0) > 0)
        def _():
            dg_ref[...] += part

    row = pl.BlockSpec((tm, d), lambda i: (i, 0))
    dx, dg = _pallas(body, name=name, grid=(n // tm,),
                     in_specs=[row, pl.BlockSpec((1, d), lambda i: (0, 0)), row, row],
                     out_specs=[row, pl.BlockSpec((8, d), lambda i: (0, 0))],
                     out_shape=[_sds((n, d), F32), _sds((8, d), F32)],
                     compiler_params=_params(("arbitrary",)))(x, g.reshape(1, d), dh, dres)
    return dx, dg


def _norm_rope(xh, g, cos2, sin2):
    r = lax.rsqrt(jnp.mean(xh * xh, axis=-1, keepdims=True) + NORM_EPS)
    y = xh * r * g
    if cos2 is not None:
        y = y * cos2 + pltpu.roll(y, HEAD // 2, 1) * sin2
    return y


def _norm_rope_bwd(xh, g, cos2, sin2, dout):
    if cos2 is not None:
        dy = dout * cos2 + pltpu.roll(dout * sin2, HEAD // 2, 1)
    else:
        dy = dout
    r = lax.rsqrt(jnp.mean(xh * xh, axis=-1, keepdims=True) + NORM_EPS)
    u = dy * g
    c = jnp.mean(xh * u, axis=-1, keepdims=True)
    return r * u - xh * (r * r * r * c), dy * xh * r


_QK_GROUPS = (("qa", COL["qa"], 8, 0, True), ("ka", COL["ka"], 2, 1, True),
              ("qb", COL["qb"], 12, 2, True), ("kb", COL["kb"], 12, 3, True),
              ("qc", COL["qc"], 8, 4, False), ("kc", COL["kc"], 8, 5, False))
_V_GROUPS = (("va", COL["va"], 2), ("vb", COL["vb"], 12), ("vc", COL["vc"], 8))
_PREP_OUT = (("qa", 8), ("ka", 2), ("va", 2)) + tuple((f"{t}b{g}", 4) for t in "qkv" for g in range(3)) + (
    ("qc", 8), ("kc", 8), ("vc", 8))


def _prep_src(name):
    if name[1] == "b":
        base = COL[name[0] + "b"] + int(name[2]) * B_HG * HEAD
        gain = {"q": 2, "k": 3, "v": None}[name[0]]
        return base, gain, name[0] != "v"
    base = COL[name]
    gain = {"qa": 0, "ka": 1, "va": None, "qc": 4, "kc": 5, "vc": None}[name]
    return base, gain, name in ("qa", "ka")


def _prep_dil(name):
    return B_DILS[int(name[2])] if name[1] == "b" else 1


def _to_classes(val, scr, dil):
    scr[...] = val
    return [scr[pl.ds(r, val.shape[0] // dil, stride=dil), :] for r in range(dil)]


def _from_classes(parts, scr):
    for r, part in enumerate(parts):
        scr[pl.ds(r, part.shape[0], stride=len(parts)), :] = part
    return scr[...]


def _class_spec(tm, dil, width):
    if dil == 1:
        return pl.BlockSpec((tm, width), lambda i: (i, 0))
    return pl.BlockSpec((dil, tm // dil, width), lambda i: (0, i, 0))


def _class_shape(n, dil, width):
    return (n, width) if dil == 1 else (dil, n // dil, width)


def qk_prep(proj, gains, cos2, sin2, name):
    n = proj.shape[0]
    tm = _tile(n, 256, 8)

    def body(p_ref, g_ref, c_ref, s_ref, *refs):
        outs, scr = refs[:-1], refs[-1]
        cos2v, sin2v = c_ref[...], s_ref[...]
        for (nm, heads), o_ref in zip(_PREP_OUT, outs):
            base, gain, rope = _prep_src(nm)
            dil = _prep_dil(nm)
            for h in range(heads):
                hs = slice(h * HEAD, (h + 1) * HEAD)
                xh = p_ref[:, base + h * HEAD: base + (h + 1) * HEAD].astype(F32)
                if gain is None:
                    y = xh
                else:
                    y = _norm_rope(xh, g_ref[gain:gain + 1, :], cos2v if rope else None, sin2v if rope else None)
                if dil == 1:
                    o_ref[:, hs] = y.astype(o_ref.dtype)
                else:
                    for r, part in enumerate(_to_classes(y, scr.at[h % 4], dil)):
                        o_ref[r, :, hs] = part.astype(o_ref.dtype)

    tab = pl.BlockSpec((tm, HEAD), lambda i: (i, 0))
    outs = _pallas(body, name=name, grid=(n // tm,),
                   in_specs=[pl.BlockSpec((tm, QKV_W), lambda i: (i, 0)), pl.BlockSpec((8, HEAD), lambda i: (0, 0)), tab, tab],
                   out_specs=[_class_spec(tm, _prep_dil(nm), h * HEAD) for nm, h in _PREP_OUT],
                   out_shape=[_sds(_class_shape(n, _prep_dil(nm), h * HEAD), CDT) for nm, h in _PREP_OUT],
                   scratch_shapes=[pltpu.VMEM((4, tm, HEAD), F32)],
                   compiler_params=_params(("parallel",)))(proj, gains, cos2, sin2)
    return dict(zip([nm for nm, _ in _PREP_OUT], outs))


def qk_prep_bwd(proj, gains, cos2, sin2, grads, dproj, name):
    n = proj.shape[0]
    tm = _tile(n, 128, 8)
    names = [nm for nm, _ in _PREP_OUT]

    def body(p_ref, g_ref, c_ref, s_ref, *refs):
        g_refs, dp_ref, dg_ref, scr = refs[:len(names)], refs[len(names) + 1], refs[len(names) + 2], refs[-1]
        cos2v, sin2v = c_ref[...], s_ref[...]
        dg = [jnp.zeros((tm, HEAD), F32) for _ in range(6)]
        for (nm, heads), gr in zip(_PREP_OUT, g_refs):
            base, gain, rope = _prep_src(nm)
            dil = _prep_dil(nm)
            for h in range(heads):
                sl = slice(base + h * HEAD, base + (h + 1) * HEAD)
                hs = slice(h * HEAD, (h + 1) * HEAD)
                dout = gr[:, hs] if dil == 1 else _from_classes([gr[r, :, hs] for r in range(dil)], scr.at[h % 4])
                if gain is None:
                    dx = dout
                else:
                    dx, dgr = _norm_rope_bwd(p_ref[:, sl].astype(F32), g_ref[gain:gain + 1, :], cos2v if rope else None,
                                             sin2v if rope else None, dout)
                    dg[gain] = dg[gain] + dgr
                dp_ref[:, sl] = dx.astype(dp_ref.dtype)
        part = jnp.concatenate([jnp.sum(t, axis=0, keepdims=True) for t in dg] + [jnp.zeros((2, HEAD), F32)], axis=0)

        @pl.when(pl.program_id(0) == 0)
        def _():
            dg_ref[...] = part

        @pl.when(pl.program_id(0) > 0)
        def _():
            dg_ref[...] += part

    tab = pl.BlockSpec((tm, HEAD), lambda i: (i, 0))
    dp, dg = _pallas(body, name=name, grid=(n // tm,),
                     in_specs=[pl.BlockSpec((tm, QKV_W), lambda i: (i, 0)), pl.BlockSpec((8, HEAD), lambda i: (0, 0)), tab, tab]
                     + [_class_spec(tm, _prep_dil(nm), h * HEAD) for nm, h in _PREP_OUT] + [ANY],
                     out_specs=[pl.BlockSpec((tm, QKV_W), lambda i: (i, 0)), pl.BlockSpec((8, HEAD), lambda i: (0, 0))],
                     out_shape=[_sds(dproj.shape, dproj.dtype), _sds((8, HEAD), F32)],
                     input_output_aliases={4 + len(names): 0},
                     scratch_shapes=[pltpu.VMEM((4, tm, HEAD), F32)],
                     compiler_params=_params(("arbitrary",)))(proj, gains, cos2, sin2, *[grads[k] for k in names], dproj)
    return dp, dg


def _band_geometry(m, bq_pref, radius):
    bq = min(bq_pref, m)
    return bq, min(bq + 2 * radius, m)


def _band_window(i, bq, radius, m, w):
    start = pl.multiple_of(jnp.clip(i * bq - radius, 0, m - w), 64)
    qpos = i * bq + lax.broadcasted_iota(jnp.int32, (bq, w), 0)
    kpos = start + lax.broadcasted_iota(jnp.int32, (bq, w), 1)
    return start, jnp.abs(kpos - qpos) <= radius


def _band_specs(m, bq, G, nh, seqs):
    lg, nb = seqs // nh, m // bq
    qspec = pl.BlockSpec((bq, nh * G * HEAD), lambda s, i: ((s // lg) * nb + i, s % lg))
    kspec = pl.BlockSpec((m, nh * HEAD), lambda s, i: (s // lg, s % lg))
    return qspec, kspec, lg


def band_attn_fwd(q, k, v, sink, *, seqs, G, nh, radius, name, classes=1, bq_pref=128):
    m = q.shape[0] // classes
    bq, w = _band_geometry(m, bq_pref, radius)
    has_sink = sink is not None

    def body(*refs):
        if has_sink:
            sink_ref, q_ref, k_ref, v_ref, o_ref, lse_ref = refs
        else:
            q_ref, k_ref, v_ref, o_ref, lse_ref = refs
        s_id, i = pl.program_id(0), pl.program_id(1)
        start, valid = _band_window(i, bq, radius, m, w)
        units = [(h, g) for h in range(nh) for g in range(G)]
        sls = [slice((h * G + g) * HEAD, (h * G + g + 1) * HEAD) for h, g in units]
        k_ts = [k_ref[pl.ds(start, w), h * HEAD:(h + 1) * HEAD] for h in range(nh)]
        v_ts = [v_ref[pl.ds(start, w), h * HEAD:(h + 1) * HEAD] for h in range(nh)]
        q_ts = [q_ref[:, sl] for sl in sls]
        sks = [sink_ref[0, (s_id * nh + h) * G + g] for h, g in units] if has_sink else None
        ss = [jnp.where(valid, lax.dot_general(q_t, k_ts[h], _DIMS["nt"], preferred_element_type=F32) * SCALE, NEG)
              for q_t, (h, g) in zip(q_ts, units)]
        mxs = [jnp.max(s, axis=-1, keepdims=True) for s in ss]
        if has_sink:
            mxs = [jnp.maximum(mx, sk) for mx, sk in zip(mxs, sks)]
        ps = [jnp.exp(s - mx) for s, mx in zip(ss, mxs)]
        dens = [jnp.sum(p, axis=-1, keepdims=True) for p in ps]
        if has_sink:
            dens = [den + jnp.exp(sk - mx) for den, sk, mx in zip(dens, sks, mxs)]
        outs = [jnp.dot((p / den).astype(CDT), v_ts[h], preferred_element_type=F32)
                for p, den, (h, g) in zip(ps, dens, units)]
        for sl, o, mx, den in zip(sls, outs, mxs, dens):
            o_ref[:, sl] = o
            lse_ref[:, sl] = jnp.broadcast_to(mx + jnp.log(den), (bq, HEAD))

    qspec, kspec, lg = _band_specs(m, bq, G, nh, seqs)
    in_specs = ([pl.BlockSpec(memory_space=pltpu.SMEM)] if has_sink else []) + [qspec, kspec, kspec]
    args = ((sink,) if has_sink else ()) + (q, k, v)
    return _pallas(body, name=name, grid=(classes * lg, m // bq), in_specs=in_specs, out_specs=[qspec, qspec],
                   out_shape=[_sds(q.shape, F32), _sds(q.shape, F32)],
                   compiler_params=_params(("parallel", "arbitrary")))(*args)


def band_attn_bwd(q, k, v, sink, o, lse, do, dlse, *, seqs, G, nh, radius, name, classes=1, bq_pref=128):
    m = q.shape[0] // classes
    bq, w = _band_geometry(m, bq_pref, radius)
    has_sink, has_dlse = sink is not None, dlse is not None
    assert nh * G <= 8

    def body(*refs):
        refs = list(refs)
        sink_ref = refs.pop(0) if has_sink else None
        q_ref, k_ref, v_ref, o_ref, lse_ref, do_ref = refs[:6]
        refs = refs[6:]
        dlse_ref = refs.pop(0) if has_dlse else None
        dq_ref, dk_ref, dv_ref = refs[:3]
        dsink_ref = refs[3] if has_sink else None
        s_id, i = pl.program_id(0), pl.program_id(1)

        @pl.when(i == 0)
        def _():
            dk_ref[...] = jnp.zeros_like(dk_ref)
            dv_ref[...] = jnp.zeros_like(dv_ref)
            if has_sink:
                dsink_ref[...] = jnp.zeros_like(dsink_ref)

        start, valid = _band_window(i, bq, radius, m, w)
        units = [(h, g) for h in range(nh) for g in range(G)]
        sls = [slice((h * G + g) * HEAD, (h * G + g + 1) * HEAD) for h, g in units]
        kss = [slice(h * HEAD, (h + 1) * HEAD) for h in range(nh)]
        k_ts = [k_ref[pl.ds(start, w), ks] for ks in kss]
        v_ts = [v_ref[pl.ds(start, w), ks] for ks in kss]
        q_ts = [q_ref[:, sl] for sl in sls]
        lse_ts = [lse_ref[:, sl][:, :1] for sl in sls]
        do_ts = [do_ref[:, sl] for sl in sls]
        deltas = [jnp.sum(do_t * o_ref[:, sl], axis=-1, keepdims=True) for do_t, sl in zip(do_ts, sls)]
        dlse_ts = [dlse_ref[:, sl][:, :1] for sl in sls] if has_dlse else None
        dk_old = [dk_ref[pl.ds(start, w), ks] for ks in kss]
        dv_old = [dv_ref[pl.ds(start, w), ks] for ks in kss]
        dsink_old = dsink_ref[...] if has_sink else None

        ps = [jnp.exp(jnp.where(valid, lax.dot_general(q_t, k_ts[h], _DIMS["nt"], preferred_element_type=F32) * SCALE, NEG)
                      - lse_t) for q_t, lse_t, (h, g) in zip(q_ts, lse_ts, units)]
        do_cs = [do_t.astype(CDT) for do_t in do_ts]
        dps = [lax.dot_general(do_c, v_ts[h], _DIMS["nt"], preferred_element_type=F32) for do_c, (h, g) in zip(do_cs, units)]
        ts = [dp - delta for dp, delta in zip(dps, deltas)]
        if has_dlse:
            ts = [t + dl for t, dl in zip(ts, dlse_ts)]
        dss = [((p * t) * SCALE).astype(CDT) for p, t in zip(ps, ts)]
        dqs = [jnp.dot(ds, k_ts[h], preferred_element_type=F32) for ds, (h, g) in zip(dss, units)]
        dvs = [lax.dot_general(p.astype(CDT), do_c, _DIMS["tn"], preferred_element_type=F32) for p, do_c in zip(ps, do_cs)]
        dks = [lax.dot_general(ds, q_t, _DIMS["tn"], preferred_element_type=F32) for ds, q_t in zip(dss, q_ts)]
        dk_new = [dk_old[h] + sum(dks[h * G + g] for g in range(G)) for h in range(nh)]
        dv_new = [dv_old[h] + sum(dvs[h * G + g] for g in range(G)) for h in range(nh)]
        if has_sink:
            rows = []
            for (h, g), lse_t, delta in zip(units, lse_ts, deltas):
                sk = sink_ref[0, (s_id * nh + h) * G + g]
                rows.append(jnp.broadcast_to(-jnp.sum(jnp.exp(sk - lse_t) * delta, axis=0, keepdims=True), (1, HEAD)))
            rows += [jnp.zeros((1, HEAD), F32)] * (8 - len(rows))
            dsink_new = dsink_old + jnp.concatenate(rows, axis=0)

        for sl, dq in zip(sls, dqs):
            dq_ref[:, sl] = dq
        for h, ks in enumerate(kss):
            dk_ref[pl.ds(start, w), ks] = dk_new[h]
            dv_ref[pl.ds(start, w), ks] = dv_new[h]
        if has_sink:
            dsink_ref[...] = dsink_new

    qspec, kspec, lg = _band_specs(m, bq, G, nh, seqs)
    in_specs = ([pl.BlockSpec(memory_space=pltpu.SMEM)] if has_sink else []) + [qspec, kspec, kspec, qspec, qspec, qspec]
    in_specs += [qspec] if has_dlse else []
    args = ((sink,) if has_sink else ()) + (q, k, v, o, lse, do) + ((dlse,) if has_dlse else ())
    out_specs = [qspec, kspec, kspec]
    out_shape = [_sds(q.shape, F32), _sds(k.shape, F32), _sds(k.shape, F32)]
    if has_sink:
        out_specs.append(pl.BlockSpec((None, 8, HEAD), lambda s, i: (s, 0, 0)))
        out_shape.append(_sds((seqs // nh, 8, HEAD), F32))
    return _pallas(body, name=name, grid=(classes * lg, m // bq), in_specs=in_specs, out_specs=out_specs,
                   out_shape=out_shape, compiler_params=_params(("parallel", "arbitrary")))(*args)


def _group_weights(lses):
    mx = jnp.maximum(jnp.maximum(lses[0], lses[1]), lses[2])
    e = [jnp.exp(l - mx) for l in lses]
    tot = e[0] + e[1] + e[2]
    return [t / tot for t in e]


def _read_group(ref, dil, h, scr):
    hs = slice(h * HEAD, (h + 1) * HEAD)
    return ref[:, hs] if dil == 1 else _from_classes([ref[r, :, hs] for r in range(dil)], scr)


def b_combine_fwd(os_, lses, name):
    n, wd = os_[0].shape
    tm = _tile(n, 512, 8)

    def body(o0, o1, o2, l0, l1, l2, out_ref, scr):
        for h in range(wd // HEAD):
            ls = [_read_group(ref, dil, h, scr.at[g]) for g, (ref, dil) in enumerate(zip((l0, l1, l2), B_DILS))]
            ovs = [_read_group(ref, dil, h, scr.at[3 + g]) for g, (ref, dil) in enumerate(zip((o0, o1, o2), B_DILS))]
            wts = _group_weights(ls)
            out_ref[:, h * HEAD:(h + 1) * HEAD] = wts[0] * ovs[0] + wts[1] * ovs[1] + wts[2] * ovs[2]

    specs = [_class_spec(tm, dil, wd) for dil in B_DILS]
    return _pallas(body, name=name, grid=(n // tm,), in_specs=specs * 2, out_specs=pl.BlockSpec((tm, wd), lambda i: (i, 0)),
                   out_shape=_sds((n, wd), F32), scratch_shapes=[pltpu.VMEM((6, tm, HEAD), F32)],
                   compiler_params=_params(("parallel",)))(*os_, *lses)


def b_combine_bwd(dout, os_, lses, name):
    n, wd = dout.shape
    tm = _tile(n, 256, 8)

    def body(d_ref, o0, o1, o2, l0, l1, l2, do0, do1, do2, dl0, dl1, dl2, scr):
        for h in range(wd // HEAD):
            hs = slice(h * HEAD, (h + 1) * HEAD)
            dv = d_ref[:, hs]
            ls = [_read_group(ref, dil, h, scr.at[g]) for g, (ref, dil) in enumerate(zip((l0, l1, l2), B_DILS))]
            ovs = [_read_group(ref, dil, h, scr.at[3 + g]) for g, (ref, dil) in enumerate(zip((o0, o1, o2), B_DILS))]
            wts = _group_weights(ls)
            dws = [jnp.broadcast_to(jnp.sum(dv * ov, axis=-1, keepdims=True), (tm, HEAD)) for ov in ovs]
            mean = wts[0] * dws[0] + wts[1] * dws[1] + wts[2] * dws[2]
            for g, (wt, dw, do_ref, dl_ref, dil) in enumerate(zip(wts, dws, (do0, do1, do2), (dl0, dl1, dl2), B_DILS)):
                if dil == 1:
                    do_ref[:, hs] = wt * dv
                    dl_ref[:, hs] = wt * (dw - mean)
                else:
                    for r, part in enumerate(_to_classes(wt * dv, scr.at[g], dil)):
                        do_ref[r, :, hs] = part
                    for r, part in enumerate(_to_classes(wt * (dw - mean), scr.at[3 + g], dil)):
                        dl_ref[r, :, hs] = part

    specs = [_class_spec(tm, dil, wd) for dil in B_DILS]
    outs = _pallas(body, name=name, grid=(n // tm,), in_specs=[pl.BlockSpec((tm, wd), lambda i: (i, 0))] + specs * 2,
                   out_specs=specs * 2, out_shape=[_sds(_class_shape(n, dil, wd), F32) for dil in B_DILS] * 2,
                   scratch_shapes=[pltpu.VMEM((6, tm, HEAD), F32)],
                   compiler_params=_params(("parallel",)))(dout, *os_, *lses)
    return outs[:3], outs[3:]


def _c_rows(n):
    rows = n // GRID_W
    return rows, min(C_WIN_ROWS, rows)


def _c_row_start(r, rows, wr):
    return jnp.clip(r - wr // 2, 0, rows - wr)


def _c_bias_index(r, rows, wr):
    return _c_row_start(r, rows, wr) - r + (C_WIN_ROWS - 1)


def _col_shift_select(tile, cq, inverse):
    lanes = tile.shape[1]
    for b in range(6):
        amt = (lanes - (1 << b)) if inverse else (1 << b)
        tile = jnp.where(((cq >> b) & 1) == 1, pltpu.roll(tile, amt, 1), tile)
    return tile


def rpb_expand(rwin, name):
    lanes = rwin.shape[-1]

    def body(r_ref, b_ref):
        cq = lax.broadcasted_iota(jnp.int32, (GRID_W, lanes), 0)
        ck = lax.broadcasted_iota(jnp.int32, (GRID_W, lanes), 1) % GRID_W
        cs = jnp.clip(cq - C_WIN_COLS // 2, 0, GRID_W - C_WIN_COLS)
        ok = (ck >= cs) & (ck < cs + C_WIN_COLS)
        for i0 in range(C_WIN_ROWS):
            tile = jnp.broadcast_to(r_ref[i0], (GRID_W, lanes))
            tile = pltpu.roll(tile, lanes - (C_WIN_COLS - 1), 1)
            tile = _col_shift_select(tile, cq, False)
            b_ref[i0] = jnp.where(ok, tile, NEG)

    return _pallas(body, name=name, grid=(C_HEADS,),
                   in_specs=[pl.BlockSpec((None, C_WIN_ROWS, 1, lanes), lambda h: (h, 0, 0, 0))],
                   out_specs=pl.BlockSpec((None, C_WIN_ROWS, GRID_W, lanes), lambda h: (h, 0, 0, 0)),
                   out_shape=_sds((C_HEADS, C_WIN_ROWS, GRID_W, lanes), F32),
                   compiler_params=_params(("parallel",)))(rwin)


def rpb_reduce(dbias, name):
    lanes = dbias.shape[-1]
    wr = lanes // GRID_W

    def body(d_ref, o_ref):
        cq = lax.broadcasted_iota(jnp.int32, (GRID_W, lanes), 0)
        o_ref[...] = jnp.zeros_like(o_ref)
        for i0 in range(C_WIN_ROWS):
            tile = _col_shift_select(d_ref[i0], cq, True)
            tile = pltpu.roll(tile, C_WIN_COLS - 1, 1)
            vec = jnp.sum(tile, axis=0, keepdims=True)
            for w in range(wr):
                o_ref[i0 + w:i0 + w + 1, :] += vec[:, w * GRID_W:(w + 1) * GRID_W]

    return _pallas(body, name=name, grid=(C_HEADS,),
                   in_specs=[pl.BlockSpec((None, C_WIN_ROWS, GRID_W, lanes), lambda h: (h, 0, 0, 0))],
                   out_specs=pl.BlockSpec((None, 16, GRID_W), lambda h: (h, 0, 0)),
                   out_shape=_sds((C_HEADS, 16, GRID_W), F32),
                   compiler_params=_params(("parallel",)))(dbias)


def _store_or_add(ref, val, first):
    @pl.when(first)
    def _():
        ref[...] = val

    @pl.when(jnp.logical_not(first))
    def _():
        ref[...] += val


def c_attn_fwd(q, k, v, bias, name, nh=4):
    n = q.shape[0]
    rows, wr = _c_rows(n)
    wk = wr * GRID_W

    def body(q_ref, k_ref, v_ref, b_ref, o_ref, lse_ref):
        r = pl.program_id(1)
        start = pl.multiple_of(_c_row_start(r, rows, wr) * GRID_W, GRID_W)
        sls = [slice(h * HEAD, (h + 1) * HEAD) for h in range(nh)]
        ss = [lax.dot_general(q_ref[:, sl], k_ref[pl.ds(start, wk), sl], _DIMS["nt"], preferred_element_type=F32)
              * SCALE + b_ref[h] for h, sl in enumerate(sls)]
        mxs = [jnp.max(s, axis=-1, keepdims=True) for s in ss]
        ps = [jnp.exp(s - mx) for s, mx in zip(ss, mxs)]
        dens = [jnp.sum(p, axis=-1, keepdims=True) for p in ps]
        outs = [jnp.dot((p / den).astype(CDT), v_ref[pl.ds(start, wk), sl], preferred_element_type=F32)
                for p, den, sl in zip(ps, dens, sls)]
        for sl, o, mx, den in zip(sls, outs, mxs, dens):
            o_ref[:, sl] = o
            lse_ref[:, sl] = jnp.broadcast_to(mx + jnp.log(den), (GRID_W, HEAD))

    qspec = pl.BlockSpec((GRID_W, nh * HEAD), lambda h, r: (r, h))
    kspec = pl.BlockSpec((n, nh * HEAD), lambda h, r: (0, h))
    bspec = pl.BlockSpec((nh, None, GRID_W, wk), lambda h, r: (h, _c_bias_index(r, rows, wr), 0, 0))
    return _pallas(body, name=name, grid=(C_HEADS // nh, rows), in_specs=[qspec, kspec, kspec, bspec],
                   out_specs=[qspec, qspec], out_shape=[_sds(q.shape, F32), _sds(q.shape, F32)],
                   compiler_params=_params(("parallel", "arbitrary")))(q, k, v, bias)


def c_attn_bwd(q, k, v, bias, o, lse, do, name, nh=2):
    n = q.shape[0]
    rows, wr = _c_rows(n)
    wk = wr * GRID_W

    def body(q_ref, k_ref, v_ref, b_ref, o_ref, lse_ref, do_ref, dq_ref, dk_ref, dv_ref, db_ref):
        r = pl.program_id(1)
        rs = _c_row_start(r, rows, wr)
        start = pl.multiple_of(rs * GRID_W, GRID_W)

        @pl.when(r == 0)
        def _():
            dk_ref[...] = jnp.zeros_like(dk_ref)
            dv_ref[...] = jnp.zeros_like(dv_ref)

        prev = _c_row_start(jnp.maximum(r - 1, 0), rows, wr) - jnp.maximum(r - 1, 0)
        first = (r == 0) | (prev != rs - r)
        sls = [slice(h * HEAD, (h + 1) * HEAD) for h in range(nh)]
        k_ts = [k_ref[pl.ds(start, wk), sl] for sl in sls]
        v_ts = [v_ref[pl.ds(start, wk), sl] for sl in sls]
        q_ts = [q_ref[:, sl] for sl in sls]
        lse_ts = [lse_ref[:, sl][:, :1] for sl in sls]
        do_ts = [do_ref[:, sl] for sl in sls]
        deltas = [jnp.sum(do_t * o_ref[:, sl], axis=-1, keepdims=True) for do_t, sl in zip(do_ts, sls)]
        biases = [b_ref[h] for h in range(nh)]
        dk_old = [dk_ref[pl.ds(start, wk), sl] for sl in sls]
        dv_old = [dv_ref[pl.ds(start, wk), sl] for sl in sls]

        ps = [jnp.exp(lax.dot_general(q_t, k_t, _DIMS["nt"], preferred_element_type=F32) * SCALE + b - lse_t)
              for q_t, k_t, b, lse_t in zip(q_ts, k_ts, biases, lse_ts)]
        do_cs = [do_t.astype(CDT) for do_t in do_ts]
        dps = [lax.dot_general(do_c, v_t, _DIMS["nt"], preferred_element_type=F32) for do_c, v_t in zip(do_cs, v_ts)]
        dss = [p * (dp - delta) for p, dp, delta in zip(ps, dps, deltas)]
        ds_cs = [(ds * SCALE).astype(CDT) for ds in dss]
        dqs = [jnp.dot(ds_c, k_t, preferred_element_type=F32) for ds_c, k_t in zip(ds_cs, k_ts)]
        dv_new = [old + lax.dot_general(p.astype(CDT), do_c, _DIMS["tn"], preferred_element_type=F32)
                  for old, p, do_c in zip(dv_old, ps, do_cs)]
        dk_new = [old + lax.dot_general(ds_c, q_t, _DIMS["tn"], preferred_element_type=F32)
                  for old, ds_c, q_t in zip(dk_old, ds_cs, q_ts)]

        for h, sl in enumerate(sls):
            dq_ref[:, sl] = dqs[h]
            dk_ref[pl.ds(start, wk), sl] = dk_new[h]
            dv_ref[pl.ds(start, wk), sl] = dv_new[h]
        for h in range(nh):
            _store_or_add(db_ref.at[h], dss[h], first)

    qspec = pl.BlockSpec((GRID_W, nh * HEAD), lambda h, r: (r, h))
    kspec = pl.BlockSpec((n, nh * HEAD), lambda h, r: (0, h))
    bspec = pl.BlockSpec((nh, None, GRID_W, wk), lambda h, r: (h, _c_bias_index(r, rows, wr), 0, 0))
    return _pallas(body, name=name, grid=(C_HEADS // nh, rows),
                   in_specs=[qspec, kspec, kspec, bspec, qspec, qspec, qspec],
                   out_specs=[qspec, kspec, kspec, bspec],
                   out_shape=[_sds(q.shape, F32), _sds(k.shape, F32), _sds(k.shape, F32), _sds(bias.shape, F32)],
                   compiler_params=_params(("parallel", "arbitrary")))(q, k, v, bias, o, lse, do)


def _sigmoid(z):
    return 1.0 / (1.0 + jnp.exp(-z))


def _gate_specs(n, d):
    tm, tn = _tile(n, 256, 8), _tile(math.gcd(d, QKV_W), 1024)
    nj = d // tn
    tile = pl.BlockSpec((tm, tn), lambda i, j: (i, j))
    gl = [pl.BlockSpec((tm, tn), functools.partial(lambda i, j, b: (i, (QKV_W + b * d) // tn + j), b=b)) for b in range(3)]
    return tm, tn, nj, tile, gl


def gate_merge(proj, ys, name):
    n, d = ys[0].shape
    tm, tn, nj, tile, gl = _gate_specs(n, d)

    def body(g0, g1, g2, y0, y1, y2, out_ref):
        acc = (_sigmoid(g0[...].astype(F32)) * y0[...] + _sigmoid(g1[...].astype(F32)) * y1[...]
               + _sigmoid(g2[...].astype(F32)) * y2[...])
        out_ref[...] = acc.astype(out_ref.dtype)

    return _pallas(body, name=name, grid=(n // tm, nj), in_specs=gl + [tile] * 3, out_specs=tile,
                   out_shape=_sds((n, d), CDT), compiler_params=_params(("parallel", "parallel")))(proj, proj, proj, *ys)


def gate_bwd(proj, ys, dmerged, name):
    n, d = dmerged.shape
    tm, tn, nj, _, _ = _gate_specs(n, d)

    def body(g_ref, y0, y1, y2, dm_ref, dy_ref, dp_ref):
        b = pl.program_id(2)
        y = jnp.where(b == 0, y0[...], jnp.where(b == 1, y1[...], y2[...]))
        dm = dm_ref[...]
        sg = _sigmoid(g_ref[...].astype(F32))
        dy_ref[...] = (dm * sg).astype(dy_ref.dtype)
        dp_ref[...] = (dm * y * sg * (1.0 - sg)).astype(dp_ref.dtype)

    gl = pl.BlockSpec((tm, tn), lambda i, j, b: (i, QKV_W // tn + b * nj + j))
    tile = pl.BlockSpec((tm, tn), lambda i, j, b: (i, j))
    return _pallas(body, name=name, grid=(n // tm, nj, 3), in_specs=[gl, tile, tile, tile, tile],
                   out_specs=[pl.BlockSpec((None, tm, tn), lambda i, j, b: (b, i, j)), gl],
                   out_shape=[_sds((3, n, d), CDT), _sds(proj.shape, CDT)],
                   compiler_params=_params(("parallel", "parallel", "arbitrary")))(proj, *ys, dmerged)


def gate_up_swiglu(h2, wg, name):
    n, d = h2.shape
    ns = wg.shape[2]
    ff = 2 * ns
    tm, tn = _tile(n, 512, 8), _tile(ns, 1408)
    nj = ns // tn

    def body(a_ref, bg_ref, bu_ref, gu_ref, act_ref):
        a = a_ref[...].astype(CDT)
        gt = jnp.dot(a, bg_ref[...].astype(CDT), preferred_element_type=F32)
        up = jnp.dot(a, bu_ref[...].astype(CDT), preferred_element_type=F32)
        gu_ref[0] = gt.astype(gu_ref.dtype)
        gu_ref[1] = up.astype(gu_ref.dtype)
        act_ref[...] = (gt * _sigmoid(gt) * up).astype(act_ref.dtype)

    return _pallas(body, name=name, grid=(n // tm, 2 * nj),
                   in_specs=[pl.BlockSpec((tm, d), lambda i, j: (i, 0)),
                             pl.BlockSpec((None, d, tn), lambda i, j: (j // nj, 0, j % nj)),
                             pl.BlockSpec((None, d, tn), lambda i, j: (2 + j // nj, 0, j % nj))],
                   out_specs=[pl.BlockSpec((2, tm, tn), lambda i, j: (0, i, j)), pl.BlockSpec((tm, tn), lambda i, j: (i, j))],
                   out_shape=[_sds((2, n, ff), ADT), _sds((n, ff), CDT)],
                   compiler_params=_params(("parallel", "parallel")))(h2, wg, wg)


def d_gate_up(dx, w_down, gu, name):
    n, d = dx.shape
    ff = w_down.shape[0]
    tm, tn = _tile(n, 1024, 8), _tile(ff, 512)

    def body(a_ref, b_ref, gu_ref, d_ref):
        da = lax.dot_general(a_ref[...].astype(CDT), b_ref[...].astype(CDT), _DIMS["nt"], preferred_element_type=F32)
        gt, up = gu_ref[0].astype(F32), gu_ref[1].astype(F32)
        sg = _sigmoid(gt)
        d_ref[0] = (da * up * (sg + gt * sg * (1.0 - sg))).astype(d_ref.dtype)
        d_ref[1] = (da * gt * sg).astype(d_ref.dtype)

    blk = pl.BlockSpec((2, tm, tn), lambda i, j: (0, i, j))
    return _pallas(body, name=name, grid=(n // tm, ff // tn),
                   in_specs=[pl.BlockSpec((tm, d), lambda i, j: (i, 0)), pl.BlockSpec((tn, d), lambda i, j: (j, 0)), blk],
                   out_specs=blk, out_shape=_sds((2, n, ff), CDT),
                   compiler_params=_params(("parallel", "parallel")))(dx, w_down, gu)


def loss_head(y, target, name):
    n, d = y.shape
    tm = _tile(n, 512, 8)
    nsteps = n // tm

    def body(y_ref, t_ref, l_ref, dy_ref, acc_ref):
        i = pl.program_id(0)
        e = y_ref[...] - t_ref[...]
        dy_ref[...] = e * (1.0 / d)
        part = jnp.sum((e * e).reshape(tm // 8, 8, d), axis=0)

        @pl.when(i == 0)
        def _():
            acc_ref[...] = part

        @pl.when(i > 0)
        def _():
            acc_ref[...] += part

        @pl.when(i == nsteps - 1)
        def _():
            tot = jnp.sum(jnp.sum(acc_ref[...], axis=1, keepdims=True), axis=0, keepdims=True) * (0.5 / d)
            l_ref[...] = jnp.broadcast_to(tot, (8, HEAD))

    row = pl.BlockSpec((tm, d), lambda i: (i, 0))
    return _pallas(body, name=name, grid=(nsteps,), in_specs=[row, row],
                   out_specs=[pl.BlockSpec((8, HEAD), lambda i: (0, 0)), row],
                   out_shape=[_sds((8, HEAD), F32), _sds((n, d), F32)],
                   scratch_shapes=[pltpu.VMEM((8, d), F32)],
                   compiler_params=_params(("arbitrary",)))(y, target)


def adamw(w, g, m, v, name):
    r, c = w.shape
    tr = _tile(r, max(8, (1 << 19) // c), 8)
    c1 = 1.0 - ADAM_B1 ** ADAM_STEP
    c2 = 1.0 - ADAM_B2 ** ADAM_STEP

    def body(w_ref, g_ref, m_ref, v_ref, d_ref, mo_ref, vo_ref):
        gv = g_ref[...]
        mn = ADAM_B1 * m_ref[...] + (1.0 - ADAM_B1) * gv
        vn = ADAM_B2 * v_ref[...] + (1.0 - ADAM_B2) * (gv * gv)
        d_ref[...] = -ADAM_LR * ((mn / c1) / (jnp.sqrt(vn / c2) + ADAM_EPS) + ADAM_WD * w_ref[...])
        mo_ref[...] = mn
        vo_ref[...] = vn

    row = pl.BlockSpec((tr, c), lambda i: (i, 0))
    return _pallas(body, name=name, grid=(r // tr,), in_specs=[row] * 4, out_specs=[row] * 3,
                   out_shape=[_sds((r, c), F32)] * 3, compiler_params=_params(("parallel",)))(w, g, m, v)


ANY = pl.BlockSpec(memory_space=pl.ANY)


def _place():
    x, y, c = lax.axis_index("x"), lax.axis_index("y"), lax.axis_index("c")
    return x, y, c, [(1 - x, y), (x, 1 - y), (1 - x, 1 - y)]


def _rcopy(src, dst, send_sems, recv_sems, k, to):
    return pltpu.make_async_remote_copy(src_ref=src, dst_ref=dst, send_sem=send_sems.at[k], recv_sem=recv_sems.at[k],
                                        device_id=to, device_id_type=MESH)


def cast_place(chip_idx, shards, layer, name):
    _, k, ns = shards.shape
    tr = _tile(k, max(16, (1 << 19) // ns), 16)

    def body(k_ref, s_ref, o_ref):
        o_ref[...] = s_ref[...].astype(o_ref.dtype)

    gs = pltpu.PrefetchScalarGridSpec(
        num_scalar_prefetch=1, grid=(k // tr,),
        in_specs=[pl.BlockSpec((None, tr, ns), lambda i, k_ref: (layer, i, 0))],
        out_specs=pl.BlockSpec((None, tr, ns), lambda i, k_ref: (k_ref[0], i, 0)))
    return _pallas(body, name=name, grid_spec=gs, out_shape=_sds((N_CHIPS, k, ns), CDT),
                   compiler_params=_params(("parallel",)))(chip_idx, shards)


HBM = pl.BlockSpec(memory_space=pltpu.HBM)
SEM = pl.BlockSpec(memory_space=pltpu.SEMAPHORE)
EFFECT = pltpu.SideEffectType.DATAFLOW_SIDE_EFFECTING


def _in_hbm(a):
    return pltpu.with_memory_space_constraint(a, pltpu.HBM)


def _gather_copies(refs, send_sems, recv_sems):
    x, y, c, chips = _place()
    me = 2 * x + y
    out = []
    for t, ref in enumerate(refs):
        kh = ref.shape[1] // 2
        for j, (px, py) in enumerate(chips):
            send = _rcopy(ref.at[me, pl.ds(c * kh, kh)], ref.at[me, pl.ds(c * kh, kh)], send_sems, recv_sems,
                          3 * t + j, (px, py, c))
            land = ref.at[2 * px + py, pl.ds(c * kh, kh)]
            out.append((send, _rcopy(land, land, send_sems, recv_sems, 3 * t + j, (px, py, c))))
    return out


def gather_start(bufs, after, name):
    nt = len(bufs)

    def body(*refs):
        ins, send_sems, recv_sems, token = refs[:nt], refs[nt + 1], refs[nt + 2], refs[-1]
        for send, _ in _gather_copies(ins, send_sems, recv_sems):
            send.start()
        token[...] = jnp.zeros_like(token)

    outs = _pallas(body, name=name, in_specs=[HBM] * nt + [ANY],
                   out_specs=(SEM, SEM) + (HBM,) * nt + (pl.BlockSpec(memory_space=pltpu.VMEM),),
                   out_shape=(pltpu.SemaphoreType.DMA((3 * nt,)), pltpu.SemaphoreType.DMA((3 * nt,)))
                   + tuple(pltpu.HBM(b.shape, b.dtype) for b in bufs) + (_sds((8, HEAD), F32),),
                   input_output_aliases={t: 2 + t for t in range(nt)},
                   compiler_params=pltpu.CompilerParams(has_side_effects=EFFECT))(*[_in_hbm(b) for b in bufs], after)
    return outs[0], outs[1], list(outs[2:2 + nt]), outs[-1]


def gather_wait(bufs, send_sems, recv_sems, after, name):
    nt = len(bufs)

    def body(*refs):
        ins, s_sems, r_sems = refs[:nt], refs[nt], refs[nt + 1]
        for send, land in _gather_copies(ins, s_sems, r_sems):
            send.wait_send()
            land.wait_recv()

    return _pallas(body, name=name, in_specs=[HBM] * nt + [SEM, SEM, ANY], out_specs=[HBM] * nt,
                   out_shape=[pltpu.HBM(b.shape, b.dtype) for b in bufs],
                   input_output_aliases={t: t for t in range(nt)},
                   compiler_params=pltpu.CompilerParams(has_side_effects=EFFECT))(*bufs, send_sems, recv_sems, after)


def pair_forward(bufs, name):
    nt = len(bufs)

    def body(*refs):
        outs = refs[nt:2 * nt]
        send_sems, recv_sems = refs[2 * nt:]
        x, y, c, chips = _place()
        cps = []
        for t in range(nt):
            kh = outs[t].shape[1] // 2
            for j, (px, py) in enumerate(chips):
                blk = outs[t].at[2 * px + py, pl.ds(c * kh, kh)]
                cps.append(_rcopy(blk, blk, send_sems, recv_sems, 3 * t + j, (x, y, 1 - c)))
                cps[-1].start()
        for t in range(nt):
            kh = outs[t].shape[1] // 2
            for j, (px, py) in enumerate(chips):
                blk = outs[t].at[2 * px + py, pl.ds((1 - c) * kh, kh)]
                _rcopy(blk, blk, send_sems, recv_sems, 3 * t + j, (x, y, 1 - c)).wait_recv()
        for cp in cps:
            cp.wait_send()

    return _pallas(body, name=name, in_specs=[ANY] * nt, out_specs=[ANY] * nt,
                   out_shape=[_sds(b.shape, b.dtype) for b in bufs],
                   input_output_aliases={t: t for t in range(nt)},
                   scratch_shapes=[pltpu.SemaphoreType.DMA((3 * nt,)), pltpu.SemaphoreType.DMA((3 * nt,))],
                   compiler_params=pltpu.CompilerParams(has_side_effects=True))(*bufs)


def pair_exchange(grads, name):
    nt = len(grads)

    def body(*refs):
        ins, outs = refs[:nt], refs[nt:2 * nt]
        send_sems, recv_sems = refs[2 * nt:]
        x, y, c, _ = _place()
        sibling = (x, y, 1 - c)
        cps = []
        for t in range(nt):
            kh = ins[t].shape[1] // 2
            cps.append(_rcopy(ins[t].at[:, pl.ds((1 - c) * kh, kh), :], outs[t], send_sems, recv_sems, t, sibling))
            cps[-1].start()
        for cp in cps:
            cp.wait_recv()
        for cp in cps:
            cp.wait_send()

    return _pallas(body, name=name, in_specs=[ANY] * nt, out_specs=[ANY] * nt,
                   out_shape=[_sds((N_CHIPS, g.shape[1] // 2, g.shape[2]), g.dtype) for g in grads],
                   scratch_shapes=[pltpu.SemaphoreType.DMA((nt,)), pltpu.SemaphoreType.DMA((nt,))],
                   compiler_params=pltpu.CompilerParams(has_side_effects=True))(*grads)


def _exchange_copies(sums, lands, send_sems, recv_sems):
    x, y, c, chips = _place()
    return [_rcopy(s.at[2 * px + py], l.at[j], send_sems, recv_sems, 3 * t + j, (px, py, c))
            for t, (s, l) in enumerate(zip(sums, lands)) for j, (px, py) in enumerate(chips)]


def exchange_start(sums, name):
    nt = len(sums)
    lands = [lax.empty((3,) + s.shape[1:], s.dtype) for s in sums]

    def body(*refs):
        ins, zones, send_sems, recv_sems, token = refs[:nt], refs[nt:2 * nt], refs[2 * nt], refs[2 * nt + 1], refs[-1]
        for cp in _exchange_copies(ins, zones, send_sems, recv_sems):
            cp.start()
        token[...] = jnp.zeros_like(token)

    outs = _pallas(body, name=name, in_specs=[HBM] * (2 * nt),
                   out_specs=(SEM, SEM) + (HBM,) * (2 * nt) + (pl.BlockSpec(memory_space=pltpu.VMEM),),
                   out_shape=(pltpu.SemaphoreType.DMA((3 * nt,)), pltpu.SemaphoreType.DMA((3 * nt,)))
                   + tuple(pltpu.HBM(a.shape, a.dtype) for a in list(sums) + lands) + (_sds((8, HEAD), F32),),
                   input_output_aliases={t: 2 + t for t in range(2 * nt)},
                   compiler_params=pltpu.CompilerParams(has_side_effects=EFFECT))(*[_in_hbm(a) for a in list(sums) + lands])
    return outs[0], outs[1], list(outs[2:2 + nt]), list(outs[2 + nt:2 + 2 * nt]), outs[-1]


def exchange_wait(sums, lands, send_sems, recv_sems, after, name):
    nt = len(sums)

    def body(*refs):
        ins, zones, s_sems, r_sems = refs[:nt], refs[nt:2 * nt], refs[2 * nt], refs[2 * nt + 1]
        for cp in _exchange_copies(ins, zones, s_sems, r_sems):
            cp.wait_send()
            cp.wait_recv()

    outs = _pallas(body, name=name, in_specs=[HBM] * (2 * nt) + [SEM, SEM, ANY], out_specs=[HBM] * (2 * nt),
                   out_shape=[pltpu.HBM(a.shape, a.dtype) for a in list(sums) + list(lands)],
                   input_output_aliases={t: t for t in range(2 * nt)},
                   compiler_params=pltpu.CompilerParams(has_side_effects=EFFECT))(*sums, *lands, send_sems, recv_sems, after)
    return list(outs[:nt]), list(outs[nt:])


def pair_share(halves, name):
    nt = len(halves)

    def body(*refs):
        ins, outs = refs[:nt], refs[nt:2 * nt]
        send_sems, recv_sems = refs[2 * nt:]
        x, y, c, _ = _place()
        cps = []
        for t in range(nt):
            cps.append(_rcopy(ins[t], outs[t], send_sems, recv_sems, t, (x, y, 1 - c)))
            cps[-1].start()
        for cp in cps:
            cp.wait_recv()
        for cp in cps:
            cp.wait_send()

    return _pallas(body, name=name, in_specs=[ANY] * nt, out_specs=[ANY] * nt,
                   out_shape=[_sds(h.shape, h.dtype) for h in halves],
                   scratch_shapes=[pltpu.SemaphoreType.DMA((nt,)), pltpu.SemaphoreType.DMA((nt,))],
                   compiler_params=pltpu.CompilerParams(has_side_effects=True))(*halves)


def small_allreduce(pack, name):
    r = pack.shape[0]

    def body(in_ref, out_ref, buf, send_sems, recv_sems):
        x, y, c, _ = _place()
        me = 4 * x + 2 * y + c
        sends = []
        for k in range(1, 8):
            to = ((x + ((k >> 2) & 1)) % 2, (y + ((k >> 1) & 1)) % 2, (c + (k & 1)) % 2)
            cp = _rcopy(in_ref, buf.at[me], send_sems, recv_sems, k - 1, to)
            cp.start()
            sends.append((cp, to))
        buf[pl.ds(me, 1)] = in_ref[...][None]
        for k, (_, to) in enumerate(sends):
            peer = 4 * to[0] + 2 * to[1] + to[2]
            _rcopy(in_ref, buf.at[peer], send_sems, recv_sems, k, to).wait_recv()
        for cp, _ in sends:
            cp.wait_send()
        acc = buf[0]
        for d in range(1, 8):
            acc = acc + buf[d]
        out_ref[...] = acc

    vm = pl.BlockSpec(memory_space=pltpu.VMEM)
    return _pallas(body, name=name, in_specs=[vm], out_specs=vm, out_shape=_sds((r, HEAD), F32),
                   scratch_shapes=[pltpu.VMEM((8, r, HEAD), F32), pltpu.SemaphoreType.DMA((7,)),
                                   pltpu.SemaphoreType.DMA((7,))],
                   compiler_params=pltpu.CompilerParams(has_side_effects=True))(pack)


def add_halves(c_idx, grad, other, name):
    _, k, ns = grad.shape
    kh = k // 2
    tr = _tile(kh, max(16, (1 << 19) // ns), 16)
    nr = kh // tr

    def body(c_ref, g_ref, o_ref, s_ref):
        s_ref[...] = (g_ref[...].astype(F32) + o_ref[...].astype(F32)).astype(s_ref.dtype)

    gs = pltpu.PrefetchScalarGridSpec(
        num_scalar_prefetch=1, grid=(N_CHIPS, nr),
        in_specs=[pl.BlockSpec((None, tr, ns), lambda g, i, c_ref: (g, c_ref[0] * nr + i, 0)),
                  pl.BlockSpec((None, tr, ns), lambda g, i, c_ref: (g, i, 0))],
        out_specs=pl.BlockSpec((None, tr, ns), lambda g, i, c_ref: (g, i, 0)))
    return _pallas(body, name=name, grid_spec=gs, out_shape=_sds((N_CHIPS, kh, ns), XDT),
                   compiler_params=_params(("parallel", "parallel")))(c_idx, grad, other)


def add_chips(chip_idx, sums, recv, stack, layer, n_layers, name):
    _, kh, ns = sums.shape
    tr = _tile(kh, max(16, (1 << 19) // ns), 16)
    has_stack = stack is not None

    def body(k_ref, s_ref, r0, r1, r2, *rest):
        o_ref = rest[-1]
        o_ref[...] = ((s_ref[...].astype(F32) + r0[...].astype(F32)) + r1[...].astype(F32)) + r2[...].astype(F32)

    rspec = [pl.BlockSpec((None, tr, ns), functools.partial(lambda i, k_ref, j: (j, i, 0), j=j)) for j in range(3)]
    gs = pltpu.PrefetchScalarGridSpec(
        num_scalar_prefetch=1, grid=(kh // tr,),
        in_specs=[pl.BlockSpec((None, tr, ns), lambda i, k_ref: (k_ref[0], i, 0))] + rspec + ([ANY] if has_stack else []),
        out_specs=pl.BlockSpec((None, tr, ns), lambda i, k_ref: (layer, i, 0)))
    args = (chip_idx, sums, recv, recv, recv) + ((stack,) if has_stack else ())
    return _pallas(body, name=name, grid_spec=gs, out_shape=_sds((n_layers, kh, ns), F32),
                   input_output_aliases={5: 0} if has_stack else {},
                   compiler_params=_params(("parallel",)))(*args)


def adamw_big(c_idx, w, m, v, mine, other, name):
    nl, k, ns = w.shape
    kh = k // 2
    tr = _tile(kh, max(8, (1 << 18) // ns), 8)
    nr = kh // tr
    c1 = 1.0 - ADAM_B1 ** ADAM_STEP
    c2 = 1.0 - ADAM_B2 ** ADAM_STEP

    def body(c_ref, w_ref, m_ref, v_ref, a_ref, b_ref, g_ref, d_ref, mo_ref, vo_ref):
        gv = jnp.where(pl.program_id(2) == c_ref[0], a_ref[...], b_ref[...])
        mn = ADAM_B1 * m_ref[...] + (1.0 - ADAM_B1) * gv
        vn = ADAM_B2 * v_ref[...] + (1.0 - ADAM_B2) * (gv * gv)
        g_ref[...] = gv
        d_ref[...] = -ADAM_LR * ((mn / c1) / (jnp.sqrt(vn / c2) + ADAM_EPS) + ADAM_WD * w_ref[...])
        mo_ref[...] = mn
        vo_ref[...] = vn

    full = pl.BlockSpec((None, tr, ns), lambda l, i, hh, c_ref: (l, hh * nr + i, 0))
    half = pl.BlockSpec((None, tr, ns), lambda l, i, hh, c_ref: (l, i, 0))
    gs = pltpu.PrefetchScalarGridSpec(num_scalar_prefetch=1, grid=(nl, nr, 2),
                                      in_specs=[full, full, full, half, half], out_specs=[full] * 4)
    return _pallas(body, name=name, grid_spec=gs, out_shape=[_sds(w.shape, F32)] * 4,
                   compiler_params=_params(("parallel", "parallel", "arbitrary")))(c_idx, w, m, v, mine, other)


W_NAMES = ("w_in", "w_br_a", "w_br_b", "w_br_c", "w_o", "w_gate_up", "w_down")


def _rope_tables(n):
    half = HEAD // 2
    inv_freq = ROPE_THETA ** (-jnp.arange(half, dtype=F32) * 2.0 / HEAD)
    ang = jnp.arange(n, dtype=F32)[:, None] * inv_freq[None, :]
    cos, sin = jnp.cos(ang), jnp.sin(ang)
    return jnp.concatenate([cos, cos], axis=-1), jnp.concatenate([-sin, sin], axis=-1)


def _rpb_windows(rpb):
    pad = jnp.pad(rpb, ((0, 0), (0, 1), (0, GRID_W - rpb.shape[2])))
    wins = [pad[:, i0:i0 + C_WIN_ROWS].reshape(C_HEADS, 1, C_WIN_ROWS * GRID_W) for i0 in range(C_WIN_ROWS)]
    return jnp.stack(wins, axis=1)


def _rows(t):
    return t.reshape(-1, t.shape[-1])


def _like(t, ref):
    return t.reshape(ref.shape)


def layer_fwd(x, p, w, cos2, sin2, rest=None):
    n, d = x.shape
    s = {"x": x}
    s["h"] = rmsnorm_fwd(x, p["norm1_g"], "norm1")
    s["proj"] = mm_x_wcol(s["h"], w["w_in"], ADT, "proj")
    gains = jnp.pad(p["qk_norm_g"], ((0, 2), (0, 0)))
    pp = s["pp"] = qk_prep(s["proj"], gains, cos2, sin2, "qk_prep")
    sink = p["sink_a"].reshape(1, A_Q_HEADS)
    s["oa"], s["lse_a"] = band_attn_fwd(pp["qa"], pp["ka"], pp["va"], sink, seqs=A_KV_HEADS, G=A_GROUP,
                                        nh=A_KV_HEADS, radius=A_RADIUS, name="attn_a")
    s["ob"], s["lse_b"] = [], []
    for g, dil in enumerate(B_DILS):
        o, lse = band_attn_fwd(_rows(pp[f"qb{g}"]), _rows(pp[f"kb{g}"]), _rows(pp[f"vb{g}"]), None, seqs=B_HG, G=1,
                               nh=B_HG, radius=B_RADIUS, classes=dil, name=f"attn_b{g}")
        s["ob"].append(_like(o, pp[f"qb{g}"]))
        s["lse_b"].append(_like(lse, pp[f"qb{g}"]))
    ob = b_combine_fwd(s["ob"], s["lse_b"], "b_combine")
    s["bias"] = rpb_expand(_rpb_windows(p["rpb_c"]), "rpb_expand")
    s["oc"], s["lse_c"] = c_attn_fwd(pp["qc"], pp["kc"], pp["vc"], s["bias"], "attn_c")
    s["o_in"] = (s["oa"], ob, s["oc"])
    if rest is not None:
        w = {**w, **rest(s["oc"])}
    s["w"] = w
    s["ys"] = [mm_x_wcol(o, w[k], ADT, "branch_" + k[-1]) for o, k in zip(s["o_in"], ("w_br_a", "w_br_b", "w_br_c"))]
    s["merged"] = gate_merge(s["proj"], s["ys"], "gate_merge")
    s["x_mid"] = mm_x_w(s["merged"], w["w_o"], "out_proj", res=x)
    s["h2"] = rmsnorm_fwd(s["x_mid"], p["norm2_g"], "norm2")
    s["gu"], s["act"] = gate_up_swiglu(s["h2"], w["w_gate_up"], "gate_up")
    x_out = mm_x_w(s["act"], w["w_down"], "down", res=s["x_mid"], tk_pref=2816)
    return x_out, s


def layer_bwd(dx_out, s, p, cos2, sin2, on_dws, early=False):
    n, d = dx_out.shape
    pp, w = s["pp"], s["w"]
    dgu = d_gate_up(dx_out, w["w_down"], s["gu"], "d_gate_up")
    dw_down = mm_aT_d(s["act"], dx_out, "dw_down")
    dh2 = mm_x_wcolT(dgu, w["w_gate_up"], "d_h2", stacked_in=2)
    dw_gu = mm_aT_d_wcol(s["h2"], dgu, "dw_gate_up", tn_pref=1408, stacked_in=2)
    dx_mid, dg2 = rmsnorm_bwd(s["x_mid"], p["norm2_g"], dh2, dx_out, "norm2_bwd")
    dws = {"w_gate_up": dw_gu, "w_down": dw_down.reshape(N_CHIPS, dw_down.shape[0] // N_CHIPS, d)}
    token = on_dws(dws) if early else None

    dmerged = mm_x_wT(dx_mid, w["w_o"], "d_merged", after=token)
    dw_o = mm_aT_d(s["merged"], dx_mid, "dw_o")
    dys, dproj = gate_bwd(s["proj"], s["ys"], dmerged, "gate_bwd")
    dos, dw_br = [], []
    for b, (o, k) in enumerate(zip(s["o_in"], ("w_br_a", "w_br_b", "w_br_c"))):
        dos.append(mm_x_wcolT(dys, w[k], "d_o_" + k[-1], lead=b))
        dw_br.append(mm_aT_d_wcol(o, dys, "dw_br_" + k[-1], lead=b))

    grads = {}
    sink = p["sink_a"].reshape(1, A_Q_HEADS)
    grads["qa"], grads["ka"], grads["va"], dsink = band_attn_bwd(
        pp["qa"], pp["ka"], pp["va"], sink, s["oa"], s["lse_a"], dos[0], None,
        seqs=A_KV_HEADS, G=A_GROUP, nh=A_KV_HEADS, radius=A_RADIUS, name="attn_a_bwd")
    dobs, dlses = b_combine_bwd(dos[1], s["ob"], s["lse_b"], "b_combine_bwd")
    for g, dil in enumerate(B_DILS):
        dq, dk, dv = band_attn_bwd(_rows(pp[f"qb{g}"]), _rows(pp[f"kb{g}"]), _rows(pp[f"vb{g}"]), None,
                                   _rows(s["ob"][g]), _rows(s["lse_b"][g]), _rows(dobs[g]), _rows(dlses[g]),
                                   seqs=B_HG, G=1, nh=2, radius=B_RADIUS, classes=dil, name=f"attn_b{g}_bwd")
        grads[f"qb{g}"], grads[f"kb{g}"], grads[f"vb{g}"] = [_like(t, pp[f"qb{g}"]) for t in (dq, dk, dv)]
    grads["qc"], grads["kc"], grads["vc"], dbias = c_attn_bwd(pp["qc"], pp["kc"], pp["vc"], s["bias"], s["oc"],
                                                              s["lse_c"], dos[2], "attn_c_bwd")
    drpb = rpb_reduce(dbias, "rpb_reduce")[:, :2 * C_WIN_ROWS - 1, :2 * C_WIN_COLS - 1]
    gains = jnp.pad(p["qk_norm_g"], ((0, 2), (0, 0)))
    dproj, dgains = qk_prep_bwd(s["proj"], gains, cos2, sin2, grads, dproj, "qk_prep_bwd")
    dw_in = mm_aT_d_wcol(s["h"], dproj, "dw_in")
    rest = {"w_in": dw_in, "w_br_a": dw_br[0], "w_br_b": dw_br[1], "w_br_c": dw_br[2],
            "w_o": dw_o.reshape(N_CHIPS, d // N_CHIPS, d)}
    token = on_dws(rest if early else {**rest, **dws})
    dh = mm_x_wcolT(dproj, w["w_in"], "d_h", after=token)
    dx_in, dg1 = rmsnorm_bwd(s["x"], p["norm1_g"], dh, dx_mid, "norm1_bwd")
    small = {"norm1_g": dg1[0], "qk_norm_g": dgains[:6], "sink_a": dsink[0, :, 0],
             "rpb_c": drpb, "norm2_g": dg2[0]}
    return dx_in, small


SMALL_NAMES = ("norm1_g", "qk_norm_g", "sink_a", "rpb_c", "norm2_g")


def _pack_small(parts, extra=None):
    flat = [parts[k].reshape(-1) for k in SMALL_NAMES]
    flat.append(jnp.zeros((1,), F32) if extra is None else extra.reshape(1))
    v = jnp.concatenate(flat)
    rows = -(-v.shape[0] // (8 * HEAD)) * 8
    return jnp.pad(v, (0, rows * HEAD - v.shape[0])).reshape(rows, HEAD)


def _unpack_small(pack, like):
    v = pack.reshape(-1)
    out, off = {}, 0
    for k in SMALL_NAMES:
        size = math.prod(like[k].shape)
        out[k] = v[off:off + size].reshape(like[k].shape)
        off += size
    return out, v[off]


def kernel(x, norm1_g, w_in, qk_norm_g, sink_a, rpb_c, w_br_a, w_br_b, w_br_c, w_o, norm2_g, w_gate_up, w_down, loss_target, m_norm1_g, m_w_in, m_qk_norm_g, m_sink_a, m_rpb_c, m_w_br_a, m_w_br_b, m_w_br_c, m_w_o, m_norm2_g, m_w_gate_up, m_w_down, v_norm1_g, v_w_in, v_qk_norm_g, v_sink_a, v_rpb_c, v_w_br_a, v_w_br_b, v_w_br_c, v_w_o, v_norm2_g, v_w_gate_up, v_w_down):
    big = dict(w_in=w_in, w_br_a=w_br_a, w_br_b=w_br_b, w_br_c=w_br_c, w_o=w_o, w_gate_up=w_gate_up, w_down=w_down)
    big_m = dict(w_in=m_w_in, w_br_a=m_w_br_a, w_br_b=m_w_br_b, w_br_c=m_w_br_c, w_o=m_w_o, w_gate_up=m_w_gate_up, w_down=m_w_down)
    big_v = dict(w_in=v_w_in, w_br_a=v_w_br_a, w_br_b=v_w_br_b, w_br_c=v_w_br_c, w_o=v_w_o, w_gate_up=v_w_gate_up, w_down=v_w_down)
    small = dict(norm1_g=norm1_g, qk_norm_g=qk_norm_g, sink_a=sink_a, rpb_c=rpb_c, norm2_g=norm2_g)
    small_m = dict(norm1_g=m_norm1_g, qk_norm_g=m_qk_norm_g, sink_a=m_sink_a, rpb_c=m_rpb_c, norm2_g=m_norm2_g)
    small_v = dict(norm1_g=v_norm1_g, qk_norm_g=v_qk_norm_g, sink_a=v_sink_a, rpb_c=v_rpb_c, norm2_g=v_norm2_g)
    n_layers = w_in.shape[0]
    n, d = x.shape[1], x.shape[2]
    c_idx = lax.axis_index("c").astype(jnp.int32).reshape(1)
    chip_idx = (2 * lax.axis_index("x") + lax.axis_index("y")).astype(jnp.int32).reshape(1)
    cos2, sin2 = _rope_tables(n)

    def gathered(names, started, after):
        send_sems, recv_sems, bufs, _ = started
        got = pair_forward(gather_wait(bufs, send_sems, recv_sems, after, "gather_wait"), "pair_forward")
        w = dict(zip(names, got))
        if "w_o" in w:
            w["w_o"] = w["w_o"].reshape(d, d)
            w["w_down"] = w["w_down"].reshape(-1, d)
        return w

    def start_gather(names, l, after):
        return gather_start([cast_place(chip_idx, big[k], l, "cast_" + k) for k in names], after, "gather_start")

    first, others_0 = W_NAMES[:1], W_NAMES[1:]
    started = start_gather(first, 0, chip_idx)
    started_rest = start_gather(others_0, 0, started[3])
    weights = gathered(first, started, started_rest[3])
    rest = lambda after: gathered(others_0, started_rest, after)

    xs, saved = x[0], []
    for l in range(n_layers):
        p = {k: small[k][l] for k in SMALL_NAMES}
        if l + 1 < n_layers:
            started = start_gather(W_NAMES, l + 1, weights["w_in"])
            p["norm1_g"] = p["norm1_g"] + started[3][0, 0]
        xs, s = layer_fwd(xs, p, weights, cos2, sin2, rest)
        saved.append(s)
        if l + 1 < n_layers:
            weights, rest = gathered(W_NAMES, started, xs), None
    loss_tile, dx = loss_head(xs, loss_target[0], "loss_head")

    halves = {k: None for k in W_NAMES}
    small_g = [None] * n_layers
    pending = []

    def finish_exchanges(after):
        for l, names, (send_sems, recv_sems, sums, lands, _) in pending:
            sums, from_chips = exchange_wait(sums, lands, send_sems, recv_sems, after, "exchange_wait")
            for k, sm, r in zip(names, sums, from_chips):
                halves[k] = add_chips(chip_idx, sm, r, halves[k], l, n_layers, "add_chips_" + k)
        pending.clear()

    def make_on_dws(l, wait_first):
        def on_dws(dws):
            names = [k for k in W_NAMES if k in dws]
            parts = [dws[k] for k in names]
            if wait_first:
                finish_exchanges(parts[0])
            from_sibling = pair_exchange(parts, "pair_exchange")
            sums = [add_halves(c_idx, g, o, "add_halves_" + k) for g, o, k in zip(parts, from_sibling, names)]
            pending.append((l, names, exchange_start(sums, "exchange_start")))
            return pending[-1][2][4]
        return on_dws

    for l in reversed(range(n_layers)):
        p = {k: small[k][l] for k in SMALL_NAMES}
        last = l == 0
        dx, small_g[l] = layer_bwd(dx, saved[l], p, cos2, sin2, make_on_dws(l, not last), early=last)
    finish_exchanges(dx)
    halves = [halves[k] for k in W_NAMES]
    others = pair_share(halves, "pair_share")

    mine = {k: jnp.stack([small_g[l][k] for l in range(n_layers)]) for k in SMALL_NAMES}
    total = small_allreduce(_pack_small(mine, loss_tile[0, 0]), "small_allreduce")
    grad_small, loss = _unpack_small(total, small)

    outs = {}
    for k, mine_half, other_half in zip(W_NAMES, halves, others):
        outs[k] = adamw_big(c_idx, big[k], big_m[k], big_v[k], mine_half, other_half, "adamw_" + k)
    res = adamw(_pack_small(small), _pack_small(grad_small), _pack_small(small_m), _pack_small(small_v), "adamw_small")
    unp = [_unpack_small(t, small)[0] for t in res]
    for k in SMALL_NAMES:
        outs[k] = (grad_small[k],) + tuple(u[k] for u in unp)

    order = ("norm1_g", "w_in", "qk_norm_g", "sink_a", "rpb_c", "w_br_a", "w_br_b", "w_br_c", "w_o", "norm2_g",
             "w_gate_up", "w_down")
    return (loss, dx[None]) + tuple(outs[k][i] for i in range(4) for k in order)
```

```python
import functools
import math

import jax
import jax.numpy as jnp
from jax import lax
from jax.experimental import pallas as pl
from jax.experimental.pallas import tpu as pltpu

F32 = jnp.float32
CDT = jnp.bfloat16
XDT = jnp.bfloat16
ADT = jnp.bfloat16

HEAD = 128
NORM_EPS = 1e-6
ROPE_THETA = 10000.0
A_Q_HEADS, A_KV_HEADS, A_GROUP, A_RADIUS = 8, 2, 4, 128
B_DILS = (1, 4, 16)
B_RADIUS = 64
B_HG = 4
C_HEADS, GRID_W, C_WIN_ROWS, C_WIN_COLS = 8, 64, 8, 16
QKV_W = 9216
COL = dict(qa=0, ka=1024, va=1280, qb=1536, kb=3072, vb=4608, qc=6144, kc=7168, vc=8192)
NEG = -1e30
SCALE = HEAD ** -0.5
N_CHIPS = 4

ADAM_LR, ADAM_B1, ADAM_B2, ADAM_EPS, ADAM_WD, ADAM_STEP = 0.001, 0.9, 0.999, 1e-08, 0.01, 10

VMEM_LIMIT = 56 * 1024 * 1024
MESH = pl.DeviceIdType.MESH


def _pallas(body, **kw):
    return pl.pallas_call(body, **kw)


def _params(sem=None, **kw):
    if sem is not None:
        kw["dimension_semantics"] = sem
    return pltpu.CompilerParams(vmem_limit_bytes=VMEM_LIMIT, **kw)


def _tile(dim, pref, mult=128):
    best = None
    for t in range(mult, min(dim, pref) + 1, mult):
        if dim % t == 0:
            best = t
    return dim if best is None else best


def _sds(shape, dtype):
    return jax.ShapeDtypeStruct(tuple(shape), dtype)


_DIMS = {"nn": (((1,), (0,)), ((), ())), "nt": (((1,), (1,)), ((), ())), "tn": (((0,), (0,)), ((), ()))}


def _matmul(a, b, *, mode, grid, a_spec, b_spec, o_spec, out_shape, acc_shape, name, res=None, res_spec=None,
            after=None):
    nk = grid[2]
    has_res = res is not None
    n_in = 2 + int(has_res) + int(after is not None)

    def body(*refs):
        a_ref, b_ref = refs[:2]
        r_ref = refs[2] if has_res else None
        o_ref, rest = refs[n_in], refs[n_in + 1:]
        p = lax.dot_general(a_ref[...].astype(CDT), b_ref[...].astype(CDT), _DIMS[mode],
                            preferred_element_type=F32)

        def finish(acc):
            if has_res:
                acc = acc + r_ref[...].astype(F32)
            o_ref[...] = acc.astype(o_ref.dtype)

        if nk == 1:
            finish(p)
        else:
            acc_ref = rest[0]
            k = pl.program_id(2)

            @pl.when(k == 0)
            def _():
                acc_ref[...] = p

            @pl.when(k > 0)
            def _():
                acc_ref[...] += p

            @pl.when(k == nk - 1)
            def _():
                finish(acc_ref[...])

    in_specs = [a_spec, b_spec] + ([res_spec] if has_res else [])
    args = (a, b) + ((res,) if has_res else ())
    if after is not None:
        in_specs.append(pl.BlockSpec(after.shape, lambda i, j, kk: (0, 0)))
        args += (after,)
    scratch = [] if nk == 1 else [pltpu.VMEM(acc_shape, F32)]
    return _pallas(body, name=name, grid=grid, in_specs=in_specs, out_specs=o_spec, out_shape=out_shape,
                   scratch_shapes=scratch, compiler_params=_params(("parallel", "parallel", "arbitrary")))(*args)


def mm_x_wcol(a, wg, out_dtype, name, tm_pref=1024, tn_pref=1024, stacked_out=1):
    m, k = a.shape
    ns = wg.shape[2]
    tm, tn = _tile(m, tm_pref, 8), _tile(ns, tn_pref)
    nj = ns // tn
    grid = (m // tm, N_CHIPS * nj, 1)
    a_spec = pl.BlockSpec((tm, k), lambda i, j, kk: (i, 0))
    b_spec = pl.BlockSpec((None, k, tn), lambda i, j, kk: (j // nj, 0, j % nj))
    if stacked_out == 1:
        o_spec = pl.BlockSpec((tm, tn), lambda i, j, kk: (i, j))
        out_shape = _sds((m, N_CHIPS * ns), out_dtype)
    else:
        per = N_CHIPS * nj // stacked_out
        o_spec = pl.BlockSpec((None, tm, tn), lambda i, j, kk: (j // per, i, j % per))
        out_shape = _sds((stacked_out, m, N_CHIPS * ns // stacked_out), out_dtype)
    return _matmul(a, wg, mode="nn", grid=grid, a_spec=a_spec, b_spec=b_spec, o_spec=o_spec,
                   out_shape=out_shape, acc_shape=(tm, tn), name=name)


def mm_x_wcolT(d, wg, name, res=None, tm_pref=1024, tn_pref=512, tk_pref=4096, stacked_in=1, lead=None, after=None):
    kdim, ns = wg.shape[1], wg.shape[2]
    m = d.shape[-2]
    tm, tn, tk = _tile(m, tm_pref, 8), _tile(kdim, tn_pref), _tile(ns, tk_pref)
    nkk = ns // tk
    grid = (m // tm, kdim // tn, N_CHIPS * nkk)
    if lead is not None:
        a_spec = pl.BlockSpec((None, tm, tk), lambda i, j, kk: (lead, i, kk))
    elif stacked_in == 1:
        a_spec = pl.BlockSpec((tm, tk), lambda i, j, kk: (i, kk))
    else:
        per = N_CHIPS * nkk // stacked_in
        a_spec = pl.BlockSpec((None, tm, tk), lambda i, j, kk: (kk // per, i, kk % per))
    b_spec = pl.BlockSpec((None, tn, tk), lambda i, j, kk: (kk // nkk, j, kk % nkk))
    o_spec = pl.BlockSpec((tm, tn), lambda i, j, kk: (i, j))
    return _matmul(d, wg, mode="nt", grid=grid, a_spec=a_spec, b_spec=b_spec, o_spec=o_spec,
                   out_shape=_sds((m, kdim), F32), acc_shape=(tm, tn), name=name,
                   res=res, res_spec=None if res is None else o_spec, after=after)


def mm_aT_d_wcol(a, d, name, tm_pref=512, tk_pref=4096, tn_pref=1024, stacked_in=1, lead=None):
    m, kdim = a.shape
    ntot = d.shape[-1] * stacked_in
    ns = ntot // N_CHIPS
    tm, tkm, tn = _tile(kdim, tm_pref), _tile(m, tk_pref, 8), _tile(ns, tn_pref)
    nj = ns // tn
    grid = (kdim // tm, N_CHIPS * nj, m // tkm)
    a_spec = pl.BlockSpec((tkm, tm), lambda i, j, kk: (kk, i))
    if lead is not None:
        b_spec = pl.BlockSpec((None, tkm, tn), lambda i, j, kk: (lead, kk, j))
    elif stacked_in == 1:
        b_spec = pl.BlockSpec((tkm, tn), lambda i, j, kk: (kk, j))
    else:
        per = N_CHIPS * nj // stacked_in
        b_spec = pl.BlockSpec((None, tkm, tn), lambda i, j, kk: (j // per, kk, j % per))
    o_spec = pl.BlockSpec((None, tm, tn), lambda i, j, kk: (j // nj, i, j % nj))
    return _matmul(a, d, mode="tn", grid=grid, a_spec=a_spec, b_spec=b_spec, o_spec=o_spec,
                   out_shape=_sds((N_CHIPS, kdim, ns), XDT), acc_shape=(tm, tn), name=name)


def mm_x_w(a, w, name, res=None, out_dtype=F32, tm_pref=1024, tn_pref=1024, tk_pref=2048):
    m, k = a.shape
    n = w.shape[1]
    tm, tn, tk = _tile(m, tm_pref, 8), _tile(n, tn_pref), _tile(k, tk_pref)
    grid = (m // tm, n // tn, k // tk)
    o_spec = pl.BlockSpec((tm, tn), lambda i, j, kk: (i, j))
    return _matmul(a, w, mode="nn", grid=grid,
                   a_spec=pl.BlockSpec((tm, tk), lambda i, j, kk: (i, kk)),
                   b_spec=pl.BlockSpec((tk, tn), lambda i, j, kk: (kk, j)),
                   o_spec=o_spec, out_shape=_sds((m, n), out_dtype), acc_shape=(tm, tn), name=name,
                   res=res, res_spec=None if res is None else o_spec)


def mm_x_wT(d, w, name, out_dtype=F32, tm_pref=1024, tn_pref=1024, after=None):
    m, n = d.shape
    k = w.shape[0]
    tm, tn = _tile(m, tm_pref, 8), _tile(k, tn_pref)
    grid = (m // tm, k // tn, 1)
    return _matmul(d, w, mode="nt", grid=grid,
                   a_spec=pl.BlockSpec((tm, n), lambda i, j, kk: (i, 0)),
                   b_spec=pl.BlockSpec((tn, n), lambda i, j, kk: (j, 0)),
                   o_spec=pl.BlockSpec((tm, tn), lambda i, j, kk: (i, j)),
                   out_shape=_sds((m, k), out_dtype), acc_shape=(tm, tn), name=name, after=after)


def mm_aT_d(a, d, name, tm_pref=512, tn_pref=512, tk_pref=4096):
    m, k = a.shape
    n = d.shape[1]
    tm, tn, tk = _tile(k, tm_pref), _tile(n, tn_pref), _tile(m, tk_pref, 8)
    grid = (k // tm, n // tn, m // tk)
    return _matmul(a, d, mode="tn", grid=grid,
                   a_spec=pl.BlockSpec((tk, tm), lambda i, j, kk: (kk, i)),
                   b_spec=pl.BlockSpec((tk, tn), lambda i, j, kk: (kk, j)),
                   o_spec=pl.BlockSpec((tm, tn), lambda i, j, kk: (i, j)),
                   out_shape=_sds((k, n), XDT), acc_shape=(tm, tn), name=name)


def rmsnorm_fwd(x, g, name):
    n, d = x.shape
    tm = _tile(n, 512, 8)

    def body(x_ref, g_ref, h_ref):
        xv = x_ref[...]
        r = lax.rsqrt(jnp.mean(xv * xv, axis=-1, keepdims=True) + NORM_EPS)
        h_ref[...] = (xv * r * g_ref[...]).astype(h_ref.dtype)

    return _pallas(body, name=name, grid=(n // tm,),
                   in_specs=[pl.BlockSpec((tm, d), lambda i: (i, 0)), pl.BlockSpec((1, d), lambda i: (0, 0))],
                   out_specs=pl.BlockSpec((tm, d), lambda i: (i, 0)), out_shape=_sds((n, d), CDT),
                   compiler_params=_params(("parallel",)))(x, g.reshape(1, d))


def rmsnorm_bwd(x, g, dh, dres, name):
    n, d = x.shape
    tm = _tile(n, 256, 8)

    def body(x_ref, g_ref, dh_ref, dres_ref, dx_ref, dg_ref):
        xv = x_ref[...]
        r = lax.rsqrt(jnp.mean(xv * xv, axis=-1, keepdims=True) + NORM_EPS)
        dhv = dh_ref[...]
        u = dhv * g_ref[...]
        c = jnp.mean(xv * u, axis=-1, keepdims=True)
        dx_ref[...] = dres_ref[...] + r * u - xv * (r * r * r * c)
        part = jnp.broadcast_to(jnp.sum(dhv * xv * r, axis=0, keepdims=True), (8, d))

        @pl.when(pl.program_id(0) == 0)
        def _():
            dg_ref[...] = part

        @pl.when(pl.program_id(0) > 0)
        def _():
            dg_ref[...] += part

    row = pl.BlockSpec((tm, d), lambda i: (i, 0))
    dx, dg = _pallas(body, name=name, grid=(n // tm,),
                     in_specs=[row, pl.BlockSpec((1, d), lambda i: (0, 0)), row, row],
                     out_specs=[row, pl.BlockSpec((8, d), lambda i: (0, 0))],
                     out_shape=[_sds((n, d), F32), _sds((8, d), F32)],
                     compiler_params=_params(("arbitrary",)))(x, g.reshape(1, d), dh, dres)
    return dx, dg


def _norm_rope(xh, g, cos2, sin2):
    r = lax.rsqrt(jnp.mean(xh * xh, axis=-1, keepdims=True) + NORM_EPS)
    y = xh * r * g
    if cos2 is not None:
        y = y * cos2 + pltpu.roll(y, HEAD // 2, 1) * sin2
    return y


def _norm_rope_bwd(xh, g, cos2, sin2, dout):
    if cos2 is not None:
        dy = dout * cos2 + pltpu.roll(dout * sin2, HEAD // 2, 1)
    else:
        dy = dout
    r = lax.rsqrt(jnp.mean(xh * xh, axis=-1, keepdims=True) + NORM_EPS)
    u = dy * g
    c = jnp.mean(xh * u, axis=-1, keepdims=True)
    return r * u - xh * (r * r * r * c), dy * xh * r


_QK_GROUPS = (("qa", COL["qa"], 8, 0, True), ("ka", COL["ka"], 2, 1, True),
              ("qb", COL["qb"], 12, 2, True), ("kb", COL["kb"], 12, 3, True),
              ("qc", COL["qc"], 8, 4, False), ("kc", COL["kc"], 8, 5, False))
_V_GROUPS = (("va", COL["va"], 2), ("vb", COL["vb"], 12), ("vc", COL["vc"], 8))
_PREP_OUT = (("qa", 8), ("ka", 2), ("va", 2)) + tuple((f"{t}b{g}", 4) for t in "qkv" for g in range(3)) + (
    ("qc", 8), ("kc", 8), ("vc", 8))


def _prep_src(name):
    if name[1] == "b":
        base = COL[name[0] + "b"] + int(name[2]) * B_HG * HEAD
        gain = {"q": 2, "k": 3, "v": None}[name[0]]
        return base, gain, name[0] != "v"
    base = COL[name]
    gain = {"qa": 0, "ka": 1, "va": None, "qc": 4, "kc": 5, "vc": None}[name]
    return base, gain, name in ("qa", "ka")


def _prep_dil(name):
    return B_DILS[int(name[2])] if name[1] == "b" else 1


def _to_classes(val, scr, dil):
    scr[...] = val
    return [scr[pl.ds(r, val.shape[0] // dil, stride=dil), :] for r in range(dil)]


def _from_classes(parts, scr):
    for r, part in enumerate(parts):
        scr[pl.ds(r, part.shape[0], stride=len(parts)), :] = part
    return scr[...]


def _class_spec(tm, dil, width):
    if dil == 1:
        return pl.BlockSpec((tm, width), lambda i: (i, 0))
    return pl.BlockSpec((dil, tm // dil, width), lambda i: (0, i, 0))


def _class_shape(n, dil, width):
    return (n, width) if dil == 1 else (dil, n // dil, width)


def qk_prep(proj, gains, cos2, sin2, name):
    n = proj.shape[0]
    tm = _tile(n, 256, 8)

    def body(p_ref, g_ref, c_ref, s_ref, *refs):
        outs, scr = refs[:-1], refs[-1]
        cos2v, sin2v = c_ref[...], s_ref[...]
        for (nm, heads), o_ref in zip(_PREP_OUT, outs):
            base, gain, rope = _prep_src(nm)
            dil = _prep_dil(nm)
            for h in range(heads):
                hs = slice(h * HEAD, (h + 1) * HEAD)
                xh = p_ref[:, base + h * HEAD: base + (h + 1) * HEAD].astype(F32)
                if gain is None:
                    y = xh
                else:
                    y = _norm_rope(xh, g_ref[gain:gain + 1, :], cos2v if rope else None, sin2v if rope else None)
                if dil == 1:
                    o_ref[:, hs] = y.astype(o_ref.dtype)
                else:
                    for r, part in enumerate(_to_classes(y, scr.at[h % 4], dil)):
                        o_ref[r, :, hs] = part.astype(o_ref.dtype)

    tab = pl.BlockSpec((tm, HEAD), lambda i: (i, 0))
    outs = _pallas(body, name=name, grid=(n // tm,),
                   in_specs=[pl.BlockSpec((tm, QKV_W), lambda i: (i, 0)), pl.BlockSpec((8, HEAD), lambda i: (0, 0)), tab, tab],
                   out_specs=[_class_spec(tm, _prep_dil(nm), h * HEAD) for nm, h in _PREP_OUT],
                   out_shape=[_sds(_class_shape(n, _prep_dil(nm), h * HEAD), CDT) for nm, h in _PREP_OUT],
                   scratch_shapes=[pltpu.VMEM((4, tm, HEAD), F32)],
                   compiler_params=_params(("parallel",)))(proj, gains, cos2, sin2)
    return dict(zip([nm for nm, _ in _PREP_OUT], outs))


def qk_prep_bwd(proj, gains, cos2, sin2, grads, dproj, name):
    n = proj.shape[0]
    tm = _tile(n, 128, 8)
    names = [nm for nm, _ in _PREP_OUT]

    def body(p_ref, g_ref, c_ref, s_ref, *refs):
        g_refs, dp_ref, dg_ref, scr = refs[:len(names)], refs[len(names) + 1], refs[len(names) + 2], refs[-1]
        cos2v, sin2v = c_ref[...], s_ref[...]
        dg = [jnp.zeros((tm, HEAD), F32) for _ in range(6)]
        for (nm, heads), gr in zip(_PREP_OUT, g_refs):
            base, gain, rope = _prep_src(nm)
            dil = _prep_dil(nm)
            for h in range(heads):
                sl = slice(base + h * HEAD, base + (h + 1) * HEAD)
                hs = slice(h * HEAD, (h + 1) * HEAD)
                dout = gr[:, hs] if dil == 1 else _from_classes([gr[r, :, hs] for r in range(dil)], scr.at[h % 4])
                if gain is None:
                    dx = dout
                else:
                    dx, dgr = _norm_rope_bwd(p_ref[:, sl].astype(F32), g_ref[gain:gain + 1, :], cos2v if rope else None,
                                             sin2v if rope else None, dout)
                    dg[gain] = dg[gain] + dgr
                dp_ref[:, sl] = dx.astype(dp_ref.dtype)
        part = jnp.concatenate([jnp.sum(t, axis=0, keepdims=True) for t in dg] + [jnp.zeros((2, HEAD), F32)], axis=0)

        @pl.when(pl.program_id(0) == 0)
        def _():
            dg_ref[...] = part

        @pl.when(pl.program_id(0) > 0)
        def _():
            dg_ref[...] += part

    tab = pl.BlockSpec((tm, HEAD), lambda i: (i, 0))
    dp, dg = _pallas(body, name=name, grid=(n // tm,),
                     in_specs=[pl.BlockSpec((tm, QKV_W), lambda i: (i, 0)), pl.BlockSpec((8, HEAD), lambda i: (0, 0)), tab, tab]
                     + [_class_spec(tm, _prep_dil(nm), h * HEAD) for nm, h in _PREP_OUT] + [ANY],
                     out_specs=[pl.BlockSpec((tm, QKV_W), lambda i: (i, 0)), pl.BlockSpec((8, HEAD), lambda i: (0, 0))],
                     out_shape=[_sds(dproj.shape, dproj.dtype), _sds((8, HEAD), F32)],
                     input_output_aliases={4 + len(names): 0},
                     scratch_shapes=[pltpu.VMEM((4, tm, HEAD), F32)],
                     compiler_params=_params(("arbitrary",)))(proj, gains, cos2, sin2, *[grads[k] for k in names], dproj)
    return dp, dg


def _band_geometry(m, bq_pref, radius):
    bq = min(bq_pref, m)
    return bq, min(bq + 2 * radius, m)


def _band_window(i, bq, radius, m, w):
    start = pl.multiple_of(jnp.clip(i * bq - radius, 0, m - w), 64)
    qpos = i * bq + lax.broadcasted_iota(jnp.int32, (bq, w), 0)
    kpos = start + lax.broadcasted_iota(jnp.int32, (bq, w), 1)
    return start, jnp.abs(kpos - qpos) <= radius


def _band_specs(m, bq, G, nh, seqs):
    lg, nb = seqs // nh, m // bq
    qspec = pl.BlockSpec((bq, nh * G * HEAD), lambda s, i: ((s // lg) * nb + i, s % lg))
    kspec = pl.BlockSpec((m, nh * HEAD), lambda s, i: (s // lg, s % lg))
    return qspec, kspec, lg


def band_attn_fwd(q, k, v, sink, *, seqs, G, nh, radius, name, classes=1, bq_pref=128):
    m = q.shape[0] // classes
    bq, w = _band_geometry(m, bq_pref, radius)
    has_sink = sink is not None

    def body(*refs):
        if has_sink:
            sink_ref, q_ref, k_ref, v_ref, o_ref, lse_ref = refs
        else:
            q_ref, k_ref, v_ref, o_ref, lse_ref = refs
        s_id, i = pl.program_id(0), pl.program_id(1)
        start, valid = _band_window(i, bq, radius, m, w)
        units = [(h, g) for h in range(nh) for g in range(G)]
        sls = [slice((h * G + g) * HEAD, (h * G + g + 1) * HEAD) for h, g in units]
        k_ts = [k_ref[pl.ds(start, w), h * HEAD:(h + 1) * HEAD] for h in range(nh)]
        v_ts = [v_ref[pl.ds(start, w), h * HEAD:(h + 1) * HEAD] for h in range(nh)]
        q_ts = [q_ref[:, sl] for sl in sls]
        sks = [sink_ref[0, (s_id * nh + h) * G + g] for h, g in units] if has_sink else None
        ss = [jnp.where(valid, lax.dot_general(q_t, k_ts[h], _DIMS["nt"], preferred_element_type=F32) * SCALE, NEG)
              for q_t, (h, g) in zip(q_ts, units)]
        mxs = [jnp.max(s, axis=-1, keepdims=True) for s in ss]
        if has_sink:
            mxs = [jnp.maximum(mx, sk) for mx, sk in zip(mxs, sks)]
        ps = [jnp.exp(s - mx) for s, mx in zip(ss, mxs)]
        dens = [jnp.sum(p, axis=-1, keepdims=True) for p in ps]
        if has_sink:
            dens = [den + jnp.exp(sk - mx) for den, sk, mx in zip(dens, sks, mxs)]
        outs = [jnp.dot((p / den).astype(CDT), v_ts[h], preferred_element_type=F32)
                for p, den, (h, g) in zip(ps, dens, units)]
        for sl, o, mx, den in zip(sls, outs, mxs, dens):
            o_ref[:, sl] = o
            lse_ref[:, sl] = jnp.broadcast_to(mx + jnp.log(den), (bq, HEAD))

    qspec, kspec, lg = _band_specs(m, bq, G, nh, seqs)
    in_specs = ([pl.BlockSpec(memory_space=pltpu.SMEM)] if has_sink else []) + [qspec, kspec, kspec]
    args = ((sink,) if has_sink else ()) + (q, k, v)
    return _pallas(body, name=name, grid=(classes * lg, m // bq), in_specs=in_specs, out_specs=[qspec, qspec],
                   out_shape=[_sds(q.shape, F32), _sds(q.shape, F32)],
                   compiler_params=_params(("parallel", "arbitrary")))(*args)


def band_attn_bwd(q, k, v, sink, o, lse, do, dlse, *, seqs, G, nh, radius, name, classes=1, bq_pref=128):
    m = q.shape[0] // classes
    bq, w = _band_geometry(m, bq_pref, radius)
    has_sink, has_dlse = sink is not None, dlse is not None
    assert nh * G <= 8

    def body(*refs):
        refs = list(refs)
        sink_ref = refs.pop(0) if has_sink else None
        q_ref, k_ref, v_ref, o_ref, lse_ref, do_ref = refs[:6]
        refs = refs[6:]
        dlse_ref = refs.pop(0) if has_dlse else None
        dq_ref, dk_ref, dv_ref = refs[:3]
        dsink_ref = refs[3] if has_sink else None
        s_id, i = pl.program_id(0), pl.program_id(1)

        @pl.when(i == 0)
        def _():
            dk_ref[...] = jnp.zeros_like(dk_ref)
            dv_ref[...] = jnp.zeros_like(dv_ref)
            if has_sink:
                dsink_ref[...] = jnp.zeros_like(dsink_ref)

        start, valid = _band_window(i, bq, radius, m, w)
        units = [(h, g) for h in range(nh) for g in range(G)]
        sls = [slice((h * G + g) * HEAD, (h * G + g + 1) * HEAD) for h, g in units]
        kss = [slice(h * HEAD, (h + 1) * HEAD) for h in range(nh)]
        k_ts = [k_ref[pl.ds(start, w), ks] for ks in kss]
        v_ts = [v_ref[pl.ds(start, w), ks] for ks in kss]
        q_ts = [q_ref[:, sl] for sl in sls]
        lse_ts = [lse_ref[:, sl][:, :1] for sl in sls]
        do_ts = [do_ref[:, sl] for sl in sls]
        deltas = [jnp.sum(do_t * o_ref[:, sl], axis=-1, keepdims=True) for do_t, sl in zip(do_ts, sls)]
        dlse_ts = [dlse_ref[:, sl][:, :1] for sl in sls] if has_dlse else None
        dk_old = [dk_ref[pl.ds(start, w), ks] for ks in kss]
        dv_old = [dv_ref[pl.ds(start, w), ks] for ks in kss]
        dsink_old = dsink_ref[...] if has_sink else None

        ps = [jnp.exp(jnp.where(valid, lax.dot_general(q_t, k_ts[h], _DIMS["nt"], preferred_element_type=F32) * SCALE, NEG)
                      - lse_t) for q_t, lse_t, (h, g) in zip(q_ts, lse_ts, units)]
        do_cs = [do_t.astype(CDT) for do_t in do_ts]
        dps = [lax.dot_general(do_c, v_ts[h], _DIMS["nt"], preferred_element_type=F32) for do_c, (h, g) in zip(do_cs, units)]
        ts = [dp - delta for dp, delta in zip(dps, deltas)]
        if has_dlse:
            ts = [t + dl for t, dl in zip(ts, dlse_ts)]
        dss = [((p * t) * SCALE).astype(CDT) for p, t in zip(ps, ts)]
        dqs = [jnp.dot(ds, k_ts[h], preferred_element_type=F32) for ds, (h, g) in zip(dss, units)]
        dvs = [lax.dot_general(p.astype(CDT), do_c, _DIMS["tn"], preferred_element_type=F32) for p, do_c in zip(ps, do_cs)]
        dks = [lax.dot_general(ds, q_t, _DIMS["tn"], preferred_element_type=F32) for ds, q_t in zip(dss, q_ts)]
        dk_new = [dk_old[h] + sum(dks[h * G + g] for g in range(G)) for h in range(nh)]
        dv_new = [dv_old[h] + sum(dvs[h * G + g] for g in range(G)) for h in range(nh)]
        if has_sink:
            rows = []
            for (h, g), lse_t, delta in zip(units, lse_ts, deltas):
                sk = sink_ref[0, (s_id * nh + h) * G + g]
                rows.append(jnp.broadcast_to(-jnp.sum(jnp.exp(sk - lse_t) * delta, axis=0, keepdims=True), (1, HEAD)))
            rows += [jnp.zeros((1, HEAD), F32)] * (8 - len(rows))
            dsink_new = dsink_old + jnp.concatenate(rows, axis=0)

        for sl, dq in zip(sls, dqs):
            dq_ref[:, sl] = dq
        for h, ks in enumerate(kss):
            dk_ref[pl.ds(start, w), ks] = dk_new[h]
            dv_ref[pl.ds(start, w), ks] = dv_new[h]
        if has_sink:
            dsink_ref[...] = dsink_new

    qspec, kspec, lg = _band_specs(m, bq, G, nh, seqs)
    kin = pl.BlockSpec(kspec.block_shape, kspec.index_map, pipeline_mode=pl.Buffered(1))
    in_specs = ([pl.BlockSpec(memory_space=pltpu.SMEM)] if has_sink else []) + [qspec, kin, kin, qspec, qspec, qspec]
    in_specs += [qspec] if has_dlse else []
    args = ((sink,) if has_sink else ()) + (q, k, v, o, lse, do) + ((dlse,) if has_dlse else ())
    out_specs = [qspec, kspec, kspec]
    out_shape = [_sds(q.shape, F32), _sds(k.shape, F32), _sds(k.shape, F32)]
    if has_sink:
        out_specs.append(pl.BlockSpec((None, 8, HEAD), lambda s, i: (s, 0, 0)))
        out_shape.append(_sds((seqs // nh, 8, HEAD), F32))
    return _pallas(body, name=name, grid=(classes * lg, m // bq), in_specs=in_specs, out_specs=out_specs,
                   out_shape=out_shape, compiler_params=_params(("parallel", "arbitrary")))(*args)


def _group_weights(lses):
    mx = jnp.maximum(jnp.maximum(lses[0], lses[1]), lses[2])
    e = [jnp.exp(l - mx) for l in lses]
    tot = e[0] + e[1] + e[2]
    return [t / tot for t in e]


def _read_group(ref, dil, h, scr):
    hs = slice(h * HEAD, (h + 1) * HEAD)
    return ref[:, hs] if dil == 1 else _from_classes([ref[r, :, hs] for r in range(dil)], scr)


def b_combine_fwd(os_, lses, name):
    n, wd = os_[0].shape
    tm = _tile(n, 512, 8)

    def body(o0, o1, o2, l0, l1, l2, out_ref, scr):
        for h in range(wd // HEAD):
            ls = [_read_group(ref, dil, h, scr.at[g]) for g, (ref, dil) in enumerate(zip((l0, l1, l2), B_DILS))]
            ovs = [_read_group(ref, dil, h, scr.at[3 + g]) for g, (ref, dil) in enumerate(zip((o0, o1, o2), B_DILS))]
            wts = _group_weights(ls)
            out_ref[:, h * HEAD:(h + 1) * HEAD] = wts[0] * ovs[0] + wts[1] * ovs[1] + wts[2] * ovs[2]

    specs = [_class_spec(tm, dil, wd) for dil in B_DILS]
    return _pallas(body, name=name, grid=(n // tm,), in_specs=specs * 2, out_specs=pl.BlockSpec((tm, wd), lambda i: (i, 0)),
                   out_shape=_sds((n, wd), F32), scratch_shapes=[pltpu.VMEM((6, tm, HEAD), F32)],
                   compiler_params=_params(("parallel",)))(*os_, *lses)


def b_combine_bwd(dout, os_, lses, name):
    n, wd = dout.shape
    tm = _tile(n, 256, 8)

    def body(d_ref, o0, o1, o2, l0, l1, l2, do0, do1, do2, dl0, dl1, dl2, scr):
        for h in range(wd // HEAD):
            hs = slice(h * HEAD, (h + 1) * HEAD)
            dv = d_ref[:, hs]
            ls = [_read_group(ref, dil, h, scr.at[g]) for g, (ref, dil) in enumerate(zip((l0, l1, l2), B_DILS))]
            ovs = [_read_group(ref, dil, h, scr.at[3 + g]) for g, (ref, dil) in enumerate(zip((o0, o1, o2), B_DILS))]
            wts = _group_weights(ls)
            dws = [jnp.broadcast_to(jnp.sum(dv * ov, axis=-1, keepdims=True), (tm, HEAD)) for ov in ovs]
            mean = wts[0] * dws[0] + wts[1] * dws[1] + wts[2] * dws[2]
            for g, (wt, dw, do_ref, dl_ref, dil) in enumerate(zip(wts, dws, (do0, do1, do2), (dl0, dl1, dl2), B_DILS)):
                if dil == 1:
                    do_ref[:, hs] = wt * dv
                    dl_ref[:, hs] = wt * (dw - mean)
                else:
                    for r, part in enumerate(_to_classes(wt * dv, scr.at[g], dil)):
                        do_ref[r, :, hs] = part
                    for r, part in enumerate(_to_classes(wt * (dw - mean), scr.at[3 + g], dil)):
                        dl_ref[r, :, hs] = part

    specs = [_class_spec(tm, dil, wd) for dil in B_DILS]
    outs = _pallas(body, name=name, grid=(n // tm,), in_specs=[pl.BlockSpec((tm, wd), lambda i: (i, 0))] + specs * 2,
                   out_specs=specs * 2, out_shape=[_sds(_class_shape(n, dil, wd), F32) for dil in B_DILS] * 2,
                   scratch_shapes=[pltpu.VMEM((6, tm, HEAD), F32)],
                   compiler_params=_params(("parallel",)))(dout, *os_, *lses)
    return outs[:3], outs[3:]


def _c_rows(n):
    rows = n // GRID_W
    return rows, min(C_WIN_ROWS, rows)


def _c_row_start(r, rows, wr):
    return jnp.clip(r - wr // 2, 0, rows - wr)


def _c_bias_index(r, rows, wr):
    return _c_row_start(r, rows, wr) - r + (C_WIN_ROWS - 1)


def _col_shift_select(tile, cq, inverse):
    lanes = tile.shape[1]
    for b in range(6):
        amt = (lanes - (1 << b)) if inverse else (1 << b)
        tile = jnp.where(((cq >> b) & 1) == 1, pltpu.roll(tile, amt, 1), tile)
    return tile


def rpb_expand(rwin, name):
    lanes = rwin.shape[-1]

    def body(r_ref, b_ref):
        cq = lax.broadcasted_iota(jnp.int32, (GRID_W, lanes), 0)
        ck = lax.broadcasted_iota(jnp.int32, (GRID_W, lanes), 1) % GRID_W
        cs = jnp.clip(cq - C_WIN_COLS // 2, 0, GRID_W - C_WIN_COLS)
        ok = (ck >= cs) & (ck < cs + C_WIN_COLS)
        for i0 in range(C_WIN_ROWS):
            tile = jnp.broadcast_to(r_ref[i0], (GRID_W, lanes))
            tile = pltpu.roll(tile, lanes - (C_WIN_COLS - 1), 1)
            tile = _col_shift_select(tile, cq, False)
            b_ref[i0] = jnp.where(ok, tile, NEG)

    return _pallas(body, name=name, grid=(C_HEADS,),
                   in_specs=[pl.BlockSpec((None, C_WIN_ROWS, 1, lanes), lambda h: (h, 0, 0, 0))],
                   out_specs=pl.BlockSpec((None, C_WIN_ROWS, GRID_W, lanes), lambda h: (h, 0, 0, 0)),
                   out_shape=_sds((C_HEADS, C_WIN_ROWS, GRID_W, lanes), F32),
                   compiler_params=_params(("parallel",)))(rwin)


def rpb_reduce(dbias, name):
    lanes = dbias.shape[-1]
    wr = lanes // GRID_W

    def body(d_ref, o_ref):
        cq = lax.broadcasted_iota(jnp.int32, (GRID_W, lanes), 0)
        o_ref[...] = jnp.zeros_like(o_ref)
        for i0 in range(C_WIN_ROWS):
            tile = _col_shift_select(d_ref[i0], cq, True)
            tile = pltpu.roll(tile, C_WIN_COLS - 1, 1)
            vec = jnp.sum(tile, axis=0, keepdims=True)
            for w in range(wr):
                o_ref[i0 + w:i0 + w + 1, :] += vec[:, w * GRID_W:(w + 1) * GRID_W]

    return _pallas(body, name=name, grid=(C_HEADS,),
                   in_specs=[pl.BlockSpec((None, C_WIN_ROWS, GRID_W, lanes), lambda h: (h, 0, 0, 0))],
                   out_specs=pl.BlockSpec((None, 16, GRID_W), lambda h: (h, 0, 0)),
                   out_shape=_sds((C_HEADS, 16, GRID_W), F32),
                   compiler_params=_params(("parallel",)))(dbias)


def _store_or_add(ref, val, first):
    @pl.when(first)
    def _():
        ref[...] = val

    @pl.when(jnp.logical_not(first))
    def _():
        ref[...] += val


def c_attn_fwd(q, k, v, bias, name, nh=C_HEADS):
    n = q.shape[0]
    rows, wr = _c_rows(n)
    wk = wr * GRID_W

    def body(q_ref, k_ref, v_ref, b_ref, o_ref, lse_ref):
        r = pl.program_id(1)
        start = pl.multiple_of(_c_row_start(r, rows, wr) * GRID_W, GRID_W)
        sls = [slice(h * HEAD, (h + 1) * HEAD) for h in range(nh)]
        ss = [lax.dot_general(q_ref[:, sl], k_ref[pl.ds(start, wk), sl], _DIMS["nt"], preferred_element_type=F32)
              * SCALE + b_ref[h] for h, sl in enumerate(sls)]
        mxs = [jnp.max(s, axis=-1, keepdims=True) for s in ss]
        ps = [jnp.exp(s - mx) for s, mx in zip(ss, mxs)]
        dens = [jnp.sum(p, axis=-1, keepdims=True) for p in ps]
        outs = [jnp.dot((p / den).astype(CDT), v_ref[pl.ds(start, wk), sl], preferred_element_type=F32)
                for p, den, sl in zip(ps, dens, sls)]
        for sl, o, mx, den in zip(sls, outs, mxs, dens):
            o_ref[:, sl] = o
            lse_ref[:, sl] = jnp.broadcast_to(mx + jnp.log(den), (GRID_W, HEAD))

    qspec = pl.BlockSpec((GRID_W, nh * HEAD), lambda h, r: (r, h))
    kspec = pl.BlockSpec((n, nh * HEAD), lambda h, r: (0, h), pipeline_mode=pl.Buffered(1))
    bspec = pl.BlockSpec((nh, None, GRID_W, wk), lambda h, r: (h, _c_bias_index(r, rows, wr), 0, 0))
    return _pallas(body, name=name, grid=(C_HEADS // nh, rows), in_specs=[qspec, kspec, kspec, bspec],
                   out_specs=[qspec, qspec], out_shape=[_sds(q.shape, F32), _sds(q.shape, F32)],
                   compiler_params=_params(("parallel", "arbitrary")))(q, k, v, bias)


def c_attn_bwd(q, k, v, bias, o, lse, do, name, nh=4):
    n = q.shape[0]
    rows, wr = _c_rows(n)
    wk = wr * GRID_W

    def body(q_ref, k_ref, v_ref, b_ref, o_ref, lse_ref, do_ref, dq_ref, dk_ref, dv_ref, db_ref):
        r = pl.program_id(1)
        rs = _c_row_start(r, rows, wr)
        start = pl.multiple_of(rs * GRID_W, GRID_W)

        @pl.when(r == 0)
        def _():
            dk_ref[...] = jnp.zeros_like(dk_ref)
            dv_ref[...] = jnp.zeros_like(dv_ref)

        prev = _c_row_start(jnp.maximum(r - 1, 0), rows, wr) - jnp.maximum(r - 1, 0)
        first = (r == 0) | (prev != rs - r)
        sls = [slice(h * HEAD, (h + 1) * HEAD) for h in range(nh)]
        k_ts = [k_ref[pl.ds(start, wk), sl] for sl in sls]
        v_ts = [v_ref[pl.ds(start, wk), sl] for sl in sls]
        q_ts = [q_ref[:, sl] for sl in sls]
        lse_ts = [lse_ref[:, sl][:, :1] for sl in sls]
        do_ts = [do_ref[:, sl] for sl in sls]
        deltas = [jnp.sum(do_t * o_ref[:, sl], axis=-1, keepdims=True) for do_t, sl in zip(do_ts, sls)]
        biases = [b_ref[h] for h in range(nh)]
        dk_old = [dk_ref[pl.ds(start, wk), sl] for sl in sls]
        dv_old = [dv_ref[pl.ds(start, wk), sl] for sl in sls]

        ps = [jnp.exp(lax.dot_general(q_t, k_t, _DIMS["nt"], preferred_element_type=F32) * SCALE + b - lse_t)
              for q_t, k_t, b, lse_t in zip(q_ts, k_ts, biases, lse_ts)]
        do_cs = [do_t.astype(CDT) for do_t in do_ts]
        dps = [lax.dot_general(do_c, v_t, _DIMS["nt"], preferred_element_type=F32) for do_c, v_t in zip(do_cs, v_ts)]
        dss = [p * (dp - delta) for p, dp, delta in zip(ps, dps, deltas)]
        ds_cs = [(ds * SCALE).astype(CDT) for ds in dss]
        dqs = [jnp.dot(ds_c, k_t, preferred_element_type=F32) for ds_c, k_t in zip(ds_cs, k_ts)]
        dv_new = [old + lax.dot_general(p.astype(CDT), do_c, _DIMS["tn"], preferred_element_type=F32)
                  for old, p, do_c in zip(dv_old, ps, do_cs)]
        dk_new = [old + lax.dot_general(ds_c, q_t, _DIMS["tn"], preferred_element_type=F32)
                  for old, ds_c, q_t in zip(dk_old, ds_cs, q_ts)]

        for h, sl in enumerate(sls):
            dq_ref[:, sl] = dqs[h]
            dk_ref[pl.ds(start, wk), sl] = dk_new[h]
            dv_ref[pl.ds(start, wk), sl] = dv_new[h]
        for h in range(nh):
            _store_or_add(db_ref.at[h], dss[h], first)

    qspec = pl.BlockSpec((GRID_W, nh * HEAD), lambda h, r: (r, h))
    kspec = pl.BlockSpec((n, nh * HEAD), lambda h, r: (0, h))
    kin = pl.BlockSpec((n, nh * HEAD), lambda h, r: (0, h), pipeline_mode=pl.Buffered(1))
    bspec = pl.BlockSpec((nh, None, GRID_W, wk), lambda h, r: (h, _c_bias_index(r, rows, wr), 0, 0))
    return _pallas(body, name=name, grid=(C_HEADS // nh, rows),
                   in_specs=[qspec, kin, kin, bspec, qspec, qspec, qspec],
                   out_specs=[qspec, kspec, kspec, bspec],
                   out_shape=[_sds(q.shape, F32), _sds(k.shape, F32), _sds(k.shape, F32), _sds(bias.shape, F32)],
                   compiler_params=_params(("parallel", "arbitrary")))(q, k, v, bias, o, lse, do)


def _sigmoid(z):
    return 1.0 / (1.0 + jnp.exp(-z))


def _gate_specs(n, d):
    tm, tn = _tile(n, 256, 8), _tile(math.gcd(d, QKV_W), 1024)
    nj = d // tn
    tile = pl.BlockSpec((tm, tn), lambda i, j: (i, j))
    gl = [pl.BlockSpec((tm, tn), functools.partial(lambda i, j, b: (i, (QKV_W + b * d) // tn + j), b=b)) for b in range(3)]
    return tm, tn, nj, tile, gl


def gate_merge(proj, ys, name):
    n, d = ys[0].shape
    tm, tn, nj, tile, gl = _gate_specs(n, d)

    def body(g0, g1, g2, y0, y1, y2, out_ref):
        acc = (_sigmoid(g0[...].astype(F32)) * y0[...] + _sigmoid(g1[...].astype(F32)) * y1[...]
               + _sigmoid(g2[...].astype(F32)) * y2[...])
        out_ref[...] = acc.astype(out_ref.dtype)

    return _pallas(body, name=name, grid=(n // tm, nj), in_specs=gl + [tile] * 3, out_specs=tile,
                   out_shape=_sds((n, d), CDT), compiler_params=_params(("parallel", "parallel")))(proj, proj, proj, *ys)


def gate_bwd(proj, ys, dmerged, name):
    n, d = dmerged.shape
    tm, tn, nj, _, _ = _gate_specs(n, d)

    def body(g_ref, y0, y1, y2, dm_ref, dy_ref, dp_ref):
        b = pl.program_id(2)
        y = jnp.where(b == 0, y0[...], jnp.where(b == 1, y1[...], y2[...]))
        dm = dm_ref[...]
        sg = _sigmoid(g_ref[...].astype(F32))
        dy_ref[...] = (dm * sg).astype(dy_ref.dtype)
        dp_ref[...] = (dm * y * sg * (1.0 - sg)).astype(dp_ref.dtype)

    gl = pl.BlockSpec((tm, tn), lambda i, j, b: (i, QKV_W // tn + b * nj + j))
    tile = pl.BlockSpec((tm, tn), lambda i, j, b: (i, j))
    return _pallas(body, name=name, grid=(n // tm, nj, 3), in_specs=[gl, tile, tile, tile, tile],
                   out_specs=[pl.BlockSpec((None, tm, tn), lambda i, j, b: (b, i, j)), gl],
                   out_shape=[_sds((3, n, d), CDT), _sds(proj.shape, CDT)],
                   compiler_params=_params(("parallel", "parallel", "arbitrary")))(proj, *ys, dmerged)


def gate_up_swiglu(h2, wg, name, after=None):
    n, d = h2.shape
    ns = wg.shape[2]
    ff = 2 * ns
    tm, tn = _tile(n, 512, 8), _tile(ns, 1408)
    nj = ns // tn
    extra = [] if after is None else [after]

    def body(a_ref, bg_ref, bu_ref, *refs):
        gu_ref, act_ref = refs[len(extra):]
        a = a_ref[...].astype(CDT)
        gt = jnp.dot(a, bg_ref[...].astype(CDT), preferred_element_type=F32)
        up = jnp.dot(a, bu_ref[...].astype(CDT), preferred_element_type=F32)
        gu_ref[0] = gt.astype(gu_ref.dtype)
        gu_ref[1] = up.astype(gu_ref.dtype)
        act_ref[...] = (gt * _sigmoid(gt) * up).astype(act_ref.dtype)

    return _pallas(body, name=name, grid=(n // tm, 2 * nj),
                   in_specs=[pl.BlockSpec((tm, d), lambda i, j: (i, 0)),
                             pl.BlockSpec((None, d, tn), lambda i, j: (j // nj, 0, j % nj)),
                             pl.BlockSpec((None, d, tn), lambda i, j: (2 + j // nj, 0, j % nj))]
                   + [pl.BlockSpec(a.shape, lambda i, j: (0, 0)) for a in extra],
                   out_specs=[pl.BlockSpec((2, tm, tn), lambda i, j: (0, i, j)), pl.BlockSpec((tm, tn), lambda i, j: (i, j))],
                   out_shape=[_sds((2, n, ff), ADT), _sds((n, ff), CDT)],
                   compiler_params=_params(("parallel", "parallel")))(h2, wg, wg, *extra)


def d_gate_up(dx, w_down, gu, name):
    n, d = dx.shape
    ff = w_down.shape[0]
    tm, tn = _tile(n, 1024, 8), _tile(ff, 512)

    def body(a_ref, b_ref, gu_ref, d_ref):
        da = lax.dot_general(a_ref[...].astype(CDT), b_ref[...].astype(CDT), _DIMS["nt"], preferred_element_type=F32)
        gt, up = gu_ref[0].astype(F32), gu_ref[1].astype(F32)
        sg = _sigmoid(gt)
        d_ref[0] = (da * up * (sg + gt * sg * (1.0 - sg))).astype(d_ref.dtype)
        d_ref[1] = (da * gt * sg).astype(d_ref.dtype)

    blk = pl.BlockSpec((2, tm, tn), lambda i, j: (0, i, j))
    return _pallas(body, name=name, grid=(n // tm, ff // tn),
                   in_specs=[pl.BlockSpec((tm, d), lambda i, j: (i, 0)), pl.BlockSpec((tn, d), lambda i, j: (j, 0)), blk],
                   out_specs=blk, out_shape=_sds((2, n, ff), CDT),
                   compiler_params=_params(("parallel", "parallel")))(dx, w_down, gu)


def loss_head(y, target, name):
    n, d = y.shape
    tm = _tile(n, 512, 8)
    nsteps = n // tm

    def body(y_ref, t_ref, l_ref, dy_ref, acc_ref):
        i = pl.program_id(0)
        e = y_ref[...] - t_ref[...]
        dy_ref[...] = e * (1.0 / d)
        part = jnp.sum((e * e).reshape(tm // 8, 8, d), axis=0)

        @pl.when(i == 0)
        def _():
            acc_ref[...] = part

        @pl.when(i > 0)
        def _():
            acc_ref[...] += part

        @pl.when(i == nsteps - 1)
        def _():
            tot = jnp.sum(jnp.sum(acc_ref[...], axis=1, keepdims=True), axis=0, keepdims=True) * (0.5 / d)
            l_ref[...] = jnp.broadcast_to(tot, (8, HEAD))

    row = pl.BlockSpec((tm, d), lambda i: (i, 0))
    return _pallas(body, name=name, grid=(nsteps,), in_specs=[row, row],
                   out_specs=[pl.BlockSpec((8, HEAD), lambda i: (0, 0)), row],
                   out_shape=[_sds((8, HEAD), F32), _sds((n, d), F32)],
                   scratch_shapes=[pltpu.VMEM((8, d), F32)],
                   compiler_params=_params(("arbitrary",)))(y, target)


def adamw(w, g, m, v, name):
    r, c = w.shape
    tr = _tile(r, max(8, (1 << 19) // c), 8)
    c1 = 1.0 - ADAM_B1 ** ADAM_STEP
    c2 = 1.0 - ADAM_B2 ** ADAM_STEP

    def body(w_ref, g_ref, m_ref, v_ref, d_ref, mo_ref, vo_ref):
        gv = g_ref[...]
        mn = ADAM_B1 * m_ref[...] + (1.0 - ADAM_B1) * gv
        vn = ADAM_B2 * v_ref[...] + (1.0 - ADAM_B2) * (gv * gv)
        d_ref[...] = -ADAM_LR * ((mn / c1) / (jnp.sqrt(vn / c2) + ADAM_EPS) + ADAM_WD * w_ref[...])
        mo_ref[...] = mn
        vo_ref[...] = vn

    row = pl.BlockSpec((tr, c), lambda i: (i, 0))
    return _pallas(body, name=name, grid=(r // tr,), in_specs=[row] * 4, out_specs=[row] * 3,
                   out_shape=[_sds((r, c), F32)] * 3, compiler_params=_params(("parallel",)))(w, g, m, v)


ANY = pl.BlockSpec(memory_space=pl.ANY)


def _place():
    x, y, c = lax.axis_index("x"), lax.axis_index("y"), lax.axis_index("c")
    return x, y, c, [(1 - x, y), (x, 1 - y), (1 - x, 1 - y)]


def _rcopy(src, dst, send_sems, recv_sems, k, to):
    return pltpu.make_async_remote_copy(src_ref=src, dst_ref=dst, send_sem=send_sems.at[k], recv_sem=recv_sems.at[k],
                                        device_id=to, device_id_type=MESH)


def cast_place(chip_idx, shards, layer, name):
    _, k, ns = shards.shape
    tr = _tile(k, max(16, (1 << 19) // ns), 16)

    def body(k_ref, s_ref, o_ref):
        o_ref[...] = s_ref[...].astype(o_ref.dtype)

    gs = pltpu.PrefetchScalarGridSpec(
        num_scalar_prefetch=1, grid=(k // tr,),
        in_specs=[pl.BlockSpec((None, tr, ns), lambda i, k_ref: (layer, i, 0))],
        out_specs=pl.BlockSpec((None, tr, ns), lambda i, k_ref: (k_ref[0], i, 0)))
    return _pallas(body, name=name, grid_spec=gs, out_shape=_sds((N_CHIPS, k, ns), CDT),
                   compiler_params=_params(("parallel",)))(chip_idx, shards)


HBM = pl.BlockSpec(memory_space=pltpu.HBM)
SEM = pl.BlockSpec(memory_space=pltpu.SEMAPHORE)
EFFECT = pltpu.SideEffectType.DATAFLOW_SIDE_EFFECTING


def _in_hbm(a):
    return pltpu.with_memory_space_constraint(a, pltpu.HBM)


def _gather_copies(refs, send_sems, recv_sems):
    x, y, c, chips = _place()
    me = 2 * x + y
    out = []
    for t, ref in enumerate(refs):
        kh = ref.shape[1] // 2
        for j, (px, py) in enumerate(chips):
            send = _rcopy(ref.at[me, pl.ds(c * kh, kh)], ref.at[me, pl.ds(c * kh, kh)], send_sems, recv_sems,
                          3 * t + j, (px, py, c))
            land = ref.at[2 * px + py, pl.ds(c * kh, kh)]
            out.append((send, _rcopy(land, land, send_sems, recv_sems, 3 * t + j, (px, py, c))))
    return out


def gather_start(bufs, after, name):
    nt = len(bufs)

    def body(*refs):
        ins, send_sems, recv_sems, token = refs[:nt], refs[nt + 1], refs[nt + 2], refs[-1]
        for send, _ in _gather_copies(ins, send_sems, recv_sems):
            send.start()
        token[...] = jnp.zeros_like(token)

    outs = _pallas(body, name=name, in_specs=[HBM] * nt + [ANY],
                   out_specs=(SEM, SEM) + (HBM,) * nt + (pl.BlockSpec(memory_space=pltpu.VMEM),),
                   out_shape=(pltpu.SemaphoreType.DMA((3 * nt,)), pltpu.SemaphoreType.DMA((3 * nt,)))
                   + tuple(pltpu.HBM(b.shape, b.dtype) for b in bufs) + (_sds((8, HEAD), F32),),
                   input_output_aliases={t: 2 + t for t in range(nt)},
                   compiler_params=pltpu.CompilerParams(has_side_effects=EFFECT))(*[_in_hbm(b) for b in bufs], after)
    return outs[0], outs[1], list(outs[2:2 + nt]), outs[-1]


def gather_wait(bufs, send_sems, recv_sems, after, name):
    nt = len(bufs)

    def body(*refs):
        ins, s_sems, r_sems = refs[:nt], refs[nt], refs[nt + 1]
        for send, land in _gather_copies(ins, s_sems, r_sems):
            send.wait_send()
            land.wait_recv()

    return _pallas(body, name=name, in_specs=[HBM] * nt + [SEM, SEM, ANY], out_specs=[HBM] * nt,
                   out_shape=[pltpu.HBM(b.shape, b.dtype) for b in bufs],
                   input_output_aliases={t: t for t in range(nt)},
                   compiler_params=pltpu.CompilerParams(has_side_effects=EFFECT))(*bufs, send_sems, recv_sems, after)


def _forward_copies(refs, send_sems, recv_sems):
    x, y, c, chips = _place()
    out = []
    for t, ref in enumerate(refs):
        kh = ref.shape[1] // 2
        for j, (px, py) in enumerate(chips):
            mine = ref.at[2 * px + py, pl.ds(c * kh, kh)]
            land = ref.at[2 * px + py, pl.ds((1 - c) * kh, kh)]
            out.append((_rcopy(mine, mine, send_sems, recv_sems, 3 * t + j, (x, y, 1 - c)),
                        _rcopy(land, land, send_sems, recv_sems, 3 * t + j, (x, y, 1 - c))))
    return out


def forward_start(bufs, after, name):
    nt = len(bufs)

    def body(*refs):
        ins, send_sems, recv_sems, token = refs[:nt], refs[nt + 1], refs[nt + 2], refs[-1]
        for send, _ in _forward_copies(ins, send_sems, recv_sems):
            send.start()
        token[...] = jnp.zeros_like(token)

    outs = _pallas(body, name=name, in_specs=[HBM] * nt + [ANY],
                   out_specs=(SEM, SEM) + (HBM,) * nt + (pl.BlockSpec(memory_space=pltpu.VMEM),),
                   out_shape=(pltpu.SemaphoreType.DMA((3 * nt,)), pltpu.SemaphoreType.DMA((3 * nt,)))
                   + tuple(pltpu.HBM(b.shape, b.dtype) for b in bufs) + (_sds((8, HEAD), F32),),
                   input_output_aliases={t: 2 + t for t in range(nt)},
                   compiler_params=pltpu.CompilerParams(has_side_effects=EFFECT))(*[_in_hbm(b) for b in bufs], after)
    return outs[0], outs[1], list(outs[2:2 + nt]), outs[-1]


def forward_wait(bufs, send_sems, recv_sems, after, name):
    nt = len(bufs)

    def body(*refs):
        ins, s_sems, r_sems = refs[:nt], refs[nt], refs[nt + 1]
        for send, land in _forward_copies(ins, s_sems, r_sems):
            send.wait_send()
            land.wait_recv()

    return _pallas(body, name=name, in_specs=[HBM] * nt + [SEM, SEM, ANY], out_specs=[HBM] * nt,
                   out_shape=[pltpu.HBM(b.shape, b.dtype) for b in bufs],
                   input_output_aliases={t: t for t in range(nt)},
                   compiler_params=pltpu.CompilerParams(has_side_effects=EFFECT))(*bufs, send_sems, recv_sems, after)


def pair_forward(bufs, name):
    nt = len(bufs)

    def body(*refs):
        outs = refs[nt:2 * nt]
        send_sems, recv_sems = refs[2 * nt:]
        x, y, c, chips = _place()
        cps = []
        for t in range(nt):
            kh = outs[t].shape[1] // 2
            for j, (px, py) in enumerate(chips):
                blk = outs[t].at[2 * px + py, pl.ds(c * kh, kh)]
                cps.append(_rcopy(blk, blk, send_sems, recv_sems, 3 * t + j, (x, y, 1 - c)))
                cps[-1].start()
        for t in range(nt):
            kh = outs[t].shape[1] // 2
            for j, (px, py) in enumerate(chips):
                blk = outs[t].at[2 * px + py, pl.ds((1 - c) * kh, kh)]
                _rcopy(blk, blk, send_sems, recv_sems, 3 * t + j, (x, y, 1 - c)).wait_recv()
        for cp in cps:
            cp.wait_send()

    return _pallas(body, name=name, in_specs=[ANY] * nt, out_specs=[ANY] * nt,
                   out_shape=[_sds(b.shape, b.dtype) for b in bufs],
                   input_output_aliases={t: t for t in range(nt)},
                   scratch_shapes=[pltpu.SemaphoreType.DMA((3 * nt,)), pltpu.SemaphoreType.DMA((3 * nt,))],
                   compiler_params=pltpu.CompilerParams(has_side_effects=True))(*bufs)


def pair_exchange(grads, name):
    nt = len(grads)

    def body(*refs):
        ins, outs = refs[:nt], refs[nt:2 * nt]
        send_sems, recv_sems = refs[2 * nt:]
        x, y, c, _ = _place()
        sibling = (x, y, 1 - c)
        cps = []
        for t in range(nt):
            kh = ins[t].shape[1] // 2
            cps.append(_rcopy(ins[t].at[:, pl.ds((1 - c) * kh, kh), :], outs[t], send_sems, recv_sems, t, sibling))
            cps[-1].start()
        for cp in cps:
            cp.wait_recv()
        for cp in cps:
            cp.wait_send()

    return _pallas(body, name=name, in_specs=[ANY] * nt, out_specs=[ANY] * nt,
                   out_shape=[_sds((N_CHIPS, g.shape[1] // 2, g.shape[2]), g.dtype) for g in grads],
                   scratch_shapes=[pltpu.SemaphoreType.DMA((nt,)), pltpu.SemaphoreType.DMA((nt,))],
                   compiler_params=pltpu.CompilerParams(has_side_effects=True))(*grads)


def _exchange_copies(sums, lands, send_sems, recv_sems):
    x, y, c, chips = _place()
    return [_rcopy(s.at[2 * px + py], l.at[j], send_sems, recv_sems, 3 * t + j, (px, py, c))
            for t, (s, l) in enumerate(zip(sums, lands)) for j, (px, py) in enumerate(chips)]


def exchange_start(sums, name):
    nt = len(sums)
    lands = [lax.empty((3,) + s.shape[1:], s.dtype) for s in sums]

    def body(*refs):
        ins, zones, send_sems, recv_sems, token = refs[:nt], refs[nt:2 * nt], refs[2 * nt], refs[2 * nt + 1], refs[-1]
        for cp in _exchange_copies(ins, zones, send_sems, recv_sems):
            cp.start()
        token[...] = jnp.zeros_like(token)

    outs = _pallas(body, name=name, in_specs=[HBM] * (2 * nt),
                   out_specs=(SEM, SEM) + (HBM,) * (2 * nt) + (pl.BlockSpec(memory_space=pltpu.VMEM),),
                   out_shape=(pltpu.SemaphoreType.DMA((3 * nt,)), pltpu.SemaphoreType.DMA((3 * nt,)))
                   + tuple(pltpu.HBM(a.shape, a.dtype) for a in list(sums) + lands) + (_sds((8, HEAD), F32),),
                   input_output_aliases={t: 2 + t for t in range(2 * nt)},
                   compiler_params=pltpu.CompilerParams(has_side_effects=EFFECT))(*[_in_hbm(a) for a in list(sums) + lands])
    return outs[0], outs[1], list(outs[2:2 + nt]), list(outs[2 + nt:2 + 2 * nt]), outs[-1]


def exchange_wait(sums, lands, send_sems, recv_sems, after, name):
    nt = len(sums)

    def body(*refs):
        ins, zones, s_sems, r_sems = refs[:nt], refs[nt:2 * nt], refs[2 * nt], refs[2 * nt + 1]
        for cp in _exchange_copies(ins, zones, s_sems, r_sems):
            cp.wait_send()
            cp.wait_recv()

    outs = _pallas(body, name=name, in_specs=[HBM] * (2 * nt) + [SEM, SEM, ANY], out_specs=[HBM] * (2 * nt),
                   out_shape=[pltpu.HBM(a.shape, a.dtype) for a in list(sums) + list(lands)],
                   input_output_aliases={t: t for t in range(2 * nt)},
                   compiler_params=pltpu.CompilerParams(has_side_effects=EFFECT))(*sums, *lands, send_sems, recv_sems, after)
    return list(outs[:nt]), list(outs[nt:])


def pair_share(halves, name):
    nt = len(halves)

    def body(*refs):
        ins, outs = refs[:nt], refs[nt:2 * nt]
        send_sems, recv_sems = refs[2 * nt:]
        x, y, c, _ = _place()
        cps = []
        for t in range(nt):
            cps.append(_rcopy(ins[t], outs[t], send_sems, recv_sems, t, (x, y, 1 - c)))
            cps[-1].start()
        for cp in cps:
            cp.wait_recv()
        for cp in cps:
            cp.wait_send()

    return _pallas(body, name=name, in_specs=[ANY] * nt, out_specs=[ANY] * nt,
                   out_shape=[_sds(h.shape, h.dtype) for h in halves],
                   scratch_shapes=[pltpu.SemaphoreType.DMA((nt,)), pltpu.SemaphoreType.DMA((nt,))],
                   compiler_params=pltpu.CompilerParams(has_side_effects=True))(*halves)


def small_allreduce(pack, name):
    r = pack.shape[0]

    def body(in_ref, out_ref, buf, send_sems, recv_sems):
        x, y, c, _ = _place()
        me = 4 * x + 2 * y + c
        sends = []
        for k in range(1, 8):
            to = ((x + ((k >> 2) & 1)) % 2, (y + ((k >> 1) & 1)) % 2, (c + (k & 1)) % 2)
            cp = _rcopy(in_ref, buf.at[me], send_sems, recv_sems, k - 1, to)
            cp.start()
            sends.append((cp, to))
        buf[pl.ds(me, 1)] = in_ref[...][None]
        for k, (_, to) in enumerate(sends):
            peer = 4 * to[0] + 2 * to[1] + to[2]
            _rcopy(in_ref, buf.at[peer], send_sems, recv_sems, k, to).wait_recv()
        for cp, _ in sends:
            cp.wait_send()
        acc = buf[0]
        for d in range(1, 8):
            acc = acc + buf[d]
        out_ref[...] = acc

    vm = pl.BlockSpec(memory_space=pltpu.VMEM)
    return _pallas(body, name=name, in_specs=[vm], out_specs=vm, out_shape=_sds((r, HEAD), F32),
                   scratch_shapes=[pltpu.VMEM((8, r, HEAD), F32), pltpu.SemaphoreType.DMA((7,)),
                                   pltpu.SemaphoreType.DMA((7,))],
                   compiler_params=pltpu.CompilerParams(has_side_effects=True))(pack)


def add_halves(c_idx, grad, other, name):
    _, k, ns = grad.shape
    kh = k // 2
    tr = _tile(kh, max(16, (1 << 19) // ns), 16)
    nr = kh // tr

    def body(c_ref, g_ref, o_ref, s_ref):
        s_ref[...] = (g_ref[...].astype(F32) + o_ref[...].astype(F32)).astype(s_ref.dtype)

    gs = pltpu.PrefetchScalarGridSpec(
        num_scalar_prefetch=1, grid=(N_CHIPS, nr),
        in_specs=[pl.BlockSpec((None, tr, ns), lambda g, i, c_ref: (g, c_ref[0] * nr + i, 0)),
                  pl.BlockSpec((None, tr, ns), lambda g, i, c_ref: (g, i, 0))],
        out_specs=pl.BlockSpec((None, tr, ns), lambda g, i, c_ref: (g, i, 0)))
    return _pallas(body, name=name, grid_spec=gs, out_shape=_sds((N_CHIPS, kh, ns), XDT),
                   compiler_params=_params(("parallel", "parallel")))(c_idx, grad, other)


def add_chips(chip_idx, sums, recv, stack, layer, n_layers, name):
    _, kh, ns = sums.shape
    tr = _tile(kh, max(16, (1 << 19) // ns), 16)
    has_stack = stack is not None

    def body(k_ref, s_ref, r0, r1, r2, *rest):
        o_ref = rest[-1]
        o_ref[...] = ((s_ref[...].astype(F32) + r0[...].astype(F32)) + r1[...].astype(F32)) + r2[...].astype(F32)

    rspec = [pl.BlockSpec((None, tr, ns), functools.partial(lambda i, k_ref, j: (j, i, 0), j=j)) for j in range(3)]
    gs = pltpu.PrefetchScalarGridSpec(
        num_scalar_prefetch=1, grid=(kh // tr,),
        in_specs=[pl.BlockSpec((None, tr, ns), lambda i, k_ref: (k_ref[0], i, 0))] + rspec + ([ANY] if has_stack else []),
        out_specs=pl.BlockSpec((None, tr, ns), lambda i, k_ref: (layer, i, 0)))
    args = (chip_idx, sums, recv, recv, recv) + ((stack,) if has_stack else ())
    return _pallas(body, name=name, grid_spec=gs, out_shape=_sds((n_layers, kh, ns), F32),
                   input_output_aliases={5: 0} if has_stack else {},
                   compiler_params=_params(("parallel",)))(*args)


def adamw_big(c_idx, w, m, v, mine, other, name):
    nl, k, ns = w.shape
    kh = k // 2
    tr = _tile(kh, max(8, (1 << 18) // ns), 8)
    nr = kh // tr
    c1 = 1.0 - ADAM_B1 ** ADAM_STEP
    c2 = 1.0 - ADAM_B2 ** ADAM_STEP

    def body(c_ref, w_ref, m_ref, v_ref, a_ref, b_ref, g_ref, d_ref, mo_ref, vo_ref):
        gv = jnp.where(pl.program_id(2) == c_ref[0], a_ref[...], b_ref[...])
        mn = ADAM_B1 * m_ref[...] + (1.0 - ADAM_B1) * gv
        vn = ADAM_B2 * v_ref[...] + (1.0 - ADAM_B2) * (gv * gv)
        g_ref[...] = gv
        d_ref[...] = -ADAM_LR * ((mn / c1) / (jnp.sqrt(vn / c2) + ADAM_EPS) + ADAM_WD * w_ref[...])
        mo_ref[...] = mn
        vo_ref[...] = vn

    full = pl.BlockSpec((None, tr, ns), lambda l, i, hh, c_ref: (l, hh * nr + i, 0))
    half = pl.BlockSpec((None, tr, ns), lambda l, i, hh, c_ref: (l, i, 0))
    gs = pltpu.PrefetchScalarGridSpec(num_scalar_prefetch=1, grid=(nl, nr, 2),
                                      in_specs=[full, full, full, half, half], out_specs=[full] * 4)
    return _pallas(body, name=name, grid_spec=gs, out_shape=[_sds(w.shape, F32)] * 4,
                   compiler_params=_params(("parallel", "parallel", "arbitrary")))(c_idx, w, m, v, mine, other)


W_NAMES = ("w_in", "w_br_a", "w_br_b", "w_br_c", "w_o", "w_gate_up", "w_down")


def _rope_tables(n):
    half = HEAD // 2
    inv_freq = ROPE_THETA ** (-jnp.arange(half, dtype=F32) * 2.0 / HEAD)
    ang = jnp.arange(n, dtype=F32)[:, None] * inv_freq[None, :]
    cos, sin = jnp.cos(ang), jnp.sin(ang)
    return jnp.concatenate([cos, cos], axis=-1), jnp.concatenate([-sin, sin], axis=-1)


def _rpb_windows(rpb):
    pad = jnp.pad(rpb, ((0, 0), (0, 1), (0, GRID_W - rpb.shape[2])))
    wins = [pad[:, i0:i0 + C_WIN_ROWS].reshape(C_HEADS, 1, C_WIN_ROWS * GRID_W) for i0 in range(C_WIN_ROWS)]
    return jnp.stack(wins, axis=1)


def _rows(t):
    return t.reshape(-1, t.shape[-1])


def _like(t, ref):
    return t.reshape(ref.shape)


def layer_fwd(x, p, w, cos2, sin2, rest=None, mid=None):
    n, d = x.shape
    s = {"x": x}
    s["h"] = rmsnorm_fwd(x, p["norm1_g"], "norm1")
    s["proj"] = mm_x_wcol(s["h"], w["w_in"], ADT, "proj")
    gains = jnp.pad(p["qk_norm_g"], ((0, 2), (0, 0)))
    pp = s["pp"] = qk_prep(s["proj"], gains, cos2, sin2, "qk_prep")
    sink = p["sink_a"].reshape(1, A_Q_HEADS)
    s["oa"], s["lse_a"] = band_attn_fwd(pp["qa"], pp["ka"], pp["va"], sink, seqs=A_KV_HEADS, G=A_GROUP,
                                        nh=A_KV_HEADS, radius=A_RADIUS, name="attn_a")
    s["ob"], s["lse_b"] = [], []
    for g, dil in enumerate(B_DILS):
        o, lse = band_attn_fwd(_rows(pp[f"qb{g}"]), _rows(pp[f"kb{g}"]), _rows(pp[f"vb{g}"]), None, seqs=B_HG, G=1,
                               nh=B_HG, radius=B_RADIUS, classes=dil, name=f"attn_b{g}")
        s["ob"].append(_like(o, pp[f"qb{g}"]))
        s["lse_b"].append(_like(lse, pp[f"qb{g}"]))
    ob = b_combine_fwd(s["ob"], s["lse_b"], "b_combine")
    s["bias"] = rpb_expand(_rpb_windows(p["rpb_c"]), "rpb_expand")
    s["oc"], s["lse_c"] = c_attn_fwd(pp["qc"], pp["kc"], pp["vc"], s["bias"], "attn_c")
    s["o_in"] = (s["oa"], ob, s["oc"])
    if rest is not None:
        w = {**w, **rest(s["oc"])}
    s["w"] = w
    s["ys"] = [mm_x_wcol(o, w[k], ADT, "branch_" + k[-1]) for o, k in zip(s["o_in"], ("w_br_a", "w_br_b", "w_br_c"))]
    s["merged"] = gate_merge(s["proj"], s["ys"], "gate_merge")
    s["x_mid"] = mm_x_w(s["merged"], w["w_o"], "out_proj", res=x)
    s["h2"] = rmsnorm_fwd(s["x_mid"], p["norm2_g"], "norm2")
    token = None if mid is None else mid(s["h2"])
    s["gu"], s["act"] = gate_up_swiglu(s["h2"], w["w_gate_up"], "gate_up", after=token)
    x_out = mm_x_w(s["act"], w["w_down"], "down", res=s["x_mid"], tk_pref=2816)
    return x_out, s


def layer_bwd(dx_out, s, p, cos2, sin2, on_dws, early=False):
    n, d = dx_out.shape
    pp, w = s["pp"], s["w"]
    dgu = d_gate_up(dx_out, w["w_down"], s["gu"], "d_gate_up")
    dw_down = mm_aT_d(s["act"], dx_out, "dw_down")
    dh2 = mm_x_wcolT(dgu, w["w_gate_up"], "d_h2", stacked_in=2)
    dw_gu = mm_aT_d_wcol(s["h2"], dgu, "dw_gate_up", tn_pref=1408, stacked_in=2)
    dx_mid, dg2 = rmsnorm_bwd(s["x_mid"], p["norm2_g"], dh2, dx_out, "norm2_bwd")
    dws = {"w_gate_up": dw_gu, "w_down": dw_down.reshape(N_CHIPS, dw_down.shape[0] // N_CHIPS, d)}
    token = on_dws(dws) if early else None

    dmerged = mm_x_wT(dx_mid, w["w_o"], "d_merged", after=token)
    dw_o = mm_aT_d(s["merged"], dx_mid, "dw_o")
    dys, dproj = gate_bwd(s["proj"], s["ys"], dmerged, "gate_bwd")
    dos, dw_br = [], []
    for b, (o, k) in enumerate(zip(s["o_in"], ("w_br_a", "w_br_b", "w_br_c"))):
        dos.append(mm_x_wcolT(dys, w[k], "d_o_" + k[-1], lead=b))
        dw_br.append(mm_aT_d_wcol(o, dys, "dw_br_" + k[-1], lead=b))

    grads = {}
    sink = p["sink_a"].reshape(1, A_Q_HEADS)
    grads["qa"], grads["ka"], grads["va"], dsink = band_attn_bwd(
        pp["qa"], pp["ka"], pp["va"], sink, s["oa"], s["lse_a"], dos[0], None,
        seqs=A_KV_HEADS, G=A_GROUP, nh=A_KV_HEADS, radius=A_RADIUS, name="attn_a_bwd")
    dobs, dlses = b_combine_bwd(dos[1], s["ob"], s["lse_b"], "b_combine_bwd")
    for g, dil in enumerate(B_DILS):
        dq, dk, dv = band_attn_bwd(_rows(pp[f"qb{g}"]), _rows(pp[f"kb{g}"]), _rows(pp[f"vb{g}"]), None,
                                   _rows(s["ob"][g]), _rows(s["lse_b"][g]), _rows(dobs[g]), _rows(dlses[g]),
                                   seqs=B_HG, G=1, nh=B_HG, radius=B_RADIUS, classes=dil, name=f"attn_b{g}_bwd")
        grads[f"qb{g}"], grads[f"kb{g}"], grads[f"vb{g}"] = [_like(t, pp[f"qb{g}"]) for t in (dq, dk, dv)]
    grads["qc"], grads["kc"], grads["vc"], dbias = c_attn_bwd(pp["qc"], pp["kc"], pp["vc"], s["bias"], s["oc"],
                                                              s["lse_c"], dos[2], "attn_c_bwd")
    drpb = rpb_reduce(dbias, "rpb_reduce")[:, :2 * C_WIN_ROWS - 1, :2 * C_WIN_COLS - 1]
    gains = jnp.pad(p["qk_norm_g"], ((0, 2), (0, 0)))
    dproj, dgains = qk_prep_bwd(s["proj"], gains, cos2, sin2, grads, dproj, "qk_prep_bwd")
    dw_in = mm_aT_d_wcol(s["h"], dproj, "dw_in")
    rest = {"w_in": dw_in, "w_br_a": dw_br[0], "w_br_b": dw_br[1], "w_br_c": dw_br[2],
            "w_o": dw_o.reshape(N_CHIPS, d // N_CHIPS, d)}
    token = on_dws(rest if early else {**rest, **dws})
    dh = mm_x_wcolT(dproj, w["w_in"], "d_h", after=token)
    dx_in, dg1 = rmsnorm_bwd(s["x"], p["norm1_g"], dh, dx_mid, "norm1_bwd")
    small = {"norm1_g": dg1[0], "qk_norm_g": dgains[:6], "sink_a": dsink[0, :, 0],
             "rpb_c": drpb, "norm2_g": dg2[0]}
    return dx_in, small


SMALL_NAMES = ("norm1_g", "qk_norm_g", "sink_a", "rpb_c", "norm2_g")


def _pack_small(parts, extra=None):
    flat = [parts[k].reshape(-1) for k in SMALL_NAMES]
    flat.append(jnp.zeros((1,), F32) if extra is None else extra.reshape(1))
    v = jnp.concatenate(flat)
    rows = -(-v.shape[0] // (8 * HEAD)) * 8
    return jnp.pad(v, (0, rows * HEAD - v.shape[0])).reshape(rows, HEAD)


def _unpack_small(pack, like):
    v = pack.reshape(-1)
    out, off = {}, 0
    for k in SMALL_NAMES:
        size = math.prod(like[k].shape)
        out[k] = v[off:off + size].reshape(like[k].shape)
        off += size
    return out, v[off]


def kernel(x, norm1_g, w_in, qk_norm_g, sink_a, rpb_c, w_br_a, w_br_b, w_br_c, w_o, norm2_g, w_gate_up, w_down, loss_target, m_norm1_g, m_w_in, m_qk_norm_g, m_sink_a, m_rpb_c, m_w_br_a, m_w_br_b, m_w_br_c, m_w_o, m_norm2_g, m_w_gate_up, m_w_down, v_norm1_g, v_w_in, v_qk_norm_g, v_sink_a, v_rpb_c, v_w_br_a, v_w_br_b, v_w_br_c, v_w_o, v_norm2_g, v_w_gate_up, v_w_down):
    big = dict(w_in=w_in, w_br_a=w_br_a, w_br_b=w_br_b, w_br_c=w_br_c, w_o=w_o, w_gate_up=w_gate_up, w_down=w_down)
    big_m = dict(w_in=m_w_in, w_br_a=m_w_br_a, w_br_b=m_w_br_b, w_br_c=m_w_br_c, w_o=m_w_o, w_gate_up=m_w_gate_up, w_down=m_w_down)
    big_v = dict(w_in=v_w_in, w_br_a=v_w_br_a, w_br_b=v_w_br_b, w_br_c=v_w_br_c, w_o=v_w_o, w_gate_up=v_w_gate_up, w_down=v_w_down)
    small = dict(norm1_g=norm1_g, qk_norm_g=qk_norm_g, sink_a=sink_a, rpb_c=rpb_c, norm2_g=norm2_g)
    small_m = dict(norm1_g=m_norm1_g, qk_norm_g=m_qk_norm_g, sink_a=m_sink_a, rpb_c=m_rpb_c, norm2_g=m_norm2_g)
    small_v = dict(norm1_g=v_norm1_g, qk_norm_g=v_qk_norm_g, sink_a=v_sink_a, rpb_c=v_rpb_c, norm2_g=v_norm2_g)
    n_layers = w_in.shape[0]
    n, d = x.shape[1], x.shape[2]
    c_idx = lax.axis_index("c").astype(jnp.int32).reshape(1)
    chip_idx = (2 * lax.axis_index("x") + lax.axis_index("y")).astype(jnp.int32).reshape(1)
    cos2, sin2 = _rope_tables(n)

    def as_weights(names, got):
        w = dict(zip(names, got))
        if "w_o" in w:
            w["w_o"] = w["w_o"].reshape(d, d)
            w["w_down"] = w["w_down"].reshape(-1, d)
        return w

    def gathered(names, started, after):
        send_sems, recv_sems, bufs, _ = started
        return as_weights(names, pair_forward(gather_wait(bufs, send_sems, recv_sems, after, "gather_wait"), "pair_forward"))

    def start_gather(names, l, after):
        return gather_start([cast_place(chip_idx, big[k], l, "cast_" + k) for k in names], after, "gather_start")

    first, others_0 = W_NAMES[:1], W_NAMES[1:]
    started = start_gather(first, 0, chip_idx)
    started_rest = start_gather(others_0, 0, started[3])
    weights = gathered(first, started, started_rest[3])
    rest = lambda after: gathered(others_0, started_rest, after)

    xs, saved = x[0], []
    for l in range(n_layers):
        p = {k: small[k][l] for k in SMALL_NAMES}
        mid, fwd = None, {}
        if l + 1 < n_layers:
            started = start_gather(W_NAMES, l + 1, weights["w_in"])
            p["norm1_g"] = p["norm1_g"] + started[3][0, 0]

            def mid(after, started=started, fwd=fwd):
                send_sems, recv_sems, bufs, _ = started
                arrived = gather_wait(bufs, send_sems, recv_sems, after, "gather_wait")
                fwd["started"] = forward_start(arrived, after, "forward_start")
                return fwd["started"][3]

        xs, s = layer_fwd(xs, p, weights, cos2, sin2, rest, mid)
        saved.append(s)
        if l + 1 < n_layers:
            send_sems, recv_sems, bufs, _ = fwd["started"]
            weights, rest = as_weights(W_NAMES, forward_wait(bufs, send_sems, recv_sems, xs, "forward_wait")), None
    loss_tile, dx = loss_head(xs, loss_target[0], "loss_head")

    halves = {k: None for k in W_NAMES}
    small_g = [None] * n_layers
    pending = []

    def finish_exchanges(after):
        for l, names, (send_sems, recv_sems, sums, lands, _) in pending:
            sums, from_chips = exchange_wait(sums, lands, send_sems, recv_sems, after, "exchange_wait")
            for k, sm, r in zip(names, sums, from_chips):
                halves[k] = add_chips(chip_idx, sm, r, halves[k], l, n_layers, "add_chips_" + k)
        pending.clear()

    def make_on_dws(l, wait_first):
        def on_dws(dws):
            names = [k for k in W_NAMES if k in dws]
            parts = [dws[k] for k in names]
            if wait_first:
                finish_exchanges(parts[0])
            from_sibling = pair_exchange(parts, "pair_exchange")
            sums = [add_halves(c_idx, g, o, "add_halves_" + k) for g, o, k in zip(parts, from_sibling, names)]
            pending.append((l, names, exchange_start(sums, "exchange_start")))
            return pending[-1][2][4]
        return on_dws

    for l in reversed(range(n_layers)):
        p = {k: small[k][l] for k in SMALL_NAMES}
        last = l == 0
        dx, small_g[l] = layer_bwd(dx, saved[l], p, cos2, sin2, make_on_dws(l, not last), early=last)
    finish_exchanges(dx)
    halves = [halves[k] for k in W_NAMES]
    others = pair_share(halves, "pair_share")

    mine = {k: jnp.stack([small_g[l][k] for l in range(n_layers)]) for k in SMALL_NAMES}
    total = small_allreduce(_pack_small(mine, loss_tile[0, 0]), "small_allreduce")
    grad_small, loss = _unpack_small(total, small)

    outs = {}
    for k, mine_half, other_half in zip(W_NAMES, halves, others):
        outs[k] = adamw_big(c_idx, big[k], big_m[k], big_v[k], mine_half, other_half, "adamw_" + k)
    res = adamw(_pack_small(small), _pack_small(grad_small), _pack_small(small_m), _pack_small(small_v), "adamw_small")
    unp = [_unpack_small(t, small)[0] for t in res]
    for k in SMALL_NAMES:
        outs[k] = (grad_small[k],) + tuple(u[k] for u in unp)

    order = ("norm1_g", "w_in", "qk_norm_g", "sink_a", "rpb_c", "w_br_a", "w_br_b", "w_br_c", "w_o", "norm2_g",
             "w_gate_up", "w_down")
    return (loss, dx[None]) + tuple(outs[k][i] for i in range(4) for k in order)
```

```python
import functools
import math

import jax
import jax.numpy as jnp
from jax import lax
from jax.experimental import pallas as pl
from jax.experimental.pallas import tpu as pltpu

F32 = jnp.float32
CDT = jnp.bfloat16
XDT = jnp.bfloat16
ADT = jnp.bfloat16

HEAD = 128
NORM_EPS = 1e-6
ROPE_THETA = 10000.0
A_Q_HEADS, A_KV_HEADS, A_GROUP, A_RADIUS = 8, 2, 4, 128
B_DILS = (1, 4, 16)
B_RADIUS = 64
B_HG = 4
C_HEADS, GRID_W, C_WIN_ROWS, C_WIN_COLS = 8, 64, 8, 16
QKV_W = 9216
COL = dict(qa=0, ka=1024, va=1280, qb=1536, kb=3072, vb=4608, qc=6144, kc=7168, vc=8192)
NEG = -1e30
SCALE = HEAD ** -0.5
N_CHIPS = 4

ADAM_LR, ADAM_B1, ADAM_B2, ADAM_EPS, ADAM_WD, ADAM_STEP = 0.001, 0.9, 0.999, 1e-08, 0.01, 10

VMEM_LIMIT = 56 * 1024 * 1024
MESH = pl.DeviceIdType.MESH


def _pallas(body, **kw):
    return pl.pallas_call(body, **kw)


def _params(sem=None, **kw):
    if sem is not None:
        kw["dimension_semantics"] = sem
    return pltpu.CompilerParams(vmem_limit_bytes=VMEM_LIMIT, **kw)


def _tile(dim, pref, mult=128):
    best = None
    for t in range(mult, min(dim, pref) + 1, mult):
        if dim % t == 0:
            best = t
    return dim if best is None else best


def _sds(shape, dtype):
    return jax.ShapeDtypeStruct(tuple(shape), dtype)


_DIMS = {"nn": (((1,), (0,)), ((), ())), "nt": (((1,), (1,)), ((), ())), "tn": (((0,), (0,)), ((), ()))}


def _matmul(a, b, *, mode, grid, a_spec, b_spec, o_spec, out_shape, acc_shape, name, res=None, res_spec=None,
            after=None):
    nk = grid[2]
    has_res = res is not None
    n_in = 2 + int(has_res) + int(after is not None)

    def body(*refs):
        a_ref, b_ref = refs[:2]
        r_ref = refs[2] if has_res else None
        o_ref, rest = refs[n_in], refs[n_in + 1:]
        p = lax.dot_general(a_ref[...].astype(CDT), b_ref[...].astype(CDT), _DIMS[mode],
                            preferred_element_type=F32)

        def finish(acc):
            if has_res:
                acc = acc + r_ref[...].astype(F32)
            o_ref[...] = acc.astype(o_ref.dtype)

        if nk == 1:
            finish(p)
        else:
            acc_ref = rest[0]
            k = pl.program_id(2)

            @pl.when(k == 0)
            def _():
                acc_ref[...] = p

            @pl.when(k > 0)
            def _():
                acc_ref[...] += p

            @pl.when(k == nk - 1)
            def _():
                finish(acc_ref[...])

    in_specs = [a_spec, b_spec] + ([res_spec] if has_res else [])
    args = (a, b) + ((res,) if has_res else ())
    if after is not None:
        in_specs.append(pl.BlockSpec(after.shape, lambda i, j, kk: (0, 0)))
        args += (after,)
    scratch = [] if nk == 1 else [pltpu.VMEM(acc_shape, F32)]
    return _pallas(body, name=name, grid=grid, in_specs=in_specs, out_specs=o_spec, out_shape=out_shape,
                   scratch_shapes=scratch, compiler_params=_params(("parallel", "parallel", "arbitrary")))(*args)


def mm_x_wcol(a, wg, out_dtype, name, tm_pref=1024, tn_pref=1024, stacked_out=1):
    m, k = a.shape
    ns = wg.shape[2]
    tm, tn = _tile(m, tm_pref, 8), _tile(ns, tn_pref)
    nj = ns // tn
    grid = (m // tm, N_CHIPS * nj, 1)
    a_spec = pl.BlockSpec((tm, k), lambda i, j, kk: (i, 0))
    b_spec = pl.BlockSpec((None, k, tn), lambda i, j, kk: (j // nj, 0, j % nj))
    if stacked_out == 1:
        o_spec = pl.BlockSpec((tm, tn), lambda i, j, kk: (i, j))
        out_shape = _sds((m, N_CHIPS * ns), out_dtype)
    else:
        per = N_CHIPS * nj // stacked_out
        o_spec = pl.BlockSpec((None, tm, tn), lambda i, j, kk: (j // per, i, j % per))
        out_shape = _sds((stacked_out, m, N_CHIPS * ns // stacked_out), out_dtype)
    return _matmul(a, wg, mode="nn", grid=grid, a_spec=a_spec, b_spec=b_spec, o_spec=o_spec,
                   out_shape=out_shape, acc_shape=(tm, tn), name=name)


def mm_x_wcolT(d, wg, name, res=None, tm_pref=1024, tn_pref=512, tk_pref=4096, stacked_in=1, lead=None, after=None):
    kdim, ns = wg.shape[1], wg.shape[2]
    m = d.shape[-2]
    tm, tn, tk = _tile(m, tm_pref, 8), _tile(kdim, tn_pref), _tile(ns, tk_pref)
    nkk = ns // tk
    grid = (m // tm, kdim // tn, N_CHIPS * nkk)
    if lead is not None:
        a_spec = pl.BlockSpec((None, tm, tk), lambda i, j, kk: (lead, i, kk))
    elif stacked_in == 1:
        a_spec = pl.BlockSpec((tm, tk), lambda i, j, kk: (i, kk))
    else:
        per = N_CHIPS * nkk // stacked_in
        a_spec = pl.BlockSpec((None, tm, tk), lambda i, j, kk: (kk // per, i, kk % per))
    b_spec = pl.BlockSpec((None, tn, tk), lambda i, j, kk: (kk // nkk, j, kk % nkk))
    o_spec = pl.BlockSpec((tm, tn), lambda i, j, kk: (i, j))
    return _matmul(d, wg, mode="nt", grid=grid, a_spec=a_spec, b_spec=b_spec, o_spec=o_spec,
                   out_shape=_sds((m, kdim), F32), acc_shape=(tm, tn), name=name,
                   res=res, res_spec=None if res is None else o_spec, after=after)


def mm_aT_d_wcol(a, d, name, tm_pref=512, tk_pref=4096, tn_pref=1024, stacked_in=1, lead=None):
    m, kdim = a.shape
    ntot = d.shape[-1] * stacked_in
    ns = ntot // N_CHIPS
    tm, tkm, tn = _tile(kdim, tm_pref), _tile(m, tk_pref, 8), _tile(ns, tn_pref)
    nj = ns // tn
    grid = (kdim // tm, N_CHIPS * nj, m // tkm)
    a_spec = pl.BlockSpec((tkm, tm), lambda i, j, kk: (kk, i))
    if lead is not None:
        b_spec = pl.BlockSpec((None, tkm, tn), lambda i, j, kk: (lead, kk, j))
    elif stacked_in == 1:
        b_spec = pl.BlockSpec((tkm, tn), lambda i, j, kk: (kk, j))
    else:
        per = N_CHIPS * nj // stacked_in
        b_spec = pl.BlockSpec((None, tkm, tn), lambda i, j, kk: (j // per, kk, j % per))
    o_spec = pl.BlockSpec((None, tm, tn), lambda i, j, kk: (j // nj, i, j % nj))
    return _matmul(a, d, mode="tn", grid=grid, a_spec=a_spec, b_spec=b_spec, o_spec=o_spec,
                   out_shape=_sds((N_CHIPS, kdim, ns), XDT), acc_shape=(tm, tn), name=name)


def mm_x_w(a, w, name, res=None, out_dtype=F32, tm_pref=1024, tn_pref=1024, tk_pref=2048):
    m, k = a.shape
    n = w.shape[1]
    tm, tn, tk = _tile(m, tm_pref, 8), _tile(n, tn_pref), _tile(k, tk_pref)
    grid = (m // tm, n // tn, k // tk)
    o_spec = pl.BlockSpec((tm, tn), lambda i, j, kk: (i, j))
    return _matmul(a, w, mode="nn", grid=grid,
                   a_spec=pl.BlockSpec((tm, tk), lambda i, j, kk: (i, kk)),
                   b_spec=pl.BlockSpec((tk, tn), lambda i, j, kk: (kk, j)),
                   o_spec=o_spec, out_shape=_sds((m, n), out_dtype), acc_shape=(tm, tn), name=name,
                   res=res, res_spec=None if res is None else o_spec)


def mm_x_wT(d, w, name, out_dtype=F32, tm_pref=1024, tn_pref=1024, after=None):
    m, n = d.shape
    k = w.shape[0]
    tm, tn = _tile(m, tm_pref, 8), _tile(k, tn_pref)
    grid = (m // tm, k // tn, 1)
    return _matmul(d, w, mode="nt", grid=grid,
                   a_spec=pl.BlockSpec((tm, n), lambda i, j, kk: (i, 0)),
                   b_spec=pl.BlockSpec((tn, n), lambda i, j, kk: (j, 0)),
                   o_spec=pl.BlockSpec((tm, tn), lambda i, j, kk: (i, j)),
                   out_shape=_sds((m, k), out_dtype), acc_shape=(tm, tn), name=name, after=after)


def mm_aT_d(a, d, name, tm_pref=512, tn_pref=512, tk_pref=4096):
    m, k = a.shape
    n = d.shape[1]
    tm, tn, tk = _tile(k, tm_pref), _tile(n, tn_pref), _tile(m, tk_pref, 8)
    grid = (k // tm, n // tn, m // tk)
    return _matmul(a, d, mode="tn", grid=grid,
                   a_spec=pl.BlockSpec((tk, tm), lambda i, j, kk: (kk, i)),
                   b_spec=pl.BlockSpec((tk, tn), lambda i, j, kk: (kk, j)),
                   o_spec=pl.BlockSpec((tm, tn), lambda i, j, kk: (i, j)),
                   out_shape=_sds((k, n), XDT), acc_shape=(tm, tn), name=name)


def rmsnorm_fwd(x, g, name):
    n, d = x.shape
    tm = _tile(n, 512, 8)

    def body(x_ref, g_ref, h_ref):
        xv = x_ref[...]
        r = lax.rsqrt(jnp.mean(xv * xv, axis=-1, keepdims=True) + NORM_EPS)
        h_ref[...] = (xv * r * g_ref[...]).astype(h_ref.dtype)

    return _pallas(body, name=name, grid=(n // tm,),
                   in_specs=[pl.BlockSpec((tm, d), lambda i: (i, 0)), pl.BlockSpec((1, d), lambda i: (0, 0))],
                   out_specs=pl.BlockSpec((tm, d), lambda i: (i, 0)), out_shape=_sds((n, d), CDT),
                   compiler_params=_params(("parallel",)))(x, g.reshape(1, d))


def rmsnorm_bwd(x, g, dh, dres, name):
    n, d = x.shape
    tm = _tile(n, 256, 8)

    def body(x_ref, g_ref, dh_ref, dres_ref, dx_ref, dxc_ref, dg_ref):
        xv = x_ref[...]
        r = lax.rsqrt(jnp.mean(xv * xv, axis=-1, keepdims=True) + NORM_EPS)
        dhv = dh_ref[...]
        u = dhv * g_ref[...]
        c = jnp.mean(xv * u, axis=-1, keepdims=True)
        dxv = dres_ref[...] + r * u - xv * (r * r * r * c)
        dx_ref[...] = dxv
        dxc_ref[...] = dxv.astype(dxc_ref.dtype)
        part = jnp.broadcast_to(jnp.sum(dhv * xv * r, axis=0, keepdims=True), (8, d))

        @pl.when(pl.program_id(0) == 0)
        def _():
            dg_ref[...] = part

        @pl.when(pl.program_id(0) > 0)
        def _():
            dg_ref[...] += part

    row = pl.BlockSpec((tm, d), lambda i: (i, 0))
    return _pallas(body, name=name, grid=(n // tm,),
                   in_specs=[row, pl.BlockSpec((1, d), lambda i: (0, 0)), row, row],
                   out_specs=[row, row, pl.BlockSpec((8, d), lambda i: (0, 0))],
                   out_shape=[_sds((n, d), F32), _sds((n, d), CDT), _sds((8, d), F32)],
                   compiler_params=_params(("arbitrary",)))(x, g.reshape(1, d), dh, dres)


def _norm_rope(xh, g, cos2, sin2):
    r = lax.rsqrt(jnp.mean(xh * xh, axis=-1, keepdims=True) + NORM_EPS)
    y = xh * r * g
    if cos2 is not None:
        y = y * cos2 + pltpu.roll(y, HEAD // 2, 1) * sin2
    return y


def _norm_rope_bwd(xh, g, cos2, sin2, dout):
    if cos2 is not None:
        dy = dout * cos2 + pltpu.roll(dout * sin2, HEAD // 2, 1)
    else:
        dy = dout
    r = lax.rsqrt(jnp.mean(xh * xh, axis=-1, keepdims=True) + NORM_EPS)
    u = dy * g
    c = jnp.mean(xh * u, axis=-1, keepdims=True)
    return r * u - xh * (r * r * r * c), dy * xh * r


_QK_GROUPS = (("qa", COL["qa"], 8, 0, True), ("ka", COL["ka"], 2, 1, True),
              ("qb", COL["qb"], 12, 2, True), ("kb", COL["kb"], 12, 3, True),
              ("qc", COL["qc"], 8, 4, False), ("kc", COL["kc"], 8, 5, False))
_V_GROUPS = (("va", COL["va"], 2), ("vb", COL["vb"], 12), ("vc", COL["vc"], 8))
_PREP_OUT = (("qa", 8), ("ka", 2), ("va", 2)) + tuple((f"{t}b{g}", 4) for t in "qkv" for g in range(3)) + (
    ("qc", 8), ("kc", 8), ("vc", 8))


def _prep_src(name):
    if name[1] == "b":
        base = COL[name[0] + "b"] + int(name[2]) * B_HG * HEAD
        gain = {"q": 2, "k": 3, "v": None}[name[0]]
        return base, gain, name[0] != "v"
    base = COL[name]
    gain = {"qa": 0, "ka": 1, "va": None, "qc": 4, "kc": 5, "vc": None}[name]
    return base, gain, name in ("qa", "ka")


def _prep_dil(name):
    return B_DILS[int(name[2])] if name[1] == "b" else 1


def _to_classes(val, scr, dil):
    scr[...] = val
    return [scr[pl.ds(r, val.shape[0] // dil, stride=dil), :] for r in range(dil)]


def _from_classes(parts, scr):
    for r, part in enumerate(parts):
        scr[pl.ds(r, part.shape[0], stride=len(parts)), :] = part
    return scr[...]


def _class_spec(tm, dil, width):
    if dil == 1:
        return pl.BlockSpec((tm, width), lambda i: (i, 0))
    return pl.BlockSpec((dil, tm // dil, width), lambda i: (0, i, 0))


def _class_shape(n, dil, width):
    return (n, width) if dil == 1 else (dil, n // dil, width)


def qk_prep(proj, gains, cos2, sin2, name):
    n = proj.shape[0]
    tm = _tile(n, 256, 8)

    def body(p_ref, g_ref, c_ref, s_ref, *refs):
        outs, scr = refs[:-1], refs[-1]
        cos2v, sin2v = c_ref[...], s_ref[...]
        for (nm, heads), o_ref in zip(_PREP_OUT, outs):
            base, gain, rope = _prep_src(nm)
            dil = _prep_dil(nm)
            for h in range(heads):
                hs = slice(h * HEAD, (h + 1) * HEAD)
                xh = p_ref[:, base + h * HEAD: base + (h + 1) * HEAD].astype(F32)
                if gain is None:
                    y = xh
                else:
                    y = _norm_rope(xh, g_ref[gain:gain + 1, :], cos2v if rope else None, sin2v if rope else None)
                if dil == 1:
                    o_ref[:, hs] = y.astype(o_ref.dtype)
                else:
                    for r, part in enumerate(_to_classes(y, scr.at[h % 4], dil)):
                        o_ref[r, :, hs] = part.astype(o_ref.dtype)

    tab = pl.BlockSpec((tm, HEAD), lambda i: (i, 0))
    outs = _pallas(body, name=name, grid=(n // tm,),
                   in_specs=[pl.BlockSpec((tm, QKV_W), lambda i: (i, 0)), pl.BlockSpec((8, HEAD), lambda i: (0, 0)), tab, tab],
                   out_specs=[_class_spec(tm, _prep_dil(nm), h * HEAD) for nm, h in _PREP_OUT],
                   out_shape=[_sds(_class_shape(n, _prep_dil(nm), h * HEAD), CDT) for nm, h in _PREP_OUT],
                   scratch_shapes=[pltpu.VMEM((4, tm, HEAD), F32)],
                   compiler_params=_params(("parallel",)))(proj, gains, cos2, sin2)
    return dict(zip([nm for nm, _ in _PREP_OUT], outs))


def qk_prep_bwd(proj, gains, cos2, sin2, grads, dproj, name):
    n = proj.shape[0]
    tm = _tile(n, 128, 8)
    names = [nm for nm, _ in _PREP_OUT]

    def body(p_ref, g_ref, c_ref, s_ref, *refs):
        g_refs, dp_ref, dg_ref, scr = refs[:len(names)], refs[len(names) + 1], refs[len(names) + 2], refs[-1]
        cos2v, sin2v = c_ref[...], s_ref[...]
        dg = [jnp.zeros((tm, HEAD), F32) for _ in range(6)]
        for (nm, heads), gr in zip(_PREP_OUT, g_refs):
            base, gain, rope = _prep_src(nm)
            dil = _prep_dil(nm)
            for h in range(heads):
                sl = slice(base + h * HEAD, base + (h + 1) * HEAD)
                hs = slice(h * HEAD, (h + 1) * HEAD)
                dout = gr[:, hs] if dil == 1 else _from_classes([gr[r, :, hs] for r in range(dil)], scr.at[h % 4])
                if gain is None:
                    dx = dout
                else:
                    dx, dgr = _norm_rope_bwd(p_ref[:, sl].astype(F32), g_ref[gain:gain + 1, :], cos2v if rope else None,
                                             sin2v if rope else None, dout)
                    dg[gain] = dg[gain] + dgr
                dp_ref[:, sl] = dx.astype(dp_ref.dtype)
        part = jnp.concatenate([jnp.sum(t, axis=0, keepdims=True) for t in dg] + [jnp.zeros((2, HEAD), F32)], axis=0)

        @pl.when(pl.program_id(0) == 0)
        def _():
            dg_ref[...] = part

        @pl.when(pl.program_id(0) > 0)
        def _():
            dg_ref[...] += part

    tab = pl.BlockSpec((tm, HEAD), lambda i: (i, 0))
    dp, dg = _pallas(body, name=name, grid=(n // tm,),
                     in_specs=[pl.BlockSpec((tm, QKV_W), lambda i: (i, 0)), pl.BlockSpec((8, HEAD), lambda i: (0, 0)), tab, tab]
                     + [_class_spec(tm, _prep_dil(nm), h * HEAD) for nm, h in _PREP_OUT] + [ANY],
                     out_specs=[pl.BlockSpec((tm, QKV_W), lambda i: (i, 0)), pl.BlockSpec((8, HEAD), lambda i: (0, 0))],
                     out_shape=[_sds(dproj.shape, dproj.dtype), _sds((8, HEAD), F32)],
                     input_output_aliases={4 + len(names): 0},
                     scratch_shapes=[pltpu.VMEM((4, tm, HEAD), F32)],
                     compiler_params=_params(("arbitrary",)))(proj, gains, cos2, sin2, *[grads[k] for k in names], dproj)
    return dp, dg


def _band_geometry(m, bq_pref, radius):
    bq = min(bq_pref, m)
    return bq, min(bq + 2 * radius, m)


def _band_window(i, bq, radius, m, w):
    start = pl.multiple_of(jnp.clip(i * bq - radius, 0, m - w), 64)
    qpos = i * bq + lax.broadcasted_iota(jnp.int32, (bq, w), 0)
    kpos = start + lax.broadcasted_iota(jnp.int32, (bq, w), 1)
    return start, jnp.abs(kpos - qpos) <= radius


def _band_specs(m, bq, G, nh, seqs):
    lg, nb = seqs // nh, m // bq
    qspec = pl.BlockSpec((bq, nh * G * HEAD), lambda s, i: ((s // lg) * nb + i, s % lg))
    kspec = pl.BlockSpec((m, nh * HEAD), lambda s, i: (s // lg, s % lg))
    return qspec, kspec, lg


def band_attn_fwd(q, k, v, sink, *, seqs, G, nh, radius, name, classes=1, bq_pref=128):
    m = q.shape[0] // classes
    bq, w = _band_geometry(m, bq_pref, radius)
    has_sink = sink is not None

    def body(*refs):
        if has_sink:
            sink_ref, q_ref, k_ref, v_ref, o_ref, lse_ref = refs
        else:
            q_ref, k_ref, v_ref, o_ref, lse_ref = refs
        s_id, i = pl.program_id(0), pl.program_id(1)
        start, valid = _band_window(i, bq, radius, m, w)
        units = [(h, g) for h in range(nh) for g in range(G)]
        sls = [slice((h * G + g) * HEAD, (h * G + g + 1) * HEAD) for h, g in units]
        k_ts = [k_ref[pl.ds(start, w), h * HEAD:(h + 1) * HEAD] for h in range(nh)]
        v_ts = [v_ref[pl.ds(start, w), h * HEAD:(h + 1) * HEAD] for h in range(nh)]
        q_ts = [q_ref[:, sl] for sl in sls]
        sks = [sink_ref[0, (s_id * nh + h) * G + g] for h, g in units] if has_sink else None
        ss = [jnp.where(valid, lax.dot_general(q_t, k_ts[h], _DIMS["nt"], preferred_element_type=F32) * SCALE, NEG)
              for q_t, (h, g) in zip(q_ts, units)]
        mxs = [jnp.max(s, axis=-1, keepdims=True) for s in ss]
        if has_sink:
            mxs = [jnp.maximum(mx, sk) for mx, sk in zip(mxs, sks)]
        ps = [jnp.exp(s - mx) for s, mx in zip(ss, mxs)]
        dens = [jnp.sum(p, axis=-1, keepdims=True) for p in ps]
        if has_sink:
            dens = [den + jnp.exp(sk - mx) for den, sk, mx in zip(dens, sks, mxs)]
        outs = [jnp.dot((p / den).astype(CDT), v_ts[h], preferred_element_type=F32)
                for p, den, (h, g) in zip(ps, dens, units)]
        for sl, o, mx, den in zip(sls, outs, mxs, dens):
            o_ref[:, sl] = o
            lse_ref[:, sl] = jnp.broadcast_to(mx + jnp.log(den), (bq, HEAD))

    qspec, kspec, lg = _band_specs(m, bq, G, nh, seqs)
    in_specs = ([pl.BlockSpec(memory_space=pltpu.SMEM)] if has_sink else []) + [qspec, kspec, kspec]
    args = ((sink,) if has_sink else ()) + (q, k, v)
    return _pallas(body, name=name, grid=(classes * lg, m // bq), in_specs=in_specs, out_specs=[qspec, qspec],
                   out_shape=[_sds(q.shape, F32), _sds(q.shape, F32)],
                   compiler_params=_params(("parallel", "arbitrary")))(*args)


def band_attn_bwd(q, k, v, sink, o, lse, do, dlse, *, seqs, G, nh, radius, name, classes=1, bq_pref=128):
    m = q.shape[0] // classes
    bq, w = _band_geometry(m, bq_pref, radius)
    has_sink, has_dlse = sink is not None, dlse is not None
    assert nh * G <= 8

    def body(*refs):
        refs = list(refs)
        sink_ref = refs.pop(0) if has_sink else None
        q_ref, k_ref, v_ref, o_ref, lse_ref, do_ref = refs[:6]
        refs = refs[6:]
        dlse_ref = refs.pop(0) if has_dlse else None
        dq_ref, dk_ref, dv_ref = refs[:3]
        dsink_ref = refs[3] if has_sink else None
        s_id, i = pl.program_id(0), pl.program_id(1)

        @pl.when(i == 0)
        def _():
            dk_ref[...] = jnp.zeros_like(dk_ref)
            dv_ref[...] = jnp.zeros_like(dv_ref)
            if has_sink:
                dsink_ref[...] = jnp.zeros_like(dsink_ref)

        start, valid = _band_window(i, bq, radius, m, w)
        units = [(h, g) for h in range(nh) for g in range(G)]
        sls = [slice((h * G + g) * HEAD, (h * G + g + 1) * HEAD) for h, g in units]
        kss = [slice(h * HEAD, (h + 1) * HEAD) for h in range(nh)]
        k_ts = [k_ref[pl.ds(start, w), ks] for ks in kss]
        v_ts = [v_ref[pl.ds(start, w), ks] for ks in kss]
        q_ts = [q_ref[:, sl] for sl in sls]
        lse_ts = [lse_ref[:, sl][:, :1] for sl in sls]
        do_ts = [do_ref[:, sl] for sl in sls]
        deltas = [jnp.sum(do_t * o_ref[:, sl], axis=-1, keepdims=True) for do_t, sl in zip(do_ts, sls)]
        dlse_ts = [dlse_ref[:, sl][:, :1] for sl in sls] if has_dlse else None
        dk_old = [dk_ref[pl.ds(start, w), ks] for ks in kss]
        dv_old = [dv_ref[pl.ds(start, w), ks] for ks in kss]
        dsink_old = dsink_ref[...] if has_sink else None

        ps = [jnp.exp(jnp.where(valid, lax.dot_general(q_t, k_ts[h], _DIMS["nt"], preferred_element_type=F32) * SCALE, NEG)
                      - lse_t) for q_t, lse_t, (h, g) in zip(q_ts, lse_ts, units)]
        do_cs = [do_t.astype(CDT) for do_t in do_ts]
        dps = [lax.dot_general(do_c, v_ts[h], _DIMS["nt"], preferred_element_type=F32) for do_c, (h, g) in zip(do_cs, units)]
        ts = [dp - delta for dp, delta in zip(dps, deltas)]
        if has_dlse:
            ts = [t + dl for t, dl in zip(ts, dlse_ts)]
        dss = [((p * t) * SCALE).astype(CDT) for p, t in zip(ps, ts)]
        dqs = [jnp.dot(ds, k_ts[h], preferred_element_type=F32) for ds, (h, g) in zip(dss, units)]
        dvs = [lax.dot_general(p.astype(CDT), do_c, _DIMS["tn"], preferred_element_type=F32) for p, do_c in zip(ps, do_cs)]
        dks = [lax.dot_general(ds, q_t, _DIMS["tn"], preferred_element_type=F32) for ds, q_t in zip(dss, q_ts)]
        dk_new = [dk_old[h] + sum(dks[h * G + g] for g in range(G)) for h in range(nh)]
        dv_new = [dv_old[h] + sum(dvs[h * G + g] for g in range(G)) for h in range(nh)]
        if has_sink:
            rows = []
            for (h, g), lse_t, delta in zip(units, lse_ts, deltas):
                sk = sink_ref[0, (s_id * nh + h) * G + g]
                rows.append(jnp.broadcast_to(-jnp.sum(jnp.exp(sk - lse_t) * delta, axis=0, keepdims=True), (1, HEAD)))
            rows += [jnp.zeros((1, HEAD), F32)] * (8 - len(rows))
            dsink_new = dsink_old + jnp.concatenate(rows, axis=0)

        for sl, dq in zip(sls, dqs):
            dq_ref[:, sl] = dq
        for h, ks in enumerate(kss):
            dk_ref[pl.ds(start, w), ks] = dk_new[h]
            dv_ref[pl.ds(start, w), ks] = dv_new[h]
        if has_sink:
            dsink_ref[...] = dsink_new

    qspec, kspec, lg = _band_specs(m, bq, G, nh, seqs)
    kin = pl.BlockSpec(kspec.block_shape, kspec.index_map, pipeline_mode=pl.Buffered(1))
    in_specs = ([pl.BlockSpec(memory_space=pltpu.SMEM)] if has_sink else []) + [qspec, kin, kin, qspec, qspec, qspec]
    in_specs += [qspec] if has_dlse else []
    args = ((sink,) if has_sink else ()) + (q, k, v, o, lse, do) + ((dlse,) if has_dlse else ())
    out_specs = [qspec, kspec, kspec]
    out_shape = [_sds(q.shape, F32), _sds(k.shape, F32), _sds(k.shape, F32)]
    if has_sink:
        out_specs.append(pl.BlockSpec((None, 8, HEAD), lambda s, i: (s, 0, 0)))
        out_shape.append(_sds((seqs // nh, 8, HEAD), F32))
    return _pallas(body, name=name, grid=(classes * lg, m // bq), in_specs=in_specs, out_specs=out_specs,
                   out_shape=out_shape, compiler_params=_params(("parallel", "arbitrary")))(*args)


def _group_weights(lses):
    mx = jnp.maximum(jnp.maximum(lses[0], lses[1]), lses[2])
    e = [jnp.exp(l - mx) for l in lses]
    tot = e[0] + e[1] + e[2]
    return [t / tot for t in e]


def _read_group(ref, dil, h, scr):
    hs = slice(h * HEAD, (h + 1) * HEAD)
    return ref[:, hs] if dil == 1 else _from_classes([ref[r, :, hs] for r in range(dil)], scr)


def b_combine_fwd(os_, lses, name):
    n, wd = os_[0].shape
    tm = _tile(n, 512, 8)

    def body(o0, o1, o2, l0, l1, l2, out_ref, scr):
        for h in range(wd // HEAD):
            ls = [_read_group(ref, dil, h, scr.at[g]) for g, (ref, dil) in enumerate(zip((l0, l1, l2), B_DILS))]
            ovs = [_read_group(ref, dil, h, scr.at[3 + g]) for g, (ref, dil) in enumerate(zip((o0, o1, o2), B_DILS))]
            wts = _group_weights(ls)
            out_ref[:, h * HEAD:(h + 1) * HEAD] = wts[0] * ovs[0] + wts[1] * ovs[1] + wts[2] * ovs[2]

    specs = [_class_spec(tm, dil, wd) for dil in B_DILS]
    return _pallas(body, name=name, grid=(n // tm,), in_specs=specs * 2, out_specs=pl.BlockSpec((tm, wd), lambda i: (i, 0)),
                   out_shape=_sds((n, wd), F32), scratch_shapes=[pltpu.VMEM((6, tm, HEAD), F32)],
                   compiler_params=_params(("parallel",)))(*os_, *lses)


def b_combine_bwd(dout, os_, lses, name):
    n, wd = dout.shape
    tm = _tile(n, 256, 8)

    def body(d_ref, o0, o1, o2, l0, l1, l2, do0, do1, do2, dl0, dl1, dl2, scr):
        for h in range(wd // HEAD):
            hs = slice(h * HEAD, (h + 1) * HEAD)
            dv = d_ref[:, hs]
            ls = [_read_group(ref, dil, h, scr.at[g]) for g, (ref, dil) in enumerate(zip((l0, l1, l2), B_DILS))]
            ovs = [_read_group(ref, dil, h, scr.at[3 + g]) for g, (ref, dil) in enumerate(zip((o0, o1, o2), B_DILS))]
            wts = _group_weights(ls)
            dws = [jnp.broadcast_to(jnp.sum(dv * ov, axis=-1, keepdims=True), (tm, HEAD)) for ov in ovs]
            mean = wts[0] * dws[0] + wts[1] * dws[1] + wts[2] * dws[2]
            for g, (wt, dw, do_ref, dl_ref, dil) in enumerate(zip(wts, dws, (do0, do1, do2), (dl0, dl1, dl2), B_DILS)):
                if dil == 1:
                    do_ref[:, hs] = wt * dv
                    dl_ref[:, hs] = wt * (dw - mean)
                else:
                    for r, part in enumerate(_to_classes(wt * dv, scr.at[g], dil)):
                        do_ref[r, :, hs] = part
                    for r, part in enumerate(_to_classes(wt * (dw - mean), scr.at[3 + g], dil)):
                        dl_ref[r, :, hs] = part

    specs = [_class_spec(tm, dil, wd) for dil in B_DILS]
    outs = _pallas(body, name=name, grid=(n // tm,), in_specs=[pl.BlockSpec((tm, wd), lambda i: (i, 0))] + specs * 2,
                   out_specs=specs * 2, out_shape=[_sds(_class_shape(n, dil, wd), F32) for dil in B_DILS] * 2,
                   scratch_shapes=[pltpu.VMEM((6, tm, HEAD), F32)],
                   compiler_params=_params(("parallel",)))(dout, *os_, *lses)
    return outs[:3], outs[3:]


def _c_rows(n):
    rows = n // GRID_W
    return rows, min(C_WIN_ROWS, rows)


def _c_row_start(r, rows, wr):
    return jnp.clip(r - wr // 2, 0, rows - wr)


def _c_bias_index(r, rows, wr):
    return _c_row_start(r, rows, wr) - r + (C_WIN_ROWS - 1)


def _col_shift_select(tile, cq, inverse):
    lanes = tile.shape[1]
    for b in range(6):
        amt = (lanes - (1 << b)) if inverse else (1 << b)
        tile = jnp.where(((cq >> b) & 1) == 1, pltpu.roll(tile, amt, 1), tile)
    return tile


def rpb_expand(rwin, name):
    lanes = rwin.shape[-1]

    def body(r_ref, b_ref):
        cq = lax.broadcasted_iota(jnp.int32, (GRID_W, lanes), 0)
        ck = lax.broadcasted_iota(jnp.int32, (GRID_W, lanes), 1) % GRID_W
        cs = jnp.clip(cq - C_WIN_COLS // 2, 0, GRID_W - C_WIN_COLS)
        ok = (ck >= cs) & (ck < cs + C_WIN_COLS)
        for i0 in range(C_WIN_ROWS):
            tile = jnp.broadcast_to(r_ref[i0], (GRID_W, lanes))
            tile = pltpu.roll(tile, lanes - (C_WIN_COLS - 1), 1)
            tile = _col_shift_select(tile, cq, False)
            b_ref[i0] = jnp.where(ok, tile, NEG)

    return _pallas(body, name=name, grid=(C_HEADS,),
                   in_specs=[pl.BlockSpec((None, C_WIN_ROWS, 1, lanes), lambda h: (h, 0, 0, 0))],
                   out_specs=pl.BlockSpec((None, C_WIN_ROWS, GRID_W, lanes), lambda h: (h, 0, 0, 0)),
                   out_shape=_sds((C_HEADS, C_WIN_ROWS, GRID_W, lanes), F32),
                   compiler_params=_params(("parallel",)))(rwin)


def rpb_reduce(dbias, name):
    lanes = dbias.shape[-1]
    wr = lanes // GRID_W

    def body(d_ref, o_ref):
        cq = lax.broadcasted_iota(jnp.int32, (GRID_W, lanes), 0)
        o_ref[...] = jnp.zeros_like(o_ref)
        for i0 in range(C_WIN_ROWS):
            tile = _col_shift_select(d_ref[i0], cq, True)
            tile = pltpu.roll(tile, C_WIN_COLS - 1, 1)
            vec = jnp.sum(tile, axis=0, keepdims=True)
            for w in range(wr):
                o_ref[i0 + w:i0 + w + 1, :] += vec[:, w * GRID_W:(w + 1) * GRID_W]

    return _pallas(body, name=name, grid=(C_HEADS,),
                   in_specs=[pl.BlockSpec((None, C_WIN_ROWS, GRID_W, lanes), lambda h: (h, 0, 0, 0))],
                   out_specs=pl.BlockSpec((None, 16, GRID_W), lambda h: (h, 0, 0)),
                   out_shape=_sds((C_HEADS, 16, GRID_W), F32),
                   compiler_params=_params(("parallel",)))(dbias)


def _store_or_add(ref, val, first):
    @pl.when(first)
    def _():
        ref[...] = val

    @pl.when(jnp.logical_not(first))
    def _():
        ref[...] += val


def c_attn_fwd(q, k, v, bias, name, nh=C_HEADS):
    n = q.shape[0]
    rows, wr = _c_rows(n)
    wk = wr * GRID_W

    def body(q_ref, k_ref, v_ref, b_ref, o_ref, lse_ref):
        r = pl.program_id(1)
        start = pl.multiple_of(_c_row_start(r, rows, wr) * GRID_W, GRID_W)
        sls = [slice(h * HEAD, (h + 1) * HEAD) for h in range(nh)]
        ss = [lax.dot_general(q_ref[:, sl], k_ref[pl.ds(start, wk), sl], _DIMS["nt"], preferred_element_type=F32)
              * SCALE + b_ref[h] for h, sl in enumerate(sls)]
        mxs = [jnp.max(s, axis=-1, keepdims=True) for s in ss]
        ps = [jnp.exp(s - mx) for s, mx in zip(ss, mxs)]
        dens = [jnp.sum(p, axis=-1, keepdims=True) for p in ps]
        outs = [jnp.dot((p / den).astype(CDT), v_ref[pl.ds(start, wk), sl], preferred_element_type=F32)
                for p, den, sl in zip(ps, dens, sls)]
        for sl, o, mx, den in zip(sls, outs, mxs, dens):
            o_ref[:, sl] = o
            lse_ref[:, sl] = jnp.broadcast_to(mx + jnp.log(den), (GRID_W, HEAD))

    qspec = pl.BlockSpec((GRID_W, nh * HEAD), lambda h, r: (r, h))
    kspec = pl.BlockSpec((n, nh * HEAD), lambda h, r: (0, h), pipeline_mode=pl.Buffered(1))
    bspec = pl.BlockSpec((nh, None, GRID_W, wk), lambda h, r: (h, _c_bias_index(r, rows, wr), 0, 0))
    return _pallas(body, name=name, grid=(C_HEADS // nh, rows), in_specs=[qspec, kspec, kspec, bspec],
                   out_specs=[qspec, qspec], out_shape=[_sds(q.shape, F32), _sds(q.shape, F32)],
                   compiler_params=_params(("parallel", "arbitrary")))(q, k, v, bias)


def c_attn_bwd(q, k, v, bias, o, lse, do, name, nh=4):
    n = q.shape[0]
    rows, wr = _c_rows(n)
    wk = wr * GRID_W

    def body(q_ref, k_ref, v_ref, b_ref, o_ref, lse_ref, do_ref, dq_ref, dk_ref, dv_ref, db_ref):
        r = pl.program_id(1)
        rs = _c_row_start(r, rows, wr)
        start = pl.multiple_of(rs * GRID_W, GRID_W)

        @pl.when(r == 0)
        def _():
            dk_ref[...] = jnp.zeros_like(dk_ref)
            dv_ref[...] = jnp.zeros_like(dv_ref)

        prev = _c_row_start(jnp.maximum(r - 1, 0), rows, wr) - jnp.maximum(r - 1, 0)
        first = (r == 0) | (prev != rs - r)
        sls = [slice(h * HEAD, (h + 1) * HEAD) for h in range(nh)]
        k_ts = [k_ref[pl.ds(start, wk), sl] for sl in sls]
        v_ts = [v_ref[pl.ds(start, wk), sl] for sl in sls]
        q_ts = [q_ref[:, sl] for sl in sls]
        lse_ts = [lse_ref[:, sl][:, :1] for sl in sls]
        do_ts = [do_ref[:, sl] for sl in sls]
        deltas = [jnp.sum(do_t * o_ref[:, sl], axis=-1, keepdims=True) for do_t, sl in zip(do_ts, sls)]
        biases = [b_ref[h] for h in range(nh)]
        dk_old = [dk_ref[pl.ds(start, wk), sl] for sl in sls]
        dv_old = [dv_ref[pl.ds(start, wk), sl] for sl in sls]

        ps = [jnp.exp(lax.dot_general(q_t, k_t, _DIMS["nt"], preferred_element_type=F32) * SCALE + b - lse_t)
              for q_t, k_t, b, lse_t in zip(q_ts, k_ts, biases, lse_ts)]
        do_cs = [do_t.astype(CDT) for do_t in do_ts]
        dps = [lax.dot_general(do_c, v_t, _DIMS["nt"], preferred_element_type=F32) for do_c, v_t in zip(do_cs, v_ts)]
        dss = [p * (dp - delta) for p, dp, delta in zip(ps, dps, deltas)]
        ds_cs = [(ds * SCALE).astype(CDT) for ds in dss]
        dqs = [jnp.dot(ds_c, k_t, preferred_element_type=F32) for ds_c, k_t in zip(ds_cs, k_ts)]
        dv_new = [old + lax.dot_general(p.astype(CDT), do_c, _DIMS["tn"], preferred_element_type=F32)
                  for old, p, do_c in zip(dv_old, ps, do_cs)]
        dk_new = [old + lax.dot_general(ds_c, q_t, _DIMS["tn"], preferred_element_type=F32)
                  for old, ds_c, q_t in zip(dk_old, ds_cs, q_ts)]

        for h, sl in enumerate(sls):
            dq_ref[:, sl] = dqs[h]
            dk_ref[pl.ds(start, wk), sl] = dk_new[h]
            dv_ref[pl.ds(start, wk), sl] = dv_new[h]
        for h in range(nh):
            _store_or_add(db_ref.at[h], dss[h], first)

    qspec = pl.BlockSpec((GRID_W, nh * HEAD), lambda h, r: (r, h))
    kspec = pl.BlockSpec((n, nh * HEAD), lambda h, r: (0, h))
    kin = pl.BlockSpec((n, nh * HEAD), lambda h, r: (0, h), pipeline_mode=pl.Buffered(1))
    bspec = pl.BlockSpec((nh, None, GRID_W, wk), lambda h, r: (h, _c_bias_index(r, rows, wr), 0, 0))
    return _pallas(body, name=name, grid=(C_HEADS // nh, rows),
                   in_specs=[qspec, kin, kin, bspec, qspec, qspec, qspec],
                   out_specs=[qspec, kspec, kspec, bspec],
                   out_shape=[_sds(q.shape, F32), _sds(k.shape, F32), _sds(k.shape, F32), _sds(bias.shape, F32)],
                   compiler_params=_params(("parallel", "arbitrary")))(q, k, v, bias, o, lse, do)


def _sigmoid(z):
    return 1.0 / (1.0 + jnp.exp(-z))


def _gate_specs(n, d):
    tm, tn = _tile(n, 256, 8), _tile(math.gcd(d, QKV_W), 1024)
    nj = d // tn
    tile = pl.BlockSpec((tm, tn), lambda i, j: (i, j))
    gl = [pl.BlockSpec((tm, tn), functools.partial(lambda i, j, b: (i, (QKV_W + b * d) // tn + j), b=b)) for b in range(3)]
    return tm, tn, nj, tile, gl


def gate_merge(proj, ys, name):
    n, d = ys[0].shape
    tm, tn, nj, tile, gl = _gate_specs(n, d)

    def body(g0, g1, g2, y0, y1, y2, out_ref):
        acc = (_sigmoid(g0[...].astype(F32)) * y0[...] + _sigmoid(g1[...].astype(F32)) * y1[...]
               + _sigmoid(g2[...].astype(F32)) * y2[...])
        out_ref[...] = acc.astype(out_ref.dtype)

    return _pallas(body, name=name, grid=(n // tm, nj), in_specs=gl + [tile] * 3, out_specs=tile,
                   out_shape=_sds((n, d), CDT), compiler_params=_params(("parallel", "parallel")))(proj, proj, proj, *ys)


def gate_bwd(proj, ys, dmerged, name):
    n, d = dmerged.shape
    tm, tn, nj, _, _ = _gate_specs(n, d)

    def body(g_ref, y0, y1, y2, dm_ref, dy_ref, dp_ref):
        b = pl.program_id(2)
        y = jnp.where(b == 0, y0[...], jnp.where(b == 1, y1[...], y2[...]))
        dm = dm_ref[...]
        sg = _sigmoid(g_ref[...].astype(F32))
        dy_ref[...] = (dm * sg).astype(dy_ref.dtype)
        dp_ref[...] = (dm * y * sg * (1.0 - sg)).astype(dp_ref.dtype)

    gl = pl.BlockSpec((tm, tn), lambda i, j, b: (i, QKV_W // tn + b * nj + j))
    tile = pl.BlockSpec((tm, tn), lambda i, j, b: (i, j))
    return _pallas(body, name=name, grid=(n // tm, nj, 3), in_specs=[gl, tile, tile, tile, tile],
                   out_specs=[pl.BlockSpec((None, tm, tn), lambda i, j, b: (b, i, j)), gl],
                   out_shape=[_sds((3, n, d), CDT), _sds(proj.shape, CDT)],
                   compiler_params=_params(("parallel", "parallel", "arbitrary")))(proj, *ys, dmerged)


def gate_up_swiglu(h2, wg, name, after=None):
    n, d = h2.shape
    ns = wg.shape[2]
    ff = 2 * ns
    tm, tn = _tile(n, 512, 8), _tile(ns, 1408)
    nj = ns // tn
    extra = [] if after is None else [after]

    def body(a_ref, bg_ref, bu_ref, *refs):
        gu_ref, act_ref = refs[len(extra):]
        a = a_ref[...].astype(CDT)
        gt = jnp.dot(a, bg_ref[...].astype(CDT), preferred_element_type=F32)
        up = jnp.dot(a, bu_ref[...].astype(CDT), preferred_element_type=F32)
        gu_ref[0] = gt.astype(gu_ref.dtype)
        gu_ref[1] = up.astype(gu_ref.dtype)
        act_ref[...] = (gt * _sigmoid(gt) * up).astype(act_ref.dtype)

    return _pallas(body, name=name, grid=(n // tm, 2 * nj),
                   in_specs=[pl.BlockSpec((tm, d), lambda i, j: (i, 0)),
                             pl.BlockSpec((None, d, tn), lambda i, j: (j // nj, 0, j % nj)),
                             pl.BlockSpec((None, d, tn), lambda i, j: (2 + j // nj, 0, j % nj))]
                   + [pl.BlockSpec(a.shape, lambda i, j: (0, 0)) for a in extra],
                   out_specs=[pl.BlockSpec((2, tm, tn), lambda i, j: (0, i, j)), pl.BlockSpec((tm, tn), lambda i, j: (i, j))],
                   out_shape=[_sds((2, n, ff), ADT), _sds((n, ff), CDT)],
                   compiler_params=_params(("parallel", "parallel")))(h2, wg, wg, *extra)


def d_gate_up(dx, w_down, gu, name):
    n, d = dx.shape
    ff = w_down.shape[0]
    tm, tn = _tile(n, 1024, 8), _tile(ff, 512)

    def body(a_ref, b_ref, gu_ref, d_ref):
        da = lax.dot_general(a_ref[...].astype(CDT), b_ref[...].astype(CDT), _DIMS["nt"], preferred_element_type=F32)
        gt, up = gu_ref[0].astype(F32), gu_ref[1].astype(F32)
        sg = _sigmoid(gt)
        d_ref[0] = (da * up * (sg + gt * sg * (1.0 - sg))).astype(d_ref.dtype)
        d_ref[1] = (da * gt * sg).astype(d_ref.dtype)

    blk = pl.BlockSpec((2, tm, tn), lambda i, j: (0, i, j))
    return _pallas(body, name=name, grid=(n // tm, ff // tn),
                   in_specs=[pl.BlockSpec((tm, d), lambda i, j: (i, 0)), pl.BlockSpec((tn, d), lambda i, j: (j, 0)), blk],
                   out_specs=blk, out_shape=_sds((2, n, ff), CDT),
                   compiler_params=_params(("parallel", "parallel")))(dx, w_down, gu)


def loss_head(y, target, name):
    n, d = y.shape
    tm = _tile(n, 512, 8)
    nsteps = n // tm

    def body(y_ref, t_ref, l_ref, dy_ref, dyc_ref, acc_ref):
        i = pl.program_id(0)
        e = y_ref[...] - t_ref[...]
        dy_ref[...] = e * (1.0 / d)
        dyc_ref[...] = (e * (1.0 / d)).astype(dyc_ref.dtype)
        part = jnp.sum((e * e).reshape(tm // 8, 8, d), axis=0)

        @pl.when(i == 0)
        def _():
            acc_ref[...] = part

        @pl.when(i > 0)
        def _():
            acc_ref[...] += part

        @pl.when(i == nsteps - 1)
        def _():
            tot = jnp.sum(jnp.sum(acc_ref[...], axis=1, keepdims=True), axis=0, keepdims=True) * (0.5 / d)
            l_ref[...] = jnp.broadcast_to(tot, (8, HEAD))

    row = pl.BlockSpec((tm, d), lambda i: (i, 0))
    return _pallas(body, name=name, grid=(nsteps,), in_specs=[row, row],
                   out_specs=[pl.BlockSpec((8, HEAD), lambda i: (0, 0)), row, row],
                   out_shape=[_sds((8, HEAD), F32), _sds((n, d), F32), _sds((n, d), CDT)],
                   scratch_shapes=[pltpu.VMEM((8, d), F32)],
                   compiler_params=_params(("arbitrary",)))(y, target)


def adamw(w, g, m, v, name):
    r, c = w.shape
    tr = _tile(r, max(8, (1 << 19) // c), 8)
    c1 = 1.0 - ADAM_B1 ** ADAM_STEP
    c2 = 1.0 - ADAM_B2 ** ADAM_STEP

    def body(w_ref, g_ref, m_ref, v_ref, d_ref, mo_ref, vo_ref):
        gv = g_ref[...]
        mn = ADAM_B1 * m_ref[...] + (1.0 - ADAM_B1) * gv
        vn = ADAM_B2 * v_ref[...] + (1.0 - ADAM_B2) * (gv * gv)
        d_ref[...] = -ADAM_LR * ((mn / c1) / (jnp.sqrt(vn / c2) + ADAM_EPS) + ADAM_WD * w_ref[...])
        mo_ref[...] = mn
        vo_ref[...] = vn

    row = pl.BlockSpec((tr, c), lambda i: (i, 0))
    return _pallas(body, name=name, grid=(r // tr,), in_specs=[row] * 4, out_specs=[row] * 3,
                   out_shape=[_sds((r, c), F32)] * 3, compiler_params=_params(("parallel",)))(w, g, m, v)


ANY = pl.BlockSpec(memory_space=pl.ANY)


def _place():
    x, y, c = lax.axis_index("x"), lax.axis_index("y"), lax.axis_index("c")
    return x, y, c, [(1 - x, y), (x, 1 - y), (1 - x, 1 - y)]


def _rcopy(src, dst, send_sems, recv_sems, k, to):
    return pltpu.make_async_remote_copy(src_ref=src, dst_ref=dst, send_sem=send_sems.at[k], recv_sem=recv_sems.at[k],
                                        device_id=to, device_id_type=MESH)


def cast_place(chip_idx, shards, layer, name):
    _, k, ns = shards.shape
    tr = _tile(k, max(16, (1 << 19) // ns), 16)

    def body(k_ref, s_ref, o_ref):
        o_ref[...] = s_ref[...].astype(o_ref.dtype)

    gs = pltpu.PrefetchScalarGridSpec(
        num_scalar_prefetch=1, grid=(k // tr,),
        in_specs=[pl.BlockSpec((None, tr, ns), lambda i, k_ref: (layer, i, 0))],
        out_specs=pl.BlockSpec((None, tr, ns), lambda i, k_ref: (k_ref[0], i, 0)))
    return _pallas(body, name=name, grid_spec=gs, out_shape=_sds((N_CHIPS, k, ns), CDT),
                   compiler_params=_params(("parallel",)))(chip_idx, shards)


HBM = pl.BlockSpec(memory_space=pltpu.HBM)
SEM = pl.BlockSpec(memory_space=pltpu.SEMAPHORE)
EFFECT = pltpu.SideEffectType.DATAFLOW_SIDE_EFFECTING


def _in_hbm(a):
    return pltpu.with_memory_space_constraint(a, pltpu.HBM)


def _gather_copies(refs, send_sems, recv_sems):
    x, y, c, chips = _place()
    me = 2 * x + y
    out = []
    for t, ref in enumerate(refs):
        kh = ref.shape[1] // 2
        for j, (px, py) in enumerate(chips):
            send = _rcopy(ref.at[me, pl.ds(c * kh, kh)], ref.at[me, pl.ds(c * kh, kh)], send_sems, recv_sems,
                          3 * t + j, (px, py, c))
            land = ref.at[2 * px + py, pl.ds(c * kh, kh)]
            out.append((send, _rcopy(land, land, send_sems, recv_sems, 3 * t + j, (px, py, c))))
    return out


def gather_start(bufs, after, name):
    nt = len(bufs)

    def body(*refs):
        ins, send_sems, recv_sems, token = refs[:nt], refs[nt + 1], refs[nt + 2], refs[-1]
        for send, _ in _gather_copies(ins, send_sems, recv_sems):
            send.start()
        token[...] = jnp.zeros_like(token)

    outs = _pallas(body, name=name, in_specs=[HBM] * nt + [ANY],
                   out_specs=(SEM, SEM) + (HBM,) * nt + (pl.BlockSpec(memory_space=pltpu.VMEM),),
                   out_shape=(pltpu.SemaphoreType.DMA((3 * nt,)), pltpu.SemaphoreType.DMA((3 * nt,)))
                   + tuple(pltpu.HBM(b.shape, b.dtype) for b in bufs) + (_sds((8, HEAD), F32),),
                   input_output_aliases={t: 2 + t for t in range(nt)},
                   compiler_params=pltpu.CompilerParams(has_side_effects=EFFECT))(*[_in_hbm(b) for b in bufs], after)
    return outs[0], outs[1], list(outs[2:2 + nt]), outs[-1]


def gather_wait(bufs, send_sems, recv_sems, after, name):
    nt = len(bufs)

    def body(*refs):
        ins, s_sems, r_sems = refs[:nt], refs[nt], refs[nt + 1]
        for send, land in _gather_copies(ins, s_sems, r_sems):
            send.wait_send()
            land.wait_recv()

    return _pallas(body, name=name, in_specs=[HBM] * nt + [SEM, SEM, ANY], out_specs=[HBM] * nt,
                   out_shape=[pltpu.HBM(b.shape, b.dtype) for b in bufs],
                   input_output_aliases={t: t for t in range(nt)},
                   compiler_params=pltpu.CompilerParams(has_side_effects=EFFECT))(*bufs, send_sems, recv_sems, after)


def _forward_copies(refs, send_sems, recv_sems):
    x, y, c, chips = _place()
    out = []
    for t, ref in enumerate(refs):
        kh = ref.shape[1] // 2
        for j, (px, py) in enumerate(chips):
            mine = ref.at[2 * px + py, pl.ds(c * kh, kh)]
            land = ref.at[2 * px + py, pl.ds((1 - c) * kh, kh)]
            out.append((_rcopy(mine, mine, send_sems, recv_sems, 3 * t + j, (x, y, 1 - c)),
                        _rcopy(land, land, send_sems, recv_sems, 3 * t + j, (x, y, 1 - c))))
    return out


def forward_start(bufs, after, name):
    nt = len(bufs)

    def body(*refs):
        ins, send_sems, recv_sems, token = refs[:nt], refs[nt + 1], refs[nt + 2], refs[-1]
        for send, _ in _forward_copies(ins, send_sems, recv_sems):
            send.start()
        token[...] = jnp.zeros_like(token)

    outs = _pallas(body, name=name, in_specs=[HBM] * nt + [ANY],
                   out_specs=(SEM, SEM) + (HBM,) * nt + (pl.BlockSpec(memory_space=pltpu.VMEM),),
                   out_shape=(pltpu.SemaphoreType.DMA((3 * nt,)), pltpu.SemaphoreType.DMA((3 * nt,)))
                   + tuple(pltpu.HBM(b.shape, b.dtype) for b in bufs) + (_sds((8, HEAD), F32),),
                   input_output_aliases={t: 2 + t for t in range(nt)},
                   compiler_params=pltpu.CompilerParams(has_side_effects=EFFECT))(*[_in_hbm(b) for b in bufs], after)
    return outs[0], outs[1], list(outs[2:2 + nt]), outs[-1]


def forward_wait(bufs, send_sems, recv_sems, after, name):
    nt = len(bufs)

    def body(*refs):
        ins, s_sems, r_sems = refs[:nt], refs[nt], refs[nt + 1]
        for send, land in _forward_copies(ins, s_sems, r_sems):
            send.wait_send()
            land.wait_recv()

    return _pallas(body, name=name, in_specs=[HBM] * nt + [SEM, SEM, ANY], out_specs=[HBM] * nt,
                   out_shape=[pltpu.HBM(b.shape, b.dtype) for b in bufs],
                   input_output_aliases={t: t for t in range(nt)},
                   compiler_params=pltpu.CompilerParams(has_side_effects=EFFECT))(*bufs, send_sems, recv_sems, after)


def pair_forward(bufs, name):
    nt = len(bufs)

    def body(*refs):
        outs = refs[nt:2 * nt]
        send_sems, recv_sems = refs[2 * nt:]
        x, y, c, chips = _place()
        cps = []
        for t in range(nt):
            kh = outs[t].shape[1] // 2
            for j, (px, py) in enumerate(chips):
                blk = outs[t].at[2 * px + py, pl.ds(c * kh, kh)]
                cps.append(_rcopy(blk, blk, send_sems, recv_sems, 3 * t + j, (x, y, 1 - c)))
                cps[-1].start()
        for t in range(nt):
            kh = outs[t].shape[1] // 2
            for j, (px, py) in enumerate(chips):
                blk = outs[t].at[2 * px + py, pl.ds((1 - c) * kh, kh)]
                _rcopy(blk, blk, send_sems, recv_sems, 3 * t + j, (x, y, 1 - c)).wait_recv()
        for cp in cps:
            cp.wait_send()

    return _pallas(body, name=name, in_specs=[ANY] * nt, out_specs=[ANY] * nt,
                   out_shape=[_sds(b.shape, b.dtype) for b in bufs],
                   input_output_aliases={t: t for t in range(nt)},
                   scratch_shapes=[pltpu.SemaphoreType.DMA((3 * nt,)), pltpu.SemaphoreType.DMA((3 * nt,))],
                   compiler_params=pltpu.CompilerParams(has_side_effects=True))(*bufs)


def pair_exchange(grads, name):
    nt = len(grads)

    def body(*refs):
        ins, outs = refs[:nt], refs[nt:2 * nt]
        send_sems, recv_sems = refs[2 * nt:]
        x, y, c, _ = _place()
        sibling = (x, y, 1 - c)
        cps = []
        for t in range(nt):
            kh = ins[t].shape[1] // 2
            cps.append(_rcopy(ins[t].at[:, pl.ds((1 - c) * kh, kh), :], outs[t], send_sems, recv_sems, t, sibling))
            cps[-1].start()
        for cp in cps:
            cp.wait_recv()
        for cp in cps:
            cp.wait_send()

    return _pallas(body, name=name, in_specs=[ANY] * nt, out_specs=[ANY] * nt,
                   out_shape=[_sds((N_CHIPS, g.shape[1] // 2, g.shape[2]), g.dtype) for g in grads],
                   scratch_shapes=[pltpu.SemaphoreType.DMA((nt,)), pltpu.SemaphoreType.DMA((nt,))],
                   compiler_params=pltpu.CompilerParams(has_side_effects=True))(*grads)


def _exchange_copies(sums, lands, send_sems, recv_sems):
    x, y, c, chips = _place()
    return [_rcopy(s.at[2 * px + py], l.at[j], send_sems, recv_sems, 3 * t + j, (px, py, c))
            for t, (s, l) in enumerate(zip(sums, lands)) for j, (px, py) in enumerate(chips)]


def exchange_start(sums, name):
    nt = len(sums)
    lands = [lax.empty((3,) + s.shape[1:], s.dtype) for s in sums]

    def body(*refs):
        ins, zones, send_sems, recv_sems, token = refs[:nt], refs[nt:2 * nt], refs[2 * nt], refs[2 * nt + 1], refs[-1]
        for cp in _exchange_copies(ins, zones, send_sems, recv_sems):
            cp.start()
        token[...] = jnp.zeros_like(token)

    outs = _pallas(body, name=name, in_specs=[HBM] * (2 * nt),
                   out_specs=(SEM, SEM) + (HBM,) * (2 * nt) + (pl.BlockSpec(memory_space=pltpu.VMEM),),
                   out_shape=(pltpu.SemaphoreType.DMA((3 * nt,)), pltpu.SemaphoreType.DMA((3 * nt,)))
                   + tuple(pltpu.HBM(a.shape, a.dtype) for a in list(sums) + lands) + (_sds((8, HEAD), F32),),
                   input_output_aliases={t: 2 + t for t in range(2 * nt)},
                   compiler_params=pltpu.CompilerParams(has_side_effects=EFFECT))(*[_in_hbm(a) for a in list(sums) + lands])
    return outs[0], outs[1], list(outs[2:2 + nt]), list(outs[2 + nt:2 + 2 * nt]), outs[-1]


def exchange_wait(sums, lands, send_sems, recv_sems, after, name):
    nt = len(sums)

    def body(*refs):
        ins, zones, s_sems, r_sems = refs[:nt], refs[nt:2 * nt], refs[2 * nt], refs[2 * nt + 1]
        for cp in _exchange_copies(ins, zones, s_sems, r_sems):
            cp.wait_send()
            cp.wait_recv()

    outs = _pallas(body, name=name, in_specs=[HBM] * (2 * nt) + [SEM, SEM, ANY], out_specs=[HBM] * (2 * nt),
                   out_shape=[pltpu.HBM(a.shape, a.dtype) for a in list(sums) + list(lands)],
                   input_output_aliases={t: t for t in range(2 * nt)},
                   compiler_params=pltpu.CompilerParams(has_side_effects=EFFECT))(*sums, *lands, send_sems, recv_sems, after)
    return list(outs[:nt]), list(outs[nt:])


def pair_share(halves, name):
    nt = len(halves)

    def body(*refs):
        ins, outs = refs[:nt], refs[nt:2 * nt]
        send_sems, recv_sems = refs[2 * nt:]
        x, y, c, _ = _place()
        cps = []
        for t in range(nt):
            cps.append(_rcopy(ins[t], outs[t], send_sems, recv_sems, t, (x, y, 1 - c)))
            cps[-1].start()
        for cp in cps:
            cp.wait_recv()
        for cp in cps:
            cp.wait_send()

    return _pallas(body, name=name, in_specs=[ANY] * nt, out_specs=[ANY] * nt,
                   out_shape=[_sds(h.shape, h.dtype) for h in halves],
                   scratch_shapes=[pltpu.SemaphoreType.DMA((nt,)), pltpu.SemaphoreType.DMA((nt,))],
                   compiler_params=pltpu.CompilerParams(has_side_effects=True))(*halves)


def small_allreduce(pack, name):
    r = pack.shape[0]

    def body(in_ref, out_ref, buf, send_sems, recv_sems):
        x, y, c, _ = _place()
        me = 4 * x + 2 * y + c
        sends = []
        for k in range(1, 8):
            to = ((x + ((k >> 2) & 1)) % 2, (y + ((k >> 1) & 1)) % 2, (c + (k & 1)) % 2)
            cp = _rcopy(in_ref, buf.at[me], send_sems, recv_sems, k - 1, to)
            cp.start()
            sends.append((cp, to))
        buf[pl.ds(me, 1)] = in_ref[...][None]
        for k, (_, to) in enumerate(sends):
            peer = 4 * to[0] + 2 * to[1] + to[2]
            _rcopy(in_ref, buf.at[peer], send_sems, recv_sems, k, to).wait_recv()
        for cp, _ in sends:
            cp.wait_send()
        acc = buf[0]
        for d in range(1, 8):
            acc = acc + buf[d]
        out_ref[...] = acc

    vm = pl.BlockSpec(memory_space=pltpu.VMEM)
    return _pallas(body, name=name, in_specs=[vm], out_specs=vm, out_shape=_sds((r, HEAD), F32),
                   scratch_shapes=[pltpu.VMEM((8, r, HEAD), F32), pltpu.SemaphoreType.DMA((7,)),
                                   pltpu.SemaphoreType.DMA((7,))],
                   compiler_params=pltpu.CompilerParams(has_side_effects=True))(pack)


def add_halves(c_idx, grad, other, name):
    _, k, ns = grad.shape
    kh = k // 2
    tr = _tile(kh, max(16, (1 << 19) // ns), 16)
    nr = kh // tr

    def body(c_ref, g_ref, o_ref, s_ref):
        s_ref[...] = (g_ref[...].astype(F32) + o_ref[...].astype(F32)).astype(s_ref.dtype)

    gs = pltpu.PrefetchScalarGridSpec(
        num_scalar_prefetch=1, grid=(N_CHIPS, nr),
        in_specs=[pl.BlockSpec((None, tr, ns), lambda g, i, c_ref: (g, c_ref[0] * nr + i, 0)),
                  pl.BlockSpec((None, tr, ns), lambda g, i, c_ref: (g, i, 0))],
        out_specs=pl.BlockSpec((None, tr, ns), lambda g, i, c_ref: (g, i, 0)))
    return _pallas(body, name=name, grid_spec=gs, out_shape=_sds((N_CHIPS, kh, ns), XDT),
                   compiler_params=_params(("parallel", "parallel")))(c_idx, grad, other)


def add_chips(chip_idx, sums, recv, stack, layer, n_layers, name):
    _, kh, ns = sums.shape
    tr = _tile(kh, max(16, (1 << 19) // ns), 16)
    has_stack = stack is not None

    def body(k_ref, s_ref, r0, r1, r2, *rest):
        o_ref = rest[-1]
        o_ref[...] = ((s_ref[...].astype(F32) + r0[...].astype(F32)) + r1[...].astype(F32)) + r2[...].astype(F32)

    rspec = [pl.BlockSpec((None, tr, ns), functools.partial(lambda i, k_ref, j: (j, i, 0), j=j)) for j in range(3)]
    gs = pltpu.PrefetchScalarGridSpec(
        num_scalar_prefetch=1, grid=(kh // tr,),
        in_specs=[pl.BlockSpec((None, tr, ns), lambda i, k_ref: (k_ref[0], i, 0))] + rspec + ([ANY] if has_stack else []),
        out_specs=pl.BlockSpec((None, tr, ns), lambda i, k_ref: (layer, i, 0)))
    args = (chip_idx, sums, recv, recv, recv) + ((stack,) if has_stack else ())
    return _pallas(body, name=name, grid_spec=gs, out_shape=_sds((n_layers, kh, ns), F32),
                   input_output_aliases={5: 0} if has_stack else {},
                   compiler_params=_params(("parallel",)))(*args)


def adamw_big(c_idx, w, m, v, mine, other, name):
    nl, k, ns = w.shape
    kh = k // 2
    tr = _tile(kh, max(8, (1 << 18) // ns), 8)
    nr = kh // tr
    c1 = 1.0 - ADAM_B1 ** ADAM_STEP
    c2 = 1.0 - ADAM_B2 ** ADAM_STEP

    def body(c_ref, w_ref, m_ref, v_ref, a_ref, b_ref, g_ref, d_ref, mo_ref, vo_ref):
        gv = jnp.where(pl.program_id(2) == c_ref[0], a_ref[...], b_ref[...])
        mn = ADAM_B1 * m_ref[...] + (1.0 - ADAM_B1) * gv
        vn = ADAM_B2 * v_ref[...] + (1.0 - ADAM_B2) * (gv * gv)
        g_ref[...] = gv
        d_ref[...] = -ADAM_LR * ((mn / c1) / (jnp.sqrt(vn / c2) + ADAM_EPS) + ADAM_WD * w_ref[...])
        mo_ref[...] = mn
        vo_ref[...] = vn

    full = pl.BlockSpec((None, tr, ns), lambda l, i, hh, c_ref: (l, hh * nr + i, 0))
    half = pl.BlockSpec((None, tr, ns), lambda l, i, hh, c_ref: (l, i, 0))
    gs = pltpu.PrefetchScalarGridSpec(num_scalar_prefetch=1, grid=(nl, nr, 2),
                                      in_specs=[full, full, full, half, half], out_specs=[full] * 4)
    return _pallas(body, name=name, grid_spec=gs, out_shape=[_sds(w.shape, F32)] * 4,
                   compiler_params=_params(("parallel", "parallel", "arbitrary")))(c_idx, w, m, v, mine, other)


W_NAMES = ("w_in", "w_br_a", "w_br_b", "w_br_c", "w_o", "w_gate_up", "w_down")


def _rope_tables(n):
    half = HEAD // 2
    inv_freq = ROPE_THETA ** (-jnp.arange(half, dtype=F32) * 2.0 / HEAD)
    ang = jnp.arange(n, dtype=F32)[:, None] * inv_freq[None, :]
    cos, sin = jnp.cos(ang), jnp.sin(ang)
    return jnp.concatenate([cos, cos], axis=-1), jnp.concatenate([-sin, sin], axis=-1)


def _rpb_windows(rpb):
    pad = jnp.pad(rpb, ((0, 0), (0, 1), (0, GRID_W - rpb.shape[2])))
    wins = [pad[:, i0:i0 + C_WIN_ROWS].reshape(C_HEADS, 1, C_WIN_ROWS * GRID_W) for i0 in range(C_WIN_ROWS)]
    return jnp.stack(wins, axis=1)


def _rows(t):
    return t.reshape(-1, t.shape[-1])


def _like(t, ref):
    return t.reshape(ref.shape)


def layer_fwd(x, p, w, cos2, sin2, rest=None, mid=None):
    n, d = x.shape
    s = {"x": x}
    s["h"] = rmsnorm_fwd(x, p["norm1_g"], "norm1")
    s["proj"] = mm_x_wcol(s["h"], w["w_in"], ADT, "proj")
    gains = jnp.pad(p["qk_norm_g"], ((0, 2), (0, 0)))
    pp = s["pp"] = qk_prep(s["proj"], gains, cos2, sin2, "qk_prep")
    sink = p["sink_a"].reshape(1, A_Q_HEADS)
    s["oa"], s["lse_a"] = band_attn_fwd(pp["qa"], pp["ka"], pp["va"], sink, seqs=A_KV_HEADS, G=A_GROUP,
                                        nh=A_KV_HEADS, radius=A_RADIUS, name="attn_a")
    s["ob"], s["lse_b"] = [], []
    for g, dil in enumerate(B_DILS):
        o, lse = band_attn_fwd(_rows(pp[f"qb{g}"]), _rows(pp[f"kb{g}"]), _rows(pp[f"vb{g}"]), None, seqs=B_HG, G=1,
                               nh=B_HG, radius=B_RADIUS, classes=dil, name=f"attn_b{g}")
        s["ob"].append(_like(o, pp[f"qb{g}"]))
        s["lse_b"].append(_like(lse, pp[f"qb{g}"]))
    ob = b_combine_fwd(s["ob"], s["lse_b"], "b_combine")
    s["bias"] = rpb_expand(_rpb_windows(p["rpb_c"]), "rpb_expand")
    s["oc"], s["lse_c"] = c_attn_fwd(pp["qc"], pp["kc"], pp["vc"], s["bias"], "attn_c")
    s["o_in"] = (s["oa"], ob, s["oc"])
    if rest is not None:
        w = {**w, **rest(s["oc"])}
    s["w"] = w
    s["ys"] = [mm_x_wcol(o, w[k], ADT, "branch_" + k[-1]) for o, k in zip(s["o_in"], ("w_br_a", "w_br_b", "w_br_c"))]
    s["merged"] = gate_merge(s["proj"], s["ys"], "gate_merge")
    s["x_mid"] = mm_x_w(s["merged"], w["w_o"], "out_proj", res=x)
    s["h2"] = rmsnorm_fwd(s["x_mid"], p["norm2_g"], "norm2")
    token = None if mid is None else mid(s["h2"])
    s["gu"], s["act"] = gate_up_swiglu(s["h2"], w["w_gate_up"], "gate_up", after=token)
    x_out = mm_x_w(s["act"], w["w_down"], "down", res=s["x_mid"], tk_pref=2816)
    return x_out, s


def layer_bwd(dx_out, s, p, cos2, sin2, on_dws, early=False):
    dx_out, dx_out_c = dx_out
    n, d = dx_out.shape
    pp, w = s["pp"], s["w"]
    dgu = d_gate_up(dx_out_c, w["w_down"], s["gu"], "d_gate_up")
    dw_down = mm_aT_d(s["act"], dx_out_c, "dw_down")
    dh2 = mm_x_wcolT(dgu, w["w_gate_up"], "d_h2", stacked_in=2)
    dw_gu = mm_aT_d_wcol(s["h2"], dgu, "dw_gate_up", tn_pref=1408, stacked_in=2)
    dx_mid, dx_mid_c, dg2 = rmsnorm_bwd(s["x_mid"], p["norm2_g"], dh2, dx_out, "norm2_bwd")
    dws = {"w_gate_up": dw_gu, "w_down": dw_down.reshape(N_CHIPS, dw_down.shape[0] // N_CHIPS, d)}
    token = on_dws(dws) if early else None

    dmerged = mm_x_wT(dx_mid_c, w["w_o"], "d_merged", after=token)
    dw_o = mm_aT_d(s["merged"], dx_mid_c, "dw_o")
    dys, dproj = gate_bwd(s["proj"], s["ys"], dmerged, "gate_bwd")
    dos, dw_br = [], []
    for b, (o, k) in enumerate(zip(s["o_in"], ("w_br_a", "w_br_b", "w_br_c"))):
        dos.append(mm_x_wcolT(dys, w[k], "d_o_" + k[-1], lead=b))
        dw_br.append(mm_aT_d_wcol(o, dys, "dw_br_" + k[-1], lead=b))

    grads = {}
    sink = p["sink_a"].reshape(1, A_Q_HEADS)
    grads["qa"], grads["ka"], grads["va"], dsink = band_attn_bwd(
        pp["qa"], pp["ka"], pp["va"], sink, s["oa"], s["lse_a"], dos[0], None,
        seqs=A_KV_HEADS, G=A_GROUP, nh=A_KV_HEADS, radius=A_RADIUS, name="attn_a_bwd")
    dobs, dlses = b_combine_bwd(dos[1], s["ob"], s["lse_b"], "b_combine_bwd")
    for g, dil in enumerate(B_DILS):
        dq, dk, dv = band_attn_bwd(_rows(pp[f"qb{g}"]), _rows(pp[f"kb{g}"]), _rows(pp[f"vb{g}"]), None,
                                   _rows(s["ob"][g]), _rows(s["lse_b"][g]), _rows(dobs[g]), _rows(dlses[g]),
                                   seqs=B_HG, G=1, nh=B_HG, radius=B_RADIUS, classes=dil, name=f"attn_b{g}_bwd")
        grads[f"qb{g}"], grads[f"kb{g}"], grads[f"vb{g}"] = [_like(t, pp[f"qb{g}"]) for t in (dq, dk, dv)]
    grads["qc"], grads["kc"], grads["vc"], dbias = c_attn_bwd(pp["qc"], pp["kc"], pp["vc"], s["bias"], s["oc"],
                                                              s["lse_c"], dos[2], "attn_c_bwd")
    drpb = rpb_reduce(dbias, "rpb_reduce")[:, :2 * C_WIN_ROWS - 1, :2 * C_WIN_COLS - 1]
    gains = jnp.pad(p["qk_norm_g"], ((0, 2), (0, 0)))
    dproj, dgains = qk_prep_bwd(s["proj"], gains, cos2, sin2, grads, dproj, "qk_prep_bwd")
    dw_in = mm_aT_d_wcol(s["h"], dproj, "dw_in")
    rest = {"w_in": dw_in, "w_br_a": dw_br[0], "w_br_b": dw_br[1], "w_br_c": dw_br[2],
            "w_o": dw_o.reshape(N_CHIPS, d // N_CHIPS, d)}
    token = on_dws(rest if early else {**rest, **dws})
    dh = mm_x_wcolT(dproj, w["w_in"], "d_h", after=token)
    dx_in, dx_in_c, dg1 = rmsnorm_bwd(s["x"], p["norm1_g"], dh, dx_mid, "norm1_bwd")
    small = {"norm1_g": dg1[0], "qk_norm_g": dgains[:6], "sink_a": dsink[0, :, 0],
             "rpb_c": drpb, "norm2_g": dg2[0]}
    return (dx_in, dx_in_c), small


SMALL_NAMES = ("norm1_g", "qk_norm_g", "sink_a", "rpb_c", "norm2_g")


def _pack_small(parts, extra=None):
    flat = [parts[k].reshape(-1) for k in SMALL_NAMES]
    flat.append(jnp.zeros((1,), F32) if extra is None else extra.reshape(1))
    v = jnp.concatenate(flat)
    rows = -(-v.shape[0] // (8 * HEAD)) * 8
    return jnp.pad(v, (0, rows * HEAD - v.shape[0])).reshape(rows, HEAD)


def _unpack_small(pack, like):
    v = pack.reshape(-1)
    out, off = {}, 0
    for k in SMALL_NAMES:
        size = math.prod(like[k].shape)
        out[k] = v[off:off + size].reshape(like[k].shape)
        off += size
    return out, v[off]


def kernel(x, norm1_g, w_in, qk_norm_g, sink_a, rpb_c, w_br_a, w_br_b, w_br_c, w_o, norm2_g, w_gate_up, w_down, loss_target, m_norm1_g, m_w_in, m_qk_norm_g, m_sink_a, m_rpb_c, m_w_br_a, m_w_br_b, m_w_br_c, m_w_o, m_norm2_g, m_w_gate_up, m_w_down, v_norm1_g, v_w_in, v_qk_norm_g, v_sink_a, v_rpb_c, v_w_br_a, v_w_br_b, v_w_br_c, v_w_o, v_norm2_g, v_w_gate_up, v_w_down):
    big = dict(w_in=w_in, w_br_a=w_br_a, w_br_b=w_br_b, w_br_c=w_br_c, w_o=w_o, w_gate_up=w_gate_up, w_down=w_down)
    big_m = dict(w_in=m_w_in, w_br_a=m_w_br_a, w_br_b=m_w_br_b, w_br_c=m_w_br_c, w_o=m_w_o, w_gate_up=m_w_gate_up, w_down=m_w_down)
    big_v = dict(w_in=v_w_in, w_br_a=v_w_br_a, w_br_b=v_w_br_b, w_br_c=v_w_br_c, w_o=v_w_o, w_gate_up=v_w_gate_up, w_down=v_w_down)
    small = dict(norm1_g=norm1_g, qk_norm_g=qk_norm_g, sink_a=sink_a, rpb_c=rpb_c, norm2_g=norm2_g)
    small_m = dict(norm1_g=m_norm1_g, qk_norm_g=m_qk_norm_g, sink_a=m_sink_a, rpb_c=m_rpb_c, norm2_g=m_norm2_g)
    small_v = dict(norm1_g=v_norm1_g, qk_norm_g=v_qk_norm_g, sink_a=v_sink_a, rpb_c=v_rpb_c, norm2_g=v_norm2_g)
    n_layers = w_in.shape[0]
    n, d = x.shape[1], x.shape[2]
    c_idx = lax.axis_index("c").astype(jnp.int32).reshape(1)
    chip_idx = (2 * lax.axis_index("x") + lax.axis_index("y")).astype(jnp.int32).reshape(1)
    cos2, sin2 = _rope_tables(n)

    def as_weights(names, got):
        w = dict(zip(names, got))
        if "w_o" in w:
            w["w_o"] = w["w_o"].reshape(d, d)
            w["w_down"] = w["w_down"].reshape(-1, d)
        return w

    def gathered(names, started, after):
        send_sems, recv_sems, bufs, _ = started
        return as_weights(names, pair_forward(gather_wait(bufs, send_sems, recv_sems, after, "gather_wait"), "pair_forward"))

    def placed(names, l):
        return [cast_place(chip_idx, big[k], l, "cast_" + k) for k in names]

    first, others_0 = W_NAMES[:1], W_NAMES[1:]
    started = gather_start(placed(first, 0), chip_idx, "gather_start")
    started_rest = gather_start(placed(others_0, 0), started[3], "gather_start")
    later = [placed(W_NAMES, l) for l in range(1, n_layers)]
    weights = gathered(first, started, later[-1][-1] if later else started_rest[3])
    rest = lambda after: gathered(others_0, started_rest, after)

    xs, saved = x[0], []
    for l in range(n_layers):
        p = {k: small[k][l] for k in SMALL_NAMES}
        mid, fwd = None, {}
        if l + 1 < n_layers:
            started = gather_start(later[l], weights["w_in"], "gather_start")
            p["norm1_g"] = p["norm1_g"] + started[3][0, 0]

            def mid(after, started=started, fwd=fwd):
                send_sems, recv_sems, bufs, _ = started
                arrived = gather_wait(bufs, send_sems, recv_sems, after, "gather_wait")
                fwd["started"] = forward_start(arrived, after, "forward_start")
                return fwd["started"][3]

        xs, s = layer_fwd(xs, p, weights, cos2, sin2, rest, mid if l > 0 else None)
        saved.append(s)
        if l + 1 < n_layers and l > 0:
            send_sems, recv_sems, bufs, _ = fwd["started"]
            weights, rest = as_weights(W_NAMES, forward_wait(bufs, send_sems, recv_sems, xs, "forward_wait")), None
        elif l + 1 < n_layers:
            weights, rest = gathered(W_NAMES, started, xs), None
    loss_tile, *dx = loss_head(xs, loss_target[0], "loss_head")

    halves = {k: None for k in W_NAMES}
    small_g = [None] * n_layers
    pending = []

    def finish_exchanges(after):
        for l, names, (send_sems, recv_sems, sums, lands, _) in pending:
            sums, from_chips = exchange_wait(sums, lands, send_sems, recv_sems, after, "exchange_wait")
            for k, sm, r in zip(names, sums, from_chips):
                halves[k] = add_chips(chip_idx, sm, r, halves[k], l, n_layers, "add_chips_" + k)
        pending.clear()

    def make_on_dws(l, wait_first):
        def on_dws(dws):
            names = [k for k in W_NAMES if k in dws]
            parts = [dws[k] for k in names]
            if wait_first:
                finish_exchanges(parts[0])
            from_sibling = pair_exchange(parts, "pair_exchange")
            sums = [add_halves(c_idx, g, o, "add_halves_" + k) for g, o, k in zip(parts, from_sibling, names)]
            pending.append((l, names, exchange_start(sums, "exchange_start")))
            return pending[-1][2][4]
        return on_dws

    for l in reversed(range(n_layers)):
        p = {k: small[k][l] for k in SMALL_NAMES}
        last = l == 0
        dx, small_g[l] = layer_bwd(dx, saved[l], p, cos2, sin2, make_on_dws(l, not last), early=last)
    dx = dx[0]
    finish_exchanges(dx)
    halves = [halves[k] for k in W_NAMES]
    others = pair_share(halves, "pair_share")

    mine = {k: jnp.stack([small_g[l][k] for l in range(n_layers)]) for k in SMALL_NAMES}
    total = small_allreduce(_pack_small(mine, loss_tile[0, 0]), "small_allreduce")
    grad_small, loss = _unpack_small(total, small)

    outs = {}
    for k, mine_half, other_half in zip(W_NAMES, halves, others):
        outs[k] = adamw_big(c_idx, big[k], big_m[k], big_v[k], mine_half, other_half, "adamw_" + k)
    res = adamw(_pack_small(small), _pack_small(grad_small), _pack_small(small_m), _pack_small(small_v), "adamw_small")
    unp = [_unpack_small(t, small)[0] for t in res]
    for k in SMALL_NAMES:
        outs[k] = (grad_small[k],) + tuple(u[k] for u in unp)

    order = ("norm1_g", "w_in", "qk_norm_g", "sink_a", "rpb_c", "w_br_a", "w_br_b", "w_br_c", "w_o", "norm2_g",
             "w_gate_up", "w_down")
    return (loss, dx[None]) + tuple(outs[k][i] for i in range(4) for k in order)
```

```python
import functools
import math

import jax
import jax.numpy as jnp
from jax import lax
from jax.experimental import pallas as pl
from jax.experimental.pallas import tpu as pltpu

F32 = jnp.float32
CDT = jnp.bfloat16
XDT = jnp.bfloat16
ADT = jnp.bfloat16

HEAD = 128
NORM_EPS = 1e-6
ROPE_THETA = 10000.0
A_Q_HEADS, A_KV_HEADS, A_GROUP, A_RADIUS = 8, 2, 4, 128
B_DILS = (1, 4, 16)
B_RADIUS = 64
B_HG = 4
C_HEADS, GRID_W, C_WIN_ROWS, C_WIN_COLS = 8, 64, 8, 16
QKV_W = 9216
COL = dict(qa=0, ka=1024, va=1280, qb=1536, kb=3072, vb=4608, qc=6144, kc=7168, vc=8192)
NEG = -1e30
SCALE = HEAD ** -0.5
N_CHIPS = 4

ADAM_LR, ADAM_B1, ADAM_B2, ADAM_EPS, ADAM_WD, ADAM_STEP = 0.001, 0.9, 0.999, 1e-08, 0.01, 10

VMEM_LIMIT = 56 * 1024 * 1024
MESH = pl.DeviceIdType.MESH


def _pallas(body, **kw):
    return pl.pallas_call(body, **kw)


def _params(sem=None, **kw):
    if sem is not None:
        kw["dimension_semantics"] = sem
    return pltpu.CompilerParams(vmem_limit_bytes=VMEM_LIMIT, **kw)


def _tile(dim, pref, mult=128):
    best = None
    for t in range(mult, min(dim, pref) + 1, mult):
        if dim % t == 0:
            best = t
    return dim if best is None else best


def _sds(shape, dtype):
    return jax.ShapeDtypeStruct(tuple(shape), dtype)


_DIMS = {"nn": (((1,), (0,)), ((), ())), "nt": (((1,), (1,)), ((), ())), "tn": (((0,), (0,)), ((), ()))}


def _matmul(a, b, *, mode, grid, a_spec, b_spec, o_spec, out_shape, acc_shape, name, res=None, res_spec=None,
            after=None):
    nk = grid[2]
    has_res = res is not None
    n_in = 2 + int(has_res) + int(after is not None)

    def body(*refs):
        a_ref, b_ref = refs[:2]
        r_ref = refs[2] if has_res else None
        o_ref, rest = refs[n_in], refs[n_in + 1:]
        p = lax.dot_general(a_ref[...].astype(CDT), b_ref[...].astype(CDT), _DIMS[mode],
                            preferred_element_type=F32)

        def finish(acc):
            if has_res:
                acc = acc + r_ref[...].astype(F32)
            o_ref[...] = acc.astype(o_ref.dtype)

        if nk == 1:
            finish(p)
        else:
            acc_ref = rest[0]
            k = pl.program_id(2)

            @pl.when(k == 0)
            def _():
                acc_ref[...] = p

            @pl.when(k > 0)
            def _():
                acc_ref[...] += p

            @pl.when(k == nk - 1)
            def _():
                finish(acc_ref[...])

    in_specs = [a_spec, b_spec] + ([res_spec] if has_res else [])
    args = (a, b) + ((res,) if has_res else ())
    if after is not None:
        in_specs.append(pl.BlockSpec(after.shape, lambda i, j, kk: (0, 0)))
        args += (after,)
    scratch = [] if nk == 1 else [pltpu.VMEM(acc_shape, F32)]
    return _pallas(body, name=name, grid=grid, in_specs=in_specs, out_specs=o_spec, out_shape=out_shape,
                   scratch_shapes=scratch, compiler_params=_params(("parallel", "parallel", "arbitrary")))(*args)


def mm_x_wcol(a, wg, out_dtype, name, tm_pref=1024, tn_pref=1024, stacked_out=1):
    m, k = a.shape
    ns = wg.shape[2]
    tm, tn = _tile(m, tm_pref, 8), _tile(ns, tn_pref)
    nj = ns // tn
    grid = (m // tm, N_CHIPS * nj, 1)
    a_spec = pl.BlockSpec((tm, k), lambda i, j, kk: (i, 0))
    b_spec = pl.BlockSpec((None, k, tn), lambda i, j, kk: (j // nj, 0, j % nj))
    if stacked_out == 1:
        o_spec = pl.BlockSpec((tm, tn), lambda i, j, kk: (i, j))
        out_shape = _sds((m, N_CHIPS * ns), out_dtype)
    else:
        per = N_CHIPS * nj // stacked_out
        o_spec = pl.BlockSpec((None, tm, tn), lambda i, j, kk: (j // per, i, j % per))
        out_shape = _sds((stacked_out, m, N_CHIPS * ns // stacked_out), out_dtype)
    return _matmul(a, wg, mode="nn", grid=grid, a_spec=a_spec, b_spec=b_spec, o_spec=o_spec,
                   out_shape=out_shape, acc_shape=(tm, tn), name=name)


def mm_x_wcolT(d, wg, name, res=None, tm_pref=1024, tn_pref=512, tk_pref=4096, stacked_in=1, lead=None, after=None):
    kdim, ns = wg.shape[1], wg.shape[2]
    m = d.shape[-2]
    tm, tn, tk = _tile(m, tm_pref, 8), _tile(kdim, tn_pref), _tile(ns, tk_pref)
    nkk = ns // tk
    grid = (m // tm, kdim // tn, N_CHIPS * nkk)
    if lead is not None:
        a_spec = pl.BlockSpec((None, tm, tk), lambda i, j, kk: (lead, i, kk))
    elif stacked_in == 1:
        a_spec = pl.BlockSpec((tm, tk), lambda i, j, kk: (i, kk))
    else:
        per = N_CHIPS * nkk // stacked_in
        a_spec = pl.BlockSpec((None, tm, tk), lambda i, j, kk: (kk // per, i, kk % per))
    b_spec = pl.BlockSpec((None, tn, tk), lambda i, j, kk: (kk // nkk, j, kk % nkk))
    o_spec = pl.BlockSpec((tm, tn), lambda i, j, kk: (i, j))
    return _matmul(d, wg, mode="nt", grid=grid, a_spec=a_spec, b_spec=b_spec, o_spec=o_spec,
                   out_shape=_sds((m, kdim), F32), acc_shape=(tm, tn), name=name,
                   res=res, res_spec=None if res is None else o_spec, after=after)


def mm_aT_d_wcol(a, d, name, tm_pref=512, tk_pref=4096, tn_pref=1024, stacked_in=1, lead=None):
    m, kdim = a.shape
    ntot = d.shape[-1] * stacked_in
    ns = ntot // N_CHIPS
    tm, tkm, tn = _tile(kdim, tm_pref), _tile(m, tk_pref, 8), _tile(ns, tn_pref)
    nj = ns // tn
    grid = (kdim // tm, N_CHIPS * nj, m // tkm)
    a_spec = pl.BlockSpec((tkm, tm), lambda i, j, kk: (kk, i))
    if lead is not None:
        b_spec = pl.BlockSpec((None, tkm, tn), lambda i, j, kk: (lead, kk, j))
    elif stacked_in == 1:
        b_spec = pl.BlockSpec((tkm, tn), lambda i, j, kk: (kk, j))
    else:
        per = N_CHIPS * nj // stacked_in
        b_spec = pl.BlockSpec((None, tkm, tn), lambda i, j, kk: (j // per, kk, j % per))
    o_spec = pl.BlockSpec((None, tm, tn), lambda i, j, kk: (j // nj, i, j % nj))
    return _matmul(a, d, mode="tn", grid=grid, a_spec=a_spec, b_spec=b_spec, o_spec=o_spec,
                   out_shape=_sds((N_CHIPS, kdim, ns), XDT), acc_shape=(tm, tn), name=name)


def mm_x_w(a, w, name, res=None, out_dtype=F32, tm_pref=1024, tn_pref=1024, tk_pref=2048):
    m, k = a.shape
    n = w.shape[1]
    tm, tn, tk = _tile(m, tm_pref, 8), _tile(n, tn_pref), _tile(k, tk_pref)
    grid = (m // tm, n // tn, k // tk)
    o_spec = pl.BlockSpec((tm, tn), lambda i, j, kk: (i, j))
    return _matmul(a, w, mode="nn", grid=grid,
                   a_spec=pl.BlockSpec((tm, tk), lambda i, j, kk: (i, kk)),
                   b_spec=pl.BlockSpec((tk, tn), lambda i, j, kk: (kk, j)),
                   o_spec=o_spec, out_shape=_sds((m, n), out_dtype), acc_shape=(tm, tn), name=name,
                   res=res, res_spec=None if res is None else o_spec)


def mm_x_wT(d, w, name, out_dtype=F32, tm_pref=1024, tn_pref=1024, after=None):
    m, n = d.shape
    k = w.shape[0]
    tm, tn = _tile(m, tm_pref, 8), _tile(k, tn_pref)
    grid = (m // tm, k // tn, 1)
    return _matmul(d, w, mode="nt", grid=grid,
                   a_spec=pl.BlockSpec((tm, n), lambda i, j, kk: (i, 0)),
                   b_spec=pl.BlockSpec((tn, n), lambda i, j, kk: (j, 0)),
                   o_spec=pl.BlockSpec((tm, tn), lambda i, j, kk: (i, j)),
                   out_shape=_sds((m, k), out_dtype), acc_shape=(tm, tn), name=name, after=after)


def mm_aT_d(a, d, name, tm_pref=512, tn_pref=512, tk_pref=4096):
    m, k = a.shape
    n = d.shape[1]
    tm, tn, tk = _tile(k, tm_pref), _tile(n, tn_pref), _tile(m, tk_pref, 8)
    grid = (k // tm, n // tn, m // tk)
    return _matmul(a, d, mode="tn", grid=grid,
                   a_spec=pl.BlockSpec((tk, tm), lambda i, j, kk: (kk, i)),
                   b_spec=pl.BlockSpec((tk, tn), lambda i, j, kk: (kk, j)),
                   o_spec=pl.BlockSpec((tm, tn), lambda i, j, kk: (i, j)),
                   out_shape=_sds((k, n), XDT), acc_shape=(tm, tn), name=name)


def rmsnorm_fwd(x, g, name):
    n, d = x.shape
    tm = _tile(n, 512, 8)

    def body(x_ref, g_ref, h_ref):
        xv = x_ref[...]
        r = lax.rsqrt(jnp.mean(xv * xv, axis=-1, keepdims=True) + NORM_EPS)
        h_ref[...] = (xv * r * g_ref[...]).astype(h_ref.dtype)

    return _pallas(body, name=name, grid=(n // tm,),
                   in_specs=[pl.BlockSpec((tm, d), lambda i: (i, 0)), pl.BlockSpec((1, d), lambda i: (0, 0))],
                   out_specs=pl.BlockSpec((tm, d), lambda i: (i, 0)), out_shape=_sds((n, d), CDT),
                   compiler_params=_params(("parallel",)))(x, g.reshape(1, d))


def rmsnorm_bwd(x, g, dh, dres, name):
    n, d = x.shape
    tm = _tile(n, 256, 8)

    def body(x_ref, g_ref, dh_ref, dres_ref, dx_ref, dxc_ref, dg_ref):
        xv = x_ref[...]
        r = lax.rsqrt(jnp.mean(xv * xv, axis=-1, keepdims=True) + NORM_EPS)
        dhv = dh_ref[...]
        u = dhv * g_ref[...]
        c = jnp.mean(xv * u, axis=-1, keepdims=True)
        dxv = dres_ref[...] + r * u - xv * (r * r * r * c)
        dx_ref[...] = dxv
        dxc_ref[...] = dxv.astype(dxc_ref.dtype)
        part = jnp.broadcast_to(jnp.sum(dhv * xv * r, axis=0, keepdims=True), (8, d))

        @pl.when(pl.program_id(0) == 0)
        def _():
            dg_ref[...] = part

        @pl.when(pl.program_id(0) > 0)
        def _():
            dg_ref[...] += part

    row = pl.BlockSpec((tm, d), lambda i: (i, 0))
    return _pallas(body, name=name, grid=(n // tm,),
                   in_specs=[row, pl.BlockSpec((1, d), lambda i: (0, 0)), row, row],
                   out_specs=[row, row, pl.BlockSpec((8, d), lambda i: (0, 0))],
                   out_shape=[_sds((n, d), F32), _sds((n, d), CDT), _sds((8, d), F32)],
                   compiler_params=_params(("arbitrary",)))(x, g.reshape(1, d), dh, dres)


def _norm_rope(xh, g, cos2, sin2):
    r = lax.rsqrt(jnp.mean(xh * xh, axis=-1, keepdims=True) + NORM_EPS)
    y = xh * r * g
    if cos2 is not None:
        y = y * cos2 + pltpu.roll(y, HEAD // 2, 1) * sin2
    return y


def _norm_rope_bwd(xh, g, cos2, sin2, dout):
    if cos2 is not None:
        dy = dout * cos2 + pltpu.roll(dout * sin2, HEAD // 2, 1)
    else:
        dy = dout
    r = lax.rsqrt(jnp.mean(xh * xh, axis=-1, keepdims=True) + NORM_EPS)
    u = dy * g
    c = jnp.mean(xh * u, axis=-1, keepdims=True)
    return r * u - xh * (r * r * r * c), dy * xh * r


_QK_GROUPS = (("qa", COL["qa"], 8, 0, True), ("ka", COL["ka"], 2, 1, True),
              ("qb", COL["qb"], 12, 2, True), ("kb", COL["kb"], 12, 3, True),
              ("qc", COL["qc"], 8, 4, False), ("kc", COL["kc"], 8, 5, False))
_V_GROUPS = (("va", COL["va"], 2), ("vb", COL["vb"], 12), ("vc", COL["vc"], 8))
_PREP_OUT = (("qa", 8), ("ka", 2), ("va", 2)) + tuple((f"{t}b{g}", 4) for t in "qkv" for g in range(3)) + (
    ("qc", 8), ("kc", 8), ("vc", 8))


def _prep_src(name):
    if name[1] == "b":
        base = COL[name[0] + "b"] + int(name[2]) * B_HG * HEAD
        gain = {"q": 2, "k": 3, "v": None}[name[0]]
        return base, gain, name[0] != "v"
    base = COL[name]
    gain = {"qa": 0, "ka": 1, "va": None, "qc": 4, "kc": 5, "vc": None}[name]
    return base, gain, name in ("qa", "ka")


def _prep_dil(name):
    return B_DILS[int(name[2])] if name[1] == "b" else 1


def _to_classes(val, scr, dil):
    scr[...] = val
    return [scr[pl.ds(r, val.shape[0] // dil, stride=dil), :] for r in range(dil)]


def _from_classes(parts, scr):
    for r, part in enumerate(parts):
        scr[pl.ds(r, part.shape[0], stride=len(parts)), :] = part
    return scr[...]


def _class_spec(tm, dil, width):
    if dil == 1:
        return pl.BlockSpec((tm, width), lambda i: (i, 0))
    return pl.BlockSpec((dil, tm // dil, width), lambda i: (0, i, 0))


def _class_shape(n, dil, width):
    return (n, width) if dil == 1 else (dil, n // dil, width)


def qk_prep(proj, gains, cos2, sin2, name):
    n = proj.shape[0]
    tm = _tile(n, 256, 8)

    def body(p_ref, g_ref, c_ref, s_ref, *refs):
        outs, scr = refs[:-1], refs[-1]
        cos2v, sin2v = c_ref[...], s_ref[...]
        for (nm, heads), o_ref in zip(_PREP_OUT, outs):
            base, gain, rope = _prep_src(nm)
            dil = _prep_dil(nm)
            for h in range(heads):
                hs = slice(h * HEAD, (h + 1) * HEAD)
                xh = p_ref[:, base + h * HEAD: base + (h + 1) * HEAD].astype(F32)
                if gain is None:
                    y = xh
                else:
                    y = _norm_rope(xh, g_ref[gain:gain + 1, :], cos2v if rope else None, sin2v if rope else None)
                if dil == 1:
                    o_ref[:, hs] = y.astype(o_ref.dtype)
                else:
                    for r, part in enumerate(_to_classes(y, scr.at[h % 4], dil)):
                        o_ref[r, :, hs] = part.astype(o_ref.dtype)

    tab = pl.BlockSpec((tm, HEAD), lambda i: (i, 0))
    outs = _pallas(body, name=name, grid=(n // tm,),
                   in_specs=[pl.BlockSpec((tm, QKV_W), lambda i: (i, 0)), pl.BlockSpec((8, HEAD), lambda i: (0, 0)), tab, tab],
                   out_specs=[_class_spec(tm, _prep_dil(nm), h * HEAD) for nm, h in _PREP_OUT],
                   out_shape=[_sds(_class_shape(n, _prep_dil(nm), h * HEAD), CDT) for nm, h in _PREP_OUT],
                   scratch_shapes=[pltpu.VMEM((4, tm, HEAD), F32)],
                   compiler_params=_params(("parallel",)))(proj, gains, cos2, sin2)
    return dict(zip([nm for nm, _ in _PREP_OUT], outs))


def qk_prep_bwd(proj, gains, cos2, sin2, grads, dproj, name):
    n = proj.shape[0]
    tm = _tile(n, 128, 8)
    names = [nm for nm, _ in _PREP_OUT]

    def body(p_ref, g_ref, c_ref, s_ref, *refs):
        g_refs, dp_ref, dg_ref, scr = refs[:len(names)], refs[len(names) + 1], refs[len(names) + 2], refs[-1]
        cos2v, sin2v = c_ref[...], s_ref[...]
        dg = [jnp.zeros((tm, HEAD), F32) for _ in range(6)]
        for (nm, heads), gr in zip(_PREP_OUT, g_refs):
            base, gain, rope = _prep_src(nm)
            dil = _prep_dil(nm)
            for h in range(heads):
                sl = slice(base + h * HEAD, base + (h + 1) * HEAD)
                hs = slice(h * HEAD, (h + 1) * HEAD)
                dout = gr[:, hs] if dil == 1 else _from_classes([gr[r, :, hs] for r in range(dil)], scr.at[h % 4])
                if gain is None:
                    dx = dout
                else:
                    dx, dgr = _norm_rope_bwd(p_ref[:, sl].astype(F32), g_ref[gain:gain + 1, :], cos2v if rope else None,
                                             sin2v if rope else None, dout)
                    dg[gain] = dg[gain] + dgr
                dp_ref[:, sl] = dx.astype(dp_ref.dtype)
        part = jnp.concatenate([jnp.sum(t, axis=0, keepdims=True) for t in dg] + [jnp.zeros((2, HEAD), F32)], axis=0)

        @pl.when(pl.program_id(0) == 0)
        def _():
            dg_ref[...] = part

        @pl.when(pl.program_id(0) > 0)
        def _():
            dg_ref[...] += part

    tab = pl.BlockSpec((tm, HEAD), lambda i: (i, 0))
    dp, dg = _pallas(body, name=name, grid=(n // tm,),
                     in_specs=[pl.BlockSpec((tm, QKV_W), lambda i: (i, 0)), pl.BlockSpec((8, HEAD), lambda i: (0, 0)), tab, tab]
                     + [_class_spec(tm, _prep_dil(nm), h * HEAD) for nm, h in _PREP_OUT] + [ANY],
                     out_specs=[pl.BlockSpec((tm, QKV_W), lambda i: (i, 0)), pl.BlockSpec((8, HEAD), lambda i: (0, 0))],
                     out_shape=[_sds(dproj.shape, dproj.dtype), _sds((8, HEAD), F32)],
                     input_output_aliases={4 + len(names): 0},
                     scratch_shapes=[pltpu.VMEM((4, tm, HEAD), F32)],
                     compiler_params=_params(("arbitrary",)))(proj, gains, cos2, sin2, *[grads[k] for k in names], dproj)
    return dp, dg


def _band_geometry(m, bq_pref, radius):
    bq = min(bq_pref, m)
    return bq, min(bq + 2 * radius, m)


def _band_window(i, bq, radius, m, w):
    start = pl.multiple_of(jnp.clip(i * bq - radius, 0, m - w), 64)
    qpos = i * bq + lax.broadcasted_iota(jnp.int32, (bq, w), 0)
    kpos = start + lax.broadcasted_iota(jnp.int32, (bq, w), 1)
    return start, jnp.abs(kpos - qpos) <= radius


def _band_specs(m, bq, G, nh, seqs):
    lg, nb = seqs // nh, m // bq
    qspec = pl.BlockSpec((bq, nh * G * HEAD), lambda s, i: ((s // lg) * nb + i, s % lg))
    kspec = pl.BlockSpec((m, nh * HEAD), lambda s, i: (s // lg, s % lg))
    return qspec, kspec, lg


def band_attn_fwd(q, k, v, sink, *, seqs, G, nh, radius, name, classes=1, bq_pref=128):
    m = q.shape[0] // classes
    bq, w = _band_geometry(m, bq_pref, radius)
    has_sink = sink is not None

    def body(*refs):
        if has_sink:
            sink_ref, q_ref, k_ref, v_ref, o_ref, lse_ref = refs
        else:
            q_ref, k_ref, v_ref, o_ref, lse_ref = refs
        s_id, i = pl.program_id(0), pl.program_id(1)
        start, valid = _band_window(i, bq, radius, m, w)
        units = [(h, g) for h in range(nh) for g in range(G)]
        sls = [slice((h * G + g) * HEAD, (h * G + g + 1) * HEAD) for h, g in units]
        k_ts = [k_ref[pl.ds(start, w), h * HEAD:(h + 1) * HEAD] for h in range(nh)]
        v_ts = [v_ref[pl.ds(start, w), h * HEAD:(h + 1) * HEAD] for h in range(nh)]
        q_ts = [q_ref[:, sl] for sl in sls]
        sks = [sink_ref[0, (s_id * nh + h) * G + g] for h, g in units] if has_sink else None
        ss = [jnp.where(valid, lax.dot_general(q_t, k_ts[h], _DIMS["nt"], preferred_element_type=F32) * SCALE, NEG)
              for q_t, (h, g) in zip(q_ts, units)]
        mxs = [jnp.max(s, axis=-1, keepdims=True) for s in ss]
        if has_sink:
            mxs = [jnp.maximum(mx, sk) for mx, sk in zip(mxs, sks)]
        ps = [jnp.exp(s - mx) for s, mx in zip(ss, mxs)]
        dens = [jnp.sum(p, axis=-1, keepdims=True) for p in ps]
        if has_sink:
            dens = [den + jnp.exp(sk - mx) for den, sk, mx in zip(dens, sks, mxs)]
        outs = [jnp.dot((p / den).astype(CDT), v_ts[h], preferred_element_type=F32)
                for p, den, (h, g) in zip(ps, dens, units)]
        for sl, o, mx, den in zip(sls, outs, mxs, dens):
            o_ref[:, sl] = o
            lse_ref[:, sl] = jnp.broadcast_to(mx + jnp.log(den), (bq, HEAD))

    qspec, kspec, lg = _band_specs(m, bq, G, nh, seqs)
    in_specs = ([pl.BlockSpec(memory_space=pltpu.SMEM)] if has_sink else []) + [qspec, kspec, kspec]
    args = ((sink,) if has_sink else ()) + (q, k, v)
    return _pallas(body, name=name, grid=(classes * lg, m // bq), in_specs=in_specs, out_specs=[qspec, qspec],
                   out_shape=[_sds(q.shape, F32), _sds(q.shape, F32)],
                   compiler_params=_params(("parallel", "arbitrary")))(*args)


def band_attn_bwd(q, k, v, sink, o, lse, do, dlse, *, seqs, G, nh, radius, name, classes=1, bq_pref=128):
    m = q.shape[0] // classes
    bq, w = _band_geometry(m, bq_pref, radius)
    has_sink, has_dlse = sink is not None, dlse is not None
    assert nh * G <= 8

    def body(*refs):
        refs = list(refs)
        sink_ref = refs.pop(0) if has_sink else None
        q_ref, k_ref, v_ref, o_ref, lse_ref, do_ref = refs[:6]
        refs = refs[6:]
        dlse_ref = refs.pop(0) if has_dlse else None
        dq_ref, dk_ref, dv_ref = refs[:3]
        dsink_ref = refs[3] if has_sink else None
        s_id, i = pl.program_id(0), pl.program_id(1)

        @pl.when(i == 0)
        def _():
            dk_ref[...] = jnp.zeros_like(dk_ref)
            dv_ref[...] = jnp.zeros_like(dv_ref)
            if has_sink:
                dsink_ref[...] = jnp.zeros_like(dsink_ref)

        start, valid = _band_window(i, bq, radius, m, w)
        units = [(h, g) for h in range(nh) for g in range(G)]
        sls = [slice((h * G + g) * HEAD, (h * G + g + 1) * HEAD) for h, g in units]
        kss = [slice(h * HEAD, (h + 1) * HEAD) for h in range(nh)]
        k_ts = [k_ref[pl.ds(start, w), ks] for ks in kss]
        v_ts = [v_ref[pl.ds(start, w), ks] for ks in kss]
        q_ts = [q_ref[:, sl] for sl in sls]
        lse_ts = [lse_ref[:, sl][:, :1] for sl in sls]
        do_ts = [do_ref[:, sl] for sl in sls]
        deltas = [jnp.sum(do_t * o_ref[:, sl], axis=-1, keepdims=True) for do_t, sl in zip(do_ts, sls)]
        dlse_ts = [dlse_ref[:, sl][:, :1] for sl in sls] if has_dlse else None
        dk_old = [dk_ref[pl.ds(start, w), ks] for ks in kss]
        dv_old = [dv_ref[pl.ds(start, w), ks] for ks in kss]
        dsink_old = dsink_ref[...] if has_sink else None

        ps = [jnp.exp(jnp.where(valid, lax.dot_general(q_t, k_ts[h], _DIMS["nt"], preferred_element_type=F32) * SCALE, NEG)
                      - lse_t) for q_t, lse_t, (h, g) in zip(q_ts, lse_ts, units)]
        do_cs = [do_t.astype(CDT) for do_t in do_ts]
        dps = [lax.dot_general(do_c, v_ts[h], _DIMS["nt"], preferred_element_type=F32) for do_c, (h, g) in zip(do_cs, units)]
        ts = [dp - delta for dp, delta in zip(dps, deltas)]
        if has_dlse:
            ts = [t + dl for t, dl in zip(ts, dlse_ts)]
        dss = [((p * t) * SCALE).astype(CDT) for p, t in zip(ps, ts)]
        dqs = [jnp.dot(ds, k_ts[h], preferred_element_type=F32) for ds, (h, g) in zip(dss, units)]
        dvs = [lax.dot_general(p.astype(CDT), do_c, _DIMS["tn"], preferred_element_type=F32) for p, do_c in zip(ps, do_cs)]
        dks = [lax.dot_general(ds, q_t, _DIMS["tn"], preferred_element_type=F32) for ds, q_t in zip(dss, q_ts)]
        dk_new = [dk_old[h] + sum(dks[h * G + g] for g in range(G)) for h in range(nh)]
        dv_new = [dv_old[h] + sum(dvs[h * G + g] for g in range(G)) for h in range(nh)]
        if has_sink:
            rows = []
            for (h, g), lse_t, delta in zip(units, lse_ts, deltas):
                sk = sink_ref[0, (s_id * nh + h) * G + g]
                rows.append(jnp.broadcast_to(-jnp.sum(jnp.exp(sk - lse_t) * delta, axis=0, keepdims=True), (1, HEAD)))
            rows += [jnp.zeros((1, HEAD), F32)] * (8 - len(rows))
            dsink_new = dsink_old + jnp.concatenate(rows, axis=0)

        for sl, dq in zip(sls, dqs):
            dq_ref[:, sl] = dq
        for h, ks in enumerate(kss):
            dk_ref[pl.ds(start, w), ks] = dk_new[h]
            dv_ref[pl.ds(start, w), ks] = dv_new[h]
        if has_sink:
            dsink_ref[...] = dsink_new

    qspec, kspec, lg = _band_specs(m, bq, G, nh, seqs)
    kin = pl.BlockSpec(kspec.block_shape, kspec.index_map, pipeline_mode=pl.Buffered(1))
    in_specs = ([pl.BlockSpec(memory_space=pltpu.SMEM)] if has_sink else []) + [qspec, kin, kin, qspec, qspec, qspec]
    in_specs += [qspec] if has_dlse else []
    args = ((sink,) if has_sink else ()) + (q, k, v, o, lse, do) + ((dlse,) if has_dlse else ())
    out_specs = [qspec, kspec, kspec]
    out_shape = [_sds(q.shape, F32), _sds(k.shape, F32), _sds(k.shape, F32)]
    if has_sink:
        out_specs.append(pl.BlockSpec((None, 8, HEAD), lambda s, i: (s, 0, 0)))
        out_shape.append(_sds((seqs // nh, 8, HEAD), F32))
    return _pallas(body, name=name, grid=(classes * lg, m // bq), in_specs=in_specs, out_specs=out_specs,
                   out_shape=out_shape, compiler_params=_params(("parallel", "arbitrary")))(*args)


def _group_weights(lses):
    mx = jnp.maximum(jnp.maximum(lses[0], lses[1]), lses[2])
    e = [jnp.exp(l - mx) for l in lses]
    tot = e[0] + e[1] + e[2]
    return [t / tot for t in e]


def _read_group(ref, dil, h, scr):
    hs = slice(h * HEAD, (h + 1) * HEAD)
    return ref[:, hs] if dil == 1 else _from_classes([ref[r, :, hs] for r in range(dil)], scr)


def b_combine_fwd(os_, lses, name):
    n, wd = os_[0].shape
    tm = _tile(n, 512, 8)

    def body(o0, o1, o2, l0, l1, l2, out_ref, scr):
        for h in range(wd // HEAD):
            ls = [_read_group(ref, dil, h, scr.at[g]) for g, (ref, dil) in enumerate(zip((l0, l1, l2), B_DILS))]
            ovs = [_read_group(ref, dil, h, scr.at[3 + g]) for g, (ref, dil) in enumerate(zip((o0, o1, o2), B_DILS))]
            wts = _group_weights(ls)
            out_ref[:, h * HEAD:(h + 1) * HEAD] = wts[0] * ovs[0] + wts[1] * ovs[1] + wts[2] * ovs[2]

    specs = [_class_spec(tm, dil, wd) for dil in B_DILS]
    return _pallas(body, name=name, grid=(n // tm,), in_specs=specs * 2, out_specs=pl.BlockSpec((tm, wd), lambda i: (i, 0)),
                   out_shape=_sds((n, wd), F32), scratch_shapes=[pltpu.VMEM((6, tm, HEAD), F32)],
                   compiler_params=_params(("parallel",)))(*os_, *lses)


def b_combine_bwd(dout, os_, lses, name):
    n, wd = dout.shape
    tm = _tile(n, 256, 8)

    def body(d_ref, o0, o1, o2, l0, l1, l2, do0, do1, do2, dl0, dl1, dl2, scr):
        for h in range(wd // HEAD):
            hs = slice(h * HEAD, (h + 1) * HEAD)
            dv = d_ref[:, hs]
            ls = [_read_group(ref, dil, h, scr.at[g]) for g, (ref, dil) in enumerate(zip((l0, l1, l2), B_DILS))]
            ovs = [_read_group(ref, dil, h, scr.at[3 + g]) for g, (ref, dil) in enumerate(zip((o0, o1, o2), B_DILS))]
            wts = _group_weights(ls)
            dws = [jnp.broadcast_to(jnp.sum(dv * ov, axis=-1, keepdims=True), (tm, HEAD)) for ov in ovs]
            mean = wts[0] * dws[0] + wts[1] * dws[1] + wts[2] * dws[2]
            for g, (wt, dw, do_ref, dl_ref, dil) in enumerate(zip(wts, dws, (do0, do1, do2), (dl0, dl1, dl2), B_DILS)):
                if dil == 1:
                    do_ref[:, hs] = wt * dv
                    dl_ref[:, hs] = wt * (dw - mean)
                else:
                    for r, part in enumerate(_to_classes(wt * dv, scr.at[g], dil)):
                        do_ref[r, :, hs] = part
                    for r, part in enumerate(_to_classes(wt * (dw - mean), scr.at[3 + g], dil)):
                        dl_ref[r, :, hs] = part

    specs = [_class_spec(tm, dil, wd) for dil in B_DILS]
    outs = _pallas(body, name=name, grid=(n // tm,), in_specs=[pl.BlockSpec((tm, wd), lambda i: (i, 0))] + specs * 2,
                   out_specs=specs * 2, out_shape=[_sds(_class_shape(n, dil, wd), F32) for dil in B_DILS] * 2,
                   scratch_shapes=[pltpu.VMEM((6, tm, HEAD), F32)],
                   compiler_params=_params(("parallel",)))(dout, *os_, *lses)
    return outs[:3], outs[3:]


def _c_rows(n):
    rows = n // GRID_W
    return rows, min(C_WIN_ROWS, rows)


def _c_row_start(r, rows, wr):
    return jnp.clip(r - wr // 2, 0, rows - wr)


def _c_bias_index(r, rows, wr):
    return _c_row_start(r, rows, wr) - r + (C_WIN_ROWS - 1)


def _col_shift_select(tile, cq, inverse):
    lanes = tile.shape[1]
    for b in range(6):
        amt = (lanes - (1 << b)) if inverse else (1 << b)
        tile = jnp.where(((cq >> b) & 1) == 1, pltpu.roll(tile, amt, 1), tile)
    return tile


def rpb_expand(rwin, name):
    lanes = rwin.shape[-1]

    def body(r_ref, b_ref):
        cq = lax.broadcasted_iota(jnp.int32, (GRID_W, lanes), 0)
        ck = lax.broadcasted_iota(jnp.int32, (GRID_W, lanes), 1) % GRID_W
        cs = jnp.clip(cq - C_WIN_COLS // 2, 0, GRID_W - C_WIN_COLS)
        ok = (ck >= cs) & (ck < cs + C_WIN_COLS)
        for i0 in range(C_WIN_ROWS):
            tile = jnp.broadcast_to(r_ref[i0], (GRID_W, lanes))
            tile = pltpu.roll(tile, lanes - (C_WIN_COLS - 1), 1)
            tile = _col_shift_select(tile, cq, False)
            b_ref[i0] = jnp.where(ok, tile, NEG)

    return _pallas(body, name=name, grid=(C_HEADS,),
                   in_specs=[pl.BlockSpec((None, C_WIN_ROWS, 1, lanes), lambda h: (h, 0, 0, 0))],
                   out_specs=pl.BlockSpec((None, C_WIN_ROWS, GRID_W, lanes), lambda h: (h, 0, 0, 0)),
                   out_shape=_sds((C_HEADS, C_WIN_ROWS, GRID_W, lanes), F32),
                   compiler_params=_params(("parallel",)))(rwin)


def rpb_reduce(dbias, name):
    lanes = dbias.shape[-1]
    wr = lanes // GRID_W

    def body(d_ref, o_ref):
        cq = lax.broadcasted_iota(jnp.int32, (GRID_W, lanes), 0)
        o_ref[...] = jnp.zeros_like(o_ref)
        for i0 in range(C_WIN_ROWS):
            tile = _col_shift_select(d_ref[i0], cq, True)
            tile = pltpu.roll(tile, C_WIN_COLS - 1, 1)
            vec = jnp.sum(tile, axis=0, keepdims=True)
            for w in range(wr):
                o_ref[i0 + w:i0 + w + 1, :] += vec[:, w * GRID_W:(w + 1) * GRID_W]

    return _pallas(body, name=name, grid=(C_HEADS,),
                   in_specs=[pl.BlockSpec((None, C_WIN_ROWS, GRID_W, lanes), lambda h: (h, 0, 0, 0))],
                   out_specs=pl.BlockSpec((None, 16, GRID_W), lambda h: (h, 0, 0)),
                   out_shape=_sds((C_HEADS, 16, GRID_W), F32),
                   compiler_params=_params(("parallel",)))(dbias)


def _store_or_add(ref, val, first):
    @pl.when(first)
    def _():
        ref[...] = val

    @pl.when(jnp.logical_not(first))
    def _():
        ref[...] += val


def c_attn_fwd(q, k, v, bias, name, nh=C_HEADS):
    n = q.shape[0]
    rows, wr = _c_rows(n)
    wk = wr * GRID_W

    def body(q_ref, k_ref, v_ref, b_ref, o_ref, lse_ref):
        r = pl.program_id(1)
        start = pl.multiple_of(_c_row_start(r, rows, wr) * GRID_W, GRID_W)
        sls = [slice(h * HEAD, (h + 1) * HEAD) for h in range(nh)]
        ss = [lax.dot_general(q_ref[:, sl], k_ref[pl.ds(start, wk), sl], _DIMS["nt"], preferred_element_type=F32)
              * SCALE + b_ref[h] for h, sl in enumerate(sls)]
        mxs = [jnp.max(s, axis=-1, keepdims=True) for s in ss]
        ps = [jnp.exp(s - mx) for s, mx in zip(ss, mxs)]
        dens = [jnp.sum(p, axis=-1, keepdims=True) for p in ps]
        outs = [jnp.dot((p / den).astype(CDT), v_ref[pl.ds(start, wk), sl], preferred_element_type=F32)
                for p, den, sl in zip(ps, dens, sls)]
        for sl, o, mx, den in zip(sls, outs, mxs, dens):
            o_ref[:, sl] = o
            lse_ref[:, sl] = jnp.broadcast_to(mx + jnp.log(den), (GRID_W, HEAD))

    qspec = pl.BlockSpec((GRID_W, nh * HEAD), lambda h, r: (r, h))
    kspec = pl.BlockSpec((n, nh * HEAD), lambda h, r: (0, h), pipeline_mode=pl.Buffered(1))
    bspec = pl.BlockSpec((nh, None, GRID_W, wk), lambda h, r: (h, _c_bias_index(r, rows, wr), 0, 0))
    return _pallas(body, name=name, grid=(C_HEADS // nh, rows), in_specs=[qspec, kspec, kspec, bspec],
                   out_specs=[qspec, qspec], out_shape=[_sds(q.shape, F32), _sds(q.shape, F32)],
                   compiler_params=_params(("parallel", "arbitrary")))(q, k, v, bias)


def c_attn_bwd(q, k, v, bias, o, lse, do, name, nh=4):
    n = q.shape[0]
    rows, wr = _c_rows(n)
    wk = wr * GRID_W

    def body(q_ref, k_ref, v_ref, b_ref, o_ref, lse_ref, do_ref, dq_ref, dk_ref, dv_ref, db_ref):
        r = pl.program_id(1)
        rs = _c_row_start(r, rows, wr)
        start = pl.multiple_of(rs * GRID_W, GRID_W)

        @pl.when(r == 0)
        def _():
            dk_ref[...] = jnp.zeros_like(dk_ref)
            dv_ref[...] = jnp.zeros_like(dv_ref)

        prev = _c_row_start(jnp.maximum(r - 1, 0), rows, wr) - jnp.maximum(r - 1, 0)
        first = (r == 0) | (prev != rs - r)
        sls = [slice(h * HEAD, (h + 1) * HEAD) for h in range(nh)]
        k_ts = [k_ref[pl.ds(start, wk), sl] for sl in sls]
        v_ts = [v_ref[pl.ds(start, wk), sl] for sl in sls]
        q_ts = [q_ref[:, sl] for sl in sls]
        lse_ts = [lse_ref[:, sl][:, :1] for sl in sls]
        do_ts = [do_ref[:, sl] for sl in sls]
        deltas = [jnp.sum(do_t * o_ref[:, sl], axis=-1, keepdims=True) for do_t, sl in zip(do_ts, sls)]
        biases = [b_ref[h] for h in range(nh)]
        dk_old = [dk_ref[pl.ds(start, wk), sl] for sl in sls]
        dv_old = [dv_ref[pl.ds(start, wk), sl] for sl in sls]

        ps = [jnp.exp(lax.dot_general(q_t, k_t, _DIMS["nt"], preferred_element_type=F32) * SCALE + b - lse_t)
              for q_t, k_t, b, lse_t in zip(q_ts, k_ts, biases, lse_ts)]
        do_cs = [do_t.astype(CDT) for do_t in do_ts]
        dps = [lax.dot_general(do_c, v_t, _DIMS["nt"], preferred_element_type=F32) for do_c, v_t in zip(do_cs, v_ts)]
        dss = [p * (dp - delta) for p, dp, delta in zip(ps, dps, deltas)]
        ds_cs = [(ds * SCALE).astype(CDT) for ds in dss]
        dqs = [jnp.dot(ds_c, k_t, preferred_element_type=F32) for ds_c, k_t in zip(ds_cs, k_ts)]
        dv_new = [old + lax.dot_general(p.astype(CDT), do_c, _DIMS["tn"], preferred_element_type=F32)
                  for old, p, do_c in zip(dv_old, ps, do_cs)]
        dk_new = [old + lax.dot_general(ds_c, q_t, _DIMS["tn"], preferred_element_type=F32)
                  for old, ds_c, q_t in zip(dk_old, ds_cs, q_ts)]

        for h, sl in enumerate(sls):
            dq_ref[:, sl] = dqs[h]
            dk_ref[pl.ds(start, wk), sl] = dk_new[h]
            dv_ref[pl.ds(start, wk), sl] = dv_new[h]
        for h in range(nh):
            _store_or_add(db_ref.at[h], dss[h], first)

    qspec = pl.BlockSpec((GRID_W, nh * HEAD), lambda h, r: (r, h))
    kspec = pl.BlockSpec((n, nh * HEAD), lambda h, r: (0, h))
    kin = pl.BlockSpec((n, nh * HEAD), lambda h, r: (0, h), pipeline_mode=pl.Buffered(1))
    bspec = pl.BlockSpec((nh, None, GRID_W, wk), lambda h, r: (h, _c_bias_index(r, rows, wr), 0, 0))
    return _pallas(body, name=name, grid=(C_HEADS // nh, rows),
                   in_specs=[qspec, kin, kin, bspec, qspec, qspec, qspec],
                   out_specs=[qspec, kspec, kspec, bspec],
                   out_shape=[_sds(q.shape, F32), _sds(k.shape, F32), _sds(k.shape, F32), _sds(bias.shape, F32)],
                   compiler_params=_params(("parallel", "arbitrary")))(q, k, v, bias, o, lse, do)


def _sigmoid(z):
    return 1.0 / (1.0 + jnp.exp(-z))


def _gate_specs(n, d):
    tm, tn = _tile(n, 256, 8), _tile(math.gcd(d, QKV_W), 1024)
    nj = d // tn
    tile = pl.BlockSpec((tm, tn), lambda i, j: (i, j))
    gl = [pl.BlockSpec((tm, tn), functools.partial(lambda i, j, b: (i, (QKV_W + b * d) // tn + j), b=b)) for b in range(3)]
    return tm, tn, nj, tile, gl


def gate_merge(proj, ys, name):
    n, d = ys[0].shape
    tm, tn, nj, tile, gl = _gate_specs(n, d)

    def body(g0, g1, g2, y0, y1, y2, out_ref):
        acc = (_sigmoid(g0[...].astype(F32)) * y0[...] + _sigmoid(g1[...].astype(F32)) * y1[...]
               + _sigmoid(g2[...].astype(F32)) * y2[...])
        out_ref[...] = acc.astype(out_ref.dtype)

    return _pallas(body, name=name, grid=(n // tm, nj), in_specs=gl + [tile] * 3, out_specs=tile,
                   out_shape=_sds((n, d), CDT), compiler_params=_params(("parallel", "parallel")))(proj, proj, proj, *ys)


def gate_bwd(proj, ys, dmerged, name):
    n, d = dmerged.shape
    tm, tn, nj, _, _ = _gate_specs(n, d)

    def body(g_ref, y0, y1, y2, dm_ref, dy_ref, dp_ref):
        b = pl.program_id(2)
        y = jnp.where(b == 0, y0[...], jnp.where(b == 1, y1[...], y2[...]))
        dm = dm_ref[...]
        sg = _sigmoid(g_ref[...].astype(F32))
        dy_ref[...] = (dm * sg).astype(dy_ref.dtype)
        dp_ref[...] = (dm * y * sg * (1.0 - sg)).astype(dp_ref.dtype)

    gl = pl.BlockSpec((tm, tn), lambda i, j, b: (i, QKV_W // tn + b * nj + j))
    tile = pl.BlockSpec((tm, tn), lambda i, j, b: (i, j))
    return _pallas(body, name=name, grid=(n // tm, nj, 3), in_specs=[gl, tile, tile, tile, tile],
                   out_specs=[pl.BlockSpec((None, tm, tn), lambda i, j, b: (b, i, j)), gl],
                   out_shape=[_sds((3, n, d), CDT), _sds(proj.shape, CDT)],
                   compiler_params=_params(("parallel", "parallel", "arbitrary")))(proj, *ys, dmerged)


def gate_up_swiglu(h2, wg, name, after=None):
    n, d = h2.shape
    ns = wg.shape[2]
    ff = 2 * ns
    tm, tn = _tile(n, 512, 8), _tile(ns, 1408)
    nj = ns // tn
    extra = [] if after is None else [after]

    def body(a_ref, bg_ref, bu_ref, *refs):
        gu_ref, act_ref = refs[len(extra):]
        a = a_ref[...].astype(CDT)
        gt = jnp.dot(a, bg_ref[...].astype(CDT), preferred_element_type=F32)
        up = jnp.dot(a, bu_ref[...].astype(CDT), preferred_element_type=F32)
        gu_ref[0] = gt.astype(gu_ref.dtype)
        gu_ref[1] = up.astype(gu_ref.dtype)
        act_ref[...] = (gt * _sigmoid(gt) * up).astype(act_ref.dtype)

    return _pallas(body, name=name, grid=(n // tm, 2 * nj),
                   in_specs=[pl.BlockSpec((tm, d), lambda i, j: (i, 0)),
                             pl.BlockSpec((None, d, tn), lambda i, j: (j // nj, 0, j % nj)),
                             pl.BlockSpec((None, d, tn), lambda i, j: (2 + j // nj, 0, j % nj))]
                   + [pl.BlockSpec(a.shape, lambda i, j: (0, 0)) for a in extra],
                   out_specs=[pl.BlockSpec((2, tm, tn), lambda i, j: (0, i, j)), pl.BlockSpec((tm, tn), lambda i, j: (i, j))],
                   out_shape=[_sds((2, n, ff), ADT), _sds((n, ff), CDT)],
                   compiler_params=_params(("parallel", "parallel")))(h2, wg, wg, *extra)


def d_gate_up(dx, w_down, gu, name):
    n, d = dx.shape
    ff = w_down.shape[0]
    tm, tn = _tile(n, 1024, 8), _tile(ff, 512)

    def body(a_ref, b_ref, gu_ref, d_ref):
        da = lax.dot_general(a_ref[...].astype(CDT), b_ref[...].astype(CDT), _DIMS["nt"], preferred_element_type=F32)
        gt, up = gu_ref[0].astype(F32), gu_ref[1].astype(F32)
        sg = _sigmoid(gt)
        d_ref[0] = (da * up * (sg + gt * sg * (1.0 - sg))).astype(d_ref.dtype)
        d_ref[1] = (da * gt * sg).astype(d_ref.dtype)

    blk = pl.BlockSpec((2, tm, tn), lambda i, j: (0, i, j))
    return _pallas(body, name=name, grid=(n // tm, ff // tn),
                   in_specs=[pl.BlockSpec((tm, d), lambda i, j: (i, 0)), pl.BlockSpec((tn, d), lambda i, j: (j, 0)), blk],
                   out_specs=blk, out_shape=_sds((2, n, ff), CDT),
                   compiler_params=_params(("parallel", "parallel")))(dx, w_down, gu)


def loss_head(y, target, name):
    n, d = y.shape
    tm = _tile(n, 512, 8)
    nsteps = n // tm

    def body(y_ref, t_ref, l_ref, dy_ref, dyc_ref, acc_ref):
        i = pl.program_id(0)
        e = y_ref[...] - t_ref[...]
        dy_ref[...] = e * (1.0 / d)
        dyc_ref[...] = (e * (1.0 / d)).astype(dyc_ref.dtype)
        part = jnp.sum((e * e).reshape(tm // 8, 8, d), axis=0)

        @pl.when(i == 0)
        def _():
            acc_ref[...] = part

        @pl.when(i > 0)
        def _():
            acc_ref[...] += part

        @pl.when(i == nsteps - 1)
        def _():
            tot = jnp.sum(jnp.sum(acc_ref[...], axis=1, keepdims=True), axis=0, keepdims=True) * (0.5 / d)
            l_ref[...] = jnp.broadcast_to(tot, (8, HEAD))

    row = pl.BlockSpec((tm, d), lambda i: (i, 0))
    return _pallas(body, name=name, grid=(nsteps,), in_specs=[row, row],
                   out_specs=[pl.BlockSpec((8, HEAD), lambda i: (0, 0)), row, row],
                   out_shape=[_sds((8, HEAD), F32), _sds((n, d), F32), _sds((n, d), CDT)],
                   scratch_shapes=[pltpu.VMEM((8, d), F32)],
                   compiler_params=_params(("arbitrary",)))(y, target)


def adamw(w, g, m, v, name):
    r, c = w.shape
    tr = _tile(r, max(8, (1 << 19) // c), 8)
    c1 = 1.0 - ADAM_B1 ** ADAM_STEP
    c2 = 1.0 - ADAM_B2 ** ADAM_STEP

    def body(w_ref, g_ref, m_ref, v_ref, d_ref, mo_ref, vo_ref):
        gv = g_ref[...]
        mn = ADAM_B1 * m_ref[...] + (1.0 - ADAM_B1) * gv
        vn = ADAM_B2 * v_ref[...] + (1.0 - ADAM_B2) * (gv * gv)
        d_ref[...] = -ADAM_LR * ((mn / c1) / (jnp.sqrt(vn / c2) + ADAM_EPS) + ADAM_WD * w_ref[...])
        mo_ref[...] = mn
        vo_ref[...] = vn

    row = pl.BlockSpec((tr, c), lambda i: (i, 0))
    return _pallas(body, name=name, grid=(r // tr,), in_specs=[row] * 4, out_specs=[row] * 3,
                   out_shape=[_sds((r, c), F32)] * 3, compiler_params=_params(("parallel",)))(w, g, m, v)


ANY = pl.BlockSpec(memory_space=pl.ANY)


def _place():
    x, y, c = lax.axis_index("x"), lax.axis_index("y"), lax.axis_index("c")
    return x, y, c, [(1 - x, y), (x, 1 - y), (1 - x, 1 - y)]


def _rcopy(src, dst, send_sems, recv_sems, k, to):
    return pltpu.make_async_remote_copy(src_ref=src, dst_ref=dst, send_sem=send_sems.at[k], recv_sem=recv_sems.at[k],
                                        device_id=to, device_id_type=MESH)


def cast_place(chip_idx, shards, layer, name):
    _, k, ns = shards.shape
    tr = _tile(k, max(16, (1 << 19) // ns), 16)

    def body(k_ref, s_ref, o_ref):
        o_ref[...] = s_ref[...].astype(o_ref.dtype)

    gs = pltpu.PrefetchScalarGridSpec(
        num_scalar_prefetch=1, grid=(k // tr,),
        in_specs=[pl.BlockSpec((None, tr, ns), lambda i, k_ref: (layer, i, 0))],
        out_specs=pl.BlockSpec((None, tr, ns), lambda i, k_ref: (k_ref[0], i, 0)))
    return _pallas(body, name=name, grid_spec=gs, out_shape=_sds((N_CHIPS, k, ns), CDT),
                   compiler_params=_params(("parallel",)))(chip_idx, shards)


HBM = pl.BlockSpec(memory_space=pltpu.HBM)
SEM = pl.BlockSpec(memory_space=pltpu.SEMAPHORE)
EFFECT = pltpu.SideEffectType.DATAFLOW_SIDE_EFFECTING


def _in_hbm(a):
    return pltpu.with_memory_space_constraint(a, pltpu.HBM)


def _gather_copies(refs, send_sems, recv_sems):
    x, y, c, chips = _place()
    me = 2 * x + y
    out = []
    for t, ref in enumerate(refs):
        kh = ref.shape[1] // 2
        for j, (px, py) in enumerate(chips):
            send = _rcopy(ref.at[me, pl.ds(c * kh, kh)], ref.at[me, pl.ds(c * kh, kh)], send_sems, recv_sems,
                          3 * t + j, (px, py, c))
            land = ref.at[2 * px + py, pl.ds(c * kh, kh)]
            out.append((send, _rcopy(land, land, send_sems, recv_sems, 3 * t + j, (px, py, c))))
    return out


def gather_start(bufs, after, name):
    nt = len(bufs)
    after = list(after) if isinstance(after, (list, tuple)) else [after]
    na = len(after)

    def body(*refs):
        ins, send_sems, recv_sems, token = refs[:nt], refs[nt + na], refs[nt + na + 1], refs[-1]
        for send, _ in _gather_copies(ins, send_sems, recv_sems):
            send.start()
        token[...] = jnp.zeros_like(token)

    outs = _pallas(body, name=name, in_specs=[HBM] * nt + [ANY] * na,
                   out_specs=(SEM, SEM) + (HBM,) * nt + (pl.BlockSpec(memory_space=pltpu.VMEM),),
                   out_shape=(pltpu.SemaphoreType.DMA((3 * nt,)), pltpu.SemaphoreType.DMA((3 * nt,)))
                   + tuple(pltpu.HBM(b.shape, b.dtype) for b in bufs) + (_sds((8, HEAD), F32),),
                   input_output_aliases={t: 2 + t for t in range(nt)},
                   compiler_params=pltpu.CompilerParams(has_side_effects=EFFECT))(*[_in_hbm(b) for b in bufs], *after)
    return outs[0], outs[1], list(outs[2:2 + nt]), outs[-1]


def gather_wait(bufs, send_sems, recv_sems, after, name):
    nt = len(bufs)

    def body(*refs):
        ins, s_sems, r_sems = refs[:nt], refs[nt], refs[nt + 1]
        for send, land in _gather_copies(ins, s_sems, r_sems):
            send.wait_send()
            land.wait_recv()

    return _pallas(body, name=name, in_specs=[HBM] * nt + [SEM, SEM, ANY], out_specs=[HBM] * nt,
                   out_shape=[pltpu.HBM(b.shape, b.dtype) for b in bufs],
                   input_output_aliases={t: t for t in range(nt)},
                   compiler_params=pltpu.CompilerParams(has_side_effects=EFFECT))(*bufs, send_sems, recv_sems, after)


def _forward_copies(refs, send_sems, recv_sems):
    x, y, c, chips = _place()
    out = []
    for t, ref in enumerate(refs):
        kh = ref.shape[1] // 2
        for j, (px, py) in enumerate(chips):
            mine = ref.at[2 * px + py, pl.ds(c * kh, kh)]
            land = ref.at[2 * px + py, pl.ds((1 - c) * kh, kh)]
            out.append((_rcopy(mine, mine, send_sems, recv_sems, 3 * t + j, (x, y, 1 - c)),
                        _rcopy(land, land, send_sems, recv_sems, 3 * t + j, (x, y, 1 - c))))
    return out


def forward_start(bufs, after, name):
    nt = len(bufs)

    def body(*refs):
        ins, send_sems, recv_sems, token = refs[:nt], refs[nt + 1], refs[nt + 2], refs[-1]
        for send, _ in _forward_copies(ins, send_sems, recv_sems):
            send.start()
        token[...] = jnp.zeros_like(token)

    outs = _pallas(body, name=name, in_specs=[HBM] * nt + [ANY],
                   out_specs=(SEM, SEM) + (HBM,) * nt + (pl.BlockSpec(memory_space=pltpu.VMEM),),
                   out_shape=(pltpu.SemaphoreType.DMA((3 * nt,)), pltpu.SemaphoreType.DMA((3 * nt,)))
                   + tuple(pltpu.HBM(b.shape, b.dtype) for b in bufs) + (_sds((8, HEAD), F32),),
                   input_output_aliases={t: 2 + t for t in range(nt)},
                   compiler_params=pltpu.CompilerParams(has_side_effects=EFFECT))(*[_in_hbm(b) for b in bufs], after)
    return outs[0], outs[1], list(outs[2:2 + nt]), outs[-1]


def forward_wait(bufs, send_sems, recv_sems, after, name):
    nt = len(bufs)

    def body(*refs):
        ins, s_sems, r_sems = refs[:nt], refs[nt], refs[nt + 1]
        for send, land in _forward_copies(ins, s_sems, r_sems):
            send.wait_send()
            land.wait_recv()

    return _pallas(body, name=name, in_specs=[HBM] * nt + [SEM, SEM, ANY], out_specs=[HBM] * nt,
                   out_shape=[pltpu.HBM(b.shape, b.dtype) for b in bufs],
                   input_output_aliases={t: t for t in range(nt)},
                   compiler_params=pltpu.CompilerParams(has_side_effects=EFFECT))(*bufs, send_sems, recv_sems, after)


def pair_forward(bufs, name):
    nt = len(bufs)

    def body(*refs):
        outs = refs[nt:2 * nt]
        send_sems, recv_sems = refs[2 * nt:]
        x, y, c, chips = _place()
        cps = []
        for t in range(nt):
            kh = outs[t].shape[1] // 2
            for j, (px, py) in enumerate(chips):
                blk = outs[t].at[2 * px + py, pl.ds(c * kh, kh)]
                cps.append(_rcopy(blk, blk, send_sems, recv_sems, 3 * t + j, (x, y, 1 - c)))
                cps[-1].start()
        for t in range(nt):
            kh = outs[t].shape[1] // 2
            for j, (px, py) in enumerate(chips):
                blk = outs[t].at[2 * px + py, pl.ds((1 - c) * kh, kh)]
                _rcopy(blk, blk, send_sems, recv_sems, 3 * t + j, (x, y, 1 - c)).wait_recv()
        for cp in cps:
            cp.wait_send()

    return _pallas(body, name=name, in_specs=[ANY] * nt, out_specs=[ANY] * nt,
                   out_shape=[_sds(b.shape, b.dtype) for b in bufs],
                   input_output_aliases={t: t for t in range(nt)},
                   scratch_shapes=[pltpu.SemaphoreType.DMA((3 * nt,)), pltpu.SemaphoreType.DMA((3 * nt,))],
                   compiler_params=pltpu.CompilerParams(has_side_effects=True))(*bufs)


def pair_exchange(grads, name):
    nt = len(grads)

    def body(*refs):
        ins, outs = refs[:nt], refs[nt:2 * nt]
        send_sems, recv_sems = refs[2 * nt:]
        x, y, c, _ = _place()
        sibling = (x, y, 1 - c)
        cps = []
        for t in range(nt):
            kh = ins[t].shape[1] // 2
            cps.append(_rcopy(ins[t].at[:, pl.ds((1 - c) * kh, kh), :], outs[t], send_sems, recv_sems, t, sibling))
            cps[-1].start()
        for cp in cps:
            cp.wait_recv()
        for cp in cps:
            cp.wait_send()

    return _pallas(body, name=name, in_specs=[ANY] * nt, out_specs=[ANY] * nt,
                   out_shape=[_sds((N_CHIPS, g.shape[1] // 2, g.shape[2]), g.dtype) for g in grads],
                   scratch_shapes=[pltpu.SemaphoreType.DMA((nt,)), pltpu.SemaphoreType.DMA((nt,))],
                   compiler_params=pltpu.CompilerParams(has_side_effects=True))(*grads)


def _exchange_copies(sums, lands, send_sems, recv_sems):
    x, y, c, chips = _place()
    return [_rcopy(s.at[2 * px + py], l.at[j], send_sems, recv_sems, 3 * t + j, (px, py, c))
            for t, (s, l) in enumerate(zip(sums, lands)) for j, (px, py) in enumerate(chips)]


def exchange_start(sums, name):
    nt = len(sums)
    lands = [lax.empty((3,) + s.shape[1:], s.dtype) for s in sums]

    def body(*refs):
        ins, zones, send_sems, recv_sems, token = refs[:nt], refs[nt:2 * nt], refs[2 * nt], refs[2 * nt + 1], refs[-1]
        for cp in _exchange_copies(ins, zones, send_sems, recv_sems):
            cp.start()
        token[...] = jnp.zeros_like(token)

    outs = _pallas(body, name=name, in_specs=[HBM] * (2 * nt),
                   out_specs=(SEM, SEM) + (HBM,) * (2 * nt) + (pl.BlockSpec(memory_space=pltpu.VMEM),),
                   out_shape=(pltpu.SemaphoreType.DMA((3 * nt,)), pltpu.SemaphoreType.DMA((3 * nt,)))
                   + tuple(pltpu.HBM(a.shape, a.dtype) for a in list(sums) + lands) + (_sds((8, HEAD), F32),),
                   input_output_aliases={t: 2 + t for t in range(2 * nt)},
                   compiler_params=pltpu.CompilerParams(has_side_effects=EFFECT))(*[_in_hbm(a) for a in list(sums) + lands])
    return outs[0], outs[1], list(outs[2:2 + nt]), list(outs[2 + nt:2 + 2 * nt]), outs[-1]


def exchange_wait(sums, lands, send_sems, recv_sems, after, name):
    nt = len(sums)

    def body(*refs):
        ins, zones, s_sems, r_sems = refs[:nt], refs[nt:2 * nt], refs[2 * nt], refs[2 * nt + 1]
        for cp in _exchange_copies(ins, zones, s_sems, r_sems):
            cp.wait_send()
            cp.wait_recv()

    outs = _pallas(body, name=name, in_specs=[HBM] * (2 * nt) + [SEM, SEM, ANY], out_specs=[HBM] * (2 * nt),
                   out_shape=[pltpu.HBM(a.shape, a.dtype) for a in list(sums) + list(lands)],
                   input_output_aliases={t: t for t in range(2 * nt)},
                   compiler_params=pltpu.CompilerParams(has_side_effects=EFFECT))(*sums, *lands, send_sems, recv_sems, after)
    return list(outs[:nt]), list(outs[nt:])


def pair_share(halves, name):
    nt = len(halves)

    def body(*refs):
        ins, outs = refs[:nt], refs[nt:2 * nt]
        send_sems, recv_sems = refs[2 * nt:]
        x, y, c, _ = _place()
        cps = []
        for t in range(nt):
            cps.append(_rcopy(ins[t], outs[t], send_sems, recv_sems, t, (x, y, 1 - c)))
            cps[-1].start()
        for cp in cps:
            cp.wait_recv()
        for cp in cps:
            cp.wait_send()

    return _pallas(body, name=name, in_specs=[ANY] * nt, out_specs=[ANY] * nt,
                   out_shape=[_sds(h.shape, h.dtype) for h in halves],
                   scratch_shapes=[pltpu.SemaphoreType.DMA((nt,)), pltpu.SemaphoreType.DMA((nt,))],
                   compiler_params=pltpu.CompilerParams(has_side_effects=True))(*halves)


def small_allreduce(pack, name):
    r = pack.shape[0]

    def body(in_ref, out_ref, buf, send_sems, recv_sems):
        x, y, c, _ = _place()
        me = 4 * x + 2 * y + c
        sends = []
        for k in range(1, 8):
            to = ((x + ((k >> 2) & 1)) % 2, (y + ((k >> 1) & 1)) % 2, (c + (k & 1)) % 2)
            cp = _rcopy(in_ref, buf.at[me], send_sems, recv_sems, k - 1, to)
            cp.start()
            sends.append((cp, to))
        buf[pl.ds(me, 1)] = in_ref[...][None]
        for k, (_, to) in enumerate(sends):
            peer = 4 * to[0] + 2 * to[1] + to[2]
            _rcopy(in_ref, buf.at[peer], send_sems, recv_sems, k, to).wait_recv()
        for cp, _ in sends:
            cp.wait_send()
        acc = buf[0]
        for d in range(1, 8):
            acc = acc + buf[d]
        out_ref[...] = acc

    vm = pl.BlockSpec(memory_space=pltpu.VMEM)
    return _pallas(body, name=name, in_specs=[vm], out_specs=vm, out_shape=_sds((r, HEAD), F32),
                   scratch_shapes=[pltpu.VMEM((8, r, HEAD), F32), pltpu.SemaphoreType.DMA((7,)),
                                   pltpu.SemaphoreType.DMA((7,))],
                   compiler_params=pltpu.CompilerParams(has_side_effects=True))(pack)


def add_halves(c_idx, grad, other, name):
    _, k, ns = grad.shape
    kh = k // 2
    tr = _tile(kh, max(16, (1 << 19) // ns), 16)
    nr = kh // tr

    def body(c_ref, g_ref, o_ref, s_ref):
        s_ref[...] = (g_ref[...].astype(F32) + o_ref[...].astype(F32)).astype(s_ref.dtype)

    gs = pltpu.PrefetchScalarGridSpec(
        num_scalar_prefetch=1, grid=(N_CHIPS, nr),
        in_specs=[pl.BlockSpec((None, tr, ns), lambda g, i, c_ref: (g, c_ref[0] * nr + i, 0)),
                  pl.BlockSpec((None, tr, ns), lambda g, i, c_ref: (g, i, 0))],
        out_specs=pl.BlockSpec((None, tr, ns), lambda g, i, c_ref: (g, i, 0)))
    return _pallas(body, name=name, grid_spec=gs, out_shape=_sds((N_CHIPS, kh, ns), XDT),
                   compiler_params=_params(("parallel", "parallel")))(c_idx, grad, other)


def add_chips(chip_idx, sums, recv, stack, layer, n_layers, name):
    _, kh, ns = sums.shape
    tr = _tile(kh, max(16, (1 << 19) // ns), 16)
    has_stack = stack is not None

    def body(k_ref, s_ref, r0, r1, r2, *rest):
        o_ref = rest[-1]
        o_ref[...] = ((s_ref[...].astype(F32) + r0[...].astype(F32)) + r1[...].astype(F32)) + r2[...].astype(F32)

    rspec = [pl.BlockSpec((None, tr, ns), functools.partial(lambda i, k_ref, j: (j, i, 0), j=j)) for j in range(3)]
    gs = pltpu.PrefetchScalarGridSpec(
        num_scalar_prefetch=1, grid=(kh // tr,),
        in_specs=[pl.BlockSpec((None, tr, ns), lambda i, k_ref: (k_ref[0], i, 0))] + rspec + ([ANY] if has_stack else []),
        out_specs=pl.BlockSpec((None, tr, ns), lambda i, k_ref: (layer, i, 0)))
    args = (chip_idx, sums, recv, recv, recv) + ((stack,) if has_stack else ())
    return _pallas(body, name=name, grid_spec=gs, out_shape=_sds((n_layers, kh, ns), F32),
                   input_output_aliases={5: 0} if has_stack else {},
                   compiler_params=_params(("parallel",)))(*args)


def adamw_big(c_idx, w, m, v, mine, other, name):
    nl, k, ns = w.shape
    kh = k // 2
    tr = _tile(kh, max(8, (1 << 18) // ns), 8)
    nr = kh // tr
    c1 = 1.0 - ADAM_B1 ** ADAM_STEP
    c2 = 1.0 - ADAM_B2 ** ADAM_STEP

    def body(c_ref, w_ref, m_ref, v_ref, a_ref, b_ref, g_ref, d_ref, mo_ref, vo_ref):
        gv = jnp.where(pl.program_id(2) == c_ref[0], a_ref[...], b_ref[...])
        mn = ADAM_B1 * m_ref[...] + (1.0 - ADAM_B1) * gv
        vn = ADAM_B2 * v_ref[...] + (1.0 - ADAM_B2) * (gv * gv)
        g_ref[...] = gv
        d_ref[...] = -ADAM_LR * ((mn / c1) / (jnp.sqrt(vn / c2) + ADAM_EPS) + ADAM_WD * w_ref[...])
        mo_ref[...] = mn
        vo_ref[...] = vn

    full = pl.BlockSpec((None, tr, ns), lambda l, i, hh, c_ref: (l, hh * nr + i, 0))
    half = pl.BlockSpec((None, tr, ns), lambda l, i, hh, c_ref: (l, i, 0))
    gs = pltpu.PrefetchScalarGridSpec(num_scalar_prefetch=1, grid=(nl, nr, 2),
                                      in_specs=[full, full, full, half, half], out_specs=[full] * 4)
    return _pallas(body, name=name, grid_spec=gs, out_shape=[_sds(w.shape, F32)] * 4,
                   compiler_params=_params(("parallel", "parallel", "arbitrary")))(c_idx, w, m, v, mine, other)


W_NAMES = ("w_in", "w_br_a", "w_br_b", "w_br_c", "w_o", "w_gate_up", "w_down")


def _rope_tables(n):
    half = HEAD // 2
    inv_freq = ROPE_THETA ** (-jnp.arange(half, dtype=F32) * 2.0 / HEAD)
    ang = jnp.arange(n, dtype=F32)[:, None] * inv_freq[None, :]
    cos, sin = jnp.cos(ang), jnp.sin(ang)
    return jnp.concatenate([cos, cos], axis=-1), jnp.concatenate([-sin, sin], axis=-1)


def _rpb_windows(rpb):
    pad = jnp.pad(rpb, ((0, 0), (0, 1), (0, GRID_W - rpb.shape[2])))
    wins = [pad[:, i0:i0 + C_WIN_ROWS].reshape(C_HEADS, 1, C_WIN_ROWS * GRID_W) for i0 in range(C_WIN_ROWS)]
    return jnp.stack(wins, axis=1)


def _rows(t):
    return t.reshape(-1, t.shape[-1])


def _like(t, ref):
    return t.reshape(ref.shape)


def layer_fwd(x, p, w, cos2, sin2, rest=None, mid=None):
    n, d = x.shape
    s = {"x": x}
    s["h"] = rmsnorm_fwd(x, p["norm1_g"], "norm1")
    s["proj"] = mm_x_wcol(s["h"], w["w_in"], ADT, "proj")
    gains = jnp.pad(p["qk_norm_g"], ((0, 2), (0, 0)))
    pp = s["pp"] = qk_prep(s["proj"], gains, cos2, sin2, "qk_prep")
    sink = p["sink_a"].reshape(1, A_Q_HEADS)
    s["oa"], s["lse_a"] = band_attn_fwd(pp["qa"], pp["ka"], pp["va"], sink, seqs=A_KV_HEADS, G=A_GROUP,
                                        nh=A_KV_HEADS, radius=A_RADIUS, name="attn_a")
    s["ob"], s["lse_b"] = [], []
    for g, dil in enumerate(B_DILS):
        o, lse = band_attn_fwd(_rows(pp[f"qb{g}"]), _rows(pp[f"kb{g}"]), _rows(pp[f"vb{g}"]), None, seqs=B_HG, G=1,
                               nh=B_HG, radius=B_RADIUS, classes=dil, name=f"attn_b{g}")
        s["ob"].append(_like(o, pp[f"qb{g}"]))
        s["lse_b"].append(_like(lse, pp[f"qb{g}"]))
    ob = b_combine_fwd(s["ob"], s["lse_b"], "b_combine")
    s["bias"] = rpb_expand(_rpb_windows(p["rpb_c"]), "rpb_expand")
    s["oc"], s["lse_c"] = c_attn_fwd(pp["qc"], pp["kc"], pp["vc"], s["bias"], "attn_c")
    s["o_in"] = (s["oa"], ob, s["oc"])
    if rest is not None:
        w = {**w, **rest(s["oc"])}
    s["w"] = w
    s["ys"] = [mm_x_wcol(o, w[k], ADT, "branch_" + k[-1]) for o, k in zip(s["o_in"], ("w_br_a", "w_br_b", "w_br_c"))]
    s["merged"] = gate_merge(s["proj"], s["ys"], "gate_merge")
    s["x_mid"] = mm_x_w(s["merged"], w["w_o"], "out_proj", res=x)
    s["h2"] = rmsnorm_fwd(s["x_mid"], p["norm2_g"], "norm2")
    token = None if mid is None else mid(s["h2"])
    s["gu"], s["act"] = gate_up_swiglu(s["h2"], w["w_gate_up"], "gate_up", after=token)
    x_out = mm_x_w(s["act"], w["w_down"], "down", res=s["x_mid"], tk_pref=2816)
    return x_out, s


def layer_bwd(dx_out, s, p, cos2, sin2, on_dws, early=False):
    dx_out, dx_out_c = dx_out
    n, d = dx_out.shape
    pp, w = s["pp"], s["w"]
    dgu = d_gate_up(dx_out_c, w["w_down"], s["gu"], "d_gate_up")
    dw_down = mm_aT_d(s["act"], dx_out_c, "dw_down")
    dh2 = mm_x_wcolT(dgu, w["w_gate_up"], "d_h2", stacked_in=2)
    dw_gu = mm_aT_d_wcol(s["h2"], dgu, "dw_gate_up", tn_pref=1408, stacked_in=2)
    dx_mid, dx_mid_c, dg2 = rmsnorm_bwd(s["x_mid"], p["norm2_g"], dh2, dx_out, "norm2_bwd")
    dws = {"w_gate_up": dw_gu, "w_down": dw_down.reshape(N_CHIPS, dw_down.shape[0] // N_CHIPS, d)}
    token = on_dws(dws) if early else None

    dmerged = mm_x_wT(dx_mid_c, w["w_o"], "d_merged", after=token)
    dw_o = mm_aT_d(s["merged"], dx_mid_c, "dw_o")
    dys, dproj = gate_bwd(s["proj"], s["ys"], dmerged, "gate_bwd")
    dos, dw_br = [], []
    for b, (o, k) in enumerate(zip(s["o_in"], ("w_br_a", "w_br_b", "w_br_c"))):
        dos.append(mm_x_wcolT(dys, w[k], "d_o_" + k[-1], lead=b))
        dw_br.append(mm_aT_d_wcol(o, dys, "dw_br_" + k[-1], lead=b))

    grads = {}
    sink = p["sink_a"].reshape(1, A_Q_HEADS)
    grads["qa"], grads["ka"], grads["va"], dsink = band_attn_bwd(
        pp["qa"], pp["ka"], pp["va"], sink, s["oa"], s["lse_a"], dos[0], None,
        seqs=A_KV_HEADS, G=A_GROUP, nh=A_KV_HEADS, radius=A_RADIUS, name="attn_a_bwd")
    dobs, dlses = b_combine_bwd(dos[1], s["ob"], s["lse_b"], "b_combine_bwd")
    for g, dil in enumerate(B_DILS):
        dq, dk, dv = band_attn_bwd(_rows(pp[f"qb{g}"]), _rows(pp[f"kb{g}"]), _rows(pp[f"vb{g}"]), None,
                                   _rows(s["ob"][g]), _rows(s["lse_b"][g]), _rows(dobs[g]), _rows(dlses[g]),
                                   seqs=B_HG, G=1, nh=B_HG, radius=B_RADIUS, classes=dil, name=f"attn_b{g}_bwd")
        grads[f"qb{g}"], grads[f"kb{g}"], grads[f"vb{g}"] = [_like(t, pp[f"qb{g}"]) for t in (dq, dk, dv)]
    grads["qc"], grads["kc"], grads["vc"], dbias = c_attn_bwd(pp["qc"], pp["kc"], pp["vc"], s["bias"], s["oc"],
                                                              s["lse_c"], dos[2], "attn_c_bwd")
    drpb = rpb_reduce(dbias, "rpb_reduce")[:, :2 * C_WIN_ROWS - 1, :2 * C_WIN_COLS - 1]
    gains = jnp.pad(p["qk_norm_g"], ((0, 2), (0, 0)))
    dproj, dgains = qk_prep_bwd(s["proj"], gains, cos2, sin2, grads, dproj, "qk_prep_bwd")
    dw_in = mm_aT_d_wcol(s["h"], dproj, "dw_in")
    rest = {"w_in": dw_in, "w_br_a": dw_br[0], "w_br_b": dw_br[1], "w_br_c": dw_br[2],
            "w_o": dw_o.reshape(N_CHIPS, d // N_CHIPS, d)}
    token = on_dws(rest if early else {**rest, **dws})
    dh = mm_x_wcolT(dproj, w["w_in"], "d_h", after=token)
    dx_in, dx_in_c, dg1 = rmsnorm_bwd(s["x"], p["norm1_g"], dh, dx_mid, "norm1_bwd")
    small = {"norm1_g": dg1[0], "qk_norm_g": dgains[:6], "sink_a": dsink[0, :, 0],
             "rpb_c": drpb, "norm2_g": dg2[0]}
    return (dx_in, dx_in_c), small


SMALL_NAMES = ("norm1_g", "qk_norm_g", "sink_a", "rpb_c", "norm2_g")


def _pack_small(parts, extra=None):
    flat = [parts[k].reshape(-1) for k in SMALL_NAMES]
    flat.append(jnp.zeros((1,), F32) if extra is None else extra.reshape(1))
    v = jnp.concatenate(flat)
    rows = -(-v.shape[0] // (8 * HEAD)) * 8
    return jnp.pad(v, (0, rows * HEAD - v.shape[0])).reshape(rows, HEAD)


def _unpack_small(pack, like):
    v = pack.reshape(-1)
    out, off = {}, 0
    for k in SMALL_NAMES:
        size = math.prod(like[k].shape)
        out[k] = v[off:off + size].reshape(like[k].shape)
        off += size
    return out, v[off]


def kernel(x, norm1_g, w_in, qk_norm_g, sink_a, rpb_c, w_br_a, w_br_b, w_br_c, w_o, norm2_g, w_gate_up, w_down, loss_target, m_norm1_g, m_w_in, m_qk_norm_g, m_sink_a, m_rpb_c, m_w_br_a, m_w_br_b, m_w_br_c, m_w_o, m_norm2_g, m_w_gate_up, m_w_down, v_norm1_g, v_w_in, v_qk_norm_g, v_sink_a, v_rpb_c, v_w_br_a, v_w_br_b, v_w_br_c, v_w_o, v_norm2_g, v_w_gate_up, v_w_down):
    big = dict(w_in=w_in, w_br_a=w_br_a, w_br_b=w_br_b, w_br_c=w_br_c, w_o=w_o, w_gate_up=w_gate_up, w_down=w_down)
    big_m = dict(w_in=m_w_in, w_br_a=m_w_br_a, w_br_b=m_w_br_b, w_br_c=m_w_br_c, w_o=m_w_o, w_gate_up=m_w_gate_up, w_down=m_w_down)
    big_v = dict(w_in=v_w_in, w_br_a=v_w_br_a, w_br_b=v_w_br_b, w_br_c=v_w_br_c, w_o=v_w_o, w_gate_up=v_w_gate_up, w_down=v_w_down)
    small = dict(norm1_g=norm1_g, qk_norm_g=qk_norm_g, sink_a=sink_a, rpb_c=rpb_c, norm2_g=norm2_g)
    small_m = dict(norm1_g=m_norm1_g, qk_norm_g=m_qk_norm_g, sink_a=m_sink_a, rpb_c=m_rpb_c, norm2_g=m_norm2_g)
    small_v = dict(norm1_g=v_norm1_g, qk_norm_g=v_qk_norm_g, sink_a=v_sink_a, rpb_c=v_rpb_c, norm2_g=v_norm2_g)
    n_layers = w_in.shape[0]
    n, d = x.shape[1], x.shape[2]
    c_idx = lax.axis_index("c").astype(jnp.int32).reshape(1)
    chip_idx = (2 * lax.axis_index("x") + lax.axis_index("y")).astype(jnp.int32).reshape(1)
    cos2, sin2 = _rope_tables(n)

    def as_weights(names, got):
        w = dict(zip(names, got))
        if "w_o" in w:
            w["w_o"] = w["w_o"].reshape(d, d)
            w["w_down"] = w["w_down"].reshape(-1, d)
        return w

    def gathered(names, started, after):
        send_sems, recv_sems, bufs, _ = started
        return as_weights(names, pair_forward(gather_wait(bufs, send_sems, recv_sems, after, "gather_wait"), "pair_forward"))

    def placed(names, l):
        return [cast_place(chip_idx, big[k], l, "cast_" + k) for k in names]

    first, others_0 = W_NAMES[:1], W_NAMES[1:]
    started = gather_start(placed(first, 0), chip_idx, "gather_start")
    later = [placed(W_NAMES, l) for l in range(1, n_layers)]
    started_rest = gather_start(placed(others_0, 0), [started[3]] + [b for bufs in later for b in bufs], "gather_start")
    weights = gathered(first, started, started_rest[3])
    rest = lambda after: gathered(others_0, started_rest, after)

    xs, saved = x[0], []
    for l in range(n_layers):
        p = {k: small[k][l] for k in SMALL_NAMES}
        mid, fwd = None, {}
        if l + 1 < n_layers:
            started = gather_start(later[l], weights["w_in"], "gather_start")
            p["norm1_g"] = p["norm1_g"] + started[3][0, 0]

            def mid(after, started=started, fwd=fwd):
                send_sems, recv_sems, bufs, _ = started
                arrived = gather_wait(bufs, send_sems, recv_sems, after, "gather_wait")
                fwd["started"] = forward_start(arrived, after, "forward_start")
                return fwd["started"][3]

        xs, s = layer_fwd(xs, p, weights, cos2, sin2, rest, mid if l > 0 else None)
        saved.append(s)
        if l + 1 < n_layers and l > 0:
            send_sems, recv_sems, bufs, _ = fwd["started"]
            weights, rest = as_weights(W_NAMES, forward_wait(bufs, send_sems, recv_sems, xs, "forward_wait")), None
        elif l + 1 < n_layers:
            weights, rest = gathered(W_NAMES, started, xs), None
    loss_tile, *dx = loss_head(xs, loss_target[0], "loss_head")

    halves = {k: None for k in W_NAMES}
    small_g = [None] * n_layers
    pending = []

    def finish_exchanges(after):
        for l, names, (send_sems, recv_sems, sums, lands, _) in pending:
            sums, from_chips = exchange_wait(sums, lands, send_sems, recv_sems, after, "exchange_wait")
            for k, sm, r in zip(names, sums, from_chips):
                halves[k] = add_chips(chip_idx, sm, r, halves[k], l, n_layers, "add_chips_" + k)
        pending.clear()

    def make_on_dws(l, wait_first):
        def on_dws(dws):
            names = [k for k in W_NAMES if k in dws]
            parts = [dws[k] for k in names]
            if wait_first:
                finish_exchanges(parts[0])
            from_sibling = pair_exchange(parts, "pair_exchange")
            sums = [add_halves(c_idx, g, o, "add_halves_" + k) for g, o, k in zip(parts, from_sibling, names)]
            pending.append((l, names, exchange_start(sums, "exchange_start")))
            return pending[-1][2][4]
        return on_dws

    for l in reversed(range(n_layers)):
        p = {k: small[k][l] for k in SMALL_NAMES}
        last = l == 0
        dx, small_g[l] = layer_bwd(dx, saved[l], p, cos2, sin2, make_on_dws(l, not last), early=last)
    dx = dx[0]
    finish_exchanges(dx)
    halves = [halves[k] for k in W_NAMES]
    others = pair_share(halves, "pair_share")

    mine = {k: jnp.stack([small_g[l][k] for l in range(n_layers)]) for k in SMALL_NAMES}
    total = small_allreduce(_pack_small(mine, loss_tile[0, 0]), "small_allreduce")
    grad_small, loss = _unpack_small(total, small)

    outs = {}
    for k, mine_half, other_half in zip(W_NAMES, halves, others):
        outs[k] = adamw_big(c_idx, big[k], big_m[k], big_v[k], mine_half, other_half, "adamw_" + k)
    res = adamw(_pack_small(small), _pack_small(grad_small), _pack_small(small_m), _pack_small(small_v), "adamw_small")
    unp = [_unpack_small(t, small)[0] for t in res]
    for k in SMALL_NAMES:
        outs[k] = (grad_small[k],) + tuple(u[k] for u in unp)

    order = ("norm1_g", "w_in", "qk_norm_g", "sink_a", "rpb_c", "w_br_a", "w_br_b", "w_br_c", "w_o", "norm2_g",
             "w_gate_up", "w_down")
    return (loss, dx[None]) + tuple(outs[k][i] for i in range(4) for k in order)
```

```python
import functools
import math

import jax
import jax.numpy as jnp
from jax import lax
from jax.experimental import pallas as pl
from jax.experimental.pallas import tpu as pltpu

F32 = jnp.float32
CDT = jnp.bfloat16
XDT = jnp.bfloat16
ADT = jnp.bfloat16

HEAD = 128
NORM_EPS = 1e-6
ROPE_THETA = 10000.0
A_Q_HEADS, A_KV_HEADS, A_GROUP, A_RADIUS = 8, 2, 4, 128
B_DILS = (1, 4, 16)
B_RADIUS = 64
B_HG = 4
C_HEADS, GRID_W, C_WIN_ROWS, C_WIN_COLS = 8, 64, 8, 16
QKV_W = 9216
COL = dict(qa=0, ka=1024, va=1280, qb=1536, kb=3072, vb=4608, qc=6144, kc=7168, vc=8192)
NEG = -1e30
SCALE = HEAD ** -0.5
N_CHIPS = 4

ADAM_LR, ADAM_B1, ADAM_B2, ADAM_EPS, ADAM_WD, ADAM_STEP = 0.001, 0.9, 0.999, 1e-08, 0.01, 10

VMEM_LIMIT = 56 * 1024 * 1024
MESH = pl.DeviceIdType.MESH


def _pallas(body, **kw):
    return pl.pallas_call(body, **kw)


def _params(sem=None, **kw):
    if sem is not None:
        kw["dimension_semantics"] = sem
    return pltpu.CompilerParams(vmem_limit_bytes=VMEM_LIMIT, **kw)


def _tile(dim, pref, mult=128):
    best = None
    for t in range(mult, min(dim, pref) + 1, mult):
        if dim % t == 0:
            best = t
    return dim if best is None else best


def _sds(shape, dtype):
    return jax.ShapeDtypeStruct(tuple(shape), dtype)


_DIMS = {"nn": (((1,), (0,)), ((), ())), "nt": (((1,), (1,)), ((), ())), "tn": (((0,), (0,)), ((), ()))}


def _matmul(a, b, *, mode, grid, a_spec, b_spec, o_spec, out_shape, acc_shape, name, res=None, res_spec=None,
            after=None):
    nk = grid[2]
    has_res = res is not None
    n_in = 2 + int(has_res) + int(after is not None)

    def body(*refs):
        a_ref, b_ref = refs[:2]
        r_ref = refs[2] if has_res else None
        o_ref, rest = refs[n_in], refs[n_in + 1:]
        p = lax.dot_general(a_ref[...].astype(CDT), b_ref[...].astype(CDT), _DIMS[mode],
                            preferred_element_type=F32)

        def finish(acc):
            if has_res:
                acc = acc + r_ref[...].astype(F32)
            o_ref[...] = acc.astype(o_ref.dtype)

        if nk == 1:
            finish(p)
        else:
            acc_ref = rest[0]
            k = pl.program_id(2)

            @pl.when(k == 0)
            def _():
                acc_ref[...] = p

            @pl.when(k > 0)
            def _():
                acc_ref[...] += p

            @pl.when(k == nk - 1)
            def _():
                finish(acc_ref[...])

    in_specs = [a_spec, b_spec] + ([res_spec] if has_res else [])
    args = (a, b) + ((res,) if has_res else ())
    if after is not None:
        in_specs.append(pl.BlockSpec(after.shape, lambda i, j, kk: (0, 0)))
        args += (after,)
    scratch = [] if nk == 1 else [pltpu.VMEM(acc_shape, F32)]
    return _pallas(body, name=name, grid=grid, in_specs=in_specs, out_specs=o_spec, out_shape=out_shape,
                   scratch_shapes=scratch, compiler_params=_params(("parallel", "parallel", "arbitrary")))(*args)


def mm_x_wcol(a, wg, out_dtype, name, tm_pref=1024, tn_pref=1024, stacked_out=1):
    m, k = a.shape
    ns = wg.shape[2]
    tm, tn = _tile(m, tm_pref, 8), _tile(ns, tn_pref)
    nj = ns // tn
    grid = (m // tm, N_CHIPS * nj, 1)
    a_spec = pl.BlockSpec((tm, k), lambda i, j, kk: (i, 0))
    b_spec = pl.BlockSpec((None, k, tn), lambda i, j, kk: (j // nj, 0, j % nj))
    if stacked_out == 1:
        o_spec = pl.BlockSpec((tm, tn), lambda i, j, kk: (i, j))
        out_shape = _sds((m, N_CHIPS * ns), out_dtype)
    else:
        per = N_CHIPS * nj // stacked_out
        o_spec = pl.BlockSpec((None, tm, tn), lambda i, j, kk: (j // per, i, j % per))
        out_shape = _sds((stacked_out, m, N_CHIPS * ns // stacked_out), out_dtype)
    return _matmul(a, wg, mode="nn", grid=grid, a_spec=a_spec, b_spec=b_spec, o_spec=o_spec,
                   out_shape=out_shape, acc_shape=(tm, tn), name=name)


def mm_x_wcolT(d, wg, name, res=None, tm_pref=1024, tn_pref=512, tk_pref=4096, stacked_in=1, lead=None, after=None):
    kdim, ns = wg.shape[1], wg.shape[2]
    m = d.shape[-2]
    tm, tn, tk = _tile(m, tm_pref, 8), _tile(kdim, tn_pref), _tile(ns, tk_pref)
    nkk = ns // tk
    grid = (m // tm, kdim // tn, N_CHIPS * nkk)
    if lead is not None:
        a_spec = pl.BlockSpec((None, tm, tk), lambda i, j, kk: (lead, i, kk))
    elif stacked_in == 1:
        a_spec = pl.BlockSpec((tm, tk), lambda i, j, kk: (i, kk))
    else:
        per = N_CHIPS * nkk // stacked_in
        a_spec = pl.BlockSpec((None, tm, tk), lambda i, j, kk: (kk // per, i, kk % per))
    b_spec = pl.BlockSpec((None, tn, tk), lambda i, j, kk: (kk // nkk, j, kk % nkk))
    o_spec = pl.BlockSpec((tm, tn), lambda i, j, kk: (i, j))
    return _matmul(d, wg, mode="nt", grid=grid, a_spec=a_spec, b_spec=b_spec, o_spec=o_spec,
                   out_shape=_sds((m, kdim), F32), acc_shape=(tm, tn), name=name,
                   res=res, res_spec=None if res is None else o_spec, after=after)


def mm_aT_d_wcol(a, d, name, tm_pref=512, tk_pref=4096, tn_pref=1024, stacked_in=1, lead=None):
    m, kdim = a.shape
    ntot = d.shape[-1] * stacked_in
    ns = ntot // N_CHIPS
    tm, tkm, tn = _tile(kdim, tm_pref), _tile(m, tk_pref, 8), _tile(ns, tn_pref)
    nj = ns // tn
    grid = (kdim // tm, N_CHIPS * nj, m // tkm)
    a_spec = pl.BlockSpec((tkm, tm), lambda i, j, kk: (kk, i))
    if lead is not None:
        b_spec = pl.BlockSpec((None, tkm, tn), lambda i, j, kk: (lead, kk, j))
    elif stacked_in == 1:
        b_spec = pl.BlockSpec((tkm, tn), lambda i, j, kk: (kk, j))
    else:
        per = N_CHIPS * nj // stacked_in
        b_spec = pl.BlockSpec((None, tkm, tn), lambda i, j, kk: (j // per, kk, j % per))
    o_spec = pl.BlockSpec((None, tm, tn), lambda i, j, kk: (j // nj, i, j % nj))
    return _matmul(a, d, mode="tn", grid=grid, a_spec=a_spec, b_spec=b_spec, o_spec=o_spec,
                   out_shape=_sds((N_CHIPS, kdim, ns), XDT), acc_shape=(tm, tn), name=name)


def mm_x_w(a, w, name, res=None, out_dtype=F32, tm_pref=1024, tn_pref=1024, tk_pref=2048):
    m, k = a.shape
    n = w.shape[1]
    tm, tn, tk = _tile(m, tm_pref, 8), _tile(n, tn_pref), _tile(k, tk_pref)
    grid = (m // tm, n // tn, k // tk)
    o_spec = pl.BlockSpec((tm, tn), lambda i, j, kk: (i, j))
    return _matmul(a, w, mode="nn", grid=grid,
                   a_spec=pl.BlockSpec((tm, tk), lambda i, j, kk: (i, kk)),
                   b_spec=pl.BlockSpec((tk, tn), lambda i, j, kk: (kk, j)),
                   o_spec=o_spec, out_shape=_sds((m, n), out_dtype), acc_shape=(tm, tn), name=name,
                   res=res, res_spec=None if res is None else o_spec)


def mm_x_wT(d, w, name, out_dtype=F32, tm_pref=1024, tn_pref=1024, after=None):
    m, n = d.shape
    k = w.shape[0]
    tm, tn = _tile(m, tm_pref, 8), _tile(k, tn_pref)
    grid = (m // tm, k // tn, 1)
    return _matmul(d, w, mode="nt", grid=grid,
                   a_spec=pl.BlockSpec((tm, n), lambda i, j, kk: (i, 0)),
                   b_spec=pl.BlockSpec((tn, n), lambda i, j, kk: (j, 0)),
                   o_spec=pl.BlockSpec((tm, tn), lambda i, j, kk: (i, j)),
                   out_shape=_sds((m, k), out_dtype), acc_shape=(tm, tn), name=name, after=after)


def mm_aT_d(a, d, name, tm_pref=512, tn_pref=512, tk_pref=4096):
    m, k = a.shape
    n = d.shape[1]
    tm, tn, tk = _tile(k, tm_pref), _tile(n, tn_pref), _tile(m, tk_pref, 8)
    grid = (k // tm, n // tn, m // tk)
    return _matmul(a, d, mode="tn", grid=grid,
                   a_spec=pl.BlockSpec((tk, tm), lambda i, j, kk: (kk, i)),
                   b_spec=pl.BlockSpec((tk, tn), lambda i, j, kk: (kk, j)),
                   o_spec=pl.BlockSpec((tm, tn), lambda i, j, kk: (i, j)),
                   out_shape=_sds((k, n), XDT), acc_shape=(tm, tn), name=name)


def rmsnorm_fwd(x, g, name):
    n, d = x.shape
    tm = _tile(n, 512, 8)

    def body(x_ref, g_ref, h_ref):
        xv = x_ref[...]
        r = lax.rsqrt(jnp.mean(xv * xv, axis=-1, keepdims=True) + NORM_EPS)
        h_ref[...] = (xv * r * g_ref[...]).astype(h_ref.dtype)

    return _pallas(body, name=name, grid=(n // tm,),
                   in_specs=[pl.BlockSpec((tm, d), lambda i: (i, 0)), pl.BlockSpec((1, d), lambda i: (0, 0))],
                   out_specs=pl.BlockSpec((tm, d), lambda i: (i, 0)), out_shape=_sds((n, d), CDT),
                   compiler_params=_params(("parallel",)))(x, g.reshape(1, d))


def rmsnorm_bwd(x, g, dh, dres, name):
    n, d = x.shape
    tm = _tile(n, 256, 8)

    def body(x_ref, g_ref, dh_ref, dres_ref, dx_ref, dxc_ref, dg_ref):
        xv = x_ref[...]
        r = lax.rsqrt(jnp.mean(xv * xv, axis=-1, keepdims=True) + NORM_EPS)
        dhv = dh_ref[...]
        u = dhv * g_ref[...]
        c = jnp.mean(xv * u, axis=-1, keepdims=True)
        dxv = dres_ref[...] + r * u - xv * (r * r * r * c)
        dx_ref[...] = dxv
        dxc_ref[...] = dxv.astype(dxc_ref.dtype)
        part = jnp.broadcast_to(jnp.sum(dhv * xv * r, axis=0, keepdims=True), (8, d))

        @pl.when(pl.program_id(0) == 0)
        def _():
            dg_ref[...] = part

        @pl.when(pl.program_id(0) > 0)
        def _():
            dg_ref[...] += part

    row = pl.BlockSpec((tm, d), lambda i: (i, 0))
    return _pallas(body, name=name, grid=(n // tm,),
                   in_specs=[row, pl.BlockSpec((1, d), lambda i: (0, 0)), row, row],
                   out_specs=[row, row, pl.BlockSpec((8, d), lambda i: (0, 0))],
                   out_shape=[_sds((n, d), F32), _sds((n, d), CDT), _sds((8, d), F32)],
                   compiler_params=_params(("arbitrary",)))(x, g.reshape(1, d), dh, dres)


def _norm_rope(xh, g, cos2, sin2):
    r = lax.rsqrt(jnp.mean(xh * xh, axis=-1, keepdims=True) + NORM_EPS)
    y = xh * r * g
    if cos2 is not None:
        y = y * cos2 + pltpu.roll(y, HEAD // 2, 1) * sin2
    return y


def _norm_rope_bwd(xh, g, cos2, sin2, dout):
    if cos2 is not None:
        dy = dout * cos2 + pltpu.roll(dout * sin2, HEAD // 2, 1)
    else:
        dy = dout
    r = lax.rsqrt(jnp.mean(xh * xh, axis=-1, keepdims=True) + NORM_EPS)
    u = dy * g
    c = jnp.mean(xh * u, axis=-1, keepdims=True)
    return r * u - xh * (r * r * r * c), dy * xh * r


_QK_GROUPS = (("qa", COL["qa"], 8, 0, True), ("ka", COL["ka"], 2, 1, True),
              ("qb", COL["qb"], 12, 2, True), ("kb", COL["kb"], 12, 3, True),
              ("qc", COL["qc"], 8, 4, False), ("kc", COL["kc"], 8, 5, False))
_V_GROUPS = (("va", COL["va"], 2), ("vb", COL["vb"], 12), ("vc", COL["vc"], 8))
_PREP_OUT = (("qa", 8), ("ka", 2), ("va", 2)) + tuple((f"{t}b{g}", 4) for t in "qkv" for g in range(3)) + (
    ("qc", 8), ("kc", 8), ("vc", 8))


def _prep_src(name):
    if name[1] == "b":
        base = COL[name[0] + "b"] + int(name[2]) * B_HG * HEAD
        gain = {"q": 2, "k": 3, "v": None}[name[0]]
        return base, gain, name[0] != "v"
    base = COL[name]
    gain = {"qa": 0, "ka": 1, "va": None, "qc": 4, "kc": 5, "vc": None}[name]
    return base, gain, name in ("qa", "ka")


def _prep_dil(name):
    return B_DILS[int(name[2])] if name[1] == "b" else 1


def _to_classes(val, scr, dil):
    scr[...] = val
    return [scr[pl.ds(r, val.shape[0] // dil, stride=dil), :] for r in range(dil)]


def _from_classes(parts, scr):
    for r, part in enumerate(parts):
        scr[pl.ds(r, part.shape[0], stride=len(parts)), :] = part
    return scr[...]


def _class_spec(tm, dil, width):
    if dil == 1:
        return pl.BlockSpec((tm, width), lambda i: (i, 0))
    return pl.BlockSpec((dil, tm // dil, width), lambda i: (0, i, 0))


def _class_shape(n, dil, width):
    return (n, width) if dil == 1 else (dil, n // dil, width)


def qk_prep(proj, gains, cos2, sin2, name):
    n = proj.shape[0]
    tm = _tile(n, 256, 8)

    def body(p_ref, g_ref, c_ref, s_ref, *refs):
        outs, scr = refs[:-1], refs[-1]
        cos2v, sin2v = c_ref[...], s_ref[...]
        for (nm, heads), o_ref in zip(_PREP_OUT, outs):
            base, gain, rope = _prep_src(nm)
            dil = _prep_dil(nm)
            for h in range(heads):
                hs = slice(h * HEAD, (h + 1) * HEAD)
                xh = p_ref[:, base + h * HEAD: base + (h + 1) * HEAD].astype(F32)
                if gain is None:
                    y = xh
                else:
                    y = _norm_rope(xh, g_ref[gain:gain + 1, :], cos2v if rope else None, sin2v if rope else None)
                if dil == 1:
                    o_ref[:, hs] = y.astype(o_ref.dtype)
                else:
                    for r, part in enumerate(_to_classes(y, scr.at[h % 4], dil)):
                        o_ref[r, :, hs] = part.astype(o_ref.dtype)

    tab = pl.BlockSpec((tm, HEAD), lambda i: (i, 0))
    outs = _pallas(body, name=name, grid=(n // tm,),
                   in_specs=[pl.BlockSpec((tm, QKV_W), lambda i: (i, 0)), pl.BlockSpec((8, HEAD), lambda i: (0, 0)), tab, tab],
                   out_specs=[_class_spec(tm, _prep_dil(nm), h * HEAD) for nm, h in _PREP_OUT],
                   out_shape=[_sds(_class_shape(n, _prep_dil(nm), h * HEAD), CDT) for nm, h in _PREP_OUT],
                   scratch_shapes=[pltpu.VMEM((4, tm, HEAD), F32)],
                   compiler_params=_params(("parallel",)))(proj, gains, cos2, sin2)
    return dict(zip([nm for nm, _ in _PREP_OUT], outs))


def qk_prep_bwd(proj, gains, cos2, sin2, grads, dproj, name):
    n = proj.shape[0]
    tm = _tile(n, 128, 8)
    names = [nm for nm, _ in _PREP_OUT]

    def body(p_ref, g_ref, c_ref, s_ref, *refs):
        g_refs, dp_ref, dg_ref, scr = refs[:len(names)], refs[len(names) + 1], refs[len(names) + 2], refs[-1]
        cos2v, sin2v = c_ref[...], s_ref[...]
        dg = [jnp.zeros((tm, HEAD), F32) for _ in range(6)]
        for (nm, heads), gr in zip(_PREP_OUT, g_refs):
            base, gain, rope = _prep_src(nm)
            dil = _prep_dil(nm)
            for h in range(heads):
                sl = slice(base + h * HEAD, base + (h + 1) * HEAD)
                hs = slice(h * HEAD, (h + 1) * HEAD)
                dout = gr[:, hs] if dil == 1 else _from_classes([gr[r, :, hs] for r in range(dil)], scr.at[h % 4])
                if gain is None:
                    dx = dout
                else:
                    dx, dgr = _norm_rope_bwd(p_ref[:, sl].astype(F32), g_ref[gain:gain + 1, :], cos2v if rope else None,
                                             sin2v if rope else None, dout)
                    dg[gain] = dg[gain] + dgr
                dp_ref[:, sl] = dx.astype(dp_ref.dtype)
        part = jnp.concatenate([jnp.sum(t, axis=0, keepdims=True) for t in dg] + [jnp.zeros((2, HEAD), F32)], axis=0)

        @pl.when(pl.program_id(0) == 0)
        def _():
            dg_ref[...] = part

        @pl.when(pl.program_id(0) > 0)
        def _():
            dg_ref[...] += part

    tab = pl.BlockSpec((tm, HEAD), lambda i: (i, 0))
    dp, dg = _pallas(body, name=name, grid=(n // tm,),
                     in_specs=[pl.BlockSpec((tm, QKV_W), lambda i: (i, 0)), pl.BlockSpec((8, HEAD), lambda i: (0, 0)), tab, tab]
                     + [_class_spec(tm, _prep_dil(nm), h * HEAD) for nm, h in _PREP_OUT] + [ANY],
                     out_specs=[pl.BlockSpec((tm, QKV_W), lambda i: (i, 0)), pl.BlockSpec((8, HEAD), lambda i: (0, 0))],
                     out_shape=[_sds(dproj.shape, dproj.dtype), _sds((8, HEAD), F32)],
                     input_output_aliases={4 + len(names): 0},
                     scratch_shapes=[pltpu.VMEM((4, tm, HEAD), F32)],
                     compiler_params=_params(("arbitrary",)))(proj, gains, cos2, sin2, *[grads[k] for k in names], dproj)
    return dp, dg


def _band_geometry(m, bq_pref, radius):
    bq = min(bq_pref, m)
    return bq, min(bq + 2 * radius, m)


def _band_window(i, bq, radius, m, w):
    start = pl.multiple_of(jnp.clip(i * bq - radius, 0, m - w), 64)
    qpos = i * bq + lax.broadcasted_iota(jnp.int32, (bq, w), 0)
    kpos = start + lax.broadcasted_iota(jnp.int32, (bq, w), 1)
    return start, jnp.abs(kpos - qpos) <= radius


def _band_specs(m, bq, G, nh, seqs):
    lg, nb = seqs // nh, m // bq
    qspec = pl.BlockSpec((bq, nh * G * HEAD), lambda s, i: ((s // lg) * nb + i, s % lg))
    kspec = pl.BlockSpec((m, nh * HEAD), lambda s, i: (s // lg, s % lg))
    return qspec, kspec, lg


def band_attn_fwd(q, k, v, sink, *, seqs, G, nh, radius, name, classes=1, bq_pref=128):
    m = q.shape[0] // classes
    bq, w = _band_geometry(m, bq_pref, radius)
    has_sink = sink is not None

    def body(*refs):
        if has_sink:
            sink_ref, q_ref, k_ref, v_ref, o_ref, lse_ref = refs
        else:
            q_ref, k_ref, v_ref, o_ref, lse_ref = refs
        s_id, i = pl.program_id(0), pl.program_id(1)
        start, valid = _band_window(i, bq, radius, m, w)
        units = [(h, g) for h in range(nh) for g in range(G)]
        sls = [slice((h * G + g) * HEAD, (h * G + g + 1) * HEAD) for h, g in units]
        k_ts = [k_ref[pl.ds(start, w), h * HEAD:(h + 1) * HEAD] for h in range(nh)]
        v_ts = [v_ref[pl.ds(start, w), h * HEAD:(h + 1) * HEAD] for h in range(nh)]
        q_ts = [q_ref[:, sl] for sl in sls]
        sks = [sink_ref[0, (s_id * nh + h) * G + g] for h, g in units] if has_sink else None
        ss = [jnp.where(valid, lax.dot_general(q_t, k_ts[h], _DIMS["nt"], preferred_element_type=F32) * SCALE, NEG)
              for q_t, (h, g) in zip(q_ts, units)]
        mxs = [jnp.max(s, axis=-1, keepdims=True) for s in ss]
        if has_sink:
            mxs = [jnp.maximum(mx, sk) for mx, sk in zip(mxs, sks)]
        ps = [jnp.exp(s - mx) for s, mx in zip(ss, mxs)]
        dens = [jnp.sum(p, axis=-1, keepdims=True) for p in ps]
        if has_sink:
            dens = [den + jnp.exp(sk - mx) for den, sk, mx in zip(dens, sks, mxs)]
        outs = [jnp.dot((p / den).astype(CDT), v_ts[h], preferred_element_type=F32)
                for p, den, (h, g) in zip(ps, dens, units)]
        for sl, o, mx, den in zip(sls, outs, mxs, dens):
            o_ref[:, sl] = o
            lse_ref[:, sl] = jnp.broadcast_to(mx + jnp.log(den), (bq, HEAD))

    qspec, kspec, lg = _band_specs(m, bq, G, nh, seqs)
    in_specs = ([pl.BlockSpec(memory_space=pltpu.SMEM)] if has_sink else []) + [qspec, kspec, kspec]
    args = ((sink,) if has_sink else ()) + (q, k, v)
    return _pallas(body, name=name, grid=(classes * lg, m // bq), in_specs=in_specs, out_specs=[qspec, qspec],
                   out_shape=[_sds(q.shape, F32), _sds(q.shape, F32)],
                   compiler_params=_params(("parallel", "arbitrary")))(*args)


def band_attn_bwd(q, k, v, sink, o, lse, do, dlse, *, seqs, G, nh, radius, name, classes=1, bq_pref=128):
    m = q.shape[0] // classes
    bq, w = _band_geometry(m, bq_pref, radius)
    has_sink, has_dlse = sink is not None, dlse is not None
    assert nh * G <= 8

    def body(*refs):
        refs = list(refs)
        sink_ref = refs.pop(0) if has_sink else None
        q_ref, k_ref, v_ref, o_ref, lse_ref, do_ref = refs[:6]
        refs = refs[6:]
        dlse_ref = refs.pop(0) if has_dlse else None
        dq_ref, dk_ref, dv_ref = refs[:3]
        dsink_ref = refs[3] if has_sink else None
        s_id, i = pl.program_id(0), pl.program_id(1)

        @pl.when(i == 0)
        def _():
            dk_ref[...] = jnp.zeros_like(dk_ref)
            dv_ref[...] = jnp.zeros_like(dv_ref)
            if has_sink:
                dsink_ref[...] = jnp.zeros_like(dsink_ref)

        start, valid = _band_window(i, bq, radius, m, w)
        units = [(h, g) for h in range(nh) for g in range(G)]
        sls = [slice((h * G + g) * HEAD, (h * G + g + 1) * HEAD) for h, g in units]
        kss = [slice(h * HEAD, (h + 1) * HEAD) for h in range(nh)]
        k_ts = [k_ref[pl.ds(start, w), ks] for ks in kss]
        v_ts = [v_ref[pl.ds(start, w), ks] for ks in kss]
        q_ts = [q_ref[:, sl] for sl in sls]
        lse_ts = [lse_ref[:, sl][:, :1] for sl in sls]
        do_ts = [do_ref[:, sl] for sl in sls]
        deltas = [jnp.sum(do_t * o_ref[:, sl], axis=-1, keepdims=True) for do_t, sl in zip(do_ts, sls)]
        dlse_ts = [dlse_ref[:, sl][:, :1] for sl in sls] if has_dlse else None
        dk_old = [dk_ref[pl.ds(start, w), ks] for ks in kss]
        dv_old = [dv_ref[pl.ds(start, w), ks] for ks in kss]
        dsink_old = dsink_ref[...] if has_sink else None

        ps = [jnp.exp(jnp.where(valid, lax.dot_general(q_t, k_ts[h], _DIMS["nt"], preferred_element_type=F32) * SCALE, NEG)
                      - lse_t) for q_t, lse_t, (h, g) in zip(q_ts, lse_ts, units)]
        do_cs = [do_t.astype(CDT) for do_t in do_ts]
        dps = [lax.dot_general(do_c, v_ts[h], _DIMS["nt"], preferred_element_type=F32) for do_c, (h, g) in zip(do_cs, units)]
        ts = [dp - delta for dp, delta in zip(dps, deltas)]
        if has_dlse:
            ts = [t + dl for t, dl in zip(ts, dlse_ts)]
        dss = [((p * t) * SCALE).astype(CDT) for p, t in zip(ps, ts)]
        dqs = [jnp.dot(ds, k_ts[h], preferred_element_type=F32) for ds, (h, g) in zip(dss, units)]
        dvs = [lax.dot_general(p.astype(CDT), do_c, _DIMS["tn"], preferred_element_type=F32) for p, do_c in zip(ps, do_cs)]
        dks = [lax.dot_general(ds, q_t, _DIMS["tn"], preferred_element_type=F32) for ds, q_t in zip(dss, q_ts)]
        dk_new = [dk_old[h] + sum(dks[h * G + g] for g in range(G)) for h in range(nh)]
        dv_new = [dv_old[h] + sum(dvs[h * G + g] for g in range(G)) for h in range(nh)]
        if has_sink:
            rows = []
            for (h, g), lse_t, delta in zip(units, lse_ts, deltas):
                sk = sink_ref[0, (s_id * nh + h) * G + g]
                rows.append(jnp.broadcast_to(-jnp.sum(jnp.exp(sk - lse_t) * delta, axis=0, keepdims=True), (1, HEAD)))
            rows += [jnp.zeros((1, HEAD), F32)] * (8 - len(rows))
            dsink_new = dsink_old + jnp.concatenate(rows, axis=0)

        for sl, dq in zip(sls, dqs):
            dq_ref[:, sl] = dq
        for h, ks in enumerate(kss):
            dk_ref[pl.ds(start, w), ks] = dk_new[h]
            dv_ref[pl.ds(start, w), ks] = dv_new[h]
        if has_sink:
            dsink_ref[...] = dsink_new

    qspec, kspec, lg = _band_specs(m, bq, G, nh, seqs)
    kin = pl.BlockSpec(kspec.block_shape, kspec.index_map, pipeline_mode=pl.Buffered(1))
    in_specs = ([pl.BlockSpec(memory_space=pltpu.SMEM)] if has_sink else []) + [qspec, kin, kin, qspec, qspec, qspec]
    in_specs += [qspec] if has_dlse else []
    args = ((sink,) if has_sink else ()) + (q, k, v, o, lse, do) + ((dlse,) if has_dlse else ())
    out_specs = [qspec, kspec, kspec]
    out_shape = [_sds(q.shape, F32), _sds(k.shape, F32), _sds(k.shape, F32)]
    if has_sink:
        out_specs.append(pl.BlockSpec((None, 8, HEAD), lambda s, i: (s, 0, 0)))
        out_shape.append(_sds((seqs // nh, 8, HEAD), F32))
    return _pallas(body, name=name, grid=(classes * lg, m // bq), in_specs=in_specs, out_specs=out_specs,
                   out_shape=out_shape, compiler_params=_params(("parallel", "arbitrary")))(*args)


def _group_weights(lses):
    mx = jnp.maximum(jnp.maximum(lses[0], lses[1]), lses[2])
    e = [jnp.exp(l - mx) for l in lses]
    tot = e[0] + e[1] + e[2]
    return [t / tot for t in e]


def _read_group(ref, dil, h, scr):
    hs = slice(h * HEAD, (h + 1) * HEAD)
    return ref[:, hs] if dil == 1 else _from_classes([ref[r, :, hs] for r in range(dil)], scr)


def b_combine_fwd(os_, lses, name):
    n, wd = os_[0].shape
    tm = _tile(n, 512, 8)

    def body(o0, o1, o2, l0, l1, l2, out_ref, scr):
        for h in range(wd // HEAD):
            ls = [_read_group(ref, dil, h, scr.at[g]) for g, (ref, dil) in enumerate(zip((l0, l1, l2), B_DILS))]
            ovs = [_read_group(ref, dil, h, scr.at[3 + g]) for g, (ref, dil) in enumerate(zip((o0, o1, o2), B_DILS))]
            wts = _group_weights(ls)
            out_ref[:, h * HEAD:(h + 1) * HEAD] = wts[0] * ovs[0] + wts[1] * ovs[1] + wts[2] * ovs[2]

    specs = [_class_spec(tm, dil, wd) for dil in B_DILS]
    return _pallas(body, name=name, grid=(n // tm,), in_specs=specs * 2, out_specs=pl.BlockSpec((tm, wd), lambda i: (i, 0)),
                   out_shape=_sds((n, wd), F32), scratch_shapes=[pltpu.VMEM((6, tm, HEAD), F32)],
                   compiler_params=_params(("parallel",)))(*os_, *lses)


def b_combine_bwd(dout, os_, lses, name):
    n, wd = dout.shape
    tm = _tile(n, 256, 8)

    def body(d_ref, o0, o1, o2, l0, l1, l2, do0, do1, do2, dl0, dl1, dl2, scr):
        for h in range(wd // HEAD):
            hs = slice(h * HEAD, (h + 1) * HEAD)
            dv = d_ref[:, hs]
            ls = [_read_group(ref, dil, h, scr.at[g]) for g, (ref, dil) in enumerate(zip((l0, l1, l2), B_DILS))]
            ovs = [_read_group(ref, dil, h, scr.at[3 + g]) for g, (ref, dil) in enumerate(zip((o0, o1, o2), B_DILS))]
            wts = _group_weights(ls)
            dws = [jnp.broadcast_to(jnp.sum(dv * ov, axis=-1, keepdims=True), (tm, HEAD)) for ov in ovs]
            mean = wts[0] * dws[0] + wts[1] * dws[1] + wts[2] * dws[2]
            for g, (wt, dw, do_ref, dl_ref, dil) in enumerate(zip(wts, dws, (do0, do1, do2), (dl0, dl1, dl2), B_DILS)):
                if dil == 1:
                    do_ref[:, hs] = wt * dv
                    dl_ref[:, hs] = wt * (dw - mean)
                else:
                    for r, part in enumerate(_to_classes(wt * dv, scr.at[g], dil)):
                        do_ref[r, :, hs] = part
                    for r, part in enumerate(_to_classes(wt * (dw - mean), scr.at[3 + g], dil)):
                        dl_ref[r, :, hs] = part

    specs = [_class_spec(tm, dil, wd) for dil in B_DILS]
    outs = _pallas(body, name=name, grid=(n // tm,), in_specs=[pl.BlockSpec((tm, wd), lambda i: (i, 0))] + specs * 2,
                   out_specs=specs * 2, out_shape=[_sds(_class_shape(n, dil, wd), F32) for dil in B_DILS] * 2,
                   scratch_shapes=[pltpu.VMEM((6, tm, HEAD), F32)],
                   compiler_params=_params(("parallel",)))(dout, *os_, *lses)
    return outs[:3], outs[3:]


def _c_rows(n):
    rows = n // GRID_W
    return rows, min(C_WIN_ROWS, rows)


def _c_row_start(r, rows, wr):
    return jnp.clip(r - wr // 2, 0, rows - wr)


def _c_bias_index(r, rows, wr):
    return _c_row_start(r, rows, wr) - r + (C_WIN_ROWS - 1)


def _col_shift_select(tile, cq, inverse):
    lanes = tile.shape[1]
    for b in range(6):
        amt = (lanes - (1 << b)) if inverse else (1 << b)
        tile = jnp.where(((cq >> b) & 1) == 1, pltpu.roll(tile, amt, 1), tile)
    return tile


def rpb_expand(rwin, name):
    lanes = rwin.shape[-1]

    def body(r_ref, b_ref):
        cq = lax.broadcasted_iota(jnp.int32, (GRID_W, lanes), 0)
        ck = lax.broadcasted_iota(jnp.int32, (GRID_W, lanes), 1) % GRID_W
        cs = jnp.clip(cq - C_WIN_COLS // 2, 0, GRID_W - C_WIN_COLS)
        ok = (ck >= cs) & (ck < cs + C_WIN_COLS)
        for i0 in range(C_WIN_ROWS):
            tile = jnp.broadcast_to(r_ref[i0], (GRID_W, lanes))
            tile = pltpu.roll(tile, lanes - (C_WIN_COLS - 1), 1)
            tile = _col_shift_select(tile, cq, False)
            b_ref[i0] = jnp.where(ok, tile, NEG)

    return _pallas(body, name=name, grid=(C_HEADS,),
                   in_specs=[pl.BlockSpec((None, C_WIN_ROWS, 1, lanes), lambda h: (h, 0, 0, 0))],
                   out_specs=pl.BlockSpec((None, C_WIN_ROWS, GRID_W, lanes), lambda h: (h, 0, 0, 0)),
                   out_shape=_sds((C_HEADS, C_WIN_ROWS, GRID_W, lanes), F32),
                   compiler_params=_params(("parallel",)))(rwin)


def rpb_reduce(dbias, name):
    lanes = dbias.shape[-1]
    wr = lanes // GRID_W

    def body(d_ref, o_ref):
        cq = lax.broadcasted_iota(jnp.int32, (GRID_W, lanes), 0)
        o_ref[...] = jnp.zeros_like(o_ref)
        for i0 in range(C_WIN_ROWS):
            tile = _col_shift_select(d_ref[i0], cq, True)
            tile = pltpu.roll(tile, C_WIN_COLS - 1, 1)
            vec = jnp.sum(tile, axis=0, keepdims=True)
            for w in range(wr):
                o_ref[i0 + w:i0 + w + 1, :] += vec[:, w * GRID_W:(w + 1) * GRID_W]

    return _pallas(body, name=name, grid=(C_HEADS,),
                   in_specs=[pl.BlockSpec((None, C_WIN_ROWS, GRID_W, lanes), lambda h: (h, 0, 0, 0))],
                   out_specs=pl.BlockSpec((None, 16, GRID_W), lambda h: (h, 0, 0)),
                   out_shape=_sds((C_HEADS, 16, GRID_W), F32),
                   compiler_params=_params(("parallel",)))(dbias)


def _store_or_add(ref, val, first):
    @pl.when(first)
    def _():
        ref[...] = val

    @pl.when(jnp.logical_not(first))
    def _():
        ref[...] += val


def c_attn_fwd(q, k, v, bias, name, nh=C_HEADS):
    n = q.shape[0]
    rows, wr = _c_rows(n)
    wk = wr * GRID_W

    def body(q_ref, k_ref, v_ref, b_ref, o_ref, lse_ref):
        r = pl.program_id(1)
        start = pl.multiple_of(_c_row_start(r, rows, wr) * GRID_W, GRID_W)
        sls = [slice(h * HEAD, (h + 1) * HEAD) for h in range(nh)]
        ss = [lax.dot_general(q_ref[:, sl], k_ref[pl.ds(start, wk), sl], _DIMS["nt"], preferred_element_type=F32)
              * SCALE + b_ref[h] for h, sl in enumerate(sls)]
        mxs = [jnp.max(s, axis=-1, keepdims=True) for s in ss]
        ps = [jnp.exp(s - mx) for s, mx in zip(ss, mxs)]
        dens = [jnp.sum(p, axis=-1, keepdims=True) for p in ps]
        outs = [jnp.dot((p / den).astype(CDT), v_ref[pl.ds(start, wk), sl], preferred_element_type=F32)
                for p, den, sl in zip(ps, dens, sls)]
        for sl, o, mx, den in zip(sls, outs, mxs, dens):
            o_ref[:, sl] = o
            lse_ref[:, sl] = jnp.broadcast_to(mx + jnp.log(den), (GRID_W, HEAD))

    qspec = pl.BlockSpec((GRID_W, nh * HEAD), lambda h, r: (r, h))
    kspec = pl.BlockSpec((n, nh * HEAD), lambda h, r: (0, h), pipeline_mode=pl.Buffered(1))
    bspec = pl.BlockSpec((nh, None, GRID_W, wk), lambda h, r: (h, _c_bias_index(r, rows, wr), 0, 0))
    return _pallas(body, name=name, grid=(C_HEADS // nh, rows), in_specs=[qspec, kspec, kspec, bspec],
                   out_specs=[qspec, qspec], out_shape=[_sds(q.shape, F32), _sds(q.shape, F32)],
                   compiler_params=_params(("parallel", "arbitrary")))(q, k, v, bias)


def c_attn_bwd(q, k, v, bias, o, lse, do, name, nh=4):
    n = q.shape[0]
    rows, wr = _c_rows(n)
    wk = wr * GRID_W

    def body(q_ref, k_ref, v_ref, b_ref, o_ref, lse_ref, do_ref, dq_ref, dk_ref, dv_ref, db_ref):
        r = pl.program_id(1)
        rs = _c_row_start(r, rows, wr)
        start = pl.multiple_of(rs * GRID_W, GRID_W)

        @pl.when(r == 0)
        def _():
            dk_ref[...] = jnp.zeros_like(dk_ref)
            dv_ref[...] = jnp.zeros_like(dv_ref)

        prev = _c_row_start(jnp.maximum(r - 1, 0), rows, wr) - jnp.maximum(r - 1, 0)
        first = (r == 0) | (prev != rs - r)
        sls = [slice(h * HEAD, (h + 1) * HEAD) for h in range(nh)]
        k_ts = [k_ref[pl.ds(start, wk), sl] for sl in sls]
        v_ts = [v_ref[pl.ds(start, wk), sl] for sl in sls]
        q_ts = [q_ref[:, sl] for sl in sls]
        lse_ts = [lse_ref[:, sl][:, :1] for sl in sls]
        do_ts = [do_ref[:, sl] for sl in sls]
        deltas = [jnp.sum(do_t * o_ref[:, sl], axis=-1, keepdims=True) for do_t, sl in zip(do_ts, sls)]
        biases = [b_ref[h] for h in range(nh)]
        dk_old = [dk_ref[pl.ds(start, wk), sl] for sl in sls]
        dv_old = [dv_ref[pl.ds(start, wk), sl] for sl in sls]

        ps = [jnp.exp(lax.dot_general(q_t, k_t, _DIMS["nt"], preferred_element_type=F32) * SCALE + b - lse_t)
              for q_t, k_t, b, lse_t in zip(q_ts, k_ts, biases, lse_ts)]
        do_cs = [do_t.astype(CDT) for do_t in do_ts]
        dps = [lax.dot_general(do_c, v_t, _DIMS["nt"], preferred_element_type=F32) for do_c, v_t in zip(do_cs, v_ts)]
        dss = [p * (dp - delta) for p, dp, delta in zip(ps, dps, deltas)]
        ds_cs = [(ds * SCALE).astype(CDT) for ds in dss]
        dqs = [jnp.dot(ds_c, k_t, preferred_element_type=F32) for ds_c, k_t in zip(ds_cs, k_ts)]
        dv_new = [old + lax.dot_general(p.astype(CDT), do_c, _DIMS["tn"], preferred_element_type=F32)
                  for old, p, do_c in zip(dv_old, ps, do_cs)]
        dk_new = [old + lax.dot_general(ds_c, q_t, _DIMS["tn"], preferred_element_type=F32)
                  for old, ds_c, q_t in zip(dk_old, ds_cs, q_ts)]

        for h, sl in enumerate(sls):
            dq_ref[:, sl] = dqs[h]
            dk_ref[pl.ds(start, wk), sl] = dk_new[h]
            dv_ref[pl.ds(start, wk), sl] = dv_new[h]
        for h in range(nh):
            _store_or_add(db_ref.at[h], dss[h], first)

    qspec = pl.BlockSpec((GRID_W, nh * HEAD), lambda h, r: (r, h))
    kspec = pl.BlockSpec((n, nh * HEAD), lambda h, r: (0, h))
    kin = pl.BlockSpec((n, nh * HEAD), lambda h, r: (0, h), pipeline_mode=pl.Buffered(1))
    bspec = pl.BlockSpec((nh, None, GRID_W, wk), lambda h, r: (h, _c_bias_index(r, rows, wr), 0, 0))
    return _pallas(body, name=name, grid=(C_HEADS // nh, rows),
                   in_specs=[qspec, kin, kin, bspec, qspec, qspec, qspec],
                   out_specs=[qspec, kspec, kspec, bspec],
                   out_shape=[_sds(q.shape, F32), _sds(k.shape, F32), _sds(k.shape, F32), _sds(bias.shape, F32)],
                   compiler_params=_params(("parallel", "arbitrary")))(q, k, v, bias, o, lse, do)


def _sigmoid(z):
    return 1.0 / (1.0 + jnp.exp(-z))


def _gate_specs(n, d):
    tm, tn = _tile(n, 256, 8), _tile(math.gcd(d, QKV_W), 1024)
    nj = d // tn
    tile = pl.BlockSpec((tm, tn), lambda i, j: (i, j))
    gl = [pl.BlockSpec((tm, tn), functools.partial(lambda i, j, b: (i, (QKV_W + b * d) // tn + j), b=b)) for b in range(3)]
    return tm, tn, nj, tile, gl


def gate_merge(proj, ys, name):
    n, d = ys[0].shape
    tm, tn, nj, tile, gl = _gate_specs(n, d)

    def body(g0, g1, g2, y0, y1, y2, out_ref):
        acc = (_sigmoid(g0[...].astype(F32)) * y0[...] + _sigmoid(g1[...].astype(F32)) * y1[...]
               + _sigmoid(g2[...].astype(F32)) * y2[...])
        out_ref[...] = acc.astype(out_ref.dtype)

    return _pallas(body, name=name, grid=(n // tm, nj), in_specs=gl + [tile] * 3, out_specs=tile,
                   out_shape=_sds((n, d), CDT), compiler_params=_params(("parallel", "parallel")))(proj, proj, proj, *ys)


def gate_bwd(proj, ys, dx, w_o, name, after=None):
    n, d = dx.shape
    tm, tn, nj, _, _ = _gate_specs(n, d)
    extra = [] if after is None else [after]

    def body(g_ref, y0, y1, y2, a_ref, w_ref, *refs):
        dy_ref, dp_ref, dm_ref = refs[len(extra):]
        b = pl.program_id(2)

        @pl.when(b == 0)
        def _():
            dm_ref[...] = lax.dot_general(a_ref[...].astype(CDT), w_ref[...].astype(CDT), _DIMS["nt"],
                                          preferred_element_type=F32)

        y = jnp.where(b == 0, y0[...], jnp.where(b == 1, y1[...], y2[...]))
        dm = dm_ref[...]
        sg = _sigmoid(g_ref[...].astype(F32))
        dy_ref[...] = (dm * sg).astype(dy_ref.dtype)
        dp_ref[...] = (dm * y * sg * (1.0 - sg)).astype(dp_ref.dtype)

    gl = pl.BlockSpec((tm, tn), lambda i, j, b: (i, QKV_W // tn + b * nj + j))
    tile = pl.BlockSpec((tm, tn), lambda i, j, b: (i, j))
    return _pallas(body, name=name, grid=(n // tm, nj, 3),
                   in_specs=[gl, tile, tile, tile, pl.BlockSpec((tm, d), lambda i, j, b: (i, 0)),
                             pl.BlockSpec((tn, d), lambda i, j, b: (j, 0))]
                   + [pl.BlockSpec(a.shape, lambda i, j, b: (0, 0)) for a in extra],
                   out_specs=[pl.BlockSpec((None, tm, tn), lambda i, j, b: (b, i, j)), gl],
                   out_shape=[_sds((3, n, d), CDT), _sds(proj.shape, CDT)],
                   scratch_shapes=[pltpu.VMEM((tm, tn), F32)],
                   compiler_params=_params(("parallel", "parallel", "arbitrary")))(proj, *ys, dx, w_o, *extra)


def gate_up_swiglu(h2, wg, name, after=None):
    n, d = h2.shape
    ns = wg.shape[2]
    ff = 2 * ns
    tm, tn = _tile(n, 512, 8), _tile(ns, 1408)
    nj = ns // tn
    extra = [] if after is None else [after]

    def body(a_ref, bg_ref, bu_ref, *refs):
        gu_ref, act_ref = refs[len(extra):]
        a = a_ref[...].astype(CDT)
        gt = jnp.dot(a, bg_ref[...].astype(CDT), preferred_element_type=F32)
        up = jnp.dot(a, bu_ref[...].astype(CDT), preferred_element_type=F32)
        gu_ref[0] = gt.astype(gu_ref.dtype)
        gu_ref[1] = up.astype(gu_ref.dtype)
        act_ref[...] = (gt * _sigmoid(gt) * up).astype(act_ref.dtype)

    return _pallas(body, name=name, grid=(n // tm, 2 * nj),
                   in_specs=[pl.BlockSpec((tm, d), lambda i, j: (i, 0)),
                             pl.BlockSpec((None, d, tn), lambda i, j: (j // nj, 0, j % nj)),
                             pl.BlockSpec((None, d, tn), lambda i, j: (2 + j // nj, 0, j % nj))]
                   + [pl.BlockSpec(a.shape, lambda i, j: (0, 0)) for a in extra],
                   out_specs=[pl.BlockSpec((2, tm, tn), lambda i, j: (0, i, j)), pl.BlockSpec((tm, tn), lambda i, j: (i, j))],
                   out_shape=[_sds((2, n, ff), ADT), _sds((n, ff), CDT)],
                   compiler_params=_params(("parallel", "parallel")))(h2, wg, wg, *extra)


def d_gate_up(dx, w_down, gu, name):
    n, d = dx.shape
    ff = w_down.shape[0]
    tm, tn = _tile(n, 1024, 8), _tile(ff, 512)

    def body(a_ref, b_ref, gu_ref, d_ref):
        da = lax.dot_general(a_ref[...].astype(CDT), b_ref[...].astype(CDT), _DIMS["nt"], preferred_element_type=F32)
        gt, up = gu_ref[0].astype(F32), gu_ref[1].astype(F32)
        sg = _sigmoid(gt)
        d_ref[0] = (da * up * (sg + gt * sg * (1.0 - sg))).astype(d_ref.dtype)
        d_ref[1] = (da * gt * sg).astype(d_ref.dtype)

    blk = pl.BlockSpec((2, tm, tn), lambda i, j: (0, i, j))
    return _pallas(body, name=name, grid=(n // tm, ff // tn),
                   in_specs=[pl.BlockSpec((tm, d), lambda i, j: (i, 0)), pl.BlockSpec((tn, d), lambda i, j: (j, 0)), blk],
                   out_specs=blk, out_shape=_sds((2, n, ff), CDT),
                   compiler_params=_params(("parallel", "parallel")))(dx, w_down, gu)


def loss_head(y, target, name):
    n, d = y.shape
    tm = _tile(n, 512, 8)
    nsteps = n // tm

    def body(y_ref, t_ref, l_ref, dy_ref, dyc_ref, acc_ref):
        i = pl.program_id(0)
        e = y_ref[...] - t_ref[...]
        dy_ref[...] = e * (1.0 / d)
        dyc_ref[...] = (e * (1.0 / d)).astype(dyc_ref.dtype)
        part = jnp.sum((e * e).reshape(tm // 8, 8, d), axis=0)

        @pl.when(i == 0)
        def _():
            acc_ref[...] = part

        @pl.when(i > 0)
        def _():
            acc_ref[...] += part

        @pl.when(i == nsteps - 1)
        def _():
            tot = jnp.sum(jnp.sum(acc_ref[...], axis=1, keepdims=True), axis=0, keepdims=True) * (0.5 / d)
            l_ref[...] = jnp.broadcast_to(tot, (8, HEAD))

    row = pl.BlockSpec((tm, d), lambda i: (i, 0))
    return _pallas(body, name=name, grid=(nsteps,), in_specs=[row, row],
                   out_specs=[pl.BlockSpec((8, HEAD), lambda i: (0, 0)), row, row],
                   out_shape=[_sds((8, HEAD), F32), _sds((n, d), F32), _sds((n, d), CDT)],
                   scratch_shapes=[pltpu.VMEM((8, d), F32)],
                   compiler_params=_params(("arbitrary",)))(y, target)


def adamw(w, g, m, v, name):
    r, c = w.shape
    tr = _tile(r, max(8, (1 << 19) // c), 8)
    c1 = 1.0 - ADAM_B1 ** ADAM_STEP
    c2 = 1.0 - ADAM_B2 ** ADAM_STEP

    def body(w_ref, g_ref, m_ref, v_ref, d_ref, mo_ref, vo_ref):
        gv = g_ref[...]
        mn = ADAM_B1 * m_ref[...] + (1.0 - ADAM_B1) * gv
        vn = ADAM_B2 * v_ref[...] + (1.0 - ADAM_B2) * (gv * gv)
        d_ref[...] = -ADAM_LR * ((mn / c1) / (jnp.sqrt(vn / c2) + ADAM_EPS) + ADAM_WD * w_ref[...])
        mo_ref[...] = mn
        vo_ref[...] = vn

    row = pl.BlockSpec((tr, c), lambda i: (i, 0))
    return _pallas(body, name=name, grid=(r // tr,), in_specs=[row] * 4, out_specs=[row] * 3,
                   out_shape=[_sds((r, c), F32)] * 3, compiler_params=_params(("parallel",)))(w, g, m, v)


ANY = pl.BlockSpec(memory_space=pl.ANY)


def _place():
    x, y, c = lax.axis_index("x"), lax.axis_index("y"), lax.axis_index("c")
    return x, y, c, [(1 - x, y), (x, 1 - y), (1 - x, 1 - y)]


def _rcopy(src, dst, send_sems, recv_sems, k, to):
    return pltpu.make_async_remote_copy(src_ref=src, dst_ref=dst, send_sem=send_sems.at[k], recv_sem=recv_sems.at[k],
                                        device_id=to, device_id_type=MESH)


def cast_place(chip_idx, shards, layer, name):
    _, k, ns = shards.shape
    tr = _tile(k, max(16, (1 << 19) // ns), 16)

    def body(k_ref, s_ref, o_ref):
        o_ref[...] = s_ref[...].astype(o_ref.dtype)

    gs = pltpu.PrefetchScalarGridSpec(
        num_scalar_prefetch=1, grid=(k // tr,),
        in_specs=[pl.BlockSpec((None, tr, ns), lambda i, k_ref: (layer, i, 0))],
        out_specs=pl.BlockSpec((None, tr, ns), lambda i, k_ref: (k_ref[0], i, 0)))
    return _pallas(body, name=name, grid_spec=gs, out_shape=_sds((N_CHIPS, k, ns), CDT),
                   compiler_params=_params(("parallel",)))(chip_idx, shards)


HBM = pl.BlockSpec(memory_space=pltpu.HBM)
SEM = pl.BlockSpec(memory_space=pltpu.SEMAPHORE)
EFFECT = pltpu.SideEffectType.DATAFLOW_SIDE_EFFECTING


def _in_hbm(a):
    return pltpu.with_memory_space_constraint(a, pltpu.HBM)


def _gather_copies(refs, send_sems, recv_sems):
    x, y, c, chips = _place()
    me = 2 * x + y
    out = []
    for t, ref in enumerate(refs):
        kh = ref.shape[1] // 2
        for j, (px, py) in enumerate(chips):
            send = _rcopy(ref.at[me, pl.ds(c * kh, kh)], ref.at[me, pl.ds(c * kh, kh)], send_sems, recv_sems,
                          3 * t + j, (px, py, c))
            land = ref.at[2 * px + py, pl.ds(c * kh, kh)]
            out.append((send, _rcopy(land, land, send_sems, recv_sems, 3 * t + j, (px, py, c))))
    return out


def gather_start(bufs, after, name):
    nt = len(bufs)
    after = list(after) if isinstance(after, (list, tuple)) else [after]
    na = len(after)

    def body(*refs):
        ins, send_sems, recv_sems, token = refs[:nt], refs[nt + na], refs[nt + na + 1], refs[-1]
        for send, _ in _gather_copies(ins, send_sems, recv_sems):
            send.start()
        token[...] = jnp.zeros_like(token)

    outs = _pallas(body, name=name, in_specs=[HBM] * nt + [ANY] * na,
                   out_specs=(SEM, SEM) + (HBM,) * nt + (pl.BlockSpec(memory_space=pltpu.VMEM),),
                   out_shape=(pltpu.SemaphoreType.DMA((3 * nt,)), pltpu.SemaphoreType.DMA((3 * nt,)))
                   + tuple(pltpu.HBM(b.shape, b.dtype) for b in bufs) + (_sds((8, HEAD), F32),),
                   input_output_aliases={t: 2 + t for t in range(nt)},
                   compiler_params=pltpu.CompilerParams(has_side_effects=EFFECT))(*[_in_hbm(b) for b in bufs], *after)
    return outs[0], outs[1], list(outs[2:2 + nt]), outs[-1]


def gather_wait(bufs, send_sems, recv_sems, after, name):
    nt = len(bufs)

    def body(*refs):
        ins, s_sems, r_sems = refs[:nt], refs[nt], refs[nt + 1]
        for send, land in _gather_copies(ins, s_sems, r_sems):
            send.wait_send()
            land.wait_recv()

    return _pallas(body, name=name, in_specs=[HBM] * nt + [SEM, SEM, ANY], out_specs=[HBM] * nt,
                   out_shape=[pltpu.HBM(b.shape, b.dtype) for b in bufs],
                   input_output_aliases={t: t for t in range(nt)},
                   compiler_params=pltpu.CompilerParams(has_side_effects=EFFECT))(*bufs, send_sems, recv_sems, after)


def _forward_copies(refs, send_sems, recv_sems):
    x, y, c, chips = _place()
    out = []
    for t, ref in enumerate(refs):
        kh = ref.shape[1] // 2
        for j, (px, py) in enumerate(chips):
            mine = ref.at[2 * px + py, pl.ds(c * kh, kh)]
            land = ref.at[2 * px + py, pl.ds((1 - c) * kh, kh)]
            out.append((_rcopy(mine, mine, send_sems, recv_sems, 3 * t + j, (x, y, 1 - c)),
                        _rcopy(land, land, send_sems, recv_sems, 3 * t + j, (x, y, 1 - c))))
    return out


def forward_start(bufs, after, name):
    nt = len(bufs)

    def body(*refs):
        ins, send_sems, recv_sems, token = refs[:nt], refs[nt + 1], refs[nt + 2], refs[-1]
        for send, _ in _forward_copies(ins, send_sems, recv_sems):
            send.start()
        token[...] = jnp.zeros_like(token)

    outs = _pallas(body, name=name, in_specs=[HBM] * nt + [ANY],
                   out_specs=(SEM, SEM) + (HBM,) * nt + (pl.BlockSpec(memory_space=pltpu.VMEM),),
                   out_shape=(pltpu.SemaphoreType.DMA((3 * nt,)), pltpu.SemaphoreType.DMA((3 * nt,)))
                   + tuple(pltpu.HBM(b.shape, b.dtype) for b in bufs) + (_sds((8, HEAD), F32),),
                   input_output_aliases={t: 2 + t for t in range(nt)},
                   compiler_params=pltpu.CompilerParams(has_side_effects=EFFECT))(*[_in_hbm(b) for b in bufs], after)
    return outs[0], outs[1], list(outs[2:2 + nt]), outs[-1]


def forward_wait(bufs, send_sems, recv_sems, after, name):
    nt = len(bufs)

    def body(*refs):
        ins, s_sems, r_sems = refs[:nt], refs[nt], refs[nt + 1]
        for send, land in _forward_copies(ins, s_sems, r_sems):
            send.wait_send()
            land.wait_recv()

    return _pallas(body, name=name, in_specs=[HBM] * nt + [SEM, SEM, ANY], out_specs=[HBM] * nt,
                   out_shape=[pltpu.HBM(b.shape, b.dtype) for b in bufs],
                   input_output_aliases={t: t for t in range(nt)},
                   compiler_params=pltpu.CompilerParams(has_side_effects=EFFECT))(*bufs, send_sems, recv_sems, after)


def pair_forward(bufs, name):
    nt = len(bufs)

    def body(*refs):
        outs = refs[nt:2 * nt]
        send_sems, recv_sems = refs[2 * nt:]
        x, y, c, chips = _place()
        cps = []
        for t in range(nt):
            kh = outs[t].shape[1] // 2
            for j, (px, py) in enumerate(chips):
                blk = outs[t].at[2 * px + py, pl.ds(c * kh, kh)]
                cps.append(_rcopy(blk, blk, send_sems, recv_sems, 3 * t + j, (x, y, 1 - c)))
                cps[-1].start()
        for t in range(nt):
            kh = outs[t].shape[1] // 2
            for j, (px, py) in enumerate(chips):
                blk = outs[t].at[2 * px + py, pl.ds((1 - c) * kh, kh)]
                _rcopy(blk, blk, send_sems, recv_sems, 3 * t + j, (x, y, 1 - c)).wait_recv()
        for cp in cps:
            cp.wait_send()

    return _pallas(body, name=name, in_specs=[ANY] * nt, out_specs=[ANY] * nt,
                   out_shape=[_sds(b.shape, b.dtype) for b in bufs],
                   input_output_aliases={t: t for t in range(nt)},
                   scratch_shapes=[pltpu.SemaphoreType.DMA((3 * nt,)), pltpu.SemaphoreType.DMA((3 * nt,))],
                   compiler_params=pltpu.CompilerParams(has_side_effects=True))(*bufs)


def pair_exchange(grads, name):
    nt = len(grads)

    def body(*refs):
        ins, outs = refs[:nt], refs[nt:2 * nt]
        send_sems, recv_sems = refs[2 * nt:]
        x, y, c, _ = _place()
        sibling = (x, y, 1 - c)
        cps = []
        for t in range(nt):
            kh = ins[t].shape[1] // 2
            cps.append(_rcopy(ins[t].at[:, pl.ds((1 - c) * kh, kh), :], outs[t], send_sems, recv_sems, t, sibling))
            cps[-1].start()
        for cp in cps:
            cp.wait_recv()
        for cp in cps:
            cp.wait_send()

    return _pallas(body, name=name, in_specs=[ANY] * nt, out_specs=[ANY] * nt,
                   out_shape=[_sds((N_CHIPS, g.shape[1] // 2, g.shape[2]), g.dtype) for g in grads],
                   scratch_shapes=[pltpu.SemaphoreType.DMA((nt,)), pltpu.SemaphoreType.DMA((nt,))],
                   compiler_params=pltpu.CompilerParams(has_side_effects=True))(*grads)


def _exchange_copies(sums, lands, send_sems, recv_sems):
    x, y, c, chips = _place()
    return [_rcopy(s.at[2 * px + py], l.at[j], send_sems, recv_sems, 3 * t + j, (px, py, c))
            for t, (s, l) in enumerate(zip(sums, lands)) for j, (px, py) in enumerate(chips)]


def exchange_start(sums, name):
    nt = len(sums)
    lands = [lax.empty((3,) + s.shape[1:], s.dtype) for s in sums]

    def body(*refs):
        ins, zones, send_sems, recv_sems, token = refs[:nt], refs[nt:2 * nt], refs[2 * nt], refs[2 * nt + 1], refs[-1]
        for cp in _exchange_copies(ins, zones, send_sems, recv_sems):
            cp.start()
        token[...] = jnp.zeros_like(token)

    outs = _pallas(body, name=name, in_specs=[HBM] * (2 * nt),
                   out_specs=(SEM, SEM) + (HBM,) * (2 * nt) + (pl.BlockSpec(memory_space=pltpu.VMEM),),
                   out_shape=(pltpu.SemaphoreType.DMA((3 * nt,)), pltpu.SemaphoreType.DMA((3 * nt,)))
                   + tuple(pltpu.HBM(a.shape, a.dtype) for a in list(sums) + lands) + (_sds((8, HEAD), F32),),
                   input_output_aliases={t: 2 + t for t in range(2 * nt)},
                   compiler_params=pltpu.CompilerParams(has_side_effects=EFFECT))(*[_in_hbm(a) for a in list(sums) + lands])
    return outs[0], outs[1], list(outs[2:2 + nt]), list(outs[2 + nt:2 + 2 * nt]), outs[-1]


def exchange_wait(sums, lands, send_sems, recv_sems, after, name):
    nt = len(sums)

    def body(*refs):
        ins, zones, s_sems, r_sems = refs[:nt], refs[nt:2 * nt], refs[2 * nt], refs[2 * nt + 1]
        for cp in _exchange_copies(ins, zones, s_sems, r_sems):
            cp.wait_send()
            cp.wait_recv()

    outs = _pallas(body, name=name, in_specs=[HBM] * (2 * nt) + [SEM, SEM, ANY], out_specs=[HBM] * (2 * nt),
                   out_shape=[pltpu.HBM(a.shape, a.dtype) for a in list(sums) + list(lands)],
                   input_output_aliases={t: t for t in range(2 * nt)},
                   compiler_params=pltpu.CompilerParams(has_side_effects=EFFECT))(*sums, *lands, send_sems, recv_sems, after)
    return list(outs[:nt]), list(outs[nt:])


def pair_share(halves, name):
    nt = len(halves)

    def body(*refs):
        ins, outs = refs[:nt], refs[nt:2 * nt]
        send_sems, recv_sems = refs[2 * nt:]
        x, y, c, _ = _place()
        cps = []
        for t in range(nt):
            cps.append(_rcopy(ins[t], outs[t], send_sems, recv_sems, t, (x, y, 1 - c)))
            cps[-1].start()
        for cp in cps:
            cp.wait_recv()
        for cp in cps:
            cp.wait_send()

    return _pallas(body, name=name, in_specs=[ANY] * nt, out_specs=[ANY] * nt,
                   out_shape=[_sds(h.shape, h.dtype) for h in halves],
                   scratch_shapes=[pltpu.SemaphoreType.DMA((nt,)), pltpu.SemaphoreType.DMA((nt,))],
                   compiler_params=pltpu.CompilerParams(has_side_effects=True))(*halves)


def small_allreduce(pack, name):
    r = pack.shape[0]

    def body(in_ref, out_ref, buf, send_sems, recv_sems):
        x, y, c, _ = _place()
        me = 4 * x + 2 * y + c
        sends = []
        for k in range(1, 8):
            to = ((x + ((k >> 2) & 1)) % 2, (y + ((k >> 1) & 1)) % 2, (c + (k & 1)) % 2)
            cp = _rcopy(in_ref, buf.at[me], send_sems, recv_sems, k - 1, to)
            cp.start()
            sends.append((cp, to))
        buf[pl.ds(me, 1)] = in_ref[...][None]
        for k, (_, to) in enumerate(sends):
            peer = 4 * to[0] + 2 * to[1] + to[2]
            _rcopy(in_ref, buf.at[peer], send_sems, recv_sems, k, to).wait_recv()
        for cp, _ in sends:
            cp.wait_send()
        acc = buf[0]
        for d in range(1, 8):
            acc = acc + buf[d]
        out_ref[...] = acc

    vm = pl.BlockSpec(memory_space=pltpu.VMEM)
    return _pallas(body, name=name, in_specs=[vm], out_specs=vm, out_shape=_sds((r, HEAD), F32),
                   scratch_shapes=[pltpu.VMEM((8, r, HEAD), F32), pltpu.SemaphoreType.DMA((7,)),
                                   pltpu.SemaphoreType.DMA((7,))],
                   compiler_params=pltpu.CompilerParams(has_side_effects=True))(pack)


def add_halves(c_idx, grad, other, name):
    _, k, ns = grad.shape
    kh = k // 2
    tr = _tile(kh, max(16, (1 << 19) // ns), 16)
    nr = kh // tr

    def body(c_ref, g_ref, o_ref, s_ref):
        s_ref[...] = (g_ref[...].astype(F32) + o_ref[...].astype(F32)).astype(s_ref.dtype)

    gs = pltpu.PrefetchScalarGridSpec(
        num_scalar_prefetch=1, grid=(N_CHIPS, nr),
        in_specs=[pl.BlockSpec((None, tr, ns), lambda g, i, c_ref: (g, c_ref[0] * nr + i, 0)),
                  pl.BlockSpec((None, tr, ns), lambda g, i, c_ref: (g, i, 0))],
        out_specs=pl.BlockSpec((None, tr, ns), lambda g, i, c_ref: (g, i, 0)))
    return _pallas(body, name=name, grid_spec=gs, out_shape=_sds((N_CHIPS, kh, ns), XDT),
                   compiler_params=_params(("parallel", "parallel")))(c_idx, grad, other)


def add_chips(chip_idx, sums, recv, stack, layer, n_layers, name):
    _, kh, ns = sums.shape
    tr = _tile(kh, max(16, (1 << 19) // ns), 16)
    has_stack = stack is not None

    def body(k_ref, s_ref, r0, r1, r2, *rest):
        o_ref = rest[-1]
        o_ref[...] = ((s_ref[...].astype(F32) + r0[...].astype(F32)) + r1[...].astype(F32)) + r2[...].astype(F32)

    rspec = [pl.BlockSpec((None, tr, ns), functools.partial(lambda i, k_ref, j: (j, i, 0), j=j)) for j in range(3)]
    gs = pltpu.PrefetchScalarGridSpec(
        num_scalar_prefetch=1, grid=(kh // tr,),
        in_specs=[pl.BlockSpec((None, tr, ns), lambda i, k_ref: (k_ref[0], i, 0))] + rspec + ([ANY] if has_stack else []),
        out_specs=pl.BlockSpec((None, tr, ns), lambda i, k_ref: (layer, i, 0)))
    args = (chip_idx, sums, recv, recv, recv) + ((stack,) if has_stack else ())
    return _pallas(body, name=name, grid_spec=gs, out_shape=_sds((n_layers, kh, ns), F32),
                   input_output_aliases={5: 0} if has_stack else {},
                   compiler_params=_params(("parallel",)))(*args)


def adamw_big(c_idx, w, m, v, mine, other, name):
    nl, k, ns = w.shape
    kh = k // 2
    tr = _tile(kh, max(8, (1 << 18) // ns), 8)
    nr = kh // tr
    c1 = 1.0 - ADAM_B1 ** ADAM_STEP
    c2 = 1.0 - ADAM_B2 ** ADAM_STEP

    def body(c_ref, w_ref, m_ref, v_ref, a_ref, b_ref, g_ref, d_ref, mo_ref, vo_ref):
        gv = jnp.where(pl.program_id(2) == c_ref[0], a_ref[...], b_ref[...])
        mn = ADAM_B1 * m_ref[...] + (1.0 - ADAM_B1) * gv
        vn = ADAM_B2 * v_ref[...] + (1.0 - ADAM_B2) * (gv * gv)
        g_ref[...] = gv
        d_ref[...] = -ADAM_LR * ((mn / c1) / (jnp.sqrt(vn / c2) + ADAM_EPS) + ADAM_WD * w_ref[...])
        mo_ref[...] = mn
        vo_ref[...] = vn

    full = pl.BlockSpec((None, tr, ns), lambda l, i, hh, c_ref: (l, hh * nr + i, 0))
    half = pl.BlockSpec((None, tr, ns), lambda l, i, hh, c_ref: (l, i, 0))
    gs = pltpu.PrefetchScalarGridSpec(num_scalar_prefetch=1, grid=(nl, nr, 2),
                                      in_specs=[full, full, full, half, half], out_specs=[full] * 4)
    return _pallas(body, name=name, grid_spec=gs, out_shape=[_sds(w.shape, F32)] * 4,
                   compiler_params=_params(("parallel", "parallel", "arbitrary")))(c_idx, w, m, v, mine, other)


W_NAMES = ("w_in", "w_br_a", "w_br_b", "w_br_c", "w_o", "w_gate_up", "w_down")


def _rope_tables(n):
    half = HEAD // 2
    inv_freq = ROPE_THETA ** (-jnp.arange(half, dtype=F32) * 2.0 / HEAD)
    ang = jnp.arange(n, dtype=F32)[:, None] * inv_freq[None, :]
    cos, sin = jnp.cos(ang), jnp.sin(ang)
    return jnp.concatenate([cos, cos], axis=-1), jnp.concatenate([-sin, sin], axis=-1)


def _rpb_windows(rpb):
    pad = jnp.pad(rpb, ((0, 0), (0, 1), (0, GRID_W - rpb.shape[2])))
    wins = [pad[:, i0:i0 + C_WIN_ROWS].reshape(C_HEADS, 1, C_WIN_ROWS * GRID_W) for i0 in range(C_WIN_ROWS)]
    return jnp.stack(wins, axis=1)


def _rows(t):
    return t.reshape(-1, t.shape[-1])


def _like(t, ref):
    return t.reshape(ref.shape)


def layer_fwd(x, p, w, cos2, sin2, rest=None, mid=None):
    n, d = x.shape
    s = {"x": x}
    s["h"] = rmsnorm_fwd(x, p["norm1_g"], "norm1")
    s["proj"] = mm_x_wcol(s["h"], w["w_in"], ADT, "proj")
    gains = jnp.pad(p["qk_norm_g"], ((0, 2), (0, 0)))
    pp = s["pp"] = qk_prep(s["proj"], gains, cos2, sin2, "qk_prep")
    sink = p["sink_a"].reshape(1, A_Q_HEADS)
    s["oa"], s["lse_a"] = band_attn_fwd(pp["qa"], pp["ka"], pp["va"], sink, seqs=A_KV_HEADS, G=A_GROUP,
                                        nh=A_KV_HEADS, radius=A_RADIUS, name="attn_a")
    s["ob"], s["lse_b"] = [], []
    for g, dil in enumerate(B_DILS):
        o, lse = band_attn_fwd(_rows(pp[f"qb{g}"]), _rows(pp[f"kb{g}"]), _rows(pp[f"vb{g}"]), None, seqs=B_HG, G=1,
                               nh=B_HG, radius=B_RADIUS, classes=dil, name=f"attn_b{g}")
        s["ob"].append(_like(o, pp[f"qb{g}"]))
        s["lse_b"].append(_like(lse, pp[f"qb{g}"]))
    ob = b_combine_fwd(s["ob"], s["lse_b"], "b_combine")
    s["bias"] = rpb_expand(_rpb_windows(p["rpb_c"]), "rpb_expand")
    s["oc"], s["lse_c"] = c_attn_fwd(pp["qc"], pp["kc"], pp["vc"], s["bias"], "attn_c")
    s["o_in"] = (s["oa"], ob, s["oc"])
    if rest is not None:
        w = {**w, **rest(s["oc"])}
    s["w"] = w
    s["ys"] = [mm_x_wcol(o, w[k], ADT, "branch_" + k[-1]) for o, k in zip(s["o_in"], ("w_br_a", "w_br_b", "w_br_c"))]
    s["merged"] = gate_merge(s["proj"], s["ys"], "gate_merge")
    s["x_mid"] = mm_x_w(s["merged"], w["w_o"], "out_proj", res=x)
    s["h2"] = rmsnorm_fwd(s["x_mid"], p["norm2_g"], "norm2")
    token = None if mid is None else mid(s["h2"])
    s["gu"], s["act"] = gate_up_swiglu(s["h2"], w["w_gate_up"], "gate_up", after=token)
    x_out = mm_x_w(s["act"], w["w_down"], "down", res=s["x_mid"], tk_pref=2816)
    return x_out, s


def layer_bwd(dx_out, s, p, cos2, sin2, on_dws, early=False):
    dx_out, dx_out_c = dx_out
    n, d = dx_out.shape
    pp, w = s["pp"], s["w"]
    dgu = d_gate_up(dx_out_c, w["w_down"], s["gu"], "d_gate_up")
    dw_down = mm_aT_d(s["act"], dx_out_c, "dw_down")
    dh2 = mm_x_wcolT(dgu, w["w_gate_up"], "d_h2", stacked_in=2)
    dw_gu = mm_aT_d_wcol(s["h2"], dgu, "dw_gate_up", tn_pref=1408, stacked_in=2)
    dx_mid, dx_mid_c, dg2 = rmsnorm_bwd(s["x_mid"], p["norm2_g"], dh2, dx_out, "norm2_bwd")
    dws = {"w_gate_up": dw_gu, "w_down": dw_down.reshape(N_CHIPS, dw_down.shape[0] // N_CHIPS, d)}
    token = on_dws(dws) if early else None

    dw_o = mm_aT_d(s["merged"], dx_mid_c, "dw_o")
    dys, dproj = gate_bwd(s["proj"], s["ys"], dx_mid_c, w["w_o"], "gate_bwd", after=token)
    dos, dw_br = [], []
    for b, (o, k) in enumerate(zip(s["o_in"], ("w_br_a", "w_br_b", "w_br_c"))):
        dos.append(mm_x_wcolT(dys, w[k], "d_o_" + k[-1], lead=b))
        dw_br.append(mm_aT_d_wcol(o, dys, "dw_br_" + k[-1], lead=b))

    grads = {}
    sink = p["sink_a"].reshape(1, A_Q_HEADS)
    grads["qa"], grads["ka"], grads["va"], dsink = band_attn_bwd(
        pp["qa"], pp["ka"], pp["va"], sink, s["oa"], s["lse_a"], dos[0], None,
        seqs=A_KV_HEADS, G=A_GROUP, nh=A_KV_HEADS, radius=A_RADIUS, name="attn_a_bwd")
    dobs, dlses = b_combine_bwd(dos[1], s["ob"], s["lse_b"], "b_combine_bwd")
    for g, dil in enumerate(B_DILS):
        dq, dk, dv = band_attn_bwd(_rows(pp[f"qb{g}"]), _rows(pp[f"kb{g}"]), _rows(pp[f"vb{g}"]), None,
                                   _rows(s["ob"][g]), _rows(s["lse_b"][g]), _rows(dobs[g]), _rows(dlses[g]),
                                   seqs=B_HG, G=1, nh=B_HG, radius=B_RADIUS, classes=dil, name=f"attn_b{g}_bwd")
        grads[f"qb{g}"], grads[f"kb{g}"], grads[f"vb{g}"] = [_like(t, pp[f"qb{g}"]) for t in (dq, dk, dv)]
    grads["qc"], grads["kc"], grads["vc"], dbias = c_attn_bwd(pp["qc"], pp["kc"], pp["vc"], s["bias"], s["oc"],
                                                              s["lse_c"], dos[2], "attn_c_bwd")
    drpb = rpb_reduce(dbias, "rpb_reduce")[:, :2 * C_WIN_ROWS - 1, :2 * C_WIN_COLS - 1]
    gains = jnp.pad(p["qk_norm_g"], ((0, 2), (0, 0)))
    dproj, dgains = qk_prep_bwd(s["proj"], gains, cos2, sin2, grads, dproj, "qk_prep_bwd")
    dw_in = mm_aT_d_wcol(s["h"], dproj, "dw_in")
    rest = {"w_in": dw_in, "w_br_a": dw_br[0], "w_br_b": dw_br[1], "w_br_c": dw_br[2],
            "w_o": dw_o.reshape(N_CHIPS, d // N_CHIPS, d)}
    token = on_dws(rest if early else {**rest, **dws})
    dh = mm_x_wcolT(dproj, w["w_in"], "d_h", after=token)
    dx_in, dx_in_c, dg1 = rmsnorm_bwd(s["x"], p["norm1_g"], dh, dx_mid, "norm1_bwd")
    small = {"norm1_g": dg1[0], "qk_norm_g": dgains[:6], "sink_a": dsink[0, :, 0],
             "rpb_c": drpb, "norm2_g": dg2[0]}
    return (dx_in, dx_in_c), small


SMALL_NAMES = ("norm1_g", "qk_norm_g", "sink_a", "rpb_c", "norm2_g")


def _pack_small(parts, extra=None):
    flat = [parts[k].reshape(-1) for k in SMALL_NAMES]
    flat.append(jnp.zeros((1,), F32) if extra is None else extra.reshape(1))
    v = jnp.concatenate(flat)
    rows = -(-v.shape[0] // (8 * HEAD)) * 8
    return jnp.pad(v, (0, rows * HEAD - v.shape[0])).reshape(rows, HEAD)


def _unpack_small(pack, like):
    v = pack.reshape(-1)
    out, off = {}, 0
    for k in SMALL_NAMES:
        size = math.prod(like[k].shape)
        out[k] = v[off:off + size].reshape(like[k].shape)
        off += size
    return out, v[off]


def kernel(x, norm1_g, w_in, qk_norm_g, sink_a, rpb_c, w_br_a, w_br_b, w_br_c, w_o, norm2_g, w_gate_up, w_down, loss_target, m_norm1_g, m_w_in, m_qk_norm_g, m_sink_a, m_rpb_c, m_w_br_a, m_w_br_b, m_w_br_c, m_w_o, m_norm2_g, m_w_gate_up, m_w_down, v_norm1_g, v_w_in, v_qk_norm_g, v_sink_a, v_rpb_c, v_w_br_a, v_w_br_b, v_w_br_c, v_w_o, v_norm2_g, v_w_gate_up, v_w_down):
    big = dict(w_in=w_in, w_br_a=w_br_a, w_br_b=w_br_b, w_br_c=w_br_c, w_o=w_o, w_gate_up=w_gate_up, w_down=w_down)
    big_m = dict(w_in=m_w_in, w_br_a=m_w_br_a, w_br_b=m_w_br_b, w_br_c=m_w_br_c, w_o=m_w_o, w_gate_up=m_w_gate_up, w_down=m_w_down)
    big_v = dict(w_in=v_w_in, w_br_a=v_w_br_a, w_br_b=v_w_br_b, w_br_c=v_w_br_c, w_o=v_w_o, w_gate_up=v_w_gate_up, w_down=v_w_down)
    small = dict(norm1_g=norm1_g, qk_norm_g=qk_norm_g, sink_a=sink_a, rpb_c=rpb_c, norm2_g=norm2_g)
    small_m = dict(norm1_g=m_norm1_g, qk_norm_g=m_qk_norm_g, sink_a=m_sink_a, rpb_c=m_rpb_c, norm2_g=m_norm2_g)
    small_v = dict(norm1_g=v_norm1_g, qk_norm_g=v_qk_norm_g, sink_a=v_sink_a, rpb_c=v_rpb_c, norm2_g=v_norm2_g)
    n_layers = w_in.shape[0]
    n, d = x.shape[1], x.shape[2]
    c_idx = lax.axis_index("c").astype(jnp.int32).reshape(1)
    chip_idx = (2 * lax.axis_index("x") + lax.axis_index("y")).astype(jnp.int32).reshape(1)
    cos2, sin2 = _rope_tables(n)

    def as_weights(names, got):
        w = dict(zip(names, got))
        if "w_o" in w:
            w["w_o"] = w["w_o"].reshape(d, d)
            w["w_down"] = w["w_down"].reshape(-1, d)
        return w

    def gathered(names, started, after):
        send_sems, recv_sems, bufs, _ = started
        return as_weights(names, pair_forward(gather_wait(bufs, send_sems, recv_sems, after, "gather_wait"), "pair_forward"))

    def placed(names, l):
        return [cast_place(chip_idx, big[k], l, "cast_" + k) for k in names]

    first, others_0 = W_NAMES[:1], W_NAMES[1:]
    started = gather_start(placed(first, 0), chip_idx, "gather_start")
    later = [placed(W_NAMES, l) for l in range(1, n_layers)]
    started_rest = gather_start(placed(others_0, 0), [started[3]] + [b for bufs in later for b in bufs], "gather_start")
    weights = gathered(first, started, started_rest[3])
    rest = lambda after: gathered(others_0, started_rest, after)

    xs, saved = x[0], []
    for l in range(n_layers):
        p = {k: small[k][l] for k in SMALL_NAMES}
        mid, fwd = None, {}
        if l + 1 < n_layers:
            started = gather_start(later[l], weights["w_in"], "gather_start")
            p["norm1_g"] = p["norm1_g"] + started[3][0, 0]

            def mid(after, started=started, fwd=fwd):
                send_sems, recv_sems, bufs, _ = started
                arrived = gather_wait(bufs, send_sems, recv_sems, after, "gather_wait")
                fwd["started"] = forward_start(arrived, after, "forward_start")
                return fwd["started"][3]

        xs, s = layer_fwd(xs, p, weights, cos2, sin2, rest, mid if l > 0 else None)
        saved.append(s)
        if l + 1 < n_layers and l > 0:
            send_sems, recv_sems, bufs, _ = fwd["started"]
            weights, rest = as_weights(W_NAMES, forward_wait(bufs, send_sems, recv_sems, xs, "forward_wait")), None
        elif l + 1 < n_layers:
            weights, rest = gathered(W_NAMES, started, xs), None
    loss_tile, *dx = loss_head(xs, loss_target[0], "loss_head")

    halves = {k: None for k in W_NAMES}
    small_g = [None] * n_layers
    pending = []

    def finish_exchanges(after):
        for l, names, (send_sems, recv_sems, sums, lands, _) in pending:
            sums, from_chips = exchange_wait(sums, lands, send_sems, recv_sems, after, "exchange_wait")
            for k, sm, r in zip(names, sums, from_chips):
                halves[k] = add_chips(chip_idx, sm, r, halves[k], l, n_layers, "add_chips_" + k)
        pending.clear()

    def make_on_dws(l, wait_first):
        def on_dws(dws):
            names = [k for k in W_NAMES if k in dws]
            parts = [dws[k] for k in names]
            if wait_first:
                finish_exchanges(parts[0])
            from_sibling = pair_exchange(parts, "pair_exchange")
            sums = [add_halves(c_idx, g, o, "add_halves_" + k) for g, o, k in zip(parts, from_sibling, names)]
            pending.append((l, names, exchange_start(sums, "exchange_start")))
            return pending[-1][2][4]
        return on_dws

    for l in reversed(range(n_layers)):
        p = {k: small[k][l] for k in SMALL_NAMES}
        last = l == 0
        dx, small_g[l] = layer_bwd(dx, saved[l], p, cos2, sin2, make_on_dws(l, not last), early=last)
    dx = dx[0]
    finish_exchanges(dx)
    halves = [halves[k] for k in W_NAMES]
    others = pair_share(halves, "pair_share")

    mine = {k: jnp.stack([small_g[l][k] for l in range(n_layers)]) for k in SMALL_NAMES}
    total = small_allreduce(_pack_small(mine, loss_tile[0, 0]), "small_allreduce")
    grad_small, loss = _unpack_small(total, small)

    outs = {}
    for k, mine_half, other_half in zip(W_NAMES, halves, others):
        outs[k] = adamw_big(c_idx, big[k], big_m[k], big_v[k], mine_half, other_half, "adamw_" + k)
    res = adamw(_pack_small(small), _pack_small(grad_small), _pack_small(small_m), _pack_small(small_v), "adamw_small")
    unp = [_unpack_small(t, small)[0] for t in res]
    for k in SMALL_NAMES:
        outs[k] = (grad_small[k],) + tuple(u[k] for u in unp)

    order = ("norm1_g", "w_in", "qk_norm_g", "sink_a", "rpb_c", "w_br_a", "w_br_b", "w_br_c", "w_o", "norm2_g",
             "w_gate_up", "w_down")
    return (loss, dx[None]) + tuple(outs[k][i] for i in range(4) for k in order)
```

```python
import functools
import math

import jax
import jax.numpy as jnp
from jax import lax
from jax.experimental import pallas as pl
from jax.experimental.pallas import tpu as pltpu

F32 = jnp.float32
CDT = jnp.bfloat16
XDT = jnp.bfloat16
ADT = jnp.bfloat16

HEAD = 128
NORM_EPS = 1e-6
ROPE_THETA = 10000.0
A_Q_HEADS, A_KV_HEADS, A_GROUP, A_RADIUS = 8, 2, 4, 128
B_DILS = (1, 4, 16)
B_RADIUS = 64
B_HG = 4
C_HEADS, GRID_W, C_WIN_ROWS, C_WIN_COLS = 8, 64, 8, 16
QKV_W = 9216
COL = dict(qa=0, ka=1024, va=1280, qb=1536, kb=3072, vb=4608, qc=6144, kc=7168, vc=8192)
NEG = -1e30
SCALE = HEAD ** -0.5
N_CHIPS = 4

ADAM_LR, ADAM_B1, ADAM_B2, ADAM_EPS, ADAM_WD, ADAM_STEP = 0.001, 0.9, 0.999, 1e-08, 0.01, 10

VMEM_LIMIT = 56 * 1024 * 1024
MESH = pl.DeviceIdType.MESH


def _pallas(body, **kw):
    return pl.pallas_call(body, **kw)


def _params(sem=None, **kw):
    if sem is not None:
        kw["dimension_semantics"] = sem
    return pltpu.CompilerParams(vmem_limit_bytes=VMEM_LIMIT, **kw)


def _tile(dim, pref, mult=128):
    best = None
    for t in range(mult, min(dim, pref) + 1, mult):
        if dim % t == 0:
            best = t
    return dim if best is None else best


def _sds(shape, dtype):
    return jax.ShapeDtypeStruct(tuple(shape), dtype)


_DIMS = {"nn": (((1,), (0,)), ((), ())), "nt": (((1,), (1,)), ((), ())), "tn": (((0,), (0,)), ((), ()))}


def _matmul(a, b, *, mode, grid, a_spec, b_spec, o_spec, out_shape, acc_shape, name, res=None, res_spec=None,
            after=None):
    nk = grid[2]
    has_res = res is not None
    n_in = 2 + int(has_res) + int(after is not None)

    def body(*refs):
        a_ref, b_ref = refs[:2]
        r_ref = refs[2] if has_res else None
        o_ref, rest = refs[n_in], refs[n_in + 1:]
        p = lax.dot_general(a_ref[...].astype(CDT), b_ref[...].astype(CDT), _DIMS[mode],
                            preferred_element_type=F32)

        def finish(acc):
            if has_res:
                acc = acc + r_ref[...].astype(F32)
            o_ref[...] = acc.astype(o_ref.dtype)

        if nk == 1:
            finish(p)
        else:
            acc_ref = rest[0]
            k = pl.program_id(2)

            @pl.when(k == 0)
            def _():
                acc_ref[...] = p

            @pl.when(k > 0)
            def _():
                acc_ref[...] += p

            @pl.when(k == nk - 1)
            def _():
                finish(acc_ref[...])

    in_specs = [a_spec, b_spec] + ([res_spec] if has_res else [])
    args = (a, b) + ((res,) if has_res else ())
    if after is not None:
        in_specs.append(pl.BlockSpec(after.shape, lambda i, j, kk: (0, 0)))
        args += (after,)
    scratch = [] if nk == 1 else [pltpu.VMEM(acc_shape, F32)]
    return _pallas(body, name=name, grid=grid, in_specs=in_specs, out_specs=o_spec, out_shape=out_shape,
                   scratch_shapes=scratch, compiler_params=_params(("parallel", "parallel", "arbitrary")))(*args)


def mm_x_wcol(a, wg, out_dtype, name, tm_pref=1024, tn_pref=1024, stacked_out=1):
    m, k = a.shape
    ns = wg.shape[2]
    tm, tn = _tile(m, tm_pref, 8), _tile(ns, tn_pref)
    nj = ns // tn
    grid = (m // tm, N_CHIPS * nj, 1)
    a_spec = pl.BlockSpec((tm, k), lambda i, j, kk: (i, 0))
    b_spec = pl.BlockSpec((None, k, tn), lambda i, j, kk: (j // nj, 0, j % nj))
    if stacked_out == 1:
        o_spec = pl.BlockSpec((tm, tn), lambda i, j, kk: (i, j))
        out_shape = _sds((m, N_CHIPS * ns), out_dtype)
    else:
        per = N_CHIPS * nj // stacked_out
        o_spec = pl.BlockSpec((None, tm, tn), lambda i, j, kk: (j // per, i, j % per))
        out_shape = _sds((stacked_out, m, N_CHIPS * ns // stacked_out), out_dtype)
    return _matmul(a, wg, mode="nn", grid=grid, a_spec=a_spec, b_spec=b_spec, o_spec=o_spec,
                   out_shape=out_shape, acc_shape=(tm, tn), name=name)


def mm_x_wcolT(d, wg, name, res=None, tm_pref=1024, tn_pref=512, tk_pref=4096, stacked_in=1, lead=None, after=None):
    kdim, ns = wg.shape[1], wg.shape[2]
    m = d.shape[-2]
    tm, tn, tk = _tile(m, tm_pref, 8), _tile(kdim, tn_pref), _tile(ns, tk_pref)
    nkk = ns // tk
    grid = (m // tm, kdim // tn, N_CHIPS * nkk)
    if lead is not None:
        a_spec = pl.BlockSpec((None, tm, tk), lambda i, j, kk: (lead, i, kk))
    elif stacked_in == 1:
        a_spec = pl.BlockSpec((tm, tk), lambda i, j, kk: (i, kk))
    else:
        per = N_CHIPS * nkk // stacked_in
        a_spec = pl.BlockSpec((None, tm, tk), lambda i, j, kk: (kk // per, i, kk % per))
    b_spec = pl.BlockSpec((None, tn, tk), lambda i, j, kk: (kk // nkk, j, kk % nkk))
    o_spec = pl.BlockSpec((tm, tn), lambda i, j, kk: (i, j))
    return _matmul(d, wg, mode="nt", grid=grid, a_spec=a_spec, b_spec=b_spec, o_spec=o_spec,
                   out_shape=_sds((m, kdim), F32), acc_shape=(tm, tn), name=name,
                   res=res, res_spec=None if res is None else o_spec, after=after)


def mm_aT_d_wcol(a, d, name, tm_pref=512, tk_pref=4096, tn_pref=1024, stacked_in=1, lead=None):
    m, kdim = a.shape
    ntot = d.shape[-1] * stacked_in
    ns = ntot // N_CHIPS
    tm, tkm, tn = _tile(kdim, tm_pref), _tile(m, tk_pref, 8), _tile(ns, tn_pref)
    nj = ns // tn
    grid = (kdim // tm, N_CHIPS * nj, m // tkm)
    a_spec = pl.BlockSpec((tkm, tm), lambda i, j, kk: (kk, i))
    if lead is not None:
        b_spec = pl.BlockSpec((None, tkm, tn), lambda i, j, kk: (lead, kk, j))
    elif stacked_in == 1:
        b_spec = pl.BlockSpec((tkm, tn), lambda i, j, kk: (kk, j))
    else:
        per = N_CHIPS * nj // stacked_in
        b_spec = pl.BlockSpec((None, tkm, tn), lambda i, j, kk: (j // per, kk, j % per))
    o_spec = pl.BlockSpec((None, tm, tn), lambda i, j, kk: (j // nj, i, j % nj))
    return _matmul(a, d, mode="tn", grid=grid, a_spec=a_spec, b_spec=b_spec, o_spec=o_spec,
                   out_shape=_sds((N_CHIPS, kdim, ns), XDT), acc_shape=(tm, tn), name=name)


def mm_x_w(a, w, name, res=None, out_dtype=F32, tm_pref=1024, tn_pref=1024, tk_pref=2048):
    m, k = a.shape
    n = w.shape[1]
    tm, tn, tk = _tile(m, tm_pref, 8), _tile(n, tn_pref), _tile(k, tk_pref)
    grid = (m // tm, n // tn, k // tk)
    o_spec = pl.BlockSpec((tm, tn), lambda i, j, kk: (i, j))
    return _matmul(a, w, mode="nn", grid=grid,
                   a_spec=pl.BlockSpec((tm, tk), lambda i, j, kk: (i, kk)),
                   b_spec=pl.BlockSpec((tk, tn), lambda i, j, kk: (kk, j)),
                   o_spec=o_spec, out_shape=_sds((m, n), out_dtype), acc_shape=(tm, tn), name=name,
                   res=res, res_spec=None if res is None else o_spec)


def mm_x_wT(d, w, name, out_dtype=F32, tm_pref=1024, tn_pref=1024, after=None):
    m, n = d.shape
    k = w.shape[0]
    tm, tn = _tile(m, tm_pref, 8), _tile(k, tn_pref)
    grid = (m // tm, k // tn, 1)
    return _matmul(d, w, mode="nt", grid=grid,
                   a_spec=pl.BlockSpec((tm, n), lambda i, j, kk: (i, 0)),
                   b_spec=pl.BlockSpec((tn, n), lambda i, j, kk: (j, 0)),
                   o_spec=pl.BlockSpec((tm, tn), lambda i, j, kk: (i, j)),
                   out_shape=_sds((m, k), out_dtype), acc_shape=(tm, tn), name=name, after=after)


def mm_aT_d(a, d, name, tm_pref=512, tn_pref=512, tk_pref=4096):
    m, k = a.shape
    n = d.shape[1]
    tm, tn, tk = _tile(k, tm_pref), _tile(n, tn_pref), _tile(m, tk_pref, 8)
    grid = (k // tm, n // tn, m // tk)
    return _matmul(a, d, mode="tn", grid=grid,
                   a_spec=pl.BlockSpec((tk, tm), lambda i, j, kk: (kk, i)),
                   b_spec=pl.BlockSpec((tk, tn), lambda i, j, kk: (kk, j)),
                   o_spec=pl.BlockSpec((tm, tn), lambda i, j, kk: (i, j)),
                   out_shape=_sds((k, n), XDT), acc_shape=(tm, tn), name=name)


def rmsnorm_fwd(x, g, name):
    n, d = x.shape
    tm = _tile(n, 512, 8)

    def body(x_ref, g_ref, h_ref):
        xv = x_ref[...]
        r = lax.rsqrt(jnp.mean(xv * xv, axis=-1, keepdims=True) + NORM_EPS)
        h_ref[...] = (xv * r * g_ref[...]).astype(h_ref.dtype)

    return _pallas(body, name=name, grid=(n // tm,),
                   in_specs=[pl.BlockSpec((tm, d), lambda i: (i, 0)), pl.BlockSpec((1, d), lambda i: (0, 0))],
                   out_specs=pl.BlockSpec((tm, d), lambda i: (i, 0)), out_shape=_sds((n, d), CDT),
                   compiler_params=_params(("parallel",)))(x, g.reshape(1, d))


def rmsnorm_bwd(x, g, dh, dres, name):
    n, d = x.shape
    tm = _tile(n, 256, 8)

    def body(x_ref, g_ref, dh_ref, dres_ref, dx_ref, dxc_ref, dg_ref):
        xv = x_ref[...]
        r = lax.rsqrt(jnp.mean(xv * xv, axis=-1, keepdims=True) + NORM_EPS)
        dhv = dh_ref[...]
        u = dhv * g_ref[...]
        c = jnp.mean(xv * u, axis=-1, keepdims=True)
        dxv = dres_ref[...] + r * u - xv * (r * r * r * c)
        dx_ref[...] = dxv
        dxc_ref[...] = dxv.astype(dxc_ref.dtype)
        part = jnp.broadcast_to(jnp.sum(dhv * xv * r, axis=0, keepdims=True), (8, d))

        @pl.when(pl.program_id(0) == 0)
        def _():
            dg_ref[...] = part

        @pl.when(pl.program_id(0) > 0)
        def _():
            dg_ref[...] += part

    row = pl.BlockSpec((tm, d), lambda i: (i, 0))
    return _pallas(body, name=name, grid=(n // tm,),
                   in_specs=[row, pl.BlockSpec((1, d), lambda i: (0, 0)), row, row],
                   out_specs=[row, row, pl.BlockSpec((8, d), lambda i: (0, 0))],
                   out_shape=[_sds((n, d), F32), _sds((n, d), CDT), _sds((8, d), F32)],
                   compiler_params=_params(("arbitrary",)))(x, g.reshape(1, d), dh, dres)


def _norm_rope(xh, g, cos2, sin2):
    r = lax.rsqrt(jnp.mean(xh * xh, axis=-1, keepdims=True) + NORM_EPS)
    y = xh * r * g
    if cos2 is not None:
        y = y * cos2 + pltpu.roll(y, HEAD // 2, 1) * sin2
    return y


def _norm_rope_bwd(xh, g, cos2, sin2, dout):
    if cos2 is not None:
        dy = dout * cos2 + pltpu.roll(dout * sin2, HEAD // 2, 1)
    else:
        dy = dout
    r = lax.rsqrt(jnp.mean(xh * xh, axis=-1, keepdims=True) + NORM_EPS)
    u = dy * g
    c = jnp.mean(xh * u, axis=-1, keepdims=True)
    return r * u - xh * (r * r * r * c), dy * xh * r


_QK_GROUPS = (("qa", COL["qa"], 8, 0, True), ("ka", COL["ka"], 2, 1, True),
              ("qb", COL["qb"], 12, 2, True), ("kb", COL["kb"], 12, 3, True),
              ("qc", COL["qc"], 8, 4, False), ("kc", COL["kc"], 8, 5, False))
_V_GROUPS = (("va", COL["va"], 2), ("vb", COL["vb"], 12), ("vc", COL["vc"], 8))
_PREP_OUT = (("qa", 8), ("ka", 2), ("va", 2)) + tuple((f"{t}b{g}", 4) for t in "qkv" for g in range(3)) + (
    ("qc", 8), ("kc", 8), ("vc", 8))


def _prep_src(name):
    if name[1] == "b":
        base = COL[name[0] + "b"] + int(name[2]) * B_HG * HEAD
        gain = {"q": 2, "k": 3, "v": None}[name[0]]
        return base, gain, name[0] != "v"
    base = COL[name]
    gain = {"qa": 0, "ka": 1, "va": None, "qc": 4, "kc": 5, "vc": None}[name]
    return base, gain, name in ("qa", "ka")


def _prep_dil(name):
    return B_DILS[int(name[2])] if name[1] == "b" else 1


def _to_classes(val, scr, dil):
    scr[...] = val
    return [scr[pl.ds(r, val.shape[0] // dil, stride=dil), :] for r in range(dil)]


def _from_classes(parts, scr):
    for r, part in enumerate(parts):
        scr[pl.ds(r, part.shape[0], stride=len(parts)), :] = part
    return scr[...]


def _class_spec(tm, dil, width):
    if dil == 1:
        return pl.BlockSpec((tm, width), lambda i: (i, 0))
    return pl.BlockSpec((dil, tm // dil, width), lambda i: (0, i, 0))


def _class_shape(n, dil, width):
    return (n, width) if dil == 1 else (dil, n // dil, width)


def qk_prep(proj, gains, cos2, sin2, name):
    n = proj.shape[0]
    tm = _tile(n, 256, 8)

    def body(p_ref, g_ref, c_ref, s_ref, *refs):
        outs, scr = refs[:-1], refs[-1]
        cos2v, sin2v = c_ref[...], s_ref[...]
        for (nm, heads), o_ref in zip(_PREP_OUT, outs):
            base, gain, rope = _prep_src(nm)
            dil = _prep_dil(nm)
            for h in range(heads):
                hs = slice(h * HEAD, (h + 1) * HEAD)
                xh = p_ref[:, base + h * HEAD: base + (h + 1) * HEAD].astype(F32)
                if gain is None:
                    y = xh
                else:
                    y = _norm_rope(xh, g_ref[gain:gain + 1, :], cos2v if rope else None, sin2v if rope else None)
                if dil == 1:
                    o_ref[:, hs] = y.astype(o_ref.dtype)
                else:
                    for r, part in enumerate(_to_classes(y, scr.at[h % 4], dil)):
                        o_ref[r, :, hs] = part.astype(o_ref.dtype)

    tab = pl.BlockSpec((tm, HEAD), lambda i: (i, 0))
    outs = _pallas(body, name=name, grid=(n // tm,),
                   in_specs=[pl.BlockSpec((tm, QKV_W), lambda i: (i, 0)), pl.BlockSpec((8, HEAD), lambda i: (0, 0)), tab, tab],
                   out_specs=[_class_spec(tm, _prep_dil(nm), h * HEAD) for nm, h in _PREP_OUT],
                   out_shape=[_sds(_class_shape(n, _prep_dil(nm), h * HEAD), CDT) for nm, h in _PREP_OUT],
                   scratch_shapes=[pltpu.VMEM((4, tm, HEAD), F32)],
                   compiler_params=_params(("parallel",)))(proj, gains, cos2, sin2)
    return dict(zip([nm for nm, _ in _PREP_OUT], outs))


def qk_prep_bwd(proj, gains, cos2, sin2, grads, dproj, name):
    n = proj.shape[0]
    tm = _tile(n, 128, 8)
    names = [nm for nm, _ in _PREP_OUT]

    def body(p_ref, g_ref, c_ref, s_ref, *refs):
        g_refs, dp_ref, dg_ref, scr = refs[:len(names)], refs[len(names) + 1], refs[len(names) + 2], refs[-1]
        cos2v, sin2v = c_ref[...], s_ref[...]
        dg = [jnp.zeros((tm, HEAD), F32) for _ in range(6)]
        for (nm, heads), gr in zip(_PREP_OUT, g_refs):
            base, gain, rope = _prep_src(nm)
            dil = _prep_dil(nm)
            for h in range(heads):
                sl = slice(base + h * HEAD, base + (h + 1) * HEAD)
                hs = slice(h * HEAD, (h + 1) * HEAD)
                dout = gr[:, hs] if dil == 1 else _from_classes([gr[r, :, hs] for r in range(dil)], scr.at[h % 4])
                if gain is None:
                    dx = dout
                else:
                    dx, dgr = _norm_rope_bwd(p_ref[:, sl].astype(F32), g_ref[gain:gain + 1, :], cos2v if rope else None,
                                             sin2v if rope else None, dout)
                    dg[gain] = dg[gain] + dgr
                dp_ref[:, sl] = dx.astype(dp_ref.dtype)
        part = jnp.concatenate([jnp.sum(t, axis=0, keepdims=True) for t in dg] + [jnp.zeros((2, HEAD), F32)], axis=0)

        @pl.when(pl.program_id(0) == 0)
        def _():
            dg_ref[...] = part

        @pl.when(pl.program_id(0) > 0)
        def _():
            dg_ref[...] += part

    tab = pl.BlockSpec((tm, HEAD), lambda i: (i, 0))
    dp, dg = _pallas(body, name=name, grid=(n // tm,),
                     in_specs=[pl.BlockSpec((tm, QKV_W), lambda i: (i, 0)), pl.BlockSpec((8, HEAD), lambda i: (0, 0)), tab, tab]
                     + [_class_spec(tm, _prep_dil(nm), h * HEAD) for nm, h in _PREP_OUT] + [ANY],
                     out_specs=[pl.BlockSpec((tm, QKV_W), lambda i: (i, 0)), pl.BlockSpec((8, HEAD), lambda i: (0, 0))],
                     out_shape=[_sds(dproj.shape, dproj.dtype), _sds((8, HEAD), F32)],
                     input_output_aliases={4 + len(names): 0},
                     scratch_shapes=[pltpu.VMEM((4, tm, HEAD), F32)],
                     compiler_params=_params(("arbitrary",)))(proj, gains, cos2, sin2, *[grads[k] for k in names], dproj)
    return dp, dg


def _band_geometry(m, bq_pref, radius):
    bq = min(bq_pref, m)
    return bq, min(bq + 2 * radius, m)


def _band_window(i, bq, radius, m, w):
    start = pl.multiple_of(jnp.clip(i * bq - radius, 0, m - w), 64)
    qpos = i * bq + lax.broadcasted_iota(jnp.int32, (bq, w), 0)
    kpos = start + lax.broadcasted_iota(jnp.int32, (bq, w), 1)
    return start, jnp.abs(kpos - qpos) <= radius


def _band_specs(m, bq, G, nh, seqs):
    lg, nb = seqs // nh, m // bq
    qspec = pl.BlockSpec((bq, nh * G * HEAD), lambda s, i: ((s // lg) * nb + i, s % lg))
    kspec = pl.BlockSpec((m, nh * HEAD), lambda s, i: (s // lg, s % lg))
    return qspec, kspec, lg


def band_attn_fwd(q, k, v, sink, *, seqs, G, nh, radius, name, classes=1, bq_pref=128):
    m = q.shape[0] // classes
    bq, w = _band_geometry(m, bq_pref, radius)
    has_sink = sink is not None

    def body(*refs):
        if has_sink:
            sink_ref, q_ref, k_ref, v_ref, o_ref, lse_ref = refs
        else:
            q_ref, k_ref, v_ref, o_ref, lse_ref = refs
        s_id, i = pl.program_id(0), pl.program_id(1)
        start, valid = _band_window(i, bq, radius, m, w)
        units = [(h, g) for h in range(nh) for g in range(G)]
        sls = [slice((h * G + g) * HEAD, (h * G + g + 1) * HEAD) for h, g in units]
        k_ts = [k_ref[pl.ds(start, w), h * HEAD:(h + 1) * HEAD] for h in range(nh)]
        v_ts = [v_ref[pl.ds(start, w), h * HEAD:(h + 1) * HEAD] for h in range(nh)]
        q_ts = [q_ref[:, sl] for sl in sls]
        sks = [sink_ref[0, (s_id * nh + h) * G + g] for h, g in units] if has_sink else None
        ss = [jnp.where(valid, lax.dot_general(q_t, k_ts[h], _DIMS["nt"], preferred_element_type=F32) * SCALE, NEG)
              for q_t, (h, g) in zip(q_ts, units)]
        mxs = [jnp.max(s, axis=-1, keepdims=True) for s in ss]
        if has_sink:
            mxs = [jnp.maximum(mx, sk) for mx, sk in zip(mxs, sks)]
        ps = [jnp.exp(s - mx) for s, mx in zip(ss, mxs)]
        dens = [jnp.sum(p, axis=-1, keepdims=True) for p in ps]
        if has_sink:
            dens = [den + jnp.exp(sk - mx) for den, sk, mx in zip(dens, sks, mxs)]
        outs = [jnp.dot((p / den).astype(CDT), v_ts[h], preferred_element_type=F32)
                for p, den, (h, g) in zip(ps, dens, units)]
        for sl, o, mx, den in zip(sls, outs, mxs, dens):
            o_ref[:, sl] = o
            lse_ref[:, sl] = jnp.broadcast_to(mx + jnp.log(den), (bq, HEAD))

    qspec, kspec, lg = _band_specs(m, bq, G, nh, seqs)
    in_specs = ([pl.BlockSpec(memory_space=pltpu.SMEM)] if has_sink else []) + [qspec, kspec, kspec]
    args = ((sink,) if has_sink else ()) + (q, k, v)
    return _pallas(body, name=name, grid=(classes * lg, m // bq), in_specs=in_specs, out_specs=[qspec, qspec],
                   out_shape=[_sds(q.shape, F32), _sds(q.shape, F32)],
                   compiler_params=_params(("parallel", "arbitrary")))(*args)


def band_attn_bwd(q, k, v, sink, o, lse, do, dlse, *, seqs, G, nh, radius, name, classes=1, bq_pref=128):
    m = q.shape[0] // classes
    bq, w = _band_geometry(m, bq_pref, radius)
    has_sink, has_dlse = sink is not None, dlse is not None
    assert nh * G <= 8

    def body(*refs):
        refs = list(refs)
        sink_ref = refs.pop(0) if has_sink else None
        q_ref, k_ref, v_ref, o_ref, lse_ref, do_ref = refs[:6]
        refs = refs[6:]
        dlse_ref = refs.pop(0) if has_dlse else None
        dq_ref, dk_ref, dv_ref = refs[:3]
        dsink_ref = refs[3] if has_sink else None
        s_id, i = pl.program_id(0), pl.program_id(1)

        @pl.when(i == 0)
        def _():
            dk_ref[...] = jnp.zeros_like(dk_ref)
            dv_ref[...] = jnp.zeros_like(dv_ref)
            if has_sink:
                dsink_ref[...] = jnp.zeros_like(dsink_ref)

        start, valid = _band_window(i, bq, radius, m, w)
        units = [(h, g) for h in range(nh) for g in range(G)]
        sls = [slice((h * G + g) * HEAD, (h * G + g + 1) * HEAD) for h, g in units]
        kss = [slice(h * HEAD, (h + 1) * HEAD) for h in range(nh)]
        k_ts = [k_ref[pl.ds(start, w), ks] for ks in kss]
        v_ts = [v_ref[pl.ds(start, w), ks] for ks in kss]
        q_ts = [q_ref[:, sl] for sl in sls]
        lse_ts = [lse_ref[:, sl][:, :1] for sl in sls]
        do_ts = [do_ref[:, sl] for sl in sls]
        deltas = [jnp.sum(do_t * o_ref[:, sl], axis=-1, keepdims=True) for do_t, sl in zip(do_ts, sls)]
        dlse_ts = [dlse_ref[:, sl][:, :1] for sl in sls] if has_dlse else None
        dk_old = [dk_ref[pl.ds(start, w), ks] for ks in kss]
        dv_old = [dv_ref[pl.ds(start, w), ks] for ks in kss]
        dsink_old = dsink_ref[...] if has_sink else None

        ps = [jnp.exp(jnp.where(valid, lax.dot_general(q_t, k_ts[h], _DIMS["nt"], preferred_element_type=F32) * SCALE, NEG)
                      - lse_t) for q_t, lse_t, (h, g) in zip(q_ts, lse_ts, units)]
        do_cs = [do_t.astype(CDT) for do_t in do_ts]
        dps = [lax.dot_general(do_c, v_ts[h], _DIMS["nt"], preferred_element_type=F32) for do_c, (h, g) in zip(do_cs, units)]
        ts = [dp - delta for dp, delta in zip(dps, deltas)]
        if has_dlse:
            ts = [t + dl for t, dl in zip(ts, dlse_ts)]
        dss = [((p * t) * SCALE).astype(CDT) for p, t in zip(ps, ts)]
        dqs = [jnp.dot(ds, k_ts[h], preferred_element_type=F32) for ds, (h, g) in zip(dss, units)]
        dvs = [lax.dot_general(p.astype(CDT), do_c, _DIMS["tn"], preferred_element_type=F32) for p, do_c in zip(ps, do_cs)]
        dks = [lax.dot_general(ds, q_t, _DIMS["tn"], preferred_element_type=F32) for ds, q_t in zip(dss, q_ts)]
        dk_new = [dk_old[h] + sum(dks[h * G + g] for g in range(G)) for h in range(nh)]
        dv_new = [dv_old[h] + sum(dvs[h * G + g] for g in range(G)) for h in range(nh)]
        if has_sink:
            rows = []
            for (h, g), lse_t, delta in zip(units, lse_ts, deltas):
                sk = sink_ref[0, (s_id * nh + h) * G + g]
                rows.append(jnp.broadcast_to(-jnp.sum(jnp.exp(sk - lse_t) * delta, axis=0, keepdims=True), (1, HEAD)))
            rows += [jnp.zeros((1, HEAD), F32)] * (8 - len(rows))
            dsink_new = dsink_old + jnp.concatenate(rows, axis=0)

        for sl, dq in zip(sls, dqs):
            dq_ref[:, sl] = dq
        for h, ks in enumerate(kss):
            dk_ref[pl.ds(start, w), ks] = dk_new[h]
            dv_ref[pl.ds(start, w), ks] = dv_new[h]
        if has_sink:
            dsink_ref[...] = dsink_new

    qspec, kspec, lg = _band_specs(m, bq, G, nh, seqs)
    kin = pl.BlockSpec(kspec.block_shape, kspec.index_map, pipeline_mode=pl.Buffered(1))
    in_specs = ([pl.BlockSpec(memory_space=pltpu.SMEM)] if has_sink else []) + [qspec, kin, kin, qspec, qspec, qspec]
    in_specs += [qspec] if has_dlse else []
    args = ((sink,) if has_sink else ()) + (q, k, v, o, lse, do) + ((dlse,) if has_dlse else ())
    out_specs = [qspec, kspec, kspec]
    out_shape = [_sds(q.shape, F32), _sds(k.shape, F32), _sds(k.shape, F32)]
    if has_sink:
        out_specs.append(pl.BlockSpec((None, 8, HEAD), lambda s, i: (s, 0, 0)))
        out_shape.append(_sds((seqs // nh, 8, HEAD), F32))
    return _pallas(body, name=name, grid=(classes * lg, m // bq), in_specs=in_specs, out_specs=out_specs,
                   out_shape=out_shape, compiler_params=_params(("parallel", "arbitrary")))(*args)


def _group_weights(lses):
    mx = jnp.maximum(jnp.maximum(lses[0], lses[1]), lses[2])
    e = [jnp.exp(l - mx) for l in lses]
    tot = e[0] + e[1] + e[2]
    return [t / tot for t in e]


def _read_group(ref, dil, h, scr):
    hs = slice(h * HEAD, (h + 1) * HEAD)
    return ref[:, hs] if dil == 1 else _from_classes([ref[r, :, hs] for r in range(dil)], scr)


def b_combine_fwd(os_, lses, name):
    n, wd = os_[0].shape
    tm = _tile(n, 512, 8)

    def body(o0, o1, o2, l0, l1, l2, out_ref, scr):
        for h in range(wd // HEAD):
            ls = [_read_group(ref, dil, h, scr.at[g]) for g, (ref, dil) in enumerate(zip((l0, l1, l2), B_DILS))]
            ovs = [_read_group(ref, dil, h, scr.at[3 + g]) for g, (ref, dil) in enumerate(zip((o0, o1, o2), B_DILS))]
            wts = _group_weights(ls)
            out_ref[:, h * HEAD:(h + 1) * HEAD] = wts[0] * ovs[0] + wts[1] * ovs[1] + wts[2] * ovs[2]

    specs = [_class_spec(tm, dil, wd) for dil in B_DILS]
    return _pallas(body, name=name, grid=(n // tm,), in_specs=specs * 2, out_specs=pl.BlockSpec((tm, wd), lambda i: (i, 0)),
                   out_shape=_sds((n, wd), F32), scratch_shapes=[pltpu.VMEM((6, tm, HEAD), F32)],
                   compiler_params=_params(("parallel",)))(*os_, *lses)


def b_combine_bwd(dout, os_, lses, name):
    n, wd = dout.shape
    tm = _tile(n, 256, 8)

    def body(d_ref, o0, o1, o2, l0, l1, l2, do0, do1, do2, dl0, dl1, dl2, scr):
        for h in range(wd // HEAD):
            hs = slice(h * HEAD, (h + 1) * HEAD)
            dv = d_ref[:, hs]
            ls = [_read_group(ref, dil, h, scr.at[g]) for g, (ref, dil) in enumerate(zip((l0, l1, l2), B_DILS))]
            ovs = [_read_group(ref, dil, h, scr.at[3 + g]) for g, (ref, dil) in enumerate(zip((o0, o1, o2), B_DILS))]
            wts = _group_weights(ls)
            dws = [jnp.broadcast_to(jnp.sum(dv * ov, axis=-1, keepdims=True), (tm, HEAD)) for ov in ovs]
            mean = wts[0] * dws[0] + wts[1] * dws[1] + wts[2] * dws[2]
            for g, (wt, dw, do_ref, dl_ref, dil) in enumerate(zip(wts, dws, (do0, do1, do2), (dl0, dl1, dl2), B_DILS)):
                if dil == 1:
                    do_ref[:, hs] = wt * dv
                    dl_ref[:, hs] = wt * (dw - mean)
                else:
                    for r, part in enumerate(_to_classes(wt * dv, scr.at[g], dil)):
                        do_ref[r, :, hs] = part
                    for r, part in enumerate(_to_classes(wt * (dw - mean), scr.at[3 + g], dil)):
                        dl_ref[r, :, hs] = part

    specs = [_class_spec(tm, dil, wd) for dil in B_DILS]
    outs = _pallas(body, name=name, grid=(n // tm,), in_specs=[pl.BlockSpec((tm, wd), lambda i: (i, 0))] + specs * 2,
                   out_specs=specs * 2, out_shape=[_sds(_class_shape(n, dil, wd), F32) for dil in B_DILS] * 2,
                   scratch_shapes=[pltpu.VMEM((6, tm, HEAD), F32)],
                   compiler_params=_params(("parallel",)))(dout, *os_, *lses)
    return outs[:3], outs[3:]


def _c_rows(n):
    rows = n // GRID_W
    return rows, min(C_WIN_ROWS, rows)


def _c_row_start(r, rows, wr):
    return jnp.clip(r - wr // 2, 0, rows - wr)


def _c_bias_index(r, rows, wr):
    return _c_row_start(r, rows, wr) - r + (C_WIN_ROWS - 1)


def _col_shift_select(tile, cq, inverse):
    lanes = tile.shape[1]
    for b in range(6):
        amt = (lanes - (1 << b)) if inverse else (1 << b)
        tile = jnp.where(((cq >> b) & 1) == 1, pltpu.roll(tile, amt, 1), tile)
    return tile


def rpb_expand(rwin, name):
    lanes = rwin.shape[-1]

    def body(r_ref, b_ref):
        cq = lax.broadcasted_iota(jnp.int32, (GRID_W, lanes), 0)
        ck = lax.broadcasted_iota(jnp.int32, (GRID_W, lanes), 1) % GRID_W
        cs = jnp.clip(cq - C_WIN_COLS // 2, 0, GRID_W - C_WIN_COLS)
        ok = (ck >= cs) & (ck < cs + C_WIN_COLS)
        for i0 in range(C_WIN_ROWS):
            tile = jnp.broadcast_to(r_ref[i0], (GRID_W, lanes))
            tile = pltpu.roll(tile, lanes - (C_WIN_COLS - 1), 1)
            tile = _col_shift_select(tile, cq, False)
            b_ref[i0] = jnp.where(ok, tile, NEG)

    return _pallas(body, name=name, grid=(C_HEADS,),
                   in_specs=[pl.BlockSpec((None, C_WIN_ROWS, 1, lanes), lambda h: (h, 0, 0, 0))],
                   out_specs=pl.BlockSpec((None, C_WIN_ROWS, GRID_W, lanes), lambda h: (h, 0, 0, 0)),
                   out_shape=_sds((C_HEADS, C_WIN_ROWS, GRID_W, lanes), F32),
                   compiler_params=_params(("parallel",)))(rwin)


def rpb_reduce(dbias, name):
    lanes = dbias.shape[-1]
    wr = lanes // GRID_W

    def body(d_ref, o_ref):
        cq = lax.broadcasted_iota(jnp.int32, (GRID_W, lanes), 0)
        o_ref[...] = jnp.zeros_like(o_ref)
        for i0 in range(C_WIN_ROWS):
            tile = _col_shift_select(d_ref[i0], cq, True)
            tile = pltpu.roll(tile, C_WIN_COLS - 1, 1)
            vec = jnp.sum(tile, axis=0, keepdims=True)
            for w in range(wr):
                o_ref[i0 + w:i0 + w + 1, :] += vec[:, w * GRID_W:(w + 1) * GRID_W]

    return _pallas(body, name=name, grid=(C_HEADS,),
                   in_specs=[pl.BlockSpec((None, C_WIN_ROWS, GRID_W, lanes), lambda h: (h, 0, 0, 0))],
                   out_specs=pl.BlockSpec((None, 16, GRID_W), lambda h: (h, 0, 0)),
                   out_shape=_sds((C_HEADS, 16, GRID_W), F32),
                   compiler_params=_params(("parallel",)))(dbias)


def _store_or_add(ref, val, first):
    @pl.when(first)
    def _():
        ref[...] = val

    @pl.when(jnp.logical_not(first))
    def _():
        ref[...] += val


def c_attn_fwd(q, k, v, bias, name, nh=C_HEADS):
    n = q.shape[0]
    rows, wr = _c_rows(n)
    wk = wr * GRID_W

    def body(q_ref, k_ref, v_ref, b_ref, o_ref, lse_ref):
        r = pl.program_id(1)
        start = pl.multiple_of(_c_row_start(r, rows, wr) * GRID_W, GRID_W)
        sls = [slice(h * HEAD, (h + 1) * HEAD) for h in range(nh)]
        ss = [lax.dot_general(q_ref[:, sl], k_ref[pl.ds(start, wk), sl], _DIMS["nt"], preferred_element_type=F32)
              * SCALE + b_ref[h] for h, sl in enumerate(sls)]
        mxs = [jnp.max(s, axis=-1, keepdims=True) for s in ss]
        ps = [jnp.exp(s - mx) for s, mx in zip(ss, mxs)]
        dens = [jnp.sum(p, axis=-1, keepdims=True) for p in ps]
        outs = [jnp.dot((p / den).astype(CDT), v_ref[pl.ds(start, wk), sl], preferred_element_type=F32)
                for p, den, sl in zip(ps, dens, sls)]
        for sl, o, mx, den in zip(sls, outs, mxs, dens):
            o_ref[:, sl] = o
            lse_ref[:, sl] = jnp.broadcast_to(mx + jnp.log(den), (GRID_W, HEAD))

    qspec = pl.BlockSpec((GRID_W, nh * HEAD), lambda h, r: (r, h))
    kspec = pl.BlockSpec((n, nh * HEAD), lambda h, r: (0, h), pipeline_mode=pl.Buffered(1))
    bspec = pl.BlockSpec((nh, None, GRID_W, wk), lambda h, r: (h, _c_bias_index(r, rows, wr), 0, 0))
    return _pallas(body, name=name, grid=(C_HEADS // nh, rows), in_specs=[qspec, kspec, kspec, bspec],
                   out_specs=[qspec, qspec], out_shape=[_sds(q.shape, F32), _sds(q.shape, F32)],
                   compiler_params=_params(("parallel", "arbitrary")))(q, k, v, bias)


def c_attn_bwd(q, k, v, bias, o, lse, do, name, nh=4):
    n = q.shape[0]
    rows, wr = _c_rows(n)
    wk = wr * GRID_W

    def body(q_ref, k_ref, v_ref, b_ref, o_ref, lse_ref, do_ref, dq_ref, dk_ref, dv_ref, db_ref):
        r = pl.program_id(1)
        rs = _c_row_start(r, rows, wr)
        start = pl.multiple_of(rs * GRID_W, GRID_W)

        @pl.when(r == 0)
        def _():
            dk_ref[...] = jnp.zeros_like(dk_ref)
            dv_ref[...] = jnp.zeros_like(dv_ref)

        prev = _c_row_start(jnp.maximum(r - 1, 0), rows, wr) - jnp.maximum(r - 1, 0)
        first = (r == 0) | (prev != rs - r)
        sls = [slice(h * HEAD, (h + 1) * HEAD) for h in range(nh)]
        k_ts = [k_ref[pl.ds(start, wk), sl] for sl in sls]
        v_ts = [v_ref[pl.ds(start, wk), sl] for sl in sls]
        q_ts = [q_ref[:, sl] for sl in sls]
        lse_ts = [lse_ref[:, sl][:, :1] for sl in sls]
        do_ts = [do_ref[:, sl] for sl in sls]
        deltas = [jnp.sum(do_t * o_ref[:, sl], axis=-1, keepdims=True) for do_t, sl in zip(do_ts, sls)]
        biases = [b_ref[h] for h in range(nh)]
        dk_old = [dk_ref[pl.ds(start, wk), sl] for sl in sls]
        dv_old = [dv_ref[pl.ds(start, wk), sl] for sl in sls]

        ps = [jnp.exp(lax.dot_general(q_t, k_t, _DIMS["nt"], preferred_element_type=F32) * SCALE + b - lse_t)
              for q_t, k_t, b, lse_t in zip(q_ts, k_ts, biases, lse_ts)]
        do_cs = [do_t.astype(CDT) for do_t in do_ts]
        dps = [lax.dot_general(do_c, v_t, _DIMS["nt"], preferred_element_type=F32) for do_c, v_t in zip(do_cs, v_ts)]
        dss = [p * (dp - delta) for p, dp, delta in zip(ps, dps, deltas)]
        ds_cs = [(ds * SCALE).astype(CDT) for ds in dss]
        dqs = [jnp.dot(ds_c, k_t, preferred_element_type=F32) for ds_c, k_t in zip(ds_cs, k_ts)]
        dv_new = [old + lax.dot_general(p.astype(CDT), do_c, _DIMS["tn"], preferred_element_type=F32)
                  for old, p, do_c in zip(dv_old, ps, do_cs)]
        dk_new = [old + lax.dot_general(ds_c, q_t, _DIMS["tn"], preferred_element_type=F32)
                  for old, ds_c, q_t in zip(dk_old, ds_cs, q_ts)]

        for h, sl in enumerate(sls):
            dq_ref[:, sl] = dqs[h]
            dk_ref[pl.ds(start, wk), sl] = dk_new[h]
            dv_ref[pl.ds(start, wk), sl] = dv_new[h]
        for h in range(nh):
            _store_or_add(db_ref.at[h], dss[h], first)

    qspec = pl.BlockSpec((GRID_W, nh * HEAD), lambda h, r: (r, h))
    kspec = pl.BlockSpec((n, nh * HEAD), lambda h, r: (0, h))
    kin = pl.BlockSpec((n, nh * HEAD), lambda h, r: (0, h), pipeline_mode=pl.Buffered(1))
    bspec = pl.BlockSpec((nh, None, GRID_W, wk), lambda h, r: (h, _c_bias_index(r, rows, wr), 0, 0))
    return _pallas(body, name=name, grid=(C_HEADS // nh, rows),
                   in_specs=[qspec, kin, kin, bspec, qspec, qspec, qspec],
                   out_specs=[qspec, kspec, kspec, bspec],
                   out_shape=[_sds(q.shape, F32), _sds(k.shape, F32), _sds(k.shape, F32), _sds(bias.shape, F32)],
                   compiler_params=_params(("parallel", "arbitrary")))(q, k, v, bias, o, lse, do)


def _sigmoid(z):
    return 1.0 / (1.0 + jnp.exp(-z))


def _gate_specs(n, d):
    tm, tn = _tile(n, 256, 8), _tile(math.gcd(d, QKV_W), 1024)
    nj = d // tn
    tile = pl.BlockSpec((tm, tn), lambda i, j: (i, j))
    gl = [pl.BlockSpec((tm, tn), functools.partial(lambda i, j, b: (i, (QKV_W + b * d) // tn + j), b=b)) for b in range(3)]
    return tm, tn, nj, tile, gl


def gate_merge(proj, ys, name):
    n, d = ys[0].shape
    tm, tn, nj, tile, gl = _gate_specs(n, d)

    def body(g0, g1, g2, y0, y1, y2, out_ref):
        acc = (_sigmoid(g0[...].astype(F32)) * y0[...] + _sigmoid(g1[...].astype(F32)) * y1[...]
               + _sigmoid(g2[...].astype(F32)) * y2[...])
        out_ref[...] = acc.astype(out_ref.dtype)

    return _pallas(body, name=name, grid=(n // tm, nj), in_specs=gl + [tile] * 3, out_specs=tile,
                   out_shape=_sds((n, d), CDT), compiler_params=_params(("parallel", "parallel")))(proj, proj, proj, *ys)


def gate_bwd(proj, ys, dmerged, name):
    n, d = dmerged.shape
    tm, tn, nj, _, _ = _gate_specs(n, d)

    def body(g_ref, y0, y1, y2, dm_ref, dy_ref, dp_ref):
        b = pl.program_id(2)
        y = jnp.where(b == 0, y0[...], jnp.where(b == 1, y1[...], y2[...]))
        dm = dm_ref[...]
        sg = _sigmoid(g_ref[...].astype(F32))
        dy_ref[...] = (dm * sg).astype(dy_ref.dtype)
        dp_ref[...] = (dm * y * sg * (1.0 - sg)).astype(dp_ref.dtype)

    gl = pl.BlockSpec((tm, tn), lambda i, j, b: (i, QKV_W // tn + b * nj + j))
    tile = pl.BlockSpec((tm, tn), lambda i, j, b: (i, j))
    return _pallas(body, name=name, grid=(n // tm, nj, 3), in_specs=[gl, tile, tile, tile, tile],
                   out_specs=[pl.BlockSpec((None, tm, tn), lambda i, j, b: (b, i, j)), gl],
                   out_shape=[_sds((3, n, d), CDT), _sds(proj.shape, CDT)],
                   compiler_params=_params(("parallel", "parallel", "arbitrary")))(proj, *ys, dmerged)


def gate_up_swiglu(h2, wg, name, after=None):
    n, d = h2.shape
    ns = wg.shape[2]
    ff = 2 * ns
    tm, tn = _tile(n, 512, 8), _tile(ns, 1408)
    nj = ns // tn
    extra = [] if after is None else [after]

    def body(a_ref, bg_ref, bu_ref, *refs):
        gu_ref, act_ref = refs[len(extra):]
        a = a_ref[...].astype(CDT)
        gt = jnp.dot(a, bg_ref[...].astype(CDT), preferred_element_type=F32)
        up = jnp.dot(a, bu_ref[...].astype(CDT), preferred_element_type=F32)
        gu_ref[0] = gt.astype(gu_ref.dtype)
        gu_ref[1] = up.astype(gu_ref.dtype)
        act_ref[...] = (gt * _sigmoid(gt) * up).astype(act_ref.dtype)

    return _pallas(body, name=name, grid=(n // tm, 2 * nj),
                   in_specs=[pl.BlockSpec((tm, d), lambda i, j: (i, 0)),
                             pl.BlockSpec((None, d, tn), lambda i, j: (j // nj, 0, j % nj)),
                             pl.BlockSpec((None, d, tn), lambda i, j: (2 + j // nj, 0, j % nj))]
                   + [pl.BlockSpec(a.shape, lambda i, j: (0, 0)) for a in extra],
                   out_specs=[pl.BlockSpec((2, tm, tn), lambda i, j: (0, i, j)), pl.BlockSpec((tm, tn), lambda i, j: (i, j))],
                   out_shape=[_sds((2, n, ff), ADT), _sds((n, ff), CDT)],
                   compiler_params=_params(("parallel", "parallel")))(h2, wg, wg, *extra)


def d_gate_up(dx, w_down, gu, name):
    n, d = dx.shape
    ff = w_down.shape[0]
    tm, tn = _tile(n, 1024, 8), _tile(ff, 512)

    def body(a_ref, b_ref, gu_ref, d_ref):
        da = lax.dot_general(a_ref[...].astype(CDT), b_ref[...].astype(CDT), _DIMS["nt"], preferred_element_type=F32)
        gt, up = gu_ref[0].astype(F32), gu_ref[1].astype(F32)
        sg = _sigmoid(gt)
        d_ref[0] = (da * up * (sg + gt * sg * (1.0 - sg))).astype(d_ref.dtype)
        d_ref[1] = (da * gt * sg).astype(d_ref.dtype)

    blk = pl.BlockSpec((2, tm, tn), lambda i, j: (0, i, j))
    return _pallas(body, name=name, grid=(n // tm, ff // tn),
                   in_specs=[pl.BlockSpec((tm, d), lambda i, j: (i, 0)), pl.BlockSpec((tn, d), lambda i, j: (j, 0)), blk],
                   out_specs=blk, out_shape=_sds((2, n, ff), CDT),
                   compiler_params=_params(("parallel", "parallel")))(dx, w_down, gu)


def loss_head(y, target, name):
    n, d = y.shape
    tm = _tile(n, 512, 8)
    nsteps = n // tm

    def body(y_ref, t_ref, l_ref, dy_ref, dyc_ref, acc_ref):
        i = pl.program_id(0)
        e = y_ref[...] - t_ref[...]
        dy_ref[...] = e * (1.0 / d)
        dyc_ref[...] = (e * (1.0 / d)).astype(dyc_ref.dtype)
        part = jnp.sum((e * e).reshape(tm // 8, 8, d), axis=0)

        @pl.when(i == 0)
        def _():
            acc_ref[...] = part

        @pl.when(i > 0)
        def _():
            acc_ref[...] += part

        @pl.when(i == nsteps - 1)
        def _():
            tot = jnp.sum(jnp.sum(acc_ref[...], axis=1, keepdims=True), axis=0, keepdims=True) * (0.5 / d)
            l_ref[...] = jnp.broadcast_to(tot, (8, HEAD))

    row = pl.BlockSpec((tm, d), lambda i: (i, 0))
    return _pallas(body, name=name, grid=(nsteps,), in_specs=[row, row],
                   out_specs=[pl.BlockSpec((8, HEAD), lambda i: (0, 0)), row, row],
                   out_shape=[_sds((8, HEAD), F32), _sds((n, d), F32), _sds((n, d), CDT)],
                   scratch_shapes=[pltpu.VMEM((8, d), F32)],
                   compiler_params=_params(("arbitrary",)))(y, target)


def adamw(w, g, m, v, name):
    r, c = w.shape
    tr = _tile(r, max(8, (1 << 19) // c), 8)
    c1 = 1.0 - ADAM_B1 ** ADAM_STEP
    c2 = 1.0 - ADAM_B2 ** ADAM_STEP

    def body(w_ref, g_ref, m_ref, v_ref, d_ref, mo_ref, vo_ref):
        gv = g_ref[...]
        mn = ADAM_B1 * m_ref[...] + (1.0 - ADAM_B1) * gv
        vn = ADAM_B2 * v_ref[...] + (1.0 - ADAM_B2) * (gv * gv)
        d_ref[...] = -ADAM_LR * ((mn / c1) / (jnp.sqrt(vn / c2) + ADAM_EPS) + ADAM_WD * w_ref[...])
        mo_ref[...] = mn
        vo_ref[...] = vn

    row = pl.BlockSpec((tr, c), lambda i: (i, 0))
    return _pallas(body, name=name, grid=(r // tr,), in_specs=[row] * 4, out_specs=[row] * 3,
                   out_shape=[_sds((r, c), F32)] * 3, compiler_params=_params(("parallel",)))(w, g, m, v)


ANY = pl.BlockSpec(memory_space=pl.ANY)


def _place():
    x, y, c = lax.axis_index("x"), lax.axis_index("y"), lax.axis_index("c")
    return x, y, c, [(1 - x, y), (x, 1 - y), (1 - x, 1 - y)]


def _rcopy(src, dst, send_sems, recv_sems, k, to):
    return pltpu.make_async_remote_copy(src_ref=src, dst_ref=dst, send_sem=send_sems.at[k], recv_sem=recv_sems.at[k],
                                        device_id=to, device_id_type=MESH)


def cast_place(chip_idx, shards, layer, name):
    _, k, ns = shards.shape
    tr = _tile(k, max(16, (1 << 19) // ns), 16)

    def body(k_ref, s_ref, o_ref):
        o_ref[...] = s_ref[...].astype(o_ref.dtype)

    gs = pltpu.PrefetchScalarGridSpec(
        num_scalar_prefetch=1, grid=(k // tr,),
        in_specs=[pl.BlockSpec((None, tr, ns), lambda i, k_ref: (layer, i, 0))],
        out_specs=pl.BlockSpec((None, tr, ns), lambda i, k_ref: (k_ref[0], i, 0)))
    return _pallas(body, name=name, grid_spec=gs, out_shape=_sds((N_CHIPS, k, ns), CDT),
                   compiler_params=_params(("parallel",)))(chip_idx, shards)


HBM = pl.BlockSpec(memory_space=pltpu.HBM)
SEM = pl.BlockSpec(memory_space=pltpu.SEMAPHORE)
EFFECT = pltpu.SideEffectType.DATAFLOW_SIDE_EFFECTING


def _in_hbm(a):
    return pltpu.with_memory_space_constraint(a, pltpu.HBM)


def _gather_copies(refs, send_sems, recv_sems):
    x, y, c, chips = _place()
    me = 2 * x + y
    out = []
    for t, ref in enumerate(refs):
        kh = ref.shape[1] // 2
        for j, (px, py) in enumerate(chips):
            send = _rcopy(ref.at[me, pl.ds(c * kh, kh)], ref.at[me, pl.ds(c * kh, kh)], send_sems, recv_sems,
                          3 * t + j, (px, py, c))
            land = ref.at[2 * px + py, pl.ds(c * kh, kh)]
            out.append((send, _rcopy(land, land, send_sems, recv_sems, 3 * t + j, (px, py, c))))
    return out


def gather_start(bufs, after, name):
    nt = len(bufs)
    after = list(after) if isinstance(after, (list, tuple)) else [after]
    na = len(after)

    def body(*refs):
        ins, send_sems, recv_sems, token = refs[:nt], refs[nt + na], refs[nt + na + 1], refs[-1]
        for send, _ in _gather_copies(ins, send_sems, recv_sems):
            send.start()
        token[...] = jnp.zeros_like(token)

    outs = _pallas(body, name=name, in_specs=[HBM] * nt + [ANY] * na,
                   out_specs=(SEM, SEM) + (HBM,) * nt + (pl.BlockSpec(memory_space=pltpu.VMEM),),
                   out_shape=(pltpu.SemaphoreType.DMA((3 * nt,)), pltpu.SemaphoreType.DMA((3 * nt,)))
                   + tuple(pltpu.HBM(b.shape, b.dtype) for b in bufs) + (_sds((8, HEAD), F32),),
                   input_output_aliases={t: 2 + t for t in range(nt)},
                   compiler_params=pltpu.CompilerParams(has_side_effects=EFFECT))(*[_in_hbm(b) for b in bufs], *after)
    return outs[0], outs[1], list(outs[2:2 + nt]), outs[-1]


def gather_wait(bufs, send_sems, recv_sems, after, name):
    nt = len(bufs)

    def body(*refs):
        ins, s_sems, r_sems = refs[:nt], refs[nt], refs[nt + 1]
        for send, land in _gather_copies(ins, s_sems, r_sems):
            send.wait_send()
            land.wait_recv()

    return _pallas(body, name=name, in_specs=[HBM] * nt + [SEM, SEM, ANY], out_specs=[HBM] * nt,
                   out_shape=[pltpu.HBM(b.shape, b.dtype) for b in bufs],
                   input_output_aliases={t: t for t in range(nt)},
                   compiler_params=pltpu.CompilerParams(has_side_effects=EFFECT))(*bufs, send_sems, recv_sems, after)


def _forward_copies(refs, send_sems, recv_sems):
    x, y, c, chips = _place()
    out = []
    for t, ref in enumerate(refs):
        kh = ref.shape[1] // 2
        for j, (px, py) in enumerate(chips):
            mine = ref.at[2 * px + py, pl.ds(c * kh, kh)]
            land = ref.at[2 * px + py, pl.ds((1 - c) * kh, kh)]
            out.append((_rcopy(mine, mine, send_sems, recv_sems, 3 * t + j, (x, y, 1 - c)),
                        _rcopy(land, land, send_sems, recv_sems, 3 * t + j, (x, y, 1 - c))))
    return out


def forward_start(bufs, after, name):
    nt = len(bufs)

    def body(*refs):
        ins, send_sems, recv_sems, token = refs[:nt], refs[nt + 1], refs[nt + 2], refs[-1]
        for send, _ in _forward_copies(ins, send_sems, recv_sems):
            send.start()
        token[...] = jnp.zeros_like(token)

    outs = _pallas(body, name=name, in_specs=[HBM] * nt + [ANY],
                   out_specs=(SEM, SEM) + (HBM,) * nt + (pl.BlockSpec(memory_space=pltpu.VMEM),),
                   out_shape=(pltpu.SemaphoreType.DMA((3 * nt,)), pltpu.SemaphoreType.DMA((3 * nt,)))
                   + tuple(pltpu.HBM(b.shape, b.dtype) for b in bufs) + (_sds((8, HEAD), F32),),
                   input_output_aliases={t: 2 + t for t in range(nt)},
                   compiler_params=pltpu.CompilerParams(has_side_effects=EFFECT))(*[_in_hbm(b) for b in bufs], after)
    return outs[0], outs[1], list(outs[2:2 + nt]), outs[-1]


def forward_wait(bufs, send_sems, recv_sems, after, name):
    nt = len(bufs)

    def body(*refs):
        ins, s_sems, r_sems = refs[:nt], refs[nt], refs[nt + 1]
        for send, land in _forward_copies(ins, s_sems, r_sems):
            send.wait_send()
            land.wait_recv()

    return _pallas(body, name=name, in_specs=[HBM] * nt + [SEM, SEM, ANY], out_specs=[HBM] * nt,
                   out_shape=[pltpu.HBM(b.shape, b.dtype) for b in bufs],
                   input_output_aliases={t: t for t in range(nt)},
                   compiler_params=pltpu.CompilerParams(has_side_effects=EFFECT))(*bufs, send_sems, recv_sems, after)


def pair_forward(bufs, name):
    nt = len(bufs)

    def body(*refs):
        outs = refs[nt:2 * nt]
        send_sems, recv_sems = refs[2 * nt:]
        x, y, c, chips = _place()
        cps = []
        for t in range(nt):
            kh = outs[t].shape[1] // 2
            for j, (px, py) in enumerate(chips):
                blk = outs[t].at[2 * px + py, pl.ds(c * kh, kh)]
                cps.append(_rcopy(blk, blk, send_sems, recv_sems, 3 * t + j, (x, y, 1 - c)))
                cps[-1].start()
        for t in range(nt):
            kh = outs[t].shape[1] // 2
            for j, (px, py) in enumerate(chips):
                blk = outs[t].at[2 * px + py, pl.ds((1 - c) * kh, kh)]
                _rcopy(blk, blk, send_sems, recv_sems, 3 * t + j, (x, y, 1 - c)).wait_recv()
        for cp in cps:
            cp.wait_send()

    return _pallas(body, name=name, in_specs=[ANY] * nt, out_specs=[ANY] * nt,
                   out_shape=[_sds(b.shape, b.dtype) for b in bufs],
                   input_output_aliases={t: t for t in range(nt)},
                   scratch_shapes=[pltpu.SemaphoreType.DMA((3 * nt,)), pltpu.SemaphoreType.DMA((3 * nt,))],
                   compiler_params=pltpu.CompilerParams(has_side_effects=True))(*bufs)


def pair_exchange(grads, name):
    nt = len(grads)

    def body(*refs):
        ins, outs = refs[:nt], refs[nt:2 * nt]
        send_sems, recv_sems = refs[2 * nt:]
        x, y, c, _ = _place()
        sibling = (x, y, 1 - c)
        cps = []
        for t in range(nt):
            kh = ins[t].shape[1] // 2
            cps.append(_rcopy(ins[t].at[:, pl.ds((1 - c) * kh, kh), :], outs[t], send_sems, recv_sems, t, sibling))
            cps[-1].start()
        for cp in cps:
            cp.wait_recv()
        for cp in cps:
            cp.wait_send()

    return _pallas(body, name=name, in_specs=[ANY] * nt, out_specs=[ANY] * nt,
                   out_shape=[_sds((N_CHIPS, g.shape[1] // 2, g.shape[2]), g.dtype) for g in grads],
                   scratch_shapes=[pltpu.SemaphoreType.DMA((nt,)), pltpu.SemaphoreType.DMA((nt,))],
                   compiler_params=pltpu.CompilerParams(has_side_effects=True))(*grads)


def _pair_copies(grads, lands, send_sems, recv_sems):
    x, y, c, _ = _place()
    out = []
    for t, (g, l) in enumerate(zip(grads, lands)):
        kh = g.shape[1] // 2
        out.append(_rcopy(g.at[:, pl.ds((1 - c) * kh, kh), :], l, send_sems, recv_sems, t, (x, y, 1 - c)))
    return out


def pair_start(grads, after, name):
    nt = len(grads)
    lands = [lax.empty((N_CHIPS, g.shape[1] // 2, g.shape[2]), g.dtype) for g in grads]

    def body(*refs):
        ins, zones, send_sems, recv_sems, token = refs[:nt], refs[nt:2 * nt], refs[2 * nt + 1], refs[2 * nt + 2], refs[-1]
        for cp in _pair_copies(ins, zones, send_sems, recv_sems):
            cp.start()
        token[...] = jnp.zeros_like(token)

    outs = _pallas(body, name=name, in_specs=[HBM] * (2 * nt) + [ANY],
                   out_specs=(SEM, SEM) + (HBM,) * (2 * nt) + (pl.BlockSpec(memory_space=pltpu.VMEM),),
                   out_shape=(pltpu.SemaphoreType.DMA((nt,)), pltpu.SemaphoreType.DMA((nt,)))
                   + tuple(pltpu.HBM(a.shape, a.dtype) for a in list(grads) + lands) + (_sds((8, HEAD), F32),),
                   input_output_aliases={t: 2 + t for t in range(2 * nt)},
                   compiler_params=pltpu.CompilerParams(has_side_effects=EFFECT))(*[_in_hbm(a) for a in list(grads) + lands], after)
    return outs[0], outs[1], list(outs[2:2 + nt]), list(outs[2 + nt:2 + 2 * nt]), outs[-1]


def pair_wait(grads, lands, send_sems, recv_sems, after, name):
    nt = len(grads)

    def body(*refs):
        ins, zones, s_sems, r_sems = refs[:nt], refs[nt:2 * nt], refs[2 * nt], refs[2 * nt + 1]
        for cp in _pair_copies(ins, zones, s_sems, r_sems):
            cp.wait_send()
            cp.wait_recv()

    outs = _pallas(body, name=name, in_specs=[HBM] * (2 * nt) + [SEM, SEM, ANY], out_specs=[HBM] * (2 * nt),
                   out_shape=[pltpu.HBM(a.shape, a.dtype) for a in list(grads) + list(lands)],
                   input_output_aliases={t: t for t in range(2 * nt)},
                   compiler_params=pltpu.CompilerParams(has_side_effects=EFFECT))(*grads, *lands, send_sems, recv_sems, after)
    return list(outs[:nt]), list(outs[nt:])


def _exchange_copies(sums, lands, send_sems, recv_sems):
    x, y, c, chips = _place()
    return [_rcopy(s.at[2 * px + py], l.at[j], send_sems, recv_sems, 3 * t + j, (px, py, c))
            for t, (s, l) in enumerate(zip(sums, lands)) for j, (px, py) in enumerate(chips)]


def exchange_start(sums, name):
    nt = len(sums)
    lands = [lax.empty((3,) + s.shape[1:], s.dtype) for s in sums]

    def body(*refs):
        ins, zones, send_sems, recv_sems, token = refs[:nt], refs[nt:2 * nt], refs[2 * nt], refs[2 * nt + 1], refs[-1]
        for cp in _exchange_copies(ins, zones, send_sems, recv_sems):
            cp.start()
        token[...] = jnp.zeros_like(token)

    outs = _pallas(body, name=name, in_specs=[HBM] * (2 * nt),
                   out_specs=(SEM, SEM) + (HBM,) * (2 * nt) + (pl.BlockSpec(memory_space=pltpu.VMEM),),
                   out_shape=(pltpu.SemaphoreType.DMA((3 * nt,)), pltpu.SemaphoreType.DMA((3 * nt,)))
                   + tuple(pltpu.HBM(a.shape, a.dtype) for a in list(sums) + lands) + (_sds((8, HEAD), F32),),
                   input_output_aliases={t: 2 + t for t in range(2 * nt)},
                   compiler_params=pltpu.CompilerParams(has_side_effects=EFFECT))(*[_in_hbm(a) for a in list(sums) + lands])
    return outs[0], outs[1], list(outs[2:2 + nt]), list(outs[2 + nt:2 + 2 * nt]), outs[-1]


def exchange_wait(sums, lands, send_sems, recv_sems, after, name):
    nt = len(sums)

    def body(*refs):
        ins, zones, s_sems, r_sems = refs[:nt], refs[nt:2 * nt], refs[2 * nt], refs[2 * nt + 1]
        for cp in _exchange_copies(ins, zones, s_sems, r_sems):
            cp.wait_send()
            cp.wait_recv()

    outs = _pallas(body, name=name, in_specs=[HBM] * (2 * nt) + [SEM, SEM, ANY], out_specs=[HBM] * (2 * nt),
                   out_shape=[pltpu.HBM(a.shape, a.dtype) for a in list(sums) + list(lands)],
                   input_output_aliases={t: t for t in range(2 * nt)},
                   compiler_params=pltpu.CompilerParams(has_side_effects=EFFECT))(*sums, *lands, send_sems, recv_sems, after)
    return list(outs[:nt]), list(outs[nt:])


def pair_share(halves, name):
    nt = len(halves)

    def body(*refs):
        ins, outs = refs[:nt], refs[nt:2 * nt]
        send_sems, recv_sems = refs[2 * nt:]
        x, y, c, _ = _place()
        cps = []
        for t in range(nt):
            cps.append(_rcopy(ins[t], outs[t], send_sems, recv_sems, t, (x, y, 1 - c)))
            cps[-1].start()
        for cp in cps:
            cp.wait_recv()
        for cp in cps:
            cp.wait_send()

    return _pallas(body, name=name, in_specs=[ANY] * nt, out_specs=[ANY] * nt,
                   out_shape=[_sds(h.shape, h.dtype) for h in halves],
                   scratch_shapes=[pltpu.SemaphoreType.DMA((nt,)), pltpu.SemaphoreType.DMA((nt,))],
                   compiler_params=pltpu.CompilerParams(has_side_effects=True))(*halves)


def small_allreduce(pack, name):
    r = pack.shape[0]

    def body(in_ref, out_ref, buf, send_sems, recv_sems):
        x, y, c, _ = _place()
        me = 4 * x + 2 * y + c
        sends = []
        for k in range(1, 8):
            to = ((x + ((k >> 2) & 1)) % 2, (y + ((k >> 1) & 1)) % 2, (c + (k & 1)) % 2)
            cp = _rcopy(in_ref, buf.at[me], send_sems, recv_sems, k - 1, to)
            cp.start()
            sends.append((cp, to))
        buf[pl.ds(me, 1)] = in_ref[...][None]
        for k, (_, to) in enumerate(sends):
            peer = 4 * to[0] + 2 * to[1] + to[2]
            _rcopy(in_ref, buf.at[peer], send_sems, recv_sems, k, to).wait_recv()
        for cp, _ in sends:
            cp.wait_send()
        acc = buf[0]
        for d in range(1, 8):
            acc = acc + buf[d]
        out_ref[...] = acc

    vm = pl.BlockSpec(memory_space=pltpu.VMEM)
    return _pallas(body, name=name, in_specs=[vm], out_specs=vm, out_shape=_sds((r, HEAD), F32),
                   scratch_shapes=[pltpu.VMEM((8, r, HEAD), F32), pltpu.SemaphoreType.DMA((7,)),
                                   pltpu.SemaphoreType.DMA((7,))],
                   compiler_params=pltpu.CompilerParams(has_side_effects=True))(pack)


def add_halves(c_idx, grad, other, name):
    _, k, ns = grad.shape
    kh = k // 2
    tr = _tile(kh, max(16, (1 << 19) // ns), 16)
    nr = kh // tr

    def body(c_ref, g_ref, o_ref, s_ref):
        s_ref[...] = (g_ref[...].astype(F32) + o_ref[...].astype(F32)).astype(s_ref.dtype)

    gs = pltpu.PrefetchScalarGridSpec(
        num_scalar_prefetch=1, grid=(N_CHIPS, nr),
        in_specs=[pl.BlockSpec((None, tr, ns), lambda g, i, c_ref: (g, c_ref[0] * nr + i, 0)),
                  pl.BlockSpec((None, tr, ns), lambda g, i, c_ref: (g, i, 0))],
        out_specs=pl.BlockSpec((None, tr, ns), lambda g, i, c_ref: (g, i, 0)))
    return _pallas(body, name=name, grid_spec=gs, out_shape=_sds((N_CHIPS, kh, ns), XDT),
                   compiler_params=_params(("parallel", "parallel")))(c_idx, grad, other)


def add_chips(chip_idx, sums, recv, stack, layer, n_layers, name):
    _, kh, ns = sums.shape
    tr = _tile(kh, max(16, (1 << 19) // ns), 16)
    has_stack = stack is not None

    def body(k_ref, s_ref, r0, r1, r2, *rest):
        o_ref = rest[-1]
        o_ref[...] = ((s_ref[...].astype(F32) + r0[...].astype(F32)) + r1[...].astype(F32)) + r2[...].astype(F32)

    rspec = [pl.BlockSpec((None, tr, ns), functools.partial(lambda i, k_ref, j: (j, i, 0), j=j)) for j in range(3)]
    gs = pltpu.PrefetchScalarGridSpec(
        num_scalar_prefetch=1, grid=(kh // tr,),
        in_specs=[pl.BlockSpec((None, tr, ns), lambda i, k_ref: (k_ref[0], i, 0))] + rspec + ([ANY] if has_stack else []),
        out_specs=pl.BlockSpec((None, tr, ns), lambda i, k_ref: (layer, i, 0)))
    args = (chip_idx, sums, recv, recv, recv) + ((stack,) if has_stack else ())
    return _pallas(body, name=name, grid_spec=gs, out_shape=_sds((n_layers, kh, ns), F32),
                   input_output_aliases={5: 0} if has_stack else {},
                   compiler_params=_params(("parallel",)))(*args)


def adamw_big(c_idx, w, m, v, mine, other, name):
    nl, k, ns = w.shape
    kh = k // 2
    tr = _tile(kh, max(8, (1 << 18) // ns), 8)
    nr = kh // tr
    c1 = 1.0 - ADAM_B1 ** ADAM_STEP
    c2 = 1.0 - ADAM_B2 ** ADAM_STEP

    def body(c_ref, w_ref, m_ref, v_ref, a_ref, b_ref, g_ref, d_ref, mo_ref, vo_ref):
        gv = jnp.where(pl.program_id(2) == c_ref[0], a_ref[...], b_ref[...])
        mn = ADAM_B1 * m_ref[...] + (1.0 - ADAM_B1) * gv
        vn = ADAM_B2 * v_ref[...] + (1.0 - ADAM_B2) * (gv * gv)
        g_ref[...] = gv
        d_ref[...] = -ADAM_LR * ((mn / c1) / (jnp.sqrt(vn / c2) + ADAM_EPS) + ADAM_WD * w_ref[...])
        mo_ref[...] = mn
        vo_ref[...] = vn

    full = pl.BlockSpec((None, tr, ns), lambda l, i, hh, c_ref: (l, hh * nr + i, 0))
    half = pl.BlockSpec((None, tr, ns), lambda l, i, hh, c_ref: (l, i, 0))
    gs = pltpu.PrefetchScalarGridSpec(num_scalar_prefetch=1, grid=(nl, nr, 2),
                                      in_specs=[full, full, full, half, half], out_specs=[full] * 4)
    return _pallas(body, name=name, grid_spec=gs, out_shape=[_sds(w.shape, F32)] * 4,
                   compiler_params=_params(("parallel", "parallel", "arbitrary")))(c_idx, w, m, v, mine, other)


W_NAMES = ("w_in", "w_br_a", "w_br_b", "w_br_c", "w_o", "w_gate_up", "w_down")


def _rope_tables(n):
    half = HEAD // 2
    inv_freq = ROPE_THETA ** (-jnp.arange(half, dtype=F32) * 2.0 / HEAD)
    ang = jnp.arange(n, dtype=F32)[:, None] * inv_freq[None, :]
    cos, sin = jnp.cos(ang), jnp.sin(ang)
    return jnp.concatenate([cos, cos], axis=-1), jnp.concatenate([-sin, sin], axis=-1)


def _rpb_windows(rpb):
    pad = jnp.pad(rpb, ((0, 0), (0, 1), (0, GRID_W - rpb.shape[2])))
    wins = [pad[:, i0:i0 + C_WIN_ROWS].reshape(C_HEADS, 1, C_WIN_ROWS * GRID_W) for i0 in range(C_WIN_ROWS)]
    return jnp.stack(wins, axis=1)


def _rows(t):
    return t.reshape(-1, t.shape[-1])


def _like(t, ref):
    return t.reshape(ref.shape)


def layer_fwd(x, p, w, cos2, sin2, rest=None, mid=None):
    n, d = x.shape
    s = {"x": x}
    s["h"] = rmsnorm_fwd(x, p["norm1_g"], "norm1")
    s["proj"] = mm_x_wcol(s["h"], w["w_in"], ADT, "proj")
    gains = jnp.pad(p["qk_norm_g"], ((0, 2), (0, 0)))
    pp = s["pp"] = qk_prep(s["proj"], gains, cos2, sin2, "qk_prep")
    sink = p["sink_a"].reshape(1, A_Q_HEADS)
    s["oa"], s["lse_a"] = band_attn_fwd(pp["qa"], pp["ka"], pp["va"], sink, seqs=A_KV_HEADS, G=A_GROUP,
                                        nh=A_KV_HEADS, radius=A_RADIUS, name="attn_a")
    s["ob"], s["lse_b"] = [], []
    for g, dil in enumerate(B_DILS):
        o, lse = band_attn_fwd(_rows(pp[f"qb{g}"]), _rows(pp[f"kb{g}"]), _rows(pp[f"vb{g}"]), None, seqs=B_HG, G=1,
                               nh=B_HG, radius=B_RADIUS, classes=dil, name=f"attn_b{g}")
        s["ob"].append(_like(o, pp[f"qb{g}"]))
        s["lse_b"].append(_like(lse, pp[f"qb{g}"]))
    ob = b_combine_fwd(s["ob"], s["lse_b"], "b_combine")
    s["bias"] = rpb_expand(_rpb_windows(p["rpb_c"]), "rpb_expand")
    s["oc"], s["lse_c"] = c_attn_fwd(pp["qc"], pp["kc"], pp["vc"], s["bias"], "attn_c")
    s["o_in"] = (s["oa"], ob, s["oc"])
    if rest is not None:
        w = {**w, **rest(s["oc"])}
    s["w"] = w
    s["ys"] = [mm_x_wcol(o, w[k], ADT, "branch_" + k[-1]) for o, k in zip(s["o_in"], ("w_br_a", "w_br_b", "w_br_c"))]
    s["merged"] = gate_merge(s["proj"], s["ys"], "gate_merge")
    s["x_mid"] = mm_x_w(s["merged"], w["w_o"], "out_proj", res=x)
    s["h2"] = rmsnorm_fwd(s["x_mid"], p["norm2_g"], "norm2")
    token = None if mid is None else mid(s["h2"])
    s["gu"], s["act"] = gate_up_swiglu(s["h2"], w["w_gate_up"], "gate_up", after=token)
    x_out = mm_x_w(s["act"], w["w_down"], "down", res=s["x_mid"], tk_pref=2816)
    return x_out, s


def layer_bwd(dx_out, s, p, cos2, sin2, on_dws, early=False, on_tail=None):
    dx_out, dx_out_c = dx_out
    n, d = dx_out.shape
    pp, w = s["pp"], s["w"]
    dgu = d_gate_up(dx_out_c, w["w_down"], s["gu"], "d_gate_up")
    dw_down = mm_aT_d(s["act"], dx_out_c, "dw_down")
    dh2 = mm_x_wcolT(dgu, w["w_gate_up"], "d_h2", stacked_in=2)
    dw_gu = mm_aT_d_wcol(s["h2"], dgu, "dw_gate_up", tn_pref=1408, stacked_in=2)
    dx_mid, dx_mid_c, dg2 = rmsnorm_bwd(s["x_mid"], p["norm2_g"], dh2, dx_out, "norm2_bwd")
    dws = {"w_gate_up": dw_gu, "w_down": dw_down.reshape(N_CHIPS, dw_down.shape[0] // N_CHIPS, d)}
    token = on_dws(dws) if early else None

    dmerged = mm_x_wT(dx_mid_c, w["w_o"], "d_merged", after=token)
    dw_o = mm_aT_d(s["merged"], dx_mid_c, "dw_o")
    dys, dproj = gate_bwd(s["proj"], s["ys"], dmerged, "gate_bwd")
    dos, dw_br = [], []
    for b, (o, k) in enumerate(zip(s["o_in"], ("w_br_a", "w_br_b", "w_br_c"))):
        dos.append(mm_x_wcolT(dys, w[k], "d_o_" + k[-1], lead=b))
        dw_br.append(mm_aT_d_wcol(o, dys, "dw_br_" + k[-1], lead=b))

    grads = {}
    sink = p["sink_a"].reshape(1, A_Q_HEADS)
    grads["qa"], grads["ka"], grads["va"], dsink = band_attn_bwd(
        pp["qa"], pp["ka"], pp["va"], sink, s["oa"], s["lse_a"], dos[0], None,
        seqs=A_KV_HEADS, G=A_GROUP, nh=A_KV_HEADS, radius=A_RADIUS, name="attn_a_bwd")
    dobs, dlses = b_combine_bwd(dos[1], s["ob"], s["lse_b"], "b_combine_bwd")
    for g, dil in enumerate(B_DILS):
        dq, dk, dv = band_attn_bwd(_rows(pp[f"qb{g}"]), _rows(pp[f"kb{g}"]), _rows(pp[f"vb{g}"]), None,
                                   _rows(s["ob"][g]), _rows(s["lse_b"][g]), _rows(dobs[g]), _rows(dlses[g]),
                                   seqs=B_HG, G=1, nh=B_HG, radius=B_RADIUS, classes=dil, name=f"attn_b{g}_bwd")
        grads[f"qb{g}"], grads[f"kb{g}"], grads[f"vb{g}"] = [_like(t, pp[f"qb{g}"]) for t in (dq, dk, dv)]
    grads["qc"], grads["kc"], grads["vc"], dbias = c_attn_bwd(pp["qc"], pp["kc"], pp["vc"], s["bias"], s["oc"],
                                                              s["lse_c"], dos[2], "attn_c_bwd")
    drpb = rpb_reduce(dbias, "rpb_reduce")[:, :2 * C_WIN_ROWS - 1, :2 * C_WIN_COLS - 1]
    gains = jnp.pad(p["qk_norm_g"], ((0, 2), (0, 0)))
    dproj, dgains = qk_prep_bwd(s["proj"], gains, cos2, sin2, grads, dproj, "qk_prep_bwd")
    dw_in = mm_aT_d_wcol(s["h"], dproj, "dw_in")
    rest = {"w_in": dw_in, "w_br_a": dw_br[0], "w_br_b": dw_br[1], "w_br_c": dw_br[2],
            "w_o": dw_o.reshape(N_CHIPS, d // N_CHIPS, d)}
    token = on_dws(rest if early else {**rest, **dws})
    dh = mm_x_wcolT(dproj, w["w_in"], "d_h", after=token)
    dx_in, dx_in_c, dg1 = rmsnorm_bwd(s["x"], p["norm1_g"], dh, dx_mid, "norm1_bwd")
    if on_tail is not None:
        dx_in_c = on_tail(dx_in, dx_in_c)
    small = {"norm1_g": dg1[0], "qk_norm_g": dgains[:6], "sink_a": dsink[0, :, 0],
             "rpb_c": drpb, "norm2_g": dg2[0]}
    return (dx_in, dx_in_c), small


SMALL_NAMES = ("norm1_g", "qk_norm_g", "sink_a", "rpb_c", "norm2_g")


def _pack_small(parts, extra=None):
    flat = [parts[k].reshape(-1) for k in SMALL_NAMES]
    flat.append(jnp.zeros((1,), F32) if extra is None else extra.reshape(1))
    v = jnp.concatenate(flat)
    rows = -(-v.shape[0] // (8 * HEAD)) * 8
    return jnp.pad(v, (0, rows * HEAD - v.shape[0])).reshape(rows, HEAD)


def _unpack_small(pack, like):
    v = pack.reshape(-1)
    out, off = {}, 0
    for k in SMALL_NAMES:
        size = math.prod(like[k].shape)
        out[k] = v[off:off + size].reshape(like[k].shape)
        off += size
    return out, v[off]


def kernel(x, norm1_g, w_in, qk_norm_g, sink_a, rpb_c, w_br_a, w_br_b, w_br_c, w_o, norm2_g, w_gate_up, w_down, loss_target, m_norm1_g, m_w_in, m_qk_norm_g, m_sink_a, m_rpb_c, m_w_br_a, m_w_br_b, m_w_br_c, m_w_o, m_norm2_g, m_w_gate_up, m_w_down, v_norm1_g, v_w_in, v_qk_norm_g, v_sink_a, v_rpb_c, v_w_br_a, v_w_br_b, v_w_br_c, v_w_o, v_norm2_g, v_w_gate_up, v_w_down):
    big = dict(w_in=w_in, w_br_a=w_br_a, w_br_b=w_br_b, w_br_c=w_br_c, w_o=w_o, w_gate_up=w_gate_up, w_down=w_down)
    big_m = dict(w_in=m_w_in, w_br_a=m_w_br_a, w_br_b=m_w_br_b, w_br_c=m_w_br_c, w_o=m_w_o, w_gate_up=m_w_gate_up, w_down=m_w_down)
    big_v = dict(w_in=v_w_in, w_br_a=v_w_br_a, w_br_b=v_w_br_b, w_br_c=v_w_br_c, w_o=v_w_o, w_gate_up=v_w_gate_up, w_down=v_w_down)
    small = dict(norm1_g=norm1_g, qk_norm_g=qk_norm_g, sink_a=sink_a, rpb_c=rpb_c, norm2_g=norm2_g)
    small_m = dict(norm1_g=m_norm1_g, qk_norm_g=m_qk_norm_g, sink_a=m_sink_a, rpb_c=m_rpb_c, norm2_g=m_norm2_g)
    small_v = dict(norm1_g=v_norm1_g, qk_norm_g=v_qk_norm_g, sink_a=v_sink_a, rpb_c=v_rpb_c, norm2_g=v_norm2_g)
    n_layers = w_in.shape[0]
    n, d = x.shape[1], x.shape[2]
    c_idx = lax.axis_index("c").astype(jnp.int32).reshape(1)
    chip_idx = (2 * lax.axis_index("x") + lax.axis_index("y")).astype(jnp.int32).reshape(1)
    cos2, sin2 = _rope_tables(n)

    def as_weights(names, got):
        w = dict(zip(names, got))
        if "w_o" in w:
            w["w_o"] = w["w_o"].reshape(d, d)
            w["w_down"] = w["w_down"].reshape(-1, d)
        return w

    def gathered(names, started, after):
        send_sems, recv_sems, bufs, _ = started
        return as_weights(names, pair_forward(gather_wait(bufs, send_sems, recv_sems, after, "gather_wait"), "pair_forward"))

    def placed(names, l):
        return [cast_place(chip_idx, big[k], l, "cast_" + k) for k in names]

    first, others_0 = W_NAMES[:1], W_NAMES[1:]
    started = gather_start(placed(first, 0), chip_idx, "gather_start")
    later = [placed(W_NAMES, l) for l in range(1, n_layers)]
    started_rest = gather_start(placed(others_0, 0), [started[3]] + [b for bufs in later for b in bufs], "gather_start")
    weights = gathered(first, started, started_rest[3])
    rest = lambda after: gathered(others_0, started_rest, after)

    xs, saved = x[0], []
    for l in range(n_layers):
        p = {k: small[k][l] for k in SMALL_NAMES}
        mid, fwd = None, {}
        if l + 1 < n_layers:
            started = gather_start(later[l], weights["w_in"], "gather_start")
            p["norm1_g"] = p["norm1_g"] + started[3][0, 0]

            def mid(after, started=started, fwd=fwd):
                send_sems, recv_sems, bufs, _ = started
                arrived = gather_wait(bufs, send_sems, recv_sems, after, "gather_wait")
                fwd["started"] = forward_start(arrived, after, "forward_start")
                return fwd["started"][3]

        xs, s = layer_fwd(xs, p, weights, cos2, sin2, rest, mid if l > 0 else None)
        saved.append(s)
        if l + 1 < n_layers and l > 0:
            send_sems, recv_sems, bufs, _ = fwd["started"]
            weights, rest = as_weights(W_NAMES, forward_wait(bufs, send_sems, recv_sems, xs, "forward_wait")), None
        elif l + 1 < n_layers:
            weights, rest = gathered(W_NAMES, started, xs), None
    loss_tile, *dx = loss_head(xs, loss_target[0], "loss_head")

    halves = {k: None for k in W_NAMES}
    small_g = [None] * n_layers
    pending = []

    def finish_exchanges(after):
        for l, names, (send_sems, recv_sems, sums, lands, _) in pending:
            sums, from_chips = exchange_wait(sums, lands, send_sems, recv_sems, after, "exchange_wait")
            for k, sm, r in zip(names, sums, from_chips):
                halves[k] = add_chips(chip_idx, sm, r, halves[k], l, n_layers, "add_chips_" + k)
        pending.clear()

    def make_on_dws(l, wait_first):
        def on_dws(dws):
            names = [k for k in W_NAMES if k in dws]
            parts = [dws[k] for k in names]
            if wait_first:
                finish_exchanges(parts[0])
            from_sibling = pair_exchange(parts, "pair_exchange")
            sums = [add_halves(c_idx, g, o, "add_halves_" + k) for g, o, k in zip(parts, from_sibling, names)]
            pending.append((l, names, exchange_start(sums, "exchange_start")))
            return pending[-1][2][4]
        return on_dws

    def make_split(l):
        box = {}

        def on_dws(dws):
            names = [k for k in W_NAMES if k in dws]
            parts = [dws[k] for k in names]
            finish_exchanges(parts[0])
            box["names"], box["started"] = names, pair_start(parts, parts[0], "pair_start")
            return box["started"][4]

        def on_tail(dx_in, dx_in_c):
            send_sems, recv_sems, parts, lands, _ = box["started"]
            parts, from_sibling = pair_wait(parts, lands, send_sems, recv_sems, dx_in, "pair_wait")
            sums = [add_halves(c_idx, g, o, "add_halves_" + k) for g, o, k in zip(parts, from_sibling, box["names"])]
            pending.append((l, box["names"], exchange_start(sums, "exchange_start")))
            return dx_in_c + pending[-1][2][4][0, 0].astype(dx_in_c.dtype)

        return on_dws, on_tail

    for l in reversed(range(n_layers)):
        p = {k: small[k][l] for k in SMALL_NAMES}
        if l == 0:
            dx, small_g[l] = layer_bwd(dx, saved[l], p, cos2, sin2, make_on_dws(l, False), early=True)
        else:
            on_dws, on_tail = make_split(l)
            dx, small_g[l] = layer_bwd(dx, saved[l], p, cos2, sin2, on_dws, on_tail=on_tail)
    dx = dx[0]
    finish_exchanges(dx)
    halves = [halves[k] for k in W_NAMES]
    others = pair_share(halves, "pair_share")

    mine = {k: jnp.stack([small_g[l][k] for l in range(n_layers)]) for k in SMALL_NAMES}
    total = small_allreduce(_pack_small(mine, loss_tile[0, 0]), "small_allreduce")
    grad_small, loss = _unpack_small(total, small)

    outs = {}
    for k, mine_half, other_half in zip(W_NAMES, halves, others):
        outs[k] = adamw_big(c_idx, big[k], big_m[k], big_v[k], mine_half, other_half, "adamw_" + k)
    res = adamw(_pack_small(small), _pack_small(grad_small), _pack_small(small_m), _pack_small(small_v), "adamw_small")
    unp = [_unpack_small(t, small)[0] for t in res]
    for k in SMALL_NAMES:
        outs[k] = (grad_small[k],) + tuple(u[k] for u in unp)

    order = ("norm1_g", "w_in", "qk_norm_g", "sink_a", "rpb_c", "w_br_a", "w_br_b", "w_br_c", "w_o", "norm2_g",
             "w_gate_up", "w_down")
    return (loss, dx[None]) + tuple(outs[k][i] for i in range(4) for k in order)
```
